```python
import jax, jax.numpy as jnp
from jax import lax
import numpy as np

D_MODEL = 1024
BATCH = 4
SEQ = 4096
DEPTH = 1

PLE_DIM = 256
EPS = 1e-6
NEG = -1e30

NSA_HEADS = 8
NSA_KV = 2
NSA_HPG = NSA_HEADS // NSA_KV
NSA_HD = 64
CMP_LEN = 32
CMP_STRIDE = 16
CMP_HIDDEN = 256
SEL_BLOCK = 64
SEL_TOPK = 16
SEL_FORCE = 1000.0
WINDOW = 512
Q_BLOCK = 128
ROPE_THETA = 500000.0
ROPE_DIM = NSA_HD // 4

ML_HEADS = 4
ML_HD = 128
ML_WIDTH = ML_HEADS * ML_HD
ML_CHUNK = 64
CONV_W = 4

N_GROUPS = 4
EXP_PER_GROUP = 4
N_EXPERTS = N_GROUPS * EXP_PER_GROUP
TOPK_IN_GROUP = 2
D_EXPERT = 256

NSA_QW = NSA_HEADS * NSA_HD
NSA_KVW = NSA_KV * NSA_HD
IN_SIZES = (NSA_QW, NSA_KVW, NSA_KVW, NSA_KVW, NSA_KVW, NSA_KVW, NSA_KVW, 3 * NSA_HEADS, 2 * ML_WIDTH, ML_WIDTH, ML_WIDTH, 2 * ML_HEADS, 2 * D_MODEL)
D_IN = sum(IN_SIZES)

kernel_name = 'hybrid_nsa_mlstm_hmoe'


def rmsnorm(x, g):
    xf = x.astype(jnp.float32)
    y = xf * lax.rsqrt(jnp.mean(xf * xf, axis=-1, keepdims=True) + EPS)
    return (y * g.astype(jnp.float32)).astype(x.dtype)


def rope_tables(positions):
    inv = ROPE_THETA ** (-jnp.arange(0, ROPE_DIM, 2, dtype=jnp.float32) / ROPE_DIM)
    ang = positions.astype(jnp.float32)[:, :, None] * inv
    return jnp.cos(ang)[:, :, None, :], jnp.sin(ang)[:, :, None, :]


def partial_rope(t, cos, sin):
    half = ROPE_DIM // 2
    tf = t.astype(jnp.float32)
    t1, t2, rest = tf[..., :half], tf[..., half:ROPE_DIM], tf[..., ROPE_DIM:]
    out = jnp.concatenate([t1 * cos - t2 * sin, t2 * cos + t1 * sin, rest], axis=-1)
    return out.astype(t.dtype)


def compress_blocks(tok_g, w1, w2, pe):
    b, g, s, d = tok_g.shape
    n_cmp = (s - CMP_LEN) // CMP_STRIDE + 1
    idx = CMP_STRIDE * jnp.arange(n_cmp)[:, None] + jnp.arange(CMP_LEN)[None, :]
    blk = tok_g[:, :, idx] + pe
    flat = blk.reshape(b, g, n_cmp, CMP_LEN * d)
    return jax.nn.gelu(flat @ w1) @ w2


def nsa_attention(q, kc_tok, vc_tok, ks_tok, vs_tok, kw_tok, vw_tok, gates, w_ck1, w_ck2, pe_ck, w_cv1, w_cv2, pe_cv):
    b, s = q.shape[:2]
    dt = q.dtype
    n_sel = s // SEL_BLOCK
    sel_k = min(SEL_TOPK, n_sel)
    scale = NSA_HD ** -0.5
    qg = q.reshape(b, s, NSA_KV, NSA_HPG, NSA_HD).transpose(0, 2, 3, 1, 4)
    gg = gates.reshape(b, s, NSA_KV, NSA_HPG, 3).transpose(0, 2, 3, 1, 4)

    def to_g(t):
        return t.transpose(0, 2, 1, 3)

    kc = compress_blocks(to_g(kc_tok), w_ck1, w_ck2, pe_ck)
    vc = compress_blocks(to_g(vc_tok), w_cv1, w_cv2, pe_cv)
    n_cmp = kc.shape[2]
    ks = to_g(ks_tok).reshape(b, NSA_KV, n_sel, SEL_BLOCK, NSA_HD)
    vs = to_g(vs_tok).reshape(b, NSA_KV, n_sel, SEL_BLOCK, NSA_HD)
    pad = ((0, 0), (0, 0), (WINDOW, 0), (0, 0))
    kw = jnp.pad(to_g(kw_tok), pad)
    vw = jnp.pad(to_g(vw_tok), pad)
    c_start = CMP_STRIDE * jnp.arange(n_cmp)
    c_end = c_start + CMP_LEN - 1
    s_start = SEL_BLOCK * jnp.arange(n_sel)
    overlap = ((c_start[:, None] < s_start[None, :] + SEL_BLOCK) & (c_end[:, None] >= s_start[None, :])).astype(jnp.float32)
    bi = jnp.arange(b)[:, None, None, None]
    gi = jnp.arange(NSA_KV)[None, :, None, None]
    sel_ids = jnp.arange(n_sel)

    def query_block(c):
        t0 = c * Q_BLOCK
        t = t0 + jnp.arange(Q_BLOCK)
        qc = lax.dynamic_slice_in_dim(qg, t0, Q_BLOCK, axis=3)
        gc = lax.dynamic_slice_in_dim(gg, t0, Q_BLOCK, axis=3).astype(jnp.float32)
        cmask = c_end[None, :] <= t[:, None]
        sc = jnp.einsum('bghqd,bgnd->bghqn', qc, kc).astype(jnp.float32) * scale
        p_cmp = jax.nn.softmax(jnp.where(cmask, sc, NEG), axis=-1)
        p_cmp = p_cmp * jnp.any(cmask, axis=-1, keepdims=True).astype(jnp.float32)
        o_cmp = jnp.einsum('bghqn,bgnd->bghqd', p_cmp.astype(dt), vc)
        imp = jnp.einsum('bghqn,ns->bgqs', p_cmp, overlap)
        cur = t // SEL_BLOCK
        forced = (sel_ids[None, :] == 0) | (sel_ids[None, :] == cur[:, None]) | (sel_ids[None, :] == cur[:, None] - 1)
        imp = jnp.where(forced, imp + SEL_FORCE, imp)
        imp = jnp.where(s_start[None, :] <= t[:, None], imp, NEG)
        _, idx = lax.top_k(imp, sel_k)
        k_sel = ks[bi, gi, idx]
        v_sel = vs[bi, gi, idx]
        kpos = idx[..., None] * SEL_BLOCK + jnp.arange(SEL_BLOCK)
        smask = (kpos <= t[:, None, None])[:, :, None]
        ss = jnp.einsum('bghqd,bgqkrd->bghqkr', qc, k_sel).astype(jnp.float32) * scale
        ss = jnp.where(smask, ss, NEG).reshape(b, NSA_KV, NSA_HPG, Q_BLOCK, sel_k * SEL_BLOCK)
        p_sel = jax.nn.softmax(ss, axis=-1).reshape(b, NSA_KV, NSA_HPG, Q_BLOCK, sel_k, SEL_BLOCK)
        o_sel = jnp.einsum('bghqkr,bgqkrd->bghqd', p_sel.astype(dt), v_sel)
        kwc = lax.dynamic_slice_in_dim(kw, t0, Q_BLOCK + WINDOW, axis=2)
        vwc = lax.dynamic_slice_in_dim(vw, t0, Q_BLOCK + WINDOW, axis=2)
        wpos = t0 - WINDOW + jnp.arange(Q_BLOCK + WINDOW)
        wmask = (wpos[None, :] <= t[:, None]) & (wpos[None, :] > t[:, None] - WINDOW) & (wpos[None, :] >= 0)
        sw = jnp.einsum('bghqd,bgkd->bghqk', qc, kwc).astype(jnp.float32) * scale
        p_win = jax.nn.softmax(jnp.where(wmask, sw, NEG), axis=-1)
        o_win = jnp.einsum('bghqk,bgkd->bghqd', p_win.astype(dt), vwc)
        o = gc[..., 0:1] * o_cmp + gc[..., 1:2] * o_sel + gc[..., 2:3] * o_win
        return o.astype(dt)

    out = lax.map(query_block, jnp.arange(s // Q_BLOCK))
    return out.transpose(1, 0, 4, 2, 3, 5).reshape(b, s, NSA_HEADS * NSA_HD)


def causal_conv_silu(u, w, bias):
    s = u.shape[1]
    up = jnp.pad(u, ((0, 0), (CONV_W - 1, 0), (0, 0)))
    y = bias
    for j in range(CONV_W):
        y = y + w[j] * up[:, j:j + s]
    return jax.nn.silu(y)


def mlstm_chunkwise(q, k, v, i_pre, f_pre):
    b, s, nh, d = q.shape
    nc = s // ML_CHUNK
    lc = ML_CHUNK

    def seq_chunks(t):
        return t.astype(jnp.float32).reshape(b, nc, lc, nh, d).transpose(1, 0, 3, 2, 4)

    def gate_chunks(t):
        return t.astype(jnp.float32).reshape(b, nc, lc, nh).transpose(1, 0, 3, 2)

    qc = seq_chunks(q)
    kc = seq_chunks(k) * (d ** -0.5)
    vc = seq_chunks(v)
    ic = gate_chunks(i_pre)
    lfc = gate_chunks(jax.nn.log_sigmoid(f_pre.astype(jnp.float32)))
    causal = jnp.tril(jnp.ones((lc, lc), dtype=bool))

    def step(carry, inp):
        C, n, m = carry
        qt, kt, vt, it, lft = inp
        bcum = jnp.cumsum(lft, axis=-1)
        dmat = jnp.where(causal, bcum[..., :, None] - bcum[..., None, :] + it[..., None, :], -jnp.inf)
        inter = bcum + m[..., None]
        m_t = jnp.maximum(jnp.max(dmat, axis=-1), inter)
        a = jnp.exp(dmat - m_t[..., None]) * jnp.einsum('bhtd,bhsd->bhts', qt, kt)
        dec = jnp.exp(inter - m_t)
        num = jnp.einsum('bhts,bhsd->bhtd', a, vt) + dec[..., None] * jnp.einsum('bhvk,bhtk->bhtv', C, qt)
        den = jnp.sum(a, axis=-1) + dec * jnp.einsum('bhk,bhtk->bht', n, qt)
        h = num / jnp.maximum(jnp.abs(den), jnp.exp(-m_t))[..., None]
        b_last = bcum[..., -1]
        g_s = b_last[..., None] - bcum + it
        m_new = jnp.maximum(b_last + m, jnp.max(g_s, axis=-1))
        w_prev = jnp.exp(b_last + m - m_new)
        w_s = jnp.exp(g_s - m_new[..., None])
        C = w_prev[..., None, None] * C + jnp.einsum('bhs,bhsv,bhsk->bhvk', w_s, vt, kt)
        n = w_prev[..., None] * n + jnp.einsum('bhs,bhsk->bhk', w_s, kt)
        return (C, n, m_new), h

    init = (jnp.zeros((b, nh, d, d), jnp.float32), jnp.zeros((b, nh, d), jnp.float32), jnp.zeros((b, nh), jnp.float32))
    _, hs = lax.scan(step, init, (qc, kc, vc, ic, lfc))
    return hs.transpose(1, 0, 3, 2, 4).reshape(b, s, nh, d)


def hier_moe(h, w_rg, b_rg, w_re, b_re, w_e13, w_e2):
    b, s, dm = h.shape
    t = h.reshape(b * s, dm)
    n_tok = t.shape[0]
    gl = (t @ w_rg).astype(jnp.float32) + b_rg.astype(jnp.float32)
    pg = jax.nn.softmax(gl, axis=-1)
    _, gsel = lax.top_k(gl, 1)
    el = ((t @ w_re).astype(jnp.float32) + b_re.astype(jnp.float32)).reshape(n_tok, N_GROUPS, EXP_PER_GROUP)
    gidx = jnp.broadcast_to(gsel[:, :, None], (n_tok, 1, EXP_PER_GROUP))
    elg = jnp.take_along_axis(el, gidx, axis=1)[:, 0]
    ev, ei = lax.top_k(elg, TOPK_IN_GROUP)
    wts = jnp.take_along_axis(pg, gsel, axis=1) * jax.nn.softmax(ev, axis=-1)
    eid = gsel * EXP_PER_GROUP + ei
    gate = jnp.sum(jax.nn.one_hot(eid, N_EXPERTS, dtype=jnp.float32) * wts[..., None], axis=1)
    a = jnp.einsum('td,edf->tef', t, w_e13)
    g_, u = jnp.split(a, 2, axis=-1)
    act = jax.nn.silu(g_) * u * gate[:, :, None].astype(t.dtype)
    y = jnp.einsum('tef,efd->td', act, w_e2)
    return y.reshape(b, s, dm).astype(h.dtype)


def setup_inputs(seed: int = 0) -> dict:
    key = jax.random.key(seed)
    ks = jax.random.split(key, 32)
    L, D = DEPTH, D_MODEL
    f32 = jnp.float32

    def nrm(k, shape, fan_in):
        return jax.random.normal(k, shape, f32) * (fan_in ** -0.5)

    def gain(k, shape):
        return 1.0 + 0.02 * jax.random.normal(k, shape, f32)

    x = jax.random.normal(ks[0], (BATCH, SEQ, D), f32)
    p = jax.random.normal(ks[1], (DEPTH, BATCH, SEQ, PLE_DIM), f32)
    positions = jax.random.randint(ks[2], (BATCH, 1), 0, 1024, dtype=jnp.int32) + jnp.arange(SEQ, dtype=jnp.int32)[None, :]
    b_i = 0.1 * jax.random.normal(ks[5], (L, ML_HEADS), f32)
    b_f = jnp.linspace(3.0, 6.0, ML_HEADS, dtype=f32)[None, :] + 0.1 * jax.random.normal(ks[6], (L, ML_HEADS), f32)
    return {
        'x': x,
        'p': p,
        'positions': positions,
        'g_mix': gain(ks[3], (L, D)),
        'w_in': nrm(ks[4], (L, D, D_IN), D),
        'b_if': jnp.concatenate([b_i, b_f], axis=-1),
        'w_ck1': nrm(ks[7], (L, CMP_LEN * NSA_HD, CMP_HIDDEN), CMP_LEN * NSA_HD),
        'w_ck2': nrm(ks[8], (L, CMP_HIDDEN, NSA_HD), CMP_HIDDEN),
        'pe_ck': 0.1 * jax.random.normal(ks[9], (L, CMP_LEN, NSA_HD), f32),
        'w_cv1': nrm(ks[10], (L, CMP_LEN * NSA_HD, CMP_HIDDEN), CMP_LEN * NSA_HD),
        'w_cv2': nrm(ks[11], (L, CMP_HIDDEN, NSA_HD), CMP_HIDDEN),
        'pe_cv': 0.1 * jax.random.normal(ks[12], (L, CMP_LEN, NSA_HD), f32),
        'w_conv': nrm(ks[13], (L, CONV_W, 2 * ML_WIDTH), CONV_W),
        'b_conv': 0.02 * jax.random.normal(ks[14], (L, 2 * ML_WIDTH), f32),
        'g_hn': gain(ks[15], (L, ML_WIDTH)),
        'w_pa': nrm(ks[16], (L, NSA_QW, D), NSA_QW),
        'w_pb': nrm(ks[17], (L, ML_WIDTH, D), ML_WIDTH),
        'w_out': nrm(ks[18], (L, D, D), D),
        'g_ffn': gain(ks[19], (L, D)),
        'w_rg': nrm(ks[20], (L, D, N_GROUPS), D),
        'b_rg': 0.01 * jax.random.normal(ks[21], (L, N_GROUPS), f32),
        'w_re': nrm(ks[22], (L, D, N_EXPERTS), D),
        'b_re': 0.01 * jax.random.normal(ks[23], (L, N_EXPERTS), f32),
        'w_e13': nrm(ks[24], (L, N_EXPERTS, D, 2 * D_EXPERT), D),
        'w_e2': nrm(ks[25], (L, N_EXPERTS, D_EXPERT, D), D_EXPERT),
        'g_ple': gain(ks[26], (L, D)),
        'w_pg': nrm(ks[27], (L, D, D), D),
        'w_pp': nrm(ks[28], (L, PLE_DIM, D), PLE_DIM),
        'g_final': gain(ks[29], (D,)),
    }


def reference(x, p, positions, g_mix, w_in, b_if, w_ck1, w_ck2, pe_ck, w_cv1, w_cv2, pe_cv, w_conv, b_conv, g_hn, w_pa, w_pb, w_out, g_ffn, w_rg, b_rg, w_re, b_re, w_e13, w_e2, g_ple, w_pg, w_pp, g_final):
    b, s, _ = x.shape
    cos, sin = rope_tables(positions)
    offs = [int(o) for o in np.cumsum(IN_SIZES)[:-1]]
    for i in range(DEPTH):
        h = rmsnorm(x, g_mix[i])
        z = h @ w_in[i]
        (q_a, kc_a, vc_a, ks_a, vs_a, kw_a, vw_a, gate_a, qk_b, v_b, o_b, if_b, merge) = jnp.split(z, offs, axis=-1)
        nsa_q = partial_rope(q_a.reshape(b, s, NSA_HEADS, NSA_HD), cos, sin)
        k_cmp = partial_rope(kc_a.reshape(b, s, NSA_KV, NSA_HD), cos, sin)
        k_sel = partial_rope(ks_a.reshape(b, s, NSA_KV, NSA_HD), cos, sin)
        k_win = partial_rope(kw_a.reshape(b, s, NSA_KV, NSA_HD), cos, sin)
        nsa_gates = jax.nn.sigmoid(gate_a.astype(jnp.float32)).reshape(b, s, NSA_HEADS, 3)
        y_a = nsa_attention(nsa_q, k_cmp, vc_a.reshape(b, s, NSA_KV, NSA_HD), k_sel, vs_a.reshape(b, s, NSA_KV, NSA_HD), k_win, vw_a.reshape(b, s, NSA_KV, NSA_HD), nsa_gates, w_ck1[i], w_ck2[i], pe_ck[i], w_cv1[i], w_cv2[i], pe_cv[i])
        qk_c = causal_conv_silu(qk_b, w_conv[i], b_conv[i])
        q_b, k_b = jnp.split(qk_c, 2, axis=-1)
        ifp = if_b.astype(jnp.float32) + b_if[i].astype(jnp.float32)
        hm = mlstm_chunkwise(q_b.reshape(b, s, ML_HEADS, ML_HD), k_b.reshape(b, s, ML_HEADS, ML_HD), v_b.reshape(b, s, ML_HEADS, ML_HD), ifp[..., :ML_HEADS], ifp[..., ML_HEADS:])
        hm = hm * jax.nn.sigmoid(o_b.astype(jnp.float32)).reshape(b, s, ML_HEADS, ML_HD)
        hm = hm * lax.rsqrt(jnp.mean(hm * hm, axis=-1, keepdims=True) + EPS) * g_hn[i].astype(jnp.float32).reshape(ML_HEADS, ML_HD)
        y_b = hm.reshape(b, s, ML_WIDTH).astype(x.dtype)
        g_a, g_b = jnp.split(merge, 2, axis=-1)
        mixed = jax.nn.sigmoid(g_a) * (y_a @ w_pa[i]) + jax.nn.sigmoid(g_b) * (y_b @ w_pb[i])
        x = x + mixed @ w_out[i]
        x = x + hier_moe(rmsnorm(x, g_ffn[i]), w_rg[i], b_rg[i], w_re[i], b_re[i], w_e13[i], w_e2[i])
        x = x + jax.nn.sigmoid(rmsnorm(x, g_ple[i]) @ w_pg[i]) * (p[i] @ w_pp[i])
    return rmsnorm(x, g_final)
```

```python
import functools
import math

import jax
import jax.numpy as jnp
from jax import lax
from jax.experimental import pallas as pl
from jax.experimental.pallas import tpu as pltpu

F32 = jnp.float32
BF16 = jnp.bfloat16

EPS = 1e-6
NEG = -1e30

D_MODEL = 1024
PLE_DIM = 256
NSA_HEADS = 8
NSA_KV = 2
NSA_HPG = NSA_HEADS // NSA_KV
NSA_HD = 64
CMP_LEN = 32
CMP_STRIDE = 16
CMP_HIDDEN = 256
SEL_BLOCK = 64
SEL_TOPK = 16
SEL_FORCE = 1000.0
WINDOW = 512
ROPE_THETA = 500000.0
ROPE_DIM = NSA_HD // 4
ML_HEADS = 4
ML_HD = 128
ML_WIDTH = ML_HEADS * ML_HD
CONV_W = 4
N_GROUPS = 4
EXP_PER_GROUP = 4
N_EXPERTS = N_GROUPS * EXP_PER_GROUP
D_EXPERT = 256

LANES = 128
QT = 128
KC = 128
ML_CHUNK = 128
ML_BLOCK = 256
TM = 256
VMEM_LIMIT = 56 * 1024 * 1024

_NT = (((1,), (1,)), ((), ()))
_TN = (((0,), (0,)), ((), ()))

SM_GATE = 0
SM_I = 3 * NSA_HEADS
SM_F = SM_I + ML_HEADS


def _dot(a, b):
    return jnp.dot(a, b, preferred_element_type=F32)


def _dot_nt(a, b):
    return lax.dot_general(a, b, _NT, preferred_element_type=F32)


def _dot_hi(a, b):
    return jnp.dot(a, b, preferred_element_type=F32, precision=lax.Precision.HIGHEST)


def _rms(x, g):
    return x * lax.rsqrt(jnp.mean(x * x, axis=-1, keepdims=True) + EPS) * g


def _sigmoid(x):
    return 1.0 / (1.0 + jnp.exp(-x))


def _const_spec(shape):
    nd = len(shape)
    return pl.BlockSpec(shape, lambda *_: (0,) * nd, pipeline_mode=pl.Buffered(1))


_C_Q = 0
_C_KC = _C_Q + NSA_HEADS * LANES
_C_VC = _C_KC + LANES
_C_KS = _C_VC + LANES
_C_KW = _C_KS + LANES
_C_SM = _C_KW + LANES
_C_QKB = _C_SM + LANES
_C_VB = _C_QKB + 2 * ML_WIDTH
_C_OB = _C_VB + ML_WIDTH
_C_MG = _C_OB + ML_WIDTH
_C_END = _C_MG + 2 * D_MODEL


def _inproj_kernel(x_ref, g_ref, w_ref, wt_ref, rc_ref, rp_ref, rm_ref,
                   q_ref, kc_ref, vc_ref, ks_ref, kw_ref, vst_ref, vwt_ref, sm_ref, smt_ref,
                   qkb_ref, vb_ref, ob_ref, mg_ref):
    hn = _rms(x_ref[...], g_ref[...]).astype(BF16)
    rc, rp, rm = rc_ref[...], rp_ref[...], rm_ref[...]

    def rope(z):
        return z * rc + pltpu.roll(z, 8, 1) * rp + pltpu.roll(z, LANES - 8, 1) * rm

    scale = NSA_HD ** -0.5
    for h in range(NSA_HEADS):
        z = _dot(hn, w_ref[:, _C_Q + h * LANES:_C_Q + (h + 1) * LANES])
        q_ref[:, h * LANES:(h + 1) * LANES] = (rope(z) * scale).astype(BF16)
    kc_ref[...] = rope(_dot(hn, w_ref[:, _C_KC:_C_KC + LANES])).astype(BF16)
    vc_ref[...] = _dot(hn, w_ref[:, _C_VC:_C_VC + LANES]).astype(BF16)
    ks_ref[...] = rope(_dot(hn, w_ref[:, _C_KS:_C_KS + LANES])).astype(BF16)
    kw_ref[...] = rope(_dot(hn, w_ref[:, _C_KW:_C_KW + LANES])).astype(BF16)
    sm_ref[...] = _dot(hn, w_ref[:, _C_SM:_C_SM + LANES])
    for c0 in range(0, 2 * ML_WIDTH, 512):
        qkb_ref[:, c0:c0 + 512] = _dot(hn, w_ref[:, _C_QKB + c0:_C_QKB + c0 + 512]).astype(BF16)
    vb_ref[...] = _dot(hn, w_ref[:, _C_VB:_C_VB + ML_WIDTH]).astype(BF16)
    ob_ref[...] = _dot(hn, w_ref[:, _C_OB:_C_OB + ML_WIDTH]).astype(BF16)
    for c0 in range(0, 2 * D_MODEL, 512):
        mg_ref[:, c0:c0 + 512] = _dot(hn, w_ref[:, _C_MG + c0:_C_MG + c0 + 512]).astype(BF16)
    zt = _dot_nt(wt_ref[...], hn)
    for i in range(TM // KC):
        vst_ref[i] = zt[0:LANES, i * KC:(i + 1) * KC].astype(BF16)
        vwt_ref[i] = zt[LANES:2 * LANES, i * KC:(i + 1) * KC].astype(BF16)
    smt_ref[...] = zt[2 * LANES:3 * LANES, :]


def _inproj(x2, g_mix, wcat, wtr, rc, rp, rm):
    t = x2.shape[0]
    row = lambda w: pl.BlockSpec((TM, w), lambda i: (i, 0))
    out_shape = [
        jax.ShapeDtypeStruct((t, NSA_HEADS * LANES), BF16),
        jax.ShapeDtypeStruct((t, LANES), BF16),
        jax.ShapeDtypeStruct((t, LANES), BF16),
        jax.ShapeDtypeStruct((t, LANES), BF16),
        jax.ShapeDtypeStruct((t, LANES), BF16),
        jax.ShapeDtypeStruct((t // KC, LANES, KC), BF16),
        jax.ShapeDtypeStruct((t // KC, LANES, KC), BF16),
        jax.ShapeDtypeStruct((t, LANES), F32),
        jax.ShapeDtypeStruct((LANES, t), F32),
        jax.ShapeDtypeStruct((t, 2 * ML_WIDTH), BF16),
        jax.ShapeDtypeStruct((t, ML_WIDTH), BF16),
        jax.ShapeDtypeStruct((t, ML_WIDTH), BF16),
        jax.ShapeDtypeStruct((t, 2 * D_MODEL), BF16),
    ]
    chunk3 = pl.BlockSpec((TM // KC, LANES, KC), lambda i: (i, 0, 0))
    out_specs = [row(NSA_HEADS * LANES), row(LANES), row(LANES), row(LANES), row(LANES), chunk3, chunk3,
                 row(LANES), pl.BlockSpec((LANES, TM), lambda i: (0, i)),
                 row(2 * ML_WIDTH), row(ML_WIDTH), row(ML_WIDTH), row(2 * D_MODEL)]
    return pl.pallas_call(
        _inproj_kernel,
        out_shape=out_shape,
        grid=(t // TM,),
        in_specs=[row(D_MODEL), _const_spec((1, D_MODEL)), _const_spec((D_MODEL, _C_END)),
                  _const_spec((3 * LANES, D_MODEL)), row(LANES), row(LANES), row(LANES)],
        out_specs=out_specs,
        compiler_params=pltpu.CompilerParams(dimension_semantics=("arbitrary",), vmem_limit_bytes=VMEM_LIMIT),
        name="inproj",
    )(x2, g_mix, wcat, wtr, rc, rp, rm)


def _gelu_tanh(x):
    return 0.5 * x * (1.0 + jnp.tanh(math.sqrt(2.0 / math.pi) * (x + 0.044715 * x * x * x)))


def _compress_kernel(rk_ref, rv_ref, wka_ref, wkb_ref, wva_ref, wvb_ref, pek_ref, pev_ref,
                     w1k_ref, w1v_ref, w2k_ref, w2vt_ref, kc_ref, vct_ref):
    nrow = rk_ref.shape[0]

    def hidden(r_ref, wa_ref, wb_ref, pe_ref, w1_ref):
        r = r_ref[...]
        ha = _dot(r, wa_ref[...])
        hb = _dot(r, wb_ref[...])
        hb = pltpu.roll(hb, nrow - 1, 0)
        c = _dot(pe_ref[...], w1_ref[...])[0:1, :]
        return [_gelu_tanh(ha[:, g * CMP_HIDDEN:(g + 1) * CMP_HIDDEN] + hb[:, g * CMP_HIDDEN:(g + 1) * CMP_HIDDEN] + c).astype(BF16)
                for g in range(NSA_KV)]

    ak = hidden(rk_ref, wka_ref, wkb_ref, pek_ref, w1k_ref)
    kc_ref[...] = (_dot(ak[0], w2k_ref[0]) + _dot(ak[1], w2k_ref[1])).astype(BF16)
    av = hidden(rv_ref, wva_ref, wvb_ref, pev_ref, w1v_ref)
    for g in range(NSA_KV):
        vct_ref[g * NSA_HD:(g + 1) * NSA_HD, :] = _dot_nt(w2vt_ref[...], av[g]).astype(BF16)


def _compress(rk, rv, wka, wkb, wva, wvb, pek, pev, w1k, w1v, w2k, w2vt):
    b, nrow, width = rk.shape
    blk = pl.BlockSpec((None, nrow, width), lambda i: (i, 0, 0))
    return pl.pallas_call(
        _compress_kernel,
        out_shape=[jax.ShapeDtypeStruct((b, nrow, LANES), BF16),
                   jax.ShapeDtypeStruct((b, LANES, nrow), BF16)],
        grid=(b,),
        in_specs=[blk, blk] + [_const_spec(a.shape) for a in (wka, wkb, wva, wvb, pek, pev, w1k, w1v, w2k, w2vt)],
        out_specs=[pl.BlockSpec((None, nrow, LANES), lambda i: (i, 0, 0)),
                   pl.BlockSpec((None, LANES, nrow), lambda i: (i, 0, 0))],
        compiler_params=pltpu.CompilerParams(dimension_semantics=("arbitrary",), vmem_limit_bytes=VMEM_LIMIT),
        name="compress",
    )(rk, rv, wka, wkb, wva, wvb, pek, pev, w1k, w1v, w2k, w2vt)


def _nsa_kernel(q_ref, kc_ref, vct_ref, ks_ref, kw_ref, vst_ref, vwt_ref, smt_ref, o_ref, sel_scr):
    g = pl.program_id(1)
    c = pl.program_id(2)
    t0 = c * QT
    ncmp = kc_ref.shape[0]
    nsel = sel_scr.shape[0]
    nw = WINDOW // KC + 1
    width = NSA_HPG * QT

    qs = jnp.concatenate([q_ref[:, h * LANES:(h + 1) * LANES] for h in range(NSA_HPG)], axis=0)
    u_row = lax.broadcasted_iota(jnp.int32, (1, width), 1) % QT
    t_row = t0 + u_row

    sc = _dot_nt(kc_ref[...], qs)
    n_col = lax.broadcasted_iota(jnp.int32, (ncmp, 1), 0)
    cmask = (CMP_STRIDE * n_col + (CMP_LEN - 1) <= t_row) & (n_col < ncmp - 1)
    s = jnp.where(cmask, sc, NEG)
    m = jnp.max(s, axis=0, keepdims=True)
    e = jnp.exp(s - m)
    anyv = (t_row >= CMP_LEN - 1).astype(F32)
    p = e * (anyv / jnp.sum(e, axis=0, keepdims=True))
    o_cmp = _dot(vct_ref[...], p.astype(BF16))

    psum = p[:, 0:QT]
    for h in range(1, NSA_HPG):
        psum = psum + p[:, h * QT:(h + 1) * QT]
    s_col = lax.broadcasted_iota(jnp.int32, (nsel, 1), 0)
    n_lane = lax.broadcasted_iota(jnp.int32, (1, ncmp), 1)
    ov = ((CMP_STRIDE * n_lane < SEL_BLOCK * (s_col + 1)) & (CMP_STRIDE * n_lane + (CMP_LEN - 1) >= SEL_BLOCK * s_col)
          ).astype(BF16)
    hi = psum.astype(BF16)
    r1 = psum - hi.astype(F32)
    mid = r1.astype(BF16)
    lo = (r1 - mid.astype(F32)).astype(BF16)
    imp = _dot(ov, hi) + _dot(ov, mid) + _dot(ov, lo)
    t1 = t0 + lax.broadcasted_iota(jnp.int32, (1, QT), 1)
    cur = lax.shift_right_logical(t1, 6)
    forced = (s_col == 0) | (s_col == cur) | (s_col == cur - 1)
    valid = SEL_BLOCK * s_col <= t1
    val = jnp.where(valid, jnp.where(forced, imp + SEL_FORCE, imp), NEG)
    rank = jnp.zeros((nsel, QT), jnp.int32)
    for i in range(nsel):
        vi = val[i:i + 1, :]
        beats = (vi > val) | ((vi == val) & (s_col > i))
        rank = rank + beats.astype(jnp.int32)
    sel = jnp.where(rank < SEL_TOPK, 1.0, 0.0).astype(F32)
    sel_scr[...] = jnp.concatenate([sel] * NSA_HPG, axis=1)

    r_col = lax.broadcasted_iota(jnp.int32, (KC, 1), 0)
    top_half = r_col < SEL_BLOCK

    def flash(s, vt, carry):
        m_o, l_o, acc = carry
        m_n = jnp.maximum(m_o, jnp.max(s, axis=0, keepdims=True))
        a = jnp.exp(m_o - m_n)
        pj = jnp.exp(s - m_n)
        l_n = a * l_o + jnp.sum(pj, axis=0, keepdims=True)
        acc = a * acc + _dot(vt, pj.astype(BF16))
        return m_n, l_n, acc

    def sel_body(j, carry):
        kj = ks_ref[pl.ds(pl.multiple_of(j * KC, KC), KC), :]
        sj = _dot_nt(kj, qs)
        ra = sel_scr[pl.ds(2 * j, 1), :]
        rb = sel_scr[pl.ds(2 * j + 1, 1), :]
        keep = jnp.where(top_half, ra, rb) > 0.5
        return flash(jnp.where(keep, sj, NEG), vst_ref[j], carry)

    init = (jnp.full((1, width), NEG, F32), jnp.zeros((1, width), F32), jnp.zeros((NSA_HD, width), F32))
    carry = lax.fori_loop(0, c, sel_body, init)
    kd = ks_ref[pl.ds(pl.multiple_of(t0, KC), KC), :]
    sd = jnp.where(r_col <= u_row, _dot_nt(kd, qs), NEG)
    _, l_s, acc_s = flash(sd, vst_ref[c], carry)
    o_sel = acc_s / l_s

    jw0 = jnp.maximum(c - (nw - 1), 0)
    start = pl.multiple_of(jw0 * KC, KC)
    sw = _dot_nt(kw_ref[pl.ds(start, nw * KC), :], qs)
    key = start + lax.broadcasted_iota(jnp.int32, (nw * KC, 1), 0)
    sw = jnp.where((key <= t_row) & (key > t_row - WINDOW), sw, NEG)
    mw = jnp.max(sw, axis=0, keepdims=True)
    pw = jnp.exp(sw - mw)
    lw = jnp.sum(pw, axis=0, keepdims=True)
    pwb = pw.astype(BF16)
    acc_w = _dot(vwt_ref[jw0], pwb[0:KC, :])
    for i in range(1, nw):
        acc_w = acc_w + _dot(vwt_ref[jw0 + i], pwb[i * KC:(i + 1) * KC, :])
    o_win = acc_w / lw

    def gate_row(br):
        rows = [smt_ref[pl.ds(SM_GATE + 3 * (NSA_HPG * g + h) + br, 1), :] for h in range(NSA_HPG)]
        return _sigmoid(jnp.concatenate(rows, axis=1))

    o_t = gate_row(0) * o_cmp + gate_row(1) * o_sel + gate_row(2) * o_win
    for pr in range(NSA_HPG // 2):
        xp = jnp.concatenate([o_t[:, (2 * pr) * QT:(2 * pr + 1) * QT], o_t[:, (2 * pr + 1) * QT:(2 * pr + 2) * QT]], axis=0)
        o_ref[:, pr * LANES:(pr + 1) * LANES] = xp.T.astype(BF16)


def _nsa(qpad, kcb, vct, ks, kw, vst, vwt, smt, b, s):
    nq = s // QT
    ncmp = kcb.shape[1]
    return pl.pallas_call(
        _nsa_kernel,
        out_shape=jax.ShapeDtypeStruct((b * s, NSA_HEADS * NSA_HD), BF16),
        grid=(b, NSA_KV, nq),
        in_specs=[
            pl.BlockSpec((QT, NSA_HPG * LANES), lambda bi, g, c: (bi * nq + c, g)),
            pl.BlockSpec((None, ncmp, LANES), lambda bi, g, c: (bi, 0, 0)),
            pl.BlockSpec((None, NSA_HD, ncmp), lambda bi, g, c: (bi, g, 0)),
            pl.BlockSpec((s, LANES), lambda bi, g, c: (bi, 0)),
            pl.BlockSpec((s, LANES), lambda bi, g, c: (bi, 0)),
            pl.BlockSpec((s // KC, NSA_HD, KC), lambda bi, g, c: (bi, g, 0)),
            pl.BlockSpec((s // KC, NSA_HD, KC), lambda bi, g, c: (bi, g, 0)),
            pl.BlockSpec((LANES, QT), lambda bi, g, c: (0, bi * nq + c)),
        ],
        out_specs=pl.BlockSpec((QT, NSA_HPG * NSA_HD), lambda bi, g, c: (bi * nq + c, g)),
        scratch_shapes=[pltpu.VMEM((s // SEL_BLOCK, NSA_HPG * QT), F32)],
        compiler_params=pltpu.CompilerParams(dimension_semantics=("arbitrary", "arbitrary", "arbitrary"),
                                             vmem_limit_bytes=VMEM_LIMIT),
        name="nsa",
    )(qpad, kcb, vct, ks, kw, vst, vwt, smt)


def _log_sigmoid(x):
    return jnp.minimum(x, 0.0) - jnp.log(1.0 + jnp.exp(-jnp.abs(x)))


def _mlstm_kernel(qk_ref, v_ref, og_ref, sm_ref, smt_ref, wc_ref, bc_ref, bifc_ref, bifr_ref, ghn_ref,
                  y_ref, ext_scr, ct_scr, n_scr, m_scr):
    lc = ML_CHUNK

    @pl.when(pl.program_id(1) == 0)
    def _():
        ext_scr[0:8, :] = jnp.zeros((8, 2 * ML_WIDTH), F32)
        ct_scr[...] = jnp.zeros_like(ct_scr)
        n_scr[...] = jnp.zeros_like(n_scr)
        m_scr[...] = jnp.zeros_like(m_scr)

    ext_scr[8:8 + ML_BLOCK, :] = qk_ref[...].astype(F32)
    y = bc_ref[...]
    for j in range(CONV_W):
        y = y + wc_ref[j:j + 1, :] * ext_scr[pl.ds(8 - (CONV_W - 1) + j, ML_BLOCK), :]
    ext_scr[0:8, :] = ext_scr[ML_BLOCK:ML_BLOCK + 8, :]
    qkc = y * _sigmoid(y)
    q_all = qkc[:, 0:ML_WIDTH].astype(BF16)
    k_all = (qkc[:, ML_WIDTH:2 * ML_WIDTH] * (ML_HD ** -0.5)).astype(BF16)

    ifc = sm_ref[...] + bifc_ref[...]
    ifr = smt_ref[...] + bifr_ref[...]
    lfc = _log_sigmoid(ifc)
    lfr = _log_sigmoid(ifr)
    rr = lax.broadcasted_iota(jnp.int32, (lc, lc), 0)
    cc = lax.broadcasted_iota(jnp.int32, (lc, lc), 1)
    causal = rr >= cc
    tri_l = causal.astype(F32)
    tri_u = (rr <= cc).astype(F32)

    for ci in range(ML_BLOCK // lc):
        lo, hi = ci * lc, (ci + 1) * lc
        bc_all = _dot_hi(tri_l, lfc[lo:hi, :])
        br_all = _dot_hi(lfr[:, lo:hi], tri_u)
        for h in range(ML_HEADS):
            hs = slice(h * ML_HD, (h + 1) * ML_HD)
            bcol = bc_all[:, SM_F + h:SM_F + h + 1]
            brow = br_all[SM_F + h:SM_F + h + 1, :]
            icol = ifc[lo:hi, SM_I + h:SM_I + h + 1]
            irow = ifr[SM_I + h:SM_I + h + 1, lo:hi]
            mprev = m_scr[h][:, 0:1]
            qh = q_all[lo:hi, hs]
            kh = k_all[lo:hi, hs]
            vh = v_ref[lo:hi, hs]
            dmat = jnp.where(causal, bcol - brow + irow, NEG)
            inter = bcol + mprev
            mt = jnp.maximum(jnp.max(dmat, axis=-1, keepdims=True), inter)
            a = jnp.exp(dmat - mt) * _dot_nt(qh, kh)
            dec = jnp.exp(inter - mt)
            ct = ct_scr[h]
            nrow = n_scr[h]
            num = _dot(a.astype(BF16), vh) + dec * _dot(qh, ct.astype(BF16))
            den = jnp.sum(a, axis=-1, keepdims=True) + dec * jnp.sum(qh.astype(F32) * nrow, axis=-1, keepdims=True)
            hh = num / jnp.maximum(jnp.abs(den), jnp.exp(-mt))
            blast = bcol[lc - 1:lc, :]
            grow = blast - brow + irow
            mnew = jnp.maximum(blast + mprev, jnp.max(grow, axis=-1, keepdims=True))
            wprev = jnp.exp(blast + mprev - mnew)
            kwt = kh.astype(F32) * jnp.exp(blast - bcol + icol - mnew)
            ct_scr[h] = wprev * ct + lax.dot_general(kwt.astype(BF16), vh, _TN, preferred_element_type=F32)
            n_scr[h] = wprev * nrow + jnp.sum(kwt, axis=0, keepdims=True)
            m_scr[h] = jnp.broadcast_to(mnew, (1, LANES))
            hm = hh * _sigmoid(og_ref[lo:hi, hs].astype(F32))
            y_ref[lo:hi, hs] = _rms(hm, ghn_ref[:, hs]).astype(BF16)


def _mlstm(qkb, vb, ob, sm, smt, wconv, bconv, bifc, bifr, ghn, b, s):
    nb = s // ML_BLOCK
    row = lambda w: pl.BlockSpec((ML_BLOCK, w), lambda bi, j: (bi * nb + j, 0))
    return pl.pallas_call(
        _mlstm_kernel,
        out_shape=jax.ShapeDtypeStruct((b * s, ML_WIDTH), BF16),
        grid=(b, nb),
        in_specs=[row(2 * ML_WIDTH), row(ML_WIDTH), row(ML_WIDTH), row(LANES),
                  pl.BlockSpec((LANES, ML_BLOCK), lambda bi, j: (0, bi * nb + j)),
                  _const_spec(wconv.shape), _const_spec(bconv.shape), _const_spec(bifc.shape),
                  _const_spec(bifr.shape), _const_spec(ghn.shape)],
        out_specs=row(ML_WIDTH),
        scratch_shapes=[pltpu.VMEM((ML_BLOCK + 8, 2 * ML_WIDTH), F32),
                        pltpu.VMEM((ML_HEADS, ML_HD, ML_HD), F32),
                        pltpu.VMEM((ML_HEADS, 1, ML_HD), F32),
                        pltpu.VMEM((ML_HEADS, 1, LANES), F32)],
        compiler_params=pltpu.CompilerParams(dimension_semantics=("arbitrary", "arbitrary"),
                                             vmem_limit_bytes=VMEM_LIMIT),
        name="mlstm",
    )(qkb, vb, ob, sm, smt, wconv, bconv, bifc, bifr, ghn)


def _merge_kernel(ya_ref, yb_ref, mg_ref, x_ref, wpa_ref, wpb_ref, wout_ref, gffn_ref, wr_ref, br_ref,
                  x1_ref, h2_ref, gate_ref):
    pa = _dot(ya_ref[...], wpa_ref[...])
    pb = _dot(yb_ref[...], wpb_ref[...])
    ga = _sigmoid(mg_ref[:, 0:D_MODEL].astype(F32))
    gb = _sigmoid(mg_ref[:, D_MODEL:2 * D_MODEL].astype(F32))
    mixed = (ga * pa + gb * pb).astype(BF16)
    x1 = x_ref[...] + _dot(mixed, wout_ref[...])
    x1_ref[...] = x1
    h2 = _rms(x1, gffn_ref[...])
    h2_ref[...] = h2.astype(BF16)

    logit = _dot_hi(h2, wr_ref[...]) + br_ref[...]
    lane = lax.broadcasted_iota(jnp.int32, logit.shape, 1)
    big = jnp.int32(LANES)
    gmask = (lane >= N_EXPERTS) & (lane < N_EXPERTS + N_GROUPS)
    gl = jnp.where(gmask, logit, NEG)
    gmax = jnp.max(gl, axis=-1, keepdims=True)
    gidx = jnp.min(jnp.where(gmask & (gl == gmax), lane, big), axis=-1, keepdims=True) - N_EXPERTS
    pg_sel = 1.0 / jnp.sum(jnp.where(gmask, jnp.exp(gl - gmax), 0.0), axis=-1, keepdims=True)
    emask = (lane < N_EXPERTS) & (lax.shift_right_logical(lane, 2) == gidx)
    el = jnp.where(emask, logit, NEG)
    e1 = jnp.max(el, axis=-1, keepdims=True)
    i1 = jnp.min(jnp.where(emask & (el == e1), lane, big), axis=-1, keepdims=True)
    emask2 = emask & (lane != i1)
    el2 = jnp.where(emask2, logit, NEG)
    e2 = jnp.max(el2, axis=-1, keepdims=True)
    i2 = jnp.min(jnp.where(emask2 & (el2 == e2), lane, big), axis=-1, keepdims=True)
    x21 = jnp.exp(e2 - e1)
    w1 = pg_sel / (1.0 + x21)
    w2 = pg_sel * x21 / (1.0 + x21)
    gate_ref[...] = jnp.where(lane == i1, w1, 0.0) + jnp.where(lane == i2, w2, 0.0)


def _merge(ya, yb, mg, x2, wpa, wpb, wout, gffn, wr, br):
    t = x2.shape[0]
    row = lambda w: pl.BlockSpec((TM, w), lambda i: (i, 0))
    return pl.pallas_call(
        _merge_kernel,
        out_shape=[jax.ShapeDtypeStruct((t, D_MODEL), F32),
                   jax.ShapeDtypeStruct((t, D_MODEL), BF16),
                   jax.ShapeDtypeStruct((t, LANES), F32)],
        grid=(t // TM,),
        in_specs=[row(NSA_HEADS * NSA_HD), row(ML_WIDTH), row(2 * D_MODEL), row(D_MODEL)]
                 + [_const_spec(a.shape) for a in (wpa, wpb, wout, gffn, wr, br)],
        out_specs=[row(D_MODEL), row(D_MODEL), row(LANES)],
        compiler_params=pltpu.CompilerParams(dimension_semantics=("arbitrary",), vmem_limit_bytes=VMEM_LIMIT),
        name="merge",
    )(ya, yb, mg, x2, wpa, wpb, wout, gffn, wr, br)


def _moe_kernel(h2_ref, gate_ref, x1_ref, p_ref, w13_ref, w2_ref, gple_ref, wpg_ref, wpp_ref, gfin_ref,
                o_ref, acc_scr):
    h2 = h2_ref[...]
    gate = gate_ref[...]
    acc_scr[...] = x1_ref[...]
    for e in range(N_EXPERTS):
        a = _dot(h2, w13_ref[e])
        gt = a[:, 0:D_EXPERT]
        act = gt * _sigmoid(gt) * a[:, D_EXPERT:2 * D_EXPERT] * gate[:, e:e + 1]
        acc_scr[...] += _dot(act.astype(BF16), w2_ref[e])
    x2 = acc_scr[...]
    h3 = _rms(x2, gple_ref[...]).astype(BF16)
    x3 = x2 + _sigmoid(_dot(h3, wpg_ref[...])) * _dot(p_ref[...].astype(BF16), wpp_ref[...])
    o_ref[...] = _rms(x3, gfin_ref[...])


def _moe(h2, gate, x1, p2, w13, w2, gple, wpg, wpp, gfin):
    t = h2.shape[0]
    row = lambda w: pl.BlockSpec((TM, w), lambda i: (i, 0))
    return pl.pallas_call(
        _moe_kernel,
        out_shape=jax.ShapeDtypeStruct((t, D_MODEL), F32),
        grid=(t // TM,),
        in_specs=[row(D_MODEL), row(LANES), row(D_MODEL), row(PLE_DIM)]
                 + [_const_spec(a.shape) for a in (w13, w2, gple, wpg, wpp, gfin)],
        out_specs=row(D_MODEL),
        scratch_shapes=[pltpu.VMEM((TM, D_MODEL), F32)],
        compiler_params=pltpu.CompilerParams(dimension_semantics=("arbitrary",), vmem_limit_bytes=VMEM_LIMIT),
        name="moe",
    )(h2, gate, x1, p2, w13, w2, gple, wpg, wpp, gfin)


def _pack_inproj_weights(w):
    d = w.shape[0]
    qw = NSA_HEADS * NSA_HD
    kvw = NSA_KV * NSA_HD
    o = 0
    wq = w[:, o:o + qw]; o += qw
    wkc = w[:, o:o + kvw]; o += kvw
    wvc = w[:, o:o + kvw]; o += kvw
    wks = w[:, o:o + kvw]; o += kvw
    wvs = w[:, o:o + kvw]; o += kvw
    wkw = w[:, o:o + kvw]; o += kvw
    wvw = w[:, o:o + kvw]; o += kvw
    wga = w[:, o:o + 3 * NSA_HEADS]; o += 3 * NSA_HEADS
    wqkb = w[:, o:o + 2 * ML_WIDTH]; o += 2 * ML_WIDTH
    wvb = w[:, o:o + ML_WIDTH]; o += ML_WIDTH
    wob = w[:, o:o + ML_WIDTH]; o += ML_WIDTH
    wif = w[:, o:o + 2 * ML_HEADS]; o += 2 * ML_HEADS
    wmg = w[:, o:o + 2 * D_MODEL]
    zero64 = jnp.zeros((d, NSA_HD), w.dtype)
    qcols = []
    for h in range(NSA_HEADS):
        wh = wq[:, h * NSA_HD:(h + 1) * NSA_HD]
        qcols += [wh, zero64] if h // NSA_HPG == 0 else [zero64, wh]
    wsm = jnp.concatenate([wga, wif, jnp.zeros((d, LANES - 3 * NSA_HEADS - 2 * ML_HEADS), w.dtype)], axis=1)
    wcat = jnp.concatenate(qcols + [wkc, wvc, wks, wkw, wsm, wqkb, wvb, wob, wmg], axis=1).astype(BF16)
    wtr = jnp.concatenate([wvs, wvw, wsm], axis=1).T.astype(BF16)
    return wcat, wtr


def _rope_lane_tables(positions):
    inv = ROPE_THETA ** (-jnp.arange(0, ROPE_DIM, 2, dtype=F32) / ROPE_DIM)
    ang = positions.astype(F32).reshape(-1, 1) * inv[None, :]
    cos, sin = jnp.cos(ang), jnp.sin(ang)
    half = ROPE_DIM // 2
    d = jnp.arange(LANES) % NSA_HD
    cos_l, sin_l = cos[:, d % half], sin[:, d % half]
    rc = jnp.where(d < ROPE_DIM, cos_l, 1.0)
    rp = jnp.where((d >= half) & (d < ROPE_DIM), sin_l, 0.0)
    rm = jnp.where(d < half, -sin_l, 0.0)
    return rc, rp, rm


def _pack_compress_weights(w1, w2, pe):
    half = CMP_LEN // 2
    w1r = w1.reshape(2, half, NSA_HD, CMP_HIDDEN)
    outs = []
    for part in range(2):
        wp = jnp.zeros((half, NSA_KV, NSA_HD, NSA_KV, CMP_HIDDEN), w1.dtype)
        for g in range(NSA_KV):
            wp = wp.at[:, g, :, g, :].set(w1r[part])
        outs.append(wp.reshape(half * NSA_KV * NSA_HD, NSA_KV * CMP_HIDDEN).astype(BF16))
    pe8 = jnp.broadcast_to(pe.reshape(1, CMP_LEN * NSA_HD), (8, CMP_LEN * NSA_HD)).astype(BF16)
    return outs[0], outs[1], pe8, w1.astype(BF16)


def kernel(x, p, positions, g_mix, w_in, b_if, w_ck1, w_ck2, pe_ck, w_cv1, w_cv2, pe_cv, w_conv, b_conv, g_hn, w_pa, w_pb, w_out, g_ffn, w_rg, b_rg, w_re, b_re, w_e13, w_e2, g_ple, w_pg, w_pp, g_final):
    b, s, d = x.shape
    t = b * s
    rc, rp, rm = _rope_lane_tables(positions)
    assert w_in.shape[0] == 1, "the final norm is fused into the layer's last kernel: single-layer problem only"
    for i in range(w_in.shape[0]):
        x2 = x.reshape(t, d)
        wcat, wtr = _pack_inproj_weights(w_in[i])
        (qpad, kc_tok, vc_tok, ks, kw, vst, vwt, sm, smt, qkb, vb, ob, mg) = _inproj(
            x2, g_mix[i].reshape(1, d), wcat, wtr, rc, rp, rm)
        wka, wkb, pek, w1k = _pack_compress_weights(w_ck1[i], w_ck2[i], pe_ck[i])
        wva, wvb, pev, w1v = _pack_compress_weights(w_cv1[i], w_cv2[i], pe_cv[i])
        zpad = jnp.zeros((CMP_HIDDEN, NSA_HD), F32)
        w2k = jnp.stack([jnp.concatenate([w_ck2[i], zpad], axis=1),
                         jnp.concatenate([zpad, w_ck2[i]], axis=1)]).astype(BF16)
        w2vt = w_cv2[i].T.astype(BF16)
        nrow = s // CMP_STRIDE
        rk = kc_tok.reshape(b, nrow, CMP_STRIDE * LANES)
        rv = vc_tok.reshape(b, nrow, CMP_STRIDE * LANES)
        kcb, vct = _compress(rk, rv, wka, wkb, wva, wvb, pek, pev, w1k, w1v, w2k, w2vt)
        ya = _nsa(qpad, kcb, vct, ks, kw, vst, vwt, smt, b, s)
        bif = b_if[i].astype(F32)
        bifc = jnp.zeros((1, LANES), F32).at[0, SM_I:SM_I + 2 * ML_HEADS].set(bif)
        bifr = bifc.reshape(LANES, 1)
        yb = _mlstm(qkb, vb, ob, sm, smt, w_conv[i], b_conv[i].reshape(1, -1), bifc, bifr,
                    g_hn[i].reshape(1, -1), b, s)
        wr = jnp.concatenate([w_re[i], w_rg[i], jnp.zeros((d, LANES - N_EXPERTS - N_GROUPS), F32)], axis=1)
        br = jnp.concatenate([b_re[i], b_rg[i], jnp.zeros((LANES - N_EXPERTS - N_GROUPS,), F32)]).reshape(1, LANES)
        x1, h2, gate = _merge(ya, yb, mg, x2, w_pa[i].astype(BF16), w_pb[i].astype(BF16), w_out[i].astype(BF16),
                              g_ffn[i].reshape(1, d), wr, br)
        gfin = g_final.reshape(1, d)
        out = _moe(h2, gate, x1, p[i].reshape(t, PLE_DIM), w_e13[i].astype(BF16), w_e2[i].astype(BF16),
                   g_ple[i].reshape(1, d), w_pg[i].astype(BF16), w_pp[i].astype(BF16), gfin)
        x = out.reshape(b, s, d)
    return x
```

```python
import functools
import math

import jax
import jax.numpy as jnp
from jax import lax
from jax.experimental import pallas as pl
from jax.experimental.pallas import tpu as pltpu

F32 = jnp.float32
BF16 = jnp.bfloat16

EPS = 1e-6
NEG = -1e30

D_MODEL = 1024
PLE_DIM = 256
NSA_HEADS = 8
NSA_KV = 2
NSA_HPG = NSA_HEADS // NSA_KV
NSA_HD = 64
CMP_LEN = 32
CMP_STRIDE = 16
CMP_HIDDEN = 256
SEL_BLOCK = 64
SEL_TOPK = 16
SEL_FORCE = 1000.0
WINDOW = 512
ROPE_THETA = 500000.0
ROPE_DIM = NSA_HD // 4
ML_HEADS = 4
ML_HD = 128
ML_WIDTH = ML_HEADS * ML_HD
CONV_W = 4
N_GROUPS = 4
EXP_PER_GROUP = 4
N_EXPERTS = N_GROUPS * EXP_PER_GROUP
D_EXPERT = 256

LANES = 128
QT = 128
KC = 128
SEL_GROUP = 512
ML_CHUNK = 128
ML_BLOCK = 256
TM = 256
VMEM_LIMIT = 56 * 1024 * 1024

_NT = (((1,), (1,)), ((), ()))
_TN = (((0,), (0,)), ((), ()))

SM_GATE = 0
SM_I = 3 * NSA_HEADS
SM_F = SM_I + ML_HEADS


def _dot(a, b):
    return jnp.dot(a, b, preferred_element_type=F32)


def _dot_nt(a, b):
    return lax.dot_general(a, b, _NT, preferred_element_type=F32)


def _dot_hi(a, b):
    return jnp.dot(a, b, preferred_element_type=F32, precision=lax.Precision.HIGHEST)


def _rms(x, g):
    return x * lax.rsqrt(jnp.mean(x * x, axis=-1, keepdims=True) + EPS) * g


def _sigmoid(x):
    return 1.0 / (1.0 + jnp.exp(-x))


def _const_spec(shape):
    nd = len(shape)
    return pl.BlockSpec(shape, lambda *_: (0,) * nd, pipeline_mode=pl.Buffered(1))


_C_Q = 0
_C_KC = _C_Q + NSA_HEADS * LANES
_C_VC = _C_KC + LANES
_C_KS = _C_VC + LANES
_C_KW = _C_KS + LANES
_C_SM = _C_KW + LANES
_C_QKB = _C_SM + LANES
_C_VB = _C_QKB + 2 * ML_WIDTH
_C_OB = _C_VB + ML_WIDTH
_C_MG = _C_OB + ML_WIDTH
_C_END = _C_MG + 2 * D_MODEL


def _inproj_kernel(x_ref, g_ref, w_ref, wt_ref, rc_ref, rp_ref, rm_ref,
                   q_ref, kc_ref, vc_ref, ks_ref, kw_ref, vst_ref, vwt_ref, sm_ref, smt_ref,
                   qkb_ref, vb_ref, ob_ref, mg_ref):
    hn = _rms(x_ref[...], g_ref[...]).astype(BF16)
    rc, rp, rm = rc_ref[...], rp_ref[...], rm_ref[...]

    def rope(z):
        return z * rc + pltpu.roll(z, 8, 1) * rp + pltpu.roll(z, LANES - 8, 1) * rm

    scale = NSA_HD ** -0.5
    for h in range(NSA_HEADS):
        z = _dot(hn, w_ref[:, _C_Q + h * LANES:_C_Q + (h + 1) * LANES])
        q_ref[:, h * LANES:(h + 1) * LANES] = (rope(z) * scale).astype(BF16)
    kc_ref[...] = rope(_dot(hn, w_ref[:, _C_KC:_C_KC + LANES])).astype(BF16)
    vc_ref[...] = _dot(hn, w_ref[:, _C_VC:_C_VC + LANES]).astype(BF16)
    ks_ref[...] = rope(_dot(hn, w_ref[:, _C_KS:_C_KS + LANES])).astype(BF16)
    kw_ref[...] = rope(_dot(hn, w_ref[:, _C_KW:_C_KW + LANES])).astype(BF16)
    sm_ref[...] = _dot(hn, w_ref[:, _C_SM:_C_SM + LANES])
    for c0 in range(0, 2 * ML_WIDTH, 512):
        qkb_ref[:, c0:c0 + 512] = _dot(hn, w_ref[:, _C_QKB + c0:_C_QKB + c0 + 512]).astype(BF16)
    vb_ref[...] = _dot(hn, w_ref[:, _C_VB:_C_VB + ML_WIDTH]).astype(BF16)
    ob_ref[...] = _dot(hn, w_ref[:, _C_OB:_C_OB + ML_WIDTH]).astype(BF16)
    for c0 in range(0, 2 * D_MODEL, 512):
        mg_ref[:, c0:c0 + 512] = _dot(hn, w_ref[:, _C_MG + c0:_C_MG + c0 + 512]).astype(BF16)
    zt = _dot_nt(wt_ref[...], hn)
    for i in range(TM // KC):
        vst_ref[i] = zt[0:LANES, i * KC:(i + 1) * KC].astype(BF16)
        vwt_ref[i] = zt[LANES:2 * LANES, i * KC:(i + 1) * KC].astype(BF16)
    smt_ref[...] = zt[2 * LANES:3 * LANES, :]


def _inproj(x2, g_mix, wcat, wtr, rc, rp, rm):
    t = x2.shape[0]
    row = lambda w: pl.BlockSpec((TM, w), lambda i: (i, 0))
    out_shape = [
        jax.ShapeDtypeStruct((t, NSA_HEADS * LANES), BF16),
        jax.ShapeDtypeStruct((t, LANES), BF16),
        jax.ShapeDtypeStruct((t, LANES), BF16),
        jax.ShapeDtypeStruct((t, LANES), BF16),
        jax.ShapeDtypeStruct((t, LANES), BF16),
        jax.ShapeDtypeStruct((t // KC, LANES, KC), BF16),
        jax.ShapeDtypeStruct((t // KC, LANES, KC), BF16),
        jax.ShapeDtypeStruct((t, LANES), F32),
        jax.ShapeDtypeStruct((LANES, t), F32),
        jax.ShapeDtypeStruct((t, 2 * ML_WIDTH), BF16),
        jax.ShapeDtypeStruct((t, ML_WIDTH), BF16),
        jax.ShapeDtypeStruct((t, ML_WIDTH), BF16),
        jax.ShapeDtypeStruct((t, 2 * D_MODEL), BF16),
    ]
    chunk3 = pl.BlockSpec((TM // KC, LANES, KC), lambda i: (i, 0, 0))
    out_specs = [row(NSA_HEADS * LANES), row(LANES), row(LANES), row(LANES), row(LANES), chunk3, chunk3,
                 row(LANES), pl.BlockSpec((LANES, TM), lambda i: (0, i)),
                 row(2 * ML_WIDTH), row(ML_WIDTH), row(ML_WIDTH), row(2 * D_MODEL)]
    return pl.pallas_call(
        _inproj_kernel,
        out_shape=out_shape,
        grid=(t // TM,),
        in_specs=[row(D_MODEL), _const_spec((1, D_MODEL)), _const_spec((D_MODEL, _C_END)),
                  _const_spec((3 * LANES, D_MODEL)), row(LANES), row(LANES), row(LANES)],
        out_specs=out_specs,
        compiler_params=pltpu.CompilerParams(dimension_semantics=("arbitrary",), vmem_limit_bytes=VMEM_LIMIT),
        name="inproj",
    )(x2, g_mix, wcat, wtr, rc, rp, rm)


def _gelu_tanh(x):
    return 0.5 * x * (1.0 + jnp.tanh(math.sqrt(2.0 / math.pi) * (x + 0.044715 * x * x * x)))


def _compress_kernel(rk_ref, rv_ref, wka_ref, wkb_ref, wva_ref, wvb_ref, pek_ref, pev_ref,
                     w1k_ref, w1v_ref, w2k_ref, w2vt_ref, kc_ref, vct_ref):
    nrow = rk_ref.shape[0]

    def hidden(r_ref, wa_ref, wb_ref, pe_ref, w1_ref):
        r = r_ref[...]
        ha = _dot(r, wa_ref[...])
        hb = _dot(r, wb_ref[...])
        hb = pltpu.roll(hb, nrow - 1, 0)
        c = _dot(pe_ref[...], w1_ref[...])[0:1, :]
        return [_gelu_tanh(ha[:, g * CMP_HIDDEN:(g + 1) * CMP_HIDDEN] + hb[:, g * CMP_HIDDEN:(g + 1) * CMP_HIDDEN] + c).astype(BF16)
                for g in range(NSA_KV)]

    ak = hidden(rk_ref, wka_ref, wkb_ref, pek_ref, w1k_ref)
    kc_ref[...] = (_dot(ak[0], w2k_ref[0]) + _dot(ak[1], w2k_ref[1])).astype(BF16)
    av = hidden(rv_ref, wva_ref, wvb_ref, pev_ref, w1v_ref)
    for g in range(NSA_KV):
        vct_ref[g * NSA_HD:(g + 1) * NSA_HD, :] = _dot_nt(w2vt_ref[...], av[g]).astype(BF16)


def _compress(rk, rv, wka, wkb, wva, wvb, pek, pev, w1k, w1v, w2k, w2vt):
    b, nrow, width = rk.shape
    blk = pl.BlockSpec((None, nrow, width), lambda i: (i, 0, 0))
    return pl.pallas_call(
        _compress_kernel,
        out_shape=[jax.ShapeDtypeStruct((b, nrow, LANES), BF16),
                   jax.ShapeDtypeStruct((b, LANES, nrow), BF16)],
        grid=(b,),
        in_specs=[blk, blk] + [_const_spec(a.shape) for a in (wka, wkb, wva, wvb, pek, pev, w1k, w1v, w2k, w2vt)],
        out_specs=[pl.BlockSpec((None, nrow, LANES), lambda i: (i, 0, 0)),
                   pl.BlockSpec((None, LANES, nrow), lambda i: (i, 0, 0))],
        compiler_params=pltpu.CompilerParams(dimension_semantics=("arbitrary",), vmem_limit_bytes=VMEM_LIMIT),
        name="compress",
    )(rk, rv, wka, wkb, wva, wvb, pek, pev, w1k, w1v, w2k, w2vt)


def _nsa_kernel(q_ref, kc_ref, vct_ref, ks_ref, kw_ref, vst_ref, vwt_ref, smt_ref, o_ref, sel_scr):
    g = pl.program_id(1)
    c = pl.program_id(2)
    t0 = c * QT
    ncmp = kc_ref.shape[0]
    nsel = sel_scr.shape[0]
    nw = WINDOW // KC + 1
    width = NSA_HPG * QT

    qs = jnp.concatenate([q_ref[:, h * LANES:(h + 1) * LANES] for h in range(NSA_HPG)], axis=0)
    u_row = lax.broadcasted_iota(jnp.int32, (1, width), 1) % QT
    t_row = t0 + u_row

    sc = _dot_nt(kc_ref[...], qs)
    n_col = lax.broadcasted_iota(jnp.int32, (ncmp, 1), 0)
    cmask = (CMP_STRIDE * n_col + (CMP_LEN - 1) <= t_row) & (n_col < ncmp - 1)
    s = jnp.where(cmask, sc, NEG)
    m = jnp.max(s, axis=0, keepdims=True)
    e = jnp.exp(s - m)
    anyv = (t_row >= CMP_LEN - 1).astype(F32)
    p = e * (anyv / jnp.sum(e, axis=0, keepdims=True))
    o_cmp = _dot(vct_ref[...], p.astype(BF16))

    psum = p[:, 0:QT]
    for h in range(1, NSA_HPG):
        psum = psum + p[:, h * QT:(h + 1) * QT]
    s_col = lax.broadcasted_iota(jnp.int32, (nsel, 1), 0)
    n_lane = lax.broadcasted_iota(jnp.int32, (1, ncmp), 1)
    ov = ((CMP_STRIDE * n_lane < SEL_BLOCK * (s_col + 1)) & (CMP_STRIDE * n_lane + (CMP_LEN - 1) >= SEL_BLOCK * s_col)
          ).astype(BF16)
    hi = psum.astype(BF16)
    r1 = psum - hi.astype(F32)
    mid = r1.astype(BF16)
    lo = (r1 - mid.astype(F32)).astype(BF16)
    imp = _dot(ov, hi) + _dot(ov, mid) + _dot(ov, lo)
    t1 = t0 + lax.broadcasted_iota(jnp.int32, (1, QT), 1)
    cur = lax.shift_right_logical(t1, 6)
    forced = (s_col == 0) | (s_col == cur) | (s_col == cur - 1)
    valid = SEL_BLOCK * s_col <= t1
    val = jnp.where(valid, jnp.where(forced, imp + SEL_FORCE, imp), NEG)
    rank = jnp.zeros((nsel, QT), jnp.int32)
    for i in range(nsel):
        vi = val[i:i + 1, :]
        beats = (vi > val) | ((vi == val) & (s_col > i))
        rank = rank + beats.astype(jnp.int32)
    sel = jnp.where(rank < SEL_TOPK, 1.0, 0.0).astype(F32)
    sel_scr[...] = jnp.concatenate([sel] * NSA_HPG, axis=1)

    r_blk = lax.broadcasted_iota(jnp.int32, (SEL_BLOCK, 1), 0)
    blk_per_grp = SEL_GROUP // SEL_BLOCK
    chunk_per_grp = SEL_GROUP // KC

    def sel_group(j, carry, causal):
        m_o, l_o, acc = carry
        k0 = pl.multiple_of(j * SEL_GROUP, SEL_GROUP)
        sj = _dot_nt(ks_ref[pl.ds(k0, SEL_GROUP), :], qs)
        slabs = []
        for i in range(blk_per_grp):
            keep = sel_scr[pl.ds(blk_per_grp * j + i, 1), :] > 0.5
            if causal:
                keep = keep & (k0 + i * SEL_BLOCK + r_blk <= t_row)
            slabs.append(jnp.where(keep, sj[i * SEL_BLOCK:(i + 1) * SEL_BLOCK, :], NEG))
        m_n = m_o
        for sl in slabs:
            m_n = jnp.maximum(m_n, jnp.max(sl, axis=0, keepdims=True))
        a = jnp.exp(m_o - m_n)
        l_n = a * l_o
        acc = a * acc
        for ci in range(chunk_per_grp):
            pj = jnp.exp(jnp.concatenate(slabs[2 * ci:2 * ci + 2], axis=0) - m_n)
            l_n = l_n + jnp.sum(pj, axis=0, keepdims=True)
            acc = acc + _dot(vst_ref[chunk_per_grp * j + ci], pj.astype(BF16))
        return m_n, l_n, acc

    n_full = lax.shift_right_logical(t0, int(math.log2(SEL_GROUP)))
    init = (jnp.full((1, width), NEG, F32), jnp.zeros((1, width), F32), jnp.zeros((NSA_HD, width), F32))
    carry = lax.fori_loop(0, n_full, functools.partial(sel_group, causal=False), init)
    _, l_s, acc_s = sel_group(n_full, carry, causal=True)
    o_sel = acc_s / l_s

    jw0 = jnp.maximum(c - (nw - 1), 0)
    start = pl.multiple_of(jw0 * KC, KC)
    sw = _dot_nt(kw_ref[pl.ds(start, nw * KC), :], qs)
    key = start + lax.broadcasted_iota(jnp.int32, (nw * KC, 1), 0)
    sw = jnp.where((key <= t_row) & (key > t_row - WINDOW), sw, NEG)
    mw = jnp.max(sw, axis=0, keepdims=True)
    pw = jnp.exp(sw - mw)
    lw = jnp.sum(pw, axis=0, keepdims=True)
    pwb = pw.astype(BF16)
    acc_w = _dot(vwt_ref[jw0], pwb[0:KC, :])
    for i in range(1, nw):
        acc_w = acc_w + _dot(vwt_ref[jw0 + i], pwb[i * KC:(i + 1) * KC, :])
    o_win = acc_w / lw

    def gate_row(br):
        rows = [smt_ref[pl.ds(SM_GATE + 3 * (NSA_HPG * g + h) + br, 1), :] for h in range(NSA_HPG)]
        return _sigmoid(jnp.concatenate(rows, axis=1))

    o_t = gate_row(0) * o_cmp + gate_row(1) * o_sel + gate_row(2) * o_win
    for pr in range(NSA_HPG // 2):
        xp = jnp.concatenate([o_t[:, (2 * pr) * QT:(2 * pr + 1) * QT], o_t[:, (2 * pr + 1) * QT:(2 * pr + 2) * QT]], axis=0)
        o_ref[:, pr * LANES:(pr + 1) * LANES] = xp.T.astype(BF16)


def _nsa(qpad, kcb, vct, ks, kw, vst, vwt, smt, b, s):
    nq = s // QT
    ncmp = kcb.shape[1]
    return pl.pallas_call(
        _nsa_kernel,
        out_shape=jax.ShapeDtypeStruct((b * s, NSA_HEADS * NSA_HD), BF16),
        grid=(b, NSA_KV, nq),
        in_specs=[
            pl.BlockSpec((QT, NSA_HPG * LANES), lambda bi, g, c: (bi * nq + c, g)),
            pl.BlockSpec((None, ncmp, LANES), lambda bi, g, c: (bi, 0, 0)),
            pl.BlockSpec((None, NSA_HD, ncmp), lambda bi, g, c: (bi, g, 0)),
            pl.BlockSpec((s, LANES), lambda bi, g, c: (bi, 0)),
            pl.BlockSpec((s, LANES), lambda bi, g, c: (bi, 0)),
            pl.BlockSpec((s // KC, NSA_HD, KC), lambda bi, g, c: (bi, g, 0)),
            pl.BlockSpec((s // KC, NSA_HD, KC), lambda bi, g, c: (bi, g, 0)),
            pl.BlockSpec((LANES, QT), lambda bi, g, c: (0, bi * nq + c)),
        ],
        out_specs=pl.BlockSpec((QT, NSA_HPG * NSA_HD), lambda bi, g, c: (bi * nq + c, g)),
        scratch_shapes=[pltpu.VMEM((s // SEL_BLOCK, NSA_HPG * QT), F32)],
        compiler_params=pltpu.CompilerParams(dimension_semantics=("arbitrary", "arbitrary", "arbitrary"),
                                             vmem_limit_bytes=VMEM_LIMIT),
        name="nsa",
    )(qpad, kcb, vct, ks, kw, vst, vwt, smt)


def _log_sigmoid(x):
    return jnp.minimum(x, 0.0) - jnp.log(1.0 + jnp.exp(-jnp.abs(x)))


def _mlstm_kernel(qk_ref, v_ref, og_ref, sm_ref, smt_ref, wc_ref, bc_ref, bifc_ref, bifr_ref, ghn_ref,
                  y_ref, ext_scr, ct_scr, n_scr, m_scr):
    lc = ML_CHUNK

    @pl.when(pl.program_id(1) == 0)
    def _():
        ext_scr[0:8, :] = jnp.zeros((8, 2 * ML_WIDTH), F32)
        ct_scr[...] = jnp.zeros_like(ct_scr)
        n_scr[...] = jnp.zeros_like(n_scr)
        m_scr[...] = jnp.zeros_like(m_scr)

    ext_scr[8:8 + ML_BLOCK, :] = qk_ref[...].astype(F32)
    y = bc_ref[...]
    for j in range(CONV_W):
        y = y + wc_ref[j:j + 1, :] * ext_scr[pl.ds(8 - (CONV_W - 1) + j, ML_BLOCK), :]
    ext_scr[0:8, :] = ext_scr[ML_BLOCK:ML_BLOCK + 8, :]
    qkc = y * _sigmoid(y)
    q_all = qkc[:, 0:ML_WIDTH].astype(BF16)
    k_all = (qkc[:, ML_WIDTH:2 * ML_WIDTH] * (ML_HD ** -0.5)).astype(BF16)

    ifc = sm_ref[...] + bifc_ref[...]
    ifr = smt_ref[...] + bifr_ref[...]
    lfc = _log_sigmoid(ifc)
    lfr = _log_sigmoid(ifr)
    rr = lax.broadcasted_iota(jnp.int32, (lc, lc), 0)
    cc = lax.broadcasted_iota(jnp.int32, (lc, lc), 1)
    causal = rr >= cc
    tri_l = causal.astype(F32)
    tri_u = (rr <= cc).astype(F32)

    for ci in range(ML_BLOCK // lc):
        lo, hi = ci * lc, (ci + 1) * lc
        bc_all = _dot_hi(tri_l, lfc[lo:hi, :])
        br_all = _dot_hi(lfr[:, lo:hi], tri_u)
        for h in range(ML_HEADS):
            hs = slice(h * ML_HD, (h + 1) * ML_HD)
            bcol = bc_all[:, SM_F + h:SM_F + h + 1]
            brow = br_all[SM_F + h:SM_F + h + 1, :]
            icol = ifc[lo:hi, SM_I + h:SM_I + h + 1]
            irow = ifr[SM_I + h:SM_I + h + 1, lo:hi]
            mprev = m_scr[h][:, 0:1]
            qh = q_all[lo:hi, hs]
            kh = k_all[lo:hi, hs]
            vh = v_ref[lo:hi, hs]
            dmat = jnp.where(causal, bcol - brow + irow, NEG)
            inter = bcol + mprev
            mt = jnp.maximum(jnp.max(dmat, axis=-1, keepdims=True), inter)
            a = jnp.exp(dmat - mt) * _dot_nt(qh, kh)
            dec = jnp.exp(inter - mt)
            ct = ct_scr[h]
            nrow = n_scr[h]
            num = _dot(a.astype(BF16), vh) + dec * _dot(qh, ct.astype(BF16))
            den = jnp.sum(a, axis=-1, keepdims=True) + dec * jnp.sum(qh.astype(F32) * nrow, axis=-1, keepdims=True)
            hh = num / jnp.maximum(jnp.abs(den), jnp.exp(-mt))
            blast = bcol[lc - 1:lc, :]
            grow = blast - brow + irow
            mnew = jnp.maximum(blast + mprev, jnp.max(grow, axis=-1, keepdims=True))
            wprev = jnp.exp(blast + mprev - mnew)
            kwt = kh.astype(F32) * jnp.exp(blast - bcol + icol - mnew)
            ct_scr[h] = wprev * ct + lax.dot_general(kwt.astype(BF16), vh, _TN, preferred_element_type=F32)
            n_scr[h] = wprev * nrow + jnp.sum(kwt, axis=0, keepdims=True)
            m_scr[h] = jnp.broadcast_to(mnew, (1, LANES))
            hm = hh * _sigmoid(og_ref[lo:hi, hs].astype(F32))
            y_ref[lo:hi, hs] = _rms(hm, ghn_ref[:, hs]).astype(BF16)


def _mlstm(qkb, vb, ob, sm, smt, wconv, bconv, bifc, bifr, ghn, b, s):
    nb = s // ML_BLOCK
    row = lambda w: pl.BlockSpec((ML_BLOCK, w), lambda bi, j: (bi * nb + j, 0))
    return pl.pallas_call(
        _mlstm_kernel,
        out_shape=jax.ShapeDtypeStruct((b * s, ML_WIDTH), BF16),
        grid=(b, nb),
        in_specs=[row(2 * ML_WIDTH), row(ML_WIDTH), row(ML_WIDTH), row(LANES),
                  pl.BlockSpec((LANES, ML_BLOCK), lambda bi, j: (0, bi * nb + j)),
                  _const_spec(wconv.shape), _const_spec(bconv.shape), _const_spec(bifc.shape),
                  _const_spec(bifr.shape), _const_spec(ghn.shape)],
        out_specs=row(ML_WIDTH),
        scratch_shapes=[pltpu.VMEM((ML_BLOCK + 8, 2 * ML_WIDTH), F32),
                        pltpu.VMEM((ML_HEADS, ML_HD, ML_HD), F32),
                        pltpu.VMEM((ML_HEADS, 1, ML_HD), F32),
                        pltpu.VMEM((ML_HEADS, 1, LANES), F32)],
        compiler_params=pltpu.CompilerParams(dimension_semantics=("arbitrary", "arbitrary"),
                                             vmem_limit_bytes=VMEM_LIMIT),
        name="mlstm",
    )(qkb, vb, ob, sm, smt, wconv, bconv, bifc, bifr, ghn)


def _merge_kernel(ya_ref, yb_ref, mg_ref, x_ref, wpa_ref, wpb_ref, wout_ref, gffn_ref, wr_ref, br_ref,
                  x1_ref, h2_ref, gate_ref):
    pa = _dot(ya_ref[...], wpa_ref[...])
    pb = _dot(yb_ref[...], wpb_ref[...])
    ga = _sigmoid(mg_ref[:, 0:D_MODEL].astype(F32))
    gb = _sigmoid(mg_ref[:, D_MODEL:2 * D_MODEL].astype(F32))
    mixed = (ga * pa + gb * pb).astype(BF16)
    x1 = x_ref[...] + _dot(mixed, wout_ref[...])
    x1_ref[...] = x1
    h2 = _rms(x1, gffn_ref[...])
    h2_ref[...] = h2.astype(BF16)

    logit = _dot_hi(h2, wr_ref[...]) + br_ref[...]
    lane = lax.broadcasted_iota(jnp.int32, logit.shape, 1)
    big = jnp.int32(LANES)
    gmask = (lane >= N_EXPERTS) & (lane < N_EXPERTS + N_GROUPS)
    gl = jnp.where(gmask, logit, NEG)
    gmax = jnp.max(gl, axis=-1, keepdims=True)
    gidx = jnp.min(jnp.where(gmask & (gl == gmax), lane, big), axis=-1, keepdims=True) - N_EXPERTS
    pg_sel = 1.0 / jnp.sum(jnp.where(gmask, jnp.exp(gl - gmax), 0.0), axis=-1, keepdims=True)
    emask = (lane < N_EXPERTS) & (lax.shift_right_logical(lane, 2) == gidx)
    el = jnp.where(emask, logit, NEG)
    e1 = jnp.max(el, axis=-1, keepdims=True)
    i1 = jnp.min(jnp.where(emask & (el == e1), lane, big), axis=-1, keepdims=True)
    emask2 = emask & (lane != i1)
    el2 = jnp.where(emask2, logit, NEG)
    e2 = jnp.max(el2, axis=-1, keepdims=True)
    i2 = jnp.min(jnp.where(emask2 & (el2 == e2), lane, big), axis=-1, keepdims=True)
    x21 = jnp.exp(e2 - e1)
    w1 = pg_sel / (1.0 + x21)
    w2 = pg_sel * x21 / (1.0 + x21)
    gate_ref[...] = jnp.where(lane == i1, w1, 0.0) + jnp.where(lane == i2, w2, 0.0)


def _merge(ya, yb, mg, x2, wpa, wpb, wout, gffn, wr, br):
    t = x2.shape[0]
    row = lambda w: pl.BlockSpec((TM, w), lambda i: (i, 0))
    return pl.pallas_call(
        _merge_kernel,
        out_shape=[jax.ShapeDtypeStruct((t, D_MODEL), F32),
                   jax.ShapeDtypeStruct((t, D_MODEL), BF16),
                   jax.ShapeDtypeStruct((t, LANES), F32)],
        grid=(t // TM,),
        in_specs=[row(NSA_HEADS * NSA_HD), row(ML_WIDTH), row(2 * D_MODEL), row(D_MODEL)]
                 + [_const_spec(a.shape) for a in (wpa, wpb, wout, gffn, wr, br)],
        out_specs=[row(D_MODEL), row(D_MODEL), row(LANES)],
        compiler_params=pltpu.CompilerParams(dimension_semantics=("arbitrary",), vmem_limit_bytes=VMEM_LIMIT),
        name="merge",
    )(ya, yb, mg, x2, wpa, wpb, wout, gffn, wr, br)


def _moe_kernel(h2_ref, gate_ref, x1_ref, p_ref, w13_ref, w2_ref, gple_ref, wpg_ref, wpp_ref, gfin_ref,
                o_ref, acc_scr):
    h2 = h2_ref[...]
    gate = gate_ref[...]
    acc_scr[...] = x1_ref[...]
    for e in range(N_EXPERTS):
        a = _dot(h2, w13_ref[e])
        gt = a[:, 0:D_EXPERT]
        act = gt * _sigmoid(gt) * a[:, D_EXPERT:2 * D_EXPERT] * gate[:, e:e + 1]
        acc_scr[...] += _dot(act.astype(BF16), w2_ref[e])
    x2 = acc_scr[...]
    h3 = _rms(x2, gple_ref[...]).astype(BF16)
    x3 = x2 + _sigmoid(_dot(h3, wpg_ref[...])) * _dot(p_ref[...].astype(BF16), wpp_ref[...])
    o_ref[...] = _rms(x3, gfin_ref[...])


def _moe(h2, gate, x1, p2, w13, w2, gple, wpg, wpp, gfin):
    t = h2.shape[0]
    row = lambda w: pl.BlockSpec((TM, w), lambda i: (i, 0))
    return pl.pallas_call(
        _moe_kernel,
        out_shape=jax.ShapeDtypeStruct((t, D_MODEL), F32),
        grid=(t // TM,),
        in_specs=[row(D_MODEL), row(LANES), row(D_MODEL), row(PLE_DIM)]
                 + [_const_spec(a.shape) for a in (w13, w2, gple, wpg, wpp, gfin)],
        out_specs=row(D_MODEL),
        scratch_shapes=[pltpu.VMEM((TM, D_MODEL), F32)],
        compiler_params=pltpu.CompilerParams(dimension_semantics=("arbitrary",), vmem_limit_bytes=VMEM_LIMIT),
        name="moe",
    )(h2, gate, x1, p2, w13, w2, gple, wpg, wpp, gfin)


def _pack_inproj_weights(w):
    d = w.shape[0]
    qw = NSA_HEADS * NSA_HD
    kvw = NSA_KV * NSA_HD
    o = 0
    wq = w[:, o:o + qw]; o += qw
    wkc = w[:, o:o + kvw]; o += kvw
    wvc = w[:, o:o + kvw]; o += kvw
    wks = w[:, o:o + kvw]; o += kvw
    wvs = w[:, o:o + kvw]; o += kvw
    wkw = w[:, o:o + kvw]; o += kvw
    wvw = w[:, o:o + kvw]; o += kvw
    wga = w[:, o:o + 3 * NSA_HEADS]; o += 3 * NSA_HEADS
    wqkb = w[:, o:o + 2 * ML_WIDTH]; o += 2 * ML_WIDTH
    wvb = w[:, o:o + ML_WIDTH]; o += ML_WIDTH
    wob = w[:, o:o + ML_WIDTH]; o += ML_WIDTH
    wif = w[:, o:o + 2 * ML_HEADS]; o += 2 * ML_HEADS
    wmg = w[:, o:o + 2 * D_MODEL]
    zero64 = jnp.zeros((d, NSA_HD), w.dtype)
    qcols = []
    for h in range(NSA_HEADS):
        wh = wq[:, h * NSA_HD:(h + 1) * NSA_HD]
        qcols += [wh, zero64] if h // NSA_HPG == 0 else [zero64, wh]
    wsm = jnp.concatenate([wga, wif, jnp.zeros((d, LANES - 3 * NSA_HEADS - 2 * ML_HEADS), w.dtype)], axis=1)
    wcat = jnp.concatenate(qcols + [wkc, wvc, wks, wkw, wsm, wqkb, wvb, wob, wmg], axis=1).astype(BF16)
    wtr = jnp.concatenate([wvs, wvw, wsm], axis=1).T.astype(BF16)
    return wcat, wtr


def _rope_lane_tables(positions):
    inv = ROPE_THETA ** (-jnp.arange(0, ROPE_DIM, 2, dtype=F32) / ROPE_DIM)
    ang = positions.astype(F32).reshape(-1, 1) * inv[None, :]
    cos, sin = jnp.cos(ang), jnp.sin(ang)
    half = ROPE_DIM // 2
    d = jnp.arange(LANES) % NSA_HD
    cos_l, sin_l = cos[:, d % half], sin[:, d % half]
    rc = jnp.where(d < ROPE_DIM, cos_l, 1.0)
    rp = jnp.where((d >= half) & (d < ROPE_DIM), sin_l, 0.0)
    rm = jnp.where(d < half, -sin_l, 0.0)
    return rc, rp, rm


def _pack_compress_weights(w1, w2, pe):
    half = CMP_LEN // 2
    w1r = w1.reshape(2, half, NSA_HD, CMP_HIDDEN)
    outs = []
    for part in range(2):
        wp = jnp.zeros((half, NSA_KV, NSA_HD, NSA_KV, CMP_HIDDEN), w1.dtype)
        for g in range(NSA_KV):
            wp = wp.at[:, g, :, g, :].set(w1r[part])
        outs.append(wp.reshape(half * NSA_KV * NSA_HD, NSA_KV * CMP_HIDDEN).astype(BF16))
    pe8 = jnp.broadcast_to(pe.reshape(1, CMP_LEN * NSA_HD), (8, CMP_LEN * NSA_HD)).astype(BF16)
    return outs[0], outs[1], pe8, w1.astype(BF16)


def kernel(x, p, positions, g_mix, w_in, b_if, w_ck1, w_ck2, pe_ck, w_cv1, w_cv2, pe_cv, w_conv, b_conv, g_hn, w_pa, w_pb, w_out, g_ffn, w_rg, b_rg, w_re, b_re, w_e13, w_e2, g_ple, w_pg, w_pp, g_final):
    b, s, d = x.shape
    t = b * s
    rc, rp, rm = _rope_lane_tables(positions)
    assert w_in.shape[0] == 1, "the final norm is fused into the layer's last kernel: single-layer problem only"
    for i in range(w_in.shape[0]):
        x2 = x.reshape(t, d)
        wcat, wtr = _pack_inproj_weights(w_in[i])
        (qpad, kc_tok, vc_tok, ks, kw, vst, vwt, sm, smt, qkb, vb, ob, mg) = _inproj(
            x2, g_mix[i].reshape(1, d), wcat, wtr, rc, rp, rm)
        wka, wkb, pek, w1k = _pack_compress_weights(w_ck1[i], w_ck2[i], pe_ck[i])
        wva, wvb, pev, w1v = _pack_compress_weights(w_cv1[i], w_cv2[i], pe_cv[i])
        zpad = jnp.zeros((CMP_HIDDEN, NSA_HD), F32)
        w2k = jnp.stack([jnp.concatenate([w_ck2[i], zpad], axis=1),
                         jnp.concatenate([zpad, w_ck2[i]], axis=1)]).astype(BF16)
        w2vt = w_cv2[i].T.astype(BF16)
        nrow = s // CMP_STRIDE
        rk = kc_tok.reshape(b, nrow, CMP_STRIDE * LANES)
        rv = vc_tok.reshape(b, nrow, CMP_STRIDE * LANES)
        kcb, vct = _compress(rk, rv, wka, wkb, wva, wvb, pek, pev, w1k, w1v, w2k, w2vt)
        ya = _nsa(qpad, kcb, vct, ks, kw, vst, vwt, smt, b, s)
        bif = b_if[i].astype(F32)
        bifc = jnp.zeros((1, LANES), F32).at[0, SM_I:SM_I + 2 * ML_HEADS].set(bif)
        bifr = bifc.reshape(LANES, 1)
        yb = _mlstm(qkb, vb, ob, sm, smt, w_conv[i], b_conv[i].reshape(1, -1), bifc, bifr,
                    g_hn[i].reshape(1, -1), b, s)
        wr = jnp.concatenate([w_re[i], w_rg[i], jnp.zeros((d, LANES - N_EXPERTS - N_GROUPS), F32)], axis=1)
        br = jnp.concatenate([b_re[i], b_rg[i], jnp.zeros((LANES - N_EXPERTS - N_GROUPS,), F32)]).reshape(1, LANES)
        x1, h2, gate = _merge(ya, yb, mg, x2, w_pa[i].astype(BF16), w_pb[i].astype(BF16), w_out[i].astype(BF16),
                              g_ffn[i].reshape(1, d), wr, br)
        gfin = g_final.reshape(1, d)
        out = _moe(h2, gate, x1, p[i].reshape(t, PLE_DIM), w_e13[i].astype(BF16), w_e2[i].astype(BF16),
                   g_ple[i].reshape(1, d), w_pg[i].astype(BF16), w_pp[i].astype(BF16), gfin)
        x = out.reshape(b, s, d)
    return x
```

```python
import functools
import math

import jax
import jax.numpy as jnp
from jax import lax
from jax.experimental import pallas as pl
from jax.experimental.pallas import tpu as pltpu

F32 = jnp.float32
BF16 = jnp.bfloat16

EPS = 1e-6
NEG = -1e30

D_MODEL = 1024
PLE_DIM = 256
NSA_HEADS = 8
NSA_KV = 2
NSA_HPG = NSA_HEADS // NSA_KV
NSA_HD = 64
CMP_LEN = 32
CMP_STRIDE = 16
CMP_HIDDEN = 256
SEL_BLOCK = 64
SEL_TOPK = 16
SEL_FORCE = 1000.0
WINDOW = 512
ROPE_THETA = 500000.0
ROPE_DIM = NSA_HD // 4
ML_HEADS = 4
ML_HD = 128
ML_WIDTH = ML_HEADS * ML_HD
CONV_W = 4
N_GROUPS = 4
EXP_PER_GROUP = 4
N_EXPERTS = N_GROUPS * EXP_PER_GROUP
D_EXPERT = 256

LANES = 128
QT = 128
KC = 128
SEL_GROUP = 512
ML_CHUNK = 128
ML_BLOCK = 256
TM = 256
VMEM_LIMIT = 56 * 1024 * 1024

_NT = (((1,), (1,)), ((), ()))
_TN = (((0,), (0,)), ((), ()))

SM_GATE = 0
SM_I = 3 * NSA_HEADS
SM_F = SM_I + ML_HEADS


def _dot(a, b):
    return jnp.dot(a, b, preferred_element_type=F32)


def _dot_nt(a, b):
    return lax.dot_general(a, b, _NT, preferred_element_type=F32)


def _dot_hi(a, b):
    return jnp.dot(a, b, preferred_element_type=F32, precision=lax.Precision.HIGHEST)


def _rms(x, g):
    return x * lax.rsqrt(jnp.mean(x * x, axis=-1, keepdims=True) + EPS) * g


def _sigmoid(x):
    return 1.0 / (1.0 + jnp.exp(-x))


def _const_spec(shape):
    nd = len(shape)
    return pl.BlockSpec(shape, lambda *_: (0,) * nd, pipeline_mode=pl.Buffered(1))


_C_Q = 0
_C_KC = _C_Q + NSA_HEADS * LANES
_C_VC = _C_KC + LANES
_C_KS = _C_VC + LANES
_C_KW = _C_KS + LANES
_C_SM = _C_KW + LANES
_C_QKB = _C_SM + LANES
_C_VB = _C_QKB + 2 * ML_WIDTH
_C_OB = _C_VB + ML_WIDTH
_C_MG = _C_OB + ML_WIDTH
_C_END = _C_MG + 2 * D_MODEL


def _inproj_kernel(x_ref, g_ref, w_ref, wt_ref, rc_ref, rp_ref, rm_ref,
                   q_ref, kc_ref, vc_ref, ks_ref, kw_ref, vst_ref, vwt_ref, sm_ref, smt_ref,
                   qkb_ref, vb_ref, ob_ref, mg_ref):
    hn = _rms(x_ref[...], g_ref[...]).astype(BF16)
    rc, rp, rm = rc_ref[...], rp_ref[...], rm_ref[...]

    def rope(z):
        return z * rc + pltpu.roll(z, 8, 1) * rp + pltpu.roll(z, LANES - 8, 1) * rm

    scale = NSA_HD ** -0.5 * math.log2(math.e)
    for h in range(NSA_HEADS):
        z = _dot(hn, w_ref[:, _C_Q + h * LANES:_C_Q + (h + 1) * LANES])
        q_ref[:, h * LANES:(h + 1) * LANES] = (rope(z) * scale).astype(BF16)
    kc_ref[...] = rope(_dot(hn, w_ref[:, _C_KC:_C_KC + LANES])).astype(BF16)
    vc_ref[...] = _dot(hn, w_ref[:, _C_VC:_C_VC + LANES]).astype(BF16)
    ks_ref[...] = rope(_dot(hn, w_ref[:, _C_KS:_C_KS + LANES])).astype(BF16)
    kw_ref[...] = rope(_dot(hn, w_ref[:, _C_KW:_C_KW + LANES])).astype(BF16)
    sm_ref[...] = _dot(hn, w_ref[:, _C_SM:_C_SM + LANES])
    for c0 in range(0, 2 * ML_WIDTH, 512):
        qkb_ref[:, c0:c0 + 512] = _dot(hn, w_ref[:, _C_QKB + c0:_C_QKB + c0 + 512]).astype(BF16)
    vb_ref[...] = _dot(hn, w_ref[:, _C_VB:_C_VB + ML_WIDTH]).astype(BF16)
    ob_ref[...] = _dot(hn, w_ref[:, _C_OB:_C_OB + ML_WIDTH]).astype(BF16)
    for c0 in range(0, 2 * D_MODEL, 512):
        mg_ref[:, c0:c0 + 512] = _dot(hn, w_ref[:, _C_MG + c0:_C_MG + c0 + 512]).astype(BF16)
    zt = _dot_nt(wt_ref[...], hn)
    for i in range(TM // KC):
        vst_ref[i] = zt[0:LANES, i * KC:(i + 1) * KC].astype(BF16)
        vwt_ref[i] = zt[LANES:2 * LANES, i * KC:(i + 1) * KC].astype(BF16)
    smt_ref[...] = zt[2 * LANES:3 * LANES, :]


def _inproj(x2, g_mix, wcat, wtr, rc, rp, rm):
    t = x2.shape[0]
    row = lambda w: pl.BlockSpec((TM, w), lambda i: (i, 0))
    out_shape = [
        jax.ShapeDtypeStruct((t, NSA_HEADS * LANES), BF16),
        jax.ShapeDtypeStruct((t, LANES), BF16),
        jax.ShapeDtypeStruct((t, LANES), BF16),
        jax.ShapeDtypeStruct((t, LANES), BF16),
        jax.ShapeDtypeStruct((t, LANES), BF16),
        jax.ShapeDtypeStruct((t // KC, LANES, KC), BF16),
        jax.ShapeDtypeStruct((t // KC, LANES, KC), BF16),
        jax.ShapeDtypeStruct((t, LANES), F32),
        jax.ShapeDtypeStruct((LANES, t), F32),
        jax.ShapeDtypeStruct((t, 2 * ML_WIDTH), BF16),
        jax.ShapeDtypeStruct((t, ML_WIDTH), BF16),
        jax.ShapeDtypeStruct((t, ML_WIDTH), BF16),
        jax.ShapeDtypeStruct((t, 2 * D_MODEL), BF16),
    ]
    chunk3 = pl.BlockSpec((TM // KC, LANES, KC), lambda i: (i, 0, 0))
    out_specs = [row(NSA_HEADS * LANES), row(LANES), row(LANES), row(LANES), row(LANES), chunk3, chunk3,
                 row(LANES), pl.BlockSpec((LANES, TM), lambda i: (0, i)),
                 row(2 * ML_WIDTH), row(ML_WIDTH), row(ML_WIDTH), row(2 * D_MODEL)]
    return pl.pallas_call(
        _inproj_kernel,
        out_shape=out_shape,
        grid=(t // TM,),
        in_specs=[row(D_MODEL), _const_spec((1, D_MODEL)), _const_spec((D_MODEL, _C_END)),
                  _const_spec((3 * LANES, D_MODEL)), row(LANES), row(LANES), row(LANES)],
        out_specs=out_specs,
        compiler_params=pltpu.CompilerParams(dimension_semantics=("arbitrary",), vmem_limit_bytes=VMEM_LIMIT),
        name="inproj",
    )(x2, g_mix, wcat, wtr, rc, rp, rm)


def _gelu_tanh(x):
    return 0.5 * x * (1.0 + jnp.tanh(math.sqrt(2.0 / math.pi) * (x + 0.044715 * x * x * x)))


def _compress_kernel(rk_ref, rv_ref, wka_ref, wkb_ref, wva_ref, wvb_ref, pek_ref, pev_ref,
                     w1k_ref, w1v_ref, w2k_ref, w2vt_ref, kc_ref, vct_ref):
    nrow = rk_ref.shape[0]

    def hidden(r_ref, wa_ref, wb_ref, pe_ref, w1_ref):
        r = r_ref[...]
        ha = _dot(r, wa_ref[...])
        hb = _dot(r, wb_ref[...])
        hb = pltpu.roll(hb, nrow - 1, 0)
        c = _dot(pe_ref[...], w1_ref[...])[0:1, :]
        return [_gelu_tanh(ha[:, g * CMP_HIDDEN:(g + 1) * CMP_HIDDEN] + hb[:, g * CMP_HIDDEN:(g + 1) * CMP_HIDDEN] + c).astype(BF16)
                for g in range(NSA_KV)]

    ak = hidden(rk_ref, wka_ref, wkb_ref, pek_ref, w1k_ref)
    kc_ref[...] = (_dot(ak[0], w2k_ref[0]) + _dot(ak[1], w2k_ref[1])).astype(BF16)
    av = hidden(rv_ref, wva_ref, wvb_ref, pev_ref, w1v_ref)
    for g in range(NSA_KV):
        vct_ref[g * NSA_HD:(g + 1) * NSA_HD, :] = _dot_nt(w2vt_ref[...], av[g]).astype(BF16)


def _compress(rk, rv, wka, wkb, wva, wvb, pek, pev, w1k, w1v, w2k, w2vt):
    b, nrow, width = rk.shape
    blk = pl.BlockSpec((None, nrow, width), lambda i: (i, 0, 0))
    return pl.pallas_call(
        _compress_kernel,
        out_shape=[jax.ShapeDtypeStruct((b, nrow, LANES), BF16),
                   jax.ShapeDtypeStruct((b, LANES, nrow), BF16)],
        grid=(b,),
        in_specs=[blk, blk] + [_const_spec(a.shape) for a in (wka, wkb, wva, wvb, pek, pev, w1k, w1v, w2k, w2vt)],
        out_specs=[pl.BlockSpec((None, nrow, LANES), lambda i: (i, 0, 0)),
                   pl.BlockSpec((None, LANES, nrow), lambda i: (i, 0, 0))],
        compiler_params=pltpu.CompilerParams(dimension_semantics=("arbitrary",), vmem_limit_bytes=VMEM_LIMIT),
        name="compress",
    )(rk, rv, wka, wkb, wva, wvb, pek, pev, w1k, w1v, w2k, w2vt)


def _nsa_kernel(q_ref, kc_ref, vct_ref, ks_ref, kw_ref, vst_ref, vwt_ref, smt_ref, o_ref, bias_scr, sx_scr, sy_scr):
    g = pl.program_id(1)
    c = pl.program_id(2)
    t0 = c * QT
    ncmp = kc_ref.shape[0]
    nsel = bias_scr.shape[0]
    nw = WINDOW // KC + 1
    width = NSA_HPG * QT

    qs = jnp.concatenate([q_ref[:, h * LANES:(h + 1) * LANES] for h in range(NSA_HPG)], axis=0)
    u_row = lax.broadcasted_iota(jnp.int32, (1, width), 1) % QT
    t_row = t0 + u_row
    r_kc = lax.broadcasted_iota(jnp.int32, (KC, 1), 0)

    n_grp = ks_ref.shape[0] // SEL_GROUP
    n_full = lax.shift_right_logical(t0, int(math.log2(SEL_GROUP)))

    def qk_group(j):
        return _dot_nt(ks_ref[pl.ds(pl.multiple_of(j * SEL_GROUP, SEL_GROUP), SEL_GROUP), :], qs)

    sc = _dot_nt(kc_ref[...], qs)

    w_slabs, w_chunks = [], []
    for i in range(nw):
        jj = c - (nw - 1) + i
        jc = jnp.maximum(jj, 0)
        si = _dot_nt(kw_ref[pl.ds(pl.multiple_of(jc * KC, KC), KC), :], qs)
        if i == 0:
            keep = (r_kc > u_row) & (jj >= 0)
        elif i == nw - 1:
            keep = r_kc <= u_row
        else:
            keep = jj >= 0
        w_slabs.append(jnp.where(keep, si, NEG))
        w_chunks.append(jc)

    n_col = lax.broadcasted_iota(jnp.int32, (ncmp, 1), 0)
    cmask = (CMP_STRIDE * n_col + (CMP_LEN - 1) <= t_row) & (n_col < ncmp - 1)
    s = jnp.where(cmask, sc, NEG)
    m = jnp.max(s, axis=0, keepdims=True)
    e = jnp.exp2(s - m)
    anyv = (t_row >= CMP_LEN - 1).astype(F32)
    p = e * (anyv / jnp.sum(e, axis=0, keepdims=True))
    o_cmp = _dot(vct_ref[...], p.astype(BF16))

    psum = p[:, 0:QT]
    for h in range(1, NSA_HPG):
        psum = psum + p[:, h * QT:(h + 1) * QT]
    s_col = lax.broadcasted_iota(jnp.int32, (nsel, 1), 0)
    n_lane = lax.broadcasted_iota(jnp.int32, (1, ncmp), 1)
    ov = ((CMP_STRIDE * n_lane < SEL_BLOCK * (s_col + 1)) & (CMP_STRIDE * n_lane + (CMP_LEN - 1) >= SEL_BLOCK * s_col)
          ).astype(BF16)
    hi = psum.astype(BF16)
    r1 = psum - hi.astype(F32)
    mid = r1.astype(BF16)
    lo = (r1 - mid.astype(F32)).astype(BF16)
    imp = _dot(ov, hi) + _dot(ov, mid) + _dot(ov, lo)

    s_diag = qk_group(n_full)
    sx_scr[...] = qk_group(0)

    mxw = w_slabs[0]
    for sl in w_slabs[1:]:
        mxw = jnp.maximum(mxw, sl)
    mw = jnp.max(mxw, axis=0, keepdims=True)
    pws = None
    acc_w = jnp.zeros((NSA_HD, width), F32)
    for sl, jc in zip(w_slabs, w_chunks):
        pw = jnp.exp2(sl - mw)
        pws = pw if pws is None else pws + pw
        acc_w = acc_w + _dot(vwt_ref[jc], pw.astype(BF16))
    o_win = acc_w / jnp.sum(pws, axis=0, keepdims=True)

    t1 = t0 + lax.broadcasted_iota(jnp.int32, (1, QT), 1)
    cur = lax.shift_right_logical(t1, 6)
    forced = (s_col == 0) | (s_col == cur) | (s_col == cur - 1)
    valid = SEL_BLOCK * s_col <= t1
    val = jnp.where(valid, jnp.where(forced, imp + SEL_FORCE, imp), NEG)
    sub = 8
    r_sub = lax.broadcasted_iota(jnp.int32, (sub, 1), 0)
    blocks = [val[r * sub:(r + 1) * sub, :] for r in range(nsel // sub)]
    ranks = [jnp.zeros((sub, QT), F32) for _ in blocks]
    for i in range(nsel):
        vi = val[i:i + 1, :]
        for r, blk in enumerate(blocks):
            if i < r * sub:
                beats = vi >= blk
            elif i >= (r + 1) * sub:
                beats = vi > blk
            else:
                beats = (vi > blk) | ((vi == blk) & (r_sub > i - r * sub))
            ranks[r] = ranks[r] + jnp.where(beats, 1.0, 0.0)
    bias = jnp.where(jnp.concatenate(ranks, axis=0) < SEL_TOPK, 0.0, NEG).astype(F32)
    bias_scr[...] = jnp.concatenate([bias] * NSA_HPG, axis=1)

    r_blk = lax.broadcasted_iota(jnp.int32, (SEL_BLOCK, 1), 0)
    blk_per_grp = SEL_GROUP // SEL_BLOCK
    chunk_per_grp = SEL_GROUP // KC
    blk_per_chunk = KC // SEL_BLOCK

    def sel_mask(sj, j, causal=False, live=None):
        slabs = []
        for i in range(blk_per_grp):
            brow = bias_scr[pl.ds(blk_per_grp * j + i, 1), :]
            if live is not None:
                brow = jnp.where(live, brow, NEG)
            sl = sj[i * SEL_BLOCK:(i + 1) * SEL_BLOCK, :] + brow
            if causal:
                sl = jnp.where(j * SEL_GROUP + i * SEL_BLOCK + r_blk <= t_row, sl, NEG)
            slabs.append(sl)
        return slabs

    def sel_update(j, slabs, carry):
        m_o, l_o, acc = carry
        mx = slabs[0]
        for sl in slabs[1:]:
            mx = jnp.maximum(mx, sl)
        m_n = jnp.maximum(m_o, jnp.max(mx, axis=0, keepdims=True))
        a = jnp.exp2(m_o - m_n)
        acc = a * acc
        psum = None
        for ci in range(chunk_per_grp):
            pj = jnp.exp2(jnp.concatenate(slabs[blk_per_chunk * ci:blk_per_chunk * (ci + 1)], axis=0) - m_n)
            psum = pj if psum is None else psum + pj
            acc = acc + _dot(vst_ref[chunk_per_grp * j + ci], pj.astype(BF16))
        l_n = a * l_o + jnp.sum(psum, axis=0, keepdims=True)
        return m_n, l_n, acc

    empty = (jnp.full((1, width), NEG, F32), jnp.zeros((1, width), F32), jnp.zeros((NSA_HD, width), F32))
    seeded = sel_update(n_full, sel_mask(s_diag, n_full, causal=True), empty)

    def pair_body(jp, carry):
        ja, jb = 2 * jp, 2 * jp + 1
        sy_scr[...] = qk_group(jb)
        carry = sel_update(ja, sel_mask(sx_scr, ja), carry)
        sx_scr[...] = qk_group(jnp.minimum(ja + 2, n_grp - 1))
        return sel_update(jb, sel_mask(sy_scr, jb, live=jb < n_full), carry)

    _, l_s, acc_s = lax.fori_loop(0, lax.shift_right_logical(n_full + 1, 1), pair_body, seeded)
    o_sel = acc_s / l_s

    def gate_row(br):
        rows = [smt_ref[pl.ds(SM_GATE + 3 * (NSA_HPG * g + h) + br, 1), :] for h in range(NSA_HPG)]
        return _sigmoid(jnp.concatenate(rows, axis=1))

    o_t = gate_row(0) * o_cmp + gate_row(1) * o_sel + gate_row(2) * o_win
    for pr in range(NSA_HPG // 2):
        xp = jnp.concatenate([o_t[:, (2 * pr) * QT:(2 * pr + 1) * QT], o_t[:, (2 * pr + 1) * QT:(2 * pr + 2) * QT]], axis=0)
        o_ref[:, pr * LANES:(pr + 1) * LANES] = xp.T.astype(BF16)


def _nsa(qpad, kcb, vct, ks, kw, vst, vwt, smt, b, s):
    nq = s // QT
    ncmp = kcb.shape[1]
    return pl.pallas_call(
        _nsa_kernel,
        out_shape=jax.ShapeDtypeStruct((b * s, NSA_HEADS * NSA_HD), BF16),
        grid=(b, NSA_KV, nq),
        in_specs=[
            pl.BlockSpec((QT, NSA_HPG * LANES), lambda bi, g, c: (bi * nq + c, g)),
            pl.BlockSpec((None, ncmp, LANES), lambda bi, g, c: (bi, 0, 0)),
            pl.BlockSpec((None, NSA_HD, ncmp), lambda bi, g, c: (bi, g, 0)),
            pl.BlockSpec((s, LANES), lambda bi, g, c: (bi, 0)),
            pl.BlockSpec((s, LANES), lambda bi, g, c: (bi, 0)),
            pl.BlockSpec((s // KC, NSA_HD, KC), lambda bi, g, c: (bi, g, 0)),
            pl.BlockSpec((s // KC, NSA_HD, KC), lambda bi, g, c: (bi, g, 0)),
            pl.BlockSpec((LANES, QT), lambda bi, g, c: (0, bi * nq + c)),
        ],
        out_specs=pl.BlockSpec((QT, NSA_HPG * NSA_HD), lambda bi, g, c: (bi * nq + c, g)),
        scratch_shapes=[pltpu.VMEM((s // SEL_BLOCK, NSA_HPG * QT), F32),
                        pltpu.VMEM((SEL_GROUP, NSA_HPG * QT), F32),
                        pltpu.VMEM((SEL_GROUP, NSA_HPG * QT), F32)],
        compiler_params=pltpu.CompilerParams(dimension_semantics=("arbitrary", "arbitrary", "arbitrary"),
                                             vmem_limit_bytes=VMEM_LIMIT),
        name="nsa",
    )(qpad, kcb, vct, ks, kw, vst, vwt, smt)


def _log_sigmoid(x):
    return jnp.minimum(x, 0.0) - jnp.log(1.0 + jnp.exp(-jnp.abs(x)))


def _mlstm_kernel(qk_ref, v_ref, og_ref, sm_ref, smt_ref, wc_ref, bc_ref, bifc_ref, bifr_ref, ghn_ref,
                  y_ref, ext_scr, ct_scr, n_scr, m_scr):
    lc = ML_CHUNK

    @pl.when(pl.program_id(1) == 0)
    def _():
        ext_scr[0:8, :] = jnp.zeros((8, 2 * ML_WIDTH), F32)
        ct_scr[...] = jnp.zeros_like(ct_scr)
        n_scr[...] = jnp.zeros_like(n_scr)
        m_scr[...] = jnp.zeros_like(m_scr)

    ext_scr[8:8 + ML_BLOCK, :] = qk_ref[...].astype(F32)
    y = bc_ref[...]
    for j in range(CONV_W):
        y = y + wc_ref[j:j + 1, :] * ext_scr[pl.ds(8 - (CONV_W - 1) + j, ML_BLOCK), :]
    ext_scr[0:8, :] = ext_scr[ML_BLOCK:ML_BLOCK + 8, :]
    qkc = y * _sigmoid(y)
    q_all = qkc[:, 0:ML_WIDTH].astype(BF16)
    k_all = (qkc[:, ML_WIDTH:2 * ML_WIDTH] * (ML_HD ** -0.5)).astype(BF16)

    ifc = sm_ref[...] + bifc_ref[...]
    ifr = smt_ref[...] + bifr_ref[...]
    lfc = _log_sigmoid(ifc)
    lfr = _log_sigmoid(ifr)
    rr = lax.broadcasted_iota(jnp.int32, (lc, lc), 0)
    cc = lax.broadcasted_iota(jnp.int32, (lc, lc), 1)
    causal = rr >= cc
    tri_l = causal.astype(F32)
    tri_u = (rr <= cc).astype(F32)

    for ci in range(ML_BLOCK // lc):
        lo, hi = ci * lc, (ci + 1) * lc
        bc_all = _dot_hi(tri_l, lfc[lo:hi, :])
        br_all = _dot_hi(lfr[:, lo:hi], tri_u)
        for h in range(ML_HEADS):
            hs = slice(h * ML_HD, (h + 1) * ML_HD)
            bcol = bc_all[:, SM_F + h:SM_F + h + 1]
            brow = br_all[SM_F + h:SM_F + h + 1, :]
            icol = ifc[lo:hi, SM_I + h:SM_I + h + 1]
            irow = ifr[SM_I + h:SM_I + h + 1, lo:hi]
            mprev = m_scr[h][:, 0:1]
            qh = q_all[lo:hi, hs]
            kh = k_all[lo:hi, hs]
            vh = v_ref[lo:hi, hs]
            dmat = jnp.where(causal, bcol - brow + irow, NEG)
            inter = bcol + mprev
            mt = jnp.maximum(jnp.max(dmat, axis=-1, keepdims=True), inter)
            a = jnp.exp(dmat - mt) * _dot_nt(qh, kh)
            dec = jnp.exp(inter - mt)
            ct = ct_scr[h]
            nrow = n_scr[h]
            num = _dot(a.astype(BF16), vh) + dec * _dot(qh, ct.astype(BF16))
            den = jnp.sum(a, axis=-1, keepdims=True) + dec * jnp.sum(qh.astype(F32) * nrow, axis=-1, keepdims=True)
            hh = num / jnp.maximum(jnp.abs(den), jnp.exp(-mt))
            blast = bcol[lc - 1:lc, :]
            grow = blast - brow + irow
            mnew = jnp.maximum(blast + mprev, jnp.max(grow, axis=-1, keepdims=True))
            wprev = jnp.exp(blast + mprev - mnew)
            kwt = kh.astype(F32) * jnp.exp(blast - bcol + icol - mnew)
            ct_scr[h] = wprev * ct + lax.dot_general(kwt.astype(BF16), vh, _TN, preferred_element_type=F32)
            n_scr[h] = wprev * nrow + jnp.sum(kwt, axis=0, keepdims=True)
            m_scr[h] = jnp.broadcast_to(mnew, (1, LANES))
            hm = hh * _sigmoid(og_ref[lo:hi, hs].astype(F32))
            y_ref[lo:hi, hs] = _rms(hm, ghn_ref[:, hs]).astype(BF16)


def _mlstm(qkb, vb, ob, sm, smt, wconv, bconv, bifc, bifr, ghn, b, s):
    nb = s // ML_BLOCK
    row = lambda w: pl.BlockSpec((ML_BLOCK, w), lambda bi, j: (bi * nb + j, 0))
    return pl.pallas_call(
        _mlstm_kernel,
        out_shape=jax.ShapeDtypeStruct((b * s, ML_WIDTH), BF16),
        grid=(b, nb),
        in_specs=[row(2 * ML_WIDTH), row(ML_WIDTH), row(ML_WIDTH), row(LANES),
                  pl.BlockSpec((LANES, ML_BLOCK), lambda bi, j: (0, bi * nb + j)),
                  _const_spec(wconv.shape), _const_spec(bconv.shape), _const_spec(bifc.shape),
                  _const_spec(bifr.shape), _const_spec(ghn.shape)],
        out_specs=row(ML_WIDTH),
        scratch_shapes=[pltpu.VMEM((ML_BLOCK + 8, 2 * ML_WIDTH), F32),
                        pltpu.VMEM((ML_HEADS, ML_HD, ML_HD), F32),
                        pltpu.VMEM((ML_HEADS, 1, ML_HD), F32),
                        pltpu.VMEM((ML_HEADS, 1, LANES), F32)],
        compiler_params=pltpu.CompilerParams(dimension_semantics=("arbitrary", "arbitrary"),
                                             vmem_limit_bytes=VMEM_LIMIT),
        name="mlstm",
    )(qkb, vb, ob, sm, smt, wconv, bconv, bifc, bifr, ghn)


def _merge_kernel(ya_ref, yb_ref, mg_ref, x_ref, wpa_ref, wpb_ref, wout_ref, gffn_ref, wr_ref, br_ref,
                  x1_ref, h2_ref, gate_ref):
    pa = _dot(ya_ref[...], wpa_ref[...])
    pb = _dot(yb_ref[...], wpb_ref[...])
    ga = _sigmoid(mg_ref[:, 0:D_MODEL].astype(F32))
    gb = _sigmoid(mg_ref[:, D_MODEL:2 * D_MODEL].astype(F32))
    mixed = (ga * pa + gb * pb).astype(BF16)
    x1 = x_ref[...] + _dot(mixed, wout_ref[...])
    x1_ref[...] = x1
    h2 = _rms(x1, gffn_ref[...])
    h2_ref[...] = h2.astype(BF16)

    logit = _dot_hi(h2, wr_ref[...]) + br_ref[...]
    lane = lax.broadcasted_iota(jnp.int32, logit.shape, 1)
    big = jnp.int32(LANES)
    gmask = (lane >= N_EXPERTS) & (lane < N_EXPERTS + N_GROUPS)
    gl = jnp.where(gmask, logit, NEG)
    gmax = jnp.max(gl, axis=-1, keepdims=True)
    gidx = jnp.min(jnp.where(gmask & (gl == gmax), lane, big), axis=-1, keepdims=True) - N_EXPERTS
    pg_sel = 1.0 / jnp.sum(jnp.where(gmask, jnp.exp(gl - gmax), 0.0), axis=-1, keepdims=True)
    emask = (lane < N_EXPERTS) & (lax.shift_right_logical(lane, 2) == gidx)
    el = jnp.where(emask, logit, NEG)
    e1 = jnp.max(el, axis=-1, keepdims=True)
    i1 = jnp.min(jnp.where(emask & (el == e1), lane, big), axis=-1, keepdims=True)
    emask2 = emask & (lane != i1)
    el2 = jnp.where(emask2, logit, NEG)
    e2 = jnp.max(el2, axis=-1, keepdims=True)
    i2 = jnp.min(jnp.where(emask2 & (el2 == e2), lane, big), axis=-1, keepdims=True)
    x21 = jnp.exp(e2 - e1)
    w1 = pg_sel / (1.0 + x21)
    w2 = pg_sel * x21 / (1.0 + x21)
    gate_ref[...] = jnp.where(lane == i1, w1, 0.0) + jnp.where(lane == i2, w2, 0.0)


def _merge(ya, yb, mg, x2, wpa, wpb, wout, gffn, wr, br):
    t = x2.shape[0]
    row = lambda w: pl.BlockSpec((TM, w), lambda i: (i, 0))
    return pl.pallas_call(
        _merge_kernel,
        out_shape=[jax.ShapeDtypeStruct((t, D_MODEL), F32),
                   jax.ShapeDtypeStruct((t, D_MODEL), BF16),
                   jax.ShapeDtypeStruct((t, LANES), F32)],
        grid=(t // TM,),
        in_specs=[row(NSA_HEADS * NSA_HD), row(ML_WIDTH), row(2 * D_MODEL), row(D_MODEL)]
                 + [_const_spec(a.shape) for a in (wpa, wpb, wout, gffn, wr, br)],
        out_specs=[row(D_MODEL), row(D_MODEL), row(LANES)],
        compiler_params=pltpu.CompilerParams(dimension_semantics=("arbitrary",), vmem_limit_bytes=VMEM_LIMIT),
        name="merge",
    )(ya, yb, mg, x2, wpa, wpb, wout, gffn, wr, br)


def _moe_kernel(h2_ref, gate_ref, x1_ref, p_ref, w13_ref, w2_ref, gple_ref, wpg_ref, wpp_ref, gfin_ref,
                o_ref, acc_scr):
    h2 = h2_ref[...]
    gate = gate_ref[...]
    acc_scr[...] = x1_ref[...]
    for e in range(N_EXPERTS):
        a = _dot(h2, w13_ref[e])
        gt = a[:, 0:D_EXPERT]
        act = gt * _sigmoid(gt) * a[:, D_EXPERT:2 * D_EXPERT] * gate[:, e:e + 1]
        acc_scr[...] += _dot(act.astype(BF16), w2_ref[e])
    x2 = acc_scr[...]
    h3 = _rms(x2, gple_ref[...]).astype(BF16)
    x3 = x2 + _sigmoid(_dot(h3, wpg_ref[...])) * _dot(p_ref[...].astype(BF16), wpp_ref[...])
    o_ref[...] = _rms(x3, gfin_ref[...])


def _moe(h2, gate, x1, p2, w13, w2, gple, wpg, wpp, gfin):
    t = h2.shape[0]
    row = lambda w: pl.BlockSpec((TM, w), lambda i: (i, 0))
    return pl.pallas_call(
        _moe_kernel,
        out_shape=jax.ShapeDtypeStruct((t, D_MODEL), F32),
        grid=(t // TM,),
        in_specs=[row(D_MODEL), row(LANES), row(D_MODEL), row(PLE_DIM)]
                 + [_const_spec(a.shape) for a in (w13, w2, gple, wpg, wpp, gfin)],
        out_specs=row(D_MODEL),
        scratch_shapes=[pltpu.VMEM((TM, D_MODEL), F32)],
        compiler_params=pltpu.CompilerParams(dimension_semantics=("arbitrary",), vmem_limit_bytes=VMEM_LIMIT),
        name="moe",
    )(h2, gate, x1, p2, w13, w2, gple, wpg, wpp, gfin)


def _pack_inproj_weights(w):
    d = w.shape[0]
    qw = NSA_HEADS * NSA_HD
    kvw = NSA_KV * NSA_HD
    o = 0
    wq = w[:, o:o + qw]; o += qw
    wkc = w[:, o:o + kvw]; o += kvw
    wvc = w[:, o:o + kvw]; o += kvw
    wks = w[:, o:o + kvw]; o += kvw
    wvs = w[:, o:o + kvw]; o += kvw
    wkw = w[:, o:o + kvw]; o += kvw
    wvw = w[:, o:o + kvw]; o += kvw
    wga = w[:, o:o + 3 * NSA_HEADS]; o += 3 * NSA_HEADS
    wqkb = w[:, o:o + 2 * ML_WIDTH]; o += 2 * ML_WIDTH
    wvb = w[:, o:o + ML_WIDTH]; o += ML_WIDTH
    wob = w[:, o:o + ML_WIDTH]; o += ML_WIDTH
    wif = w[:, o:o + 2 * ML_HEADS]; o += 2 * ML_HEADS
    wmg = w[:, o:o + 2 * D_MODEL]
    zero64 = jnp.zeros((d, NSA_HD), w.dtype)
    qcols = []
    for h in range(NSA_HEADS):
        wh = wq[:, h * NSA_HD:(h + 1) * NSA_HD]
        qcols += [wh, zero64] if h // NSA_HPG == 0 else [zero64, wh]
    wsm = jnp.concatenate([wga, wif, jnp.zeros((d, LANES - 3 * NSA_HEADS - 2 * ML_HEADS), w.dtype)], axis=1)
    wcat = jnp.concatenate(qcols + [wkc, wvc, wks, wkw, wsm, wqkb, wvb, wob, wmg], axis=1).astype(BF16)
    wtr = jnp.concatenate([wvs, wvw, wsm], axis=1).T.astype(BF16)
    return wcat, wtr


def _rope_lane_tables(positions):
    inv = ROPE_THETA ** (-jnp.arange(0, ROPE_DIM, 2, dtype=F32) / ROPE_DIM)
    ang = positions.astype(F32).reshape(-1, 1) * inv[None, :]
    cos, sin = jnp.cos(ang), jnp.sin(ang)
    half = ROPE_DIM // 2
    d = jnp.arange(LANES) % NSA_HD
    cos_l, sin_l = cos[:, d % half], sin[:, d % half]
    rc = jnp.where(d < ROPE_DIM, cos_l, 1.0)
    rp = jnp.where((d >= half) & (d < ROPE_DIM), sin_l, 0.0)
    rm = jnp.where(d < half, -sin_l, 0.0)
    return rc, rp, rm


def _pack_compress_weights(w1, w2, pe):
    half = CMP_LEN // 2
    w1r = w1.reshape(2, half, NSA_HD, CMP_HIDDEN)
    outs = []
    for part in range(2):
        wp = jnp.zeros((half, NSA_KV, NSA_HD, NSA_KV, CMP_HIDDEN), w1.dtype)
        for g in range(NSA_KV):
            wp = wp.at[:, g, :, g, :].set(w1r[part])
        outs.append(wp.reshape(half * NSA_KV * NSA_HD, NSA_KV * CMP_HIDDEN).astype(BF16))
    pe8 = jnp.broadcast_to(pe.reshape(1, CMP_LEN * NSA_HD), (8, CMP_LEN * NSA_HD)).astype(BF16)
    return outs[0], outs[1], pe8, w1.astype(BF16)


def kernel(x, p, positions, g_mix, w_in, b_if, w_ck1, w_ck2, pe_ck, w_cv1, w_cv2, pe_cv, w_conv, b_conv, g_hn, w_pa, w_pb, w_out, g_ffn, w_rg, b_rg, w_re, b_re, w_e13, w_e2, g_ple, w_pg, w_pp, g_final):
    b, s, d = x.shape
    t = b * s
    rc, rp, rm = _rope_lane_tables(positions)
    assert w_in.shape[0] == 1, "the final norm is fused into the layer's last kernel: single-layer problem only"
    for i in range(w_in.shape[0]):
        x2 = x.reshape(t, d)
        wcat, wtr = _pack_inproj_weights(w_in[i])
        (qpad, kc_tok, vc_tok, ks, kw, vst, vwt, sm, smt, qkb, vb, ob, mg) = _inproj(
            x2, g_mix[i].reshape(1, d), wcat, wtr, rc, rp, rm)
        wka, wkb, pek, w1k = _pack_compress_weights(w_ck1[i], w_ck2[i], pe_ck[i])
        wva, wvb, pev, w1v = _pack_compress_weights(w_cv1[i], w_cv2[i], pe_cv[i])
        zpad = jnp.zeros((CMP_HIDDEN, NSA_HD), F32)
        w2k = jnp.stack([jnp.concatenate([w_ck2[i], zpad], axis=1),
                         jnp.concatenate([zpad, w_ck2[i]], axis=1)]).astype(BF16)
        w2vt = w_cv2[i].T.astype(BF16)
        nrow = s // CMP_STRIDE
        rk = kc_tok.reshape(b, nrow, CMP_STRIDE * LANES)
        rv = vc_tok.reshape(b, nrow, CMP_STRIDE * LANES)
        kcb, vct = _compress(rk, rv, wka, wkb, wva, wvb, pek, pev, w1k, w1v, w2k, w2vt)
        ya = _nsa(qpad, kcb, vct, ks, kw, vst, vwt, smt, b, s)
        bif = b_if[i].astype(F32)
        bifc = jnp.zeros((1, LANES), F32).at[0, SM_I:SM_I + 2 * ML_HEADS].set(bif)
        bifr = bifc.reshape(LANES, 1)
        yb = _mlstm(qkb, vb, ob, sm, smt, w_conv[i], b_conv[i].reshape(1, -1), bifc, bifr,
                    g_hn[i].reshape(1, -1), b, s)
        wr = jnp.concatenate([w_re[i], w_rg[i], jnp.zeros((d, LANES - N_EXPERTS - N_GROUPS), F32)], axis=1)
        br = jnp.concatenate([b_re[i], b_rg[i], jnp.zeros((LANES - N_EXPERTS - N_GROUPS,), F32)]).reshape(1, LANES)
        x1, h2, gate = _merge(ya, yb, mg, x2, w_pa[i].astype(BF16), w_pb[i].astype(BF16), w_out[i].astype(BF16),
                              g_ffn[i].reshape(1, d), wr, br)
        gfin = g_final.reshape(1, d)
        out = _moe(h2, gate, x1, p[i].reshape(t, PLE_DIM), w_e13[i].astype(BF16), w_e2[i].astype(BF16),
                   g_ple[i].reshape(1, d), w_pg[i].astype(BF16), w_pp[i].astype(BF16), gfin)
        x = out.reshape(b, s, d)
    return x
```

```python
import functools
import math

import jax
import jax.numpy as jnp
from jax import lax
from jax.experimental import pallas as pl
from jax.experimental.pallas import tpu as pltpu

F32 = jnp.float32
BF16 = jnp.bfloat16

EPS = 1e-6
NEG = -1e30

D_MODEL = 1024
PLE_DIM = 256
NSA_HEADS = 8
NSA_KV = 2
NSA_HPG = NSA_HEADS // NSA_KV
NSA_HD = 64
CMP_LEN = 32
CMP_STRIDE = 16
CMP_HIDDEN = 256
SEL_BLOCK = 64
SEL_TOPK = 16
SEL_FORCE = 1000.0
WINDOW = 512
ROPE_THETA = 500000.0
ROPE_DIM = NSA_HD // 4
ML_HEADS = 4
ML_HD = 128
ML_WIDTH = ML_HEADS * ML_HD
CONV_W = 4
N_GROUPS = 4
EXP_PER_GROUP = 4
N_EXPERTS = N_GROUPS * EXP_PER_GROUP
D_EXPERT = 256

LANES = 128
QT = 128
KC = 128
SEL_GROUP = 512
ML_CHUNK = 128
ML_BLOCK = 256
TM = 256
VMEM_LIMIT = 56 * 1024 * 1024

_NT = (((1,), (1,)), ((), ()))
_TN = (((0,), (0,)), ((), ()))

SM_GATE = 0
SM_I = 3 * NSA_HEADS
SM_F = SM_I + ML_HEADS


def _dot(a, b):
    return jnp.dot(a, b, preferred_element_type=F32)


def _dot_nt(a, b):
    return lax.dot_general(a, b, _NT, preferred_element_type=F32)


def _split3(x):
    hi = x.astype(BF16)
    r1 = x - hi.astype(F32)
    mid = r1.astype(BF16)
    lo = (r1 - mid.astype(F32)).astype(BF16)
    return hi, mid, lo


def _rms(x, g):
    return x * lax.rsqrt(jnp.mean(x * x, axis=-1, keepdims=True) + EPS) * g


def _sigmoid(x):
    return 1.0 / (1.0 + jnp.exp(-x))


def _const_spec(shape):
    nd = len(shape)
    return pl.BlockSpec(shape, lambda *_: (0,) * nd, pipeline_mode=pl.Buffered(1))


_C_Q = 0
_C_KC = _C_Q + NSA_HEADS * LANES
_C_VC = _C_KC + LANES
_C_KS = _C_VC + LANES
_C_KW = _C_KS + LANES
_C_SM = _C_KW + LANES
_C_QKB = _C_SM + LANES
_C_VB = _C_QKB + 2 * ML_WIDTH
_C_OB = _C_VB + ML_WIDTH
_C_MG = _C_OB + ML_WIDTH
_C_END = _C_MG + 2 * D_MODEL


def _inproj_kernel(x_ref, g_ref, w_ref, wt_ref, rc_ref, rp_ref, rm_ref,
                   q_ref, kc_ref, vc_ref, ks_ref, kw_ref, vst_ref, vwt_ref, sm_ref, smt_ref,
                   qkb_ref, vb_ref, ob_ref, mg_ref):
    hn = _rms(x_ref[...], g_ref[...]).astype(BF16)
    rc, rp, rm = rc_ref[...], rp_ref[...], rm_ref[...]

    def rope(z):
        return z * rc + pltpu.roll(z, 8, 1) * rp + pltpu.roll(z, LANES - 8, 1) * rm

    scale = NSA_HD ** -0.5 * math.log2(math.e)
    for h in range(NSA_HEADS):
        z = _dot(hn, w_ref[:, _C_Q + h * LANES:_C_Q + (h + 1) * LANES])
        q_ref[:, h * LANES:(h + 1) * LANES] = (rope(z) * scale).astype(BF16)
    kc_ref[...] = rope(_dot(hn, w_ref[:, _C_KC:_C_KC + LANES])).astype(BF16)
    vc_ref[...] = _dot(hn, w_ref[:, _C_VC:_C_VC + LANES]).astype(BF16)
    ks_ref[...] = rope(_dot(hn, w_ref[:, _C_KS:_C_KS + LANES])).astype(BF16)
    kw_ref[...] = rope(_dot(hn, w_ref[:, _C_KW:_C_KW + LANES])).astype(BF16)
    sm_ref[...] = _dot(hn, w_ref[:, _C_SM:_C_SM + LANES])
    for c0 in range(0, 2 * ML_WIDTH, 512):
        qkb_ref[:, c0:c0 + 512] = _dot(hn, w_ref[:, _C_QKB + c0:_C_QKB + c0 + 512]).astype(BF16)
    vb_ref[...] = _dot(hn, w_ref[:, _C_VB:_C_VB + ML_WIDTH]).astype(BF16)
    ob_ref[...] = _dot(hn, w_ref[:, _C_OB:_C_OB + ML_WIDTH]).astype(BF16)
    for c0 in range(0, 2 * D_MODEL, 512):
        mg_ref[:, c0:c0 + 512] = _dot(hn, w_ref[:, _C_MG + c0:_C_MG + c0 + 512]).astype(BF16)
    zt = _dot_nt(wt_ref[...], hn)
    for i in range(TM // KC):
        vst_ref[i] = zt[0:LANES, i * KC:(i + 1) * KC].astype(BF16)
        vwt_ref[i] = zt[LANES:2 * LANES, i * KC:(i + 1) * KC].astype(BF16)
    smt_ref[...] = zt[2 * LANES:3 * LANES, :]


def _inproj(x2, g_mix, wcat, wtr, rc, rp, rm):
    t = x2.shape[0]
    row = lambda w: pl.BlockSpec((TM, w), lambda i: (i, 0))
    out_shape = [
        jax.ShapeDtypeStruct((t, NSA_HEADS * LANES), BF16),
        jax.ShapeDtypeStruct((t, LANES), BF16),
        jax.ShapeDtypeStruct((t, LANES), BF16),
        jax.ShapeDtypeStruct((t, LANES), BF16),
        jax.ShapeDtypeStruct((t, LANES), BF16),
        jax.ShapeDtypeStruct((t // KC, LANES, KC), BF16),
        jax.ShapeDtypeStruct((t // KC, LANES, KC), BF16),
        jax.ShapeDtypeStruct((t, LANES), F32),
        jax.ShapeDtypeStruct((LANES, t), F32),
        jax.ShapeDtypeStruct((t, 2 * ML_WIDTH), BF16),
        jax.ShapeDtypeStruct((t, ML_WIDTH), BF16),
        jax.ShapeDtypeStruct((t, ML_WIDTH), BF16),
        jax.ShapeDtypeStruct((t, 2 * D_MODEL), BF16),
    ]
    chunk3 = pl.BlockSpec((TM // KC, LANES, KC), lambda i: (i, 0, 0))
    out_specs = [row(NSA_HEADS * LANES), row(LANES), row(LANES), row(LANES), row(LANES), chunk3, chunk3,
                 row(LANES), pl.BlockSpec((LANES, TM), lambda i: (0, i)),
                 row(2 * ML_WIDTH), row(ML_WIDTH), row(ML_WIDTH), row(2 * D_MODEL)]
    return pl.pallas_call(
        _inproj_kernel,
        out_shape=out_shape,
        grid=(t // TM,),
        in_specs=[row(D_MODEL), _const_spec((1, D_MODEL)), _const_spec((D_MODEL, _C_END)),
                  _const_spec((3 * LANES, D_MODEL)), row(LANES), row(LANES), row(LANES)],
        out_specs=out_specs,
        compiler_params=pltpu.CompilerParams(dimension_semantics=("arbitrary",), vmem_limit_bytes=VMEM_LIMIT),
        name="inproj",
    )(x2, g_mix, wcat, wtr, rc, rp, rm)


def _gelu_tanh(x):
    return 0.5 * x * (1.0 + jnp.tanh(math.sqrt(2.0 / math.pi) * (x + 0.044715 * x * x * x)))


def _compress_kernel(rk_ref, rv_ref, wka_ref, wkb_ref, wva_ref, wvb_ref, pek_ref, pev_ref,
                     w1k_ref, w1v_ref, w2k_ref, w2vt_ref, kc_ref, vct_ref):
    nrow = rk_ref.shape[0]

    def hidden(r_ref, wa_ref, wb_ref, pe_ref, w1_ref):
        r = r_ref[...]
        ha = _dot(r, wa_ref[...])
        hb = _dot(r, wb_ref[...])
        hb = pltpu.roll(hb, nrow - 1, 0)
        c = _dot(pe_ref[...], w1_ref[...])[0:1, :]
        return [_gelu_tanh(ha[:, g * CMP_HIDDEN:(g + 1) * CMP_HIDDEN] + hb[:, g * CMP_HIDDEN:(g + 1) * CMP_HIDDEN] + c).astype(BF16)
                for g in range(NSA_KV)]

    ak = hidden(rk_ref, wka_ref, wkb_ref, pek_ref, w1k_ref)
    kc_ref[...] = (_dot(ak[0], w2k_ref[0]) + _dot(ak[1], w2k_ref[1])).astype(BF16)
    av = hidden(rv_ref, wva_ref, wvb_ref, pev_ref, w1v_ref)
    for g in range(NSA_KV):
        vct_ref[g * NSA_HD:(g + 1) * NSA_HD, :] = _dot_nt(w2vt_ref[...], av[g]).astype(BF16)


def _compress(rk, rv, wka, wkb, wva, wvb, pek, pev, w1k, w1v, w2k, w2vt):
    b, nrow, width = rk.shape
    blk = pl.BlockSpec((None, nrow, width), lambda i: (i, 0, 0))
    return pl.pallas_call(
        _compress_kernel,
        out_shape=[jax.ShapeDtypeStruct((b, nrow, LANES), BF16),
                   jax.ShapeDtypeStruct((b, LANES, nrow), BF16)],
        grid=(b,),
        in_specs=[blk, blk] + [_const_spec(a.shape) for a in (wka, wkb, wva, wvb, pek, pev, w1k, w1v, w2k, w2vt)],
        out_specs=[pl.BlockSpec((None, nrow, LANES), lambda i: (i, 0, 0)),
                   pl.BlockSpec((None, LANES, nrow), lambda i: (i, 0, 0))],
        compiler_params=pltpu.CompilerParams(dimension_semantics=("arbitrary",), vmem_limit_bytes=VMEM_LIMIT),
        name="compress",
    )(rk, rv, wka, wkb, wva, wvb, pek, pev, w1k, w1v, w2k, w2vt)


def _nsa_kernel(q_ref, kc_ref, vct_ref, ks_ref, kw_ref, vst_ref, vwt_ref, smt_ref, o_ref, bias_scr, sx_scr, sy_scr):
    g = pl.program_id(1)
    c = pl.program_id(2)
    t0 = c * QT
    ncmp = kc_ref.shape[0]
    nsel = bias_scr.shape[0]
    nw = WINDOW // KC + 1
    width = NSA_HPG * QT

    qs = jnp.concatenate([q_ref[:, h * LANES:(h + 1) * LANES] for h in range(NSA_HPG)], axis=0)
    u_row = lax.broadcasted_iota(jnp.int32, (1, width), 1) % QT
    t_row = t0 + u_row
    r_kc = lax.broadcasted_iota(jnp.int32, (KC, 1), 0)

    n_grp = ks_ref.shape[0] // SEL_GROUP
    n_full = lax.shift_right_logical(t0, int(math.log2(SEL_GROUP)))

    def qk_group(j):
        return _dot_nt(ks_ref[pl.ds(pl.multiple_of(j * SEL_GROUP, SEL_GROUP), SEL_GROUP), :], qs)

    sc = _dot_nt(kc_ref[...], qs)

    w_slabs, w_chunks = [], []
    for i in range(nw):
        jj = c - (nw - 1) + i
        jc = jnp.maximum(jj, 0)
        si = _dot_nt(kw_ref[pl.ds(pl.multiple_of(jc * KC, KC), KC), :], qs)
        if i == 0:
            keep = (r_kc > u_row) & (jj >= 0)
        elif i == nw - 1:
            keep = r_kc <= u_row
        else:
            keep = jj >= 0
        w_slabs.append(jnp.where(keep, si, NEG))
        w_chunks.append(jc)

    n_col = lax.broadcasted_iota(jnp.int32, (ncmp, 1), 0)
    cmask = (CMP_STRIDE * n_col + (CMP_LEN - 1) <= t_row) & (n_col < ncmp - 1)
    s = jnp.where(cmask, sc, NEG)
    m = jnp.max(s, axis=0, keepdims=True)
    e = jnp.exp2(s - m)
    anyv = (t_row >= CMP_LEN - 1).astype(F32)
    p = e * (anyv / jnp.sum(e, axis=0, keepdims=True))
    o_cmp = _dot(vct_ref[...], p.astype(BF16))

    psum = p[:, 0:QT]
    for h in range(1, NSA_HPG):
        psum = psum + p[:, h * QT:(h + 1) * QT]
    s_col = lax.broadcasted_iota(jnp.int32, (nsel, 1), 0)
    n_lane = lax.broadcasted_iota(jnp.int32, (1, ncmp), 1)
    ov = ((CMP_STRIDE * n_lane < SEL_BLOCK * (s_col + 1)) & (CMP_STRIDE * n_lane + (CMP_LEN - 1) >= SEL_BLOCK * s_col)
          ).astype(BF16)
    imp = sum(_dot(ov, part) for part in _split3(psum))

    s_diag = qk_group(n_full)
    sx_scr[...] = qk_group(0)

    mxw = w_slabs[0]
    for sl in w_slabs[1:]:
        mxw = jnp.maximum(mxw, sl)
    mw = jnp.max(mxw, axis=0, keepdims=True)
    pws = None
    acc_w = jnp.zeros((NSA_HD, width), F32)
    for sl, jc in zip(w_slabs, w_chunks):
        pw = jnp.exp2(sl - mw)
        pws = pw if pws is None else pws + pw
        acc_w = acc_w + _dot(vwt_ref[jc], pw.astype(BF16))
    o_win = acc_w / jnp.sum(pws, axis=0, keepdims=True)

    t1 = t0 + lax.broadcasted_iota(jnp.int32, (1, QT), 1)
    cur = lax.shift_right_logical(t1, 6)
    forced = (s_col == 0) | (s_col == cur) | (s_col == cur - 1)
    valid = SEL_BLOCK * s_col <= t1
    val = jnp.where(valid, jnp.where(forced, imp + SEL_FORCE, imp), NEG)
    sub = 8
    r_sub = lax.broadcasted_iota(jnp.int32, (sub, 1), 0)
    blocks = [val[r * sub:(r + 1) * sub, :] for r in range(nsel // sub)]
    ranks = [jnp.zeros((sub, QT), F32) for _ in blocks]
    for i in range(nsel):
        vi = val[i:i + 1, :]
        for r, blk in enumerate(blocks):
            if i < r * sub:
                beats = vi >= blk
            elif i >= (r + 1) * sub:
                beats = vi > blk
            else:
                beats = (vi > blk) | ((vi == blk) & (r_sub > i - r * sub))
            ranks[r] = ranks[r] + jnp.where(beats, 1.0, 0.0)
    bias = jnp.where(jnp.concatenate(ranks, axis=0) < SEL_TOPK, 0.0, NEG).astype(F32)
    bias_scr[...] = jnp.concatenate([bias] * NSA_HPG, axis=1)

    r_blk = lax.broadcasted_iota(jnp.int32, (SEL_BLOCK, 1), 0)
    blk_per_grp = SEL_GROUP // SEL_BLOCK
    chunk_per_grp = SEL_GROUP // KC
    blk_per_chunk = KC // SEL_BLOCK

    def sel_mask(sj, j, causal=False, live=None):
        slabs = []
        for i in range(blk_per_grp):
            brow = bias_scr[pl.ds(blk_per_grp * j + i, 1), :]
            if live is not None:
                brow = jnp.where(live, brow, NEG)
            sl = sj[i * SEL_BLOCK:(i + 1) * SEL_BLOCK, :] + brow
            if causal:
                sl = jnp.where(j * SEL_GROUP + i * SEL_BLOCK + r_blk <= t_row, sl, NEG)
            slabs.append(sl)
        return slabs

    def sel_update(j, slabs, carry):
        m_o, l_o, acc = carry
        mx = slabs[0]
        for sl in slabs[1:]:
            mx = jnp.maximum(mx, sl)
        m_n = jnp.maximum(m_o, jnp.max(mx, axis=0, keepdims=True))
        a = jnp.exp2(m_o - m_n)
        acc = a * acc
        psum = None
        for ci in range(chunk_per_grp):
            pj = jnp.exp2(jnp.concatenate(slabs[blk_per_chunk * ci:blk_per_chunk * (ci + 1)], axis=0) - m_n)
            psum = pj if psum is None else psum + pj
            acc = acc + _dot(vst_ref[chunk_per_grp * j + ci], pj.astype(BF16))
        l_n = a * l_o + jnp.sum(psum, axis=0, keepdims=True)
        return m_n, l_n, acc

    empty = (jnp.full((1, width), NEG, F32), jnp.zeros((1, width), F32), jnp.zeros((NSA_HD, width), F32))
    seeded = sel_update(n_full, sel_mask(s_diag, n_full, causal=True), empty)

    def pair_body(jp, carry):
        ja, jb = 2 * jp, 2 * jp + 1
        sy_scr[...] = qk_group(jb)
        carry = sel_update(ja, sel_mask(sx_scr, ja), carry)
        sx_scr[...] = qk_group(jnp.minimum(ja + 2, n_grp - 1))
        return sel_update(jb, sel_mask(sy_scr, jb, live=jb < n_full), carry)

    _, l_s, acc_s = lax.fori_loop(0, lax.shift_right_logical(n_full + 1, 1), pair_body, seeded)
    o_sel = acc_s / l_s

    def gate_row(br):
        rows = [smt_ref[pl.ds(SM_GATE + 3 * (NSA_HPG * g + h) + br, 1), :] for h in range(NSA_HPG)]
        return _sigmoid(jnp.concatenate(rows, axis=1))

    o_t = gate_row(0) * o_cmp + gate_row(1) * o_sel + gate_row(2) * o_win
    for pr in range(NSA_HPG // 2):
        xp = jnp.concatenate([o_t[:, (2 * pr) * QT:(2 * pr + 1) * QT], o_t[:, (2 * pr + 1) * QT:(2 * pr + 2) * QT]], axis=0)
        o_ref[:, pr * LANES:(pr + 1) * LANES] = xp.T.astype(BF16)


def _nsa(qpad, kcb, vct, ks, kw, vst, vwt, smt, b, s):
    nq = s // QT
    ncmp = kcb.shape[1]
    return pl.pallas_call(
        _nsa_kernel,
        out_shape=jax.ShapeDtypeStruct((b * s, NSA_HEADS * NSA_HD), BF16),
        grid=(b, NSA_KV, nq),
        in_specs=[
            pl.BlockSpec((QT, NSA_HPG * LANES), lambda bi, g, c: (bi * nq + c, g)),
            pl.BlockSpec((None, ncmp, LANES), lambda bi, g, c: (bi, 0, 0)),
            pl.BlockSpec((None, NSA_HD, ncmp), lambda bi, g, c: (bi, g, 0)),
            pl.BlockSpec((s, LANES), lambda bi, g, c: (bi, 0)),
            pl.BlockSpec((s, LANES), lambda bi, g, c: (bi, 0)),
            pl.BlockSpec((s // KC, NSA_HD, KC), lambda bi, g, c: (bi, g, 0)),
            pl.BlockSpec((s // KC, NSA_HD, KC), lambda bi, g, c: (bi, g, 0)),
            pl.BlockSpec((LANES, QT), lambda bi, g, c: (0, bi * nq + c)),
        ],
        out_specs=pl.BlockSpec((QT, NSA_HPG * NSA_HD), lambda bi, g, c: (bi * nq + c, g)),
        scratch_shapes=[pltpu.VMEM((s // SEL_BLOCK, NSA_HPG * QT), F32),
                        pltpu.VMEM((SEL_GROUP, NSA_HPG * QT), F32),
                        pltpu.VMEM((SEL_GROUP, NSA_HPG * QT), F32)],
        compiler_params=pltpu.CompilerParams(dimension_semantics=("arbitrary", "arbitrary", "arbitrary"),
                                             vmem_limit_bytes=VMEM_LIMIT),
        name="nsa",
    )(qpad, kcb, vct, ks, kw, vst, vwt, smt)


def _log_sigmoid(x):
    return jnp.minimum(x, 0.0) - jnp.log(1.0 + jnp.exp(-jnp.abs(x)))


def _mlstm_kernel(qk_ref, v_ref, og_ref, sm_ref, smt_ref, wc_ref, bc_ref, bifc_ref, bifr_ref, ghn_ref,
                  y_ref, ext_scr, ct_scr, n_scr, m_scr):
    lc = ML_CHUNK

    @pl.when(pl.program_id(1) == 0)
    def _():
        ext_scr[0:8, :] = jnp.zeros((8, 2 * ML_WIDTH), F32)
        ct_scr[...] = jnp.zeros_like(ct_scr)
        n_scr[...] = jnp.zeros_like(n_scr)
        m_scr[...] = jnp.zeros_like(m_scr)

    ext_scr[8:8 + ML_BLOCK, :] = qk_ref[...].astype(F32)
    y = bc_ref[...]
    for j in range(CONV_W):
        y = y + wc_ref[j:j + 1, :] * ext_scr[pl.ds(8 - (CONV_W - 1) + j, ML_BLOCK), :]
    ext_scr[0:8, :] = ext_scr[ML_BLOCK:ML_BLOCK + 8, :]
    qkc = y * _sigmoid(y)
    q_all = qkc[:, 0:ML_WIDTH].astype(BF16)
    k_all = (qkc[:, ML_WIDTH:2 * ML_WIDTH] * (ML_HD ** -0.5)).astype(BF16)

    ifc = sm_ref[...] + bifc_ref[...]
    ifr = smt_ref[...] + bifr_ref[...]
    lfc = _log_sigmoid(ifc)
    lfr = _log_sigmoid(ifr)
    rr = lax.broadcasted_iota(jnp.int32, (lc, lc), 0)
    cc = lax.broadcasted_iota(jnp.int32, (lc, lc), 1)
    causal = rr >= cc
    tri_l = causal.astype(F32)
    tri_u = (rr <= cc).astype(F32)

    for ci in range(ML_BLOCK // lc):
        lo, hi = ci * lc, (ci + 1) * lc
        bc_all = jnp.dot(tri_l, lfc[lo:hi, :], preferred_element_type=F32, precision=lax.Precision.HIGHEST)
        br_all = jnp.dot(lfr[:, lo:hi], tri_u, preferred_element_type=F32, precision=lax.Precision.HIGHEST)
        for h in range(ML_HEADS):
            hs = slice(h * ML_HD, (h + 1) * ML_HD)
            bcol = bc_all[:, SM_F + h:SM_F + h + 1]
            brow = br_all[SM_F + h:SM_F + h + 1, :]
            icol = ifc[lo:hi, SM_I + h:SM_I + h + 1]
            irow = ifr[SM_I + h:SM_I + h + 1, lo:hi]
            mprev = m_scr[h][:, 0:1]
            qh = q_all[lo:hi, hs]
            kh = k_all[lo:hi, hs]
            vh = v_ref[lo:hi, hs]
            dmat = jnp.where(causal, bcol - brow + irow, NEG)
            inter = bcol + mprev
            mt = jnp.maximum(jnp.max(dmat, axis=-1, keepdims=True), inter)
            a = jnp.exp(dmat - mt) * _dot_nt(qh, kh)
            dec = jnp.exp(inter - mt)
            ct = ct_scr[h]
            nrow = n_scr[h]
            num = _dot(a.astype(BF16), vh) + dec * _dot(qh, ct.astype(BF16))
            den = jnp.sum(a, axis=-1, keepdims=True) + dec * jnp.sum(qh.astype(F32) * nrow, axis=-1, keepdims=True)
            hh = num / jnp.maximum(jnp.abs(den), jnp.exp(-mt))
            blast = bcol[lc - 1:lc, :]
            grow = blast - brow + irow
            mnew = jnp.maximum(blast + mprev, jnp.max(grow, axis=-1, keepdims=True))
            wprev = jnp.exp(blast + mprev - mnew)
            kwt = kh.astype(F32) * jnp.exp(blast - bcol + icol - mnew)
            ct_scr[h] = wprev * ct + lax.dot_general(kwt.astype(BF16), vh, _TN, preferred_element_type=F32)
            n_scr[h] = wprev * nrow + jnp.sum(kwt, axis=0, keepdims=True)
            m_scr[h] = jnp.broadcast_to(mnew, (1, LANES))
            hm = hh * _sigmoid(og_ref[lo:hi, hs].astype(F32))
            y_ref[lo:hi, hs] = _rms(hm, ghn_ref[:, hs]).astype(BF16)


def _mlstm(qkb, vb, ob, sm, smt, wconv, bconv, bifc, bifr, ghn, b, s):
    nb = s // ML_BLOCK
    row = lambda w: pl.BlockSpec((ML_BLOCK, w), lambda bi, j: (bi * nb + j, 0))
    return pl.pallas_call(
        _mlstm_kernel,
        out_shape=jax.ShapeDtypeStruct((b * s, ML_WIDTH), BF16),
        grid=(b, nb),
        in_specs=[row(2 * ML_WIDTH), row(ML_WIDTH), row(ML_WIDTH), row(LANES),
                  pl.BlockSpec((LANES, ML_BLOCK), lambda bi, j: (0, bi * nb + j)),
                  _const_spec(wconv.shape), _const_spec(bconv.shape), _const_spec(bifc.shape),
                  _const_spec(bifr.shape), _const_spec(ghn.shape)],
        out_specs=row(ML_WIDTH),
        scratch_shapes=[pltpu.VMEM((ML_BLOCK + 8, 2 * ML_WIDTH), F32),
                        pltpu.VMEM((ML_HEADS, ML_HD, ML_HD), F32),
                        pltpu.VMEM((ML_HEADS, 1, ML_HD), F32),
                        pltpu.VMEM((ML_HEADS, 1, LANES), F32)],
        compiler_params=pltpu.CompilerParams(dimension_semantics=("arbitrary", "arbitrary"),
                                             vmem_limit_bytes=VMEM_LIMIT),
        name="mlstm",
    )(qkb, vb, ob, sm, smt, wconv, bconv, bifc, bifr, ghn)


def _merge_kernel(ya_ref, yb_ref, mg_ref, x_ref, wpa_ref, wpb_ref, wout_ref, gffn_ref, wr_ref, br_ref,
                  x1_ref, h2_ref, gate_ref):
    pa = _dot(ya_ref[...], wpa_ref[...])
    pb = _dot(yb_ref[...], wpb_ref[...])
    ga = _sigmoid(mg_ref[:, 0:D_MODEL].astype(F32))
    gb = _sigmoid(mg_ref[:, D_MODEL:2 * D_MODEL].astype(F32))
    mixed = (ga * pa + gb * pb).astype(BF16)
    x1 = x_ref[...] + _dot(mixed, wout_ref[...])
    x1_ref[...] = x1
    h2 = _rms(x1, gffn_ref[...])
    h_hi = h2.astype(BF16)
    h2_ref[...] = h_hi

    h_lo = (h2 - h_hi.astype(F32)).astype(BF16)
    r_hi = _dot(h_hi, wr_ref[...])
    logit = r_hi[:, 0:LANES] + r_hi[:, LANES:2 * LANES] + _dot(h_lo, wr_ref[:, 0:LANES]) + br_ref[...]
    lane = lax.broadcasted_iota(jnp.int32, logit.shape, 1)
    big = jnp.int32(LANES)
    gmask = (lane >= N_EXPERTS) & (lane < N_EXPERTS + N_GROUPS)
    gl = jnp.where(gmask, logit, NEG)
    gmax = jnp.max(gl, axis=-1, keepdims=True)
    gidx = jnp.min(jnp.where(gmask & (gl == gmax), lane, big), axis=-1, keepdims=True) - N_EXPERTS
    pg_sel = 1.0 / jnp.sum(jnp.where(gmask, jnp.exp(gl - gmax), 0.0), axis=-1, keepdims=True)
    emask = (lane < N_EXPERTS) & (lax.shift_right_logical(lane, 2) == gidx)
    el = jnp.where(emask, logit, NEG)
    e1 = jnp.max(el, axis=-1, keepdims=True)
    i1 = jnp.min(jnp.where(emask & (el == e1), lane, big), axis=-1, keepdims=True)
    emask2 = emask & (lane != i1)
    el2 = jnp.where(emask2, logit, NEG)
    e2 = jnp.max(el2, axis=-1, keepdims=True)
    i2 = jnp.min(jnp.where(emask2 & (el2 == e2), lane, big), axis=-1, keepdims=True)
    x21 = jnp.exp(e2 - e1)
    w1 = pg_sel / (1.0 + x21)
    w2 = pg_sel * x21 / (1.0 + x21)
    gate_ref[...] = jnp.where(lane == i1, w1, 0.0) + jnp.where(lane == i2, w2, 0.0)


def _merge(ya, yb, mg, x2, wpa, wpb, wout, gffn, wr, br):
    t = x2.shape[0]
    row = lambda w: pl.BlockSpec((TM, w), lambda i: (i, 0))
    return pl.pallas_call(
        _merge_kernel,
        out_shape=[jax.ShapeDtypeStruct((t, D_MODEL), F32),
                   jax.ShapeDtypeStruct((t, D_MODEL), BF16),
                   jax.ShapeDtypeStruct((t, LANES), F32)],
        grid=(t // TM,),
        in_specs=[row(NSA_HEADS * NSA_HD), row(ML_WIDTH), row(2 * D_MODEL), row(D_MODEL)]
                 + [_const_spec(a.shape) for a in (wpa, wpb, wout, gffn, wr, br)],
        out_specs=[row(D_MODEL), row(D_MODEL), row(LANES)],
        compiler_params=pltpu.CompilerParams(dimension_semantics=("arbitrary",), vmem_limit_bytes=VMEM_LIMIT),
        name="merge",
    )(ya, yb, mg, x2, wpa, wpb, wout, gffn, wr, br)


def _moe_kernel(h2_ref, gate_ref, x1_ref, p_ref, w13_ref, w2_ref, gple_ref, wpg_ref, wpp_ref, gfin_ref,
                o_ref, acc_scr):
    h2 = h2_ref[...]
    gate = gate_ref[...]
    acc_scr[...] = x1_ref[...]
    for e in range(N_EXPERTS):
        a = _dot(h2, w13_ref[e])
        gt = a[:, 0:D_EXPERT]
        act = gt * _sigmoid(gt) * a[:, D_EXPERT:2 * D_EXPERT] * gate[:, e:e + 1]
        acc_scr[...] += _dot(act.astype(BF16), w2_ref[e])
    x2 = acc_scr[...]
    h3 = _rms(x2, gple_ref[...]).astype(BF16)
    x3 = x2 + _sigmoid(_dot(h3, wpg_ref[...])) * _dot(p_ref[...].astype(BF16), wpp_ref[...])
    o_ref[...] = _rms(x3, gfin_ref[...])


def _moe(h2, gate, x1, p2, w13, w2, gple, wpg, wpp, gfin):
    t = h2.shape[0]
    row = lambda w: pl.BlockSpec((TM, w), lambda i: (i, 0))
    return pl.pallas_call(
        _moe_kernel,
        out_shape=jax.ShapeDtypeStruct((t, D_MODEL), F32),
        grid=(t // TM,),
        in_specs=[row(D_MODEL), row(LANES), row(D_MODEL), row(PLE_DIM)]
                 + [_const_spec(a.shape) for a in (w13, w2, gple, wpg, wpp, gfin)],
        out_specs=row(D_MODEL),
        scratch_shapes=[pltpu.VMEM((TM, D_MODEL), F32)],
        compiler_params=pltpu.CompilerParams(dimension_semantics=("arbitrary",), vmem_limit_bytes=VMEM_LIMIT),
        name="moe",
    )(h2, gate, x1, p2, w13, w2, gple, wpg, wpp, gfin)


def _pack_inproj_weights(w):
    d = w.shape[0]
    qw = NSA_HEADS * NSA_HD
    kvw = NSA_KV * NSA_HD
    o = 0
    wq = w[:, o:o + qw]; o += qw
    wkc = w[:, o:o + kvw]; o += kvw
    wvc = w[:, o:o + kvw]; o += kvw
    wks = w[:, o:o + kvw]; o += kvw
    wvs = w[:, o:o + kvw]; o += kvw
    wkw = w[:, o:o + kvw]; o += kvw
    wvw = w[:, o:o + kvw]; o += kvw
    wga = w[:, o:o + 3 * NSA_HEADS]; o += 3 * NSA_HEADS
    wqkb = w[:, o:o + 2 * ML_WIDTH]; o += 2 * ML_WIDTH
    wvb = w[:, o:o + ML_WIDTH]; o += ML_WIDTH
    wob = w[:, o:o + ML_WIDTH]; o += ML_WIDTH
    wif = w[:, o:o + 2 * ML_HEADS]; o += 2 * ML_HEADS
    wmg = w[:, o:o + 2 * D_MODEL]
    zero64 = jnp.zeros((d, NSA_HD), w.dtype)
    qcols = []
    for h in range(NSA_HEADS):
        wh = wq[:, h * NSA_HD:(h + 1) * NSA_HD]
        qcols += [wh, zero64] if h // NSA_HPG == 0 else [zero64, wh]
    wsm = jnp.concatenate([wga, wif, jnp.zeros((d, LANES - 3 * NSA_HEADS - 2 * ML_HEADS), w.dtype)], axis=1)
    wcat = jnp.concatenate(qcols + [wkc, wvc, wks, wkw, wsm, wqkb, wvb, wob, wmg], axis=1).astype(BF16)
    wtr = jnp.concatenate([wvs, wvw, wsm], axis=1).T.astype(BF16)
    return wcat, wtr


def _rope_lane_tables(positions):
    inv = ROPE_THETA ** (-jnp.arange(0, ROPE_DIM, 2, dtype=F32) / ROPE_DIM)
    ang = positions.astype(F32).reshape(-1, 1) * inv[None, :]
    cos, sin = jnp.cos(ang), jnp.sin(ang)
    half = ROPE_DIM // 2
    d = jnp.arange(LANES) % NSA_HD
    cos_l, sin_l = jnp.tile(cos, (1, LANES // half)), jnp.tile(sin, (1, LANES // half))
    rc = jnp.where(d < ROPE_DIM, cos_l, 1.0)
    rp = jnp.where((d >= half) & (d < ROPE_DIM), sin_l, 0.0)
    rm = jnp.where(d < half, -sin_l, 0.0)
    return rc, rp, rm


def _pack_compress_weights(w1, w2, pe):
    half = CMP_LEN // 2
    w1r = w1.reshape(2, half, NSA_HD, CMP_HIDDEN)
    outs = []
    for part in range(2):
        wb = w1r[part].astype(BF16)
        zb = jnp.zeros_like(wb)
        wp = jnp.stack([jnp.stack([wb, zb], axis=2), jnp.stack([zb, wb], axis=2)], axis=1)
        outs.append(wp.reshape(half * NSA_KV * NSA_HD, NSA_KV * CMP_HIDDEN))
    pe8 = jnp.broadcast_to(pe.reshape(1, CMP_LEN * NSA_HD), (8, CMP_LEN * NSA_HD)).astype(BF16)
    return outs[0], outs[1], pe8, w1.astype(BF16)


def _stages(x, p, positions, g_mix, w_in, b_if, w_ck1, w_ck2, pe_ck, w_cv1, w_cv2, pe_cv, w_conv, b_conv, g_hn, w_pa, w_pb, w_out, g_ffn, w_rg, b_rg, w_re, b_re, w_e13, w_e2, g_ple, w_pg, w_pp, g_final):
    b, s, d = x.shape
    t = b * s
    rc, rp, rm = _rope_lane_tables(positions)
    assert w_in.shape[0] == 1, "the final norm is fused into the layer's last kernel: single-layer problem only"
    for i in range(w_in.shape[0]):
        x2 = x.reshape(t, d)
        wcat, wtr = _pack_inproj_weights(w_in[i])
        (qpad, kc_tok, vc_tok, ks, kw, vst, vwt, sm, smt, qkb, vb, ob, mg) = _inproj(
            x2, g_mix[i].reshape(1, d), wcat, wtr, rc, rp, rm)
        wka, wkb, pek, w1k = _pack_compress_weights(w_ck1[i], w_ck2[i], pe_ck[i])
        wva, wvb, pev, w1v = _pack_compress_weights(w_cv1[i], w_cv2[i], pe_cv[i])
        zpad = jnp.zeros((CMP_HIDDEN, NSA_HD), F32)
        w2k = jnp.stack([jnp.concatenate([w_ck2[i], zpad], axis=1),
                         jnp.concatenate([zpad, w_ck2[i]], axis=1)]).astype(BF16)
        w2vt = w_cv2[i].T.astype(BF16)
        nrow = s // CMP_STRIDE
        rk = kc_tok.reshape(b, nrow, CMP_STRIDE * LANES)
        rv = vc_tok.reshape(b, nrow, CMP_STRIDE * LANES)
        kcb, vct = _compress(rk, rv, wka, wkb, wva, wvb, pek, pev, w1k, w1v, w2k, w2vt)
        ya = _nsa(qpad, kcb, vct, ks, kw, vst, vwt, smt, b, s)
        bif = b_if[i].astype(F32)
        bifc = jnp.zeros((1, LANES), F32).at[0, SM_I:SM_I + 2 * ML_HEADS].set(bif)
        bifr = bifc.reshape(LANES, 1)
        yb = _mlstm(qkb, vb, ob, sm, smt, w_conv[i], b_conv[i].reshape(1, -1), bifc, bifr,
                    g_hn[i].reshape(1, -1), b, s)
        wr = jnp.concatenate([w_re[i], w_rg[i], jnp.zeros((d, LANES - N_EXPERTS - N_GROUPS), F32)], axis=1)
        wr_hi = wr.astype(BF16)
        wr = jnp.concatenate([wr_hi, (wr - wr_hi.astype(F32)).astype(BF16)], axis=1)
        br =jnp.concatenate([b_re[i], b_rg[i], jnp.zeros((LANES - N_EXPERTS - N_GROUPS,), F32)]).reshape(1, LANES)
        x1, h2, gate = _merge(ya, yb, mg, x2, w_pa[i].astype(BF16), w_pb[i].astype(BF16), w_out[i].astype(BF16),
                              g_ffn[i].reshape(1, d), wr, br)
        gfin = g_final.reshape(1, d)
        out = _moe(h2, gate, x1, p[i].reshape(t, PLE_DIM), w_e13[i].astype(BF16), w_e2[i].astype(BF16),
                   g_ple[i].reshape(1, d), w_pg[i].astype(BF16), w_pp[i].astype(BF16), gfin)
        x = out.reshape(b, s, d)
    return dict(out=x, qpad=qpad, ks=ks, kcb=kcb, vct=vct, y_a=ya, y_b=yb, x1=x1, gate=gate)


def kernel(x, p, positions, g_mix, w_in, b_if, w_ck1, w_ck2, pe_ck, w_cv1, w_cv2, pe_cv, w_conv, b_conv, g_hn, w_pa, w_pb, w_out, g_ffn, w_rg, b_rg, w_re, b_re, w_e13, w_e2, g_ple, w_pg, w_pp, g_final):
    return _stages(x, p, positions, g_mix, w_in, b_if, w_ck1, w_ck2, pe_ck, w_cv1, w_cv2, pe_cv, w_conv, b_conv, g_hn,
                   w_pa, w_pb, w_out, g_ffn, w_rg, b_rg, w_re, b_re, w_e13, w_e2, g_ple, w_pg, w_pp, g_final)["out"]
```

```python
import functools
import math

import jax
import jax.numpy as jnp
from jax import lax
from jax.experimental import pallas as pl
from jax.experimental.pallas import tpu as pltpu

F32 = jnp.float32
BF16 = jnp.bfloat16

EPS = 1e-6
NEG = -1e30

D_MODEL = 1024
PLE_DIM = 256
NSA_HEADS = 8
NSA_KV = 2
NSA_HPG = NSA_HEADS // NSA_KV
NSA_HD = 64
CMP_LEN = 32
CMP_STRIDE = 16
CMP_HIDDEN = 256
SEL_BLOCK = 64
SEL_TOPK = 16
SEL_FORCE = 1000.0
WINDOW = 512
ROPE_THETA = 500000.0
ROPE_DIM = NSA_HD // 4
ML_HEADS = 4
ML_HD = 128
ML_WIDTH = ML_HEADS * ML_HD
CONV_W = 4
N_GROUPS = 4
EXP_PER_GROUP = 4
N_EXPERTS = N_GROUPS * EXP_PER_GROUP
D_EXPERT = 256

LANES = 128
QT = 128
KC = 128
SEL_GROUP = 512
ML_CHUNK = 128
ML_BLOCK = 256
TM = 256
VMEM_LIMIT = 56 * 1024 * 1024

_NT = (((1,), (1,)), ((), ()))
_TN = (((0,), (0,)), ((), ()))

SM_GATE = 0
SM_I = 3 * NSA_HEADS
SM_F = SM_I + ML_HEADS


def _dot(a, b):
    return jnp.dot(a, b, preferred_element_type=F32)


def _dot_nt(a, b):
    return lax.dot_general(a, b, _NT, preferred_element_type=F32)


def _split3(x):
    hi = x.astype(BF16)
    r1 = x - hi.astype(F32)
    mid = r1.astype(BF16)
    lo = (r1 - mid.astype(F32)).astype(BF16)
    return hi, mid, lo


def _rms(x, g):
    return x * lax.rsqrt(jnp.mean(x * x, axis=-1, keepdims=True) + EPS) * g


def _sigmoid(x):
    return 1.0 / (1.0 + jnp.exp(-x))


def _const_spec(shape):
    nd = len(shape)
    return pl.BlockSpec(shape, lambda *_: (0,) * nd, pipeline_mode=pl.Buffered(1))


_C_Q = 0
_C_KC = _C_Q + NSA_HEADS * LANES
_C_VC = _C_KC + LANES
_C_KS = _C_VC + LANES
_C_KW = _C_KS + LANES
_C_SM = _C_KW + LANES
_C_QKB = _C_SM + LANES
_C_VB = _C_QKB + 2 * ML_WIDTH
_C_OB = _C_VB + ML_WIDTH
_C_MG = _C_OB + ML_WIDTH
_C_END = _C_MG + 2 * D_MODEL


def _inproj_kernel(x_ref, g_ref, w_ref, wt_ref, rc_ref, rp_ref, rm_ref,
                   q_ref, kc_ref, vc_ref, ks_ref, kw_ref, vst_ref, vwt_ref, sm_ref, smt_ref,
                   qkb_ref, vb_ref, ob_ref, mg_ref):
    hn = _rms(x_ref[...], g_ref[...]).astype(BF16)
    rc, rp, rm = rc_ref[...], rp_ref[...], rm_ref[...]

    def rope(z):
        return z * rc + pltpu.roll(z, 8, 1) * rp + pltpu.roll(z, LANES - 8, 1) * rm

    scale = NSA_HD ** -0.5 * math.log2(math.e)
    for h in range(NSA_HEADS):
        z = _dot(hn, w_ref[:, _C_Q + h * LANES:_C_Q + (h + 1) * LANES])
        q_ref[:, h * LANES:(h + 1) * LANES] = (rope(z) * scale).astype(BF16)
    kc_ref[...] = rope(_dot(hn, w_ref[:, _C_KC:_C_KC + LANES])).astype(BF16)
    vc_ref[...] = _dot(hn, w_ref[:, _C_VC:_C_VC + LANES]).astype(BF16)
    ks_ref[...] = rope(_dot(hn, w_ref[:, _C_KS:_C_KS + LANES])).astype(BF16)
    kw_ref[...] = rope(_dot(hn, w_ref[:, _C_KW:_C_KW + LANES])).astype(BF16)
    sm_ref[...] = _dot(hn, w_ref[:, _C_SM:_C_SM + LANES])
    for c0 in range(0, 2 * ML_WIDTH, 512):
        qkb_ref[:, c0:c0 + 512] = _dot(hn, w_ref[:, _C_QKB + c0:_C_QKB + c0 + 512]).astype(BF16)
    vb_ref[...] = _dot(hn, w_ref[:, _C_VB:_C_VB + ML_WIDTH]).astype(BF16)
    ob_ref[...] = _dot(hn, w_ref[:, _C_OB:_C_OB + ML_WIDTH]).astype(BF16)
    for c0 in range(0, 2 * D_MODEL, 512):
        mg_ref[:, c0:c0 + 512] = _dot(hn, w_ref[:, _C_MG + c0:_C_MG + c0 + 512]).astype(BF16)
    zt = _dot_nt(wt_ref[...], hn)
    for i in range(TM // KC):
        vst_ref[i] = zt[0:LANES, i * KC:(i + 1) * KC].astype(BF16)
        vwt_ref[i] = zt[LANES:2 * LANES, i * KC:(i + 1) * KC].astype(BF16)
    smt_ref[...] = zt[2 * LANES:3 * LANES, :]


def _inproj(x2, g_mix, wcat, wtr, rc, rp, rm):
    t = x2.shape[0]
    row = lambda w: pl.BlockSpec((TM, w), lambda i: (i, 0))
    out_shape = [
        jax.ShapeDtypeStruct((t, NSA_HEADS * LANES), BF16),
        jax.ShapeDtypeStruct((t, LANES), BF16),
        jax.ShapeDtypeStruct((t, LANES), BF16),
        jax.ShapeDtypeStruct((t, LANES), BF16),
        jax.ShapeDtypeStruct((t, LANES), BF16),
        jax.ShapeDtypeStruct((t // KC, LANES, KC), BF16),
        jax.ShapeDtypeStruct((t // KC, LANES, KC), BF16),
        jax.ShapeDtypeStruct((t, LANES), F32),
        jax.ShapeDtypeStruct((LANES, t), F32),
        jax.ShapeDtypeStruct((t, 2 * ML_WIDTH), BF16),
        jax.ShapeDtypeStruct((t, ML_WIDTH), BF16),
        jax.ShapeDtypeStruct((t, ML_WIDTH), BF16),
        jax.ShapeDtypeStruct((t, 2 * D_MODEL), BF16),
    ]
    chunk3 = pl.BlockSpec((TM // KC, LANES, KC), lambda i: (i, 0, 0))
    out_specs = [row(NSA_HEADS * LANES), row(LANES), row(LANES), row(LANES), row(LANES), chunk3, chunk3,
                 row(LANES), pl.BlockSpec((LANES, TM), lambda i: (0, i)),
                 row(2 * ML_WIDTH), row(ML_WIDTH), row(ML_WIDTH), row(2 * D_MODEL)]
    return pl.pallas_call(
        _inproj_kernel,
        out_shape=out_shape,
        grid=(t // TM,),
        in_specs=[row(D_MODEL), _const_spec((1, D_MODEL)), _const_spec((D_MODEL, _C_END)),
                  _const_spec((3 * LANES, D_MODEL)), row(LANES), row(LANES), row(LANES)],
        out_specs=out_specs,
        compiler_params=pltpu.CompilerParams(dimension_semantics=("arbitrary",), vmem_limit_bytes=VMEM_LIMIT),
        name="inproj",
    )(x2, g_mix, wcat, wtr, rc, rp, rm)


def _gelu_tanh(x):
    return 0.5 * x * (1.0 + jnp.tanh(math.sqrt(2.0 / math.pi) * (x + 0.044715 * x * x * x)))


def _compress_kernel(rk_ref, rv_ref, wka_ref, wkb_ref, wva_ref, wvb_ref, pek_ref, pev_ref,
                     w1k_ref, w1v_ref, w2k_ref, w2vt_ref, kc_ref, vct_ref):
    nrow = rk_ref.shape[0]

    def hidden(r_ref, wa_ref, wb_ref, pe_ref, w1_ref):
        r = r_ref[...]
        ha = _dot(r, wa_ref[...])
        hb = _dot(r, wb_ref[...])
        hb = pltpu.roll(hb, nrow - 1, 0)
        c = _dot(pe_ref[...], w1_ref[...])[0:1, :]
        return [_gelu_tanh(ha[:, g * CMP_HIDDEN:(g + 1) * CMP_HIDDEN] + hb[:, g * CMP_HIDDEN:(g + 1) * CMP_HIDDEN] + c).astype(BF16)
                for g in range(NSA_KV)]

    ak = hidden(rk_ref, wka_ref, wkb_ref, pek_ref, w1k_ref)
    kc_ref[...] = (_dot(ak[0], w2k_ref[0]) + _dot(ak[1], w2k_ref[1])).astype(BF16)
    av = hidden(rv_ref, wva_ref, wvb_ref, pev_ref, w1v_ref)
    for g in range(NSA_KV):
        vct_ref[g * NSA_HD:(g + 1) * NSA_HD, :] = _dot_nt(w2vt_ref[...], av[g]).astype(BF16)


def _compress(rk, rv, wka, wkb, wva, wvb, pek, pev, w1k, w1v, w2k, w2vt):
    b, nrow, width = rk.shape
    blk = pl.BlockSpec((None, nrow, width), lambda i: (i, 0, 0))
    return pl.pallas_call(
        _compress_kernel,
        out_shape=[jax.ShapeDtypeStruct((b, nrow, LANES), BF16),
                   jax.ShapeDtypeStruct((b, LANES, nrow), BF16)],
        grid=(b,),
        in_specs=[blk, blk] + [_const_spec(a.shape) for a in (wka, wkb, wva, wvb, pek, pev, w1k, w1v, w2k, w2vt)],
        out_specs=[pl.BlockSpec((None, nrow, LANES), lambda i: (i, 0, 0)),
                   pl.BlockSpec((None, LANES, nrow), lambda i: (i, 0, 0))],
        compiler_params=pltpu.CompilerParams(dimension_semantics=("arbitrary",), vmem_limit_bytes=VMEM_LIMIT),
        name="compress",
    )(rk, rv, wka, wkb, wva, wvb, pek, pev, w1k, w1v, w2k, w2vt)


def _nsa_kernel(q_ref, kc_ref, vct_ref, ks_ref, kw_ref, vst_ref, vwt_ref, smt_ref, o_ref, bias_scr, sx_scr, sy_scr):
    g = pl.program_id(1)
    c = pl.program_id(2)
    t0 = c * QT
    ncmp = kc_ref.shape[0]
    nsel = bias_scr.shape[0]
    nw = WINDOW // KC + 1
    width = NSA_HPG * QT

    qs = jnp.concatenate([q_ref[:, h * LANES:(h + 1) * LANES] for h in range(NSA_HPG)], axis=0)
    u_row = lax.broadcasted_iota(jnp.int32, (1, width), 1) % QT
    t_row = t0 + u_row
    r_kc = lax.broadcasted_iota(jnp.int32, (KC, 1), 0)

    n_grp = ks_ref.shape[0] // SEL_GROUP
    n_full = lax.shift_right_logical(t0, int(math.log2(SEL_GROUP)))

    def qk_group(j):
        return _dot_nt(ks_ref[pl.ds(pl.multiple_of(j * SEL_GROUP, SEL_GROUP), SEL_GROUP), :], qs)

    sc = _dot_nt(kc_ref[...], qs)

    w_slabs, w_chunks = [], []
    for i in range(nw):
        jj = c - (nw - 1) + i
        jc = jnp.maximum(jj, 0)
        si = _dot_nt(kw_ref[pl.ds(pl.multiple_of(jc * KC, KC), KC), :], qs)
        if i == 0:
            keep = (r_kc > u_row) & (jj >= 0)
        elif i == nw - 1:
            keep = r_kc <= u_row
        else:
            keep = jj >= 0
        w_slabs.append(jnp.where(keep, si, NEG))
        w_chunks.append(jc)

    n_col = lax.broadcasted_iota(jnp.int32, (ncmp, 1), 0)
    cmask = (CMP_STRIDE * n_col + (CMP_LEN - 1) <= t_row) & (n_col < ncmp - 1)
    s = jnp.where(cmask, sc, NEG)
    m = jnp.max(s, axis=0, keepdims=True)
    e = jnp.exp2(s - m)
    anyv = (t_row >= CMP_LEN - 1).astype(F32)
    p = e * (anyv / jnp.sum(e, axis=0, keepdims=True))
    o_cmp = _dot(vct_ref[...], p.astype(BF16))

    psum = p[:, 0:QT]
    for h in range(1, NSA_HPG):
        psum = psum + p[:, h * QT:(h + 1) * QT]
    s_col = lax.broadcasted_iota(jnp.int32, (nsel, 1), 0)
    n_lane = lax.broadcasted_iota(jnp.int32, (1, ncmp), 1)
    ov = ((CMP_STRIDE * n_lane < SEL_BLOCK * (s_col + 1)) & (CMP_STRIDE * n_lane + (CMP_LEN - 1) >= SEL_BLOCK * s_col)
          ).astype(BF16)
    imp = sum(_dot(ov, part) for part in _split3(psum))

    s_diag = qk_group(n_full)
    sx_scr[...] = qk_group(0)

    mxw = w_slabs[0]
    for sl in w_slabs[1:]:
        mxw = jnp.maximum(mxw, sl)
    mw = jnp.max(mxw, axis=0, keepdims=True)
    pws = None
    acc_w = jnp.zeros((NSA_HD, width), F32)
    for sl, jc in zip(w_slabs, w_chunks):
        pw = jnp.exp2(sl - mw)
        pws = pw if pws is None else pws + pw
        acc_w = acc_w + _dot(vwt_ref[jc], pw.astype(BF16))
    o_win = acc_w / jnp.sum(pws, axis=0, keepdims=True)

    t1 = t0 + lax.broadcasted_iota(jnp.int32, (1, QT), 1)
    cur = lax.shift_right_logical(t1, 6)
    forced = (s_col == 0) | (s_col == cur) | (s_col == cur - 1)
    valid = SEL_BLOCK * s_col <= t1
    val = jnp.where(valid, jnp.where(forced, imp + SEL_FORCE, imp), NEG)
    sub = 8
    r_sub = lax.broadcasted_iota(jnp.int32, (sub, 1), 0)
    blocks = [val[r * sub:(r + 1) * sub, :] for r in range(nsel // sub)]
    ranks = [jnp.zeros((sub, QT), F32) for _ in blocks]
    for i in range(nsel):
        vi = val[i:i + 1, :]
        for r, blk in enumerate(blocks):
            if i < r * sub:
                beats = vi >= blk
            elif i >= (r + 1) * sub:
                beats = vi > blk
            else:
                beats = (vi > blk) | ((vi == blk) & (r_sub > i - r * sub))
            ranks[r] = ranks[r] + jnp.where(beats, 1.0, 0.0)
    bias = jnp.where(jnp.concatenate(ranks, axis=0) < SEL_TOPK, 0.0, NEG).astype(F32)
    bias_scr[...] = jnp.concatenate([bias] * NSA_HPG, axis=1)

    r_blk = lax.broadcasted_iota(jnp.int32, (SEL_BLOCK, 1), 0)
    blk_per_grp = SEL_GROUP // SEL_BLOCK
    chunk_per_grp = SEL_GROUP // KC
    blk_per_chunk = KC // SEL_BLOCK

    def sel_mask(sj, j, causal=False, live=None):
        slabs = []
        for i in range(blk_per_grp):
            brow = bias_scr[pl.ds(blk_per_grp * j + i, 1), :]
            if live is not None:
                brow = jnp.where(live, brow, NEG)
            sl = sj[i * SEL_BLOCK:(i + 1) * SEL_BLOCK, :] + brow
            if causal:
                sl = jnp.where(j * SEL_GROUP + i * SEL_BLOCK + r_blk <= t_row, sl, NEG)
            slabs.append(sl)
        return slabs

    def sel_update(j, slabs, carry):
        m_o, l_o, acc = carry
        mx = slabs[0]
        for sl in slabs[1:]:
            mx = jnp.maximum(mx, sl)
        m_n = jnp.maximum(m_o, jnp.max(mx, axis=0, keepdims=True))
        a = jnp.exp2(m_o - m_n)
        acc = a * acc
        psum = None
        for ci in range(chunk_per_grp):
            pj = jnp.exp2(jnp.concatenate(slabs[blk_per_chunk * ci:blk_per_chunk * (ci + 1)], axis=0) - m_n)
            psum = pj if psum is None else psum + pj
            acc = acc + _dot(vst_ref[chunk_per_grp * j + ci], pj.astype(BF16))
        l_n = a * l_o + jnp.sum(psum, axis=0, keepdims=True)
        return m_n, l_n, acc

    empty = (jnp.full((1, width), NEG, F32), jnp.zeros((1, width), F32), jnp.zeros((NSA_HD, width), F32))
    seeded = sel_update(n_full, sel_mask(s_diag, n_full, causal=True), empty)

    def pair_body(jp, carry):
        ja, jb = 2 * jp, 2 * jp + 1
        sy_scr[...] = qk_group(jb)
        carry = sel_update(ja, sel_mask(sx_scr, ja), carry)
        sx_scr[...] = qk_group(jnp.minimum(ja + 2, n_grp - 1))
        return sel_update(jb, sel_mask(sy_scr, jb, live=jb < n_full), carry)

    _, l_s, acc_s = lax.fori_loop(0, lax.shift_right_logical(n_full + 1, 1), pair_body, seeded)
    o_sel = acc_s / l_s

    def gate_row(br):
        rows = [smt_ref[pl.ds(SM_GATE + 3 * (NSA_HPG * g + h) + br, 1), :] for h in range(NSA_HPG)]
        return _sigmoid(jnp.concatenate(rows, axis=1))

    o_t = gate_row(0) * o_cmp + gate_row(1) * o_sel + gate_row(2) * o_win
    for pr in range(NSA_HPG // 2):
        xp = jnp.concatenate([o_t[:, (2 * pr) * QT:(2 * pr + 1) * QT], o_t[:, (2 * pr + 1) * QT:(2 * pr + 2) * QT]], axis=0)
        o_ref[:, pr * LANES:(pr + 1) * LANES] = xp.T.astype(BF16)


def _nsa(qpad, kcb, vct, ks, kw, vst, vwt, smt, b, s):
    nq = s // QT
    ncmp = kcb.shape[1]
    return pl.pallas_call(
        _nsa_kernel,
        out_shape=jax.ShapeDtypeStruct((b * s, NSA_HEADS * NSA_HD), BF16),
        grid=(b, NSA_KV, nq),
        in_specs=[
            pl.BlockSpec((QT, NSA_HPG * LANES), lambda bi, g, c: (bi * nq + c, g)),
            pl.BlockSpec((None, ncmp, LANES), lambda bi, g, c: (bi, 0, 0)),
            pl.BlockSpec((None, NSA_HD, ncmp), lambda bi, g, c: (bi, g, 0)),
            pl.BlockSpec((s, LANES), lambda bi, g, c: (bi, 0)),
            pl.BlockSpec((s, LANES), lambda bi, g, c: (bi, 0)),
            pl.BlockSpec((s // KC, NSA_HD, KC), lambda bi, g, c: (bi, g, 0)),
            pl.BlockSpec((s // KC, NSA_HD, KC), lambda bi, g, c: (bi, g, 0)),
            pl.BlockSpec((LANES, QT), lambda bi, g, c: (0, bi * nq + c)),
        ],
        out_specs=pl.BlockSpec((QT, NSA_HPG * NSA_HD), lambda bi, g, c: (bi * nq + c, g)),
        scratch_shapes=[pltpu.VMEM((s // SEL_BLOCK, NSA_HPG * QT), F32),
                        pltpu.VMEM((SEL_GROUP, NSA_HPG * QT), F32),
                        pltpu.VMEM((SEL_GROUP, NSA_HPG * QT), F32)],
        compiler_params=pltpu.CompilerParams(dimension_semantics=("arbitrary", "arbitrary", "arbitrary"),
                                             vmem_limit_bytes=VMEM_LIMIT),
        name="nsa",
    )(qpad, kcb, vct, ks, kw, vst, vwt, smt)


def _log_sigmoid(x):
    return jnp.minimum(x, 0.0) - jnp.log(1.0 + jnp.exp(-jnp.abs(x)))


def _mlstm_kernel(qk_ref, v_ref, og_ref, sm_ref, smt_ref, wc_ref, bc_ref, bifc_ref, bifr_ref, ghn_ref,
                  y_ref, ext_scr, ct_scr, n_scr, m_scr):
    lc = ML_CHUNK

    @pl.when(pl.program_id(1) == 0)
    def _():
        ext_scr[0:8, :] = jnp.zeros((8, 2 * ML_WIDTH), F32)
        ct_scr[...] = jnp.zeros_like(ct_scr)
        n_scr[...] = jnp.zeros_like(n_scr)
        m_scr[...] = jnp.zeros_like(m_scr)

    ext_scr[8:8 + ML_BLOCK, :] = qk_ref[...].astype(F32)
    y = bc_ref[...]
    for j in range(CONV_W):
        y = y + wc_ref[j:j + 1, :] * ext_scr[pl.ds(8 - (CONV_W - 1) + j, ML_BLOCK), :]
    ext_scr[0:8, :] = ext_scr[ML_BLOCK:ML_BLOCK + 8, :]
    qkc = y * _sigmoid(y)
    q_all = qkc[:, 0:ML_WIDTH].astype(BF16)
    k_all = (qkc[:, ML_WIDTH:2 * ML_WIDTH] * (ML_HD ** -0.5)).astype(BF16)

    ifc = sm_ref[...] + bifc_ref[...]
    ifr = smt_ref[...] + bifr_ref[...]
    lfc = _log_sigmoid(ifc)
    lfr = _log_sigmoid(ifr)
    rr = lax.broadcasted_iota(jnp.int32, (lc, lc), 0)
    cc = lax.broadcasted_iota(jnp.int32, (lc, lc), 1)
    causal = rr >= cc
    tri_l = causal.astype(F32)
    tri_u = (rr <= cc).astype(F32)

    for ci in range(ML_BLOCK // lc):
        lo, hi = ci * lc, (ci + 1) * lc
        bc_all = jnp.dot(tri_l, lfc[lo:hi, :], preferred_element_type=F32, precision=lax.Precision.HIGHEST)
        br_all = jnp.dot(lfr[:, lo:hi], tri_u, preferred_element_type=F32, precision=lax.Precision.HIGHEST)
        for h in range(ML_HEADS):
            hs = slice(h * ML_HD, (h + 1) * ML_HD)
            bcol = bc_all[:, SM_F + h:SM_F + h + 1]
            brow = br_all[SM_F + h:SM_F + h + 1, :]
            icol = ifc[lo:hi, SM_I + h:SM_I + h + 1]
            irow = ifr[SM_I + h:SM_I + h + 1, lo:hi]
            mprev = m_scr[h][:, 0:1]
            qh = q_all[lo:hi, hs]
            kh = k_all[lo:hi, hs]
            vh = v_ref[lo:hi, hs]
            dmat = jnp.where(causal, bcol - brow + irow, NEG)
            inter = bcol + mprev
            mt = jnp.maximum(jnp.max(dmat, axis=-1, keepdims=True), inter)
            a = jnp.exp(dmat - mt) * _dot_nt(qh, kh)
            dec = jnp.exp(inter - mt)
            ct = ct_scr[h]
            nrow = n_scr[h]
            num = _dot(a.astype(BF16), vh) + dec * _dot(qh, ct.astype(BF16))
            den = jnp.sum(a, axis=-1, keepdims=True) + dec * jnp.sum(qh.astype(F32) * nrow, axis=-1, keepdims=True)
            hh = num / jnp.maximum(jnp.abs(den), jnp.exp(-mt))
            blast = bcol[lc - 1:lc, :]
            grow = blast - brow + irow
            mnew = jnp.maximum(blast + mprev, jnp.max(grow, axis=-1, keepdims=True))
            wprev = jnp.exp(blast + mprev - mnew)
            kwt = kh.astype(F32) * jnp.exp(blast - bcol + icol - mnew)
            ct_scr[h] = wprev * ct + lax.dot_general(kwt.astype(BF16), vh, _TN, preferred_element_type=F32)
            n_scr[h] = wprev * nrow + jnp.sum(kwt, axis=0, keepdims=True)
            m_scr[h] = jnp.broadcast_to(mnew, (1, LANES))
            hm = hh * _sigmoid(og_ref[lo:hi, hs].astype(F32))
            y_ref[lo:hi, hs] = _rms(hm, ghn_ref[:, hs]).astype(BF16)


def _mlstm(qkb, vb, ob, sm, smt, wconv, bconv, bifc, bifr, ghn, b, s):
    nb = s // ML_BLOCK
    row = lambda w: pl.BlockSpec((ML_BLOCK, w), lambda bi, j: (bi * nb + j, 0))
    return pl.pallas_call(
        _mlstm_kernel,
        out_shape=jax.ShapeDtypeStruct((b * s, ML_WIDTH), BF16),
        grid=(b, nb),
        in_specs=[row(2 * ML_WIDTH), row(ML_WIDTH), row(ML_WIDTH), row(LANES),
                  pl.BlockSpec((LANES, ML_BLOCK), lambda bi, j: (0, bi * nb + j)),
                  _const_spec(wconv.shape), _const_spec(bconv.shape), _const_spec(bifc.shape),
                  _const_spec(bifr.shape), _const_spec(ghn.shape)],
        out_specs=row(ML_WIDTH),
        scratch_shapes=[pltpu.VMEM((ML_BLOCK + 8, 2 * ML_WIDTH), F32),
                        pltpu.VMEM((ML_HEADS, ML_HD, ML_HD), F32),
                        pltpu.VMEM((ML_HEADS, 1, ML_HD), F32),
                        pltpu.VMEM((ML_HEADS, 1, LANES), F32)],
        compiler_params=pltpu.CompilerParams(dimension_semantics=("arbitrary", "arbitrary"),
                                             vmem_limit_bytes=VMEM_LIMIT),
        name="mlstm",
    )(qkb, vb, ob, sm, smt, wconv, bconv, bifc, bifr, ghn)


RT_BUCKET = N_EXPERTS
RT_RANK = N_EXPERTS + 1
RT_WLO = N_EXPERTS + 2
RT_WHI = N_EXPERTS + 3
N_BUCKETS = N_GROUPS * 6
SUB = 8
X_ROWS = D_MODEL // LANES
REC = 2 * SUB


def _merge_kernel(ya_ref, yb_ref, mg_ref, x_ref, wpa_ref, wpb_ref, wout_ref, gffn_ref, wr_ref, br_ref,
                  x1_ref, hx_ref, hist_ref):
    pa = _dot(ya_ref[...], wpa_ref[...])
    pb = _dot(yb_ref[...], wpb_ref[...])
    ga = _sigmoid(mg_ref[:, 0:D_MODEL].astype(F32))
    gb = _sigmoid(mg_ref[:, D_MODEL:2 * D_MODEL].astype(F32))
    mixed = (ga * pa + gb * pb).astype(BF16)
    x1 = x_ref[...] + _dot(mixed, wout_ref[...])
    x1_ref[...] = x1
    h2 = _rms(x1, gffn_ref[...])
    h_hi = h2.astype(BF16)
    for j in range(X_ROWS):
        hx_ref[pl.ds(j, TM, stride=REC), :] = h2[:, j * LANES:(j + 1) * LANES]

    h_lo = (h2 - h_hi.astype(F32)).astype(BF16)
    r_hi = _dot(h_hi, wr_ref[...])
    logit = r_hi[:, 0:LANES] + r_hi[:, LANES:2 * LANES] + _dot(h_lo, wr_ref[:, 0:LANES]) + br_ref[...]
    lane = lax.broadcasted_iota(jnp.int32, logit.shape, 1)
    big = jnp.int32(LANES)
    gmask = (lane >= N_EXPERTS) & (lane < N_EXPERTS + N_GROUPS)
    gl = jnp.where(gmask, logit, NEG)
    gmax = jnp.max(gl, axis=-1, keepdims=True)
    gidx = jnp.min(jnp.where(gmask & (gl == gmax), lane, big), axis=-1, keepdims=True) - N_EXPERTS
    pg_sel = 1.0 / jnp.sum(jnp.where(gmask, jnp.exp(gl - gmax), 0.0), axis=-1, keepdims=True)
    emask = (lane < N_EXPERTS) & (lax.shift_right_logical(lane, 2) == gidx)
    el = jnp.where(emask, logit, NEG)
    e1 = jnp.max(el, axis=-1, keepdims=True)
    i1 = jnp.min(jnp.where(emask & (el == e1), lane, big), axis=-1, keepdims=True)
    emask2 = emask & (lane != i1)
    el2 = jnp.where(emask2, logit, NEG)
    e2 = jnp.max(el2, axis=-1, keepdims=True)
    i2 = jnp.min(jnp.where(emask2 & (el2 == e2), lane, big), axis=-1, keepdims=True)
    x21 = jnp.exp(e2 - e1)
    w1 = pg_sel / (1.0 + x21)
    w2 = pg_sel * x21 / (1.0 + x21)
    first_lo = i1 < i2
    e_lo = jnp.where(first_lo, i1, i2) - EXP_PER_GROUP * gidx
    e_hi = jnp.where(first_lo, i2, i1) - EXP_PER_GROUP * gidx
    pair = lax.shift_right_logical(e_lo * (2 * EXP_PER_GROUP - 1 - e_lo), 1) + (e_hi - e_lo - 1)
    bucket = 6 * gidx + pair
    member = lane == bucket
    onehot = jnp.where(member, 1.0, 0.0)
    rr = lax.broadcasted_iota(jnp.int32, (TM, TM), 0)
    cc = lax.broadcasted_iota(jnp.int32, (TM, TM), 1)
    earlier = _dot((rr > cc).astype(BF16), onehot.astype(BF16))
    rank = jnp.sum(jnp.where(member, earlier, 0.0), axis=-1, keepdims=True)
    slab = jnp.where(lane == i1, w1, 0.0) + jnp.where(lane == i2, w2, 0.0)
    slab = jnp.where(lane == RT_BUCKET, bucket.astype(F32), slab)
    slab = jnp.where(lane == RT_RANK, rank, slab)
    slab = jnp.where(lane == RT_WLO, jnp.where(first_lo, w1, w2), slab)
    slab = jnp.where(lane == RT_WHI, jnp.where(first_lo, w2, w1), slab)
    hx_ref[pl.ds(X_ROWS, TM, stride=REC), :] = slab
    for j in range(X_ROWS + 1, REC):
        hx_ref[pl.ds(j, TM, stride=REC), :] = jnp.zeros((TM, LANES), F32)
    hist_ref[...] = jnp.broadcast_to(jnp.sum(onehot, axis=0, keepdims=True), (8, LANES))


def _merge(ya, yb, mg, x2, wpa, wpb, wout, gffn, wr, br):
    t = x2.shape[0]
    row = lambda w: pl.BlockSpec((TM, w), lambda i: (i, 0))
    return pl.pallas_call(
        _merge_kernel,
        out_shape=[jax.ShapeDtypeStruct((t, D_MODEL), F32),
                   jax.ShapeDtypeStruct((t * REC, LANES), F32),
                   jax.ShapeDtypeStruct((t // TM * 8, LANES), F32)],
        grid=(t // TM,),
        in_specs=[row(NSA_HEADS * NSA_HD), row(ML_WIDTH), row(2 * D_MODEL), row(D_MODEL)]
                 + [_const_spec(a.shape) for a in (wpa, wpb, wout, gffn, wr, br)],
        out_specs=[row(D_MODEL), pl.BlockSpec((TM * REC, LANES), lambda i: (i, 0)),
                   pl.BlockSpec((8, LANES), lambda i: (i, 0))],
        compiler_params=pltpu.CompilerParams(dimension_semantics=("arbitrary",), vmem_limit_bytes=VMEM_LIMIT),
        name="merge",
    )(ya, yb, mg, x2, wpa, wpb, wout, gffn, wr, br)


def _rec_copy(src_ref, src_tok, dst_ref, dst_tok, sem, rows):
    src = src_ref.at[pl.ds(pl.multiple_of(src_tok * rows, rows), rows), :]
    dst = dst_ref.at[pl.ds(pl.multiple_of(dst_tok * rows, rows), rows), :]
    return pltpu.make_async_copy(src, dst, sem)


def _dispatch_kernel(pos_ref, tail_ref, hx_ref, out_ref, stage_scr, zero_scr, sem, zsem):
    i = pl.program_id(0)
    slot = lax.rem(i, 2)

    n_tiles = out_ref.shape[0] // (TM * REC)
    n_used = tail_ref[2 * N_BUCKETS]

    def zero_copy(first_slot):
        start = pl.multiple_of(first_slot * REC, TM * REC)
        return pltpu.make_async_copy(zero_scr, out_ref.at[pl.ds(start, TM * REC), :], zsem)

    @pl.when(i == 0)
    def _():
        zero_scr[...] = jnp.zeros_like(zero_scr)
        for phase in ("start", "wait"):
            for b in range(N_BUCKETS):
                @pl.when(tail_ref[N_BUCKETS + b] > 0)
                def _():
                    getattr(zero_copy(tail_ref[b]), phase)()

                @pl.when(n_used + b < n_tiles)
                def _():
                    getattr(zero_copy((n_used + b) * TM), phase)()

    stage_scr[slot] = hx_ref[...]
    base = i * TM

    def issue(r, carry):
        _rec_copy(stage_scr.at[slot], r, out_ref, pos_ref[base + r], sem.at[slot], REC).start()
        return carry

    lax.fori_loop(0, TM, issue, 0, unroll=8)

    def drain(which):
        def body(r, carry):
            _rec_copy(stage_scr.at[which], 0, out_ref, 0, sem.at[which], REC).wait()
            return carry
        lax.fori_loop(0, TM, body, 0, unroll=8)

    @pl.when(i > 0)
    def _():
        drain(1 - slot)

    @pl.when(i == pl.num_programs(0) - 1)
    def _():
        drain(slot)


def _dispatch(pos, tail, hx, n_slots):
    t = hx.shape[0] // REC
    return pl.pallas_call(
        _dispatch_kernel,
        out_shape=jax.ShapeDtypeStruct((n_slots * REC, LANES), F32),
        grid_spec=pltpu.PrefetchScalarGridSpec(
            num_scalar_prefetch=2,
            grid=(t // TM,),
            in_specs=[pl.BlockSpec((TM * REC, LANES), lambda i, pos_r, tail_r: (i, 0))],
            out_specs=pl.BlockSpec(memory_space=pl.ANY),
            scratch_shapes=[pltpu.VMEM((2, TM * REC, LANES), F32), pltpu.VMEM((TM * REC, LANES), F32),
                            pltpu.SemaphoreType.DMA((2,)), pltpu.SemaphoreType.DMA(())],
        ),
        compiler_params=pltpu.CompilerParams(dimension_semantics=("arbitrary",), vmem_limit_bytes=VMEM_LIMIT,
                                             has_side_effects=True),
        name="dispatch",
    )(pos, tail, hx)


def _moe_kernel(te_ref, nu_ref, hx_ref, w13_ref, w2_ref, y_ref):
    k = pl.program_id(0)
    n_tiles = pl.num_programs(0)

    @pl.when(k < nu_ref[0])
    def _():
        h = jnp.concatenate([hx_ref[pl.ds(j, TM, stride=REC), :] for j in range(X_ROWS)], axis=1).astype(BF16)
        slab = hx_ref[pl.ds(X_ROWS, TM, stride=REC), :]
        y = None
        for side, lane in ((0, RT_WLO), (1, RT_WHI)):
            e = te_ref[side * n_tiles + k]
            a = _dot(h, w13_ref[e])
            gt = a[:, 0:D_EXPERT]
            act = gt * _sigmoid(gt) * a[:, D_EXPERT:2 * D_EXPERT] * slab[:, lane:lane + 1]
            part = _dot(act.astype(BF16), w2_ref[e])
            y = part if y is None else y + part
        for j in range(X_ROWS):
            y_ref[pl.ds(j, TM, stride=X_ROWS), :] = y[:, j * LANES:(j + 1) * LANES]

    @pl.when(k >= nu_ref[0])
    def _():
        y_ref[...] = jnp.zeros_like(y_ref)


def _moe(tile_e, n_used, hx_sorted, w13, w2):
    n_tiles = hx_sorted.shape[0] // (TM * REC)
    return pl.pallas_call(
        _moe_kernel,
        out_shape=jax.ShapeDtypeStruct((n_tiles * TM * X_ROWS, LANES), F32),
        grid_spec=pltpu.PrefetchScalarGridSpec(
            num_scalar_prefetch=2,
            grid=(n_tiles,),
            in_specs=[pl.BlockSpec((TM * REC, LANES), lambda k, te, nu: (jnp.minimum(k, nu[0] - 1), 0)),
                      pl.BlockSpec(w13.shape, lambda k, te, nu: (0, 0, 0), pipeline_mode=pl.Buffered(1)),
                      pl.BlockSpec(w2.shape, lambda k, te, nu: (0, 0, 0), pipeline_mode=pl.Buffered(1))],
            out_specs=pl.BlockSpec((TM * X_ROWS, LANES), lambda k, te, nu: (k, 0)),
        ),
        compiler_params=pltpu.CompilerParams(dimension_semantics=("arbitrary",), vmem_limit_bytes=VMEM_LIMIT),
        name="moe",
    )(tile_e, n_used, hx_sorted, w13, w2)


def _combine_kernel(pos_ref, y_ref, x1_ref, p_ref, gple_ref, wpg_ref, wpp_ref, gfin_ref, o_ref, ybuf, sem):
    i = pl.program_id(0)
    slot = lax.rem(i, 2)

    def gather(tile, which):
        def body(r, carry):
            _rec_copy(y_ref, pos_ref[tile * TM + r], ybuf.at[which], r, sem.at[which], X_ROWS).start()
            return carry
        lax.fori_loop(0, TM, body, 0, unroll=8)

    @pl.when(i == 0)
    def _():
        gather(0, 0)

    @pl.when(i + 1 < pl.num_programs(0))
    def _():
        gather(i + 1, 1 - slot)

    def drain(r, carry):
        _rec_copy(y_ref, 0, ybuf.at[slot], 0, sem.at[slot], X_ROWS).wait()
        return carry

    lax.fori_loop(0, TM, drain, 0, unroll=8)
    yb = ybuf.at[slot]
    x2 = x1_ref[...] + jnp.concatenate([yb[pl.ds(j, TM, stride=X_ROWS), :] for j in range(X_ROWS)], axis=1)
    h3 = _rms(x2, gple_ref[...]).astype(BF16)
    x3 = x2 + _sigmoid(_dot(h3, wpg_ref[...])) * _dot(p_ref[...].astype(BF16), wpp_ref[...])
    o_ref[...] = _rms(x3, gfin_ref[...])


def _combine(pos, y_sorted, x1, p2, gple, wpg, wpp, gfin):
    t = x1.shape[0]
    row = lambda w: pl.BlockSpec((TM, w), lambda i, pos_r: (i, 0))
    const = lambda a: pl.BlockSpec(a.shape, lambda i, pos_r: (0,) * a.ndim, pipeline_mode=pl.Buffered(1))
    return pl.pallas_call(
        _combine_kernel,
        out_shape=jax.ShapeDtypeStruct((t, D_MODEL), F32),
        grid_spec=pltpu.PrefetchScalarGridSpec(
            num_scalar_prefetch=1,
            grid=(t // TM,),
            in_specs=[pl.BlockSpec(memory_space=pl.ANY), row(D_MODEL), row(PLE_DIM),
                      const(gple), const(wpg), const(wpp), const(gfin)],
            out_specs=row(D_MODEL),
            scratch_shapes=[pltpu.VMEM((2, TM * X_ROWS, LANES), F32), pltpu.SemaphoreType.DMA((2,))],
        ),
        compiler_params=pltpu.CompilerParams(dimension_semantics=("arbitrary",), vmem_limit_bytes=VMEM_LIMIT),
        name="combine",
    )(pos, y_sorted, x1, p2, gple, wpg, wpp, gfin)


def _routing_tables(hx, hist8):
    t = hx.shape[0] // REC
    nt = t // TM
    n_tiles = nt + N_BUCKETS
    hist = hist8.reshape(nt, 8, LANES)[:, 0, :]
    counts = jnp.sum(hist, axis=0)
    padded = jnp.ceil(counts / TM) * TM
    ends = jnp.cumsum(padded)
    first = (ends - padded)[None, :] + jnp.cumsum(hist, axis=0) - hist
    slab = hx.reshape(t, REC, LANES)[:, X_ROWS, :]
    lane = jnp.arange(LANES, dtype=F32)[None, :]
    mine = lane == slab[:, RT_BUCKET:RT_BUCKET + 1]
    pos = jnp.sum(jnp.where(mine, jnp.repeat(first, TM, axis=0), 0.0), axis=1) + slab[:, RT_RANK]
    starts = jnp.arange(n_tiles, dtype=F32) * TM
    tile_bucket = jnp.minimum(jnp.sum(ends[None, :N_BUCKETS] <= starts[:, None], axis=1), N_BUCKETS - 1)
    group, pair = tile_bucket // 6, tile_bucket % 6
    e_lo = EXP_PER_GROUP * group + jnp.array([0, 0, 0, 1, 1, 2], jnp.int32)[pair]
    e_hi = EXP_PER_GROUP * group + jnp.array([1, 2, 3, 2, 3, 3], jnp.int32)[pair]
    tile_e = jnp.concatenate([e_lo, e_hi]).astype(jnp.int32)
    n_used = (ends[N_BUCKETS - 1] / TM).astype(jnp.int32).reshape(1)
    tail = jnp.concatenate([(ends - TM)[:N_BUCKETS], padded[:N_BUCKETS], n_used.astype(F32)]).astype(jnp.int32)
    return pos.astype(jnp.int32), tile_e, n_used, tail, n_tiles * TM


def _pack_inproj_weights(w):
    d = w.shape[0]
    qw = NSA_HEADS * NSA_HD
    kvw = NSA_KV * NSA_HD
    o = 0
    wq = w[:, o:o + qw]; o += qw
    wkc = w[:, o:o + kvw]; o += kvw
    wvc = w[:, o:o + kvw]; o += kvw
    wks = w[:, o:o + kvw]; o += kvw
    wvs = w[:, o:o + kvw]; o += kvw
    wkw = w[:, o:o + kvw]; o += kvw
    wvw = w[:, o:o + kvw]; o += kvw
    wga = w[:, o:o + 3 * NSA_HEADS]; o += 3 * NSA_HEADS
    wqkb = w[:, o:o + 2 * ML_WIDTH]; o += 2 * ML_WIDTH
    wvb = w[:, o:o + ML_WIDTH]; o += ML_WIDTH
    wob = w[:, o:o + ML_WIDTH]; o += ML_WIDTH
    wif = w[:, o:o + 2 * ML_HEADS]; o += 2 * ML_HEADS
    wmg = w[:, o:o + 2 * D_MODEL]
    zero64 = jnp.zeros((d, NSA_HD), w.dtype)
    qcols = []
    for h in range(NSA_HEADS):
        wh = wq[:, h * NSA_HD:(h + 1) * NSA_HD]
        qcols += [wh, zero64] if h // NSA_HPG == 0 else [zero64, wh]
    wsm = jnp.concatenate([wga, wif, jnp.zeros((d, LANES - 3 * NSA_HEADS - 2 * ML_HEADS), w.dtype)], axis=1)
    wcat = jnp.concatenate(qcols + [wkc, wvc, wks, wkw, wsm, wqkb, wvb, wob, wmg], axis=1).astype(BF16)
    wtr = jnp.concatenate([wvs, wvw, wsm], axis=1).T.astype(BF16)
    return wcat, wtr


def _rope_lane_tables(positions):
    inv = ROPE_THETA ** (-jnp.arange(0, ROPE_DIM, 2, dtype=F32) / ROPE_DIM)
    ang = positions.astype(F32).reshape(-1, 1) * inv[None, :]
    cos, sin = jnp.cos(ang), jnp.sin(ang)
    half = ROPE_DIM // 2
    d = jnp.arange(LANES) % NSA_HD
    cos_l, sin_l = jnp.tile(cos, (1, LANES // half)), jnp.tile(sin, (1, LANES // half))
    rc = jnp.where(d < ROPE_DIM, cos_l, 1.0)
    rp = jnp.where((d >= half) & (d < ROPE_DIM), sin_l, 0.0)
    rm = jnp.where(d < half, -sin_l, 0.0)
    return rc, rp, rm


def _pack_compress_weights(w1, w2, pe):
    half = CMP_LEN // 2
    w1r = w1.reshape(2, half, NSA_HD, CMP_HIDDEN)
    outs = []
    for part in range(2):
        wb = w1r[part].astype(BF16)
        zb = jnp.zeros_like(wb)
        wp = jnp.stack([jnp.stack([wb, zb], axis=2), jnp.stack([zb, wb], axis=2)], axis=1)
        outs.append(wp.reshape(half * NSA_KV * NSA_HD, NSA_KV * CMP_HIDDEN))
    pe8 = jnp.broadcast_to(pe.reshape(1, CMP_LEN * NSA_HD), (8, CMP_LEN * NSA_HD)).astype(BF16)
    return outs[0], outs[1], pe8, w1.astype(BF16)


def _stages(x, p, positions, g_mix, w_in, b_if, w_ck1, w_ck2, pe_ck, w_cv1, w_cv2, pe_cv, w_conv, b_conv, g_hn, w_pa, w_pb, w_out, g_ffn, w_rg, b_rg, w_re, b_re, w_e13, w_e2, g_ple, w_pg, w_pp, g_final):
    b, s, d = x.shape
    t = b * s
    rc, rp, rm = _rope_lane_tables(positions)
    assert w_in.shape[0] == 1, "the final norm is fused into the layer's last kernel: single-layer problem only"
    for i in range(w_in.shape[0]):
        x2 = x.reshape(t, d)
        wcat, wtr = _pack_inproj_weights(w_in[i])
        (qpad, kc_tok, vc_tok, ks, kw, vst, vwt, sm, smt, qkb, vb, ob, mg) = _inproj(
            x2, g_mix[i].reshape(1, d), wcat, wtr, rc, rp, rm)
        wka, wkb, pek, w1k = _pack_compress_weights(w_ck1[i], w_ck2[i], pe_ck[i])
        wva, wvb, pev, w1v = _pack_compress_weights(w_cv1[i], w_cv2[i], pe_cv[i])
        zpad = jnp.zeros((CMP_HIDDEN, NSA_HD), F32)
        w2k = jnp.stack([jnp.concatenate([w_ck2[i], zpad], axis=1),
                         jnp.concatenate([zpad, w_ck2[i]], axis=1)]).astype(BF16)
        w2vt = w_cv2[i].T.astype(BF16)
        nrow = s // CMP_STRIDE
        rk = kc_tok.reshape(b, nrow, CMP_STRIDE * LANES)
        rv = vc_tok.reshape(b, nrow, CMP_STRIDE * LANES)
        kcb, vct = _compress(rk, rv, wka, wkb, wva, wvb, pek, pev, w1k, w1v, w2k, w2vt)
        ya = _nsa(qpad, kcb, vct, ks, kw, vst, vwt, smt, b, s)
        bif = b_if[i].astype(F32)
        bifc = jnp.zeros((1, LANES), F32).at[0, SM_I:SM_I + 2 * ML_HEADS].set(bif)
        bifr = bifc.reshape(LANES, 1)
        yb = _mlstm(qkb, vb, ob, sm, smt, w_conv[i], b_conv[i].reshape(1, -1), bifc, bifr,
                    g_hn[i].reshape(1, -1), b, s)
        wr = jnp.concatenate([w_re[i], w_rg[i], jnp.zeros((d, LANES - N_EXPERTS - N_GROUPS), F32)], axis=1)
        wr_hi = wr.astype(BF16)
        wr = jnp.concatenate([wr_hi, (wr - wr_hi.astype(F32)).astype(BF16)], axis=1)
        br =jnp.concatenate([b_re[i], b_rg[i], jnp.zeros((LANES - N_EXPERTS - N_GROUPS,), F32)]).reshape(1, LANES)
        x1, hx, hist8 = _merge(ya, yb, mg, x2, w_pa[i].astype(BF16), w_pb[i].astype(BF16), w_out[i].astype(BF16),
                               g_ffn[i].reshape(1, d), wr, br)
        pos, tile_e, n_used, tail, n_slots = _routing_tables(hx, hist8)
        hx_sorted = _dispatch(pos, tail, hx, n_slots)
        y_sorted = _moe(tile_e, n_used, hx_sorted, w_e13[i].astype(BF16), w_e2[i].astype(BF16))
        out = _combine(pos, y_sorted, x1, p[i].reshape(t, PLE_DIM), g_ple[i].reshape(1, d), w_pg[i].astype(BF16),
                       w_pp[i].astype(BF16), g_final.reshape(1, d))
        x = out.reshape(b, s, d)
    return dict(out=x, qpad=qpad, ks=ks, kcb=kcb, vct=vct, y_a=ya, y_b=yb, x1=x1, hx=hx, pos=pos)


def kernel(x, p, positions, g_mix, w_in, b_if, w_ck1, w_ck2, pe_ck, w_cv1, w_cv2, pe_cv, w_conv, b_conv, g_hn, w_pa, w_pb, w_out, g_ffn, w_rg, b_rg, w_re, b_re, w_e13, w_e2, g_ple, w_pg, w_pp, g_final):
    return _stages(x, p, positions, g_mix, w_in, b_if, w_ck1, w_ck2, pe_ck, w_cv1, w_cv2, pe_cv, w_conv, b_conv, g_hn,
                   w_pa, w_pb, w_out, g_ffn, w_rg, b_rg, w_re, b_re, w_e13, w_e2, g_ple, w_pg, w_pp, g_final)["out"]
```

```python
import functools
import math

import jax
import jax.numpy as jnp
from jax import lax
from jax.experimental import pallas as pl
from jax.experimental.pallas import tpu as pltpu

F32 = jnp.float32
BF16 = jnp.bfloat16

EPS = 1e-6
NEG = -1e30

D_MODEL = 1024
PLE_DIM = 256
NSA_HEADS = 8
NSA_KV = 2
NSA_HPG = NSA_HEADS // NSA_KV
NSA_HD = 64
CMP_LEN = 32
CMP_STRIDE = 16
CMP_HIDDEN = 256
SEL_BLOCK = 64
SEL_TOPK = 16
SEL_FORCE = 1000.0
WINDOW = 512
ROPE_THETA = 500000.0
ROPE_DIM = NSA_HD // 4
ML_HEADS = 4
ML_HD = 128
ML_WIDTH = ML_HEADS * ML_HD
CONV_W = 4
N_GROUPS = 4
EXP_PER_GROUP = 4
N_EXPERTS = N_GROUPS * EXP_PER_GROUP
D_EXPERT = 256

LANES = 128
QT = 128
KC = 128
SEL_GROUP = 512
ML_CHUNK = 128
ML_BLOCK = 256
TM = 256
VMEM_LIMIT = 56 * 1024 * 1024

_NT = (((1,), (1,)), ((), ()))
_TN = (((0,), (0,)), ((), ()))

SM_GATE = 0
SM_I = 3 * NSA_HEADS
SM_F = SM_I + ML_HEADS


def _dot(a, b):
    return jnp.dot(a, b, preferred_element_type=F32)


def _dot_nt(a, b):
    return lax.dot_general(a, b, _NT, preferred_element_type=F32)


def _split3(x):
    hi = x.astype(BF16)
    r1 = x - hi.astype(F32)
    mid = r1.astype(BF16)
    lo = (r1 - mid.astype(F32)).astype(BF16)
    return hi, mid, lo


def _rms(x, g):
    return x * lax.rsqrt(jnp.mean(x * x, axis=-1, keepdims=True) + EPS) * g


def _sigmoid(x):
    return 1.0 / (1.0 + jnp.exp(-x))


def _const_spec(shape):
    nd = len(shape)
    return pl.BlockSpec(shape, lambda *_: (0,) * nd, pipeline_mode=pl.Buffered(1))


_C_Q = 0
_C_KC = _C_Q + NSA_HEADS * LANES
_C_VC = _C_KC + LANES
_C_KS = _C_VC + LANES
_C_KW = _C_KS + LANES
_C_SM = _C_KW + LANES
_C_QKB = _C_SM + LANES
_C_VB = _C_QKB + 2 * ML_WIDTH
_C_OB = _C_VB + ML_WIDTH
_C_MG = _C_OB + ML_WIDTH
_C_END = _C_MG + 2 * D_MODEL


def _inproj_kernel(x_ref, g_ref, w_ref, wt_ref, rc_ref, rp_ref, rm_ref,
                   q_ref, kc_ref, vc_ref, ks_ref, kw_ref, vst_ref, vwt_ref, sm_ref, smt_ref,
                   qkb_ref, vb_ref, ob_ref, mg_ref):
    hn = _rms(x_ref[...], g_ref[...]).astype(BF16)
    rc, rp, rm = rc_ref[...], rp_ref[...], rm_ref[...]

    def rope(z):
        return z * rc + pltpu.roll(z, 8, 1) * rp + pltpu.roll(z, LANES - 8, 1) * rm

    scale = NSA_HD ** -0.5 * math.log2(math.e)
    for h in range(NSA_HEADS):
        z = _dot(hn, w_ref[:, _C_Q + h * LANES:_C_Q + (h + 1) * LANES])
        q_ref[:, h * LANES:(h + 1) * LANES] = (rope(z) * scale).astype(BF16)
    kc_ref[...] = rope(_dot(hn, w_ref[:, _C_KC:_C_KC + LANES])).astype(BF16)
    vc_ref[...] = _dot(hn, w_ref[:, _C_VC:_C_VC + LANES]).astype(BF16)
    ks_ref[...] = rope(_dot(hn, w_ref[:, _C_KS:_C_KS + LANES])).astype(BF16)
    kw_ref[...] = rope(_dot(hn, w_ref[:, _C_KW:_C_KW + LANES])).astype(BF16)
    sm_ref[...] = _dot(hn, w_ref[:, _C_SM:_C_SM + LANES])
    for c0 in range(0, 2 * ML_WIDTH, 512):
        qkb_ref[:, c0:c0 + 512] = _dot(hn, w_ref[:, _C_QKB + c0:_C_QKB + c0 + 512]).astype(BF16)
    vb_ref[...] = _dot(hn, w_ref[:, _C_VB:_C_VB + ML_WIDTH]).astype(BF16)
    ob_ref[...] = _dot(hn, w_ref[:, _C_OB:_C_OB + ML_WIDTH]).astype(BF16)
    for c0 in range(0, 2 * D_MODEL, 512):
        mg_ref[:, c0:c0 + 512] = _dot(hn, w_ref[:, _C_MG + c0:_C_MG + c0 + 512]).astype(BF16)
    zt = _dot_nt(wt_ref[...], hn)
    for i in range(TM // KC):
        vst_ref[i] = zt[0:LANES, i * KC:(i + 1) * KC].astype(BF16)
        vwt_ref[i] = zt[LANES:2 * LANES, i * KC:(i + 1) * KC].astype(BF16)
    smt_ref[...] = zt[2 * LANES:3 * LANES, :]


def _inproj(x2, g_mix, wcat, wtr, rc, rp, rm):
    t = x2.shape[0]
    row = lambda w: pl.BlockSpec((TM, w), lambda i: (i, 0))
    out_shape = [
        jax.ShapeDtypeStruct((t, NSA_HEADS * LANES), BF16),
        jax.ShapeDtypeStruct((t, LANES), BF16),
        jax.ShapeDtypeStruct((t, LANES), BF16),
        jax.ShapeDtypeStruct((t, LANES), BF16),
        jax.ShapeDtypeStruct((t, LANES), BF16),
        jax.ShapeDtypeStruct((t // KC, LANES, KC), BF16),
        jax.ShapeDtypeStruct((t // KC, LANES, KC), BF16),
        jax.ShapeDtypeStruct((t, LANES), F32),
        jax.ShapeDtypeStruct((LANES, t), F32),
        jax.ShapeDtypeStruct((t, 2 * ML_WIDTH), BF16),
        jax.ShapeDtypeStruct((t, ML_WIDTH), BF16),
        jax.ShapeDtypeStruct((t, ML_WIDTH), BF16),
        jax.ShapeDtypeStruct((t, 2 * D_MODEL), BF16),
    ]
    chunk3 = pl.BlockSpec((TM // KC, LANES, KC), lambda i: (i, 0, 0))
    out_specs = [row(NSA_HEADS * LANES), row(LANES), row(LANES), row(LANES), row(LANES), chunk3, chunk3,
                 row(LANES), pl.BlockSpec((LANES, TM), lambda i: (0, i)),
                 row(2 * ML_WIDTH), row(ML_WIDTH), row(ML_WIDTH), row(2 * D_MODEL)]
    return pl.pallas_call(
        _inproj_kernel,
        out_shape=out_shape,
        grid=(t // TM,),
        in_specs=[row(D_MODEL), _const_spec((1, D_MODEL)), _const_spec((D_MODEL, _C_END)),
                  _const_spec((3 * LANES, D_MODEL)), row(LANES), row(LANES), row(LANES)],
        out_specs=out_specs,
        compiler_params=pltpu.CompilerParams(dimension_semantics=("arbitrary",), vmem_limit_bytes=VMEM_LIMIT),
        name="inproj",
    )(x2, g_mix, wcat, wtr, rc, rp, rm)


def _gelu_tanh(x):
    return 0.5 * x * (1.0 + jnp.tanh(math.sqrt(2.0 / math.pi) * (x + 0.044715 * x * x * x)))


def _compress_kernel(rk_ref, rv_ref, wka_ref, wkb_ref, wva_ref, wvb_ref, pek_ref, pev_ref,
                     w1k_ref, w1v_ref, w2k_ref, w2vt_ref, kc_ref, vct_ref):
    nrow = rk_ref.shape[0]

    def hidden(r_ref, wa_ref, wb_ref, pe_ref, w1_ref):
        r = r_ref[...]
        ha = _dot(r, wa_ref[...])
        hb = _dot(r, wb_ref[...])
        hb = pltpu.roll(hb, nrow - 1, 0)
        c = _dot(pe_ref[...], w1_ref[...])[0:1, :]
        return [_gelu_tanh(ha[:, g * CMP_HIDDEN:(g + 1) * CMP_HIDDEN] + hb[:, g * CMP_HIDDEN:(g + 1) * CMP_HIDDEN] + c).astype(BF16)
                for g in range(NSA_KV)]

    ak = hidden(rk_ref, wka_ref, wkb_ref, pek_ref, w1k_ref)
    kc_ref[...] = (_dot(ak[0], w2k_ref[0]) + _dot(ak[1], w2k_ref[1])).astype(BF16)
    av = hidden(rv_ref, wva_ref, wvb_ref, pev_ref, w1v_ref)
    for g in range(NSA_KV):
        vct_ref[g * NSA_HD:(g + 1) * NSA_HD, :] = _dot_nt(w2vt_ref[...], av[g]).astype(BF16)


def _compress(rk, rv, wka, wkb, wva, wvb, pek, pev, w1k, w1v, w2k, w2vt):
    b, nrow, width = rk.shape
    blk = pl.BlockSpec((None, nrow, width), lambda i: (i, 0, 0))
    return pl.pallas_call(
        _compress_kernel,
        out_shape=[jax.ShapeDtypeStruct((b, nrow, LANES), BF16),
                   jax.ShapeDtypeStruct((b, LANES, nrow), BF16)],
        grid=(b,),
        in_specs=[blk, blk] + [_const_spec(a.shape) for a in (wka, wkb, wva, wvb, pek, pev, w1k, w1v, w2k, w2vt)],
        out_specs=[pl.BlockSpec((None, nrow, LANES), lambda i: (i, 0, 0)),
                   pl.BlockSpec((None, LANES, nrow), lambda i: (i, 0, 0))],
        compiler_params=pltpu.CompilerParams(dimension_semantics=("arbitrary",), vmem_limit_bytes=VMEM_LIMIT),
        name="compress",
    )(rk, rv, wka, wkb, wva, wvb, pek, pev, w1k, w1v, w2k, w2vt)


def _nsa_kernel(q_ref, kc_ref, vct_ref, ks_ref, kw_ref, vst_ref, vwt_ref, smt_ref, o_ref, bias_scr, sx_scr, sy_scr):
    g = pl.program_id(1)
    c = pl.program_id(2)
    t0 = c * QT
    ncmp = kc_ref.shape[0]
    nsel = bias_scr.shape[0]
    nw = WINDOW // KC + 1
    width = NSA_HPG * QT

    qs = jnp.concatenate([q_ref[:, h * LANES:(h + 1) * LANES] for h in range(NSA_HPG)], axis=0)
    u_row = lax.broadcasted_iota(jnp.int32, (1, width), 1) % QT
    t_row = t0 + u_row
    r_kc = lax.broadcasted_iota(jnp.int32, (KC, 1), 0)

    n_grp = ks_ref.shape[0] // SEL_GROUP
    n_full = lax.shift_right_logical(t0, int(math.log2(SEL_GROUP)))

    def qk_group(j):
        return _dot_nt(ks_ref[pl.ds(pl.multiple_of(j * SEL_GROUP, SEL_GROUP), SEL_GROUP), :], qs)

    sc = _dot_nt(kc_ref[...], qs)

    w_slabs, w_chunks = [], []
    for i in range(nw):
        jj = c - (nw - 1) + i
        jc = jnp.maximum(jj, 0)
        si = _dot_nt(kw_ref[pl.ds(pl.multiple_of(jc * KC, KC), KC), :], qs)
        if i == 0:
            keep = (r_kc > u_row) & (jj >= 0)
        elif i == nw - 1:
            keep = r_kc <= u_row
        else:
            keep = jj >= 0
        w_slabs.append(jnp.where(keep, si, NEG))
        w_chunks.append(jc)

    n_col = lax.broadcasted_iota(jnp.int32, (ncmp, 1), 0)
    cmask = (CMP_STRIDE * n_col + (CMP_LEN - 1) <= t_row) & (n_col < ncmp - 1)
    s = jnp.where(cmask, sc, NEG)
    m = jnp.max(s, axis=0, keepdims=True)
    e = jnp.exp2(s - m)
    anyv = (t_row >= CMP_LEN - 1).astype(F32)
    p = e * (anyv / jnp.sum(e, axis=0, keepdims=True))
    o_cmp = _dot(vct_ref[...], p.astype(BF16))

    psum = p[:, 0:QT]
    for h in range(1, NSA_HPG):
        psum = psum + p[:, h * QT:(h + 1) * QT]
    s_col = lax.broadcasted_iota(jnp.int32, (nsel, 1), 0)
    n_lane = lax.broadcasted_iota(jnp.int32, (1, ncmp), 1)
    ov = ((CMP_STRIDE * n_lane < SEL_BLOCK * (s_col + 1)) & (CMP_STRIDE * n_lane + (CMP_LEN - 1) >= SEL_BLOCK * s_col)
          ).astype(BF16)
    imp = sum(_dot(ov, part) for part in _split3(psum))

    s_diag = qk_group(n_full)
    sx_scr[...] = qk_group(0)

    mxw = w_slabs[0]
    for sl in w_slabs[1:]:
        mxw = jnp.maximum(mxw, sl)
    mw = jnp.max(mxw, axis=0, keepdims=True)
    pws = None
    acc_w = jnp.zeros((NSA_HD, width), F32)
    for sl, jc in zip(w_slabs, w_chunks):
        pw = jnp.exp2(sl - mw)
        pws = pw if pws is None else pws + pw
        acc_w = acc_w + _dot(vwt_ref[jc], pw.astype(BF16))
    o_win = acc_w / jnp.sum(pws, axis=0, keepdims=True)

    t1 = t0 + lax.broadcasted_iota(jnp.int32, (1, QT), 1)
    cur = lax.shift_right_logical(t1, 6)
    forced = (s_col == 0) | (s_col == cur) | (s_col == cur - 1)
    valid = SEL_BLOCK * s_col <= t1
    val = jnp.where(valid, jnp.where(forced, imp + SEL_FORCE, imp), NEG)
    sub = 8
    r_sub = lax.broadcasted_iota(jnp.int32, (sub, 1), 0)
    blocks = [val[r * sub:(r + 1) * sub, :] for r in range(nsel // sub)]
    ranks = [jnp.zeros((sub, QT), F32) for _ in blocks]
    for i in range(nsel):
        vi = val[i:i + 1, :]
        for r, blk in enumerate(blocks):
            if i < r * sub:
                beats = vi >= blk
            elif i >= (r + 1) * sub:
                beats = vi > blk
            else:
                beats = (vi > blk) | ((vi == blk) & (r_sub > i - r * sub))
            ranks[r] = ranks[r] + jnp.where(beats, 1.0, 0.0)
    bias = jnp.where(jnp.concatenate(ranks, axis=0) < SEL_TOPK, 0.0, NEG).astype(F32)
    bias_scr[...] = jnp.concatenate([bias] * NSA_HPG, axis=1)

    r_blk = lax.broadcasted_iota(jnp.int32, (SEL_BLOCK, 1), 0)
    blk_per_grp = SEL_GROUP // SEL_BLOCK
    chunk_per_grp = SEL_GROUP // KC
    blk_per_chunk = KC // SEL_BLOCK

    def sel_mask(sj, j, causal=False, live=None):
        slabs = []
        for i in range(blk_per_grp):
            brow = bias_scr[pl.ds(blk_per_grp * j + i, 1), :]
            if live is not None:
                brow = jnp.where(live, brow, NEG)
            sl = sj[i * SEL_BLOCK:(i + 1) * SEL_BLOCK, :] + brow
            if causal:
                sl = jnp.where(j * SEL_GROUP + i * SEL_BLOCK + r_blk <= t_row, sl, NEG)
            slabs.append(sl)
        return slabs

    def sel_update(j, slabs, carry):
        m_o, l_o, acc = carry
        mx = slabs[0]
        for sl in slabs[1:]:
            mx = jnp.maximum(mx, sl)
        m_n = jnp.maximum(m_o, jnp.max(mx, axis=0, keepdims=True))
        a = jnp.exp2(m_o - m_n)
        acc = a * acc
        psum = None
        for ci in range(chunk_per_grp):
            pj = jnp.exp2(jnp.concatenate(slabs[blk_per_chunk * ci:blk_per_chunk * (ci + 1)], axis=0) - m_n)
            psum = pj if psum is None else psum + pj
            acc = acc + _dot(vst_ref[chunk_per_grp * j + ci], pj.astype(BF16))
        l_n = a * l_o + jnp.sum(psum, axis=0, keepdims=True)
        return m_n, l_n, acc

    empty = (jnp.full((1, width), NEG, F32), jnp.zeros((1, width), F32), jnp.zeros((NSA_HD, width), F32))
    seeded = sel_update(n_full, sel_mask(s_diag, n_full, causal=True), empty)

    def pair_body(jp, carry):
        ja, jb = 2 * jp, 2 * jp + 1
        sy_scr[...] = qk_group(jb)
        carry = sel_update(ja, sel_mask(sx_scr, ja), carry)
        sx_scr[...] = qk_group(jnp.minimum(ja + 2, n_grp - 1))
        return sel_update(jb, sel_mask(sy_scr, jb, live=jb < n_full), carry)

    _, l_s, acc_s = lax.fori_loop(0, lax.shift_right_logical(n_full + 1, 1), pair_body, seeded)
    o_sel = acc_s / l_s

    def gate_row(br):
        rows = [smt_ref[pl.ds(SM_GATE + 3 * (NSA_HPG * g + h) + br, 1), :] for h in range(NSA_HPG)]
        return _sigmoid(jnp.concatenate(rows, axis=1))

    o_t = gate_row(0) * o_cmp + gate_row(1) * o_sel + gate_row(2) * o_win
    for pr in range(NSA_HPG // 2):
        xp = jnp.concatenate([o_t[:, (2 * pr) * QT:(2 * pr + 1) * QT], o_t[:, (2 * pr + 1) * QT:(2 * pr + 2) * QT]], axis=0)
        o_ref[:, pr * LANES:(pr + 1) * LANES] = xp.T.astype(BF16)


def _nsa(qpad, kcb, vct, ks, kw, vst, vwt, smt, b, s):
    nq = s // QT
    ncmp = kcb.shape[1]
    return pl.pallas_call(
        _nsa_kernel,
        out_shape=jax.ShapeDtypeStruct((b * s, NSA_HEADS * NSA_HD), BF16),
        grid=(b, NSA_KV, nq),
        in_specs=[
            pl.BlockSpec((QT, NSA_HPG * LANES), lambda bi, g, c: (bi * nq + c, g)),
            pl.BlockSpec((None, ncmp, LANES), lambda bi, g, c: (bi, 0, 0)),
            pl.BlockSpec((None, NSA_HD, ncmp), lambda bi, g, c: (bi, g, 0)),
            pl.BlockSpec((s, LANES), lambda bi, g, c: (bi, 0)),
            pl.BlockSpec((s, LANES), lambda bi, g, c: (bi, 0)),
            pl.BlockSpec((s // KC, NSA_HD, KC), lambda bi, g, c: (bi, g, 0)),
            pl.BlockSpec((s // KC, NSA_HD, KC), lambda bi, g, c: (bi, g, 0)),
            pl.BlockSpec((LANES, QT), lambda bi, g, c: (0, bi * nq + c)),
        ],
        out_specs=pl.BlockSpec((QT, NSA_HPG * NSA_HD), lambda bi, g, c: (bi * nq + c, g)),
        scratch_shapes=[pltpu.VMEM((s // SEL_BLOCK, NSA_HPG * QT), F32),
                        pltpu.VMEM((SEL_GROUP, NSA_HPG * QT), F32),
                        pltpu.VMEM((SEL_GROUP, NSA_HPG * QT), F32)],
        compiler_params=pltpu.CompilerParams(dimension_semantics=("arbitrary", "arbitrary", "arbitrary"),
                                             vmem_limit_bytes=VMEM_LIMIT),
        name="nsa",
    )(qpad, kcb, vct, ks, kw, vst, vwt, smt)


def _log_sigmoid(x):
    return jnp.minimum(x, 0.0) - jnp.log(1.0 + jnp.exp(-jnp.abs(x)))


def _mlstm_kernel(qk_ref, v_ref, og_ref, sm_ref, smt_ref, wc_ref, bc_ref, bifc_ref, bifr_ref, ghn_ref,
                  y_ref, ext_scr, ct_scr, n_scr, m_scr):
    lc = ML_CHUNK

    @pl.when(pl.program_id(1) == 0)
    def _():
        ext_scr[0:8, :] = jnp.zeros((8, 2 * ML_WIDTH), F32)
        ct_scr[...] = jnp.zeros_like(ct_scr)
        n_scr[...] = jnp.zeros_like(n_scr)
        m_scr[...] = jnp.zeros_like(m_scr)

    ext_scr[8:8 + ML_BLOCK, :] = qk_ref[...].astype(F32)
    y = bc_ref[...]
    for j in range(CONV_W):
        y = y + wc_ref[j:j + 1, :] * ext_scr[pl.ds(8 - (CONV_W - 1) + j, ML_BLOCK), :]
    ext_scr[0:8, :] = ext_scr[ML_BLOCK:ML_BLOCK + 8, :]
    qkc = y * _sigmoid(y)
    q_all = qkc[:, 0:ML_WIDTH].astype(BF16)
    k_all = (qkc[:, ML_WIDTH:2 * ML_WIDTH] * (ML_HD ** -0.5)).astype(BF16)

    ifc = sm_ref[...] + bifc_ref[...]
    ifr = smt_ref[...] + bifr_ref[...]
    lfc = _log_sigmoid(ifc)
    lfr = _log_sigmoid(ifr)
    rr = lax.broadcasted_iota(jnp.int32, (lc, lc), 0)
    cc = lax.broadcasted_iota(jnp.int32, (lc, lc), 1)
    causal = rr >= cc
    tri_l = causal.astype(F32)
    tri_u = (rr <= cc).astype(F32)

    for ci in range(ML_BLOCK // lc):
        lo, hi = ci * lc, (ci + 1) * lc
        bc_all = jnp.dot(tri_l, lfc[lo:hi, :], preferred_element_type=F32, precision=lax.Precision.HIGHEST)
        br_all = jnp.dot(lfr[:, lo:hi], tri_u, preferred_element_type=F32, precision=lax.Precision.HIGHEST)
        for h in range(ML_HEADS):
            hs = slice(h * ML_HD, (h + 1) * ML_HD)
            bcol = bc_all[:, SM_F + h:SM_F + h + 1]
            brow = br_all[SM_F + h:SM_F + h + 1, :]
            icol = ifc[lo:hi, SM_I + h:SM_I + h + 1]
            irow = ifr[SM_I + h:SM_I + h + 1, lo:hi]
            mprev = m_scr[h][:, 0:1]
            qh = q_all[lo:hi, hs]
            kh = k_all[lo:hi, hs]
            vh = v_ref[lo:hi, hs]
            dmat = jnp.where(causal, bcol - brow + irow, NEG)
            inter = bcol + mprev
            mt = jnp.maximum(jnp.max(dmat, axis=-1, keepdims=True), inter)
            a = jnp.exp(dmat - mt) * _dot_nt(qh, kh)
            dec = jnp.exp(inter - mt)
            ct = ct_scr[h]
            nrow = n_scr[h]
            num = _dot(a.astype(BF16), vh) + dec * _dot(qh, ct.astype(BF16))
            den = jnp.sum(a, axis=-1, keepdims=True) + dec * jnp.sum(qh.astype(F32) * nrow, axis=-1, keepdims=True)
            hh = num / jnp.maximum(jnp.abs(den), jnp.exp(-mt))
            blast = bcol[lc - 1:lc, :]
            grow = blast - brow + irow
            mnew = jnp.maximum(blast + mprev, jnp.max(grow, axis=-1, keepdims=True))
            wprev = jnp.exp(blast + mprev - mnew)
            kwt = kh.astype(F32) * jnp.exp(blast - bcol + icol - mnew)
            ct_scr[h] = wprev * ct + lax.dot_general(kwt.astype(BF16), vh, _TN, preferred_element_type=F32)
            n_scr[h] = wprev * nrow + jnp.sum(kwt, axis=0, keepdims=True)
            m_scr[h] = jnp.broadcast_to(mnew, (1, LANES))
            hm = hh * _sigmoid(og_ref[lo:hi, hs].astype(F32))
            y_ref[lo:hi, hs] = _rms(hm, ghn_ref[:, hs]).astype(BF16)


def _mlstm(qkb, vb, ob, sm, smt, wconv, bconv, bifc, bifr, ghn, b, s):
    nb = s // ML_BLOCK
    row = lambda w: pl.BlockSpec((ML_BLOCK, w), lambda bi, j: (bi * nb + j, 0))
    return pl.pallas_call(
        _mlstm_kernel,
        out_shape=jax.ShapeDtypeStruct((b * s, ML_WIDTH), BF16),
        grid=(b, nb),
        in_specs=[row(2 * ML_WIDTH), row(ML_WIDTH), row(ML_WIDTH), row(LANES),
                  pl.BlockSpec((LANES, ML_BLOCK), lambda bi, j: (0, bi * nb + j)),
                  _const_spec(wconv.shape), _const_spec(bconv.shape), _const_spec(bifc.shape),
                  _const_spec(bifr.shape), _const_spec(ghn.shape)],
        out_specs=row(ML_WIDTH),
        scratch_shapes=[pltpu.VMEM((ML_BLOCK + 8, 2 * ML_WIDTH), F32),
                        pltpu.VMEM((ML_HEADS, ML_HD, ML_HD), F32),
                        pltpu.VMEM((ML_HEADS, 1, ML_HD), F32),
                        pltpu.VMEM((ML_HEADS, 1, LANES), F32)],
        compiler_params=pltpu.CompilerParams(dimension_semantics=("arbitrary", "arbitrary"),
                                             vmem_limit_bytes=VMEM_LIMIT),
        name="mlstm",
    )(qkb, vb, ob, sm, smt, wconv, bconv, bifc, bifr, ghn)


RT_BUCKET = N_EXPERTS
RT_RANK = N_EXPERTS + 1
RT_WLO = N_EXPERTS + 2
RT_WHI = N_EXPERTS + 3
N_BUCKETS = N_GROUPS * 6
X_ROWS = D_MODEL // LANES
REC = 8
H_ROWS = D_MODEL // 2 // LANES


def _merge_kernel(ya_ref, yb_ref, mg_ref, x_ref, wpa_ref, wpb_ref, wout_ref, gffn_ref, wr_ref, br_ref,
                  x1_ref, hx_ref, slab_ref, hist_ref, order_ref):
    pa = _dot(ya_ref[...], wpa_ref[...])
    pb = _dot(yb_ref[...], wpb_ref[...])
    ga = _sigmoid(mg_ref[:, 0:D_MODEL].astype(F32))
    gb = _sigmoid(mg_ref[:, D_MODEL:2 * D_MODEL].astype(F32))
    mixed = (ga * pa + gb * pb).astype(BF16)
    x1 = x_ref[...] + _dot(mixed, wout_ref[...])
    x1_ref[...] = x1
    h2 = _rms(x1, gffn_ref[...])
    h_hi = h2.astype(BF16)
    hbits = pltpu.bitcast(h_hi.astype(F32), jnp.uint32)
    words = (hbits[:, 0:D_MODEL // 2] >> 16) | hbits[:, D_MODEL // 2:D_MODEL]
    for j in range(H_ROWS):
        hx_ref[pl.ds(j, TM, stride=REC), :] = words[:, j * LANES:(j + 1) * LANES]

    h_lo = (h2 - h_hi.astype(F32)).astype(BF16)
    r_hi = _dot(h_hi, wr_ref[...])
    logit = r_hi[:, 0:LANES] + r_hi[:, LANES:2 * LANES] + _dot(h_lo, wr_ref[:, 0:LANES]) + br_ref[...]
    lane = lax.broadcasted_iota(jnp.int32, logit.shape, 1)
    big = jnp.int32(LANES)
    gmask = (lane >= N_EXPERTS) & (lane < N_EXPERTS + N_GROUPS)
    gl = jnp.where(gmask, logit, NEG)
    gmax = jnp.max(gl, axis=-1, keepdims=True)
    gidx = jnp.min(jnp.where(gmask & (gl == gmax), lane, big), axis=-1, keepdims=True) - N_EXPERTS
    pg_sel = 1.0 / jnp.sum(jnp.where(gmask, jnp.exp(gl - gmax), 0.0), axis=-1, keepdims=True)
    emask = (lane < N_EXPERTS) & (lax.shift_right_logical(lane, 2) == gidx)
    el = jnp.where(emask, logit, NEG)
    e1 = jnp.max(el, axis=-1, keepdims=True)
    i1 = jnp.min(jnp.where(emask & (el == e1), lane, big), axis=-1, keepdims=True)
    emask2 = emask & (lane != i1)
    el2 = jnp.where(emask2, logit, NEG)
    e2 = jnp.max(el2, axis=-1, keepdims=True)
    i2 = jnp.min(jnp.where(emask2 & (el2 == e2), lane, big), axis=-1, keepdims=True)
    x21 = jnp.exp(e2 - e1)
    w1 = pg_sel / (1.0 + x21)
    w2 = pg_sel * x21 / (1.0 + x21)
    first_lo = i1 < i2
    e_lo = jnp.where(first_lo, i1, i2) - EXP_PER_GROUP * gidx
    e_hi = jnp.where(first_lo, i2, i1) - EXP_PER_GROUP * gidx
    pair = lax.shift_right_logical(e_lo * (2 * EXP_PER_GROUP - 1 - e_lo), 1) + (e_hi - e_lo - 1)
    bucket = 6 * gidx + pair
    member = lane == bucket
    onehot = jnp.where(member, 1.0, 0.0)
    rr = lax.broadcasted_iota(jnp.int32, (TM, TM), 0)
    cc = lax.broadcasted_iota(jnp.int32, (TM, TM), 1)
    earlier = _dot((rr > cc).astype(BF16), onehot.astype(BF16))
    rank = jnp.sum(jnp.where(member, earlier, 0.0), axis=-1, keepdims=True)
    slab = jnp.where(lane == i1, w1, 0.0) + jnp.where(lane == i2, w2, 0.0)
    slab = jnp.where(lane == RT_BUCKET, bucket.astype(F32), slab)
    slab = jnp.where(lane == RT_RANK, rank, slab)
    slab = jnp.where(lane == RT_WLO, jnp.where(first_lo, w1, w2), slab)
    slab = jnp.where(lane == RT_WHI, jnp.where(first_lo, w2, w1), slab)
    slab_ref[...] = slab
    hx_ref[pl.ds(H_ROWS, TM, stride=REC), :] = pltpu.bitcast(slab, jnp.uint32)
    for j in range(H_ROWS + 1, REC):
        hx_ref[pl.ds(j, TM, stride=REC), :] = jnp.zeros((TM, LANES), jnp.uint32)
    hist = jnp.sum(onehot, axis=0, keepdims=True)
    hist_ref[...] = jnp.broadcast_to(hist, (8, LANES))
    kk = lax.broadcasted_iota(jnp.int32, (LANES, LANES), 0)
    ll = lax.broadcasted_iota(jnp.int32, (LANES, LANES), 1)
    before = _dot(jnp.broadcast_to(hist, (8, LANES)).astype(BF16), (kk < ll).astype(BF16))[0:1, :]
    lpos = jnp.sum(jnp.where(member, before, 0.0), axis=-1, keepdims=True) + rank
    perm = (lpos == lax.broadcasted_iota(jnp.int32, (1, TM), 1).astype(F32)).astype(BF16)
    order_ref[...] = _dot(lax.broadcasted_iota(jnp.int32, (8, TM), 1).astype(BF16), perm)


def _merge(ya, yb, mg, x2, wpa, wpb, wout, gffn, wr, br):
    t = x2.shape[0]
    row = lambda w: pl.BlockSpec((TM, w), lambda i: (i, 0))
    return pl.pallas_call(
        _merge_kernel,
        out_shape=[jax.ShapeDtypeStruct((t, D_MODEL), F32),
                   jax.ShapeDtypeStruct((t * REC, LANES), jnp.uint32),
                   jax.ShapeDtypeStruct((t, LANES), F32),
                   jax.ShapeDtypeStruct((t // TM * 8, LANES), F32),
                   jax.ShapeDtypeStruct((t // TM * 8, TM), F32)],
        grid=(t // TM,),
        in_specs=[row(NSA_HEADS * NSA_HD), row(ML_WIDTH), row(2 * D_MODEL), row(D_MODEL)]
                 + [_const_spec(a.shape) for a in (wpa, wpb, wout, gffn, wr, br)],
        out_specs=[row(D_MODEL), pl.BlockSpec((TM * REC, LANES), lambda i: (i, 0)), row(LANES),
                   pl.BlockSpec((8, LANES), lambda i: (i, 0)), pl.BlockSpec((8, TM), lambda i: (i, 0))],
        compiler_params=pltpu.CompilerParams(dimension_semantics=("arbitrary",), vmem_limit_bytes=VMEM_LIMIT),
        name="merge",
    )(ya, yb, mg, x2, wpa, wpb, wout, gffn, wr, br)


def _rec_copy(src_ref, src_tok, dst_ref, dst_tok, sem, rows):
    src = src_ref.at[pl.ds(pl.multiple_of(src_tok * rows, rows), rows), :]
    dst = dst_ref.at[pl.ds(pl.multiple_of(dst_tok * rows, rows), rows), :]
    return pltpu.make_async_copy(src, dst, sem)


def _moe_kernel(te_ref, nu_ref, src_ref, srcn_ref, hx_ref, w13_ref, w2_ref, y_ref, buf, sem):
    k = pl.program_id(0)
    n_tiles = pl.num_programs(0)
    n_used = nu_ref[0]
    slot = lax.rem(k, 2)

    def gather(idx_ref, which):
        def body(r, carry):
            _rec_copy(hx_ref, idx_ref[0, r], buf.at[which], r, sem.at[which], REC).start()
            return carry
        lax.fori_loop(0, TM, body, 0, unroll=8)

    @pl.when(k == 0)
    def _():
        gather(src_ref, 0)

    @pl.when(k + 1 < n_used)
    def _():
        gather(srcn_ref, 1 - slot)

    @pl.when(k < n_used)
    def _():
        def drain(r, carry):
            _rec_copy(hx_ref, 0, buf.at[slot], 0, sem.at[slot], REC).wait()
            return carry
        lax.fori_loop(0, TM, drain, 0, unroll=8)
        rec = buf.at[slot]
        words = [rec[pl.ds(j, TM, stride=REC), :] for j in range(H_ROWS)]
        lo = [pltpu.bitcast(w << 16, F32).astype(BF16) for w in words]
        hi = [pltpu.bitcast(w & jnp.uint32(0xFFFF0000), F32).astype(BF16) for w in words]
        h = jnp.concatenate(lo + hi, axis=1)
        slab = pltpu.bitcast(rec[pl.ds(H_ROWS, TM, stride=REC), :], F32)
        y = None
        for side, lane in ((0, RT_WLO), (1, RT_WHI)):
            e = te_ref[side * n_tiles + k]
            a = _dot(h, w13_ref[e])
            gt = a[:, 0:D_EXPERT]
            act = gt * _sigmoid(gt) * a[:, D_EXPERT:2 * D_EXPERT] * slab[:, lane:lane + 1]
            part = _dot(act.astype(BF16), w2_ref[e])
            y = part if y is None else y + part
        for j in range(X_ROWS):
            y_ref[pl.ds(j, TM, stride=X_ROWS), :] = y[:, j * LANES:(j + 1) * LANES]

    @pl.when(k >= n_used)
    def _():
        y_ref[...] = jnp.zeros_like(y_ref)


def _moe(tile_e, n_used, src, hx, w13, w2):
    n_tiles = src.shape[0]
    src_spec = lambda f: pl.BlockSpec((None, 1, TM), f, memory_space=pltpu.SMEM)
    return pl.pallas_call(
        _moe_kernel,
        out_shape=jax.ShapeDtypeStruct((n_tiles * TM * X_ROWS, LANES), F32),
        grid_spec=pltpu.PrefetchScalarGridSpec(
            num_scalar_prefetch=2,
            grid=(n_tiles,),
            in_specs=[src_spec(lambda k, te, nu: (k, 0, 0)),
                      src_spec(lambda k, te, nu: (jnp.minimum(k + 1, n_tiles - 1), 0, 0)),
                      pl.BlockSpec(memory_space=pl.ANY),
                      pl.BlockSpec(w13.shape, lambda k, te, nu: (0, 0, 0), pipeline_mode=pl.Buffered(1)),
                      pl.BlockSpec(w2.shape, lambda k, te, nu: (0, 0, 0), pipeline_mode=pl.Buffered(1))],
            out_specs=pl.BlockSpec((TM * X_ROWS, LANES), lambda k, te, nu: (k, 0)),
            scratch_shapes=[pltpu.VMEM((2, TM * REC, LANES), jnp.uint32), pltpu.SemaphoreType.DMA((2,))],
        ),
        compiler_params=pltpu.CompilerParams(dimension_semantics=("arbitrary",), vmem_limit_bytes=VMEM_LIMIT),
        name="moe",
    )(tile_e, n_used, src, src, hx, w13, w2)


def _combine_kernel(pos_ref, y_ref, x1_ref, p_ref, gple_ref, wpg_ref, wpp_ref, gfin_ref, o_ref, ybuf, sem):
    i = pl.program_id(0)
    slot = lax.rem(i, 2)

    def gather(tile, which):
        def body(r, carry):
            _rec_copy(y_ref, pos_ref[tile * TM + r], ybuf.at[which], r, sem.at[which], X_ROWS).start()
            return carry
        lax.fori_loop(0, TM, body, 0, unroll=8)

    @pl.when(i == 0)
    def _():
        gather(0, 0)

    @pl.when(i + 1 < pl.num_programs(0))
    def _():
        gather(i + 1, 1 - slot)

    def drain(r, carry):
        _rec_copy(y_ref, 0, ybuf.at[slot], 0, sem.at[slot], X_ROWS).wait()
        return carry

    lax.fori_loop(0, TM, drain, 0, unroll=8)
    yb = ybuf.at[slot]
    x2 = x1_ref[...] + jnp.concatenate([yb[pl.ds(j, TM, stride=X_ROWS), :] for j in range(X_ROWS)], axis=1)
    h3 = _rms(x2, gple_ref[...]).astype(BF16)
    x3 = x2 + _sigmoid(_dot(h3, wpg_ref[...])) * _dot(p_ref[...].astype(BF16), wpp_ref[...])
    o_ref[...] = _rms(x3, gfin_ref[...])


def _combine(pos, y_sorted, x1, p2, gple, wpg, wpp, gfin):
    t = x1.shape[0]
    row = lambda w: pl.BlockSpec((TM, w), lambda i, pos_r: (i, 0))
    const = lambda a: pl.BlockSpec(a.shape, lambda i, pos_r: (0,) * a.ndim, pipeline_mode=pl.Buffered(1))
    return pl.pallas_call(
        _combine_kernel,
        out_shape=jax.ShapeDtypeStruct((t, D_MODEL), F32),
        grid_spec=pltpu.PrefetchScalarGridSpec(
            num_scalar_prefetch=1,
            grid=(t // TM,),
            in_specs=[pl.BlockSpec(memory_space=pl.ANY), row(D_MODEL), row(PLE_DIM),
                      const(gple), const(wpg), const(wpp), const(gfin)],
            out_specs=row(D_MODEL),
            scratch_shapes=[pltpu.VMEM((2, TM * X_ROWS, LANES), F32), pltpu.SemaphoreType.DMA((2,))],
        ),
        compiler_params=pltpu.CompilerParams(dimension_semantics=("arbitrary",), vmem_limit_bytes=VMEM_LIMIT),
        name="combine",
    )(pos, y_sorted, x1, p2, gple, wpg, wpp, gfin)


def _routing_tables(slab, hist8, order8):
    t = slab.shape[0]
    nt = t // TM
    n_tiles = nt + N_BUCKETS
    hist = hist8.reshape(nt, 8, LANES)[:, 0, :].astype(jnp.int32)
    order = order8.reshape(nt, 8, TM)[:, 0, :].astype(jnp.int32)
    counts = jnp.sum(hist, axis=0)
    padded = (counts + TM - 1) // TM * TM
    ends = jnp.cumsum(padded)
    base = ends - padded
    seen = jnp.cumsum(hist, axis=0) - hist
    local = jnp.cumsum(hist, axis=1) - hist
    bucket = slab[:, RT_BUCKET].astype(jnp.int32)
    tile_of = jnp.arange(t, dtype=jnp.int32) // TM
    pos = (base[None, :] + seen)[tile_of, bucket] + slab[:, RT_RANK].astype(jnp.int32)
    starts = jnp.arange(n_tiles, dtype=jnp.int32) * TM
    tile_bucket = jnp.minimum(jnp.sum(ends[None, :N_BUCKETS] <= starts[:, None], axis=1), N_BUCKETS - 1)
    group, pair = tile_bucket // 6, tile_bucket % 6
    e_lo = EXP_PER_GROUP * group + jnp.array([0, 0, 0, 1, 1, 2], jnp.int32)[pair]
    e_hi = EXP_PER_GROUP * group + jnp.array([1, 2, 3, 2, 3, 3], jnp.int32)[pair]
    tile_e = jnp.concatenate([e_lo, e_hi]).astype(jnp.int32)
    n_used = (ends[N_BUCKETS - 1] // TM).astype(jnp.int32).reshape(1)
    slot_bucket = jnp.repeat(tile_bucket, TM)
    o = jnp.arange(n_tiles * TM, dtype=jnp.int32) - base[slot_bucket]
    done = (seen + hist).T[slot_bucket]
    ti = jnp.minimum(jnp.sum(done <= o[:, None], axis=1), nt - 1)
    src = ti * TM + order[ti, jnp.clip(local[ti, slot_bucket] + o - seen[ti, slot_bucket], 0, TM - 1)]
    src = jnp.where(o < counts[slot_bucket], src, 0)
    return pos.astype(jnp.int32), src.astype(jnp.int32).reshape(n_tiles, 1, TM), tile_e, n_used


def _pack_inproj_weights(w):
    d = w.shape[0]
    qw = NSA_HEADS * NSA_HD
    kvw = NSA_KV * NSA_HD
    o = 0
    wq = w[:, o:o + qw]; o += qw
    wkc = w[:, o:o + kvw]; o += kvw
    wvc = w[:, o:o + kvw]; o += kvw
    wks = w[:, o:o + kvw]; o += kvw
    wvs = w[:, o:o + kvw]; o += kvw
    wkw = w[:, o:o + kvw]; o += kvw
    wvw = w[:, o:o + kvw]; o += kvw
    wga = w[:, o:o + 3 * NSA_HEADS]; o += 3 * NSA_HEADS
    wqkb = w[:, o:o + 2 * ML_WIDTH]; o += 2 * ML_WIDTH
    wvb = w[:, o:o + ML_WIDTH]; o += ML_WIDTH
    wob = w[:, o:o + ML_WIDTH]; o += ML_WIDTH
    wif = w[:, o:o + 2 * ML_HEADS]; o += 2 * ML_HEADS
    wmg = w[:, o:o + 2 * D_MODEL]
    zero64 = jnp.zeros((d, NSA_HD), w.dtype)
    qcols = []
    for h in range(NSA_HEADS):
        wh = wq[:, h * NSA_HD:(h + 1) * NSA_HD]
        qcols += [wh, zero64] if h // NSA_HPG == 0 else [zero64, wh]
    wsm = jnp.concatenate([wga, wif, jnp.zeros((d, LANES - 3 * NSA_HEADS - 2 * ML_HEADS), w.dtype)], axis=1)
    wcat = jnp.concatenate(qcols + [wkc, wvc, wks, wkw, wsm, wqkb, wvb, wob, wmg], axis=1).astype(BF16)
    wtr = jnp.concatenate([wvs, wvw, wsm], axis=1).T.astype(BF16)
    return wcat, wtr


def _rope_lane_tables(positions):
    inv = ROPE_THETA ** (-jnp.arange(0, ROPE_DIM, 2, dtype=F32) / ROPE_DIM)
    ang = positions.astype(F32).reshape(-1, 1) * inv[None, :]
    cos, sin = jnp.cos(ang), jnp.sin(ang)
    half = ROPE_DIM // 2
    d = jnp.arange(LANES) % NSA_HD
    cos_l, sin_l = jnp.tile(cos, (1, LANES // half)), jnp.tile(sin, (1, LANES // half))
    rc = jnp.where(d < ROPE_DIM, cos_l, 1.0)
    rp = jnp.where((d >= half) & (d < ROPE_DIM), sin_l, 0.0)
    rm = jnp.where(d < half, -sin_l, 0.0)
    return rc, rp, rm


def _pack_compress_weights(w1, w2, pe):
    half = CMP_LEN // 2
    w1r = w1.reshape(2, half, NSA_HD, CMP_HIDDEN)
    outs = []
    for part in range(2):
        wb = w1r[part].astype(BF16)
        zb = jnp.zeros_like(wb)
        wp = jnp.stack([jnp.stack([wb, zb], axis=2), jnp.stack([zb, wb], axis=2)], axis=1)
        outs.append(wp.reshape(half * NSA_KV * NSA_HD, NSA_KV * CMP_HIDDEN))
    pe8 = jnp.broadcast_to(pe.reshape(1, CMP_LEN * NSA_HD), (8, CMP_LEN * NSA_HD)).astype(BF16)
    return outs[0], outs[1], pe8, w1.astype(BF16)


def _stages(x, p, positions, g_mix, w_in, b_if, w_ck1, w_ck2, pe_ck, w_cv1, w_cv2, pe_cv, w_conv, b_conv, g_hn, w_pa, w_pb, w_out, g_ffn, w_rg, b_rg, w_re, b_re, w_e13, w_e2, g_ple, w_pg, w_pp, g_final):
    b, s, d = x.shape
    t = b * s
    rc, rp, rm = _rope_lane_tables(positions)
    assert w_in.shape[0] == 1, "the final norm is fused into the layer's last kernel: single-layer problem only"
    for i in range(w_in.shape[0]):
        x2 = x.reshape(t, d)
        wcat, wtr = _pack_inproj_weights(w_in[i])
        (qpad, kc_tok, vc_tok, ks, kw, vst, vwt, sm, smt, qkb, vb, ob, mg) = _inproj(
            x2, g_mix[i].reshape(1, d), wcat, wtr, rc, rp, rm)
        wka, wkb, pek, w1k = _pack_compress_weights(w_ck1[i], w_ck2[i], pe_ck[i])
        wva, wvb, pev, w1v = _pack_compress_weights(w_cv1[i], w_cv2[i], pe_cv[i])
        zpad = jnp.zeros((CMP_HIDDEN, NSA_HD), F32)
        w2k = jnp.stack([jnp.concatenate([w_ck2[i], zpad], axis=1),
                         jnp.concatenate([zpad, w_ck2[i]], axis=1)]).astype(BF16)
        w2vt = w_cv2[i].T.astype(BF16)
        nrow = s // CMP_STRIDE
        rk = kc_tok.reshape(b, nrow, CMP_STRIDE * LANES)
        rv = vc_tok.reshape(b, nrow, CMP_STRIDE * LANES)
        kcb, vct = _compress(rk, rv, wka, wkb, wva, wvb, pek, pev, w1k, w1v, w2k, w2vt)
        ya = _nsa(qpad, kcb, vct, ks, kw, vst, vwt, smt, b, s)
        bif = b_if[i].astype(F32)
        bifc = jnp.zeros((1, LANES), F32).at[0, SM_I:SM_I + 2 * ML_HEADS].set(bif)
        bifr = bifc.reshape(LANES, 1)
        yb = _mlstm(qkb, vb, ob, sm, smt, w_conv[i], b_conv[i].reshape(1, -1), bifc, bifr,
                    g_hn[i].reshape(1, -1), b, s)
        wr = jnp.concatenate([w_re[i], w_rg[i], jnp.zeros((d, LANES - N_EXPERTS - N_GROUPS), F32)], axis=1)
        wr_hi = wr.astype(BF16)
        wr = jnp.concatenate([wr_hi, (wr - wr_hi.astype(F32)).astype(BF16)], axis=1)
        br =jnp.concatenate([b_re[i], b_rg[i], jnp.zeros((LANES - N_EXPERTS - N_GROUPS,), F32)]).reshape(1, LANES)
        x1, hx, slab, hist8, order8 = _merge(ya, yb, mg, x2, w_pa[i].astype(BF16), w_pb[i].astype(BF16),
                                             w_out[i].astype(BF16), g_ffn[i].reshape(1, d), wr, br)
        pos, src, tile_e, n_used = _routing_tables(slab, hist8, order8)
        y_sorted = _moe(tile_e, n_used, src, hx, w_e13[i].astype(BF16), w_e2[i].astype(BF16))
        out = _combine(pos, y_sorted, x1, p[i].reshape(t, PLE_DIM), g_ple[i].reshape(1, d), w_pg[i].astype(BF16),
                       w_pp[i].astype(BF16), g_final.reshape(1, d))
        x = out.reshape(b, s, d)
    return dict(out=x, qpad=qpad, ks=ks, kcb=kcb, vct=vct, y_a=ya, y_b=yb, x1=x1, hx=hx, pos=pos)


def kernel(x, p, positions, g_mix, w_in, b_if, w_ck1, w_ck2, pe_ck, w_cv1, w_cv2, pe_cv, w_conv, b_conv, g_hn, w_pa, w_pb, w_out, g_ffn, w_rg, b_rg, w_re, b_re, w_e13, w_e2, g_ple, w_pg, w_pp, g_final):
    return _stages(x, p, positions, g_mix, w_in, b_if, w_ck1, w_ck2, pe_ck, w_cv1, w_cv2, pe_cv, w_conv, b_conv, g_hn,
                   w_pa, w_pb, w_out, g_ffn, w_rg, b_rg, w_re, b_re, w_e13, w_e2, g_ple, w_pg, w_pp, g_final)["out"]
```

```python
import functools
import math

import jax
import jax.numpy as jnp
from jax import lax
from jax.experimental import pallas as pl
from jax.experimental.pallas import tpu as pltpu

F32 = jnp.float32
BF16 = jnp.bfloat16

EPS = 1e-6
NEG = -1e30

D_MODEL = 1024
PLE_DIM = 256
NSA_HEADS = 8
NSA_KV = 2
NSA_HPG = NSA_HEADS // NSA_KV
NSA_HD = 64
CMP_LEN = 32
CMP_STRIDE = 16
CMP_HIDDEN = 256
SEL_BLOCK = 64
SEL_TOPK = 16
SEL_FORCE = 1000.0
WINDOW = 512
ROPE_THETA = 500000.0
ROPE_DIM = NSA_HD // 4
ML_HEADS = 4
ML_HD = 128
ML_WIDTH = ML_HEADS * ML_HD
CONV_W = 4
N_GROUPS = 4
EXP_PER_GROUP = 4
N_EXPERTS = N_GROUPS * EXP_PER_GROUP
D_EXPERT = 256

LANES = 128
QT = 128
KC = 128
SEL_GROUP = 512
ML_CHUNK = 128
ML_BLOCK = 256
TM = 256
VMEM_LIMIT = 56 * 1024 * 1024

_NT = (((1,), (1,)), ((), ()))
_TN = (((0,), (0,)), ((), ()))

SM_GATE = 0
SM_I = 3 * NSA_HEADS
SM_F = SM_I + ML_HEADS


def _dot(a, b):
    return jnp.dot(a, b, preferred_element_type=F32)


def _dot_nt(a, b):
    return lax.dot_general(a, b, _NT, preferred_element_type=F32)


def _split3(x):
    hi = x.astype(BF16)
    r1 = x - hi.astype(F32)
    mid = r1.astype(BF16)
    lo = (r1 - mid.astype(F32)).astype(BF16)
    return hi, mid, lo


def _rms(x, g):
    return x * lax.rsqrt(jnp.mean(x * x, axis=-1, keepdims=True) + EPS) * g


def _sigmoid(x):
    return 0.5 + 0.5 * jnp.tanh(0.5 * x)


def _const_spec(shape):
    nd = len(shape)
    return pl.BlockSpec(shape, lambda *_: (0,) * nd, pipeline_mode=pl.Buffered(1))


_C_Q = 0
_C_KC = _C_Q + NSA_HEADS * LANES
_C_VC = _C_KC + LANES
_C_KS = _C_VC + LANES
_C_KW = _C_KS + LANES
_C_SM = _C_KW + LANES
_C_QKB = _C_SM + LANES
_C_VB = _C_QKB + 2 * ML_WIDTH
_C_OB = _C_VB + ML_WIDTH
_C_MG = _C_OB + ML_WIDTH
_C_END = _C_MG + 2 * D_MODEL


def _inproj_kernel(x_ref, g_ref, w_ref, wt_ref, rc_ref, rp_ref, rm_ref,
                   q_ref, kc_ref, vc_ref, ks_ref, kw_ref, vst_ref, vwt_ref, sm_ref, smt_ref,
                   qkb_ref, vb_ref, ob_ref, mg_ref):
    hn = _rms(x_ref[...], g_ref[...]).astype(BF16)
    rc, rp, rm = rc_ref[...], rp_ref[...], rm_ref[...]

    def rope(z):
        return z * rc + pltpu.roll(z, 8, 1) * rp + pltpu.roll(z, LANES - 8, 1) * rm

    scale = NSA_HD ** -0.5 * math.log2(math.e)
    for h in range(NSA_HEADS):
        z = _dot(hn, w_ref[:, _C_Q + h * LANES:_C_Q + (h + 1) * LANES])
        q_ref[:, h * LANES:(h + 1) * LANES] = (rope(z) * scale).astype(BF16)
    kc_ref[...] = rope(_dot(hn, w_ref[:, _C_KC:_C_KC + LANES])).astype(BF16)
    vc_ref[...] = _dot(hn, w_ref[:, _C_VC:_C_VC + LANES]).astype(BF16)
    ks_ref[...] = rope(_dot(hn, w_ref[:, _C_KS:_C_KS + LANES])).astype(BF16)
    kw_ref[...] = rope(_dot(hn, w_ref[:, _C_KW:_C_KW + LANES])).astype(BF16)
    sm_ref[...] = _dot(hn, w_ref[:, _C_SM:_C_SM + LANES])
    for c0 in range(0, 2 * ML_WIDTH, 512):
        qkb_ref[:, c0:c0 + 512] = _dot(hn, w_ref[:, _C_QKB + c0:_C_QKB + c0 + 512]).astype(BF16)
    vb_ref[...] = _dot(hn, w_ref[:, _C_VB:_C_VB + ML_WIDTH]).astype(BF16)
    ob_ref[...] = _dot(hn, w_ref[:, _C_OB:_C_OB + ML_WIDTH]).astype(BF16)
    for c0 in range(0, 2 * D_MODEL, 512):
        mg_ref[:, c0:c0 + 512] = _dot(hn, w_ref[:, _C_MG + c0:_C_MG + c0 + 512]).astype(BF16)
    zt = _dot_nt(wt_ref[...], hn)
    for i in range(TM // KC):
        vst_ref[i] = zt[0:LANES, i * KC:(i + 1) * KC].astype(BF16)
        vwt_ref[i] = zt[LANES:2 * LANES, i * KC:(i + 1) * KC].astype(BF16)
    smt_ref[...] = zt[2 * LANES:3 * LANES, :]


def _inproj(x2, g_mix, wcat, wtr, rc, rp, rm):
    t = x2.shape[0]
    row = lambda w: pl.BlockSpec((TM, w), lambda i: (i, 0))
    out_shape = [
        jax.ShapeDtypeStruct((t, NSA_HEADS * LANES), BF16),
        jax.ShapeDtypeStruct((t, LANES), BF16),
        jax.ShapeDtypeStruct((t, LANES), BF16),
        jax.ShapeDtypeStruct((t, LANES), BF16),
        jax.ShapeDtypeStruct((t, LANES), BF16),
        jax.ShapeDtypeStruct((t // KC, LANES, KC), BF16),
        jax.ShapeDtypeStruct((t // KC, LANES, KC), BF16),
        jax.ShapeDtypeStruct((t, LANES), F32),
        jax.ShapeDtypeStruct((LANES, t), F32),
        jax.ShapeDtypeStruct((t, 2 * ML_WIDTH), BF16),
        jax.ShapeDtypeStruct((t, ML_WIDTH), BF16),
        jax.ShapeDtypeStruct((t, ML_WIDTH), BF16),
        jax.ShapeDtypeStruct((t, 2 * D_MODEL), BF16),
    ]
    chunk3 = pl.BlockSpec((TM // KC, LANES, KC), lambda i: (i, 0, 0))
    out_specs = [row(NSA_HEADS * LANES), row(LANES), row(LANES), row(LANES), row(LANES), chunk3, chunk3,
                 row(LANES), pl.BlockSpec((LANES, TM), lambda i: (0, i)),
                 row(2 * ML_WIDTH), row(ML_WIDTH), row(ML_WIDTH), row(2 * D_MODEL)]
    return pl.pallas_call(
        _inproj_kernel,
        out_shape=out_shape,
        grid=(t // TM,),
        in_specs=[row(D_MODEL), _const_spec((1, D_MODEL)), _const_spec((D_MODEL, _C_END)),
                  _const_spec((3 * LANES, D_MODEL)), row(LANES), row(LANES), row(LANES)],
        out_specs=out_specs,
        compiler_params=pltpu.CompilerParams(dimension_semantics=("arbitrary",), vmem_limit_bytes=VMEM_LIMIT),
        name="inproj",
    )(x2, g_mix, wcat, wtr, rc, rp, rm)


def _gelu_tanh(x):
    return 0.5 * x * (1.0 + jnp.tanh(math.sqrt(2.0 / math.pi) * (x + 0.044715 * x * x * x)))


def _compress_kernel(rk_ref, rv_ref, wka_ref, wkb_ref, wva_ref, wvb_ref, pek_ref, pev_ref,
                     w1k_ref, w1v_ref, w2k_ref, w2vt_ref, kc_ref, vct_ref):
    nrow = rk_ref.shape[0]

    def hidden(r_ref, wa_ref, wb_ref, pe_ref, w1_ref):
        r = r_ref[...]
        ha = _dot(r, wa_ref[...])
        hb = _dot(r, wb_ref[...])
        hb = pltpu.roll(hb, nrow - 1, 0)
        c = _dot(pe_ref[...], w1_ref[...])[0:1, :]
        return [_gelu_tanh(ha[:, g * CMP_HIDDEN:(g + 1) * CMP_HIDDEN] + hb[:, g * CMP_HIDDEN:(g + 1) * CMP_HIDDEN] + c).astype(BF16)
                for g in range(NSA_KV)]

    ak = hidden(rk_ref, wka_ref, wkb_ref, pek_ref, w1k_ref)
    kc_ref[...] = (_dot(ak[0], w2k_ref[0]) + _dot(ak[1], w2k_ref[1])).astype(BF16)
    av = hidden(rv_ref, wva_ref, wvb_ref, pev_ref, w1v_ref)
    for g in range(NSA_KV):
        vct_ref[g * NSA_HD:(g + 1) * NSA_HD, :] = _dot_nt(w2vt_ref[...], av[g]).astype(BF16)


def _compress(rk, rv, wka, wkb, wva, wvb, pek, pev, w1k, w1v, w2k, w2vt):
    b, nrow, width = rk.shape
    blk = pl.BlockSpec((None, nrow, width), lambda i: (i, 0, 0))
    return pl.pallas_call(
        _compress_kernel,
        out_shape=[jax.ShapeDtypeStruct((b, nrow, LANES), BF16),
                   jax.ShapeDtypeStruct((b, LANES, nrow), BF16)],
        grid=(b,),
        in_specs=[blk, blk] + [_const_spec(a.shape) for a in (wka, wkb, wva, wvb, pek, pev, w1k, w1v, w2k, w2vt)],
        out_specs=[pl.BlockSpec((None, nrow, LANES), lambda i: (i, 0, 0)),
                   pl.BlockSpec((None, LANES, nrow), lambda i: (i, 0, 0))],
        compiler_params=pltpu.CompilerParams(dimension_semantics=("arbitrary",), vmem_limit_bytes=VMEM_LIMIT),
        name="compress",
    )(rk, rv, wka, wkb, wva, wvb, pek, pev, w1k, w1v, w2k, w2vt)


def _nsa_kernel(q_ref, kc_ref, vct_ref, ks_ref, kw_ref, vst_ref, vwt_ref, smt_ref, o_ref, bias_scr, sx_scr, sy_scr):
    g = pl.program_id(1)
    c = pl.program_id(2)
    t0 = c * QT
    ncmp = kc_ref.shape[0]
    nsel = bias_scr.shape[0]
    nw = WINDOW // KC + 1
    width = NSA_HPG * QT

    qs = jnp.concatenate([q_ref[:, h * LANES:(h + 1) * LANES] for h in range(NSA_HPG)], axis=0)
    u_row = lax.broadcasted_iota(jnp.int32, (1, width), 1) % QT
    t_row = t0 + u_row
    r_kc = lax.broadcasted_iota(jnp.int32, (KC, 1), 0)

    n_grp = ks_ref.shape[0] // SEL_GROUP
    n_full = lax.shift_right_logical(t0, int(math.log2(SEL_GROUP)))

    def qk_group(j):
        return _dot_nt(ks_ref[pl.ds(pl.multiple_of(j * SEL_GROUP, SEL_GROUP), SEL_GROUP), :], qs)

    sc = _dot_nt(kc_ref[...], qs)

    w_slabs, w_chunks = [], []
    for i in range(nw):
        jj = c - (nw - 1) + i
        jc = jnp.maximum(jj, 0)
        si = _dot_nt(kw_ref[pl.ds(pl.multiple_of(jc * KC, KC), KC), :], qs)
        if i == 0:
            keep = (r_kc > u_row) & (jj >= 0)
        elif i == nw - 1:
            keep = r_kc <= u_row
        else:
            keep = jj >= 0
        w_slabs.append(jnp.where(keep, si, NEG))
        w_chunks.append(jc)

    n_col = lax.broadcasted_iota(jnp.int32, (ncmp, 1), 0)
    cmask = (CMP_STRIDE * n_col + (CMP_LEN - 1) <= t_row) & (n_col < ncmp - 1)
    s = jnp.where(cmask, sc, NEG)
    m = jnp.max(s, axis=0, keepdims=True)
    e = jnp.exp2(s - m)
    anyv = (t_row >= CMP_LEN - 1).astype(F32)
    p = e * (anyv / jnp.sum(e, axis=0, keepdims=True))
    o_cmp = _dot(vct_ref[...], p.astype(BF16))

    psum = p[:, 0:QT]
    for h in range(1, NSA_HPG):
        psum = psum + p[:, h * QT:(h + 1) * QT]
    s_col = lax.broadcasted_iota(jnp.int32, (nsel, 1), 0)
    n_lane = lax.broadcasted_iota(jnp.int32, (1, ncmp), 1)
    ov = ((CMP_STRIDE * n_lane < SEL_BLOCK * (s_col + 1)) & (CMP_STRIDE * n_lane + (CMP_LEN - 1) >= SEL_BLOCK * s_col)
          ).astype(BF16)
    imp = sum(_dot(ov, part) for part in _split3(psum))

    s_diag = qk_group(n_full)
    sx_scr[...] = qk_group(0)

    mxw = w_slabs[0]
    for sl in w_slabs[1:]:
        mxw = jnp.maximum(mxw, sl)
    mw = jnp.max(mxw, axis=0, keepdims=True)
    pws = None
    acc_w = jnp.zeros((NSA_HD, width), F32)
    for sl, jc in zip(w_slabs, w_chunks):
        pw = jnp.exp2(sl - mw)
        pws = pw if pws is None else pws + pw
        acc_w = acc_w + _dot(vwt_ref[jc], pw.astype(BF16))
    o_win = acc_w / jnp.sum(pws, axis=0, keepdims=True)

    t1 = t0 + lax.broadcasted_iota(jnp.int32, (1, QT), 1)
    cur = lax.shift_right_logical(t1, 6)
    forced = (s_col == 0) | (s_col == cur) | (s_col == cur - 1)
    valid = SEL_BLOCK * s_col <= t1
    val = jnp.where(valid, jnp.where(forced, imp + SEL_FORCE, imp), NEG)
    sub = 8
    r_sub = lax.broadcasted_iota(jnp.int32, (sub, 1), 0)
    blocks = [val[r * sub:(r + 1) * sub, :] for r in range(nsel // sub)]
    ranks = [jnp.zeros((sub, QT), F32) for _ in blocks]
    for i in range(nsel):
        vi = val[i:i + 1, :]
        for r, blk in enumerate(blocks):
            if i < r * sub:
                beats = vi >= blk
            elif i >= (r + 1) * sub:
                beats = vi > blk
            else:
                beats = (vi > blk) | ((vi == blk) & (r_sub > i - r * sub))
            ranks[r] = ranks[r] + jnp.where(beats, 1.0, 0.0)
    bias = jnp.where(jnp.concatenate(ranks, axis=0) < SEL_TOPK, 0.0, NEG).astype(F32)
    bias_scr[...] = jnp.concatenate([bias] * NSA_HPG, axis=1)

    r_blk = lax.broadcasted_iota(jnp.int32, (SEL_BLOCK, 1), 0)
    blk_per_grp = SEL_GROUP // SEL_BLOCK
    chunk_per_grp = SEL_GROUP // KC
    blk_per_chunk = KC // SEL_BLOCK

    def sel_mask(sj, j, causal=False, live=None):
        slabs = []
        for i in range(blk_per_grp):
            brow = bias_scr[pl.ds(blk_per_grp * j + i, 1), :]
            if live is not None:
                brow = jnp.where(live, brow, NEG)
            sl = sj[i * SEL_BLOCK:(i + 1) * SEL_BLOCK, :] + brow
            if causal:
                sl = jnp.where(j * SEL_GROUP + i * SEL_BLOCK + r_blk <= t_row, sl, NEG)
            slabs.append(sl)
        return slabs

    def sel_update(j, slabs, carry):
        m_o, l_o, acc = carry
        mx = slabs[0]
        for sl in slabs[1:]:
            mx = jnp.maximum(mx, sl)
        m_n = jnp.maximum(m_o, jnp.max(mx, axis=0, keepdims=True))
        a = jnp.exp2(m_o - m_n)
        acc = a * acc
        psum = None
        for ci in range(chunk_per_grp):
            pj = jnp.exp2(jnp.concatenate(slabs[blk_per_chunk * ci:blk_per_chunk * (ci + 1)], axis=0) - m_n)
            psum = pj if psum is None else psum + pj
            acc = acc + _dot(vst_ref[chunk_per_grp * j + ci], pj.astype(BF16))
        l_n = a * l_o + jnp.sum(psum, axis=0, keepdims=True)
        return m_n, l_n, acc

    empty = (jnp.full((1, width), NEG, F32), jnp.zeros((1, width), F32), jnp.zeros((NSA_HD, width), F32))
    seeded = sel_update(n_full, sel_mask(s_diag, n_full, causal=True), empty)

    def pair_body(jp, carry):
        ja, jb = 2 * jp, 2 * jp + 1
        sy_scr[...] = qk_group(jb)
        carry = sel_update(ja, sel_mask(sx_scr, ja), carry)
        sx_scr[...] = qk_group(jnp.minimum(ja + 2, n_grp - 1))
        return sel_update(jb, sel_mask(sy_scr, jb, live=jb < n_full), carry)

    _, l_s, acc_s = lax.fori_loop(0, lax.shift_right_logical(n_full + 1, 1), pair_body, seeded)
    o_sel = acc_s / l_s

    def gate_row(br):
        rows = [smt_ref[pl.ds(SM_GATE + 3 * (NSA_HPG * g + h) + br, 1), :] for h in range(NSA_HPG)]
        return _sigmoid(jnp.concatenate(rows, axis=1))

    o_t = gate_row(0) * o_cmp + gate_row(1) * o_sel + gate_row(2) * o_win
    for pr in range(NSA_HPG // 2):
        xp = jnp.concatenate([o_t[:, (2 * pr) * QT:(2 * pr + 1) * QT], o_t[:, (2 * pr + 1) * QT:(2 * pr + 2) * QT]], axis=0)
        o_ref[:, pr * LANES:(pr + 1) * LANES] = xp.T.astype(BF16)


def _nsa(qpad, kcb, vct, ks, kw, vst, vwt, smt, b, s):
    nq = s // QT
    ncmp = kcb.shape[1]
    return pl.pallas_call(
        _nsa_kernel,
        out_shape=jax.ShapeDtypeStruct((b * s, NSA_HEADS * NSA_HD), BF16),
        grid=(b, NSA_KV, nq),
        in_specs=[
            pl.BlockSpec((QT, NSA_HPG * LANES), lambda bi, g, c: (bi * nq + c, g)),
            pl.BlockSpec((None, ncmp, LANES), lambda bi, g, c: (bi, 0, 0)),
            pl.BlockSpec((None, NSA_HD, ncmp), lambda bi, g, c: (bi, g, 0)),
            pl.BlockSpec((s, LANES), lambda bi, g, c: (bi, 0)),
            pl.BlockSpec((s, LANES), lambda bi, g, c: (bi, 0)),
            pl.BlockSpec((s // KC, NSA_HD, KC), lambda bi, g, c: (bi, g, 0)),
            pl.BlockSpec((s // KC, NSA_HD, KC), lambda bi, g, c: (bi, g, 0)),
            pl.BlockSpec((LANES, QT), lambda bi, g, c: (0, bi * nq + c)),
        ],
        out_specs=pl.BlockSpec((QT, NSA_HPG * NSA_HD), lambda bi, g, c: (bi * nq + c, g)),
        scratch_shapes=[pltpu.VMEM((s // SEL_BLOCK, NSA_HPG * QT), F32),
                        pltpu.VMEM((SEL_GROUP, NSA_HPG * QT), F32),
                        pltpu.VMEM((SEL_GROUP, NSA_HPG * QT), F32)],
        compiler_params=pltpu.CompilerParams(dimension_semantics=("arbitrary", "arbitrary", "arbitrary"),
                                             vmem_limit_bytes=VMEM_LIMIT),
        name="nsa",
    )(qpad, kcb, vct, ks, kw, vst, vwt, smt)


def _log_sigmoid(x):
    return jnp.minimum(x, 0.0) - jnp.log(1.0 + jnp.exp(-jnp.abs(x)))


def _mlstm_kernel(qk_ref, v_ref, og_ref, sm_ref, smt_ref, wc_ref, bc_ref, bifc_ref, bifr_ref, ghn_ref,
                  y_ref, ext_scr, ct_scr, n_scr, m_scr):
    lc = ML_CHUNK

    @pl.when(pl.program_id(1) == 0)
    def _():
        ext_scr[0:8, :] = jnp.zeros((8, 2 * ML_WIDTH), F32)
        ct_scr[...] = jnp.zeros_like(ct_scr)
        n_scr[...] = jnp.zeros_like(n_scr)
        m_scr[...] = jnp.zeros_like(m_scr)

    ext_scr[8:8 + ML_BLOCK, :] = qk_ref[...].astype(F32)
    y = bc_ref[...]
    for j in range(CONV_W):
        y = y + wc_ref[j:j + 1, :] * ext_scr[pl.ds(8 - (CONV_W - 1) + j, ML_BLOCK), :]
    ext_scr[0:8, :] = ext_scr[ML_BLOCK:ML_BLOCK + 8, :]
    qkc = y * _sigmoid(y)
    q_all = qkc[:, 0:ML_WIDTH].astype(BF16)
    k_all = (qkc[:, ML_WIDTH:2 * ML_WIDTH] * (ML_HD ** -0.5)).astype(BF16)

    ifc = sm_ref[...] + bifc_ref[...]
    ifr = smt_ref[...] + bifr_ref[...]
    lfc = _log_sigmoid(ifc)
    lfr = _log_sigmoid(ifr)
    rr = lax.broadcasted_iota(jnp.int32, (lc, lc), 0)
    cc = lax.broadcasted_iota(jnp.int32, (lc, lc), 1)
    causal = rr >= cc
    tri_l = causal.astype(F32)
    tri_u = (rr <= cc).astype(F32)

    for ci in range(ML_BLOCK // lc):
        lo, hi = ci * lc, (ci + 1) * lc
        bc_all = jnp.dot(tri_l, lfc[lo:hi, :], preferred_element_type=F32, precision=lax.Precision.HIGHEST)
        br_all = jnp.dot(lfr[:, lo:hi], tri_u, preferred_element_type=F32, precision=lax.Precision.HIGHEST)
        for h in range(ML_HEADS):
            hs = slice(h * ML_HD, (h + 1) * ML_HD)
            bcol = bc_all[:, SM_F + h:SM_F + h + 1]
            brow = br_all[SM_F + h:SM_F + h + 1, :]
            icol = ifc[lo:hi, SM_I + h:SM_I + h + 1]
            irow = ifr[SM_I + h:SM_I + h + 1, lo:hi]
            mprev = m_scr[h][:, 0:1]
            qh = q_all[lo:hi, hs]
            kh = k_all[lo:hi, hs]
            vh = v_ref[lo:hi, hs]
            dmat = jnp.where(causal, bcol - brow + irow, NEG)
            inter = bcol + mprev
            mt = jnp.maximum(jnp.max(dmat, axis=-1, keepdims=True), inter)
            a = jnp.exp(dmat - mt) * _dot_nt(qh, kh)
            dec = jnp.exp(inter - mt)
            ct = ct_scr[h]
            nrow = n_scr[h]
            num = _dot(a.astype(BF16), vh) + dec * _dot(qh, ct.astype(BF16))
            den = jnp.sum(a, axis=-1, keepdims=True) + dec * jnp.sum(qh.astype(F32) * nrow, axis=-1, keepdims=True)
            hh = num / jnp.maximum(jnp.abs(den), jnp.exp(-mt))
            blast = bcol[lc - 1:lc, :]
            grow = blast - brow + irow
            mnew = jnp.maximum(blast + mprev, jnp.max(grow, axis=-1, keepdims=True))
            wprev = jnp.exp(blast + mprev - mnew)
            kwt = kh.astype(F32) * jnp.exp(blast - bcol + icol - mnew)
            ct_scr[h] = wprev * ct + lax.dot_general(kwt.astype(BF16), vh, _TN, preferred_element_type=F32)
            n_scr[h] = wprev * nrow + jnp.sum(kwt, axis=0, keepdims=True)
            m_scr[h] = jnp.broadcast_to(mnew, (1, LANES))
            hm = hh * _sigmoid(og_ref[lo:hi, hs].astype(F32))
            y_ref[lo:hi, hs] = _rms(hm, ghn_ref[:, hs]).astype(BF16)


def _mlstm(qkb, vb, ob, sm, smt, wconv, bconv, bifc, bifr, ghn, b, s):
    nb = s // ML_BLOCK
    row = lambda w: pl.BlockSpec((ML_BLOCK, w), lambda bi, j: (bi * nb + j, 0))
    return pl.pallas_call(
        _mlstm_kernel,
        out_shape=jax.ShapeDtypeStruct((b * s, ML_WIDTH), BF16),
        grid=(b, nb),
        in_specs=[row(2 * ML_WIDTH), row(ML_WIDTH), row(ML_WIDTH), row(LANES),
                  pl.BlockSpec((LANES, ML_BLOCK), lambda bi, j: (0, bi * nb + j)),
                  _const_spec(wconv.shape), _const_spec(bconv.shape), _const_spec(bifc.shape),
                  _const_spec(bifr.shape), _const_spec(ghn.shape)],
        out_specs=row(ML_WIDTH),
        scratch_shapes=[pltpu.VMEM((ML_BLOCK + 8, 2 * ML_WIDTH), F32),
                        pltpu.VMEM((ML_HEADS, ML_HD, ML_HD), F32),
                        pltpu.VMEM((ML_HEADS, 1, ML_HD), F32),
                        pltpu.VMEM((ML_HEADS, 1, LANES), F32)],
        compiler_params=pltpu.CompilerParams(dimension_semantics=("arbitrary", "arbitrary"),
                                             vmem_limit_bytes=VMEM_LIMIT),
        name="mlstm",
    )(qkb, vb, ob, sm, smt, wconv, bconv, bifc, bifr, ghn)


RT_BUCKET = N_EXPERTS
RT_RANK = N_EXPERTS + 1
RT_WLO = N_EXPERTS + 2
RT_WHI = N_EXPERTS + 3
N_BUCKETS = N_GROUPS * 6
X_ROWS = D_MODEL // LANES
REC = 2 * X_ROWS
DMA_UNROLL = 8


def _merge_kernel(ya_ref, yb_ref, mg_ref, x_ref, wpa_ref, wpb_ref, wout_ref, gffn_ref, wr_ref, br_ref,
                  x1_ref, hx_ref, slab_ref, hist_ref):
    pa = _dot(ya_ref[...], wpa_ref[...])
    pb = _dot(yb_ref[...], wpb_ref[...])
    ga = _sigmoid(mg_ref[:, 0:D_MODEL].astype(F32))
    gb = _sigmoid(mg_ref[:, D_MODEL:2 * D_MODEL].astype(F32))
    mixed = (ga * pa + gb * pb).astype(BF16)
    x1 = x_ref[...] + _dot(mixed, wout_ref[...])
    x1_ref[...] = x1
    h2 = _rms(x1, gffn_ref[...])
    h_hi = h2.astype(BF16)
    for j in range(X_ROWS):
        hx_ref[pl.ds(j, TM, stride=REC), :] = h2[:, j * LANES:(j + 1) * LANES]

    h_lo = (h2 - h_hi.astype(F32)).astype(BF16)
    r_hi = _dot(h_hi, wr_ref[...])
    logit = r_hi[:, 0:LANES] + r_hi[:, LANES:2 * LANES] + _dot(h_lo, wr_ref[:, 0:LANES]) + br_ref[...]
    lane = lax.broadcasted_iota(jnp.int32, logit.shape, 1)
    big = jnp.int32(LANES)
    gmask = (lane >= N_EXPERTS) & (lane < N_EXPERTS + N_GROUPS)
    gl = jnp.where(gmask, logit, NEG)
    gmax = jnp.max(gl, axis=-1, keepdims=True)
    gidx = jnp.min(jnp.where(gmask & (gl == gmax), lane, big), axis=-1, keepdims=True) - N_EXPERTS
    pg_sel = 1.0 / jnp.sum(jnp.where(gmask, jnp.exp(gl - gmax), 0.0), axis=-1, keepdims=True)
    emask = (lane < N_EXPERTS) & (lax.shift_right_logical(lane, 2) == gidx)
    el = jnp.where(emask, logit, NEG)
    e1 = jnp.max(el, axis=-1, keepdims=True)
    i1 = jnp.min(jnp.where(emask & (el == e1), lane, big), axis=-1, keepdims=True)
    emask2 = emask & (lane != i1)
    el2 = jnp.where(emask2, logit, NEG)
    e2 = jnp.max(el2, axis=-1, keepdims=True)
    i2 = jnp.min(jnp.where(emask2 & (el2 == e2), lane, big), axis=-1, keepdims=True)
    x21 = jnp.exp(e2 - e1)
    w1 = pg_sel / (1.0 + x21)
    w2 = pg_sel * x21 / (1.0 + x21)
    first_lo = i1 < i2
    e_lo = jnp.where(first_lo, i1, i2) - EXP_PER_GROUP * gidx
    e_hi = jnp.where(first_lo, i2, i1) - EXP_PER_GROUP * gidx
    pair = lax.shift_right_logical(e_lo * (2 * EXP_PER_GROUP - 1 - e_lo), 1) + (e_hi - e_lo - 1)
    bucket = 6 * gidx + pair
    member = lane == bucket
    onehot = jnp.where(member, 1.0, 0.0)
    rr = lax.broadcasted_iota(jnp.int32, (TM, TM), 0)
    cc = lax.broadcasted_iota(jnp.int32, (TM, TM), 1)
    earlier = _dot((rr > cc).astype(BF16), onehot.astype(BF16))
    rank = jnp.sum(jnp.where(member, earlier, 0.0), axis=-1, keepdims=True)
    slab = jnp.where(lane == i1, w1, 0.0) + jnp.where(lane == i2, w2, 0.0)
    slab = jnp.where(lane == RT_BUCKET, bucket.astype(F32), slab)
    slab = jnp.where(lane == RT_RANK, rank, slab)
    slab = jnp.where(lane == RT_WLO, jnp.where(first_lo, w1, w2), slab)
    slab = jnp.where(lane == RT_WHI, jnp.where(first_lo, w2, w1), slab)
    slab_ref[...] = slab
    hx_ref[pl.ds(X_ROWS, TM, stride=REC), :] = slab
    for j in range(X_ROWS + 1, REC):
        hx_ref[pl.ds(j, TM, stride=REC), :] = jnp.zeros((TM, LANES), F32)
    hist_ref[...] = jnp.broadcast_to(jnp.sum(onehot, axis=0, keepdims=True), (8, LANES))


def _merge(ya, yb, mg, x2, wpa, wpb, wout, gffn, wr, br):
    t = x2.shape[0]
    row = lambda w: pl.BlockSpec((TM, w), lambda i: (i, 0))
    return pl.pallas_call(
        _merge_kernel,
        out_shape=[jax.ShapeDtypeStruct((t, D_MODEL), F32),
                   jax.ShapeDtypeStruct((t * REC, LANES), F32),
                   jax.ShapeDtypeStruct((t, LANES), F32),
                   jax.ShapeDtypeStruct((t // TM * 8, LANES), F32)],
        grid=(t // TM,),
        in_specs=[row(NSA_HEADS * NSA_HD), row(ML_WIDTH), row(2 * D_MODEL), row(D_MODEL)]
                 + [_const_spec(a.shape) for a in (wpa, wpb, wout, gffn, wr, br)],
        out_specs=[row(D_MODEL), pl.BlockSpec((TM * REC, LANES), lambda i: (i, 0)), row(LANES),
                   pl.BlockSpec((8, LANES), lambda i: (i, 0))],
        compiler_params=pltpu.CompilerParams(dimension_semantics=("arbitrary",), vmem_limit_bytes=VMEM_LIMIT),
        name="merge",
    )(ya, yb, mg, x2, wpa, wpb, wout, gffn, wr, br)


def _rec_copy(src_ref, src_tok, dst_ref, dst_tok, sem, rows):
    src = src_ref.at[pl.ds(pl.multiple_of(src_tok * rows, rows), rows), :]
    dst = dst_ref.at[pl.ds(pl.multiple_of(dst_tok * rows, rows), rows), :]
    return pltpu.make_async_copy(src, dst, sem)


def _token_copies(n, make, wait=False):
    def body(g, carry):
        for u in range(DMA_UNROLL):
            cp = make(g * DMA_UNROLL + u)
            if wait:
                cp.wait()
            else:
                cp.start(priority=u % 2)
        return carry
    lax.fori_loop(0, n // DMA_UNROLL, body, 0)


def _dispatch_kernel(pos_ref, tail_ref, hx_ref, out_ref, stage_scr, zero_scr, sem, zsem):
    i = pl.program_id(0)
    slot = lax.rem(i, 2)
    n_tiles = out_ref.shape[0] // (TM * REC)
    n_used = tail_ref[2 * N_BUCKETS]

    def zero_copy(first_slot):
        start = pl.multiple_of(first_slot * REC, TM * REC)
        return pltpu.make_async_copy(zero_scr, out_ref.at[pl.ds(start, TM * REC), :], zsem)

    @pl.when(i == 0)
    def _():
        zero_scr[...] = jnp.zeros_like(zero_scr)
        for phase in ("start", "wait"):
            for b in range(N_BUCKETS):
                @pl.when(tail_ref[N_BUCKETS + b] > 0)
                def _():
                    getattr(zero_copy(tail_ref[b]), phase)()

                @pl.when(n_used + b < n_tiles)
                def _():
                    getattr(zero_copy((n_used + b) * TM), phase)()

    stage_scr[slot] = hx_ref[...]
    base = i * TM
    _token_copies(TM, lambda r: _rec_copy(stage_scr.at[slot], r, out_ref, pos_ref[base + r], sem.at[slot], REC))

    def drain(which):
        _token_copies(TM, lambda r: _rec_copy(stage_scr.at[which], 0, out_ref, 0, sem.at[which], REC), wait=True)

    @pl.when(i > 0)
    def _():
        drain(1 - slot)

    @pl.when(i == pl.num_programs(0) - 1)
    def _():
        drain(slot)


def _dispatch(pos, tail, hx, n_slots):
    t = hx.shape[0] // REC
    return pl.pallas_call(
        _dispatch_kernel,
        out_shape=jax.ShapeDtypeStruct((n_slots * REC, LANES), F32),
        grid_spec=pltpu.PrefetchScalarGridSpec(
            num_scalar_prefetch=2,
            grid=(t // TM,),
            in_specs=[pl.BlockSpec((TM * REC, LANES), lambda i, pos_r, tail_r: (i, 0))],
            out_specs=pl.BlockSpec(memory_space=pl.ANY),
            scratch_shapes=[pltpu.VMEM((2, TM * REC, LANES), F32), pltpu.VMEM((TM * REC, LANES), F32),
                            pltpu.SemaphoreType.DMA((2,)), pltpu.SemaphoreType.DMA(())],
        ),
        compiler_params=pltpu.CompilerParams(dimension_semantics=("arbitrary",), vmem_limit_bytes=VMEM_LIMIT,
                                             has_side_effects=True),
        name="dispatch",
    )(pos, tail, hx)


def _moe_kernel(te_ref, nu_ref, hx_ref, w13_ref, w2_ref, y_ref):
    k = pl.program_id(0)
    n_tiles = pl.num_programs(0)
    n_used = nu_ref[0]

    @pl.when(k < n_used)
    def _():
        h = jnp.concatenate([hx_ref[pl.ds(j, TM, stride=REC), :] for j in range(X_ROWS)], axis=1).astype(BF16)
        slab = hx_ref[pl.ds(X_ROWS, TM, stride=REC), :]
        y = None
        for side, lane in ((0, RT_WLO), (1, RT_WHI)):
            e = te_ref[side * n_tiles + k]
            a = _dot(h, w13_ref[e])
            gt = a[:, 0:D_EXPERT]
            act = gt * _sigmoid(gt) * a[:, D_EXPERT:2 * D_EXPERT] * slab[:, lane:lane + 1]
            part = _dot(act.astype(BF16), w2_ref[e])
            y = part if y is None else y + part
        for j in range(X_ROWS):
            y_ref[pl.ds(j, TM, stride=X_ROWS), :] = y[:, j * LANES:(j + 1) * LANES]

    @pl.when(k >= n_used)
    def _():
        y_ref[...] = jnp.zeros_like(y_ref)


def _moe(tile_e, n_used, hx_sorted, w13, w2):
    n_tiles = hx_sorted.shape[0] // (TM * REC)
    return pl.pallas_call(
        _moe_kernel,
        out_shape=jax.ShapeDtypeStruct((n_tiles * TM * X_ROWS, LANES), F32),
        grid_spec=pltpu.PrefetchScalarGridSpec(
            num_scalar_prefetch=2,
            grid=(n_tiles,),
            in_specs=[pl.BlockSpec((TM * REC, LANES), lambda k, te, nu: (jnp.minimum(k, nu[0] - 1), 0)),
                      pl.BlockSpec(w13.shape, lambda k, te, nu: (0, 0, 0), pipeline_mode=pl.Buffered(1)),
                      pl.BlockSpec(w2.shape, lambda k, te, nu: (0, 0, 0), pipeline_mode=pl.Buffered(1))],
            out_specs=pl.BlockSpec((TM * X_ROWS, LANES), lambda k, te, nu: (k, 0)),
        ),
        compiler_params=pltpu.CompilerParams(dimension_semantics=("arbitrary",), vmem_limit_bytes=VMEM_LIMIT),
        name="moe",
    )(tile_e, n_used, hx_sorted, w13, w2)


def _combine_kernel(pos_ref, y_ref, x1_ref, p_ref, gple_ref, wpg_ref, wpp_ref, gfin_ref, o_ref, ybuf, sem):
    i = pl.program_id(0)
    slot = lax.rem(i, 2)

    def gather(tile, which):
        _token_copies(TM, lambda r: _rec_copy(y_ref, pos_ref[tile * TM + r], ybuf.at[which], r, sem.at[which],
                                              X_ROWS))

    @pl.when(i == 0)
    def _():
        gather(0, 0)

    @pl.when(i + 1 < pl.num_programs(0))
    def _():
        gather(i + 1, 1 - slot)

    _token_copies(TM, lambda r: _rec_copy(y_ref, 0, ybuf.at[slot], 0, sem.at[slot], X_ROWS), wait=True)
    yb = ybuf.at[slot]
    x2 = x1_ref[...] + jnp.concatenate([yb[pl.ds(j, TM, stride=X_ROWS), :] for j in range(X_ROWS)], axis=1)
    h3 = _rms(x2, gple_ref[...]).astype(BF16)
    x3 = x2 + _sigmoid(_dot(h3, wpg_ref[...])) * _dot(p_ref[...].astype(BF16), wpp_ref[...])
    o_ref[...] = _rms(x3, gfin_ref[...])


def _combine(pos, y_sorted, x1, p2, gple, wpg, wpp, gfin):
    t = x1.shape[0]
    row = lambda w: pl.BlockSpec((TM, w), lambda i, pos_r: (i, 0))
    const = lambda a: pl.BlockSpec(a.shape, lambda i, pos_r: (0,) * a.ndim, pipeline_mode=pl.Buffered(1))
    return pl.pallas_call(
        _combine_kernel,
        out_shape=jax.ShapeDtypeStruct((t, D_MODEL), F32),
        grid_spec=pltpu.PrefetchScalarGridSpec(
            num_scalar_prefetch=1,
            grid=(t // TM,),
            in_specs=[pl.BlockSpec(memory_space=pl.ANY), row(D_MODEL), row(PLE_DIM),
                      const(gple), const(wpg), const(wpp), const(gfin)],
            out_specs=row(D_MODEL),
            scratch_shapes=[pltpu.VMEM((2, TM * X_ROWS, LANES), F32), pltpu.SemaphoreType.DMA((2,))],
        ),
        compiler_params=pltpu.CompilerParams(dimension_semantics=("arbitrary",), vmem_limit_bytes=VMEM_LIMIT),
        name="combine",
    )(pos, y_sorted, x1, p2, gple, wpg, wpp, gfin)


def _routing_tables(slab, hist8):
    t = slab.shape[0]
    nt = t // TM
    n_tiles = nt + N_BUCKETS
    hist = hist8.reshape(nt, 8, LANES)[:, 0, :]
    counts = jnp.sum(hist, axis=0)
    padded = jnp.ceil(counts / TM) * TM
    ends = jnp.cumsum(padded)
    first = (ends - padded)[None, :] + jnp.cumsum(hist, axis=0) - hist
    lane = jnp.arange(LANES, dtype=F32)[None, :]
    mine = lane == slab[:, RT_BUCKET:RT_BUCKET + 1]
    pos = jnp.sum(jnp.where(mine, jnp.repeat(first, TM, axis=0), 0.0), axis=1) + slab[:, RT_RANK]
    starts = jnp.arange(n_tiles, dtype=F32) * TM
    tile_bucket = jnp.minimum(jnp.sum(ends[None, :N_BUCKETS] <= starts[:, None], axis=1), N_BUCKETS - 1)
    group, pair = tile_bucket // 6, tile_bucket % 6
    e_lo = EXP_PER_GROUP * group + jnp.array([0, 0, 0, 1, 1, 2], jnp.int32)[pair]
    e_hi = EXP_PER_GROUP * group + jnp.array([1, 2, 3, 2, 3, 3], jnp.int32)[pair]
    tile_e = jnp.concatenate([e_lo, e_hi]).astype(jnp.int32)
    n_used = (ends[N_BUCKETS - 1] / TM).astype(jnp.int32).reshape(1)
    tail = jnp.concatenate([(ends - TM)[:N_BUCKETS], padded[:N_BUCKETS], n_used.astype(F32)]).astype(jnp.int32)
    return pos.astype(jnp.int32), tile_e, n_used, tail, n_tiles * TM


def _pack_inproj_weights(w):
    d = w.shape[0]
    qw = NSA_HEADS * NSA_HD
    kvw = NSA_KV * NSA_HD
    o = 0
    wq = w[:, o:o + qw]; o += qw
    wkc = w[:, o:o + kvw]; o += kvw
    wvc = w[:, o:o + kvw]; o += kvw
    wks = w[:, o:o + kvw]; o += kvw
    wvs = w[:, o:o + kvw]; o += kvw
    wkw = w[:, o:o + kvw]; o += kvw
    wvw = w[:, o:o + kvw]; o += kvw
    wga = w[:, o:o + 3 * NSA_HEADS]; o += 3 * NSA_HEADS
    wqkb = w[:, o:o + 2 * ML_WIDTH]; o += 2 * ML_WIDTH
    wvb = w[:, o:o + ML_WIDTH]; o += ML_WIDTH
    wob = w[:, o:o + ML_WIDTH]; o += ML_WIDTH
    wif = w[:, o:o + 2 * ML_HEADS]; o += 2 * ML_HEADS
    wmg = w[:, o:o + 2 * D_MODEL]
    zero64 = jnp.zeros((d, NSA_HD), w.dtype)
    qcols = []
    for h in range(NSA_HEADS):
        wh = wq[:, h * NSA_HD:(h + 1) * NSA_HD]
        qcols += [wh, zero64] if h // NSA_HPG == 0 else [zero64, wh]
    wsm = jnp.concatenate([wga, wif, jnp.zeros((d, LANES - 3 * NSA_HEADS - 2 * ML_HEADS), w.dtype)], axis=1)
    wcat = jnp.concatenate(qcols + [wkc, wvc, wks, wkw, wsm, wqkb, wvb, wob, wmg], axis=1).astype(BF16)
    wtr = jnp.concatenate([wvs, wvw, wsm], axis=1).T.astype(BF16)
    return wcat, wtr


def _rope_lane_tables(positions):
    inv = ROPE_THETA ** (-jnp.arange(0, ROPE_DIM, 2, dtype=F32) / ROPE_DIM)
    ang = positions.astype(F32).reshape(-1, 1) * inv[None, :]
    cos, sin = jnp.cos(ang), jnp.sin(ang)
    half = ROPE_DIM // 2
    d = jnp.arange(LANES) % NSA_HD
    cos_l, sin_l = jnp.tile(cos, (1, LANES // half)), jnp.tile(sin, (1, LANES // half))
    rc = jnp.where(d < ROPE_DIM, cos_l, 1.0)
    rp = jnp.where((d >= half) & (d < ROPE_DIM), sin_l, 0.0)
    rm = jnp.where(d < half, -sin_l, 0.0)
    return rc, rp, rm


def _pack_compress_weights(w1, w2, pe):
    half = CMP_LEN // 2
    w1r = w1.reshape(2, half, NSA_HD, CMP_HIDDEN)
    outs = []
    for part in range(2):
        wb = w1r[part].astype(BF16)
        zb = jnp.zeros_like(wb)
        wp = jnp.stack([jnp.stack([wb, zb], axis=2), jnp.stack([zb, wb], axis=2)], axis=1)
        outs.append(wp.reshape(half * NSA_KV * NSA_HD, NSA_KV * CMP_HIDDEN))
    pe8 = jnp.broadcast_to(pe.reshape(1, CMP_LEN * NSA_HD), (8, CMP_LEN * NSA_HD)).astype(BF16)
    return outs[0], outs[1], pe8, w1.astype(BF16)


def _stages(x, p, positions, g_mix, w_in, b_if, w_ck1, w_ck2, pe_ck, w_cv1, w_cv2, pe_cv, w_conv, b_conv, g_hn, w_pa, w_pb, w_out, g_ffn, w_rg, b_rg, w_re, b_re, w_e13, w_e2, g_ple, w_pg, w_pp, g_final):
    b, s, d = x.shape
    t = b * s
    rc, rp, rm = _rope_lane_tables(positions)
    assert w_in.shape[0] == 1, "the final norm is fused into the layer's last kernel: single-layer problem only"
    for i in range(w_in.shape[0]):
        x2 = x.reshape(t, d)
        wcat, wtr = _pack_inproj_weights(w_in[i])
        (qpad, kc_tok, vc_tok, ks, kw, vst, vwt, sm, smt, qkb, vb, ob, mg) = _inproj(
            x2, g_mix[i].reshape(1, d), wcat, wtr, rc, rp, rm)
        wka, wkb, pek, w1k = _pack_compress_weights(w_ck1[i], w_ck2[i], pe_ck[i])
        wva, wvb, pev, w1v = _pack_compress_weights(w_cv1[i], w_cv2[i], pe_cv[i])
        zpad = jnp.zeros((CMP_HIDDEN, NSA_HD), F32)
        w2k = jnp.stack([jnp.concatenate([w_ck2[i], zpad], axis=1),
                         jnp.concatenate([zpad, w_ck2[i]], axis=1)]).astype(BF16)
        w2vt = w_cv2[i].T.astype(BF16)
        nrow = s // CMP_STRIDE
        rk = kc_tok.reshape(b, nrow, CMP_STRIDE * LANES)
        rv = vc_tok.reshape(b, nrow, CMP_STRIDE * LANES)
        kcb, vct = _compress(rk, rv, wka, wkb, wva, wvb, pek, pev, w1k, w1v, w2k, w2vt)
        ya = _nsa(qpad, kcb, vct, ks, kw, vst, vwt, smt, b, s)
        bif = b_if[i].astype(F32)
        bifc = jnp.zeros((1, LANES), F32).at[0, SM_I:SM_I + 2 * ML_HEADS].set(bif)
        bifr = bifc.reshape(LANES, 1)
        yb = _mlstm(qkb, vb, ob, sm, smt, w_conv[i], b_conv[i].reshape(1, -1), bifc, bifr,
                    g_hn[i].reshape(1, -1), b, s)
        wr = jnp.concatenate([w_re[i], w_rg[i], jnp.zeros((d, LANES - N_EXPERTS - N_GROUPS), F32)], axis=1)
        wr_hi = wr.astype(BF16)
        wr = jnp.concatenate([wr_hi, (wr - wr_hi.astype(F32)).astype(BF16)], axis=1)
        br =jnp.concatenate([b_re[i], b_rg[i], jnp.zeros((LANES - N_EXPERTS - N_GROUPS,), F32)]).reshape(1, LANES)
        x1, hx, slab, hist8 = _merge(ya, yb, mg, x2, w_pa[i].astype(BF16), w_pb[i].astype(BF16),
                                     w_out[i].astype(BF16), g_ffn[i].reshape(1, d), wr, br)
        pos, tile_e, n_used, tail, n_slots = _routing_tables(slab, hist8)
        hx_sorted = _dispatch(pos, tail, hx, n_slots)
        y_sorted = _moe(tile_e, n_used, hx_sorted, w_e13[i].astype(BF16), w_e2[i].astype(BF16))
        out = _combine(pos, y_sorted, x1, p[i].reshape(t, PLE_DIM), g_ple[i].reshape(1, d), w_pg[i].astype(BF16),
                       w_pp[i].astype(BF16), g_final.reshape(1, d))
        x = out.reshape(b, s, d)
    return dict(out=x, qpad=qpad, ks=ks, kcb=kcb, vct=vct, y_a=ya, y_b=yb, x1=x1, hx=hx, pos=pos)


def kernel(x, p, positions, g_mix, w_in, b_if, w_ck1, w_ck2, pe_ck, w_cv1, w_cv2, pe_cv, w_conv, b_conv, g_hn, w_pa, w_pb, w_out, g_ffn, w_rg, b_rg, w_re, b_re, w_e13, w_e2, g_ple, w_pg, w_pp, g_final):
    return _stages(x, p, positions, g_mix, w_in, b_if, w_ck1, w_ck2, pe_ck, w_cv1, w_cv2, pe_cv, w_conv, b_conv, g_hn,
                   w_pa, w_pb, w_out, g_ffn, w_rg, b_rg, w_re, b_re, w_e13, w_e2, g_ple, w_pg, w_pp, g_final)["out"]
```

```python
import functools
import math

import jax
import jax.numpy as jnp
from jax import lax
from jax.experimental import pallas as pl
from jax.experimental.pallas import tpu as pltpu

F32 = jnp.float32
BF16 = jnp.bfloat16

EPS = 1e-6
NEG = -1e30

D_MODEL = 1024
PLE_DIM = 256
NSA_HEADS = 8
NSA_KV = 2
NSA_HPG = NSA_HEADS // NSA_KV
NSA_HD = 64
CMP_LEN = 32
CMP_STRIDE = 16
CMP_HIDDEN = 256
SEL_BLOCK = 64
SEL_TOPK = 16
SEL_FORCE = 1000.0
WINDOW = 512
ROPE_THETA = 500000.0
ROPE_DIM = NSA_HD // 4
ML_HEADS = 4
ML_HD = 128
ML_WIDTH = ML_HEADS * ML_HD
CONV_W = 4
N_GROUPS = 4
EXP_PER_GROUP = 4
N_EXPERTS = N_GROUPS * EXP_PER_GROUP
D_EXPERT = 256

LANES = 128
QT = 128
KC = 128
SEL_GROUP = 512
ML_CHUNK = 128
ML_BLOCK = 256
TM = 256
VMEM_LIMIT = 56 * 1024 * 1024

_NT = (((1,), (1,)), ((), ()))
_TN = (((0,), (0,)), ((), ()))

SM_GATE = 0
SM_I = 3 * NSA_HEADS
SM_F = SM_I + ML_HEADS


def _dot(a, b):
    return jnp.dot(a, b, preferred_element_type=F32)


def _dot_nt(a, b):
    return lax.dot_general(a, b, _NT, preferred_element_type=F32)


def _split3(x):
    hi = x.astype(BF16)
    r1 = x - hi.astype(F32)
    mid = r1.astype(BF16)
    lo = (r1 - mid.astype(F32)).astype(BF16)
    return hi, mid, lo


def _rms(x, g):
    return x * lax.rsqrt(jnp.mean(x * x, axis=-1, keepdims=True) + EPS) * g


def _sigmoid(x):
    return 0.5 + 0.5 * jnp.tanh(0.5 * x)


def _const_spec(shape):
    nd = len(shape)
    return pl.BlockSpec(shape, lambda *_: (0,) * nd, pipeline_mode=pl.Buffered(1))


_C_Q = 0
_C_KC = _C_Q + NSA_HEADS * LANES
_C_VC = _C_KC + LANES
_C_KS = _C_VC + LANES
_C_KW = _C_KS + LANES
_C_SM = _C_KW + LANES
_C_QKB = _C_SM + LANES
_C_VB = _C_QKB + 2 * ML_WIDTH
_C_OB = _C_VB + ML_WIDTH
_C_MG = _C_OB + ML_WIDTH
_C_END = _C_MG + 2 * D_MODEL


def _inproj_kernel(x_ref, g_ref, w_ref, wt_ref, rc_ref, rp_ref, rm_ref,
                   q_ref, kc_ref, vc_ref, ks_ref, kw_ref, vst_ref, vwt_ref, sm_ref, smt_ref,
                   qkb_ref, vb_ref, ob_ref, mg_ref):
    hn = _rms(x_ref[...], g_ref[...]).astype(BF16)
    rc, rp, rm = rc_ref[...], rp_ref[...], rm_ref[...]

    def rope(z):
        return z * rc + pltpu.roll(z, 8, 1) * rp + pltpu.roll(z, LANES - 8, 1) * rm

    scale = NSA_HD ** -0.5 * math.log2(math.e)
    for h in range(NSA_HEADS):
        z = _dot(hn, w_ref[:, _C_Q + h * LANES:_C_Q + (h + 1) * LANES])
        q_ref[:, h * LANES:(h + 1) * LANES] = (rope(z) * scale).astype(BF16)
    kc_ref[...] = rope(_dot(hn, w_ref[:, _C_KC:_C_KC + LANES])).astype(BF16)
    vc_ref[...] = _dot(hn, w_ref[:, _C_VC:_C_VC + LANES]).astype(BF16)
    ks_ref[...] = rope(_dot(hn, w_ref[:, _C_KS:_C_KS + LANES])).astype(BF16)
    kw_ref[...] = rope(_dot(hn, w_ref[:, _C_KW:_C_KW + LANES])).astype(BF16)
    sm_ref[...] = _dot(hn, w_ref[:, _C_SM:_C_SM + LANES])
    for c0 in range(0, 2 * ML_WIDTH, 512):
        qkb_ref[:, c0:c0 + 512] = _dot(hn, w_ref[:, _C_QKB + c0:_C_QKB + c0 + 512]).astype(BF16)
    vb_ref[...] = _dot(hn, w_ref[:, _C_VB:_C_VB + ML_WIDTH]).astype(BF16)
    ob_ref[...] = _dot(hn, w_ref[:, _C_OB:_C_OB + ML_WIDTH]).astype(BF16)
    for c0 in range(0, 2 * D_MODEL, 512):
        mg_ref[:, c0:c0 + 512] = _dot(hn, w_ref[:, _C_MG + c0:_C_MG + c0 + 512]).astype(BF16)
    zt = _dot_nt(wt_ref[...], hn)
    for i in range(TM // KC):
        vst_ref[i] = zt[0:LANES, i * KC:(i + 1) * KC].astype(BF16)
        vwt_ref[i] = zt[LANES:2 * LANES, i * KC:(i + 1) * KC].astype(BF16)
    smt_ref[...] = zt[2 * LANES:3 * LANES, :]


def _inproj(x2, g_mix, wcat, wtr, rc, rp, rm):
    t = x2.shape[0]
    row = lambda w: pl.BlockSpec((TM, w), lambda i: (i, 0))
    out_shape = [
        jax.ShapeDtypeStruct((t, NSA_HEADS * LANES), BF16),
        jax.ShapeDtypeStruct((t, LANES), BF16),
        jax.ShapeDtypeStruct((t, LANES), BF16),
        jax.ShapeDtypeStruct((t, LANES), BF16),
        jax.ShapeDtypeStruct((t, LANES), BF16),
        jax.ShapeDtypeStruct((t // KC, LANES, KC), BF16),
        jax.ShapeDtypeStruct((t // KC, LANES, KC), BF16),
        jax.ShapeDtypeStruct((t, LANES), F32),
        jax.ShapeDtypeStruct((LANES, t), F32),
        jax.ShapeDtypeStruct((t, 2 * ML_WIDTH), BF16),
        jax.ShapeDtypeStruct((t, ML_WIDTH), BF16),
        jax.ShapeDtypeStruct((t, ML_WIDTH), BF16),
        jax.ShapeDtypeStruct((t, 2 * D_MODEL), BF16),
    ]
    chunk3 = pl.BlockSpec((TM // KC, LANES, KC), lambda i: (i, 0, 0))
    out_specs = [row(NSA_HEADS * LANES), row(LANES), row(LANES), row(LANES), row(LANES), chunk3, chunk3,
                 row(LANES), pl.BlockSpec((LANES, TM), lambda i: (0, i)),
                 row(2 * ML_WIDTH), row(ML_WIDTH), row(ML_WIDTH), row(2 * D_MODEL)]
    return pl.pallas_call(
        _inproj_kernel,
        out_shape=out_shape,
        grid=(t // TM,),
        in_specs=[row(D_MODEL), _const_spec((1, D_MODEL)), _const_spec((D_MODEL, _C_END)),
                  _const_spec((3 * LANES, D_MODEL)), row(LANES), row(LANES), row(LANES)],
        out_specs=out_specs,
        compiler_params=pltpu.CompilerParams(dimension_semantics=("arbitrary",), vmem_limit_bytes=VMEM_LIMIT),
        name="inproj",
    )(x2, g_mix, wcat, wtr, rc, rp, rm)


def _gelu_tanh(x):
    return 0.5 * x * (1.0 + jnp.tanh(math.sqrt(2.0 / math.pi) * (x + 0.044715 * x * x * x)))


def _compress_kernel(rk_ref, rv_ref, wka_ref, wkb_ref, wva_ref, wvb_ref, pek_ref, pev_ref,
                     w1k_ref, w1v_ref, w2k_ref, w2vt_ref, kc_ref, vct_ref):
    nrow = rk_ref.shape[0]

    def hidden(r_ref, wa_ref, wb_ref, pe_ref, w1_ref):
        r = r_ref[...]
        ha = _dot(r, wa_ref[...])
        hb = _dot(r, wb_ref[...])
        hb = pltpu.roll(hb, nrow - 1, 0)
        c = _dot(pe_ref[...], w1_ref[...])[0:1, :]
        return [_gelu_tanh(ha[:, g * CMP_HIDDEN:(g + 1) * CMP_HIDDEN] + hb[:, g * CMP_HIDDEN:(g + 1) * CMP_HIDDEN] + c).astype(BF16)
                for g in range(NSA_KV)]

    ak = hidden(rk_ref, wka_ref, wkb_ref, pek_ref, w1k_ref)
    kc_ref[...] = (_dot(ak[0], w2k_ref[0]) + _dot(ak[1], w2k_ref[1])).astype(BF16)
    av = hidden(rv_ref, wva_ref, wvb_ref, pev_ref, w1v_ref)
    for g in range(NSA_KV):
        vct_ref[g * NSA_HD:(g + 1) * NSA_HD, :] = _dot_nt(w2vt_ref[...], av[g]).astype(BF16)


def _compress(rk, rv, wka, wkb, wva, wvb, pek, pev, w1k, w1v, w2k, w2vt):
    b, nrow, width = rk.shape
    blk = pl.BlockSpec((None, nrow, width), lambda i: (i, 0, 0))
    return pl.pallas_call(
        _compress_kernel,
        out_shape=[jax.ShapeDtypeStruct((b, nrow, LANES), BF16),
                   jax.ShapeDtypeStruct((b, LANES, nrow), BF16)],
        grid=(b,),
        in_specs=[blk, blk] + [_const_spec(a.shape) for a in (wka, wkb, wva, wvb, pek, pev, w1k, w1v, w2k, w2vt)],
        out_specs=[pl.BlockSpec((None, nrow, LANES), lambda i: (i, 0, 0)),
                   pl.BlockSpec((None, LANES, nrow), lambda i: (i, 0, 0))],
        compiler_params=pltpu.CompilerParams(dimension_semantics=("arbitrary",), vmem_limit_bytes=VMEM_LIMIT),
        name="compress",
    )(rk, rv, wka, wkb, wva, wvb, pek, pev, w1k, w1v, w2k, w2vt)


def _nsa_kernel(q_ref, kc_ref, vct_ref, ks_ref, kw_ref, vst_ref, vwt_ref, smt_ref, o_ref, bias_scr, sx_scr, sy_scr):
    c = pl.program_id(1)
    t0 = c * QT
    ncmp = kc_ref.shape[0]
    nsel = bias_scr.shape[0]
    nw = WINDOW // KC + 1
    gw = NSA_HPG * QT
    width = NSA_KV * gw

    def per_group(x):
        return [x[:, g * gw:(g + 1) * gw] for g in range(NSA_KV)]

    def pv(vt, p):
        pb = p.astype(BF16)
        return jnp.concatenate([_dot(vt[g * NSA_HD:(g + 1) * NSA_HD, :], pg) for g, pg in enumerate(per_group(pb))],
                               axis=1)

    qs = jnp.concatenate([q_ref[:, h * LANES:(h + 1) * LANES] for h in range(NSA_HEADS)], axis=0)
    u_row = lax.broadcasted_iota(jnp.int32, (1, width), 1) % QT
    t_row = t0 + u_row
    r_kc = lax.broadcasted_iota(jnp.int32, (KC, 1), 0)

    n_grp = ks_ref.shape[0] // SEL_GROUP
    n_full = lax.shift_right_logical(t0, int(math.log2(SEL_GROUP)))

    def qk_group(j):
        return _dot_nt(ks_ref[pl.ds(pl.multiple_of(j * SEL_GROUP, SEL_GROUP), SEL_GROUP), :], qs)

    sc = _dot_nt(kc_ref[...], qs)

    w_slabs, w_chunks = [], []
    for i in range(nw):
        jj = c - (nw - 1) + i
        jc = jnp.maximum(jj, 0)
        si = _dot_nt(kw_ref[pl.ds(pl.multiple_of(jc * KC, KC), KC), :], qs)
        if i == 0:
            keep = (r_kc > u_row) & (jj >= 0)
        elif i == nw - 1:
            keep = r_kc <= u_row
        else:
            keep = jj >= 0
        w_slabs.append(jnp.where(keep, si, NEG))
        w_chunks.append(jc)

    n_col = lax.broadcasted_iota(jnp.int32, (ncmp, 1), 0)
    cmask = (CMP_STRIDE * n_col + (CMP_LEN - 1) <= t_row) & (n_col < ncmp - 1)
    s = jnp.where(cmask, sc, NEG)
    m = jnp.max(s, axis=0, keepdims=True)
    e = jnp.exp2(s - m)
    anyv = (t_row >= CMP_LEN - 1).astype(F32)
    p = e * (anyv / jnp.sum(e, axis=0, keepdims=True))
    o_cmp = pv(vct_ref[...], p)

    psums = []
    for pg in per_group(p):
        acc_p = pg[:, 0:QT]
        for h in range(1, NSA_HPG):
            acc_p = acc_p + pg[:, h * QT:(h + 1) * QT]
        psums.append(acc_p)
    psum = jnp.concatenate(psums, axis=1)
    nq2 = NSA_KV * QT
    s_col = lax.broadcasted_iota(jnp.int32, (nsel, 1), 0)
    n_lane = lax.broadcasted_iota(jnp.int32, (1, ncmp), 1)
    ov = ((CMP_STRIDE * n_lane < SEL_BLOCK * (s_col + 1)) & (CMP_STRIDE * n_lane + (CMP_LEN - 1) >= SEL_BLOCK * s_col)
          ).astype(BF16)
    imp = sum(_dot(ov, part) for part in _split3(psum))

    s_diag = qk_group(n_full)
    sx_scr[...] = qk_group(0)

    mxw = w_slabs[0]
    for sl in w_slabs[1:]:
        mxw = jnp.maximum(mxw, sl)
    mw = jnp.max(mxw, axis=0, keepdims=True)
    pws = None
    acc_w = jnp.zeros((NSA_HD, width), F32)
    for sl, jc in zip(w_slabs, w_chunks):
        pw = jnp.exp2(sl - mw)
        pws = pw if pws is None else pws + pw
        acc_w = acc_w + pv(vwt_ref[jc], pw)
    o_win = acc_w / jnp.sum(pws, axis=0, keepdims=True)

    t1 = t0 + lax.broadcasted_iota(jnp.int32, (1, nq2), 1) % QT
    cur = lax.shift_right_logical(t1, 6)
    forced = (s_col == 0) | (s_col == cur) | (s_col == cur - 1)
    valid = SEL_BLOCK * s_col <= t1
    val = jnp.where(valid, jnp.where(forced, imp + SEL_FORCE, imp), NEG)
    sub = 8
    r_sub = lax.broadcasted_iota(jnp.int32, (sub, 1), 0)
    blocks = [val[r * sub:(r + 1) * sub, :] for r in range(nsel // sub)]
    ranks = [jnp.zeros((sub, nq2), F32) for _ in blocks]
    for i in range(nsel):
        vi = val[i:i + 1, :]
        for r, blk in enumerate(blocks):
            if i < r * sub:
                beats = vi >= blk
            elif i >= (r + 1) * sub:
                beats = vi > blk
            else:
                beats = (vi > blk) | ((vi == blk) & (r_sub > i - r * sub))
            ranks[r] = ranks[r] + jnp.where(beats, 1.0, 0.0)
    bias = jnp.where(jnp.concatenate(ranks, axis=0) < SEL_TOPK, 0.0, NEG).astype(F32)
    bias_scr[...] = jnp.concatenate([bias[:, g * QT:(g + 1) * QT] for g in range(NSA_KV) for _ in range(NSA_HPG)],
                                    axis=1)

    r_blk = lax.broadcasted_iota(jnp.int32, (SEL_BLOCK, 1), 0)
    blk_per_grp = SEL_GROUP // SEL_BLOCK
    chunk_per_grp = SEL_GROUP // KC
    blk_per_chunk = KC // SEL_BLOCK

    def sel_update(j, sj, carry, causal=False, live=None):
        m_o, l_o, acc = carry
        brows, keeps = [], []
        for i in range(blk_per_grp):
            brow = bias_scr[pl.ds(blk_per_grp * j + i, 1), :]
            brows.append(brow if live is None else jnp.where(live, brow, NEG))
            keeps.append(j * SEL_GROUP + i * SEL_BLOCK + r_blk <= t_row if causal else None)

        def block(i):
            return sj[i * SEL_BLOCK:(i + 1) * SEL_BLOCK, :]

        mx = None
        for i in range(blk_per_grp):
            sl = block(i) + brows[i]
            if causal:
                sl = jnp.where(keeps[i], sl, NEG)
            mx = sl if mx is None else jnp.maximum(mx, sl)
        m_n = jnp.maximum(m_o, jnp.max(mx, axis=0, keepdims=True))
        a = jnp.exp2(m_o - m_n)
        acc = a * acc
        psum = None
        for ci in range(chunk_per_grp):
            parts = []
            for i in range(blk_per_chunk * ci, blk_per_chunk * (ci + 1)):
                arg = block(i) + (brows[i] - m_n)
                parts.append(jnp.exp2(jnp.where(keeps[i], arg, NEG) if causal else arg))
            pj = jnp.concatenate(parts, axis=0)
            psum = pj if psum is None else psum + pj
            acc = acc + pv(vst_ref[chunk_per_grp * j + ci], pj)
        l_n = a * l_o + jnp.sum(psum, axis=0, keepdims=True)
        return m_n, l_n, acc

    empty = (jnp.full((1, width), NEG, F32), jnp.zeros((1, width), F32), jnp.zeros((NSA_HD, width), F32))
    seeded = sel_update(n_full, s_diag, empty, causal=True)

    def pair_body(jp, carry):
        ja, jb = 2 * jp, 2 * jp + 1
        sy_scr[...] = qk_group(jb)
        carry = sel_update(ja, sx_scr, carry)
        sx_scr[...] = qk_group(jnp.minimum(ja + 2, n_grp - 1))
        return sel_update(jb, sy_scr, carry, live=jb < n_full)

    _, l_s, acc_s = lax.fori_loop(0, lax.shift_right_logical(n_full + 1, 1), pair_body, seeded)
    o_sel = acc_s / l_s

    def gate_row(br):
        rows = [smt_ref[SM_GATE + 3 * h + br:SM_GATE + 3 * h + br + 1, :] for h in range(NSA_HEADS)]
        return _sigmoid(jnp.concatenate(rows, axis=1))

    o_t = gate_row(0) * o_cmp + gate_row(1) * o_sel + gate_row(2) * o_win
    for pr in range(NSA_HEADS // 2):
        xp = jnp.concatenate([o_t[:, (2 * pr) * QT:(2 * pr + 1) * QT], o_t[:, (2 * pr + 1) * QT:(2 * pr + 2) * QT]], axis=0)
        o_ref[:, pr * LANES:(pr + 1) * LANES] = xp.T.astype(BF16)


def _nsa(qpad, kcb, vct, ks, kw, vst, vwt, smt, b, s):
    nq = s // QT
    ncmp = kcb.shape[1]
    return pl.pallas_call(
        _nsa_kernel,
        out_shape=jax.ShapeDtypeStruct((b * s, NSA_HEADS * NSA_HD), BF16),
        grid=(b, nq),
        in_specs=[
            pl.BlockSpec((QT, NSA_HEADS * LANES), lambda bi, c: (bi * nq + c, 0)),
            pl.BlockSpec((None, ncmp, LANES), lambda bi, c: (bi, 0, 0)),
            pl.BlockSpec((None, NSA_KV * NSA_HD, ncmp), lambda bi, c: (bi, 0, 0)),
            pl.BlockSpec((s, LANES), lambda bi, c: (bi, 0)),
            pl.BlockSpec((s, LANES), lambda bi, c: (bi, 0)),
            pl.BlockSpec((s // KC, NSA_KV * NSA_HD, KC), lambda bi, c: (bi, 0, 0)),
            pl.BlockSpec((s // KC, NSA_KV * NSA_HD, KC), lambda bi, c: (bi, 0, 0)),
            pl.BlockSpec((LANES, QT), lambda bi, c: (0, bi * nq + c)),
        ],
        out_specs=pl.BlockSpec((QT, NSA_HEADS * NSA_HD), lambda bi, c: (bi * nq + c, 0)),
        scratch_shapes=[pltpu.VMEM((s // SEL_BLOCK, NSA_HEADS * QT), F32),
                        pltpu.VMEM((SEL_GROUP, NSA_HEADS * QT), F32),
                        pltpu.VMEM((SEL_GROUP, NSA_HEADS * QT), F32)],
        compiler_params=pltpu.CompilerParams(dimension_semantics=("arbitrary", "arbitrary"),
                                             vmem_limit_bytes=VMEM_LIMIT),
        name="nsa",
    )(qpad, kcb, vct, ks, kw, vst, vwt, smt)


def _log_sigmoid(x):
    return jnp.minimum(x, 0.0) - jnp.log(1.0 + jnp.exp(-jnp.abs(x)))


def _mlstm_kernel(qk_ref, v_ref, og_ref, sm_ref, smt_ref, wc_ref, bc_ref, bifc_ref, bifr_ref, ghn_ref,
                  y_ref, ext_scr, ct_scr, n_scr, m_scr):
    lc = ML_CHUNK

    @pl.when(pl.program_id(1) == 0)
    def _():
        ext_scr[0:8, :] = jnp.zeros((8, 2 * ML_WIDTH), F32)
        ct_scr[...] = jnp.zeros_like(ct_scr)
        n_scr[...] = jnp.zeros_like(n_scr)
        m_scr[...] = jnp.zeros_like(m_scr)

    ext_scr[8:8 + ML_BLOCK, :] = qk_ref[...].astype(F32)
    y = bc_ref[...]
    for j in range(CONV_W):
        y = y + wc_ref[j:j + 1, :] * ext_scr[pl.ds(8 - (CONV_W - 1) + j, ML_BLOCK), :]
    ext_scr[0:8, :] = ext_scr[ML_BLOCK:ML_BLOCK + 8, :]
    qkc = y * _sigmoid(y)
    q_all = qkc[:, 0:ML_WIDTH].astype(BF16)
    k_all = (qkc[:, ML_WIDTH:2 * ML_WIDTH] * (ML_HD ** -0.5)).astype(BF16)

    ifc = sm_ref[...] + bifc_ref[...]
    ifr = smt_ref[...] + bifr_ref[...]
    lfc = _log_sigmoid(ifc)
    lfr = _log_sigmoid(ifr)
    rr = lax.broadcasted_iota(jnp.int32, (lc, lc), 0)
    cc = lax.broadcasted_iota(jnp.int32, (lc, lc), 1)
    causal = rr >= cc
    tri_l = causal.astype(F32)
    tri_u = (rr <= cc).astype(F32)

    for ci in range(ML_BLOCK // lc):
        lo, hi = ci * lc, (ci + 1) * lc
        bc_all = jnp.dot(tri_l, lfc[lo:hi, :], preferred_element_type=F32, precision=lax.Precision.HIGHEST)
        br_all = jnp.dot(lfr[:, lo:hi], tri_u, preferred_element_type=F32, precision=lax.Precision.HIGHEST)
        for h in range(ML_HEADS):
            hs = slice(h * ML_HD, (h + 1) * ML_HD)
            bcol = bc_all[:, SM_F + h:SM_F + h + 1]
            brow = br_all[SM_F + h:SM_F + h + 1, :]
            icol = ifc[lo:hi, SM_I + h:SM_I + h + 1]
            irow = ifr[SM_I + h:SM_I + h + 1, lo:hi]
            mprev = m_scr[h][:, 0:1]
            qh = q_all[lo:hi, hs]
            kh = k_all[lo:hi, hs]
            vh = v_ref[lo:hi, hs]
            dmat = jnp.where(causal, bcol - brow + irow, NEG)
            inter = bcol + mprev
            mt = jnp.maximum(jnp.max(dmat, axis=-1, keepdims=True), inter)
            a = jnp.exp(dmat - mt) * _dot_nt(qh, kh)
            dec = jnp.exp(inter - mt)
            ct = ct_scr[h]
            nrow = n_scr[h]
            num = _dot(a.astype(BF16), vh) + dec * _dot(qh, ct.astype(BF16))
            den = jnp.sum(a, axis=-1, keepdims=True) + dec * jnp.sum(qh.astype(F32) * nrow, axis=-1, keepdims=True)
            hh = num / jnp.maximum(jnp.abs(den), jnp.exp(-mt))
            blast = bcol[lc - 1:lc, :]
            grow = blast - brow + irow
            mnew = jnp.maximum(blast + mprev, jnp.max(grow, axis=-1, keepdims=True))
            wprev = jnp.exp(blast + mprev - mnew)
            kwt = kh.astype(F32) * jnp.exp(blast - bcol + icol - mnew)
            ct_scr[h] = wprev * ct + lax.dot_general(kwt.astype(BF16), vh, _TN, preferred_element_type=F32)
            n_scr[h] = wprev * nrow + jnp.sum(kwt, axis=0, keepdims=True)
            m_scr[h] = jnp.broadcast_to(mnew, (1, LANES))
            hm = hh * _sigmoid(og_ref[lo:hi, hs].astype(F32))
            y_ref[lo:hi, hs] = _rms(hm, ghn_ref[:, hs]).astype(BF16)


def _mlstm(qkb, vb, ob, sm, smt, wconv, bconv, bifc, bifr, ghn, b, s):
    nb = s // ML_BLOCK
    row = lambda w: pl.BlockSpec((ML_BLOCK, w), lambda bi, j: (bi * nb + j, 0))
    return pl.pallas_call(
        _mlstm_kernel,
        out_shape=jax.ShapeDtypeStruct((b * s, ML_WIDTH), BF16),
        grid=(b, nb),
        in_specs=[row(2 * ML_WIDTH), row(ML_WIDTH), row(ML_WIDTH), row(LANES),
                  pl.BlockSpec((LANES, ML_BLOCK), lambda bi, j: (0, bi * nb + j)),
                  _const_spec(wconv.shape), _const_spec(bconv.shape), _const_spec(bifc.shape),
                  _const_spec(bifr.shape), _const_spec(ghn.shape)],
        out_specs=row(ML_WIDTH),
        scratch_shapes=[pltpu.VMEM((ML_BLOCK + 8, 2 * ML_WIDTH), F32),
                        pltpu.VMEM((ML_HEADS, ML_HD, ML_HD), F32),
                        pltpu.VMEM((ML_HEADS, 1, ML_HD), F32),
                        pltpu.VMEM((ML_HEADS, 1, LANES), F32)],
        compiler_params=pltpu.CompilerParams(dimension_semantics=("arbitrary", "arbitrary"),
                                             vmem_limit_bytes=VMEM_LIMIT),
        name="mlstm",
    )(qkb, vb, ob, sm, smt, wconv, bconv, bifc, bifr, ghn)


RT_BUCKET = N_EXPERTS
RT_RANK = N_EXPERTS + 1
RT_WLO = N_EXPERTS + 2
RT_WHI = N_EXPERTS + 3
N_BUCKETS = N_GROUPS * 6
X_ROWS = D_MODEL // LANES
REC = 2 * X_ROWS
DMA_UNROLL = 8


def _merge_kernel(ya_ref, yb_ref, mg_ref, x_ref, wpa_ref, wpb_ref, wout_ref, gffn_ref, wr_ref, br_ref,
                  x1_ref, hx_ref, slab_ref, hist_ref):
    pa = _dot(ya_ref[...], wpa_ref[...])
    pb = _dot(yb_ref[...], wpb_ref[...])
    ga = _sigmoid(mg_ref[:, 0:D_MODEL].astype(F32))
    gb = _sigmoid(mg_ref[:, D_MODEL:2 * D_MODEL].astype(F32))
    mixed = (ga * pa + gb * pb).astype(BF16)
    x1 = x_ref[...] + _dot(mixed, wout_ref[...])
    x1_ref[...] = x1
    h2 = _rms(x1, gffn_ref[...])
    h_hi = h2.astype(BF16)
    for j in range(X_ROWS):
        hx_ref[pl.ds(j, TM, stride=REC), :] = h2[:, j * LANES:(j + 1) * LANES]

    h_lo = (h2 - h_hi.astype(F32)).astype(BF16)
    r_hi = _dot(h_hi, wr_ref[...])
    logit = r_hi[:, 0:LANES] + r_hi[:, LANES:2 * LANES] + _dot(h_lo, wr_ref[:, 0:LANES]) + br_ref[...]
    lane = lax.broadcasted_iota(jnp.int32, logit.shape, 1)
    big = jnp.int32(LANES)
    gmask = (lane >= N_EXPERTS) & (lane < N_EXPERTS + N_GROUPS)
    gl = jnp.where(gmask, logit, NEG)
    gmax = jnp.max(gl, axis=-1, keepdims=True)
    gidx = jnp.min(jnp.where(gmask & (gl == gmax), lane, big), axis=-1, keepdims=True) - N_EXPERTS
    pg_sel = 1.0 / jnp.sum(jnp.where(gmask, jnp.exp(gl - gmax), 0.0), axis=-1, keepdims=True)
    emask = (lane < N_EXPERTS) & (lax.shift_right_logical(lane, 2) == gidx)
    el = jnp.where(emask, logit, NEG)
    e1 = jnp.max(el, axis=-1, keepdims=True)
    i1 = jnp.min(jnp.where(emask & (el == e1), lane, big), axis=-1, keepdims=True)
    emask2 = emask & (lane != i1)
    el2 = jnp.where(emask2, logit, NEG)
    e2 = jnp.max(el2, axis=-1, keepdims=True)
    i2 = jnp.min(jnp.where(emask2 & (el2 == e2), lane, big), axis=-1, keepdims=True)
    x21 = jnp.exp(e2 - e1)
    w1 = pg_sel / (1.0 + x21)
    w2 = pg_sel * x21 / (1.0 + x21)
    first_lo = i1 < i2
    e_lo = jnp.where(first_lo, i1, i2) - EXP_PER_GROUP * gidx
    e_hi = jnp.where(first_lo, i2, i1) - EXP_PER_GROUP * gidx
    pair = lax.shift_right_logical(e_lo * (2 * EXP_PER_GROUP - 1 - e_lo), 1) + (e_hi - e_lo - 1)
    bucket = 6 * gidx + pair
    member = lane == bucket
    onehot = jnp.where(member, 1.0, 0.0)
    rr = lax.broadcasted_iota(jnp.int32, (TM, TM), 0)
    cc = lax.broadcasted_iota(jnp.int32, (TM, TM), 1)
    earlier = _dot((rr > cc).astype(BF16), onehot.astype(BF16))
    rank = jnp.sum(jnp.where(member, earlier, 0.0), axis=-1, keepdims=True)
    slab = jnp.where(lane == i1, w1, 0.0) + jnp.where(lane == i2, w2, 0.0)
    slab = jnp.where(lane == RT_BUCKET, bucket.astype(F32), slab)
    slab = jnp.where(lane == RT_RANK, rank, slab)
    slab = jnp.where(lane == RT_WLO, jnp.where(first_lo, w1, w2), slab)
    slab = jnp.where(lane == RT_WHI, jnp.where(first_lo, w2, w1), slab)
    slab_ref[...] = slab
    hx_ref[pl.ds(X_ROWS, TM, stride=REC), :] = slab
    for j in range(X_ROWS + 1, REC):
        hx_ref[pl.ds(j, TM, stride=REC), :] = jnp.zeros((TM, LANES), F32)
    hist_ref[...] = jnp.broadcast_to(jnp.sum(onehot, axis=0, keepdims=True), (8, LANES))


def _merge(ya, yb, mg, x2, wpa, wpb, wout, gffn, wr, br):
    t = x2.shape[0]
    row = lambda w: pl.BlockSpec((TM, w), lambda i: (i, 0))
    return pl.pallas_call(
        _merge_kernel,
        out_shape=[jax.ShapeDtypeStruct((t, D_MODEL), F32),
                   jax.ShapeDtypeStruct((t * REC, LANES), F32),
                   jax.ShapeDtypeStruct((t, LANES), F32),
                   jax.ShapeDtypeStruct((t // TM * 8, LANES), F32)],
        grid=(t // TM,),
        in_specs=[row(NSA_HEADS * NSA_HD), row(ML_WIDTH), row(2 * D_MODEL), row(D_MODEL)]
                 + [_const_spec(a.shape) for a in (wpa, wpb, wout, gffn, wr, br)],
        out_specs=[row(D_MODEL), pl.BlockSpec((TM * REC, LANES), lambda i: (i, 0)), row(LANES),
                   pl.BlockSpec((8, LANES), lambda i: (i, 0))],
        compiler_params=pltpu.CompilerParams(dimension_semantics=("arbitrary",), vmem_limit_bytes=VMEM_LIMIT),
        name="merge",
    )(ya, yb, mg, x2, wpa, wpb, wout, gffn, wr, br)


def _rec_copy(src_ref, src_tok, dst_ref, dst_tok, sem, rows):
    src = src_ref.at[pl.ds(pl.multiple_of(src_tok * rows, rows), rows), :]
    dst = dst_ref.at[pl.ds(pl.multiple_of(dst_tok * rows, rows), rows), :]
    return pltpu.make_async_copy(src, dst, sem)


def _token_copies(n, make, wait=False):
    def body(g, carry):
        for u in range(DMA_UNROLL):
            cp = make(g * DMA_UNROLL + u)
            if wait:
                cp.wait()
            else:
                cp.start(priority=u % 2)
        return carry
    lax.fori_loop(0, n // DMA_UNROLL, body, 0)


def _dispatch_kernel(pos_ref, tail_ref, hx_ref, out_ref, stage_scr, zero_scr, sem, zsem):
    i = pl.program_id(0)
    slot = lax.rem(i, 2)
    n_tiles = out_ref.shape[0] // (TM * REC)
    n_used = tail_ref[2 * N_BUCKETS]

    def zero_copy(first_slot):
        start = pl.multiple_of(first_slot * REC, TM * REC)
        return pltpu.make_async_copy(zero_scr, out_ref.at[pl.ds(start, TM * REC), :], zsem)

    @pl.when(i == 0)
    def _():
        zero_scr[...] = jnp.zeros_like(zero_scr)
        for phase in ("start", "wait"):
            for b in range(N_BUCKETS):
                @pl.when(tail_ref[N_BUCKETS + b] > 0)
                def _():
                    getattr(zero_copy(tail_ref[b]), phase)()

                @pl.when(n_used + b < n_tiles)
                def _():
                    getattr(zero_copy((n_used + b) * TM), phase)()

    stage_scr[slot] = hx_ref[...]
    base = i * TM
    _token_copies(TM, lambda r: _rec_copy(stage_scr.at[slot], r, out_ref, pos_ref[base + r], sem.at[slot], REC))

    def drain(which):
        _token_copies(TM, lambda r: _rec_copy(stage_scr.at[which], 0, out_ref, 0, sem.at[which], REC), wait=True)

    @pl.when(i > 0)
    def _():
        drain(1 - slot)

    @pl.when(i == pl.num_programs(0) - 1)
    def _():
        drain(slot)


def _dispatch(pos, tail, hx, n_slots):
    t = hx.shape[0] // REC
    return pl.pallas_call(
        _dispatch_kernel,
        out_shape=jax.ShapeDtypeStruct((n_slots * REC, LANES), F32),
        grid_spec=pltpu.PrefetchScalarGridSpec(
            num_scalar_prefetch=2,
            grid=(t // TM,),
            in_specs=[pl.BlockSpec((TM * REC, LANES), lambda i, pos_r, tail_r: (i, 0))],
            out_specs=pl.BlockSpec(memory_space=pl.ANY),
            scratch_shapes=[pltpu.VMEM((2, TM * REC, LANES), F32), pltpu.VMEM((TM * REC, LANES), F32),
                            pltpu.SemaphoreType.DMA((2,)), pltpu.SemaphoreType.DMA(())],
        ),
        compiler_params=pltpu.CompilerParams(dimension_semantics=("arbitrary",), vmem_limit_bytes=VMEM_LIMIT,
                                             has_side_effects=True),
        name="dispatch",
    )(pos, tail, hx)


def _moe_kernel(te_ref, nu_ref, hx_ref, w13_ref, w2_ref, y_ref):
    k = pl.program_id(0)
    n_tiles = pl.num_programs(0)
    n_used = nu_ref[0]

    @pl.when(k < n_used)
    def _():
        h = jnp.concatenate([hx_ref[pl.ds(j, TM, stride=REC), :] for j in range(X_ROWS)], axis=1).astype(BF16)
        slab = hx_ref[pl.ds(X_ROWS, TM, stride=REC), :]
        y = None
        for side, lane in ((0, RT_WLO), (1, RT_WHI)):
            e = te_ref[side * n_tiles + k]
            a = _dot(h, w13_ref[e])
            gt = a[:, 0:D_EXPERT]
            act = gt * _sigmoid(gt) * a[:, D_EXPERT:2 * D_EXPERT] * slab[:, lane:lane + 1]
            part = _dot(act.astype(BF16), w2_ref[e])
            y = part if y is None else y + part
        for j in range(X_ROWS):
            y_ref[pl.ds(j, TM, stride=X_ROWS), :] = y[:, j * LANES:(j + 1) * LANES]

    @pl.when(k >= n_used)
    def _():
        y_ref[...] = jnp.zeros_like(y_ref)


def _moe(tile_e, n_used, hx_sorted, w13, w2):
    n_tiles = hx_sorted.shape[0] // (TM * REC)
    return pl.pallas_call(
        _moe_kernel,
        out_shape=jax.ShapeDtypeStruct((n_tiles * TM * X_ROWS, LANES), F32),
        grid_spec=pltpu.PrefetchScalarGridSpec(
            num_scalar_prefetch=2,
            grid=(n_tiles,),
            in_specs=[pl.BlockSpec((TM * REC, LANES), lambda k, te, nu: (jnp.minimum(k, nu[0] - 1), 0)),
                      pl.BlockSpec(w13.shape, lambda k, te, nu: (0, 0, 0), pipeline_mode=pl.Buffered(1)),
                      pl.BlockSpec(w2.shape, lambda k, te, nu: (0, 0, 0), pipeline_mode=pl.Buffered(1))],
            out_specs=pl.BlockSpec((TM * X_ROWS, LANES), lambda k, te, nu: (k, 0)),
        ),
        compiler_params=pltpu.CompilerParams(dimension_semantics=("arbitrary",), vmem_limit_bytes=VMEM_LIMIT),
        name="moe",
    )(tile_e, n_used, hx_sorted, w13, w2)


def _combine_kernel(pos_ref, y_ref, x1_ref, p_ref, gple_ref, wpg_ref, wpp_ref, gfin_ref, o_ref, ybuf, sem):
    i = pl.program_id(0)
    slot = lax.rem(i, 2)

    def gather(tile, which):
        _token_copies(TM, lambda r: _rec_copy(y_ref, pos_ref[tile * TM + r], ybuf.at[which], r, sem.at[which],
                                              X_ROWS))

    @pl.when(i == 0)
    def _():
        gather(0, 0)

    @pl.when(i + 1 < pl.num_programs(0))
    def _():
        gather(i + 1, 1 - slot)

    _token_copies(TM, lambda r: _rec_copy(y_ref, 0, ybuf.at[slot], 0, sem.at[slot], X_ROWS), wait=True)
    yb = ybuf.at[slot]
    x2 = x1_ref[...] + jnp.concatenate([yb[pl.ds(j, TM, stride=X_ROWS), :] for j in range(X_ROWS)], axis=1)
    h3 = _rms(x2, gple_ref[...]).astype(BF16)
    x3 = x2 + _sigmoid(_dot(h3, wpg_ref[...])) * _dot(p_ref[...].astype(BF16), wpp_ref[...])
    o_ref[...] = _rms(x3, gfin_ref[...])


def _combine(pos, y_sorted, x1, p2, gple, wpg, wpp, gfin):
    t = x1.shape[0]
    row = lambda w: pl.BlockSpec((TM, w), lambda i, pos_r: (i, 0))
    const = lambda a: pl.BlockSpec(a.shape, lambda i, pos_r: (0,) * a.ndim, pipeline_mode=pl.Buffered(1))
    return pl.pallas_call(
        _combine_kernel,
        out_shape=jax.ShapeDtypeStruct((t, D_MODEL), F32),
        grid_spec=pltpu.PrefetchScalarGridSpec(
            num_scalar_prefetch=1,
            grid=(t // TM,),
            in_specs=[pl.BlockSpec(memory_space=pl.ANY), row(D_MODEL), row(PLE_DIM),
                      const(gple), const(wpg), const(wpp), const(gfin)],
            out_specs=row(D_MODEL),
            scratch_shapes=[pltpu.VMEM((2, TM * X_ROWS, LANES), F32), pltpu.SemaphoreType.DMA((2,))],
        ),
        compiler_params=pltpu.CompilerParams(dimension_semantics=("arbitrary",), vmem_limit_bytes=VMEM_LIMIT),
        name="combine",
    )(pos, y_sorted, x1, p2, gple, wpg, wpp, gfin)


def _routing_tables(slab, hist8):
    t = slab.shape[0]
    nt = t // TM
    n_tiles = nt + N_BUCKETS
    hist = hist8.reshape(nt, 8, LANES)[:, 0, :]
    counts = jnp.sum(hist, axis=0)
    padded = jnp.ceil(counts / TM) * TM
    ends = jnp.cumsum(padded)
    first = (ends - padded)[None, :] + jnp.cumsum(hist, axis=0) - hist
    lane = jnp.arange(LANES, dtype=F32)[None, :]
    mine = lane == slab[:, RT_BUCKET:RT_BUCKET + 1]
    pos = jnp.sum(jnp.where(mine, jnp.repeat(first, TM, axis=0), 0.0), axis=1) + slab[:, RT_RANK]
    starts = jnp.arange(n_tiles, dtype=F32) * TM
    tile_bucket = jnp.minimum(jnp.sum(ends[None, :N_BUCKETS] <= starts[:, None], axis=1), N_BUCKETS - 1)
    group, pair = tile_bucket // 6, tile_bucket % 6
    e_lo = EXP_PER_GROUP * group + jnp.array([0, 0, 0, 1, 1, 2], jnp.int32)[pair]
    e_hi = EXP_PER_GROUP * group + jnp.array([1, 2, 3, 2, 3, 3], jnp.int32)[pair]
    tile_e = jnp.concatenate([e_lo, e_hi]).astype(jnp.int32)
    n_used = (ends[N_BUCKETS - 1] / TM).astype(jnp.int32).reshape(1)
    tail = jnp.concatenate([(ends - TM)[:N_BUCKETS], padded[:N_BUCKETS], n_used.astype(F32)]).astype(jnp.int32)
    return pos.astype(jnp.int32), tile_e, n_used, tail, n_tiles * TM


def _pack_inproj_weights(w):
    d = w.shape[0]
    qw = NSA_HEADS * NSA_HD
    kvw = NSA_KV * NSA_HD
    o = 0
    wq = w[:, o:o + qw]; o += qw
    wkc = w[:, o:o + kvw]; o += kvw
    wvc = w[:, o:o + kvw]; o += kvw
    wks = w[:, o:o + kvw]; o += kvw
    wvs = w[:, o:o + kvw]; o += kvw
    wkw = w[:, o:o + kvw]; o += kvw
    wvw = w[:, o:o + kvw]; o += kvw
    wga = w[:, o:o + 3 * NSA_HEADS]; o += 3 * NSA_HEADS
    wqkb = w[:, o:o + 2 * ML_WIDTH]; o += 2 * ML_WIDTH
    wvb = w[:, o:o + ML_WIDTH]; o += ML_WIDTH
    wob = w[:, o:o + ML_WIDTH]; o += ML_WIDTH
    wif = w[:, o:o + 2 * ML_HEADS]; o += 2 * ML_HEADS
    wmg = w[:, o:o + 2 * D_MODEL]
    zero64 = jnp.zeros((d, NSA_HD), w.dtype)
    qcols = []
    for h in range(NSA_HEADS):
        wh = wq[:, h * NSA_HD:(h + 1) * NSA_HD]
        qcols += [wh, zero64] if h // NSA_HPG == 0 else [zero64, wh]
    wsm = jnp.concatenate([wga, wif, jnp.zeros((d, LANES - 3 * NSA_HEADS - 2 * ML_HEADS), w.dtype)], axis=1)
    wcat = jnp.concatenate(qcols + [wkc, wvc, wks, wkw, wsm, wqkb, wvb, wob, wmg], axis=1).astype(BF16)
    wtr = jnp.concatenate([wvs, wvw, wsm], axis=1).T.astype(BF16)
    return wcat, wtr


def _rope_lane_tables(positions):
    inv = ROPE_THETA ** (-jnp.arange(0, ROPE_DIM, 2, dtype=F32) / ROPE_DIM)
    ang = positions.astype(F32).reshape(-1, 1) * inv[None, :]
    cos, sin = jnp.cos(ang), jnp.sin(ang)
    half = ROPE_DIM // 2
    d = jnp.arange(LANES) % NSA_HD
    cos_l, sin_l = jnp.tile(cos, (1, LANES // half)), jnp.tile(sin, (1, LANES // half))
    rc = jnp.where(d < ROPE_DIM, cos_l, 1.0)
    rp = jnp.where((d >= half) & (d < ROPE_DIM), sin_l, 0.0)
    rm = jnp.where(d < half, -sin_l, 0.0)
    return rc, rp, rm


def _pack_compress_weights(w1, w2, pe):
    half = CMP_LEN // 2
    w1r = w1.reshape(2, half, NSA_HD, CMP_HIDDEN)
    outs = []
    for part in range(2):
        wb = w1r[part].astype(BF16)
        zb = jnp.zeros_like(wb)
        wp = jnp.stack([jnp.stack([wb, zb], axis=2), jnp.stack([zb, wb], axis=2)], axis=1)
        outs.append(wp.reshape(half * NSA_KV * NSA_HD, NSA_KV * CMP_HIDDEN))
    pe8 = jnp.broadcast_to(pe.reshape(1, CMP_LEN * NSA_HD), (8, CMP_LEN * NSA_HD)).astype(BF16)
    return outs[0], outs[1], pe8, w1.astype(BF16)


def _stages(x, p, positions, g_mix, w_in, b_if, w_ck1, w_ck2, pe_ck, w_cv1, w_cv2, pe_cv, w_conv, b_conv, g_hn, w_pa, w_pb, w_out, g_ffn, w_rg, b_rg, w_re, b_re, w_e13, w_e2, g_ple, w_pg, w_pp, g_final):
    b, s, d = x.shape
    t = b * s
    rc, rp, rm = _rope_lane_tables(positions)
    assert w_in.shape[0] == 1, "the final norm is fused into the layer's last kernel: single-layer problem only"
    for i in range(w_in.shape[0]):
        x2 = x.reshape(t, d)
        wcat, wtr = _pack_inproj_weights(w_in[i])
        (qpad, kc_tok, vc_tok, ks, kw, vst, vwt, sm, smt, qkb, vb, ob, mg) = _inproj(
            x2, g_mix[i].reshape(1, d), wcat, wtr, rc, rp, rm)
        wka, wkb, pek, w1k = _pack_compress_weights(w_ck1[i], w_ck2[i], pe_ck[i])
        wva, wvb, pev, w1v = _pack_compress_weights(w_cv1[i], w_cv2[i], pe_cv[i])
        zpad = jnp.zeros((CMP_HIDDEN, NSA_HD), F32)
        w2k = jnp.stack([jnp.concatenate([w_ck2[i], zpad], axis=1),
                         jnp.concatenate([zpad, w_ck2[i]], axis=1)]).astype(BF16)
        w2vt = w_cv2[i].T.astype(BF16)
        nrow = s // CMP_STRIDE
        rk = kc_tok.reshape(b, nrow, CMP_STRIDE * LANES)
        rv = vc_tok.reshape(b, nrow, CMP_STRIDE * LANES)
        kcb, vct = _compress(rk, rv, wka, wkb, wva, wvb, pek, pev, w1k, w1v, w2k, w2vt)
        ya = _nsa(qpad, kcb, vct, ks, kw, vst, vwt, smt, b, s)
        bif = b_if[i].astype(F32)
        bifc = jnp.zeros((1, LANES), F32).at[0, SM_I:SM_I + 2 * ML_HEADS].set(bif)
        bifr = bifc.reshape(LANES, 1)
        yb = _mlstm(qkb, vb, ob, sm, smt, w_conv[i], b_conv[i].reshape(1, -1), bifc, bifr,
                    g_hn[i].reshape(1, -1), b, s)
        wr = jnp.concatenate([w_re[i], w_rg[i], jnp.zeros((d, LANES - N_EXPERTS - N_GROUPS), F32)], axis=1)
        wr_hi = wr.astype(BF16)
        wr = jnp.concatenate([wr_hi, (wr - wr_hi.astype(F32)).astype(BF16)], axis=1)
        br =jnp.concatenate([b_re[i], b_rg[i], jnp.zeros((LANES - N_EXPERTS - N_GROUPS,), F32)]).reshape(1, LANES)
        x1, hx, slab, hist8 = _merge(ya, yb, mg, x2, w_pa[i].astype(BF16), w_pb[i].astype(BF16),
                                     w_out[i].astype(BF16), g_ffn[i].reshape(1, d), wr, br)
        pos, tile_e, n_used, tail, n_slots = _routing_tables(slab, hist8)
        hx_sorted = _dispatch(pos, tail, hx, n_slots)
        y_sorted = _moe(tile_e, n_used, hx_sorted, w_e13[i].astype(BF16), w_e2[i].astype(BF16))
        out = _combine(pos, y_sorted, x1, p[i].reshape(t, PLE_DIM), g_ple[i].reshape(1, d), w_pg[i].astype(BF16),
                       w_pp[i].astype(BF16), g_final.reshape(1, d))
        x = out.reshape(b, s, d)
    return dict(out=x, qpad=qpad, ks=ks, kcb=kcb, vct=vct, y_a=ya, y_b=yb, x1=x1, hx=hx, pos=pos)


def kernel(x, p, positions, g_mix, w_in, b_if, w_ck1, w_ck2, pe_ck, w_cv1, w_cv2, pe_cv, w_conv, b_conv, g_hn, w_pa, w_pb, w_out, g_ffn, w_rg, b_rg, w_re, b_re, w_e13, w_e2, g_ple, w_pg, w_pp, g_final):
    return _stages(x, p, positions, g_mix, w_in, b_if, w_ck1, w_ck2, pe_ck, w_cv1, w_cv2, pe_cv, w_conv, b_conv, g_hn,
                   w_pa, w_pb, w_out, g_ffn, w_rg, b_rg, w_re, b_re, w_e13, w_e2, g_ple, w_pg, w_pp, g_final)["out"]
```

```python
import functools
import math

import jax
import jax.numpy as jnp
from jax import lax
from jax.experimental import pallas as pl
from jax.experimental.pallas import tpu as pltpu

F32 = jnp.float32
BF16 = jnp.bfloat16

EPS = 1e-6
NEG = -1e30

D_MODEL = 1024
PLE_DIM = 256
NSA_HEADS = 8
NSA_KV = 2
NSA_HPG = NSA_HEADS // NSA_KV
NSA_HD = 64
CMP_LEN = 32
CMP_STRIDE = 16
CMP_HIDDEN = 256
SEL_BLOCK = 64
SEL_TOPK = 16
SEL_FORCE = 1000.0
WINDOW = 512
ROPE_THETA = 500000.0
ROPE_DIM = NSA_HD // 4
ML_HEADS = 4
ML_HD = 128
ML_WIDTH = ML_HEADS * ML_HD
CONV_W = 4
N_GROUPS = 4
EXP_PER_GROUP = 4
N_EXPERTS = N_GROUPS * EXP_PER_GROUP
D_EXPERT = 256

LANES = 128
QT = 128
KC = 128
SEL_GROUP = 512
VT_PAD = 16
VT_ROWS = NSA_HD + VT_PAD
ML_CHUNK = 128
ML_BLOCK = 256
TM = 256
VMEM_LIMIT = 56 * 1024 * 1024

_NT = (((1,), (1,)), ((), ()))
_TN = (((0,), (0,)), ((), ()))

SM_GATE = 0
SM_I = 3 * NSA_HEADS
SM_F = SM_I + ML_HEADS


def _dot(a, b):
    return jnp.dot(a, b, preferred_element_type=F32)


def _dot_nt(a, b):
    return lax.dot_general(a, b, _NT, preferred_element_type=F32)


def _split3(x):
    hi = x.astype(BF16)
    r1 = x - hi.astype(F32)
    mid = r1.astype(BF16)
    lo = (r1 - mid.astype(F32)).astype(BF16)
    return hi, mid, lo


def _rms(x, g):
    return x * lax.rsqrt(jnp.mean(x * x, axis=-1, keepdims=True) + EPS) * g


def _sigmoid(x):
    return 0.5 + 0.5 * jnp.tanh(0.5 * x)


def _const_spec(shape):
    nd = len(shape)
    return pl.BlockSpec(shape, lambda *_: (0,) * nd, pipeline_mode=pl.Buffered(1))


_C_Q = 0
_C_KC = _C_Q + NSA_HEADS * LANES
_C_VC = _C_KC + LANES
_C_KS = _C_VC + LANES
_C_KW = _C_KS + LANES
_C_SM = _C_KW + LANES
_C_QKB = _C_SM + LANES
_C_VB = _C_QKB + 2 * ML_WIDTH
_C_OB = _C_VB + ML_WIDTH
_C_MG = _C_OB + ML_WIDTH
_C_END = _C_MG + 2 * D_MODEL


def _inproj_kernel(x_ref, g_ref, w_ref, wt_ref, rc_ref, rp_ref, rm_ref,
                   q_ref, kc_ref, vc_ref, ks_ref, kw_ref, vst_ref, vwt_ref, sm_ref, smt_ref,
                   qkb_ref, vb_ref, ob_ref, mg_ref):
    hn = _rms(x_ref[...], g_ref[...]).astype(BF16)
    rc, rp, rm = rc_ref[...], rp_ref[...], rm_ref[...]

    def rope(z):
        return z * rc + pltpu.roll(z, 8, 1) * rp + pltpu.roll(z, LANES - 8, 1) * rm

    scale = NSA_HD ** -0.5 * math.log2(math.e)
    for h in range(NSA_HEADS):
        z = _dot(hn, w_ref[:, _C_Q + h * LANES:_C_Q + (h + 1) * LANES])
        q_ref[:, h * LANES:(h + 1) * LANES] = (rope(z) * scale).astype(BF16)
    kc_ref[...] = rope(_dot(hn, w_ref[:, _C_KC:_C_KC + LANES])).astype(BF16)
    vc_ref[...] = _dot(hn, w_ref[:, _C_VC:_C_VC + LANES]).astype(BF16)
    ks_ref[...] = rope(_dot(hn, w_ref[:, _C_KS:_C_KS + LANES])).astype(BF16)
    kw_ref[...] = rope(_dot(hn, w_ref[:, _C_KW:_C_KW + LANES])).astype(BF16)
    sm_ref[...] = _dot(hn, w_ref[:, _C_SM:_C_SM + LANES])
    for c0 in range(0, 2 * ML_WIDTH, 512):
        qkb_ref[:, c0:c0 + 512] = _dot(hn, w_ref[:, _C_QKB + c0:_C_QKB + c0 + 512]).astype(BF16)
    vb_ref[...] = _dot(hn, w_ref[:, _C_VB:_C_VB + ML_WIDTH]).astype(BF16)
    ob_ref[...] = _dot(hn, w_ref[:, _C_OB:_C_OB + ML_WIDTH]).astype(BF16)
    for c0 in range(0, 2 * D_MODEL, 512):
        mg_ref[:, c0:c0 + 512] = _dot(hn, w_ref[:, _C_MG + c0:_C_MG + c0 + 512]).astype(BF16)
    zt = _dot_nt(wt_ref[...], hn)
    ones_rows = (lax.broadcasted_iota(jnp.int32, (VT_PAD, KC), 0) == 0).astype(BF16)
    for i in range(TM // KC):
        for ref, r0 in ((vst_ref, 0), (vwt_ref, LANES)):
            zc = zt[r0:r0 + LANES, i * KC:(i + 1) * KC].astype(BF16)
            ref[i] = jnp.concatenate([piece for g in range(NSA_KV)
                                      for piece in (zc[g * NSA_HD:(g + 1) * NSA_HD, :], ones_rows)], axis=0)
    smt_ref[...] = zt[2 * LANES:3 * LANES, :]


def _inproj(x2, g_mix, wcat, wtr, rc, rp, rm):
    t = x2.shape[0]
    row = lambda w: pl.BlockSpec((TM, w), lambda i: (i, 0))
    out_shape = [
        jax.ShapeDtypeStruct((t, NSA_HEADS * LANES), BF16),
        jax.ShapeDtypeStruct((t, LANES), BF16),
        jax.ShapeDtypeStruct((t, LANES), BF16),
        jax.ShapeDtypeStruct((t, LANES), BF16),
        jax.ShapeDtypeStruct((t, LANES), BF16),
        jax.ShapeDtypeStruct((t // KC, NSA_KV * VT_ROWS, KC), BF16),
        jax.ShapeDtypeStruct((t // KC, NSA_KV * VT_ROWS, KC), BF16),
        jax.ShapeDtypeStruct((t, LANES), F32),
        jax.ShapeDtypeStruct((LANES, t), F32),
        jax.ShapeDtypeStruct((t, 2 * ML_WIDTH), BF16),
        jax.ShapeDtypeStruct((t, ML_WIDTH), BF16),
        jax.ShapeDtypeStruct((t, ML_WIDTH), BF16),
        jax.ShapeDtypeStruct((t, 2 * D_MODEL), BF16),
    ]
    chunk3 = pl.BlockSpec((TM // KC, NSA_KV * VT_ROWS, KC), lambda i: (i, 0, 0))
    out_specs = [row(NSA_HEADS * LANES), row(LANES), row(LANES), row(LANES), row(LANES), chunk3, chunk3,
                 row(LANES), pl.BlockSpec((LANES, TM), lambda i: (0, i)),
                 row(2 * ML_WIDTH), row(ML_WIDTH), row(ML_WIDTH), row(2 * D_MODEL)]
    return pl.pallas_call(
        _inproj_kernel,
        out_shape=out_shape,
        grid=(t // TM,),
        in_specs=[row(D_MODEL), _const_spec((1, D_MODEL)), _const_spec((D_MODEL, _C_END)),
                  _const_spec((3 * LANES, D_MODEL)), row(LANES), row(LANES), row(LANES)],
        out_specs=out_specs,
        compiler_params=pltpu.CompilerParams(dimension_semantics=("arbitrary",), vmem_limit_bytes=VMEM_LIMIT),
        name="inproj",
    )(x2, g_mix, wcat, wtr, rc, rp, rm)


def _gelu_tanh(x):
    return 0.5 * x * (1.0 + jnp.tanh(math.sqrt(2.0 / math.pi) * (x + 0.044715 * x * x * x)))


def _compress_kernel(rk_ref, rv_ref, wka_ref, wkb_ref, wva_ref, wvb_ref, pek_ref, pev_ref,
                     w1k_ref, w1v_ref, w2k_ref, w2vt_ref, kc_ref, vct_ref):
    nrow = rk_ref.shape[0]

    def hidden(r_ref, wa_ref, wb_ref, pe_ref, w1_ref):
        r = r_ref[...]
        ha = _dot(r, wa_ref[...])
        hb = _dot(r, wb_ref[...])
        hb = pltpu.roll(hb, nrow - 1, 0)
        c = _dot(pe_ref[...], w1_ref[...])[0:1, :]
        return [_gelu_tanh(ha[:, g * CMP_HIDDEN:(g + 1) * CMP_HIDDEN] + hb[:, g * CMP_HIDDEN:(g + 1) * CMP_HIDDEN] + c).astype(BF16)
                for g in range(NSA_KV)]

    ak = hidden(rk_ref, wka_ref, wkb_ref, pek_ref, w1k_ref)
    kc_ref[...] = (_dot(ak[0], w2k_ref[0]) + _dot(ak[1], w2k_ref[1])).astype(BF16)
    av = hidden(rv_ref, wva_ref, wvb_ref, pev_ref, w1v_ref)
    for g in range(NSA_KV):
        vct_ref[g * NSA_HD:(g + 1) * NSA_HD, :] = _dot_nt(w2vt_ref[...], av[g]).astype(BF16)


def _compress(rk, rv, wka, wkb, wva, wvb, pek, pev, w1k, w1v, w2k, w2vt):
    b, nrow, width = rk.shape
    blk = pl.BlockSpec((None, nrow, width), lambda i: (i, 0, 0))
    return pl.pallas_call(
        _compress_kernel,
        out_shape=[jax.ShapeDtypeStruct((b, nrow, LANES), BF16),
                   jax.ShapeDtypeStruct((b, LANES, nrow), BF16)],
        grid=(b,),
        in_specs=[blk, blk] + [_const_spec(a.shape) for a in (wka, wkb, wva, wvb, pek, pev, w1k, w1v, w2k, w2vt)],
        out_specs=[pl.BlockSpec((None, nrow, LANES), lambda i: (i, 0, 0)),
                   pl.BlockSpec((None, LANES, nrow), lambda i: (i, 0, 0))],
        compiler_params=pltpu.CompilerParams(dimension_semantics=("arbitrary",), vmem_limit_bytes=VMEM_LIMIT),
        name="compress",
    )(rk, rv, wka, wkb, wva, wvb, pek, pev, w1k, w1v, w2k, w2vt)


def _nsa_kernel(q_ref, kc_ref, vct_ref, ks_ref, kw_ref, vst_ref, vwt_ref, smt_ref, o_ref, bias_scr, sx_scr, sy_scr):
    c = pl.program_id(1)
    t0 = c * QT
    ncmp = kc_ref.shape[0]
    nsel = bias_scr.shape[0]
    nw = WINDOW // KC + 1
    gw = NSA_HPG * QT
    width = NSA_KV * gw

    def per_group(x):
        return [x[:, g * gw:(g + 1) * gw] for g in range(NSA_KV)]

    def pv(vt, p):
        rows = vt.shape[0] // NSA_KV
        pb = p.astype(BF16)
        return jnp.concatenate([_dot(vt[g * rows:(g + 1) * rows, :], pg) for g, pg in enumerate(per_group(pb))],
                               axis=1)

    def normalised(acc):
        return acc[0:NSA_HD, :] / acc[NSA_HD:NSA_HD + 1, :]

    qs = jnp.concatenate([q_ref[:, h * LANES:(h + 1) * LANES] for h in range(NSA_HEADS)], axis=0)
    u_row = lax.broadcasted_iota(jnp.int32, (1, width), 1) % QT
    t_row = t0 + u_row
    r_kc = lax.broadcasted_iota(jnp.int32, (KC, 1), 0)

    n_grp = ks_ref.shape[0] // SEL_GROUP
    n_full = lax.shift_right_logical(t0, int(math.log2(SEL_GROUP)))

    def qk_group(j):
        return _dot_nt(ks_ref[pl.ds(pl.multiple_of(j * SEL_GROUP, SEL_GROUP), SEL_GROUP), :], qs)

    sc = _dot_nt(kc_ref[...], qs)

    w_slabs, w_chunks = [], []
    for i in range(nw):
        jj = c - (nw - 1) + i
        jc = jnp.maximum(jj, 0)
        si = _dot_nt(kw_ref[pl.ds(pl.multiple_of(jc * KC, KC), KC), :], qs)
        if i == 0:
            keep = (r_kc > u_row) & (jj >= 0)
        elif i == nw - 1:
            keep = r_kc <= u_row
        else:
            keep = jj >= 0
        w_slabs.append(jnp.where(keep, si, NEG))
        w_chunks.append(jc)

    n_col = lax.broadcasted_iota(jnp.int32, (ncmp, 1), 0)
    cmask = (CMP_STRIDE * n_col + (CMP_LEN - 1) <= t_row) & (n_col < ncmp - 1)
    s = jnp.where(cmask, sc, NEG)
    m = jnp.max(s, axis=0, keepdims=True)
    e = jnp.exp2(s - m)
    anyv = (t_row >= CMP_LEN - 1).astype(F32)
    p = e * (anyv / jnp.sum(e, axis=0, keepdims=True))
    o_cmp = pv(vct_ref[...], p)

    psums = []
    for pg in per_group(p):
        acc_p = pg[:, 0:QT]
        for h in range(1, NSA_HPG):
            acc_p = acc_p + pg[:, h * QT:(h + 1) * QT]
        psums.append(acc_p)
    psum = jnp.concatenate(psums, axis=1)
    nq2 = NSA_KV * QT
    s_col = lax.broadcasted_iota(jnp.int32, (nsel, 1), 0)
    n_lane = lax.broadcasted_iota(jnp.int32, (1, ncmp), 1)
    ov = ((CMP_STRIDE * n_lane < SEL_BLOCK * (s_col + 1)) & (CMP_STRIDE * n_lane + (CMP_LEN - 1) >= SEL_BLOCK * s_col)
          ).astype(BF16)
    imp = sum(_dot(ov, part) for part in _split3(psum))

    s_diag = qk_group(n_full)
    sx_scr[...] = qk_group(0)

    mxw = w_slabs[0]
    for sl in w_slabs[1:]:
        mxw = jnp.maximum(mxw, sl)
    mw = jnp.max(mxw, axis=0, keepdims=True)
    acc_w = jnp.zeros((VT_ROWS, width), F32)
    for sl, jc in zip(w_slabs, w_chunks):
        acc_w = acc_w + pv(vwt_ref[jc], jnp.exp2(sl - mw))
    o_win = normalised(acc_w)

    t1 = t0 + lax.broadcasted_iota(jnp.int32, (1, nq2), 1) % QT
    cur = lax.shift_right_logical(t1, 6)
    forced = (s_col == 0) | (s_col == cur) | (s_col == cur - 1)
    valid = SEL_BLOCK * s_col <= t1
    val = jnp.where(valid, jnp.where(forced, imp + SEL_FORCE, imp), NEG)
    sub = 8
    r_sub = lax.broadcasted_iota(jnp.int32, (sub, 1), 0)
    blocks = [val[r * sub:(r + 1) * sub, :] for r in range(nsel // sub)]
    ranks = [jnp.zeros((sub, nq2), F32) for _ in blocks]
    for i in range(nsel):
        vi = val[i:i + 1, :]
        for r, blk in enumerate(blocks):
            if i < r * sub:
                beats = vi >= blk
            elif i >= (r + 1) * sub:
                beats = vi > blk
            else:
                beats = (vi > blk) | ((vi == blk) & (r_sub > i - r * sub))
            ranks[r] = ranks[r] + jnp.where(beats, 1.0, 0.0)
    bias = jnp.where(jnp.concatenate(ranks, axis=0) < SEL_TOPK, 0.0, NEG).astype(F32)
    bias_scr[...] = jnp.concatenate([bias[:, g * QT:(g + 1) * QT] for g in range(NSA_KV) for _ in range(NSA_HPG)],
                                    axis=1)

    r_blk = lax.broadcasted_iota(jnp.int32, (SEL_BLOCK, 1), 0)
    blk_per_grp = SEL_GROUP // SEL_BLOCK
    chunk_per_grp = SEL_GROUP // KC
    blk_per_chunk = KC // SEL_BLOCK

    def sel_update(j, sj, carry, causal=False, live=None):
        m_o, acc = carry
        brows, keeps = [], []
        for i in range(blk_per_grp):
            brow = bias_scr[pl.ds(blk_per_grp * j + i, 1), :]
            brows.append(brow if live is None else jnp.where(live, brow, NEG))
            keeps.append(j * SEL_GROUP + i * SEL_BLOCK + r_blk <= t_row if causal else None)

        def block(i):
            return sj[i * SEL_BLOCK:(i + 1) * SEL_BLOCK, :]

        mx = None
        for i in range(blk_per_grp):
            sl = block(i) + brows[i]
            if causal:
                sl = jnp.where(keeps[i], sl, NEG)
            mx = sl if mx is None else jnp.maximum(mx, sl)
        m_n = jnp.maximum(m_o, jnp.max(mx, axis=0, keepdims=True))
        a = jnp.exp2(m_o - m_n)
        acc = a * acc
        for ci in range(chunk_per_grp):
            parts = []
            for i in range(blk_per_chunk * ci, blk_per_chunk * (ci + 1)):
                arg = block(i) + (brows[i] - m_n)
                parts.append(jnp.exp2(jnp.where(keeps[i], arg, NEG) if causal else arg))
            acc = acc + pv(vst_ref[chunk_per_grp * j + ci], jnp.concatenate(parts, axis=0))
        return m_n, acc

    empty = (jnp.full((1, width), NEG, F32), jnp.zeros((VT_ROWS, width), F32))
    seeded = sel_update(n_full, s_diag, empty, causal=True)

    def pair_body(jp, carry):
        ja, jb = 2 * jp, 2 * jp + 1
        sy_scr[...] = qk_group(jb)
        carry = sel_update(ja, sx_scr, carry)
        sx_scr[...] = qk_group(jnp.minimum(ja + 2, n_grp - 1))
        return sel_update(jb, sy_scr, carry, live=jb < n_full)

    _, acc_s = lax.fori_loop(0, lax.shift_right_logical(n_full + 1, 1), pair_body, seeded)
    o_sel = normalised(acc_s)

    def gate_row(br):
        rows = [smt_ref[SM_GATE + 3 * h + br:SM_GATE + 3 * h + br + 1, :] for h in range(NSA_HEADS)]
        return _sigmoid(jnp.concatenate(rows, axis=1))

    o_t = gate_row(0) * o_cmp + gate_row(1) * o_sel + gate_row(2) * o_win
    for pr in range(NSA_HEADS // 2):
        xp = jnp.concatenate([o_t[:, (2 * pr) * QT:(2 * pr + 1) * QT], o_t[:, (2 * pr + 1) * QT:(2 * pr + 2) * QT]], axis=0)
        o_ref[:, pr * LANES:(pr + 1) * LANES] = xp.T.astype(BF16)


def _nsa(qpad, kcb, vct, ks, kw, vst, vwt, smt, b, s):
    nq = s // QT
    ncmp = kcb.shape[1]
    return pl.pallas_call(
        _nsa_kernel,
        out_shape=jax.ShapeDtypeStruct((b * s, NSA_HEADS * NSA_HD), BF16),
        grid=(b, nq),
        in_specs=[
            pl.BlockSpec((QT, NSA_HEADS * LANES), lambda bi, c: (bi * nq + c, 0)),
            pl.BlockSpec((None, ncmp, LANES), lambda bi, c: (bi, 0, 0)),
            pl.BlockSpec((None, NSA_KV * NSA_HD, ncmp), lambda bi, c: (bi, 0, 0)),
            pl.BlockSpec((s, LANES), lambda bi, c: (bi, 0)),
            pl.BlockSpec((s, LANES), lambda bi, c: (bi, 0)),
            pl.BlockSpec((s // KC, NSA_KV * VT_ROWS, KC), lambda bi, c: (bi, 0, 0)),
            pl.BlockSpec((s // KC, NSA_KV * VT_ROWS, KC), lambda bi, c: (bi, 0, 0)),
            pl.BlockSpec((LANES, QT), lambda bi, c: (0, bi * nq + c)),
        ],
        out_specs=pl.BlockSpec((QT, NSA_HEADS * NSA_HD), lambda bi, c: (bi * nq + c, 0)),
        scratch_shapes=[pltpu.VMEM((s // SEL_BLOCK, NSA_HEADS * QT), F32),
                        pltpu.VMEM((SEL_GROUP, NSA_HEADS * QT), F32),
                        pltpu.VMEM((SEL_GROUP, NSA_HEADS * QT), F32)],
        compiler_params=pltpu.CompilerParams(dimension_semantics=("arbitrary", "arbitrary"),
                                             vmem_limit_bytes=VMEM_LIMIT),
        name="nsa",
    )(qpad, kcb, vct, ks, kw, vst, vwt, smt)


def _log_sigmoid(x):
    return jnp.minimum(x, 0.0) - jnp.log(1.0 + jnp.exp(-jnp.abs(x)))


def _mlstm_kernel(qk_ref, v_ref, og_ref, sm_ref, smt_ref, wc_ref, bc_ref, bifc_ref, bifr_ref, ghn_ref,
                  y_ref, ext_scr, ct_scr, n_scr, m_scr):
    lc = ML_CHUNK

    @pl.when(pl.program_id(1) == 0)
    def _():
        ext_scr[0:8, :] = jnp.zeros((8, 2 * ML_WIDTH), F32)
        ct_scr[...] = jnp.zeros_like(ct_scr)
        n_scr[...] = jnp.zeros_like(n_scr)
        m_scr[...] = jnp.zeros_like(m_scr)

    ext_scr[8:8 + ML_BLOCK, :] = qk_ref[...].astype(F32)
    y = bc_ref[...]
    for j in range(CONV_W):
        y = y + wc_ref[j:j + 1, :] * ext_scr[pl.ds(8 - (CONV_W - 1) + j, ML_BLOCK), :]
    ext_scr[0:8, :] = ext_scr[ML_BLOCK:ML_BLOCK + 8, :]
    qkc = y * _sigmoid(y)
    q_all = qkc[:, 0:ML_WIDTH].astype(BF16)
    k_all = (qkc[:, ML_WIDTH:2 * ML_WIDTH] * (ML_HD ** -0.5)).astype(BF16)

    ifc = sm_ref[...] + bifc_ref[...]
    ifr = smt_ref[...] + bifr_ref[...]
    lfc = _log_sigmoid(ifc)
    lfr = _log_sigmoid(ifr)
    rr = lax.broadcasted_iota(jnp.int32, (lc, lc), 0)
    cc = lax.broadcasted_iota(jnp.int32, (lc, lc), 1)
    causal = rr >= cc
    tri_l = causal.astype(F32)
    tri_u = (rr <= cc).astype(F32)

    for ci in range(ML_BLOCK // lc):
        lo, hi = ci * lc, (ci + 1) * lc
        bc_all = jnp.dot(tri_l, lfc[lo:hi, :], preferred_element_type=F32, precision=lax.Precision.HIGHEST)
        br_all = jnp.dot(lfr[:, lo:hi], tri_u, preferred_element_type=F32, precision=lax.Precision.HIGHEST)
        for h in range(ML_HEADS):
            hs = slice(h * ML_HD, (h + 1) * ML_HD)
            bcol = bc_all[:, SM_F + h:SM_F + h + 1]
            brow = br_all[SM_F + h:SM_F + h + 1, :]
            icol = ifc[lo:hi, SM_I + h:SM_I + h + 1]
            irow = ifr[SM_I + h:SM_I + h + 1, lo:hi]
            mprev = m_scr[h][:, 0:1]
            qh = q_all[lo:hi, hs]
            kh = k_all[lo:hi, hs]
            vh = v_ref[lo:hi, hs]
            dmat = jnp.where(causal, bcol - brow + irow, NEG)
            inter = bcol + mprev
            mt = jnp.maximum(jnp.max(dmat, axis=-1, keepdims=True), inter)
            a = jnp.exp(dmat - mt) * _dot_nt(qh, kh)
            dec = jnp.exp(inter - mt)
            ct = ct_scr[h]
            nrow = n_scr[h]
            num = _dot(a.astype(BF16), vh) + dec * _dot(qh, ct.astype(BF16))
            den = jnp.sum(a, axis=-1, keepdims=True) + dec * jnp.sum(qh.astype(F32) * nrow, axis=-1, keepdims=True)
            hh = num / jnp.maximum(jnp.abs(den), jnp.exp(-mt))
            blast = bcol[lc - 1:lc, :]
            grow = blast - brow + irow
            mnew = jnp.maximum(blast + mprev, jnp.max(grow, axis=-1, keepdims=True))
            wprev = jnp.exp(blast + mprev - mnew)
            kwt = kh.astype(F32) * jnp.exp(blast - bcol + icol - mnew)
            ct_scr[h] = wprev * ct + lax.dot_general(kwt.astype(BF16), vh, _TN, preferred_element_type=F32)
            n_scr[h] = wprev * nrow + jnp.sum(kwt, axis=0, keepdims=True)
            m_scr[h] = jnp.broadcast_to(mnew, (1, LANES))
            hm = hh * _sigmoid(og_ref[lo:hi, hs].astype(F32))
            y_ref[lo:hi, hs] = _rms(hm, ghn_ref[:, hs]).astype(BF16)


def _mlstm(qkb, vb, ob, sm, smt, wconv, bconv, bifc, bifr, ghn, b, s):
    nb = s // ML_BLOCK
    row = lambda w: pl.BlockSpec((ML_BLOCK, w), lambda bi, j: (bi * nb + j, 0))
    return pl.pallas_call(
        _mlstm_kernel,
        out_shape=jax.ShapeDtypeStruct((b * s, ML_WIDTH), BF16),
        grid=(b, nb),
        in_specs=[row(2 * ML_WIDTH), row(ML_WIDTH), row(ML_WIDTH), row(LANES),
                  pl.BlockSpec((LANES, ML_BLOCK), lambda bi, j: (0, bi * nb + j)),
                  _const_spec(wconv.shape), _const_spec(bconv.shape), _const_spec(bifc.shape),
                  _const_spec(bifr.shape), _const_spec(ghn.shape)],
        out_specs=row(ML_WIDTH),
        scratch_shapes=[pltpu.VMEM((ML_BLOCK + 8, 2 * ML_WIDTH), F32),
                        pltpu.VMEM((ML_HEADS, ML_HD, ML_HD), F32),
                        pltpu.VMEM((ML_HEADS, 1, ML_HD), F32),
                        pltpu.VMEM((ML_HEADS, 1, LANES), F32)],
        compiler_params=pltpu.CompilerParams(dimension_semantics=("arbitrary", "arbitrary"),
                                             vmem_limit_bytes=VMEM_LIMIT),
        name="mlstm",
    )(qkb, vb, ob, sm, smt, wconv, bconv, bifc, bifr, ghn)


RT_BUCKET = N_EXPERTS
RT_RANK = N_EXPERTS + 1
RT_WLO = N_EXPERTS + 2
RT_WHI = N_EXPERTS + 3
N_BUCKETS = N_GROUPS * 6
X_ROWS = D_MODEL // LANES
REC = 2 * X_ROWS
DMA_UNROLL = 8


def _merge_kernel(ya_ref, yb_ref, mg_ref, x_ref, wpa_ref, wpb_ref, wout_ref, gffn_ref, wr_ref, br_ref,
                  x1_ref, hx_ref, slab_ref, hist_ref):
    pa = _dot(ya_ref[...], wpa_ref[...])
    pb = _dot(yb_ref[...], wpb_ref[...])
    ga = _sigmoid(mg_ref[:, 0:D_MODEL].astype(F32))
    gb = _sigmoid(mg_ref[:, D_MODEL:2 * D_MODEL].astype(F32))
    mixed = (ga * pa + gb * pb).astype(BF16)
    x1 = x_ref[...] + _dot(mixed, wout_ref[...])
    x1_ref[...] = x1
    h2 = _rms(x1, gffn_ref[...])
    h_hi = h2.astype(BF16)
    for j in range(X_ROWS):
        hx_ref[pl.ds(j, TM, stride=REC), :] = h2[:, j * LANES:(j + 1) * LANES]

    h_lo = (h2 - h_hi.astype(F32)).astype(BF16)
    r_hi = _dot(h_hi, wr_ref[...])
    logit = r_hi[:, 0:LANES] + r_hi[:, LANES:2 * LANES] + _dot(h_lo, wr_ref[:, 0:LANES]) + br_ref[...]
    lane = lax.broadcasted_iota(jnp.int32, logit.shape, 1)
    big = jnp.int32(LANES)
    gmask = (lane >= N_EXPERTS) & (lane < N_EXPERTS + N_GROUPS)
    gl = jnp.where(gmask, logit, NEG)
    gmax = jnp.max(gl, axis=-1, keepdims=True)
    gidx = jnp.min(jnp.where(gmask & (gl == gmax), lane, big), axis=-1, keepdims=True) - N_EXPERTS
    pg_sel = 1.0 / jnp.sum(jnp.where(gmask, jnp.exp(gl - gmax), 0.0), axis=-1, keepdims=True)
    emask = (lane < N_EXPERTS) & (lax.shift_right_logical(lane, 2) == gidx)
    el = jnp.where(emask, logit, NEG)
    e1 = jnp.max(el, axis=-1, keepdims=True)
    i1 = jnp.min(jnp.where(emask & (el == e1), lane, big), axis=-1, keepdims=True)
    emask2 = emask & (lane != i1)
    el2 = jnp.where(emask2, logit, NEG)
    e2 = jnp.max(el2, axis=-1, keepdims=True)
    i2 = jnp.min(jnp.where(emask2 & (el2 == e2), lane, big), axis=-1, keepdims=True)
    x21 = jnp.exp(e2 - e1)
    w1 = pg_sel / (1.0 + x21)
    w2 = pg_sel * x21 / (1.0 + x21)
    first_lo = i1 < i2
    e_lo = jnp.where(first_lo, i1, i2) - EXP_PER_GROUP * gidx
    e_hi = jnp.where(first_lo, i2, i1) - EXP_PER_GROUP * gidx
    pair = lax.shift_right_logical(e_lo * (2 * EXP_PER_GROUP - 1 - e_lo), 1) + (e_hi - e_lo - 1)
    bucket = 6 * gidx + pair
    member = lane == bucket
    onehot = jnp.where(member, 1.0, 0.0)
    rr = lax.broadcasted_iota(jnp.int32, (TM, TM), 0)
    cc = lax.broadcasted_iota(jnp.int32, (TM, TM), 1)
    earlier = _dot((rr > cc).astype(BF16), onehot.astype(BF16))
    rank = jnp.sum(jnp.where(member, earlier, 0.0), axis=-1, keepdims=True)
    slab = jnp.where(lane == i1, w1, 0.0) + jnp.where(lane == i2, w2, 0.0)
    slab = jnp.where(lane == RT_BUCKET, bucket.astype(F32), slab)
    slab = jnp.where(lane == RT_RANK, rank, slab)
    slab = jnp.where(lane == RT_WLO, jnp.where(first_lo, w1, w2), slab)
    slab = jnp.where(lane == RT_WHI, jnp.where(first_lo, w2, w1), slab)
    slab_ref[...] = slab
    hx_ref[pl.ds(X_ROWS, TM, stride=REC), :] = slab
    for j in range(X_ROWS + 1, REC):
        hx_ref[pl.ds(j, TM, stride=REC), :] = jnp.zeros((TM, LANES), F32)
    hist_ref[...] = jnp.broadcast_to(jnp.sum(onehot, axis=0, keepdims=True), (8, LANES))


def _merge(ya, yb, mg, x2, wpa, wpb, wout, gffn, wr, br):
    t = x2.shape[0]
    row = lambda w: pl.BlockSpec((TM, w), lambda i: (i, 0))
    return pl.pallas_call(
        _merge_kernel,
        out_shape=[jax.ShapeDtypeStruct((t, D_MODEL), F32),
                   jax.ShapeDtypeStruct((t * REC, LANES), F32),
                   jax.ShapeDtypeStruct((t, LANES), F32),
                   jax.ShapeDtypeStruct((t // TM * 8, LANES), F32)],
        grid=(t // TM,),
        in_specs=[row(NSA_HEADS * NSA_HD), row(ML_WIDTH), row(2 * D_MODEL), row(D_MODEL)]
                 + [_const_spec(a.shape) for a in (wpa, wpb, wout, gffn, wr, br)],
        out_specs=[row(D_MODEL), pl.BlockSpec((TM * REC, LANES), lambda i: (i, 0)), row(LANES),
                   pl.BlockSpec((8, LANES), lambda i: (i, 0))],
        compiler_params=pltpu.CompilerParams(dimension_semantics=("arbitrary",), vmem_limit_bytes=VMEM_LIMIT),
        name="merge",
    )(ya, yb, mg, x2, wpa, wpb, wout, gffn, wr, br)


def _rec_copy(src_ref, src_tok, dst_ref, dst_tok, sem, rows):
    src = src_ref.at[pl.ds(pl.multiple_of(src_tok * rows, rows), rows), :]
    dst = dst_ref.at[pl.ds(pl.multiple_of(dst_tok * rows, rows), rows), :]
    return pltpu.make_async_copy(src, dst, sem)


def _token_copies(n, make, wait=False):
    def body(g, carry):
        for u in range(DMA_UNROLL):
            cp = make(g * DMA_UNROLL + u)
            if wait:
                cp.wait()
            else:
                cp.start(priority=u % 2)
        return carry
    lax.fori_loop(0, n // DMA_UNROLL, body, 0)


def _dispatch_kernel(pos_ref, tail_ref, hx_ref, out_ref, stage_scr, zero_scr, sem, zsem):
    i = pl.program_id(0)
    slot = lax.rem(i, 2)
    n_tiles = out_ref.shape[0] // (TM * REC)
    n_used = tail_ref[2 * N_BUCKETS]

    def zero_copy(first_slot):
        start = pl.multiple_of(first_slot * REC, TM * REC)
        return pltpu.make_async_copy(zero_scr, out_ref.at[pl.ds(start, TM * REC), :], zsem)

    @pl.when(i == 0)
    def _():
        zero_scr[...] = jnp.zeros_like(zero_scr)
        for phase in ("start", "wait"):
            for b in range(N_BUCKETS):
                @pl.when(tail_ref[N_BUCKETS + b] > 0)
                def _():
                    getattr(zero_copy(tail_ref[b]), phase)()

                @pl.when(n_used + b < n_tiles)
                def _():
                    getattr(zero_copy((n_used + b) * TM), phase)()

    stage_scr[slot] = hx_ref[...]
    base = i * TM
    _token_copies(TM, lambda r: _rec_copy(stage_scr.at[slot], r, out_ref, pos_ref[base + r], sem.at[slot], REC))

    def drain(which):
        _token_copies(TM, lambda r: _rec_copy(stage_scr.at[which], 0, out_ref, 0, sem.at[which], REC), wait=True)

    @pl.when(i > 0)
    def _():
        drain(1 - slot)

    @pl.when(i == pl.num_programs(0) - 1)
    def _():
        drain(slot)


def _dispatch(pos, tail, hx, n_slots):
    t = hx.shape[0] // REC
    return pl.pallas_call(
        _dispatch_kernel,
        out_shape=jax.ShapeDtypeStruct((n_slots * REC, LANES), F32),
        grid_spec=pltpu.PrefetchScalarGridSpec(
            num_scalar_prefetch=2,
            grid=(t // TM,),
            in_specs=[pl.BlockSpec((TM * REC, LANES), lambda i, pos_r, tail_r: (i, 0))],
            out_specs=pl.BlockSpec(memory_space=pl.ANY),
            scratch_shapes=[pltpu.VMEM((2, TM * REC, LANES), F32), pltpu.VMEM((TM * REC, LANES), F32),
                            pltpu.SemaphoreType.DMA((2,)), pltpu.SemaphoreType.DMA(())],
        ),
        compiler_params=pltpu.CompilerParams(dimension_semantics=("arbitrary",), vmem_limit_bytes=VMEM_LIMIT,
                                             has_side_effects=True),
        name="dispatch",
    )(pos, tail, hx)


def _moe_kernel(te_ref, nu_ref, hx_ref, w13_ref, w2_ref, y_ref):
    k = pl.program_id(0)
    n_tiles = pl.num_programs(0)
    n_used = nu_ref[0]

    @pl.when(k < n_used)
    def _():
        h = jnp.concatenate([hx_ref[pl.ds(j, TM, stride=REC), :] for j in range(X_ROWS)], axis=1).astype(BF16)
        slab = hx_ref[pl.ds(X_ROWS, TM, stride=REC), :]
        y = None
        for side, lane in ((0, RT_WLO), (1, RT_WHI)):
            e = te_ref[side * n_tiles + k]
            a = _dot(h, w13_ref[e])
            gt = a[:, 0:D_EXPERT]
            act = gt * _sigmoid(gt) * a[:, D_EXPERT:2 * D_EXPERT] * slab[:, lane:lane + 1]
            part = _dot(act.astype(BF16), w2_ref[e])
            y = part if y is None else y + part
        for j in range(X_ROWS):
            y_ref[pl.ds(j, TM, stride=X_ROWS), :] = y[:, j * LANES:(j + 1) * LANES]

    @pl.when(k >= n_used)
    def _():
        y_ref[...] = jnp.zeros_like(y_ref)


def _moe(tile_e, n_used, hx_sorted, w13, w2):
    n_tiles = hx_sorted.shape[0] // (TM * REC)
    return pl.pallas_call(
        _moe_kernel,
        out_shape=jax.ShapeDtypeStruct((n_tiles * TM * X_ROWS, LANES), F32),
        grid_spec=pltpu.PrefetchScalarGridSpec(
            num_scalar_prefetch=2,
            grid=(n_tiles,),
            in_specs=[pl.BlockSpec((TM * REC, LANES), lambda k, te, nu: (jnp.minimum(k, nu[0] - 1), 0)),
                      pl.BlockSpec(w13.shape, lambda k, te, nu: (0, 0, 0), pipeline_mode=pl.Buffered(1)),
                      pl.BlockSpec(w2.shape, lambda k, te, nu: (0, 0, 0), pipeline_mode=pl.Buffered(1))],
            out_specs=pl.BlockSpec((TM * X_ROWS, LANES), lambda k, te, nu: (k, 0)),
        ),
        compiler_params=pltpu.CompilerParams(dimension_semantics=("arbitrary",), vmem_limit_bytes=VMEM_LIMIT),
        name="moe",
    )(tile_e, n_used, hx_sorted, w13, w2)


def _combine_kernel(pos_ref, y_ref, x1_ref, p_ref, gple_ref, wpg_ref, wpp_ref, gfin_ref, o_ref, ybuf, sem):
    i = pl.program_id(0)
    slot = lax.rem(i, 2)

    def gather(tile, which):
        _token_copies(TM, lambda r: _rec_copy(y_ref, pos_ref[tile * TM + r], ybuf.at[which], r, sem.at[which],
                                              X_ROWS))

    @pl.when(i == 0)
    def _():
        gather(0, 0)

    @pl.when(i + 1 < pl.num_programs(0))
    def _():
        gather(i + 1, 1 - slot)

    _token_copies(TM, lambda r: _rec_copy(y_ref, 0, ybuf.at[slot], 0, sem.at[slot], X_ROWS), wait=True)
    yb = ybuf.at[slot]
    x2 = x1_ref[...] + jnp.concatenate([yb[pl.ds(j, TM, stride=X_ROWS), :] for j in range(X_ROWS)], axis=1)
    h3 = _rms(x2, gple_ref[...]).astype(BF16)
    x3 = x2 + _sigmoid(_dot(h3, wpg_ref[...])) * _dot(p_ref[...].astype(BF16), wpp_ref[...])
    o_ref[...] = _rms(x3, gfin_ref[...])


def _combine(pos, y_sorted, x1, p2, gple, wpg, wpp, gfin):
    t = x1.shape[0]
    row = lambda w: pl.BlockSpec((TM, w), lambda i, pos_r: (i, 0))
    const = lambda a: pl.BlockSpec(a.shape, lambda i, pos_r: (0,) * a.ndim, pipeline_mode=pl.Buffered(1))
    return pl.pallas_call(
        _combine_kernel,
        out_shape=jax.ShapeDtypeStruct((t, D_MODEL), F32),
        grid_spec=pltpu.PrefetchScalarGridSpec(
            num_scalar_prefetch=1,
            grid=(t // TM,),
            in_specs=[pl.BlockSpec(memory_space=pl.ANY), row(D_MODEL), row(PLE_DIM),
                      const(gple), const(wpg), const(wpp), const(gfin)],
            out_specs=row(D_MODEL),
            scratch_shapes=[pltpu.VMEM((2, TM * X_ROWS, LANES), F32), pltpu.SemaphoreType.DMA((2,))],
        ),
        compiler_params=pltpu.CompilerParams(dimension_semantics=("arbitrary",), vmem_limit_bytes=VMEM_LIMIT),
        name="combine",
    )(pos, y_sorted, x1, p2, gple, wpg, wpp, gfin)


def _routing_tables(slab, hist8):
    t = slab.shape[0]
    nt = t // TM
    n_tiles = nt + N_BUCKETS
    hist = hist8.reshape(nt, 8, LANES)[:, 0, :]
    counts = jnp.sum(hist, axis=0)
    padded = jnp.ceil(counts / TM) * TM
    ends = jnp.cumsum(padded)
    first = (ends - padded)[None, :] + jnp.cumsum(hist, axis=0) - hist
    lane = jnp.arange(LANES, dtype=F32)[None, :]
    mine = lane == slab[:, RT_BUCKET:RT_BUCKET + 1]
    pos = jnp.sum(jnp.where(mine, jnp.repeat(first, TM, axis=0), 0.0), axis=1) + slab[:, RT_RANK]
    starts = jnp.arange(n_tiles, dtype=F32) * TM
    tile_bucket = jnp.minimum(jnp.sum(ends[None, :N_BUCKETS] <= starts[:, None], axis=1), N_BUCKETS - 1)
    group, pair = tile_bucket // 6, tile_bucket % 6
    e_lo = EXP_PER_GROUP * group + jnp.array([0, 0, 0, 1, 1, 2], jnp.int32)[pair]
    e_hi = EXP_PER_GROUP * group + jnp.array([1, 2, 3, 2, 3, 3], jnp.int32)[pair]
    tile_e = jnp.concatenate([e_lo, e_hi]).astype(jnp.int32)
    n_used = (ends[N_BUCKETS - 1] / TM).astype(jnp.int32).reshape(1)
    tail = jnp.concatenate([(ends - TM)[:N_BUCKETS], padded[:N_BUCKETS], n_used.astype(F32)]).astype(jnp.int32)
    return pos.astype(jnp.int32), tile_e, n_used, tail, n_tiles * TM


def _pack_inproj_weights(w):
    d = w.shape[0]
    qw = NSA_HEADS * NSA_HD
    kvw = NSA_KV * NSA_HD
    o = 0
    wq = w[:, o:o + qw]; o += qw
    wkc = w[:, o:o + kvw]; o += kvw
    wvc = w[:, o:o + kvw]; o += kvw
    wks = w[:, o:o + kvw]; o += kvw
    wvs = w[:, o:o + kvw]; o += kvw
    wkw = w[:, o:o + kvw]; o += kvw
    wvw = w[:, o:o + kvw]; o += kvw
    wga = w[:, o:o + 3 * NSA_HEADS]; o += 3 * NSA_HEADS
    wqkb = w[:, o:o + 2 * ML_WIDTH]; o += 2 * ML_WIDTH
    wvb = w[:, o:o + ML_WIDTH]; o += ML_WIDTH
    wob = w[:, o:o + ML_WIDTH]; o += ML_WIDTH
    wif = w[:, o:o + 2 * ML_HEADS]; o += 2 * ML_HEADS
    wmg = w[:, o:o + 2 * D_MODEL]
    zero64 = jnp.zeros((d, NSA_HD), w.dtype)
    qcols = []
    for h in range(NSA_HEADS):
        wh = wq[:, h * NSA_HD:(h + 1) * NSA_HD]
        qcols += [wh, zero64] if h // NSA_HPG == 0 else [zero64, wh]
    wsm = jnp.concatenate([wga, wif, jnp.zeros((d, LANES - 3 * NSA_HEADS - 2 * ML_HEADS), w.dtype)], axis=1)
    wcat = jnp.concatenate(qcols + [wkc, wvc, wks, wkw, wsm, wqkb, wvb, wob, wmg], axis=1).astype(BF16)
    wtr = jnp.concatenate([wvs, wvw, wsm], axis=1).T.astype(BF16)
    return wcat, wtr


def _rope_lane_tables(positions):
    inv = ROPE_THETA ** (-jnp.arange(0, ROPE_DIM, 2, dtype=F32) / ROPE_DIM)
    ang = positions.astype(F32).reshape(-1, 1) * inv[None, :]
    cos, sin = jnp.cos(ang), jnp.sin(ang)
    half = ROPE_DIM // 2
    d = jnp.arange(LANES) % NSA_HD
    cos_l, sin_l = jnp.tile(cos, (1, LANES // half)), jnp.tile(sin, (1, LANES // half))
    rc = jnp.where(d < ROPE_DIM, cos_l, 1.0)
    rp = jnp.where((d >= half) & (d < ROPE_DIM), sin_l, 0.0)
    rm = jnp.where(d < half, -sin_l, 0.0)
    return rc, rp, rm


def _pack_compress_weights(w1, w2, pe):
    half = CMP_LEN // 2
    w1r = w1.reshape(2, half, NSA_HD, CMP_HIDDEN)
    outs = []
    for part in range(2):
        wb = w1r[part].astype(BF16)
        zb = jnp.zeros_like(wb)
        wp = jnp.stack([jnp.stack([wb, zb], axis=2), jnp.stack([zb, wb], axis=2)], axis=1)
        outs.append(wp.reshape(half * NSA_KV * NSA_HD, NSA_KV * CMP_HIDDEN))
    pe8 = jnp.broadcast_to(pe.reshape(1, CMP_LEN * NSA_HD), (8, CMP_LEN * NSA_HD)).astype(BF16)
    return outs[0], outs[1], pe8, w1.astype(BF16)


def _stages(x, p, positions, g_mix, w_in, b_if, w_ck1, w_ck2, pe_ck, w_cv1, w_cv2, pe_cv, w_conv, b_conv, g_hn, w_pa, w_pb, w_out, g_ffn, w_rg, b_rg, w_re, b_re, w_e13, w_e2, g_ple, w_pg, w_pp, g_final):
    b, s, d = x.shape
    t = b * s
    rc, rp, rm = _rope_lane_tables(positions)
    assert w_in.shape[0] == 1, "the final norm is fused into the layer's last kernel: single-layer problem only"
    for i in range(w_in.shape[0]):
        x2 = x.reshape(t, d)
        wcat, wtr = _pack_inproj_weights(w_in[i])
        (qpad, kc_tok, vc_tok, ks, kw, vst, vwt, sm, smt, qkb, vb, ob, mg) = _inproj(
            x2, g_mix[i].reshape(1, d), wcat, wtr, rc, rp, rm)
        wka, wkb, pek, w1k = _pack_compress_weights(w_ck1[i], w_ck2[i], pe_ck[i])
        wva, wvb, pev, w1v = _pack_compress_weights(w_cv1[i], w_cv2[i], pe_cv[i])
        zpad = jnp.zeros((CMP_HIDDEN, NSA_HD), F32)
        w2k = jnp.stack([jnp.concatenate([w_ck2[i], zpad], axis=1),
                         jnp.concatenate([zpad, w_ck2[i]], axis=1)]).astype(BF16)
        w2vt = w_cv2[i].T.astype(BF16)
        nrow = s // CMP_STRIDE
        rk = kc_tok.reshape(b, nrow, CMP_STRIDE * LANES)
        rv = vc_tok.reshape(b, nrow, CMP_STRIDE * LANES)
        kcb, vct = _compress(rk, rv, wka, wkb, wva, wvb, pek, pev, w1k, w1v, w2k, w2vt)
        ya = _nsa(qpad, kcb, vct, ks, kw, vst, vwt, smt, b, s)
        bif = b_if[i].astype(F32)
        bifc = jnp.zeros((1, LANES), F32).at[0, SM_I:SM_I + 2 * ML_HEADS].set(bif)
        bifr = bifc.reshape(LANES, 1)
        yb = _mlstm(qkb, vb, ob, sm, smt, w_conv[i], b_conv[i].reshape(1, -1), bifc, bifr,
                    g_hn[i].reshape(1, -1), b, s)
        wr = jnp.concatenate([w_re[i], w_rg[i], jnp.zeros((d, LANES - N_EXPERTS - N_GROUPS), F32)], axis=1)
        wr_hi = wr.astype(BF16)
        wr = jnp.concatenate([wr_hi, (wr - wr_hi.astype(F32)).astype(BF16)], axis=1)
        br =jnp.concatenate([b_re[i], b_rg[i], jnp.zeros((LANES - N_EXPERTS - N_GROUPS,), F32)]).reshape(1, LANES)
        x1, hx, slab, hist8 = _merge(ya, yb, mg, x2, w_pa[i].astype(BF16), w_pb[i].astype(BF16),
                                     w_out[i].astype(BF16), g_ffn[i].reshape(1, d), wr, br)
        pos, tile_e, n_used, tail, n_slots = _routing_tables(slab, hist8)
        hx_sorted = _dispatch(pos, tail, hx, n_slots)
        y_sorted = _moe(tile_e, n_used, hx_sorted, w_e13[i].astype(BF16), w_e2[i].astype(BF16))
        out = _combine(pos, y_sorted, x1, p[i].reshape(t, PLE_DIM), g_ple[i].reshape(1, d), w_pg[i].astype(BF16),
                       w_pp[i].astype(BF16), g_final.reshape(1, d))
        x = out.reshape(b, s, d)
    return dict(out=x, qpad=qpad, ks=ks, kcb=kcb, vct=vct, y_a=ya, y_b=yb, x1=x1, hx=hx, pos=pos)


def kernel(x, p, positions, g_mix, w_in, b_if, w_ck1, w_ck2, pe_ck, w_cv1, w_cv2, pe_cv, w_conv, b_conv, g_hn, w_pa, w_pb, w_out, g_ffn, w_rg, b_rg, w_re, b_re, w_e13, w_e2, g_ple, w_pg, w_pp, g_final):
    return _stages(x, p, positions, g_mix, w_in, b_if, w_ck1, w_ck2, pe_ck, w_cv1, w_cv2, pe_cv, w_conv, b_conv, g_hn,
                   w_pa, w_pb, w_out, g_ffn, w_rg, b_rg, w_re, b_re, w_e13, w_e2, g_ple, w_pg, w_pp, g_final)["out"]
```

```python
import functools
import math

import jax
import jax.numpy as jnp
from jax import lax
from jax.experimental import pallas as pl
from jax.experimental.pallas import tpu as pltpu

F32 = jnp.float32
BF16 = jnp.bfloat16

EPS = 1e-6
NEG = -1e30

D_MODEL = 1024
PLE_DIM = 256
NSA_HEADS = 8
NSA_KV = 2
NSA_HPG = NSA_HEADS // NSA_KV
NSA_HD = 64
CMP_LEN = 32
CMP_STRIDE = 16
CMP_HIDDEN = 256
SEL_BLOCK = 64
SEL_TOPK = 16
SEL_FORCE = 1000.0
WINDOW = 512
ROPE_THETA = 500000.0
ROPE_DIM = NSA_HD // 4
ML_HEADS = 4
ML_HD = 128
ML_WIDTH = ML_HEADS * ML_HD
CONV_W = 4
N_GROUPS = 4
EXP_PER_GROUP = 4
N_EXPERTS = N_GROUPS * EXP_PER_GROUP
D_EXPERT = 256

LANES = 128
QT = 128
KC = 128
SEL_GROUP = 512
VT_PAD = 16
VT_ROWS = NSA_HD + VT_PAD
ML_CHUNK = 128
ML_BLOCK = 256
TD = 512
TM = 256
VMEM_LIMIT = 56 * 1024 * 1024

_NT = (((1,), (1,)), ((), ()))
_TN = (((0,), (0,)), ((), ()))

SM_GATE = 0
SM_I = 3 * NSA_HEADS
SM_F = SM_I + ML_HEADS


def _dot(a, b):
    return jnp.dot(a, b, preferred_element_type=F32)


def _dot_nt(a, b):
    return lax.dot_general(a, b, _NT, preferred_element_type=F32)


def _split3(x):
    hi = x.astype(BF16)
    r1 = x - hi.astype(F32)
    mid = r1.astype(BF16)
    lo = (r1 - mid.astype(F32)).astype(BF16)
    return hi, mid, lo


def _rms(x, g):
    return x * lax.rsqrt(jnp.mean(x * x, axis=-1, keepdims=True) + EPS) * g


def _sigmoid(x):
    return 0.5 + 0.5 * jnp.tanh(0.5 * x)


def _const_spec(shape):
    nd = len(shape)
    return pl.BlockSpec(shape, lambda *_: (0,) * nd, pipeline_mode=pl.Buffered(1))


_C_Q = 0
_C_KC = _C_Q + NSA_HEADS * LANES
_C_VC = _C_KC + LANES
_C_KS = _C_VC + LANES
_C_KW = _C_KS + LANES
_C_SM = _C_KW + LANES
_C_QKB = _C_SM + LANES
_C_VB = _C_QKB + 2 * ML_WIDTH
_C_OB = _C_VB + ML_WIDTH
_C_MG = _C_OB + ML_WIDTH
_C_END = _C_MG + 2 * D_MODEL


def _inproj_kernel(x_ref, g_ref, w_ref, wt_ref, rc_ref, rp_ref, rm_ref,
                   q_ref, kc_ref, vc_ref, ks_ref, kw_ref, vst_ref, vwt_ref, sm_ref, smt_ref,
                   qkb_ref, vb_ref, ob_ref, mg_ref):
    hn = _rms(x_ref[...], g_ref[...]).astype(BF16)
    rc, rp, rm = rc_ref[...], rp_ref[...], rm_ref[...]

    def rope(z):
        return z * rc + pltpu.roll(z, 8, 1) * rp + pltpu.roll(z, LANES - 8, 1) * rm

    scale = NSA_HD ** -0.5 * math.log2(math.e)
    for h in range(NSA_HEADS):
        z = _dot(hn, w_ref[:, _C_Q + h * LANES:_C_Q + (h + 1) * LANES])
        q_ref[:, h * LANES:(h + 1) * LANES] = (rope(z) * scale).astype(BF16)
    kc_ref[...] = rope(_dot(hn, w_ref[:, _C_KC:_C_KC + LANES])).astype(BF16)
    vc_ref[...] = _dot(hn, w_ref[:, _C_VC:_C_VC + LANES]).astype(BF16)
    ks_ref[...] = rope(_dot(hn, w_ref[:, _C_KS:_C_KS + LANES])).astype(BF16)
    kw_ref[...] = rope(_dot(hn, w_ref[:, _C_KW:_C_KW + LANES])).astype(BF16)
    sm_ref[...] = _dot(hn, w_ref[:, _C_SM:_C_SM + LANES])
    for c0 in range(0, 2 * ML_WIDTH, 512):
        qkb_ref[:, c0:c0 + 512] = _dot(hn, w_ref[:, _C_QKB + c0:_C_QKB + c0 + 512]).astype(BF16)
    vb_ref[...] = _dot(hn, w_ref[:, _C_VB:_C_VB + ML_WIDTH]).astype(BF16)
    ob_ref[...] = _dot(hn, w_ref[:, _C_OB:_C_OB + ML_WIDTH]).astype(BF16)
    for c0 in range(0, 2 * D_MODEL, 512):
        mg_ref[:, c0:c0 + 512] = _dot(hn, w_ref[:, _C_MG + c0:_C_MG + c0 + 512]).astype(BF16)
    zt = _dot_nt(wt_ref[...], hn)
    ones_rows = (lax.broadcasted_iota(jnp.int32, (VT_PAD, KC), 0) == 0).astype(BF16)
    for i in range(TD // KC):
        for ref, r0 in ((vst_ref, 0), (vwt_ref, LANES)):
            zc = zt[r0:r0 + LANES, i * KC:(i + 1) * KC].astype(BF16)
            ref[i] = jnp.concatenate([piece for g in range(NSA_KV)
                                      for piece in (zc[g * NSA_HD:(g + 1) * NSA_HD, :], ones_rows)], axis=0)
    smt_ref[...] = zt[2 * LANES:3 * LANES, :]


def _inproj(x2, g_mix, wcat, wtr, rc, rp, rm):
    t = x2.shape[0]
    row = lambda w: pl.BlockSpec((TD, w), lambda i: (i, 0))
    out_shape = [
        jax.ShapeDtypeStruct((t, NSA_HEADS * LANES), BF16),
        jax.ShapeDtypeStruct((t, LANES), BF16),
        jax.ShapeDtypeStruct((t, LANES), BF16),
        jax.ShapeDtypeStruct((t, LANES), BF16),
        jax.ShapeDtypeStruct((t, LANES), BF16),
        jax.ShapeDtypeStruct((t // KC, NSA_KV * VT_ROWS, KC), BF16),
        jax.ShapeDtypeStruct((t // KC, NSA_KV * VT_ROWS, KC), BF16),
        jax.ShapeDtypeStruct((t, LANES), F32),
        jax.ShapeDtypeStruct((LANES, t), F32),
        jax.ShapeDtypeStruct((t, 2 * ML_WIDTH), BF16),
        jax.ShapeDtypeStruct((t, ML_WIDTH), BF16),
        jax.ShapeDtypeStruct((t, ML_WIDTH), BF16),
        jax.ShapeDtypeStruct((t, 2 * D_MODEL), BF16),
    ]
    chunk3 = pl.BlockSpec((TD // KC, NSA_KV * VT_ROWS, KC), lambda i: (i, 0, 0))
    out_specs = [row(NSA_HEADS * LANES), row(LANES), row(LANES), row(LANES), row(LANES), chunk3, chunk3,
                 row(LANES), pl.BlockSpec((LANES, TD), lambda i: (0, i)),
                 row(2 * ML_WIDTH), row(ML_WIDTH), row(ML_WIDTH), row(2 * D_MODEL)]
    return pl.pallas_call(
        _inproj_kernel,
        out_shape=out_shape,
        grid=(t // TD,),
        in_specs=[row(D_MODEL), _const_spec((1, D_MODEL)), _const_spec((D_MODEL, _C_END)),
                  _const_spec((3 * LANES, D_MODEL)), row(LANES), row(LANES), row(LANES)],
        out_specs=out_specs,
        compiler_params=pltpu.CompilerParams(dimension_semantics=("arbitrary",), vmem_limit_bytes=VMEM_LIMIT),
        name="inproj",
    )(x2, g_mix, wcat, wtr, rc, rp, rm)


def _gelu_tanh(x):
    return 0.5 * x * (1.0 + jnp.tanh(math.sqrt(2.0 / math.pi) * (x + 0.044715 * x * x * x)))


def _compress_kernel(rk_ref, rv_ref, wka_ref, wkb_ref, wva_ref, wvb_ref, pek_ref, pev_ref,
                     w1k_ref, w1v_ref, w2k_ref, w2vt_ref, kc_ref, vct_ref):
    nrow = rk_ref.shape[0]

    def hidden(r_ref, wa_ref, wb_ref, pe_ref, w1_ref):
        r = r_ref[...]
        ha = _dot(r, wa_ref[...])
        hb = _dot(r, wb_ref[...])
        hb = pltpu.roll(hb, nrow - 1, 0)
        c = _dot(pe_ref[...], w1_ref[...])[0:1, :]
        return [_gelu_tanh(ha[:, g * CMP_HIDDEN:(g + 1) * CMP_HIDDEN] + hb[:, g * CMP_HIDDEN:(g + 1) * CMP_HIDDEN] + c).astype(BF16)
                for g in range(NSA_KV)]

    ak = hidden(rk_ref, wka_ref, wkb_ref, pek_ref, w1k_ref)
    kc_ref[...] = (_dot(ak[0], w2k_ref[0]) + _dot(ak[1], w2k_ref[1])).astype(BF16)
    av = hidden(rv_ref, wva_ref, wvb_ref, pev_ref, w1v_ref)
    for g in range(NSA_KV):
        vct_ref[g * NSA_HD:(g + 1) * NSA_HD, :] = _dot_nt(w2vt_ref[...], av[g]).astype(BF16)


def _compress(rk, rv, wka, wkb, wva, wvb, pek, pev, w1k, w1v, w2k, w2vt):
    b, nrow, width = rk.shape
    blk = pl.BlockSpec((None, nrow, width), lambda i: (i, 0, 0))
    return pl.pallas_call(
        _compress_kernel,
        out_shape=[jax.ShapeDtypeStruct((b, nrow, LANES), BF16),
                   jax.ShapeDtypeStruct((b, LANES, nrow), BF16)],
        grid=(b,),
        in_specs=[blk, blk] + [_const_spec(a.shape) for a in (wka, wkb, wva, wvb, pek, pev, w1k, w1v, w2k, w2vt)],
        out_specs=[pl.BlockSpec((None, nrow, LANES), lambda i: (i, 0, 0)),
                   pl.BlockSpec((None, LANES, nrow), lambda i: (i, 0, 0))],
        compiler_params=pltpu.CompilerParams(dimension_semantics=("arbitrary",), vmem_limit_bytes=VMEM_LIMIT),
        name="compress",
    )(rk, rv, wka, wkb, wva, wvb, pek, pev, w1k, w1v, w2k, w2vt)


def _nsa_kernel(q_ref, kc_ref, vct_ref, ks_ref, kw_ref, vst_ref, vwt_ref, smt_ref, o_ref, bias_scr, sx_scr, sy_scr):
    c = pl.program_id(1)
    t0 = c * QT
    ncmp = kc_ref.shape[0]
    nsel = bias_scr.shape[0]
    nw = WINDOW // KC + 1
    gw = NSA_HPG * QT
    width = NSA_KV * gw

    def per_group(x):
        return [x[:, g * gw:(g + 1) * gw] for g in range(NSA_KV)]

    def pv(vt, p):
        rows = vt.shape[0] // NSA_KV
        pb = p.astype(BF16)
        return jnp.concatenate([_dot(vt[g * rows:(g + 1) * rows, :], pg) for g, pg in enumerate(per_group(pb))],
                               axis=1)

    def normalised(acc):
        return acc[0:NSA_HD, :] / acc[NSA_HD:NSA_HD + 1, :]

    qs = jnp.concatenate([q_ref[:, h * LANES:(h + 1) * LANES] for h in range(NSA_HEADS)], axis=0)
    u_row = lax.broadcasted_iota(jnp.int32, (1, width), 1) % QT
    t_row = t0 + u_row
    r_kc = lax.broadcasted_iota(jnp.int32, (KC, 1), 0)

    n_grp = ks_ref.shape[0] // SEL_GROUP
    n_full = lax.shift_right_logical(t0, int(math.log2(SEL_GROUP)))

    def qk_group(j):
        return _dot_nt(ks_ref[pl.ds(pl.multiple_of(j * SEL_GROUP, SEL_GROUP), SEL_GROUP), :], qs)

    sc = _dot_nt(kc_ref[...], qs)

    w_slabs, w_chunks = [], []
    for i in range(nw):
        jj = c - (nw - 1) + i
        jc = jnp.maximum(jj, 0)
        si = _dot_nt(kw_ref[pl.ds(pl.multiple_of(jc * KC, KC), KC), :], qs)
        if i == 0:
            keep = (r_kc > u_row) & (jj >= 0)
        elif i == nw - 1:
            keep = r_kc <= u_row
        else:
            keep = jj >= 0
        w_slabs.append(jnp.where(keep, si, NEG))
        w_chunks.append(jc)

    n_col = lax.broadcasted_iota(jnp.int32, (ncmp, 1), 0)
    cmask = (CMP_STRIDE * n_col + (CMP_LEN - 1) <= t_row) & (n_col < ncmp - 1)
    s = jnp.where(cmask, sc, NEG)
    m = jnp.max(s, axis=0, keepdims=True)
    e = jnp.exp2(s - m)
    anyv = (t_row >= CMP_LEN - 1).astype(F32)
    p = e * (anyv / jnp.sum(e, axis=0, keepdims=True))
    o_cmp = pv(vct_ref[...], p)

    psums = []
    for pg in per_group(p):
        acc_p = pg[:, 0:QT]
        for h in range(1, NSA_HPG):
            acc_p = acc_p + pg[:, h * QT:(h + 1) * QT]
        psums.append(acc_p)
    psum = jnp.concatenate(psums, axis=1)
    nq2 = NSA_KV * QT
    s_col = lax.broadcasted_iota(jnp.int32, (nsel, 1), 0)
    n_lane = lax.broadcasted_iota(jnp.int32, (1, ncmp), 1)
    ov = ((CMP_STRIDE * n_lane < SEL_BLOCK * (s_col + 1)) & (CMP_STRIDE * n_lane + (CMP_LEN - 1) >= SEL_BLOCK * s_col)
          ).astype(BF16)
    imp = sum(_dot(ov, part) for part in _split3(psum))

    s_diag = qk_group(n_full)
    sx_scr[...] = qk_group(0)

    mxw = w_slabs[0]
    for sl in w_slabs[1:]:
        mxw = jnp.maximum(mxw, sl)
    mw = jnp.max(mxw, axis=0, keepdims=True)
    acc_w = jnp.zeros((VT_ROWS, width), F32)
    for sl, jc in zip(w_slabs, w_chunks):
        acc_w = acc_w + pv(vwt_ref[jc], jnp.exp2(sl - mw))
    o_win = normalised(acc_w)

    t1 = t0 + lax.broadcasted_iota(jnp.int32, (1, nq2), 1) % QT
    cur = lax.shift_right_logical(t1, 6)
    forced = (s_col == 0) | (s_col == cur) | (s_col == cur - 1)
    valid = SEL_BLOCK * s_col <= t1
    val = jnp.where(valid, jnp.where(forced, imp + SEL_FORCE, imp), NEG)
    sub = 8
    r_sub = lax.broadcasted_iota(jnp.int32, (sub, 1), 0)
    blocks = [val[r * sub:(r + 1) * sub, :] for r in range(nsel // sub)]
    ranks = [jnp.zeros((sub, nq2), F32) for _ in blocks]
    for i in range(nsel):
        vi = val[i:i + 1, :]
        for r, blk in enumerate(blocks):
            if i < r * sub:
                beats = vi >= blk
            elif i >= (r + 1) * sub:
                beats = vi > blk
            else:
                beats = (vi > blk) | ((vi == blk) & (r_sub > i - r * sub))
            ranks[r] = ranks[r] + jnp.where(beats, 1.0, 0.0)
    bias = jnp.where(jnp.concatenate(ranks, axis=0) < SEL_TOPK, 0.0, NEG).astype(F32)
    bias_scr[...] = jnp.concatenate([bias[:, g * QT:(g + 1) * QT] for g in range(NSA_KV) for _ in range(NSA_HPG)],
                                    axis=1)

    r_blk = lax.broadcasted_iota(jnp.int32, (SEL_BLOCK, 1), 0)
    blk_per_grp = SEL_GROUP // SEL_BLOCK
    chunk_per_grp = SEL_GROUP // KC
    blk_per_chunk = KC // SEL_BLOCK

    def sel_update(j, sj, carry, causal=False, live=None):
        m_o, acc = carry
        brows, keeps = [], []
        for i in range(blk_per_grp):
            brow = bias_scr[pl.ds(blk_per_grp * j + i, 1), :]
            brows.append(brow if live is None else jnp.where(live, brow, NEG))
            keeps.append(j * SEL_GROUP + i * SEL_BLOCK + r_blk <= t_row if causal else None)

        def block(i):
            return sj[i * SEL_BLOCK:(i + 1) * SEL_BLOCK, :]

        mx = None
        for i in range(blk_per_grp):
            sl = block(i) + brows[i]
            if causal:
                sl = jnp.where(keeps[i], sl, NEG)
            mx = sl if mx is None else jnp.maximum(mx, sl)
        m_n = jnp.maximum(m_o, jnp.max(mx, axis=0, keepdims=True))
        a = jnp.exp2(m_o - m_n)
        acc = a * acc
        for ci in range(chunk_per_grp):
            parts = []
            for i in range(blk_per_chunk * ci, blk_per_chunk * (ci + 1)):
                arg = block(i) + (brows[i] - m_n)
                parts.append(jnp.exp2(jnp.where(keeps[i], arg, NEG) if causal else arg))
            acc = acc + pv(vst_ref[chunk_per_grp * j + ci], jnp.concatenate(parts, axis=0))
        return m_n, acc

    empty = (jnp.full((1, width), NEG, F32), jnp.zeros((VT_ROWS, width), F32))
    seeded = sel_update(n_full, s_diag, empty, causal=True)

    def pair_body(jp, carry):
        ja, jb = 2 * jp, 2 * jp + 1
        sy_scr[...] = qk_group(jb)
        carry = sel_update(ja, sx_scr, carry)
        sx_scr[...] = qk_group(jnp.minimum(ja + 2, n_grp - 1))
        return sel_update(jb, sy_scr, carry, live=jb < n_full)

    _, acc_s = lax.fori_loop(0, lax.shift_right_logical(n_full + 1, 1), pair_body, seeded)
    o_sel = normalised(acc_s)

    def gate_row(br):
        rows = [smt_ref[SM_GATE + 3 * h + br:SM_GATE + 3 * h + br + 1, :] for h in range(NSA_HEADS)]
        return _sigmoid(jnp.concatenate(rows, axis=1))

    o_t = gate_row(0) * o_cmp + gate_row(1) * o_sel + gate_row(2) * o_win
    for pr in range(NSA_HEADS // 2):
        xp = jnp.concatenate([o_t[:, (2 * pr) * QT:(2 * pr + 1) * QT], o_t[:, (2 * pr + 1) * QT:(2 * pr + 2) * QT]], axis=0)
        o_ref[:, pr * LANES:(pr + 1) * LANES] = xp.T.astype(BF16)


def _nsa(qpad, kcb, vct, ks, kw, vst, vwt, smt, b, s):
    nq = s // QT
    ncmp = kcb.shape[1]
    return pl.pallas_call(
        _nsa_kernel,
        out_shape=jax.ShapeDtypeStruct((b * s, NSA_HEADS * NSA_HD), BF16),
        grid=(b, nq),
        in_specs=[
            pl.BlockSpec((QT, NSA_HEADS * LANES), lambda bi, c: (bi * nq + c, 0)),
            pl.BlockSpec((None, ncmp, LANES), lambda bi, c: (bi, 0, 0)),
            pl.BlockSpec((None, NSA_KV * NSA_HD, ncmp), lambda bi, c: (bi, 0, 0)),
            pl.BlockSpec((s, LANES), lambda bi, c: (bi, 0)),
            pl.BlockSpec((s, LANES), lambda bi, c: (bi, 0)),
            pl.BlockSpec((s // KC, NSA_KV * VT_ROWS, KC), lambda bi, c: (bi, 0, 0)),
            pl.BlockSpec((s // KC, NSA_KV * VT_ROWS, KC), lambda bi, c: (bi, 0, 0)),
            pl.BlockSpec((LANES, QT), lambda bi, c: (0, bi * nq + c)),
        ],
        out_specs=pl.BlockSpec((QT, NSA_HEADS * NSA_HD), lambda bi, c: (bi * nq + c, 0)),
        scratch_shapes=[pltpu.VMEM((s // SEL_BLOCK, NSA_HEADS * QT), F32),
                        pltpu.VMEM((SEL_GROUP, NSA_HEADS * QT), F32),
                        pltpu.VMEM((SEL_GROUP, NSA_HEADS * QT), F32)],
        compiler_params=pltpu.CompilerParams(dimension_semantics=("arbitrary", "arbitrary"),
                                             vmem_limit_bytes=VMEM_LIMIT),
        name="nsa",
    )(qpad, kcb, vct, ks, kw, vst, vwt, smt)


def _log_sigmoid(x):
    return jnp.minimum(x, 0.0) - jnp.log(1.0 + jnp.exp(-jnp.abs(x)))


def _mlstm_kernel(qk_ref, v_ref, og_ref, sm_ref, smt_ref, wc_ref, bc_ref, bifc_ref, bifr_ref, ghn_ref,
                  y_ref, ext_scr, ct_scr, n_scr, m_scr):
    lc = ML_CHUNK

    @pl.when(pl.program_id(1) == 0)
    def _():
        ext_scr[0:8, :] = jnp.zeros((8, 2 * ML_WIDTH), F32)
        ct_scr[...] = jnp.zeros_like(ct_scr)
        n_scr[...] = jnp.zeros_like(n_scr)
        m_scr[...] = jnp.zeros_like(m_scr)

    ext_scr[8:8 + ML_BLOCK, :] = qk_ref[...].astype(F32)
    y = bc_ref[...]
    for j in range(CONV_W):
        y = y + wc_ref[j:j + 1, :] * ext_scr[pl.ds(8 - (CONV_W - 1) + j, ML_BLOCK), :]
    ext_scr[0:8, :] = ext_scr[ML_BLOCK:ML_BLOCK + 8, :]
    qkc = y * _sigmoid(y)
    q_all = qkc[:, 0:ML_WIDTH].astype(BF16)
    k_all = (qkc[:, ML_WIDTH:2 * ML_WIDTH] * (ML_HD ** -0.5)).astype(BF16)

    ifc = sm_ref[...] + bifc_ref[...]
    ifr = smt_ref[...] + bifr_ref[...]
    lfc = _log_sigmoid(ifc)
    lfr = _log_sigmoid(ifr)
    rr = lax.broadcasted_iota(jnp.int32, (lc, lc), 0)
    cc = lax.broadcasted_iota(jnp.int32, (lc, lc), 1)
    causal = rr >= cc
    tri_l = causal.astype(F32)
    tri_u = (rr <= cc).astype(F32)

    for ci in range(ML_BLOCK // lc):
        lo, hi = ci * lc, (ci + 1) * lc
        bc_all = jnp.dot(tri_l, lfc[lo:hi, :], preferred_element_type=F32, precision=lax.Precision.HIGHEST)
        br_all = jnp.dot(lfr[:, lo:hi], tri_u, preferred_element_type=F32, precision=lax.Precision.HIGHEST)
        for h in range(ML_HEADS):
            hs = slice(h * ML_HD, (h + 1) * ML_HD)
            bcol = bc_all[:, SM_F + h:SM_F + h + 1]
            brow = br_all[SM_F + h:SM_F + h + 1, :]
            icol = ifc[lo:hi, SM_I + h:SM_I + h + 1]
            irow = ifr[SM_I + h:SM_I + h + 1, lo:hi]
            mprev = m_scr[h][:, 0:1]
            qh = q_all[lo:hi, hs]
            kh = k_all[lo:hi, hs]
            vh = v_ref[lo:hi, hs]
            dmat = jnp.where(causal, bcol - brow + irow, NEG)
            inter = bcol + mprev
            mt = jnp.maximum(jnp.max(dmat, axis=-1, keepdims=True), inter)
            a = jnp.exp(dmat - mt) * _dot_nt(qh, kh)
            dec = jnp.exp(inter - mt)
            ct = ct_scr[h]
            nrow = n_scr[h]
            num = _dot(a.astype(BF16), vh) + dec * _dot(qh, ct.astype(BF16))
            den = jnp.sum(a, axis=-1, keepdims=True) + dec * jnp.sum(qh.astype(F32) * nrow, axis=-1, keepdims=True)
            hh = num / jnp.maximum(jnp.abs(den), jnp.exp(-mt))
            blast = bcol[lc - 1:lc, :]
            grow = blast - brow + irow
            mnew = jnp.maximum(blast + mprev, jnp.max(grow, axis=-1, keepdims=True))
            wprev = jnp.exp(blast + mprev - mnew)
            kwt = kh.astype(F32) * jnp.exp(blast - bcol + icol - mnew)
            ct_scr[h] = wprev * ct + lax.dot_general(kwt.astype(BF16), vh, _TN, preferred_element_type=F32)
            n_scr[h] = wprev * nrow + jnp.sum(kwt, axis=0, keepdims=True)
            m_scr[h] = jnp.broadcast_to(mnew, (1, LANES))
            hm = hh * _sigmoid(og_ref[lo:hi, hs].astype(F32))
            y_ref[lo:hi, hs] = _rms(hm, ghn_ref[:, hs]).astype(BF16)


def _mlstm(qkb, vb, ob, sm, smt, wconv, bconv, bifc, bifr, ghn, b, s):
    nb = s // ML_BLOCK
    row = lambda w: pl.BlockSpec((ML_BLOCK, w), lambda bi, j: (bi * nb + j, 0))
    return pl.pallas_call(
        _mlstm_kernel,
        out_shape=jax.ShapeDtypeStruct((b * s, ML_WIDTH), BF16),
        grid=(b, nb),
        in_specs=[row(2 * ML_WIDTH), row(ML_WIDTH), row(ML_WIDTH), row(LANES),
                  pl.BlockSpec((LANES, ML_BLOCK), lambda bi, j: (0, bi * nb + j)),
                  _const_spec(wconv.shape), _const_spec(bconv.shape), _const_spec(bifc.shape),
                  _const_spec(bifr.shape), _const_spec(ghn.shape)],
        out_specs=row(ML_WIDTH),
        scratch_shapes=[pltpu.VMEM((ML_BLOCK + 8, 2 * ML_WIDTH), F32),
                        pltpu.VMEM((ML_HEADS, ML_HD, ML_HD), F32),
                        pltpu.VMEM((ML_HEADS, 1, ML_HD), F32),
                        pltpu.VMEM((ML_HEADS, 1, LANES), F32)],
        compiler_params=pltpu.CompilerParams(dimension_semantics=("arbitrary", "arbitrary"),
                                             vmem_limit_bytes=VMEM_LIMIT),
        name="mlstm",
    )(qkb, vb, ob, sm, smt, wconv, bconv, bifc, bifr, ghn)


RT_BUCKET = N_EXPERTS
RT_RANK = N_EXPERTS + 1
RT_WLO = N_EXPERTS + 2
RT_WHI = N_EXPERTS + 3
N_BUCKETS = N_GROUPS * 6
X_ROWS = D_MODEL // LANES
REC = 2 * X_ROWS
DMA_UNROLL = 8


def _merge_kernel(ya_ref, yb_ref, mg_ref, x_ref, wpa_ref, wpb_ref, wout_ref, gffn_ref, wr_ref, br_ref,
                  x1_ref, h2_ref, slab_ref, hist_ref):
    pa = _dot(ya_ref[...], wpa_ref[...])
    pb = _dot(yb_ref[...], wpb_ref[...])
    ga = _sigmoid(mg_ref[:, 0:D_MODEL].astype(F32))
    gb = _sigmoid(mg_ref[:, D_MODEL:2 * D_MODEL].astype(F32))
    mixed = (ga * pa + gb * pb).astype(BF16)
    x1 = x_ref[...] + _dot(mixed, wout_ref[...])
    x1_ref[...] = x1
    h2 = _rms(x1, gffn_ref[...])
    h_hi = h2.astype(BF16)
    h2_ref[...] = h2

    h_lo = (h2 - h_hi.astype(F32)).astype(BF16)
    r_hi = _dot(h_hi, wr_ref[...])
    logit = r_hi[:, 0:LANES] + r_hi[:, LANES:2 * LANES] + _dot(h_lo, wr_ref[:, 0:LANES]) + br_ref[...]
    lane = lax.broadcasted_iota(jnp.int32, logit.shape, 1)
    big = jnp.int32(LANES)
    gmask = (lane >= N_EXPERTS) & (lane < N_EXPERTS + N_GROUPS)
    gl = jnp.where(gmask, logit, NEG)
    gmax = jnp.max(gl, axis=-1, keepdims=True)
    gidx = jnp.min(jnp.where(gmask & (gl == gmax), lane, big), axis=-1, keepdims=True) - N_EXPERTS
    pg_sel = 1.0 / jnp.sum(jnp.where(gmask, jnp.exp(gl - gmax), 0.0), axis=-1, keepdims=True)
    emask = (lane < N_EXPERTS) & (lax.shift_right_logical(lane, 2) == gidx)
    el = jnp.where(emask, logit, NEG)
    e1 = jnp.max(el, axis=-1, keepdims=True)
    i1 = jnp.min(jnp.where(emask & (el == e1), lane, big), axis=-1, keepdims=True)
    emask2 = emask & (lane != i1)
    el2 = jnp.where(emask2, logit, NEG)
    e2 = jnp.max(el2, axis=-1, keepdims=True)
    i2 = jnp.min(jnp.where(emask2 & (el2 == e2), lane, big), axis=-1, keepdims=True)
    x21 = jnp.exp(e2 - e1)
    w1 = pg_sel / (1.0 + x21)
    w2 = pg_sel * x21 / (1.0 + x21)
    first_lo = i1 < i2
    e_lo = jnp.where(first_lo, i1, i2) - EXP_PER_GROUP * gidx
    e_hi = jnp.where(first_lo, i2, i1) - EXP_PER_GROUP * gidx
    pair = lax.shift_right_logical(e_lo * (2 * EXP_PER_GROUP - 1 - e_lo), 1) + (e_hi - e_lo - 1)
    bucket = 6 * gidx + pair
    member = lane == bucket
    onehot = jnp.where(member, 1.0, 0.0)
    rr = lax.broadcasted_iota(jnp.int32, (TD, TD), 0)
    cc = lax.broadcasted_iota(jnp.int32, (TD, TD), 1)
    earlier = _dot((rr > cc).astype(BF16), onehot.astype(BF16))
    rank = jnp.sum(jnp.where(member, earlier, 0.0), axis=-1, keepdims=True)
    slab = jnp.where(lane == i1, w1, 0.0) + jnp.where(lane == i2, w2, 0.0)
    slab = jnp.where(lane == RT_BUCKET, bucket.astype(F32), slab)
    slab = jnp.where(lane == RT_RANK, rank, slab)
    slab = jnp.where(lane == RT_WLO, jnp.where(first_lo, w1, w2), slab)
    slab = jnp.where(lane == RT_WHI, jnp.where(first_lo, w2, w1), slab)
    slab_ref[...] = slab
    hist_ref[...] = jnp.broadcast_to(jnp.sum(onehot, axis=0, keepdims=True), (8, LANES))


def _merge(ya, yb, mg, x2, wpa, wpb, wout, gffn, wr, br):
    t = x2.shape[0]
    row = lambda w: pl.BlockSpec((TD, w), lambda i: (i, 0))
    return pl.pallas_call(
        _merge_kernel,
        out_shape=[jax.ShapeDtypeStruct((t, D_MODEL), F32),
                   jax.ShapeDtypeStruct((t, D_MODEL), F32),
                   jax.ShapeDtypeStruct((t, LANES), F32),
                   jax.ShapeDtypeStruct((t // TD * 8, LANES), F32)],
        grid=(t // TD,),
        in_specs=[row(NSA_HEADS * NSA_HD), row(ML_WIDTH), row(2 * D_MODEL), row(D_MODEL)]
                 + [_const_spec(a.shape) for a in (wpa, wpb, wout, gffn, wr, br)],
        out_specs=[row(D_MODEL), row(D_MODEL), row(LANES),
                   pl.BlockSpec((8, LANES), lambda i: (i, 0))],
        compiler_params=pltpu.CompilerParams(dimension_semantics=("arbitrary",), vmem_limit_bytes=VMEM_LIMIT),
        name="merge",
    )(ya, yb, mg, x2, wpa, wpb, wout, gffn, wr, br)


def _rec_copy(src_ref, src_tok, dst_ref, dst_tok, sem, rows):
    src = src_ref.at[pl.ds(pl.multiple_of(src_tok * rows, rows), rows), :]
    dst = dst_ref.at[pl.ds(pl.multiple_of(dst_tok * rows, rows), rows), :]
    return pltpu.make_async_copy(src, dst, sem)


def _token_copies(n, make, wait=False):
    def body(g, carry):
        for u in range(DMA_UNROLL):
            cp = make(g * DMA_UNROLL + u)
            if wait:
                cp.wait()
            else:
                cp.start(priority=u % 2)
        return carry
    lax.fori_loop(0, n // DMA_UNROLL, body, 0)


def _dispatch_kernel(pos_ref, tail_ref, h2_ref, slab_ref, out_ref, stage_scr, zero_scr, sem, zsem):
    i = pl.program_id(0)
    slot = lax.rem(i, 2)
    n_tiles = out_ref.shape[0] // (TM * REC)
    n_used = tail_ref[2 * N_BUCKETS]

    def zero_copy(first_slot):
        start = pl.multiple_of(first_slot * REC, TM * REC)
        return pltpu.make_async_copy(zero_scr, out_ref.at[pl.ds(start, TM * REC), :], zsem)

    @pl.when(i == 0)
    def _():
        zero_scr[...] = jnp.zeros_like(zero_scr)
        for phase in ("start", "wait"):
            for b in range(N_BUCKETS):
                @pl.when(tail_ref[N_BUCKETS + b] > 0)
                def _():
                    getattr(zero_copy(tail_ref[b]), phase)()

                @pl.when(n_used + b < n_tiles)
                def _():
                    getattr(zero_copy((n_used + b) * TM), phase)()

    stage = stage_scr.at[slot]
    for j in range(X_ROWS):
        stage[pl.ds(j, TM, stride=REC), :] = h2_ref[:, j * LANES:(j + 1) * LANES]
    stage[pl.ds(X_ROWS, TM, stride=REC), :] = slab_ref[...]
    for j in range(X_ROWS + 1, REC):
        stage[pl.ds(j, TM, stride=REC), :] = jnp.zeros((TM, LANES), F32)
    base = i * TM
    _token_copies(TM, lambda r: _rec_copy(stage_scr.at[slot], r, out_ref, pos_ref[base + r], sem.at[slot], REC))

    def drain(which):
        _token_copies(TM, lambda r: _rec_copy(stage_scr.at[which], 0, out_ref, 0, sem.at[which], REC), wait=True)

    @pl.when(i > 0)
    def _():
        drain(1 - slot)

    @pl.when(i == pl.num_programs(0) - 1)
    def _():
        drain(slot)


def _dispatch(pos, tail, h2, slab, n_slots):
    t = h2.shape[0]
    return pl.pallas_call(
        _dispatch_kernel,
        out_shape=jax.ShapeDtypeStruct((n_slots * REC, LANES), F32),
        grid_spec=pltpu.PrefetchScalarGridSpec(
            num_scalar_prefetch=2,
            grid=(t // TM,),
            in_specs=[pl.BlockSpec((TM, D_MODEL), lambda i, pos_r, tail_r: (i, 0)),
                      pl.BlockSpec((TM, LANES), lambda i, pos_r, tail_r: (i, 0))],
            out_specs=pl.BlockSpec(memory_space=pl.ANY),
            scratch_shapes=[pltpu.VMEM((2, TM * REC, LANES), F32), pltpu.VMEM((TM * REC, LANES), F32),
                            pltpu.SemaphoreType.DMA((2,)), pltpu.SemaphoreType.DMA(())],
        ),
        compiler_params=pltpu.CompilerParams(dimension_semantics=("arbitrary",), vmem_limit_bytes=VMEM_LIMIT,
                                             has_side_effects=True),
        name="dispatch",
    )(pos, tail, h2, slab)


def _moe_kernel(te_ref, nu_ref, hx_ref, w13_ref, w2_ref, y_ref):
    k = pl.program_id(0)
    n_tiles = pl.num_programs(0)
    n_used = nu_ref[0]

    @pl.when(k < n_used)
    def _():
        h = jnp.concatenate([hx_ref[pl.ds(j, TM, stride=REC), :] for j in range(X_ROWS)], axis=1).astype(BF16)
        slab = hx_ref[pl.ds(X_ROWS, TM, stride=REC), :]
        y = None
        for side, lane in ((0, RT_WLO), (1, RT_WHI)):
            e = te_ref[side * n_tiles + k]
            a = _dot(h, w13_ref[e])
            gt = a[:, 0:D_EXPERT]
            act = gt * _sigmoid(gt) * a[:, D_EXPERT:2 * D_EXPERT] * slab[:, lane:lane + 1]
            part = _dot(act.astype(BF16), w2_ref[e])
            y = part if y is None else y + part
        for j in range(X_ROWS):
            y_ref[pl.ds(j, TM, stride=X_ROWS), :] = y[:, j * LANES:(j + 1) * LANES]

    @pl.when(k >= n_used)
    def _():
        y_ref[...] = jnp.zeros_like(y_ref)


def _moe(tile_e, n_used, hx_sorted, w13, w2):
    n_tiles = hx_sorted.shape[0] // (TM * REC)
    return pl.pallas_call(
        _moe_kernel,
        out_shape=jax.ShapeDtypeStruct((n_tiles * TM * X_ROWS, LANES), F32),
        grid_spec=pltpu.PrefetchScalarGridSpec(
            num_scalar_prefetch=2,
            grid=(n_tiles,),
            in_specs=[pl.BlockSpec((TM * REC, LANES), lambda k, te, nu: (jnp.minimum(k, nu[0] - 1), 0)),
                      pl.BlockSpec(w13.shape, lambda k, te, nu: (0, 0, 0), pipeline_mode=pl.Buffered(1)),
                      pl.BlockSpec(w2.shape, lambda k, te, nu: (0, 0, 0), pipeline_mode=pl.Buffered(1))],
            out_specs=pl.BlockSpec((TM * X_ROWS, LANES), lambda k, te, nu: (k, 0)),
        ),
        compiler_params=pltpu.CompilerParams(dimension_semantics=("arbitrary",), vmem_limit_bytes=VMEM_LIMIT),
        name="moe",
    )(tile_e, n_used, hx_sorted, w13, w2)


def _combine_kernel(pos_ref, y_ref, x1_ref, p_ref, gple_ref, wpg_ref, wpp_ref, gfin_ref, o_ref, ybuf, sem):
    i = pl.program_id(0)
    slot = lax.rem(i, 2)

    def gather(tile, which):
        _token_copies(TM, lambda r: _rec_copy(y_ref, pos_ref[tile * TM + r], ybuf.at[which], r, sem.at[which],
                                              X_ROWS))

    @pl.when(i == 0)
    def _():
        gather(0, 0)

    @pl.when(i + 1 < pl.num_programs(0))
    def _():
        gather(i + 1, 1 - slot)

    _token_copies(TM, lambda r: _rec_copy(y_ref, 0, ybuf.at[slot], 0, sem.at[slot], X_ROWS), wait=True)
    yb = ybuf.at[slot]
    x2 = x1_ref[...] + jnp.concatenate([yb[pl.ds(j, TM, stride=X_ROWS), :] for j in range(X_ROWS)], axis=1)
    h3 = _rms(x2, gple_ref[...]).astype(BF16)
    x3 = x2 + _sigmoid(_dot(h3, wpg_ref[...])) * _dot(p_ref[...].astype(BF16), wpp_ref[...])
    o_ref[...] = _rms(x3, gfin_ref[...])


def _combine(pos, y_sorted, x1, p2, gple, wpg, wpp, gfin):
    t = x1.shape[0]
    row = lambda w: pl.BlockSpec((TM, w), lambda i, pos_r: (i, 0))
    const = lambda a: pl.BlockSpec(a.shape, lambda i, pos_r: (0,) * a.ndim, pipeline_mode=pl.Buffered(1))
    return pl.pallas_call(
        _combine_kernel,
        out_shape=jax.ShapeDtypeStruct((t, D_MODEL), F32),
        grid_spec=pltpu.PrefetchScalarGridSpec(
            num_scalar_prefetch=1,
            grid=(t // TM,),
            in_specs=[pl.BlockSpec(memory_space=pl.ANY), row(D_MODEL), row(PLE_DIM),
                      const(gple), const(wpg), const(wpp), const(gfin)],
            out_specs=row(D_MODEL),
            scratch_shapes=[pltpu.VMEM((2, TM * X_ROWS, LANES), F32), pltpu.SemaphoreType.DMA((2,))],
        ),
        compiler_params=pltpu.CompilerParams(dimension_semantics=("arbitrary",), vmem_limit_bytes=VMEM_LIMIT),
        name="combine",
    )(pos, y_sorted, x1, p2, gple, wpg, wpp, gfin)


def _routing_tables(slab, hist8):
    t = slab.shape[0]
    nt = t // TD
    n_tiles = t // TM + N_BUCKETS
    hist = hist8.reshape(nt, 8, LANES)[:, 0, :]
    counts = jnp.sum(hist, axis=0)
    padded = jnp.ceil(counts / TM) * TM
    ends = jnp.cumsum(padded)
    first = (ends - padded)[None, :] + jnp.cumsum(hist, axis=0) - hist
    lane = jnp.arange(LANES, dtype=F32)[None, :]
    mine = lane == slab[:, RT_BUCKET:RT_BUCKET + 1]
    pos = jnp.sum(jnp.where(mine, jnp.repeat(first, TD, axis=0), 0.0), axis=1) + slab[:, RT_RANK]
    starts = jnp.arange(n_tiles, dtype=F32) * TM
    tile_bucket = jnp.minimum(jnp.sum(ends[None, :N_BUCKETS] <= starts[:, None], axis=1), N_BUCKETS - 1)
    group, pair = tile_bucket // 6, tile_bucket % 6
    e_lo = EXP_PER_GROUP * group + jnp.array([0, 0, 0, 1, 1, 2], jnp.int32)[pair]
    e_hi = EXP_PER_GROUP * group + jnp.array([1, 2, 3, 2, 3, 3], jnp.int32)[pair]
    tile_e = jnp.concatenate([e_lo, e_hi]).astype(jnp.int32)
    n_used = (ends[N_BUCKETS - 1] / TM).astype(jnp.int32).reshape(1)
    tail = jnp.concatenate([(ends - TM)[:N_BUCKETS], padded[:N_BUCKETS], n_used.astype(F32)]).astype(jnp.int32)
    return pos.astype(jnp.int32), tile_e, n_used, tail, n_tiles * TM


def _pack_inproj_weights(w):
    d = w.shape[0]
    qw = NSA_HEADS * NSA_HD
    kvw = NSA_KV * NSA_HD
    o = 0
    wq = w[:, o:o + qw]; o += qw
    wkc = w[:, o:o + kvw]; o += kvw
    wvc = w[:, o:o + kvw]; o += kvw
    wks = w[:, o:o + kvw]; o += kvw
    wvs = w[:, o:o + kvw]; o += kvw
    wkw = w[:, o:o + kvw]; o += kvw
    wvw = w[:, o:o + kvw]; o += kvw
    wga = w[:, o:o + 3 * NSA_HEADS]; o += 3 * NSA_HEADS
    wqkb = w[:, o:o + 2 * ML_WIDTH]; o += 2 * ML_WIDTH
    wvb = w[:, o:o + ML_WIDTH]; o += ML_WIDTH
    wob = w[:, o:o + ML_WIDTH]; o += ML_WIDTH
    wif = w[:, o:o + 2 * ML_HEADS]; o += 2 * ML_HEADS
    wmg = w[:, o:o + 2 * D_MODEL]
    zero64 = jnp.zeros((d, NSA_HD), w.dtype)
    qcols = []
    for h in range(NSA_HEADS):
        wh = wq[:, h * NSA_HD:(h + 1) * NSA_HD]
        qcols += [wh, zero64] if h // NSA_HPG == 0 else [zero64, wh]
    wsm = jnp.concatenate([wga, wif, jnp.zeros((d, LANES - 3 * NSA_HEADS - 2 * ML_HEADS), w.dtype)], axis=1)
    wcat = jnp.concatenate(qcols + [wkc, wvc, wks, wkw, wsm, wqkb, wvb, wob, wmg], axis=1).astype(BF16)
    wtr = jnp.concatenate([wvs, wvw, wsm], axis=1).T.astype(BF16)
    return wcat, wtr


def _rope_lane_tables(positions):
    inv = ROPE_THETA ** (-jnp.arange(0, ROPE_DIM, 2, dtype=F32) / ROPE_DIM)
    ang = positions.astype(F32).reshape(-1, 1) * inv[None, :]
    cos, sin = jnp.cos(ang), jnp.sin(ang)
    half = ROPE_DIM // 2
    d = jnp.arange(LANES) % NSA_HD
    cos_l, sin_l = jnp.tile(cos, (1, LANES // half)), jnp.tile(sin, (1, LANES // half))
    rc = jnp.where(d < ROPE_DIM, cos_l, 1.0)
    rp = jnp.where((d >= half) & (d < ROPE_DIM), sin_l, 0.0)
    rm = jnp.where(d < half, -sin_l, 0.0)
    return rc, rp, rm


def _pack_compress_weights(w1, w2, pe):
    half = CMP_LEN // 2
    w1r = w1.reshape(2, half, NSA_HD, CMP_HIDDEN)
    outs = []
    for part in range(2):
        wb = w1r[part].astype(BF16)
        zb = jnp.zeros_like(wb)
        wp = jnp.stack([jnp.stack([wb, zb], axis=2), jnp.stack([zb, wb], axis=2)], axis=1)
        outs.append(wp.reshape(half * NSA_KV * NSA_HD, NSA_KV * CMP_HIDDEN))
    pe8 = jnp.broadcast_to(pe.reshape(1, CMP_LEN * NSA_HD), (8, CMP_LEN * NSA_HD)).astype(BF16)
    return outs[0], outs[1], pe8, w1.astype(BF16)


def _stages(x, p, positions, g_mix, w_in, b_if, w_ck1, w_ck2, pe_ck, w_cv1, w_cv2, pe_cv, w_conv, b_conv, g_hn, w_pa, w_pb, w_out, g_ffn, w_rg, b_rg, w_re, b_re, w_e13, w_e2, g_ple, w_pg, w_pp, g_final):
    b, s, d = x.shape
    t = b * s
    rc, rp, rm = _rope_lane_tables(positions)
    assert w_in.shape[0] == 1, "the final norm is fused into the layer's last kernel: single-layer problem only"
    for i in range(w_in.shape[0]):
        x2 = x.reshape(t, d)
        wcat, wtr = _pack_inproj_weights(w_in[i])
        (qpad, kc_tok, vc_tok, ks, kw, vst, vwt, sm, smt, qkb, vb, ob, mg) = _inproj(
            x2, g_mix[i].reshape(1, d), wcat, wtr, rc, rp, rm)
        wka, wkb, pek, w1k = _pack_compress_weights(w_ck1[i], w_ck2[i], pe_ck[i])
        wva, wvb, pev, w1v = _pack_compress_weights(w_cv1[i], w_cv2[i], pe_cv[i])
        zpad = jnp.zeros((CMP_HIDDEN, NSA_HD), F32)
        w2k = jnp.stack([jnp.concatenate([w_ck2[i], zpad], axis=1),
                         jnp.concatenate([zpad, w_ck2[i]], axis=1)]).astype(BF16)
        w2vt = w_cv2[i].T.astype(BF16)
        nrow = s // CMP_STRIDE
        rk = kc_tok.reshape(b, nrow, CMP_STRIDE * LANES)
        rv = vc_tok.reshape(b, nrow, CMP_STRIDE * LANES)
        kcb, vct = _compress(rk, rv, wka, wkb, wva, wvb, pek, pev, w1k, w1v, w2k, w2vt)
        ya = _nsa(qpad, kcb, vct, ks, kw, vst, vwt, smt, b, s)
        bif = b_if[i].astype(F32)
        bifc = jnp.zeros((1, LANES), F32).at[0, SM_I:SM_I + 2 * ML_HEADS].set(bif)
        bifr = bifc.reshape(LANES, 1)
        yb = _mlstm(qkb, vb, ob, sm, smt, w_conv[i], b_conv[i].reshape(1, -1), bifc, bifr,
                    g_hn[i].reshape(1, -1), b, s)
        wr = jnp.concatenate([w_re[i], w_rg[i], jnp.zeros((d, LANES - N_EXPERTS - N_GROUPS), F32)], axis=1)
        wr_hi = wr.astype(BF16)
        wr = jnp.concatenate([wr_hi, (wr - wr_hi.astype(F32)).astype(BF16)], axis=1)
        br =jnp.concatenate([b_re[i], b_rg[i], jnp.zeros((LANES - N_EXPERTS - N_GROUPS,), F32)]).reshape(1, LANES)
        x1, h2, slab, hist8 = _merge(ya, yb, mg, x2, w_pa[i].astype(BF16), w_pb[i].astype(BF16),
                                     w_out[i].astype(BF16), g_ffn[i].reshape(1, d), wr, br)
        pos, tile_e, n_used, tail, n_slots = _routing_tables(slab, hist8)
        hx_sorted = _dispatch(pos, tail, h2, slab, n_slots)
        y_sorted = _moe(tile_e, n_used, hx_sorted, w_e13[i].astype(BF16), w_e2[i].astype(BF16))
        out = _combine(pos, y_sorted, x1, p[i].reshape(t, PLE_DIM), g_ple[i].reshape(1, d), w_pg[i].astype(BF16),
                       w_pp[i].astype(BF16), g_final.reshape(1, d))
        x = out.reshape(b, s, d)
    return dict(out=x, qpad=qpad, ks=ks, kcb=kcb, vct=vct, y_a=ya, y_b=yb, x1=x1, pos=pos)


def kernel(x, p, positions, g_mix, w_in, b_if, w_ck1, w_ck2, pe_ck, w_cv1, w_cv2, pe_cv, w_conv, b_conv, g_hn, w_pa, w_pb, w_out, g_ffn, w_rg, b_rg, w_re, b_re, w_e13, w_e2, g_ple, w_pg, w_pp, g_final):
    return _stages(x, p, positions, g_mix, w_in, b_if, w_ck1, w_ck2, pe_ck, w_cv1, w_cv2, pe_cv, w_conv, b_conv, g_hn,
                   w_pa, w_pb, w_out, g_ffn, w_rg, b_rg, w_re, b_re, w_e13, w_e2, g_ple, w_pg, w_pp, g_final)["out"]
```

```python
import functools
import math

import jax
import jax.numpy as jnp
from jax import lax
from jax.experimental import pallas as pl
from jax.experimental.pallas import tpu as pltpu

F32 = jnp.float32
BF16 = jnp.bfloat16

EPS = 1e-6
NEG = -1e30

D_MODEL = 1024
PLE_DIM = 256
NSA_HEADS = 8
NSA_KV = 2
NSA_HPG = NSA_HEADS // NSA_KV
NSA_HD = 64
CMP_LEN = 32
CMP_STRIDE = 16
CMP_HIDDEN = 256
SEL_BLOCK = 64
SEL_TOPK = 16
SEL_FORCE = 1000.0
WINDOW = 512
ROPE_THETA = 500000.0
ROPE_DIM = NSA_HD // 4
ML_HEADS = 4
ML_HD = 128
ML_WIDTH = ML_HEADS * ML_HD
CONV_W = 4
N_GROUPS = 4
EXP_PER_GROUP = 4
N_EXPERTS = N_GROUPS * EXP_PER_GROUP
D_EXPERT = 256

LANES = 128
QT = 128
KC = 128
SEL_GROUP = 512
VT_PAD = 16
VT_ROWS = NSA_HD + VT_PAD
ML_CHUNK = 128
ML_BLOCK = 256
TD = 512
TM = 256
VMEM_LIMIT = 56 * 1024 * 1024

_NT = (((1,), (1,)), ((), ()))
_TN = (((0,), (0,)), ((), ()))

SM_GATE = 0
SM_I = 3 * NSA_HEADS
SM_F = SM_I + ML_HEADS


def _dot(a, b):
    return jnp.dot(a, b, preferred_element_type=F32)


def _dot_nt(a, b):
    return lax.dot_general(a, b, _NT, preferred_element_type=F32)


def _split3(x):
    hi = x.astype(BF16)
    r1 = x - hi.astype(F32)
    mid = r1.astype(BF16)
    lo = (r1 - mid.astype(F32)).astype(BF16)
    return hi, mid, lo


def _rms(x, g):
    return x * lax.rsqrt(jnp.mean(x * x, axis=-1, keepdims=True) + EPS) * g


def _sigmoid(x):
    return 0.5 + 0.5 * jnp.tanh(0.5 * x)


def _const_spec(shape):
    nd = len(shape)
    return pl.BlockSpec(shape, lambda *_: (0,) * nd, pipeline_mode=pl.Buffered(1))


_C_Q = 0
_C_KC = _C_Q + NSA_HEADS * NSA_HD
_C_VC = _C_KC + LANES
_C_KS = _C_VC + LANES
_C_KW = _C_KS + LANES
_C_SM = _C_KW + LANES
_C_QKB = _C_SM + LANES
_C_VB = _C_QKB + 2 * ML_WIDTH
_C_OB = _C_VB + ML_WIDTH
_C_MG = _C_OB + ML_WIDTH
_C_END = _C_MG + 2 * D_MODEL


def _inproj_kernel(x_ref, g_ref, w_ref, wt_ref, rc_ref, rp_ref, rm_ref,
                   q_ref, kc_ref, vc_ref, ks_ref, kw_ref, vst_ref, vwt_ref, sm_ref, smt_ref,
                   qkb_ref, vb_ref, ob_ref, mg_ref):
    hn = _rms(x_ref[...], g_ref[...]).astype(BF16)
    rc, rp, rm = rc_ref[...], rp_ref[...], rm_ref[...]

    def rope(z):
        return z * rc + pltpu.roll(z, 8, 1) * rp + pltpu.roll(z, LANES - 8, 1) * rm

    scale = NSA_HD ** -0.5 * math.log2(math.e)
    for h in range(NSA_HEADS * NSA_HD // LANES):
        z = _dot(hn, w_ref[:, _C_Q + h * LANES:_C_Q + (h + 1) * LANES])
        q_ref[:, h * LANES:(h + 1) * LANES] = (rope(z) * scale).astype(BF16)
    kc_ref[...] = rope(_dot(hn, w_ref[:, _C_KC:_C_KC + LANES])).astype(BF16)
    vc_ref[...] = _dot(hn, w_ref[:, _C_VC:_C_VC + LANES]).astype(BF16)
    ks_ref[...] = rope(_dot(hn, w_ref[:, _C_KS:_C_KS + LANES])).astype(BF16)
    kw_ref[...] = rope(_dot(hn, w_ref[:, _C_KW:_C_KW + LANES])).astype(BF16)
    sm_ref[...] = _dot(hn, w_ref[:, _C_SM:_C_SM + LANES])
    for c0 in range(0, 2 * ML_WIDTH, 512):
        qkb_ref[:, c0:c0 + 512] = _dot(hn, w_ref[:, _C_QKB + c0:_C_QKB + c0 + 512]).astype(BF16)
    vb_ref[...] = _dot(hn, w_ref[:, _C_VB:_C_VB + ML_WIDTH]).astype(BF16)
    ob_ref[...] = _dot(hn, w_ref[:, _C_OB:_C_OB + ML_WIDTH]).astype(BF16)
    for c0 in range(0, 2 * D_MODEL, 512):
        mg_ref[:, c0:c0 + 512] = _dot(hn, w_ref[:, _C_MG + c0:_C_MG + c0 + 512]).astype(BF16)
    zt = _dot_nt(wt_ref[...], hn)
    ones_rows = (lax.broadcasted_iota(jnp.int32, (VT_PAD, KC), 0) == 0).astype(BF16)
    for i in range(TD // KC):
        for ref, r0 in ((vst_ref, 0), (vwt_ref, LANES)):
            zc = zt[r0:r0 + LANES, i * KC:(i + 1) * KC].astype(BF16)
            ref[i] = jnp.concatenate([piece for g in range(NSA_KV)
                                      for piece in (zc[g * NSA_HD:(g + 1) * NSA_HD, :], ones_rows)], axis=0)
    smt_ref[...] = zt[2 * LANES:3 * LANES, :]


def _inproj(x2, g_mix, wcat, wtr, rc, rp, rm):
    t = x2.shape[0]
    row = lambda w: pl.BlockSpec((TD, w), lambda i: (i, 0))
    out_shape = [
        jax.ShapeDtypeStruct((t, NSA_HEADS * NSA_HD), BF16),
        jax.ShapeDtypeStruct((t, LANES), BF16),
        jax.ShapeDtypeStruct((t, LANES), BF16),
        jax.ShapeDtypeStruct((t, LANES), BF16),
        jax.ShapeDtypeStruct((t, LANES), BF16),
        jax.ShapeDtypeStruct((t // KC, NSA_KV * VT_ROWS, KC), BF16),
        jax.ShapeDtypeStruct((t // KC, NSA_KV * VT_ROWS, KC), BF16),
        jax.ShapeDtypeStruct((t, LANES), F32),
        jax.ShapeDtypeStruct((LANES, t), F32),
        jax.ShapeDtypeStruct((t, 2 * ML_WIDTH), BF16),
        jax.ShapeDtypeStruct((t, ML_WIDTH), BF16),
        jax.ShapeDtypeStruct((t, ML_WIDTH), BF16),
        jax.ShapeDtypeStruct((t, 2 * D_MODEL), BF16),
    ]
    chunk3 = pl.BlockSpec((TD // KC, NSA_KV * VT_ROWS, KC), lambda i: (i, 0, 0))
    out_specs = [row(NSA_HEADS * NSA_HD), row(LANES), row(LANES), row(LANES), row(LANES), chunk3, chunk3,
                 row(LANES), pl.BlockSpec((LANES, TD), lambda i: (0, i)),
                 row(2 * ML_WIDTH), row(ML_WIDTH), row(ML_WIDTH), row(2 * D_MODEL)]
    return pl.pallas_call(
        _inproj_kernel,
        out_shape=out_shape,
        grid=(t // TD,),
        in_specs=[row(D_MODEL), _const_spec((1, D_MODEL)), _const_spec((D_MODEL, _C_END)),
                  _const_spec((3 * LANES, D_MODEL)), row(LANES), row(LANES), row(LANES)],
        out_specs=out_specs,
        compiler_params=pltpu.CompilerParams(dimension_semantics=("arbitrary",), vmem_limit_bytes=VMEM_LIMIT),
        name="inproj",
    )(x2, g_mix, wcat, wtr, rc, rp, rm)


def _gelu_tanh(x):
    return 0.5 * x * (1.0 + jnp.tanh(math.sqrt(2.0 / math.pi) * (x + 0.044715 * x * x * x)))


def _compress_kernel(rk_ref, rv_ref, wka_ref, wkb_ref, wva_ref, wvb_ref, pek_ref, pev_ref,
                     w1k_ref, w1v_ref, w2k_ref, w2vt_ref, kc_ref, vct_ref):
    nrow = rk_ref.shape[0]

    def hidden(r_ref, wa_ref, wb_ref, pe_ref, w1_ref):
        r = r_ref[...]
        ha = _dot(r, wa_ref[...])
        hb = _dot(r, wb_ref[...])
        hb = pltpu.roll(hb, nrow - 1, 0)
        c = _dot(pe_ref[...], w1_ref[...])[0:1, :]
        return [_gelu_tanh(ha[:, g * CMP_HIDDEN:(g + 1) * CMP_HIDDEN] + hb[:, g * CMP_HIDDEN:(g + 1) * CMP_HIDDEN] + c).astype(BF16)
                for g in range(NSA_KV)]

    ak = hidden(rk_ref, wka_ref, wkb_ref, pek_ref, w1k_ref)
    kc_ref[...] = (_dot(ak[0], w2k_ref[0]) + _dot(ak[1], w2k_ref[1])).astype(BF16)
    av = hidden(rv_ref, wva_ref, wvb_ref, pev_ref, w1v_ref)
    for g in range(NSA_KV):
        vct_ref[g * NSA_HD:(g + 1) * NSA_HD, :] = _dot_nt(w2vt_ref[...], av[g]).astype(BF16)


def _compress(rk, rv, wka, wkb, wva, wvb, pek, pev, w1k, w1v, w2k, w2vt):
    b, nrow, width = rk.shape
    blk = pl.BlockSpec((None, nrow, width), lambda i: (i, 0, 0))
    return pl.pallas_call(
        _compress_kernel,
        out_shape=[jax.ShapeDtypeStruct((b, nrow, LANES), BF16),
                   jax.ShapeDtypeStruct((b, LANES, nrow), BF16)],
        grid=(b,),
        in_specs=[blk, blk] + [_const_spec(a.shape) for a in (wka, wkb, wva, wvb, pek, pev, w1k, w1v, w2k, w2vt)],
        out_specs=[pl.BlockSpec((None, nrow, LANES), lambda i: (i, 0, 0)),
                   pl.BlockSpec((None, LANES, nrow), lambda i: (i, 0, 0))],
        compiler_params=pltpu.CompilerParams(dimension_semantics=("arbitrary",), vmem_limit_bytes=VMEM_LIMIT),
        name="compress",
    )(rk, rv, wka, wkb, wva, wvb, pek, pev, w1k, w1v, w2k, w2vt)


def _nsa_kernel(q_ref, kc_ref, vct_ref, ks_ref, kw_ref, vst_ref, vwt_ref, smt_ref, o_ref, bias_scr, sx_scr, sy_scr):
    c = pl.program_id(1)
    t0 = c * QT
    ncmp = kc_ref.shape[0]
    nsel = bias_scr.shape[0]
    nw = WINDOW // KC + 1
    gw = NSA_HPG * QT
    width = NSA_KV * gw

    def per_group(x):
        return [x[:, g * gw:(g + 1) * gw] for g in range(NSA_KV)]

    def pv(vt, p):
        rows = vt.shape[0] // NSA_KV
        pb = p.astype(BF16)
        return jnp.concatenate([_dot(vt[g * rows:(g + 1) * rows, :], pg) for g, pg in enumerate(per_group(pb))],
                               axis=1)

    def normalised(acc):
        return acc[0:NSA_HD, :] / acc[NSA_HD:NSA_HD + 1, :]

    low_half = lax.broadcasted_iota(jnp.int32, (1, LANES), 1) < NSA_HD
    q_heads = []
    for h in range(NSA_HEADS):
        pair = q_ref[:, (h // 2) * LANES:(h // 2 + 1) * LANES].astype(F32)
        want_low = h // NSA_HPG == 0
        if (h % 2 == 0) != want_low:
            pair = pltpu.roll(pair, NSA_HD, 1)
        q_heads.append(jnp.where(low_half if want_low else ~low_half, pair, 0.0).astype(BF16))
    qs = jnp.concatenate(q_heads, axis=0)
    u_row = lax.broadcasted_iota(jnp.int32, (1, width), 1) % QT
    t_row = t0 + u_row
    r_kc = lax.broadcasted_iota(jnp.int32, (KC, 1), 0)

    n_grp = ks_ref.shape[0] // SEL_GROUP
    n_full = lax.shift_right_logical(t0, int(math.log2(SEL_GROUP)))

    def qk_group(j):
        return _dot_nt(ks_ref[pl.ds(pl.multiple_of(j * SEL_GROUP, SEL_GROUP), SEL_GROUP), :], qs)

    sc = _dot_nt(kc_ref[...], qs)

    w_slabs, w_chunks = [], []
    for i in range(nw):
        jj = c - (nw - 1) + i
        jc = jnp.maximum(jj, 0)
        si = _dot_nt(kw_ref[pl.ds(pl.multiple_of(jc * KC, KC), KC), :], qs)
        if i == 0:
            keep = (r_kc > u_row) & (jj >= 0)
        elif i == nw - 1:
            keep = r_kc <= u_row
        else:
            keep = jj >= 0
        w_slabs.append(jnp.where(keep, si, NEG))
        w_chunks.append(jc)

    n_col = lax.broadcasted_iota(jnp.int32, (ncmp, 1), 0)
    cmask = (CMP_STRIDE * n_col + (CMP_LEN - 1) <= t_row) & (n_col < ncmp - 1)
    s = jnp.where(cmask, sc, NEG)
    m = jnp.max(s, axis=0, keepdims=True)
    e = jnp.exp2(s - m)
    anyv = (t_row >= CMP_LEN - 1).astype(F32)
    p = e * (anyv / jnp.sum(e, axis=0, keepdims=True))
    o_cmp = pv(vct_ref[...], p)

    psums = []
    for pg in per_group(p):
        acc_p = pg[:, 0:QT]
        for h in range(1, NSA_HPG):
            acc_p = acc_p + pg[:, h * QT:(h + 1) * QT]
        psums.append(acc_p)
    psum = jnp.concatenate(psums, axis=1)
    nq2 = NSA_KV * QT
    s_col = lax.broadcasted_iota(jnp.int32, (nsel, 1), 0)
    n_lane = lax.broadcasted_iota(jnp.int32, (1, ncmp), 1)
    ov = ((CMP_STRIDE * n_lane < SEL_BLOCK * (s_col + 1)) & (CMP_STRIDE * n_lane + (CMP_LEN - 1) >= SEL_BLOCK * s_col)
          ).astype(BF16)
    imp = sum(_dot(ov, part) for part in _split3(psum))

    s_diag = qk_group(n_full)
    sx_scr[...] = qk_group(0)

    mxw = w_slabs[0]
    for sl in w_slabs[1:]:
        mxw = jnp.maximum(mxw, sl)
    mw = jnp.max(mxw, axis=0, keepdims=True)
    acc_w = jnp.zeros((VT_ROWS, width), F32)
    for sl, jc in zip(w_slabs, w_chunks):
        acc_w = acc_w + pv(vwt_ref[jc], jnp.exp2(sl - mw))
    o_win = normalised(acc_w)

    t1 = t0 + lax.broadcasted_iota(jnp.int32, (1, nq2), 1) % QT
    cur = lax.shift_right_logical(t1, 6)
    forced = (s_col == 0) | (s_col == cur) | (s_col == cur - 1)
    valid = SEL_BLOCK * s_col <= t1
    val = jnp.where(valid, jnp.where(forced, imp + SEL_FORCE, imp), NEG)
    sub = 8
    r_sub = lax.broadcasted_iota(jnp.int32, (sub, 1), 0)
    blocks = [val[r * sub:(r + 1) * sub, :] for r in range(nsel // sub)]
    ranks = [jnp.zeros((sub, nq2), F32) for _ in blocks]
    for i in range(nsel):
        vi = val[i:i + 1, :]
        for r, blk in enumerate(blocks):
            if i < r * sub:
                beats = vi >= blk
            elif i >= (r + 1) * sub:
                beats = vi > blk
            else:
                beats = (vi > blk) | ((vi == blk) & (r_sub > i - r * sub))
            ranks[r] = ranks[r] + jnp.where(beats, 1.0, 0.0)
    bias = jnp.where(jnp.concatenate(ranks, axis=0) < SEL_TOPK, 0.0, NEG).astype(F32)
    bias_scr[...] = jnp.concatenate([bias[:, g * QT:(g + 1) * QT] for g in range(NSA_KV) for _ in range(NSA_HPG)],
                                    axis=1)

    r_blk = lax.broadcasted_iota(jnp.int32, (SEL_BLOCK, 1), 0)
    blk_per_grp = SEL_GROUP // SEL_BLOCK
    chunk_per_grp = SEL_GROUP // KC
    blk_per_chunk = KC // SEL_BLOCK

    def sel_update(j, sj, carry, causal=False, live=None):
        m_o, acc = carry
        brows, keeps = [], []
        for i in range(blk_per_grp):
            brow = bias_scr[pl.ds(blk_per_grp * j + i, 1), :]
            brows.append(brow if live is None else jnp.where(live, brow, NEG))
            keeps.append(j * SEL_GROUP + i * SEL_BLOCK + r_blk <= t_row if causal else None)

        def block(i):
            return sj[i * SEL_BLOCK:(i + 1) * SEL_BLOCK, :]

        mx = None
        for i in range(blk_per_grp):
            sl = block(i) + brows[i]
            if causal:
                sl = jnp.where(keeps[i], sl, NEG)
            mx = sl if mx is None else jnp.maximum(mx, sl)
        m_n = jnp.maximum(m_o, jnp.max(mx, axis=0, keepdims=True))
        a = jnp.exp2(m_o - m_n)
        acc = a * acc
        for ci in range(chunk_per_grp):
            parts = []
            for i in range(blk_per_chunk * ci, blk_per_chunk * (ci + 1)):
                arg = block(i) + (brows[i] - m_n)
                parts.append(jnp.exp2(jnp.where(keeps[i], arg, NEG) if causal else arg))
            acc = acc + pv(vst_ref[chunk_per_grp * j + ci], jnp.concatenate(parts, axis=0))
        return m_n, acc

    empty = (jnp.full((1, width), NEG, F32), jnp.zeros((VT_ROWS, width), F32))
    seeded = sel_update(n_full, s_diag, empty, causal=True)

    def pair_body(jp, carry):
        ja, jb = 2 * jp, 2 * jp + 1
        sy_scr[...] = qk_group(jb)
        carry = sel_update(ja, sx_scr, carry)
        sx_scr[...] = qk_group(jnp.minimum(ja + 2, n_grp - 1))
        return sel_update(jb, sy_scr, carry)

    n_pairs = lax.shift_right_logical(n_full, 1)
    carry = lax.fori_loop(0, n_pairs, pair_body, seeded)
    _, acc_s = lax.cond(n_full - 2 * n_pairs == 1, lambda cr: sel_update(n_full - 1, sx_scr, cr), lambda cr: cr,
                        carry)
    o_sel = normalised(acc_s)

    def gate_row(br):
        rows = [smt_ref[SM_GATE + 3 * h + br:SM_GATE + 3 * h + br + 1, :] for h in range(NSA_HEADS)]
        return _sigmoid(jnp.concatenate(rows, axis=1))

    o_t = gate_row(0) * o_cmp + gate_row(1) * o_sel + gate_row(2) * o_win
    for pr in range(NSA_HEADS // 2):
        xp = jnp.concatenate([o_t[:, (2 * pr) * QT:(2 * pr + 1) * QT], o_t[:, (2 * pr + 1) * QT:(2 * pr + 2) * QT]], axis=0)
        o_ref[:, pr * LANES:(pr + 1) * LANES] = xp.T.astype(BF16)


def _nsa(qpad, kcb, vct, ks, kw, vst, vwt, smt, b, s):
    nq = s // QT
    ncmp = kcb.shape[1]
    return pl.pallas_call(
        _nsa_kernel,
        out_shape=jax.ShapeDtypeStruct((b * s, NSA_HEADS * NSA_HD), BF16),
        grid=(b, nq),
        in_specs=[
            pl.BlockSpec((QT, NSA_HEADS * NSA_HD), lambda bi, c: (bi * nq + c, 0)),
            pl.BlockSpec((None, ncmp, LANES), lambda bi, c: (bi, 0, 0)),
            pl.BlockSpec((None, NSA_KV * NSA_HD, ncmp), lambda bi, c: (bi, 0, 0)),
            pl.BlockSpec((s, LANES), lambda bi, c: (bi, 0)),
            pl.BlockSpec((s, LANES), lambda bi, c: (bi, 0)),
            pl.BlockSpec((s // KC, NSA_KV * VT_ROWS, KC), lambda bi, c: (bi, 0, 0)),
            pl.BlockSpec((s // KC, NSA_KV * VT_ROWS, KC), lambda bi, c: (bi, 0, 0)),
            pl.BlockSpec((LANES, QT), lambda bi, c: (0, bi * nq + c)),
        ],
        out_specs=pl.BlockSpec((QT, NSA_HEADS * NSA_HD), lambda bi, c: (bi * nq + c, 0)),
        scratch_shapes=[pltpu.VMEM((s // SEL_BLOCK, NSA_HEADS * QT), F32),
                        pltpu.VMEM((SEL_GROUP, NSA_HEADS * QT), F32),
                        pltpu.VMEM((SEL_GROUP, NSA_HEADS * QT), F32)],
        compiler_params=pltpu.CompilerParams(dimension_semantics=("arbitrary", "arbitrary"),
                                             vmem_limit_bytes=VMEM_LIMIT),
        name="nsa",
    )(qpad, kcb, vct, ks, kw, vst, vwt, smt)


def _log_sigmoid(x):
    return jnp.minimum(x, 0.0) - jnp.log(1.0 + jnp.exp(-jnp.abs(x)))


def _mlstm_kernel(qk_ref, v_ref, og_ref, sm_ref, smt_ref, wc_ref, bc_ref, bifc_ref, bifr_ref, ghn_ref,
                  y_ref, ext_scr, ct_scr, n_scr, m_scr):
    lc = ML_CHUNK

    @pl.when(pl.program_id(1) == 0)
    def _():
        ext_scr[0:8, :] = jnp.zeros((8, 2 * ML_WIDTH), F32)
        ct_scr[...] = jnp.zeros_like(ct_scr)
        n_scr[...] = jnp.zeros_like(n_scr)
        m_scr[...] = jnp.zeros_like(m_scr)

    ext_scr[8:8 + ML_BLOCK, :] = qk_ref[...].astype(F32)
    y = bc_ref[...]
    for j in range(CONV_W):
        y = y + wc_ref[j:j + 1, :] * ext_scr[pl.ds(8 - (CONV_W - 1) + j, ML_BLOCK), :]
    ext_scr[0:8, :] = ext_scr[ML_BLOCK:ML_BLOCK + 8, :]
    qkc = y * _sigmoid(y)
    q_all = qkc[:, 0:ML_WIDTH].astype(BF16)
    k_all = (qkc[:, ML_WIDTH:2 * ML_WIDTH] * (ML_HD ** -0.5)).astype(BF16)

    ifc = sm_ref[...] + bifc_ref[...]
    ifr = smt_ref[...] + bifr_ref[...]
    lfc = _log_sigmoid(ifc)
    lfr = _log_sigmoid(ifr)
    rr = lax.broadcasted_iota(jnp.int32, (lc, lc), 0)
    cc = lax.broadcasted_iota(jnp.int32, (lc, lc), 1)
    causal = rr >= cc
    tri_l = causal.astype(F32)
    tri_u = (rr <= cc).astype(F32)

    for ci in range(ML_BLOCK // lc):
        lo, hi = ci * lc, (ci + 1) * lc
        bc_all = jnp.dot(tri_l, lfc[lo:hi, :], preferred_element_type=F32, precision=lax.Precision.HIGHEST)
        br_all = jnp.dot(lfr[:, lo:hi], tri_u, preferred_element_type=F32, precision=lax.Precision.HIGHEST)
        heads = range(ML_HEADS)
        hsl = [slice(h * ML_HD, (h + 1) * ML_HD) for h in heads]
        bcol = [bc_all[:, SM_F + h:SM_F + h + 1] for h in heads]
        brow = [br_all[SM_F + h:SM_F + h + 1, :] for h in heads]
        icol = [ifc[lo:hi, SM_I + h:SM_I + h + 1] for h in heads]
        irow = [ifr[SM_I + h:SM_I + h + 1, lo:hi] for h in heads]
        mprev = [m_scr[h][:, 0:1] for h in heads]
        qh = [q_all[lo:hi, hsl[h]] for h in heads]
        kh = [k_all[lo:hi, hsl[h]] for h in heads]
        vh = [v_ref[lo:hi, hsl[h]] for h in heads]
        ct = [ct_scr[h] for h in heads]
        nrow = [n_scr[h] for h in heads]
        qk = [_dot_nt(qh[h], kh[h]) for h in heads]
        qc = [_dot(qh[h], ct[h].astype(BF16)) for h in heads]
        dmat = [jnp.where(causal, bcol[h] - brow[h] + irow[h], NEG) for h in heads]
        inter = [bcol[h] + mprev[h] for h in heads]
        mt = [jnp.maximum(jnp.max(dmat[h], axis=-1, keepdims=True), inter[h]) for h in heads]
        a = [jnp.exp(dmat[h] - mt[h]) * qk[h] for h in heads]
        dec = [jnp.exp(inter[h] - mt[h]) for h in heads]
        num = [_dot(a[h].astype(BF16), vh[h]) + dec[h] * qc[h] for h in heads]
        den = [jnp.sum(a[h], axis=-1, keepdims=True)
               + dec[h] * jnp.sum(qh[h].astype(F32) * nrow[h], axis=-1, keepdims=True) for h in heads]
        blast = [bcol[h][lc - 1:lc, :] for h in heads]
        mnew = [jnp.maximum(blast[h] + mprev[h], jnp.max(blast[h] - brow[h] + irow[h], axis=-1, keepdims=True))
                for h in heads]
        wprev = [jnp.exp(blast[h] + mprev[h] - mnew[h]) for h in heads]
        kwt = [kh[h].astype(F32) * jnp.exp(blast[h] - bcol[h] + icol[h] - mnew[h]) for h in heads]
        for h in heads:
            ct_scr[h] = wprev[h] * ct[h] + lax.dot_general(kwt[h].astype(BF16), vh[h], _TN,
                                                           preferred_element_type=F32)
            n_scr[h] = wprev[h] * nrow[h] + jnp.sum(kwt[h], axis=0, keepdims=True)
            m_scr[h] = jnp.broadcast_to(mnew[h], (1, LANES))
        hm = [num[h] / jnp.maximum(jnp.abs(den[h]), jnp.exp(-mt[h])) * _sigmoid(og_ref[lo:hi, hsl[h]].astype(F32))
              for h in heads]
        for h in heads:
            y_ref[lo:hi, hsl[h]] = _rms(hm[h], ghn_ref[:, hsl[h]]).astype(BF16)


def _mlstm(qkb, vb, ob, sm, smt, wconv, bconv, bifc, bifr, ghn, b, s):
    nb = s // ML_BLOCK
    row = lambda w: pl.BlockSpec((ML_BLOCK, w), lambda bi, j: (bi * nb + j, 0))
    return pl.pallas_call(
        _mlstm_kernel,
        out_shape=jax.ShapeDtypeStruct((b * s, ML_WIDTH), BF16),
        grid=(b, nb),
        in_specs=[row(2 * ML_WIDTH), row(ML_WIDTH), row(ML_WIDTH), row(LANES),
                  pl.BlockSpec((LANES, ML_BLOCK), lambda bi, j: (0, bi * nb + j)),
                  _const_spec(wconv.shape), _const_spec(bconv.shape), _const_spec(bifc.shape),
                  _const_spec(bifr.shape), _const_spec(ghn.shape)],
        out_specs=row(ML_WIDTH),
        scratch_shapes=[pltpu.VMEM((ML_BLOCK + 8, 2 * ML_WIDTH), F32),
                        pltpu.VMEM((ML_HEADS, ML_HD, ML_HD), F32),
                        pltpu.VMEM((ML_HEADS, 1, ML_HD), F32),
                        pltpu.VMEM((ML_HEADS, 1, LANES), F32)],
        compiler_params=pltpu.CompilerParams(dimension_semantics=("arbitrary", "arbitrary"),
                                             vmem_limit_bytes=VMEM_LIMIT),
        name="mlstm",
    )(qkb, vb, ob, sm, smt, wconv, bconv, bifc, bifr, ghn)


RT_BUCKET = N_EXPERTS
RT_RANK = N_EXPERTS + 1
RT_WLO = N_EXPERTS + 2
RT_WHI = N_EXPERTS + 3
N_BUCKETS = N_GROUPS * 6
X_ROWS = D_MODEL // LANES
REC = 2 * X_ROWS
DMA_UNROLL = 8


def _merge_kernel(ya_ref, yb_ref, mg_ref, x_ref, wpa_ref, wpb_ref, wout_ref, gffn_ref, wr_ref, br_ref,
                  x1_ref, h2_ref, slab_ref, hist_ref):
    pa = _dot(ya_ref[...], wpa_ref[...])
    pb = _dot(yb_ref[...], wpb_ref[...])
    ga = _sigmoid(mg_ref[:, 0:D_MODEL].astype(F32))
    gb = _sigmoid(mg_ref[:, D_MODEL:2 * D_MODEL].astype(F32))
    mixed = (ga * pa + gb * pb).astype(BF16)
    x1 = x_ref[...] + _dot(mixed, wout_ref[...])
    x1_ref[...] = x1
    h2 = _rms(x1, gffn_ref[...])
    h_hi = h2.astype(BF16)
    h2_ref[...] = h2

    h_lo = (h2 - h_hi.astype(F32)).astype(BF16)
    r_hi = _dot(h_hi, wr_ref[...])
    logit = r_hi[:, 0:LANES] + r_hi[:, LANES:2 * LANES] + _dot(h_lo, wr_ref[:, 0:LANES]) + br_ref[...]
    lane = lax.broadcasted_iota(jnp.int32, logit.shape, 1)
    big = jnp.int32(LANES)
    gmask = (lane >= N_EXPERTS) & (lane < N_EXPERTS + N_GROUPS)
    gl = jnp.where(gmask, logit, NEG)
    gmax = jnp.max(gl, axis=-1, keepdims=True)
    gidx = jnp.min(jnp.where(gmask & (gl == gmax), lane, big), axis=-1, keepdims=True) - N_EXPERTS
    pg_sel = 1.0 / jnp.sum(jnp.where(gmask, jnp.exp(gl - gmax), 0.0), axis=-1, keepdims=True)
    emask = (lane < N_EXPERTS) & (lax.shift_right_logical(lane, 2) == gidx)
    el = jnp.where(emask, logit, NEG)
    e1 = jnp.max(el, axis=-1, keepdims=True)
    i1 = jnp.min(jnp.where(emask & (el == e1), lane, big), axis=-1, keepdims=True)
    emask2 = emask & (lane != i1)
    el2 = jnp.where(emask2, logit, NEG)
    e2 = jnp.max(el2, axis=-1, keepdims=True)
    i2 = jnp.min(jnp.where(emask2 & (el2 == e2), lane, big), axis=-1, keepdims=True)
    x21 = jnp.exp(e2 - e1)
    w1 = pg_sel / (1.0 + x21)
    w2 = pg_sel * x21 / (1.0 + x21)
    first_lo = i1 < i2
    e_lo = jnp.where(first_lo, i1, i2) - EXP_PER_GROUP * gidx
    e_hi = jnp.where(first_lo, i2, i1) - EXP_PER_GROUP * gidx
    pair = lax.shift_right_logical(e_lo * (2 * EXP_PER_GROUP - 1 - e_lo), 1) + (e_hi - e_lo - 1)
    bucket = 6 * gidx + pair
    member = lane == bucket
    onehot = jnp.where(member, 1.0, 0.0)
    rr = lax.broadcasted_iota(jnp.int32, (TD, TD), 0)
    cc = lax.broadcasted_iota(jnp.int32, (TD, TD), 1)
    earlier = _dot((rr > cc).astype(BF16), onehot.astype(BF16))
    rank = jnp.sum(jnp.where(member, earlier, 0.0), axis=-1, keepdims=True)
    slab = jnp.where(lane == i1, w1, 0.0) + jnp.where(lane == i2, w2, 0.0)
    slab = jnp.where(lane == RT_BUCKET, bucket.astype(F32), slab)
    slab = jnp.where(lane == RT_RANK, rank, slab)
    slab = jnp.where(lane == RT_WLO, jnp.where(first_lo, w1, w2), slab)
    slab = jnp.where(lane == RT_WHI, jnp.where(first_lo, w2, w1), slab)
    slab_ref[...] = slab
    hist_ref[...] = jnp.broadcast_to(jnp.sum(onehot, axis=0, keepdims=True), (8, LANES))


def _merge(ya, yb, mg, x2, wpa, wpb, wout, gffn, wr, br):
    t = x2.shape[0]
    row = lambda w: pl.BlockSpec((TD, w), lambda i: (i, 0))
    return pl.pallas_call(
        _merge_kernel,
        out_shape=[jax.ShapeDtypeStruct((t, D_MODEL), F32),
                   jax.ShapeDtypeStruct((t, D_MODEL), F32),
                   jax.ShapeDtypeStruct((t, LANES), F32),
                   jax.ShapeDtypeStruct((t // TD * 8, LANES), F32)],
        grid=(t // TD,),
        in_specs=[row(NSA_HEADS * NSA_HD), row(ML_WIDTH), row(2 * D_MODEL), row(D_MODEL)]
                 + [_const_spec(a.shape) for a in (wpa, wpb, wout, gffn, wr, br)],
        out_specs=[row(D_MODEL), row(D_MODEL), row(LANES),
                   pl.BlockSpec((8, LANES), lambda i: (i, 0))],
        compiler_params=pltpu.CompilerParams(dimension_semantics=("arbitrary",), vmem_limit_bytes=VMEM_LIMIT),
        name="merge",
    )(ya, yb, mg, x2, wpa, wpb, wout, gffn, wr, br)


def _rec_copy(src_ref, src_tok, dst_ref, dst_tok, sem, rows):
    src = src_ref.at[pl.ds(pl.multiple_of(src_tok * rows, rows), rows), :]
    dst = dst_ref.at[pl.ds(pl.multiple_of(dst_tok * rows, rows), rows), :]
    return pltpu.make_async_copy(src, dst, sem)


def _token_copies(n, make, wait=False):
    def body(g, carry):
        for u in range(DMA_UNROLL):
            cp = make(g * DMA_UNROLL + u)
            if wait:
                cp.wait()
            else:
                cp.start(priority=u % 2)
        return carry
    lax.fori_loop(0, n // DMA_UNROLL, body, 0)


def _dispatch_kernel(pos_ref, tail_ref, h2_ref, slab_ref, out_ref, stage_scr, zero_scr, sem, zsem):
    i = pl.program_id(0)
    slot = lax.rem(i, 2)
    n_tiles = out_ref.shape[0] // (TM * REC)
    n_used = tail_ref[2 * N_BUCKETS]

    def zero_copy(first_slot):
        start = pl.multiple_of(first_slot * REC, TM * REC)
        return pltpu.make_async_copy(zero_scr, out_ref.at[pl.ds(start, TM * REC), :], zsem)

    @pl.when(i == 0)
    def _():
        zero_scr[...] = jnp.zeros_like(zero_scr)
        for phase in ("start", "wait"):
            for b in range(N_BUCKETS):
                @pl.when(tail_ref[N_BUCKETS + b] > 0)
                def _():
                    getattr(zero_copy(tail_ref[b]), phase)()

                @pl.when(n_used + b < n_tiles)
                def _():
                    getattr(zero_copy((n_used + b) * TM), phase)()

    stage = stage_scr.at[slot]
    for j in range(X_ROWS):
        stage[pl.ds(j, TM, stride=REC), :] = h2_ref[:, j * LANES:(j + 1) * LANES]
    stage[pl.ds(X_ROWS, TM, stride=REC), :] = slab_ref[...]
    for j in range(X_ROWS + 1, REC):
        stage[pl.ds(j, TM, stride=REC), :] = jnp.zeros((TM, LANES), F32)
    base = i * TM
    _token_copies(TM, lambda r: _rec_copy(stage_scr.at[slot], r, out_ref, pos_ref[base + r], sem.at[slot], REC))

    def drain(which):
        _token_copies(TM, lambda r: _rec_copy(stage_scr.at[which], 0, out_ref, 0, sem.at[which], REC), wait=True)

    @pl.when(i > 0)
    def _():
        drain(1 - slot)

    @pl.when(i == pl.num_programs(0) - 1)
    def _():
        drain(slot)


def _dispatch(pos, tail, h2, slab, n_slots):
    t = h2.shape[0]
    return pl.pallas_call(
        _dispatch_kernel,
        out_shape=jax.ShapeDtypeStruct((n_slots * REC, LANES), F32),
        grid_spec=pltpu.PrefetchScalarGridSpec(
            num_scalar_prefetch=2,
            grid=(t // TM,),
            in_specs=[pl.BlockSpec((TM, D_MODEL), lambda i, pos_r, tail_r: (i, 0)),
                      pl.BlockSpec((TM, LANES), lambda i, pos_r, tail_r: (i, 0))],
            out_specs=pl.BlockSpec(memory_space=pl.ANY),
            scratch_shapes=[pltpu.VMEM((2, TM * REC, LANES), F32), pltpu.VMEM((TM * REC, LANES), F32),
                            pltpu.SemaphoreType.DMA((2,)), pltpu.SemaphoreType.DMA(())],
        ),
        compiler_params=pltpu.CompilerParams(dimension_semantics=("arbitrary",), vmem_limit_bytes=VMEM_LIMIT,
                                             has_side_effects=True),
        name="dispatch",
    )(pos, tail, h2, slab)


def _moe_kernel(te_ref, nu_ref, hx_ref, w13_ref, w2_ref, y_ref):
    k = pl.program_id(0)
    n_tiles = pl.num_programs(0)
    n_used = nu_ref[0]

    @pl.when(k < n_used)
    def _():
        h = jnp.concatenate([hx_ref[pl.ds(j, TM, stride=REC), :] for j in range(X_ROWS)], axis=1).astype(BF16)
        slab = hx_ref[pl.ds(X_ROWS, TM, stride=REC), :]
        y = None
        for side, lane in ((0, RT_WLO), (1, RT_WHI)):
            e = te_ref[side * n_tiles + k]
            a = _dot(h, w13_ref[e])
            gt = a[:, 0:D_EXPERT]
            act = gt * _sigmoid(gt) * a[:, D_EXPERT:2 * D_EXPERT] * slab[:, lane:lane + 1]
            part = _dot(act.astype(BF16), w2_ref[e])
            y = part if y is None else y + part
        for j in range(X_ROWS):
            y_ref[pl.ds(j, TM, stride=X_ROWS), :] = y[:, j * LANES:(j + 1) * LANES]

    @pl.when(k >= n_used)
    def _():
        y_ref[...] = jnp.zeros_like(y_ref)


def _moe(tile_e, n_used, hx_sorted, w13, w2):
    n_tiles = hx_sorted.shape[0] // (TM * REC)
    return pl.pallas_call(
        _moe_kernel,
        out_shape=jax.ShapeDtypeStruct((n_tiles * TM * X_ROWS, LANES), F32),
        grid_spec=pltpu.PrefetchScalarGridSpec(
            num_scalar_prefetch=2,
            grid=(n_tiles,),
            in_specs=[pl.BlockSpec((TM * REC, LANES), lambda k, te, nu: (jnp.minimum(k, nu[0] - 1), 0)),
                      pl.BlockSpec(w13.shape, lambda k, te, nu: (0, 0, 0), pipeline_mode=pl.Buffered(1)),
                      pl.BlockSpec(w2.shape, lambda k, te, nu: (0, 0, 0), pipeline_mode=pl.Buffered(1))],
            out_specs=pl.BlockSpec((TM * X_ROWS, LANES), lambda k, te, nu: (k, 0)),
        ),
        compiler_params=pltpu.CompilerParams(dimension_semantics=("arbitrary",), vmem_limit_bytes=VMEM_LIMIT),
        name="moe",
    )(tile_e, n_used, hx_sorted, w13, w2)


def _combine_kernel(pos_ref, y_ref, x1_ref, p_ref, gple_ref, wpg_ref, wpp_ref, gfin_ref, o_ref, ybuf, sem):
    i = pl.program_id(0)
    slot = lax.rem(i, 2)

    def gather(tile, which):
        _token_copies(TD, lambda r: _rec_copy(y_ref, pos_ref[tile * TD + r], ybuf.at[which], r, sem.at[which],
                                              X_ROWS))

    @pl.when(i == 0)
    def _():
        gather(0, 0)

    @pl.when(i + 1 < pl.num_programs(0))
    def _():
        gather(i + 1, 1 - slot)

    _token_copies(TD, lambda r: _rec_copy(y_ref, 0, ybuf.at[slot], 0, sem.at[slot], X_ROWS), wait=True)
    yb = ybuf.at[slot]
    x2 = x1_ref[...] + jnp.concatenate([yb[pl.ds(j, TD, stride=X_ROWS), :] for j in range(X_ROWS)], axis=1)
    h3 = _rms(x2, gple_ref[...]).astype(BF16)
    x3 = x2 + _sigmoid(_dot(h3, wpg_ref[...])) * _dot(p_ref[...].astype(BF16), wpp_ref[...])
    o_ref[...] = _rms(x3, gfin_ref[...])


def _combine(pos, y_sorted, x1, p2, gple, wpg, wpp, gfin):
    t = x1.shape[0]
    row = lambda w: pl.BlockSpec((TD, w), lambda i, pos_r: (i, 0))
    const = lambda a: pl.BlockSpec(a.shape, lambda i, pos_r: (0,) * a.ndim, pipeline_mode=pl.Buffered(1))
    return pl.pallas_call(
        _combine_kernel,
        out_shape=jax.ShapeDtypeStruct((t, D_MODEL), F32),
        grid_spec=pltpu.PrefetchScalarGridSpec(
            num_scalar_prefetch=1,
            grid=(t // TD,),
            in_specs=[pl.BlockSpec(memory_space=pl.ANY), row(D_MODEL), row(PLE_DIM),
                      const(gple), const(wpg), const(wpp), const(gfin)],
            out_specs=row(D_MODEL),
            scratch_shapes=[pltpu.VMEM((2, TD * X_ROWS, LANES), F32), pltpu.SemaphoreType.DMA((2,))],
        ),
        compiler_params=pltpu.CompilerParams(dimension_semantics=("arbitrary",), vmem_limit_bytes=VMEM_LIMIT),
        name="combine",
    )(pos, y_sorted, x1, p2, gple, wpg, wpp, gfin)


def _routing_tables(slab, hist8):
    t = slab.shape[0]
    nt = t // TD
    n_tiles = t // TM + N_BUCKETS
    hist = hist8.reshape(nt, 8, LANES)[:, 0, :]
    counts = jnp.sum(hist, axis=0)
    padded = jnp.ceil(counts / TM) * TM
    ends = jnp.cumsum(padded)
    first = (ends - padded)[None, :] + jnp.cumsum(hist, axis=0) - hist
    lane = jnp.arange(LANES, dtype=F32)[None, :]
    mine = lane == slab[:, RT_BUCKET:RT_BUCKET + 1]
    pos = jnp.sum(jnp.where(mine, jnp.repeat(first, TD, axis=0), 0.0), axis=1) + slab[:, RT_RANK]
    starts = jnp.arange(n_tiles, dtype=F32) * TM
    tile_bucket = jnp.minimum(jnp.sum(ends[None, :N_BUCKETS] <= starts[:, None], axis=1), N_BUCKETS - 1)
    group, pair = tile_bucket // 6, tile_bucket % 6
    e_lo = EXP_PER_GROUP * group + jnp.array([0, 0, 0, 1, 1, 2], jnp.int32)[pair]
    e_hi = EXP_PER_GROUP * group + jnp.array([1, 2, 3, 2, 3, 3], jnp.int32)[pair]
    tile_e = jnp.concatenate([e_lo, e_hi]).astype(jnp.int32)
    n_used = (ends[N_BUCKETS - 1] / TM).astype(jnp.int32).reshape(1)
    tail = jnp.concatenate([(ends - TM)[:N_BUCKETS], padded[:N_BUCKETS], n_used.astype(F32)]).astype(jnp.int32)
    return pos.astype(jnp.int32), tile_e, n_used, tail, n_tiles * TM


def _pack_inproj_weights(w):
    d = w.shape[0]
    qw = NSA_HEADS * NSA_HD
    kvw = NSA_KV * NSA_HD
    o = 0
    wq = w[:, o:o + qw]; o += qw
    wkc = w[:, o:o + kvw]; o += kvw
    wvc = w[:, o:o + kvw]; o += kvw
    wks = w[:, o:o + kvw]; o += kvw
    wvs = w[:, o:o + kvw]; o += kvw
    wkw = w[:, o:o + kvw]; o += kvw
    wvw = w[:, o:o + kvw]; o += kvw
    wga = w[:, o:o + 3 * NSA_HEADS]; o += 3 * NSA_HEADS
    wqkb = w[:, o:o + 2 * ML_WIDTH]; o += 2 * ML_WIDTH
    wvb = w[:, o:o + ML_WIDTH]; o += ML_WIDTH
    wob = w[:, o:o + ML_WIDTH]; o += ML_WIDTH
    wif = w[:, o:o + 2 * ML_HEADS]; o += 2 * ML_HEADS
    wmg = w[:, o:o + 2 * D_MODEL]
    wsm = jnp.concatenate([wga, wif, jnp.zeros((d, LANES - 3 * NSA_HEADS - 2 * ML_HEADS), w.dtype)], axis=1)
    wcat = jnp.concatenate([wq, wkc, wvc, wks, wkw, wsm, wqkb, wvb, wob, wmg], axis=1).astype(BF16)
    wtr = jnp.concatenate([wvs, wvw, wsm], axis=1).T.astype(BF16)
    return wcat, wtr


def _rope_lane_tables(positions):
    inv = ROPE_THETA ** (-jnp.arange(0, ROPE_DIM, 2, dtype=F32) / ROPE_DIM)
    ang = positions.astype(F32).reshape(-1, 1) * inv[None, :]
    cos, sin = jnp.cos(ang), jnp.sin(ang)
    half = ROPE_DIM // 2
    d = jnp.arange(LANES) % NSA_HD
    cos_l, sin_l = jnp.tile(cos, (1, LANES // half)), jnp.tile(sin, (1, LANES // half))
    rc = jnp.where(d < ROPE_DIM, cos_l, 1.0)
    rp = jnp.where((d >= half) & (d < ROPE_DIM), sin_l, 0.0)
    rm = jnp.where(d < half, -sin_l, 0.0)
    return rc, rp, rm


def _pack_compress_weights(w1, w2, pe):
    half = CMP_LEN // 2
    w1r = w1.reshape(2, half, NSA_HD, CMP_HIDDEN)
    outs = []
    for part in range(2):
        wb = w1r[part].astype(BF16)
        zb = jnp.zeros_like(wb)
        wp = jnp.stack([jnp.stack([wb, zb], axis=2), jnp.stack([zb, wb], axis=2)], axis=1)
        outs.append(wp.reshape(half * NSA_KV * NSA_HD, NSA_KV * CMP_HIDDEN))
    pe8 = jnp.broadcast_to(pe.reshape(1, CMP_LEN * NSA_HD), (8, CMP_LEN * NSA_HD)).astype(BF16)
    return outs[0], outs[1], pe8, w1.astype(BF16)


def _stages(x, p, positions, g_mix, w_in, b_if, w_ck1, w_ck2, pe_ck, w_cv1, w_cv2, pe_cv, w_conv, b_conv, g_hn, w_pa, w_pb, w_out, g_ffn, w_rg, b_rg, w_re, b_re, w_e13, w_e2, g_ple, w_pg, w_pp, g_final):
    b, s, d = x.shape
    t = b * s
    rc, rp, rm = _rope_lane_tables(positions)
    assert w_in.shape[0] == 1, "the final norm is fused into the layer's last kernel: single-layer problem only"
    for i in range(w_in.shape[0]):
        x2 = x.reshape(t, d)
        wcat, wtr = _pack_inproj_weights(w_in[i])
        (qpad, kc_tok, vc_tok, ks, kw, vst, vwt, sm, smt, qkb, vb, ob, mg) = _inproj(
            x2, g_mix[i].reshape(1, d), wcat, wtr, rc, rp, rm)
        wka, wkb, pek, w1k = _pack_compress_weights(w_ck1[i], w_ck2[i], pe_ck[i])
        wva, wvb, pev, w1v = _pack_compress_weights(w_cv1[i], w_cv2[i], pe_cv[i])
        zpad = jnp.zeros((CMP_HIDDEN, NSA_HD), F32)
        w2k = jnp.stack([jnp.concatenate([w_ck2[i], zpad], axis=1),
                         jnp.concatenate([zpad, w_ck2[i]], axis=1)]).astype(BF16)
        w2vt = w_cv2[i].T.astype(BF16)
        nrow = s // CMP_STRIDE
        rk = kc_tok.reshape(b, nrow, CMP_STRIDE * LANES)
        rv = vc_tok.reshape(b, nrow, CMP_STRIDE * LANES)
        kcb, vct = _compress(rk, rv, wka, wkb, wva, wvb, pek, pev, w1k, w1v, w2k, w2vt)
        ya = _nsa(qpad, kcb, vct, ks, kw, vst, vwt, smt, b, s)
        bif = b_if[i].astype(F32)
        bifc = jnp.zeros((1, LANES), F32).at[0, SM_I:SM_I + 2 * ML_HEADS].set(bif)
        bifr = bifc.reshape(LANES, 1)
        yb = _mlstm(qkb, vb, ob, sm, smt, w_conv[i], b_conv[i].reshape(1, -1), bifc, bifr,
                    g_hn[i].reshape(1, -1), b, s)
        wr = jnp.concatenate([w_re[i], w_rg[i], jnp.zeros((d, LANES - N_EXPERTS - N_GROUPS), F32)], axis=1)
        wr_hi = wr.astype(BF16)
        wr = jnp.concatenate([wr_hi, (wr - wr_hi.astype(F32)).astype(BF16)], axis=1)
        br =jnp.concatenate([b_re[i], b_rg[i], jnp.zeros((LANES - N_EXPERTS - N_GROUPS,), F32)]).reshape(1, LANES)
        x1, h2, slab, hist8 = _merge(ya, yb, mg, x2, w_pa[i].astype(BF16), w_pb[i].astype(BF16),
                                     w_out[i].astype(BF16), g_ffn[i].reshape(1, d), wr, br)
        pos, tile_e, n_used, tail, n_slots = _routing_tables(slab, hist8)
        hx_sorted = _dispatch(pos, tail, h2, slab, n_slots)
        y_sorted = _moe(tile_e, n_used, hx_sorted, w_e13[i].astype(BF16), w_e2[i].astype(BF16))
        out = _combine(pos, y_sorted, x1, p[i].reshape(t, PLE_DIM), g_ple[i].reshape(1, d), w_pg[i].astype(BF16),
                       w_pp[i].astype(BF16), g_final.reshape(1, d))
        x = out.reshape(b, s, d)
    return dict(out=x, qpad=qpad, ks=ks, kcb=kcb, vct=vct, y_a=ya, y_b=yb, x1=x1, pos=pos)


def kernel(x, p, positions, g_mix, w_in, b_if, w_ck1, w_ck2, pe_ck, w_cv1, w_cv2, pe_cv, w_conv, b_conv, g_hn, w_pa, w_pb, w_out, g_ffn, w_rg, b_rg, w_re, b_re, w_e13, w_e2, g_ple, w_pg, w_pp, g_final):
    return _stages(x, p, positions, g_mix, w_in, b_if, w_ck1, w_ck2, pe_ck, w_cv1, w_cv2, pe_cv, w_conv, b_conv, g_hn,
                   w_pa, w_pb, w_out, g_ffn, w_rg, b_rg, w_re, b_re, w_e13, w_e2, g_ple, w_pg, w_pp, g_final)["out"]
```

```python
import functools
import math

import jax
import jax.numpy as jnp
from jax import lax
from jax.experimental import pallas as pl
from jax.experimental.pallas import tpu as pltpu

F32 = jnp.float32
BF16 = jnp.bfloat16

EPS = 1e-6
NEG = -1e30

D_MODEL = 1024
PLE_DIM = 256
NSA_HEADS = 8
NSA_KV = 2
NSA_HPG = NSA_HEADS // NSA_KV
NSA_HD = 64
CMP_LEN = 32
CMP_STRIDE = 16
CMP_HIDDEN = 256
SEL_BLOCK = 64
SEL_TOPK = 16
SEL_FORCE = 1000.0
WINDOW = 512
ROPE_THETA = 500000.0
ROPE_DIM = NSA_HD // 4
ML_HEADS = 4
ML_HD = 128
ML_WIDTH = ML_HEADS * ML_HD
CONV_W = 4
N_GROUPS = 4
EXP_PER_GROUP = 4
N_EXPERTS = N_GROUPS * EXP_PER_GROUP
D_EXPERT = 256

LANES = 128
QT = 128
KC = 128
SEL_GROUP = 512
VT_PAD = 16
VT_ROWS = NSA_HD + VT_PAD
ML_CHUNK = 128
ML_BLOCK = 256
TD = 512
TM = 256
VMEM_LIMIT = 56 * 1024 * 1024

_NT = (((1,), (1,)), ((), ()))
_TN = (((0,), (0,)), ((), ()))

SM_GATE = 0
SM_I = 3 * NSA_HEADS
SM_F = SM_I + ML_HEADS


def _dot(a, b):
    return jnp.dot(a, b, preferred_element_type=F32)


def _dot_nt(a, b):
    return lax.dot_general(a, b, _NT, preferred_element_type=F32)


def _split3(x):
    hi = x.astype(BF16)
    r1 = x - hi.astype(F32)
    mid = r1.astype(BF16)
    lo = (r1 - mid.astype(F32)).astype(BF16)
    return hi, mid, lo


def _rms(x, g):
    return x * lax.rsqrt(jnp.mean(x * x, axis=-1, keepdims=True) + EPS) * g


def _sigmoid(x):
    return 0.5 + 0.5 * jnp.tanh(0.5 * x)


def _const_spec(shape):
    nd = len(shape)
    return pl.BlockSpec(shape, lambda *_: (0,) * nd, pipeline_mode=pl.Buffered(1))


_C_Q = 0
_C_KC = _C_Q + NSA_HEADS * NSA_HD
_C_VC = _C_KC + LANES
_C_KS = _C_VC + LANES
_C_KW = _C_KS + LANES
_C_SM = _C_KW + LANES
_C_QKB = _C_SM + LANES
_C_VB = _C_QKB + 2 * ML_WIDTH
_C_OB = _C_VB + ML_WIDTH
_C_MG = _C_OB + ML_WIDTH
_C_END = _C_MG + 2 * D_MODEL


def _inproj_kernel(x_ref, g_ref, w_ref, wt_ref, rc_ref, rp_ref, rm_ref,
                   q_ref, kc_ref, vc_ref, ks_ref, kw_ref, vst_ref, vwt_ref, sm_ref, smt_ref,
                   qkb_ref, vb_ref, ob_ref, mg_ref):
    hn = _rms(x_ref[...], g_ref[...]).astype(BF16)
    rc, rp, rm = rc_ref[...], rp_ref[...], rm_ref[...]

    def rope(z):
        return z * rc + pltpu.roll(z, 8, 1) * rp + pltpu.roll(z, LANES - 8, 1) * rm

    scale = NSA_HD ** -0.5 * math.log2(math.e)
    for h in range(NSA_HEADS * NSA_HD // LANES):
        z = _dot(hn, w_ref[:, _C_Q + h * LANES:_C_Q + (h + 1) * LANES])
        q_ref[:, h * LANES:(h + 1) * LANES] = (rope(z) * scale).astype(BF16)
    kc_ref[...] = rope(_dot(hn, w_ref[:, _C_KC:_C_KC + LANES])).astype(BF16)
    vc_ref[...] = _dot(hn, w_ref[:, _C_VC:_C_VC + LANES]).astype(BF16)
    ks_ref[...] = rope(_dot(hn, w_ref[:, _C_KS:_C_KS + LANES])).astype(BF16)
    kw_ref[...] = rope(_dot(hn, w_ref[:, _C_KW:_C_KW + LANES])).astype(BF16)
    sm_ref[...] = _dot(hn, w_ref[:, _C_SM:_C_SM + LANES])
    for c0 in range(0, 2 * ML_WIDTH, 512):
        qkb_ref[:, c0:c0 + 512] = _dot(hn, w_ref[:, _C_QKB + c0:_C_QKB + c0 + 512]).astype(BF16)
    vb_ref[...] = _dot(hn, w_ref[:, _C_VB:_C_VB + ML_WIDTH]).astype(BF16)
    ob_ref[...] = _dot(hn, w_ref[:, _C_OB:_C_OB + ML_WIDTH]).astype(BF16)
    for c0 in range(0, 2 * D_MODEL, 512):
        mg_ref[:, c0:c0 + 512] = _dot(hn, w_ref[:, _C_MG + c0:_C_MG + c0 + 512]).astype(BF16)
    zt = _dot_nt(wt_ref[...], hn)
    ones_rows = (lax.broadcasted_iota(jnp.int32, (VT_PAD, KC), 0) == 0).astype(BF16)
    for i in range(TD // KC):
        for ref, r0 in ((vst_ref, 0), (vwt_ref, LANES)):
            zc = zt[r0:r0 + LANES, i * KC:(i + 1) * KC].astype(BF16)
            ref[i] = jnp.concatenate([piece for g in range(NSA_KV)
                                      for piece in (zc[g * NSA_HD:(g + 1) * NSA_HD, :], ones_rows)], axis=0)
    smt_ref[...] = zt[2 * LANES:3 * LANES, :]


def _inproj(x2, g_mix, wcat, wtr, rc, rp, rm):
    t = x2.shape[0]
    row = lambda w: pl.BlockSpec((TD, w), lambda i: (i, 0))
    out_shape = [
        jax.ShapeDtypeStruct((t, NSA_HEADS * NSA_HD), BF16),
        jax.ShapeDtypeStruct((t, LANES), BF16),
        jax.ShapeDtypeStruct((t, LANES), BF16),
        jax.ShapeDtypeStruct((t, LANES), BF16),
        jax.ShapeDtypeStruct((t, LANES), BF16),
        jax.ShapeDtypeStruct((t // KC, NSA_KV * VT_ROWS, KC), BF16),
        jax.ShapeDtypeStruct((t // KC, NSA_KV * VT_ROWS, KC), BF16),
        jax.ShapeDtypeStruct((t, LANES), F32),
        jax.ShapeDtypeStruct((LANES, t), F32),
        jax.ShapeDtypeStruct((t, 2 * ML_WIDTH), BF16),
        jax.ShapeDtypeStruct((t, ML_WIDTH), BF16),
        jax.ShapeDtypeStruct((t, ML_WIDTH), BF16),
        jax.ShapeDtypeStruct((t, 2 * D_MODEL), BF16),
    ]
    chunk3 = pl.BlockSpec((TD // KC, NSA_KV * VT_ROWS, KC), lambda i: (i, 0, 0))
    out_specs = [row(NSA_HEADS * NSA_HD), row(LANES), row(LANES), row(LANES), row(LANES), chunk3, chunk3,
                 row(LANES), pl.BlockSpec((LANES, TD), lambda i: (0, i)),
                 row(2 * ML_WIDTH), row(ML_WIDTH), row(ML_WIDTH), row(2 * D_MODEL)]
    return pl.pallas_call(
        _inproj_kernel,
        out_shape=out_shape,
        grid=(t // TD,),
        in_specs=[row(D_MODEL), _const_spec((1, D_MODEL)), _const_spec((D_MODEL, _C_END)),
                  _const_spec((3 * LANES, D_MODEL)), row(LANES), row(LANES), row(LANES)],
        out_specs=out_specs,
        compiler_params=pltpu.CompilerParams(dimension_semantics=("arbitrary",), vmem_limit_bytes=VMEM_LIMIT),
        name="inproj",
    )(x2, g_mix, wcat, wtr, rc, rp, rm)


def _gelu_tanh(x):
    return 0.5 * x * (1.0 + jnp.tanh(math.sqrt(2.0 / math.pi) * (x + 0.044715 * x * x * x)))


def _compress_kernel(rk_ref, rv_ref, wka_ref, wkb_ref, wva_ref, wvb_ref, pek_ref, pev_ref,
                     w1k_ref, w1v_ref, w2k_ref, w2vt_ref, kc_ref, vct_ref):
    nrow = rk_ref.shape[0]

    def hidden(r_ref, wa_ref, wb_ref, pe_ref, w1_ref):
        r = r_ref[...]
        ha = _dot(r, wa_ref[...])
        hb = _dot(r, wb_ref[...])
        hb = pltpu.roll(hb, nrow - 1, 0)
        c = _dot(pe_ref[...], w1_ref[...])[0:1, :]
        return [_gelu_tanh(ha[:, g * CMP_HIDDEN:(g + 1) * CMP_HIDDEN] + hb[:, g * CMP_HIDDEN:(g + 1) * CMP_HIDDEN] + c).astype(BF16)
                for g in range(NSA_KV)]

    ak = hidden(rk_ref, wka_ref, wkb_ref, pek_ref, w1k_ref)
    kc_ref[...] = (_dot(ak[0], w2k_ref[0]) + _dot(ak[1], w2k_ref[1])).astype(BF16)
    av = hidden(rv_ref, wva_ref, wvb_ref, pev_ref, w1v_ref)
    for g in range(NSA_KV):
        vct_ref[g * NSA_HD:(g + 1) * NSA_HD, :] = _dot_nt(w2vt_ref[...], av[g]).astype(BF16)


def _compress(rk, rv, wka, wkb, wva, wvb, pek, pev, w1k, w1v, w2k, w2vt):
    b, nrow, width = rk.shape
    blk = pl.BlockSpec((None, nrow, width), lambda i: (i, 0, 0))
    return pl.pallas_call(
        _compress_kernel,
        out_shape=[jax.ShapeDtypeStruct((b, nrow, LANES), BF16),
                   jax.ShapeDtypeStruct((b, LANES, nrow), BF16)],
        grid=(b,),
        in_specs=[blk, blk] + [_const_spec(a.shape) for a in (wka, wkb, wva, wvb, pek, pev, w1k, w1v, w2k, w2vt)],
        out_specs=[pl.BlockSpec((None, nrow, LANES), lambda i: (i, 0, 0)),
                   pl.BlockSpec((None, LANES, nrow), lambda i: (i, 0, 0))],
        compiler_params=pltpu.CompilerParams(dimension_semantics=("arbitrary",), vmem_limit_bytes=VMEM_LIMIT),
        name="compress",
    )(rk, rv, wka, wkb, wva, wvb, pek, pev, w1k, w1v, w2k, w2vt)


def _nsa_kernel(q_ref, kc_ref, vct_ref, ks_ref, kw_ref, vst_ref, vwt_ref, smt_ref, o_ref, bias_scr, sx_scr, sy_scr):
    c = pl.program_id(1)
    t0 = c * QT
    ncmp = kc_ref.shape[0]
    nsel = bias_scr.shape[0]
    nw = WINDOW // KC + 1
    gw = NSA_HPG * QT
    width = NSA_KV * gw

    def per_group(x):
        return [x[:, g * gw:(g + 1) * gw] for g in range(NSA_KV)]

    def pv(vt, p):
        rows = vt.shape[0] // NSA_KV
        pb = p.astype(BF16)
        return jnp.concatenate([_dot(vt[g * rows:(g + 1) * rows, :], pg) for g, pg in enumerate(per_group(pb))],
                               axis=1)

    def normalised(acc):
        return acc[0:NSA_HD, :] / acc[NSA_HD:NSA_HD + 1, :]

    low_half = lax.broadcasted_iota(jnp.int32, (1, LANES), 1) < NSA_HD
    q_heads = []
    for h in range(NSA_HEADS):
        pair = q_ref[:, (h // 2) * LANES:(h // 2 + 1) * LANES].astype(F32)
        want_low = h // NSA_HPG == 0
        if (h % 2 == 0) != want_low:
            pair = pltpu.roll(pair, NSA_HD, 1)
        q_heads.append(jnp.where(low_half if want_low else ~low_half, pair, 0.0).astype(BF16))
    qs = jnp.concatenate(q_heads, axis=0)
    u_row = lax.broadcasted_iota(jnp.int32, (1, width), 1) % QT
    t_row = t0 + u_row
    r_kc = lax.broadcasted_iota(jnp.int32, (KC, 1), 0)

    n_grp = ks_ref.shape[0] // SEL_GROUP
    n_full = lax.shift_right_logical(t0, int(math.log2(SEL_GROUP)))

    def qk_group(j):
        return _dot_nt(ks_ref[pl.ds(pl.multiple_of(j * SEL_GROUP, SEL_GROUP), SEL_GROUP), :], qs)

    sc = _dot_nt(kc_ref[...], qs)

    w_slabs, w_chunks = [], []
    for i in range(nw):
        jj = c - (nw - 1) + i
        jc = jnp.maximum(jj, 0)
        si = _dot_nt(kw_ref[pl.ds(pl.multiple_of(jc * KC, KC), KC), :], qs)
        if i == 0:
            keep = (r_kc > u_row) & (jj >= 0)
        elif i == nw - 1:
            keep = r_kc <= u_row
        else:
            keep = jj >= 0
        w_slabs.append(jnp.where(keep, si, NEG))
        w_chunks.append(jc)

    n_col = lax.broadcasted_iota(jnp.int32, (ncmp, 1), 0)
    cmask = (CMP_STRIDE * n_col + (CMP_LEN - 1) <= t_row) & (n_col < ncmp - 1)
    s = jnp.where(cmask, sc, NEG)
    m = jnp.max(s, axis=0, keepdims=True)
    e = jnp.exp2(s - m)
    anyv = (t_row >= CMP_LEN - 1).astype(F32)
    p = e * (anyv / jnp.sum(e, axis=0, keepdims=True))
    o_cmp = pv(vct_ref[...], p)

    psums = []
    for pg in per_group(p):
        acc_p = pg[:, 0:QT]
        for h in range(1, NSA_HPG):
            acc_p = acc_p + pg[:, h * QT:(h + 1) * QT]
        psums.append(acc_p)
    psum = jnp.concatenate(psums, axis=1)
    nq2 = NSA_KV * QT
    s_col = lax.broadcasted_iota(jnp.int32, (nsel, 1), 0)
    n_lane = lax.broadcasted_iota(jnp.int32, (1, ncmp), 1)
    ov = ((CMP_STRIDE * n_lane < SEL_BLOCK * (s_col + 1)) & (CMP_STRIDE * n_lane + (CMP_LEN - 1) >= SEL_BLOCK * s_col)
          ).astype(BF16)
    imp = sum(_dot(ov, part) for part in _split3(psum))

    s_diag = qk_group(n_full)
    sx_scr[...] = qk_group(0)

    mxw = w_slabs[0]
    for sl in w_slabs[1:]:
        mxw = jnp.maximum(mxw, sl)
    mw = jnp.max(mxw, axis=0, keepdims=True)
    acc_w = jnp.zeros((VT_ROWS, width), F32)
    for sl, jc in zip(w_slabs, w_chunks):
        acc_w = acc_w + pv(vwt_ref[jc], jnp.exp2(sl - mw))
    o_win = normalised(acc_w)

    t1 = t0 + lax.broadcasted_iota(jnp.int32, (1, nq2), 1) % QT
    cur = lax.shift_right_logical(t1, 6)
    forced = (s_col == 0) | (s_col == cur) | (s_col == cur - 1)
    valid = SEL_BLOCK * s_col <= t1
    val = jnp.where(valid, jnp.where(forced, imp + SEL_FORCE, imp), NEG)
    sub = 8
    r_sub = lax.broadcasted_iota(jnp.int32, (sub, 1), 0)
    blocks = [val[r * sub:(r + 1) * sub, :] for r in range(nsel // sub)]
    ranks = [jnp.zeros((sub, nq2), F32) for _ in blocks]
    for i in range(nsel):
        vi = val[i:i + 1, :]
        for r, blk in enumerate(blocks):
            if i < r * sub:
                beats = vi >= blk
            elif i >= (r + 1) * sub:
                beats = vi > blk
            else:
                beats = (vi > blk) | ((vi == blk) & (r_sub > i - r * sub))
            ranks[r] = ranks[r] + jnp.where(beats, 1.0, 0.0)
    bias = jnp.where(jnp.concatenate(ranks, axis=0) < SEL_TOPK, 0.0, NEG).astype(F32)
    bias_scr[...] = jnp.concatenate([bias[:, g * QT:(g + 1) * QT] for g in range(NSA_KV) for _ in range(NSA_HPG)],
                                    axis=1)

    r_blk = lax.broadcasted_iota(jnp.int32, (SEL_BLOCK, 1), 0)
    blk_per_grp = SEL_GROUP // SEL_BLOCK
    chunk_per_grp = SEL_GROUP // KC
    blk_per_chunk = KC // SEL_BLOCK

    def sel_update(j, sj, carry, causal=False, live=None):
        m_o, acc = carry
        brows, keeps = [], []
        for i in range(blk_per_grp):
            brow = bias_scr[pl.ds(blk_per_grp * j + i, 1), :]
            brows.append(brow if live is None else jnp.where(live, brow, NEG))
            keeps.append(j * SEL_GROUP + i * SEL_BLOCK + r_blk <= t_row if causal else None)

        def block(i):
            return sj[i * SEL_BLOCK:(i + 1) * SEL_BLOCK, :]

        mx = None
        for i in range(blk_per_grp):
            sl = block(i) + brows[i]
            if causal:
                sl = jnp.where(keeps[i], sl, NEG)
            mx = sl if mx is None else jnp.maximum(mx, sl)
        m_n = jnp.maximum(m_o, jnp.max(mx, axis=0, keepdims=True))
        a = jnp.exp2(m_o - m_n)
        acc = a * acc
        for ci in range(chunk_per_grp):
            parts = []
            for i in range(blk_per_chunk * ci, blk_per_chunk * (ci + 1)):
                arg = block(i) + (brows[i] - m_n)
                parts.append(jnp.exp2(jnp.where(keeps[i], arg, NEG) if causal else arg))
            acc = acc + pv(vst_ref[chunk_per_grp * j + ci], jnp.concatenate(parts, axis=0))
        return m_n, acc

    empty = (jnp.full((1, width), NEG, F32), jnp.zeros((VT_ROWS, width), F32))
    seeded = sel_update(n_full, s_diag, empty, causal=True)

    def pair_body(jp, carry):
        ja, jb = 2 * jp, 2 * jp + 1
        sy_scr[...] = qk_group(jb)
        carry = sel_update(ja, sx_scr, carry)
        sx_scr[...] = qk_group(jnp.minimum(ja + 2, n_grp - 1))
        return sel_update(jb, sy_scr, carry)

    n_pairs = lax.shift_right_logical(n_full, 1)
    carry = lax.fori_loop(0, n_pairs, pair_body, seeded)
    _, acc_s = lax.cond(n_full - 2 * n_pairs == 1, lambda cr: sel_update(n_full - 1, sx_scr, cr), lambda cr: cr,
                        carry)
    o_sel = normalised(acc_s)

    def gate_row(br):
        rows = [smt_ref[SM_GATE + 3 * h + br:SM_GATE + 3 * h + br + 1, :] for h in range(NSA_HEADS)]
        return _sigmoid(jnp.concatenate(rows, axis=1))

    o_t = gate_row(0) * o_cmp + gate_row(1) * o_sel + gate_row(2) * o_win
    for pr in range(NSA_HEADS // 2):
        xp = jnp.concatenate([o_t[:, (2 * pr) * QT:(2 * pr + 1) * QT], o_t[:, (2 * pr + 1) * QT:(2 * pr + 2) * QT]], axis=0)
        o_ref[:, pr * LANES:(pr + 1) * LANES] = xp.T.astype(BF16)


def _nsa(qpad, kcb, vct, ks, kw, vst, vwt, smt, b, s):
    nq = s // QT
    ncmp = kcb.shape[1]
    return pl.pallas_call(
        _nsa_kernel,
        out_shape=jax.ShapeDtypeStruct((b * s, NSA_HEADS * NSA_HD), BF16),
        grid=(b, nq),
        in_specs=[
            pl.BlockSpec((QT, NSA_HEADS * NSA_HD), lambda bi, c: (bi * nq + c, 0)),
            pl.BlockSpec((None, ncmp, LANES), lambda bi, c: (bi, 0, 0)),
            pl.BlockSpec((None, NSA_KV * NSA_HD, ncmp), lambda bi, c: (bi, 0, 0)),
            pl.BlockSpec((s, LANES), lambda bi, c: (bi, 0)),
            pl.BlockSpec((s, LANES), lambda bi, c: (bi, 0)),
            pl.BlockSpec((s // KC, NSA_KV * VT_ROWS, KC), lambda bi, c: (bi, 0, 0)),
            pl.BlockSpec((s // KC, NSA_KV * VT_ROWS, KC), lambda bi, c: (bi, 0, 0)),
            pl.BlockSpec((LANES, QT), lambda bi, c: (0, bi * nq + c)),
        ],
        out_specs=pl.BlockSpec((QT, NSA_HEADS * NSA_HD), lambda bi, c: (bi * nq + c, 0)),
        scratch_shapes=[pltpu.VMEM((s // SEL_BLOCK, NSA_HEADS * QT), F32),
                        pltpu.VMEM((SEL_GROUP, NSA_HEADS * QT), F32),
                        pltpu.VMEM((SEL_GROUP, NSA_HEADS * QT), F32)],
        compiler_params=pltpu.CompilerParams(dimension_semantics=("arbitrary", "arbitrary"),
                                             vmem_limit_bytes=VMEM_LIMIT),
        name="nsa",
    )(qpad, kcb, vct, ks, kw, vst, vwt, smt)


def _log_sigmoid(x):
    return jnp.minimum(x, 0.0) - jnp.log(1.0 + jnp.exp(-jnp.abs(x)))


def _mlstm_kernel(qk_ref, v_ref, og_ref, sm_ref, smt_ref, wc_ref, bc_ref, bifc_ref, bifr_ref, ghn_ref,
                  y_ref, ext_scr, ct_scr, n_scr, m_scr):
    lc = ML_CHUNK

    @pl.when(pl.program_id(1) == 0)
    def _():
        ext_scr[0:8, :] = jnp.zeros((8, 2 * ML_WIDTH), F32)
        ct_scr[...] = jnp.zeros_like(ct_scr)
        n_scr[...] = jnp.zeros_like(n_scr)
        m_scr[...] = jnp.zeros_like(m_scr)

    ext_scr[8:8 + ML_BLOCK, :] = qk_ref[...].astype(F32)
    y = bc_ref[...]
    for j in range(CONV_W):
        y = y + wc_ref[j:j + 1, :] * ext_scr[pl.ds(8 - (CONV_W - 1) + j, ML_BLOCK), :]
    ext_scr[0:8, :] = ext_scr[ML_BLOCK:ML_BLOCK + 8, :]
    qkc = y * _sigmoid(y)
    q_all = qkc[:, 0:ML_WIDTH].astype(BF16)
    k_all = (qkc[:, ML_WIDTH:2 * ML_WIDTH] * (ML_HD ** -0.5)).astype(BF16)

    ifc = sm_ref[...] + bifc_ref[...]
    ifr = smt_ref[...] + bifr_ref[...]
    lfc = _log_sigmoid(ifc)
    lfr = _log_sigmoid(ifr)
    rr = lax.broadcasted_iota(jnp.int32, (lc, lc), 0)
    cc = lax.broadcasted_iota(jnp.int32, (lc, lc), 1)
    causal = rr >= cc
    tri_l = causal.astype(F32)
    tri_u = (rr <= cc).astype(F32)

    for ci in range(ML_BLOCK // lc):
        lo, hi = ci * lc, (ci + 1) * lc
        bc_all = jnp.dot(tri_l, lfc[lo:hi, :], preferred_element_type=F32, precision=lax.Precision.HIGHEST)
        br_all = jnp.dot(lfr[:, lo:hi], tri_u, preferred_element_type=F32, precision=lax.Precision.HIGHEST)
        heads = range(ML_HEADS)
        hsl = [slice(h * ML_HD, (h + 1) * ML_HD) for h in heads]
        bcol = [bc_all[:, SM_F + h:SM_F + h + 1] for h in heads]
        brow = [br_all[SM_F + h:SM_F + h + 1, :] for h in heads]
        icol = [ifc[lo:hi, SM_I + h:SM_I + h + 1] for h in heads]
        irow = [ifr[SM_I + h:SM_I + h + 1, lo:hi] for h in heads]
        mprev = [m_scr[h][:, 0:1] for h in heads]
        qh = [q_all[lo:hi, hsl[h]] for h in heads]
        kh = [k_all[lo:hi, hsl[h]] for h in heads]
        vh = [v_ref[lo:hi, hsl[h]] for h in heads]
        ct = [ct_scr[h] for h in heads]
        nrow = [n_scr[h] for h in heads]
        qk = [_dot_nt(qh[h], kh[h]) for h in heads]
        qc = [_dot(qh[h], ct[h].astype(BF16)) for h in heads]
        dmat = [jnp.where(causal, bcol[h] - brow[h] + irow[h], NEG) for h in heads]
        inter = [bcol[h] + mprev[h] for h in heads]
        mt = [jnp.maximum(jnp.max(dmat[h], axis=-1, keepdims=True), inter[h]) for h in heads]
        a = [jnp.exp(dmat[h] - mt[h]) * qk[h] for h in heads]
        dec = [jnp.exp(inter[h] - mt[h]) for h in heads]
        num = [_dot(a[h].astype(BF16), vh[h]) + dec[h] * qc[h] for h in heads]
        den = [jnp.sum(a[h], axis=-1, keepdims=True)
               + dec[h] * jnp.sum(qh[h].astype(F32) * nrow[h], axis=-1, keepdims=True) for h in heads]
        blast = [bcol[h][lc - 1:lc, :] for h in heads]
        mnew = [jnp.maximum(blast[h] + mprev[h], jnp.max(blast[h] - brow[h] + irow[h], axis=-1, keepdims=True))
                for h in heads]
        wprev = [jnp.exp(blast[h] + mprev[h] - mnew[h]) for h in heads]
        kwt = [kh[h].astype(F32) * jnp.exp(blast[h] - bcol[h] + icol[h] - mnew[h]) for h in heads]
        for h in heads:
            ct_scr[h] = wprev[h] * ct[h] + lax.dot_general(kwt[h].astype(BF16), vh[h], _TN,
                                                           preferred_element_type=F32)
            n_scr[h] = wprev[h] * nrow[h] + jnp.sum(kwt[h], axis=0, keepdims=True)
            m_scr[h] = jnp.broadcast_to(mnew[h], (1, LANES))
        hm = [num[h] / jnp.maximum(jnp.abs(den[h]), jnp.exp(-mt[h])) * _sigmoid(og_ref[lo:hi, hsl[h]].astype(F32))
              for h in heads]
        for h in heads:
            y_ref[lo:hi, hsl[h]] = _rms(hm[h], ghn_ref[:, hsl[h]]).astype(BF16)


def _mlstm(qkb, vb, ob, sm, smt, wconv, bconv, bifc, bifr, ghn, b, s):
    nb = s // ML_BLOCK
    row = lambda w: pl.BlockSpec((ML_BLOCK, w), lambda bi, j: (bi * nb + j, 0))
    return pl.pallas_call(
        _mlstm_kernel,
        out_shape=jax.ShapeDtypeStruct((b * s, ML_WIDTH), BF16),
        grid=(b, nb),
        in_specs=[row(2 * ML_WIDTH), row(ML_WIDTH), row(ML_WIDTH), row(LANES),
                  pl.BlockSpec((LANES, ML_BLOCK), lambda bi, j: (0, bi * nb + j)),
                  _const_spec(wconv.shape), _const_spec(bconv.shape), _const_spec(bifc.shape),
                  _const_spec(bifr.shape), _const_spec(ghn.shape)],
        out_specs=row(ML_WIDTH),
        scratch_shapes=[pltpu.VMEM((ML_BLOCK + 8, 2 * ML_WIDTH), F32),
                        pltpu.VMEM((ML_HEADS, ML_HD, ML_HD), F32),
                        pltpu.VMEM((ML_HEADS, 1, ML_HD), F32),
                        pltpu.VMEM((ML_HEADS, 1, LANES), F32)],
        compiler_params=pltpu.CompilerParams(dimension_semantics=("arbitrary", "arbitrary"),
                                             vmem_limit_bytes=VMEM_LIMIT),
        name="mlstm",
    )(qkb, vb, ob, sm, smt, wconv, bconv, bifc, bifr, ghn)


RT_BUCKET = N_EXPERTS
RT_RANK = N_EXPERTS + 1
RT_WLO = N_EXPERTS + 2
RT_WHI = N_EXPERTS + 3
N_BUCKETS = N_GROUPS * 6
X_ROWS = D_MODEL // LANES
REC = 2 * X_ROWS
DMA_UNROLL = 8


def _merge_kernel(ya_ref, yb_ref, mg_ref, x_ref, wpa_ref, wpb_ref, wout_ref, gffn_ref, wr_ref, br_ref,
                  x1_ref, h2_ref, slab_ref, hist_ref):
    pa = _dot(ya_ref[...], wpa_ref[...])
    pb = _dot(yb_ref[...], wpb_ref[...])
    ga = _sigmoid(mg_ref[:, 0:D_MODEL].astype(F32))
    gb = _sigmoid(mg_ref[:, D_MODEL:2 * D_MODEL].astype(F32))
    mixed = (ga * pa + gb * pb).astype(BF16)
    x1 = x_ref[...] + _dot(mixed, wout_ref[...])
    x1_ref[...] = x1
    h2 = _rms(x1, gffn_ref[...])
    h_hi = h2.astype(BF16)
    h2_ref[...] = h2

    h_lo = (h2 - h_hi.astype(F32)).astype(BF16)
    r_hi = _dot(h_hi, wr_ref[...])
    logit = r_hi[:, 0:LANES] + r_hi[:, LANES:2 * LANES] + _dot(h_lo, wr_ref[:, 0:LANES]) + br_ref[...]
    lane = lax.broadcasted_iota(jnp.int32, logit.shape, 1)
    big = jnp.int32(LANES)
    gmask = (lane >= N_EXPERTS) & (lane < N_EXPERTS + N_GROUPS)
    gl = jnp.where(gmask, logit, NEG)
    gmax = jnp.max(gl, axis=-1, keepdims=True)
    gidx = jnp.min(jnp.where(gmask & (gl == gmax), lane, big), axis=-1, keepdims=True) - N_EXPERTS
    pg_sel = 1.0 / jnp.sum(jnp.where(gmask, jnp.exp(gl - gmax), 0.0), axis=-1, keepdims=True)
    emask = (lane < N_EXPERTS) & (lax.shift_right_logical(lane, 2) == gidx)
    el = jnp.where(emask, logit, NEG)
    e1 = jnp.max(el, axis=-1, keepdims=True)
    i1 = jnp.min(jnp.where(emask & (el == e1), lane, big), axis=-1, keepdims=True)
    emask2 = emask & (lane != i1)
    el2 = jnp.where(emask2, logit, NEG)
    e2 = jnp.max(el2, axis=-1, keepdims=True)
    i2 = jnp.min(jnp.where(emask2 & (el2 == e2), lane, big), axis=-1, keepdims=True)
    x21 = jnp.exp(e2 - e1)
    w1 = pg_sel / (1.0 + x21)
    w2 = pg_sel * x21 / (1.0 + x21)
    first_lo = i1 < i2
    e_lo = jnp.where(first_lo, i1, i2) - EXP_PER_GROUP * gidx
    e_hi = jnp.where(first_lo, i2, i1) - EXP_PER_GROUP * gidx
    pair = lax.shift_right_logical(e_lo * (2 * EXP_PER_GROUP - 1 - e_lo), 1) + (e_hi - e_lo - 1)
    bucket = 6 * gidx + pair
    member = lane == bucket
    onehot = jnp.where(member, 1.0, 0.0)
    rr = lax.broadcasted_iota(jnp.int32, (TD, TD), 0)
    cc = lax.broadcasted_iota(jnp.int32, (TD, TD), 1)
    earlier = _dot((rr > cc).astype(BF16), onehot.astype(BF16))
    rank = jnp.sum(jnp.where(member, earlier, 0.0), axis=-1, keepdims=True)
    slab = jnp.where(lane == i1, w1, 0.0) + jnp.where(lane == i2, w2, 0.0)
    slab = jnp.where(lane == RT_BUCKET, bucket.astype(F32), slab)
    slab = jnp.where(lane == RT_RANK, rank, slab)
    slab = jnp.where(lane == RT_WLO, jnp.where(first_lo, w1, w2), slab)
    slab = jnp.where(lane == RT_WHI, jnp.where(first_lo, w2, w1), slab)
    slab_ref[...] = slab
    hist_ref[...] = jnp.broadcast_to(jnp.sum(onehot, axis=0, keepdims=True), (8, LANES))


def _merge(ya, yb, mg, x2, wpa, wpb, wout, gffn, wr, br):
    t = x2.shape[0]
    row = lambda w: pl.BlockSpec((TD, w), lambda i: (i, 0))
    return pl.pallas_call(
        _merge_kernel,
        out_shape=[jax.ShapeDtypeStruct((t, D_MODEL), F32),
                   jax.ShapeDtypeStruct((t, D_MODEL), F32),
                   jax.ShapeDtypeStruct((t, LANES), F32),
                   jax.ShapeDtypeStruct((t // TD * 8, LANES), F32)],
        grid=(t // TD,),
        in_specs=[row(NSA_HEADS * NSA_HD), row(ML_WIDTH), row(2 * D_MODEL), row(D_MODEL)]
                 + [_const_spec(a.shape) for a in (wpa, wpb, wout, gffn, wr, br)],
        out_specs=[row(D_MODEL), row(D_MODEL), row(LANES),
                   pl.BlockSpec((8, LANES), lambda i: (i, 0))],
        compiler_params=pltpu.CompilerParams(dimension_semantics=("arbitrary",), vmem_limit_bytes=VMEM_LIMIT),
        name="merge",
    )(ya, yb, mg, x2, wpa, wpb, wout, gffn, wr, br)


def _rec_copy(src_ref, src_tok, dst_ref, dst_tok, sem, rows):
    src = src_ref.at[pl.ds(pl.multiple_of(src_tok * rows, rows), rows), :]
    dst = dst_ref.at[pl.ds(pl.multiple_of(dst_tok * rows, rows), rows), :]
    return pltpu.make_async_copy(src, dst, sem)


def _token_copies(n, make, wait=False):
    def body(g, carry):
        for u in range(DMA_UNROLL):
            cp = make(g * DMA_UNROLL + u)
            if wait:
                cp.wait()
            else:
                cp.start(priority=u % 2)
        return carry
    lax.fori_loop(0, n // DMA_UNROLL, body, 0)


def _dispatch_kernel(pos_ref, tail_ref, h2_ref, slab_ref, out_ref, stage_scr, zero_scr, sem, zsem):
    i = pl.program_id(0)
    slot = lax.rem(i, 2)
    n_tiles = out_ref.shape[0] // (TM * REC)
    n_used = tail_ref[2 * N_BUCKETS]

    def zero_copy(first_slot):
        start = pl.multiple_of(first_slot * REC, TM * REC)
        return pltpu.make_async_copy(zero_scr, out_ref.at[pl.ds(start, TM * REC), :], zsem)

    @pl.when(i == 0)
    def _():
        zero_scr[...] = jnp.zeros_like(zero_scr)
        for phase in ("start", "wait"):
            for b in range(N_BUCKETS):
                @pl.when(tail_ref[N_BUCKETS + b] > 0)
                def _():
                    getattr(zero_copy(tail_ref[b]), phase)()

                @pl.when(n_used + b < n_tiles)
                def _():
                    getattr(zero_copy((n_used + b) * TM), phase)()

    stage = stage_scr.at[slot]
    for j in range(X_ROWS):
        stage[pl.ds(j, TM, stride=REC), :] = h2_ref[:, j * LANES:(j + 1) * LANES]
    stage[pl.ds(X_ROWS, TM, stride=REC), :] = slab_ref[...]
    for j in range(X_ROWS + 1, REC):
        stage[pl.ds(j, TM, stride=REC), :] = jnp.zeros((TM, LANES), F32)
    base = i * TM
    _token_copies(TM, lambda r: _rec_copy(stage_scr.at[slot], r, out_ref, pos_ref[base + r], sem.at[slot], REC))

    def drain(which):
        _token_copies(TM, lambda r: _rec_copy(stage_scr.at[which], 0, out_ref, 0, sem.at[which], REC), wait=True)

    @pl.when(i > 0)
    def _():
        drain(1 - slot)

    @pl.when(i == pl.num_programs(0) - 1)
    def _():
        drain(slot)


def _dispatch(pos, tail, h2, slab, n_slots):
    t = h2.shape[0]
    return pl.pallas_call(
        _dispatch_kernel,
        out_shape=jax.ShapeDtypeStruct((n_slots * REC, LANES), F32),
        grid_spec=pltpu.PrefetchScalarGridSpec(
            num_scalar_prefetch=2,
            grid=(t // TM,),
            in_specs=[pl.BlockSpec((TM, D_MODEL), lambda i, pos_r, tail_r: (i, 0)),
                      pl.BlockSpec((TM, LANES), lambda i, pos_r, tail_r: (i, 0))],
            out_specs=pl.BlockSpec(memory_space=pl.ANY),
            scratch_shapes=[pltpu.VMEM((2, TM * REC, LANES), F32), pltpu.VMEM((TM * REC, LANES), F32),
                            pltpu.SemaphoreType.DMA((2,)), pltpu.SemaphoreType.DMA(())],
        ),
        compiler_params=pltpu.CompilerParams(dimension_semantics=("arbitrary",), vmem_limit_bytes=VMEM_LIMIT,
                                             has_side_effects=True),
        name="dispatch",
    )(pos, tail, h2, slab)


MOE_TILES = 2


def _moe_kernel(te_ref, nu_ref, hx_ref, w13_ref, w2_ref, y_ref):
    step = pl.program_id(0)
    n_tiles = pl.num_programs(0) * MOE_TILES
    n_used = nu_ref[0]

    @pl.when(step * MOE_TILES < n_used)
    def _():
        subs = range(MOE_TILES)
        hs = [jnp.concatenate([hx_ref[pl.ds(sub * TM * REC + j, TM, stride=REC), :] for j in range(X_ROWS)],
                              axis=1).astype(BF16) for sub in subs]
        slabs = [hx_ref[pl.ds(sub * TM * REC + X_ROWS, TM, stride=REC), :] for sub in subs]
        ys = [None] * MOE_TILES
        for side, lane in ((0, RT_WLO), (1, RT_WHI)):
            es = [te_ref[side * n_tiles + step * MOE_TILES + sub] for sub in subs]
            up = [_dot(hs[sub], w13_ref[es[sub]]) for sub in subs]
            act = [(up[sub][:, 0:D_EXPERT] * _sigmoid(up[sub][:, 0:D_EXPERT]) * up[sub][:, D_EXPERT:2 * D_EXPERT]
                    * slabs[sub][:, lane:lane + 1]).astype(BF16) for sub in subs]
            for sub in subs:
                part = _dot(act[sub], w2_ref[es[sub]])
                ys[sub] = part if ys[sub] is None else ys[sub] + part
        for sub in subs:
            for j in range(X_ROWS):
                y_ref[pl.ds(sub * TM * X_ROWS + j, TM, stride=X_ROWS), :] = ys[sub][:, j * LANES:(j + 1) * LANES]

    @pl.when(step * MOE_TILES >= n_used)
    def _():
        y_ref[...] = jnp.zeros_like(y_ref)


def _moe(tile_e, n_used, hx_sorted, w13, w2):
    n_tiles = hx_sorted.shape[0] // (TM * REC)
    rows = MOE_TILES * TM
    last = lambda nu: (nu[0] - 1) // MOE_TILES
    return pl.pallas_call(
        _moe_kernel,
        out_shape=jax.ShapeDtypeStruct((n_tiles * TM * X_ROWS, LANES), F32),
        grid_spec=pltpu.PrefetchScalarGridSpec(
            num_scalar_prefetch=2,
            grid=(n_tiles // MOE_TILES,),
            in_specs=[pl.BlockSpec((rows * REC, LANES), lambda k, te, nu: (jnp.minimum(k, last(nu)), 0)),
                      pl.BlockSpec(w13.shape, lambda k, te, nu: (0, 0, 0), pipeline_mode=pl.Buffered(1)),
                      pl.BlockSpec(w2.shape, lambda k, te, nu: (0, 0, 0), pipeline_mode=pl.Buffered(1))],
            out_specs=pl.BlockSpec((rows * X_ROWS, LANES), lambda k, te, nu: (k, 0)),
        ),
        compiler_params=pltpu.CompilerParams(dimension_semantics=("arbitrary",), vmem_limit_bytes=VMEM_LIMIT),
        name="moe",
    )(tile_e, n_used, hx_sorted, w13, w2)


def _combine_kernel(pos_ref, y_ref, x1_ref, p_ref, gple_ref, wpg_ref, wpp_ref, gfin_ref, o_ref, ybuf, sem):
    i = pl.program_id(0)
    slot = lax.rem(i, 2)

    def gather(tile, which):
        _token_copies(TD, lambda r: _rec_copy(y_ref, pos_ref[tile * TD + r], ybuf.at[which], r, sem.at[which],
                                              X_ROWS))

    @pl.when(i == 0)
    def _():
        gather(0, 0)

    @pl.when(i + 1 < pl.num_programs(0))
    def _():
        gather(i + 1, 1 - slot)

    _token_copies(TD, lambda r: _rec_copy(y_ref, 0, ybuf.at[slot], 0, sem.at[slot], X_ROWS), wait=True)
    yb = ybuf.at[slot]
    x2 = x1_ref[...] + jnp.concatenate([yb[pl.ds(j, TD, stride=X_ROWS), :] for j in range(X_ROWS)], axis=1)
    h3 = _rms(x2, gple_ref[...]).astype(BF16)
    x3 = x2 + _sigmoid(_dot(h3, wpg_ref[...])) * _dot(p_ref[...].astype(BF16), wpp_ref[...])
    o_ref[...] = _rms(x3, gfin_ref[...])


def _combine(pos, y_sorted, x1, p2, gple, wpg, wpp, gfin):
    t = x1.shape[0]
    row = lambda w: pl.BlockSpec((TD, w), lambda i, pos_r: (i, 0))
    const = lambda a: pl.BlockSpec(a.shape, lambda i, pos_r: (0,) * a.ndim, pipeline_mode=pl.Buffered(1))
    return pl.pallas_call(
        _combine_kernel,
        out_shape=jax.ShapeDtypeStruct((t, D_MODEL), F32),
        grid_spec=pltpu.PrefetchScalarGridSpec(
            num_scalar_prefetch=1,
            grid=(t // TD,),
            in_specs=[pl.BlockSpec(memory_space=pl.ANY), row(D_MODEL), row(PLE_DIM),
                      const(gple), const(wpg), const(wpp), const(gfin)],
            out_specs=row(D_MODEL),
            scratch_shapes=[pltpu.VMEM((2, TD * X_ROWS, LANES), F32), pltpu.SemaphoreType.DMA((2,))],
        ),
        compiler_params=pltpu.CompilerParams(dimension_semantics=("arbitrary",), vmem_limit_bytes=VMEM_LIMIT),
        name="combine",
    )(pos, y_sorted, x1, p2, gple, wpg, wpp, gfin)


def _routing_tables(slab, hist8):
    t = slab.shape[0]
    nt = t // TD
    n_tiles = t // TM + N_BUCKETS
    hist = hist8.reshape(nt, 8, LANES)[:, 0, :]
    counts = jnp.sum(hist, axis=0)
    padded = jnp.ceil(counts / TM) * TM
    ends = jnp.cumsum(padded)
    first = (ends - padded)[None, :] + jnp.cumsum(hist, axis=0) - hist
    lane = jnp.arange(LANES, dtype=F32)[None, :]
    mine = lane == slab[:, RT_BUCKET:RT_BUCKET + 1]
    pos = jnp.sum(jnp.where(mine, jnp.repeat(first, TD, axis=0), 0.0), axis=1) + slab[:, RT_RANK]
    starts = jnp.arange(n_tiles, dtype=F32) * TM
    tile_bucket = jnp.minimum(jnp.sum(ends[None, :N_BUCKETS] <= starts[:, None], axis=1), N_BUCKETS - 1)
    group, pair = tile_bucket // 6, tile_bucket % 6
    e_lo = EXP_PER_GROUP * group + jnp.array([0, 0, 0, 1, 1, 2], jnp.int32)[pair]
    e_hi = EXP_PER_GROUP * group + jnp.array([1, 2, 3, 2, 3, 3], jnp.int32)[pair]
    tile_e = jnp.concatenate([e_lo, e_hi]).astype(jnp.int32)
    n_used = (ends[N_BUCKETS - 1] / TM).astype(jnp.int32).reshape(1)
    tail = jnp.concatenate([(ends - TM)[:N_BUCKETS], padded[:N_BUCKETS], n_used.astype(F32)]).astype(jnp.int32)
    return pos.astype(jnp.int32), tile_e, n_used, tail, n_tiles * TM


def _pack_inproj_weights(w):
    d = w.shape[0]
    qw = NSA_HEADS * NSA_HD
    kvw = NSA_KV * NSA_HD
    o = 0
    wq = w[:, o:o + qw]; o += qw
    wkc = w[:, o:o + kvw]; o += kvw
    wvc = w[:, o:o + kvw]; o += kvw
    wks = w[:, o:o + kvw]; o += kvw
    wvs = w[:, o:o + kvw]; o += kvw
    wkw = w[:, o:o + kvw]; o += kvw
    wvw = w[:, o:o + kvw]; o += kvw
    wga = w[:, o:o + 3 * NSA_HEADS]; o += 3 * NSA_HEADS
    wqkb = w[:, o:o + 2 * ML_WIDTH]; o += 2 * ML_WIDTH
    wvb = w[:, o:o + ML_WIDTH]; o += ML_WIDTH
    wob = w[:, o:o + ML_WIDTH]; o += ML_WIDTH
    wif = w[:, o:o + 2 * ML_HEADS]; o += 2 * ML_HEADS
    wmg = w[:, o:o + 2 * D_MODEL]
    wsm = jnp.concatenate([wga, wif, jnp.zeros((d, LANES - 3 * NSA_HEADS - 2 * ML_HEADS), w.dtype)], axis=1)
    wcat = jnp.concatenate([wq, wkc, wvc, wks, wkw, wsm, wqkb, wvb, wob, wmg], axis=1).astype(BF16)
    wtr = jnp.concatenate([wvs, wvw, wsm], axis=1).T.astype(BF16)
    return wcat, wtr


def _rope_lane_tables(positions):
    inv = ROPE_THETA ** (-jnp.arange(0, ROPE_DIM, 2, dtype=F32) / ROPE_DIM)
    ang = positions.astype(F32).reshape(-1, 1) * inv[None, :]
    cos, sin = jnp.cos(ang), jnp.sin(ang)
    half = ROPE_DIM // 2
    d = jnp.arange(LANES) % NSA_HD
    cos_l, sin_l = jnp.tile(cos, (1, LANES // half)), jnp.tile(sin, (1, LANES // half))
    rc = jnp.where(d < ROPE_DIM, cos_l, 1.0)
    rp = jnp.where((d >= half) & (d < ROPE_DIM), sin_l, 0.0)
    rm = jnp.where(d < half, -sin_l, 0.0)
    return rc, rp, rm


def _pack_compress_weights(w1, w2, pe):
    half = CMP_LEN // 2
    w1r = w1.reshape(2, half, NSA_HD, CMP_HIDDEN)
    outs = []
    for part in range(2):
        wb = w1r[part].astype(BF16)
        zb = jnp.zeros_like(wb)
        wp = jnp.stack([jnp.stack([wb, zb], axis=2), jnp.stack([zb, wb], axis=2)], axis=1)
        outs.append(wp.reshape(half * NSA_KV * NSA_HD, NSA_KV * CMP_HIDDEN))
    pe8 = jnp.broadcast_to(pe.reshape(1, CMP_LEN * NSA_HD), (8, CMP_LEN * NSA_HD)).astype(BF16)
    return outs[0], outs[1], pe8, w1.astype(BF16)


def _stages(x, p, positions, g_mix, w_in, b_if, w_ck1, w_ck2, pe_ck, w_cv1, w_cv2, pe_cv, w_conv, b_conv, g_hn, w_pa, w_pb, w_out, g_ffn, w_rg, b_rg, w_re, b_re, w_e13, w_e2, g_ple, w_pg, w_pp, g_final):
    b, s, d = x.shape
    t = b * s
    rc, rp, rm = _rope_lane_tables(positions)
    assert w_in.shape[0] == 1, "the final norm is fused into the layer's last kernel: single-layer problem only"
    for i in range(w_in.shape[0]):
        x2 = x.reshape(t, d)
        wcat, wtr = _pack_inproj_weights(w_in[i])
        (qpad, kc_tok, vc_tok, ks, kw, vst, vwt, sm, smt, qkb, vb, ob, mg) = _inproj(
            x2, g_mix[i].reshape(1, d), wcat, wtr, rc, rp, rm)
        wka, wkb, pek, w1k = _pack_compress_weights(w_ck1[i], w_ck2[i], pe_ck[i])
        wva, wvb, pev, w1v = _pack_compress_weights(w_cv1[i], w_cv2[i], pe_cv[i])
        zpad = jnp.zeros((CMP_HIDDEN, NSA_HD), F32)
        w2k = jnp.stack([jnp.concatenate([w_ck2[i], zpad], axis=1),
                         jnp.concatenate([zpad, w_ck2[i]], axis=1)]).astype(BF16)
        w2vt = w_cv2[i].T.astype(BF16)
        nrow = s // CMP_STRIDE
        rk = kc_tok.reshape(b, nrow, CMP_STRIDE * LANES)
        rv = vc_tok.reshape(b, nrow, CMP_STRIDE * LANES)
        kcb, vct = _compress(rk, rv, wka, wkb, wva, wvb, pek, pev, w1k, w1v, w2k, w2vt)
        ya = _nsa(qpad, kcb, vct, ks, kw, vst, vwt, smt, b, s)
        bif = b_if[i].astype(F32)
        bifc = jnp.zeros((1, LANES), F32).at[0, SM_I:SM_I + 2 * ML_HEADS].set(bif)
        bifr = bifc.reshape(LANES, 1)
        yb = _mlstm(qkb, vb, ob, sm, smt, w_conv[i], b_conv[i].reshape(1, -1), bifc, bifr,
                    g_hn[i].reshape(1, -1), b, s)
        wr = jnp.concatenate([w_re[i], w_rg[i], jnp.zeros((d, LANES - N_EXPERTS - N_GROUPS), F32)], axis=1)
        wr_hi = wr.astype(BF16)
        wr = jnp.concatenate([wr_hi, (wr - wr_hi.astype(F32)).astype(BF16)], axis=1)
        br =jnp.concatenate([b_re[i], b_rg[i], jnp.zeros((LANES - N_EXPERTS - N_GROUPS,), F32)]).reshape(1, LANES)
        x1, h2, slab, hist8 = _merge(ya, yb, mg, x2, w_pa[i].astype(BF16), w_pb[i].astype(BF16),
                                     w_out[i].astype(BF16), g_ffn[i].reshape(1, d), wr, br)
        pos, tile_e, n_used, tail, n_slots = _routing_tables(slab, hist8)
        hx_sorted = _dispatch(pos, tail, h2, slab, n_slots)
        y_sorted = _moe(tile_e, n_used, hx_sorted, w_e13[i].astype(BF16), w_e2[i].astype(BF16))
        out = _combine(pos, y_sorted, x1, p[i].reshape(t, PLE_DIM), g_ple[i].reshape(1, d), w_pg[i].astype(BF16),
                       w_pp[i].astype(BF16), g_final.reshape(1, d))
        x = out.reshape(b, s, d)
    return dict(out=x, qpad=qpad, ks=ks, kcb=kcb, vct=vct, y_a=ya, y_b=yb, x1=x1, pos=pos)


def kernel(x, p, positions, g_mix, w_in, b_if, w_ck1, w_ck2, pe_ck, w_cv1, w_cv2, pe_cv, w_conv, b_conv, g_hn, w_pa, w_pb, w_out, g_ffn, w_rg, b_rg, w_re, b_re, w_e13, w_e2, g_ple, w_pg, w_pp, g_final):
    return _stages(x, p, positions, g_mix, w_in, b_if, w_ck1, w_ck2, pe_ck, w_cv1, w_cv2, pe_cv, w_conv, b_conv, g_hn,
                   w_pa, w_pb, w_out, g_ffn, w_rg, b_rg, w_re, b_re, w_e13, w_e2, g_ple, w_pg, w_pp, g_final)["out"]
```

```python
import functools
import math

import jax
import jax.numpy as jnp
from jax import lax
from jax.experimental import pallas as pl
from jax.experimental.pallas import tpu as pltpu

F32 = jnp.float32
BF16 = jnp.bfloat16

EPS = 1e-6
NEG = -1e30

D_MODEL = 1024
PLE_DIM = 256
NSA_HEADS = 8
NSA_KV = 2
NSA_HPG = NSA_HEADS // NSA_KV
NSA_HD = 64
CMP_LEN = 32
CMP_STRIDE = 16
CMP_HIDDEN = 256
SEL_BLOCK = 64
SEL_TOPK = 16
SEL_FORCE = 1000.0
WINDOW = 512
ROPE_THETA = 500000.0
ROPE_DIM = NSA_HD // 4
ML_HEADS = 4
ML_HD = 128
ML_WIDTH = ML_HEADS * ML_HD
CONV_W = 4
N_GROUPS = 4
EXP_PER_GROUP = 4
N_EXPERTS = N_GROUPS * EXP_PER_GROUP
D_EXPERT = 256

LANES = 128
QT = 128
KC = 128
SEL_GROUP = 512
VT_PAD = 16
VT_ROWS = NSA_HD + VT_PAD
ML_CHUNK = 128
ML_BLOCK = 256
TD = 512
TM = 256
VMEM_LIMIT = 56 * 1024 * 1024

_NT = (((1,), (1,)), ((), ()))
_TN = (((0,), (0,)), ((), ()))

SM_GATE = 0
SM_I = 3 * NSA_HEADS
SM_F = SM_I + ML_HEADS


def _dot(a, b):
    return jnp.dot(a, b, preferred_element_type=F32)


def _dot_nt(a, b):
    return lax.dot_general(a, b, _NT, preferred_element_type=F32)


def _split3(x):
    hi = x.astype(BF16)
    r1 = x - hi.astype(F32)
    mid = r1.astype(BF16)
    lo = (r1 - mid.astype(F32)).astype(BF16)
    return hi, mid, lo


def _rms(x, g):
    return x * lax.rsqrt(jnp.mean(x * x, axis=-1, keepdims=True) + EPS) * g


def _sigmoid(x):
    return 0.5 + 0.5 * jnp.tanh(0.5 * x)


def _const_spec(shape):
    nd = len(shape)
    return pl.BlockSpec(shape, lambda *_: (0,) * nd, pipeline_mode=pl.Buffered(1))


_C_Q = 0
_C_KC = _C_Q + NSA_HEADS * NSA_HD
_C_VC = _C_KC + LANES
_C_KS = _C_VC + LANES
_C_KW = _C_KS + LANES
_C_SM = _C_KW + LANES
_C_QKB = _C_SM + LANES
_C_VB = _C_QKB + 2 * ML_WIDTH
_C_OB = _C_VB + ML_WIDTH
_C_MG = _C_OB + ML_WIDTH
_C_END = _C_MG + 2 * D_MODEL


def _inproj_kernel(x_ref, g_ref, w_ref, wt_ref, rc_ref, rp_ref, rm_ref,
                   q_ref, kc_ref, vc_ref, ks_ref, kw_ref, vst_ref, vwt_ref, sm_ref, smt_ref,
                   qkb_ref, vb_ref, ob_ref, mg_ref):
    hn = _rms(x_ref[...], g_ref[...]).astype(BF16)
    rc, rp, rm = rc_ref[...], rp_ref[...], rm_ref[...]

    def rope(z):
        return z * rc + pltpu.roll(z, 8, 1) * rp + pltpu.roll(z, LANES - 8, 1) * rm

    scale = NSA_HD ** -0.5 * math.log2(math.e)
    for h in range(NSA_HEADS * NSA_HD // LANES):
        z = _dot(hn, w_ref[:, _C_Q + h * LANES:_C_Q + (h + 1) * LANES])
        q_ref[:, h * LANES:(h + 1) * LANES] = (rope(z) * scale).astype(BF16)
    kc_ref[...] = rope(_dot(hn, w_ref[:, _C_KC:_C_KC + LANES])).astype(BF16)
    vc_ref[...] = _dot(hn, w_ref[:, _C_VC:_C_VC + LANES]).astype(BF16)
    ks_ref[...] = rope(_dot(hn, w_ref[:, _C_KS:_C_KS + LANES])).astype(BF16)
    kw_ref[...] = rope(_dot(hn, w_ref[:, _C_KW:_C_KW + LANES])).astype(BF16)
    sm_ref[...] = _dot(hn, w_ref[:, _C_SM:_C_SM + LANES])
    for c0 in range(0, 2 * ML_WIDTH, 512):
        qkb_ref[:, c0:c0 + 512] = _dot(hn, w_ref[:, _C_QKB + c0:_C_QKB + c0 + 512]).astype(BF16)
    vb_ref[...] = _dot(hn, w_ref[:, _C_VB:_C_VB + ML_WIDTH]).astype(BF16)
    ob_ref[...] = _dot(hn, w_ref[:, _C_OB:_C_OB + ML_WIDTH]).astype(BF16)
    for c0 in range(0, 2 * D_MODEL, 512):
        mg_ref[:, c0:c0 + 512] = _dot(hn, w_ref[:, _C_MG + c0:_C_MG + c0 + 512]).astype(BF16)
    zt = _dot_nt(wt_ref[...], hn)
    ones_rows = (lax.broadcasted_iota(jnp.int32, (VT_PAD, KC), 0) == 0).astype(BF16)
    for i in range(TD // KC):
        for ref, r0 in ((vst_ref, 0), (vwt_ref, LANES)):
            zc = zt[r0:r0 + LANES, i * KC:(i + 1) * KC].astype(BF16)
            ref[i] = jnp.concatenate([piece for g in range(NSA_KV)
                                      for piece in (zc[g * NSA_HD:(g + 1) * NSA_HD, :], ones_rows)], axis=0)
    smt_ref[...] = zt[2 * LANES:3 * LANES, :]


def _inproj(x2, g_mix, wcat, wtr, rc, rp, rm):
    t = x2.shape[0]
    row = lambda w: pl.BlockSpec((TD, w), lambda i: (i, 0))
    out_shape = [
        jax.ShapeDtypeStruct((t, NSA_HEADS * NSA_HD), BF16),
        jax.ShapeDtypeStruct((t, LANES), BF16),
        jax.ShapeDtypeStruct((t, LANES), BF16),
        jax.ShapeDtypeStruct((t, LANES), BF16),
        jax.ShapeDtypeStruct((t, LANES), BF16),
        jax.ShapeDtypeStruct((t // KC, NSA_KV * VT_ROWS, KC), BF16),
        jax.ShapeDtypeStruct((t // KC, NSA_KV * VT_ROWS, KC), BF16),
        jax.ShapeDtypeStruct((t, LANES), F32),
        jax.ShapeDtypeStruct((LANES, t), F32),
        jax.ShapeDtypeStruct((t, 2 * ML_WIDTH), BF16),
        jax.ShapeDtypeStruct((t, ML_WIDTH), BF16),
        jax.ShapeDtypeStruct((t, ML_WIDTH), BF16),
        jax.ShapeDtypeStruct((t, 2 * D_MODEL), BF16),
    ]
    chunk3 = pl.BlockSpec((TD // KC, NSA_KV * VT_ROWS, KC), lambda i: (i, 0, 0))
    out_specs = [row(NSA_HEADS * NSA_HD), row(LANES), row(LANES), row(LANES), row(LANES), chunk3, chunk3,
                 row(LANES), pl.BlockSpec((LANES, TD), lambda i: (0, i)),
                 row(2 * ML_WIDTH), row(ML_WIDTH), row(ML_WIDTH), row(2 * D_MODEL)]
    return pl.pallas_call(
        _inproj_kernel,
        out_shape=out_shape,
        grid=(t // TD,),
        in_specs=[row(D_MODEL), _const_spec((1, D_MODEL)), _const_spec((D_MODEL, _C_END)),
                  _const_spec((3 * LANES, D_MODEL)), row(LANES), row(LANES), row(LANES)],
        out_specs=out_specs,
        compiler_params=pltpu.CompilerParams(dimension_semantics=("arbitrary",), vmem_limit_bytes=VMEM_LIMIT),
        name="inproj",
    )(x2, g_mix, wcat, wtr, rc, rp, rm)


def _gelu_tanh(x):
    return 0.5 * x * (1.0 + jnp.tanh(math.sqrt(2.0 / math.pi) * (x + 0.044715 * x * x * x)))


def _compress_kernel(rk_ref, rv_ref, wka_ref, wkb_ref, wva_ref, wvb_ref, pek_ref, pev_ref,
                     w1k_ref, w1v_ref, w2k_ref, w2vt_ref, kc_ref, vct_ref):
    nrow = rk_ref.shape[0]

    def hidden(r_ref, wa_ref, wb_ref, pe_ref, w1_ref):
        r = r_ref[...]
        ha = _dot(r, wa_ref[...])
        hb = _dot(r, wb_ref[...])
        hb = pltpu.roll(hb, nrow - 1, 0)
        c = _dot(pe_ref[...], w1_ref[...])[0:1, :]
        return [_gelu_tanh(ha[:, g * CMP_HIDDEN:(g + 1) * CMP_HIDDEN] + hb[:, g * CMP_HIDDEN:(g + 1) * CMP_HIDDEN] + c).astype(BF16)
                for g in range(NSA_KV)]

    ak = hidden(rk_ref, wka_ref, wkb_ref, pek_ref, w1k_ref)
    kc_ref[...] = (_dot(ak[0], w2k_ref[0]) + _dot(ak[1], w2k_ref[1])).astype(BF16)
    av = hidden(rv_ref, wva_ref, wvb_ref, pev_ref, w1v_ref)
    for g in range(NSA_KV):
        vct_ref[g * NSA_HD:(g + 1) * NSA_HD, :] = _dot_nt(w2vt_ref[...], av[g]).astype(BF16)


def _compress(rk, rv, wka, wkb, wva, wvb, pek, pev, w1k, w1v, w2k, w2vt):
    b, nrow, width = rk.shape
    blk = pl.BlockSpec((None, nrow, width), lambda i: (i, 0, 0))
    return pl.pallas_call(
        _compress_kernel,
        out_shape=[jax.ShapeDtypeStruct((b, nrow, LANES), BF16),
                   jax.ShapeDtypeStruct((b, LANES, nrow), BF16)],
        grid=(b,),
        in_specs=[blk, blk] + [_const_spec(a.shape) for a in (wka, wkb, wva, wvb, pek, pev, w1k, w1v, w2k, w2vt)],
        out_specs=[pl.BlockSpec((None, nrow, LANES), lambda i: (i, 0, 0)),
                   pl.BlockSpec((None, LANES, nrow), lambda i: (i, 0, 0))],
        compiler_params=pltpu.CompilerParams(dimension_semantics=("arbitrary",), vmem_limit_bytes=VMEM_LIMIT),
        name="compress",
    )(rk, rv, wka, wkb, wva, wvb, pek, pev, w1k, w1v, w2k, w2vt)


def _nsa_kernel(q_ref, kc_ref, vct_ref, ks_ref, kw_ref, vst_ref, vwt_ref, smt_ref, o_ref, bias_scr, sx_scr, sy_scr):
    c = pl.program_id(1)
    t0 = c * QT
    ncmp = kc_ref.shape[0]
    nsel = bias_scr.shape[0]
    nw = WINDOW // KC + 1
    gw = NSA_HPG * QT
    width = NSA_KV * gw

    def per_group(x):
        return [x[:, g * gw:(g + 1) * gw] for g in range(NSA_KV)]

    def pv(vt, p):
        rows = vt.shape[0] // NSA_KV
        pb = p.astype(BF16)
        return jnp.concatenate([_dot(vt[g * rows:(g + 1) * rows, :], pg) for g, pg in enumerate(per_group(pb))],
                               axis=1)

    def normalised(acc):
        return acc[0:NSA_HD, :] / acc[NSA_HD:NSA_HD + 1, :]

    low_half = lax.broadcasted_iota(jnp.int32, (1, LANES), 1) < NSA_HD
    q_heads = []
    for h in range(NSA_HEADS):
        pair = q_ref[:, (h // 2) * LANES:(h // 2 + 1) * LANES].astype(F32)
        want_low = h // NSA_HPG == 0
        if (h % 2 == 0) != want_low:
            pair = pltpu.roll(pair, NSA_HD, 1)
        q_heads.append(jnp.where(low_half if want_low else ~low_half, pair, 0.0).astype(BF16))
    qs = jnp.concatenate(q_heads, axis=0)
    u_row = lax.broadcasted_iota(jnp.int32, (1, width), 1) % QT
    t_row = t0 + u_row
    r_kc = lax.broadcasted_iota(jnp.int32, (KC, 1), 0)

    n_grp = ks_ref.shape[0] // SEL_GROUP
    n_full = lax.shift_right_logical(t0, int(math.log2(SEL_GROUP)))

    def qk_group(j):
        return _dot_nt(ks_ref[pl.ds(pl.multiple_of(j * SEL_GROUP, SEL_GROUP), SEL_GROUP), :], qs)

    sc = _dot_nt(kc_ref[...], qs)

    w_slabs, w_chunks = [], []
    for i in range(nw):
        jj = c - (nw - 1) + i
        jc = jnp.maximum(jj, 0)
        si = _dot_nt(kw_ref[pl.ds(pl.multiple_of(jc * KC, KC), KC), :], qs)
        if i == 0:
            keep = (r_kc > u_row) & (jj >= 0)
        elif i == nw - 1:
            keep = r_kc <= u_row
        else:
            keep = jj >= 0
        w_slabs.append(jnp.where(keep, si, NEG))
        w_chunks.append(jc)

    n_col = lax.broadcasted_iota(jnp.int32, (ncmp, 1), 0)
    cmask = (CMP_STRIDE * n_col + (CMP_LEN - 1) <= t_row) & (n_col < ncmp - 1)
    s = jnp.where(cmask, sc, NEG)
    m = jnp.max(s, axis=0, keepdims=True)
    e = jnp.exp2(s - m)
    anyv = (t_row >= CMP_LEN - 1).astype(F32)
    p = e * (anyv / jnp.sum(e, axis=0, keepdims=True))
    o_cmp = pv(vct_ref[...], p)

    psums = []
    for pg in per_group(p):
        acc_p = pg[:, 0:QT]
        for h in range(1, NSA_HPG):
            acc_p = acc_p + pg[:, h * QT:(h + 1) * QT]
        psums.append(acc_p)
    psum = jnp.concatenate(psums, axis=1)
    nq2 = NSA_KV * QT
    s_col = lax.broadcasted_iota(jnp.int32, (nsel, 1), 0)
    n_lane = lax.broadcasted_iota(jnp.int32, (1, ncmp), 1)
    ov = ((CMP_STRIDE * n_lane < SEL_BLOCK * (s_col + 1)) & (CMP_STRIDE * n_lane + (CMP_LEN - 1) >= SEL_BLOCK * s_col)
          ).astype(BF16)
    imp = sum(_dot(ov, part) for part in _split3(psum))

    s_diag = qk_group(n_full)
    sx_scr[...] = qk_group(0)

    mxw = w_slabs[0]
    for sl in w_slabs[1:]:
        mxw = jnp.maximum(mxw, sl)
    mw = jnp.max(mxw, axis=0, keepdims=True)
    acc_w = jnp.zeros((VT_ROWS, width), F32)
    for sl, jc in zip(w_slabs, w_chunks):
        acc_w = acc_w + pv(vwt_ref[jc], jnp.exp2(sl - mw))
    o_win = normalised(acc_w)

    t1 = t0 + lax.broadcasted_iota(jnp.int32, (1, nq2), 1) % QT
    cur = lax.shift_right_logical(t1, 6)
    forced = (s_col == 0) | (s_col == cur) | (s_col == cur - 1)
    valid = SEL_BLOCK * s_col <= t1
    val = jnp.where(valid, jnp.where(forced, imp + SEL_FORCE, imp), NEG)
    sub = 8
    r_sub = lax.broadcasted_iota(jnp.int32, (sub, 1), 0)
    blocks = [val[r * sub:(r + 1) * sub, :] for r in range(nsel // sub)]
    ranks = [jnp.zeros((sub, nq2), F32) for _ in blocks]
    for i in range(nsel):
        vi = val[i:i + 1, :]
        for r, blk in enumerate(blocks):
            if i < r * sub:
                beats = vi >= blk
            elif i >= (r + 1) * sub:
                beats = vi > blk
            else:
                beats = (vi > blk) | ((vi == blk) & (r_sub > i - r * sub))
            ranks[r] = ranks[r] + jnp.where(beats, 1.0, 0.0)
    bias = jnp.where(jnp.concatenate(ranks, axis=0) < SEL_TOPK, 0.0, NEG).astype(F32)
    bias_scr[...] = jnp.concatenate([bias[:, g * QT:(g + 1) * QT] for g in range(NSA_KV) for _ in range(NSA_HPG)],
                                    axis=1)

    r_blk = lax.broadcasted_iota(jnp.int32, (SEL_BLOCK, 1), 0)
    blk_per_grp = SEL_GROUP // SEL_BLOCK
    chunk_per_grp = SEL_GROUP // KC
    blk_per_chunk = KC // SEL_BLOCK

    def sel_update(j, sj, carry, causal=False, live=None):
        m_o, acc = carry
        brows, keeps = [], []
        for i in range(blk_per_grp):
            brow = bias_scr[pl.ds(blk_per_grp * j + i, 1), :]
            brows.append(brow if live is None else jnp.where(live, brow, NEG))
            keeps.append(j * SEL_GROUP + i * SEL_BLOCK + r_blk <= t_row if causal else None)

        def block(i):
            return sj[i * SEL_BLOCK:(i + 1) * SEL_BLOCK, :]

        mx = None
        for i in range(blk_per_grp):
            sl = block(i) + brows[i]
            if causal:
                sl = jnp.where(keeps[i], sl, NEG)
            mx = sl if mx is None else jnp.maximum(mx, sl)
        m_n = jnp.maximum(m_o, jnp.max(mx, axis=0, keepdims=True))
        a = jnp.exp2(m_o - m_n)
        acc = a * acc
        for ci in range(chunk_per_grp):
            parts = []
            for i in range(blk_per_chunk * ci, blk_per_chunk * (ci + 1)):
                arg = block(i) + (brows[i] - m_n)
                parts.append(jnp.exp2(jnp.where(keeps[i], arg, NEG) if causal else arg))
            acc = acc + pv(vst_ref[chunk_per_grp * j + ci], jnp.concatenate(parts, axis=0))
        return m_n, acc

    empty = (jnp.full((1, width), NEG, F32), jnp.zeros((VT_ROWS, width), F32))
    seeded = sel_update(n_full, s_diag, empty, causal=True)

    def pair_body(jp, carry):
        ja, jb = 2 * jp, 2 * jp + 1
        sy_scr[...] = qk_group(jb)
        carry = sel_update(ja, sx_scr, carry)
        sx_scr[...] = qk_group(jnp.minimum(ja + 2, n_grp - 1))
        return sel_update(jb, sy_scr, carry)

    n_pairs = lax.shift_right_logical(n_full, 1)
    carry = lax.fori_loop(0, n_pairs, pair_body, seeded)
    _, acc_s = lax.cond(n_full - 2 * n_pairs == 1, lambda cr: sel_update(n_full - 1, sx_scr, cr), lambda cr: cr,
                        carry)
    o_sel = normalised(acc_s)

    def gate_row(br):
        rows = [smt_ref[SM_GATE + 3 * h + br:SM_GATE + 3 * h + br + 1, :] for h in range(NSA_HEADS)]
        return _sigmoid(jnp.concatenate(rows, axis=1))

    o_t = gate_row(0) * o_cmp + gate_row(1) * o_sel + gate_row(2) * o_win
    for pr in range(NSA_HEADS // 2):
        xp = jnp.concatenate([o_t[:, (2 * pr) * QT:(2 * pr + 1) * QT], o_t[:, (2 * pr + 1) * QT:(2 * pr + 2) * QT]], axis=0)
        o_ref[:, pr * LANES:(pr + 1) * LANES] = xp.T.astype(BF16)


def _nsa(qpad, kcb, vct, ks, kw, vst, vwt, smt, b, s):
    nq = s // QT
    ncmp = kcb.shape[1]
    return pl.pallas_call(
        _nsa_kernel,
        out_shape=jax.ShapeDtypeStruct((b * s, NSA_HEADS * NSA_HD), BF16),
        grid=(b, nq),
        in_specs=[
            pl.BlockSpec((QT, NSA_HEADS * NSA_HD), lambda bi, c: (bi * nq + c, 0)),
            pl.BlockSpec((None, ncmp, LANES), lambda bi, c: (bi, 0, 0)),
            pl.BlockSpec((None, NSA_KV * NSA_HD, ncmp), lambda bi, c: (bi, 0, 0)),
            pl.BlockSpec((s, LANES), lambda bi, c: (bi, 0)),
            pl.BlockSpec((s, LANES), lambda bi, c: (bi, 0)),
            pl.BlockSpec((s // KC, NSA_KV * VT_ROWS, KC), lambda bi, c: (bi, 0, 0)),
            pl.BlockSpec((s // KC, NSA_KV * VT_ROWS, KC), lambda bi, c: (bi, 0, 0)),
            pl.BlockSpec((LANES, QT), lambda bi, c: (0, bi * nq + c)),
        ],
        out_specs=pl.BlockSpec((QT, NSA_HEADS * NSA_HD), lambda bi, c: (bi * nq + c, 0)),
        scratch_shapes=[pltpu.VMEM((s // SEL_BLOCK, NSA_HEADS * QT), F32),
                        pltpu.VMEM((SEL_GROUP, NSA_HEADS * QT), F32),
                        pltpu.VMEM((SEL_GROUP, NSA_HEADS * QT), F32)],
        compiler_params=pltpu.CompilerParams(dimension_semantics=("arbitrary", "arbitrary"),
                                             vmem_limit_bytes=VMEM_LIMIT),
        name="nsa",
    )(qpad, kcb, vct, ks, kw, vst, vwt, smt)


def _log_sigmoid(x):
    return jnp.minimum(x, 0.0) - jnp.log(1.0 + jnp.exp(-jnp.abs(x)))


def _mlstm_kernel(qk_ref, v_ref, og_ref, sm_ref, smt_ref, wc_ref, bc_ref, bifc_ref, bifr_ref, ghn_ref,
                  y_ref, ext_scr, ct_scr, n_scr, m_scr):
    lc = ML_CHUNK

    @pl.when(pl.program_id(1) == 0)
    def _():
        ext_scr[0:8, :] = jnp.zeros((8, 2 * ML_WIDTH), F32)
        ct_scr[...] = jnp.zeros_like(ct_scr)
        n_scr[...] = jnp.zeros_like(n_scr)
        m_scr[...] = jnp.zeros_like(m_scr)

    ext_scr[8:8 + ML_BLOCK, :] = qk_ref[...].astype(F32)
    y = bc_ref[...]
    for j in range(CONV_W):
        y = y + wc_ref[j:j + 1, :] * ext_scr[pl.ds(8 - (CONV_W - 1) + j, ML_BLOCK), :]
    ext_scr[0:8, :] = ext_scr[ML_BLOCK:ML_BLOCK + 8, :]
    qkc = y * _sigmoid(y)
    q_all = qkc[:, 0:ML_WIDTH].astype(BF16)
    k_all = (qkc[:, ML_WIDTH:2 * ML_WIDTH] * (ML_HD ** -0.5)).astype(BF16)

    ifc = sm_ref[...] + bifc_ref[...]
    ifr = smt_ref[...] + bifr_ref[...]
    lfc = _log_sigmoid(ifc)
    lfr = _log_sigmoid(ifr)
    rr = lax.broadcasted_iota(jnp.int32, (lc, lc), 0)
    cc = lax.broadcasted_iota(jnp.int32, (lc, lc), 1)
    causal = rr >= cc
    tri_l = causal.astype(F32)
    tri_u = (rr <= cc).astype(F32)

    for ci in range(ML_BLOCK // lc):
        lo, hi = ci * lc, (ci + 1) * lc
        bc_all = jnp.dot(tri_l, lfc[lo:hi, :], preferred_element_type=F32, precision=lax.Precision.HIGHEST)
        br_all = jnp.dot(lfr[:, lo:hi], tri_u, preferred_element_type=F32, precision=lax.Precision.HIGHEST)
        heads = range(ML_HEADS)
        hsl = [slice(h * ML_HD, (h + 1) * ML_HD) for h in heads]
        bcol = [bc_all[:, SM_F + h:SM_F + h + 1] for h in heads]
        brow = [br_all[SM_F + h:SM_F + h + 1, :] for h in heads]
        icol = [ifc[lo:hi, SM_I + h:SM_I + h + 1] for h in heads]
        irow = [ifr[SM_I + h:SM_I + h + 1, lo:hi] for h in heads]
        mprev = [m_scr[h][:, 0:1] for h in heads]
        qh = [q_all[lo:hi, hsl[h]] for h in heads]
        kh = [k_all[lo:hi, hsl[h]] for h in heads]
        vh = [v_ref[lo:hi, hsl[h]] for h in heads]
        ct = [ct_scr[h] for h in heads]
        nrow = [n_scr[h] for h in heads]
        qk = [_dot_nt(qh[h], kh[h]) for h in heads]
        qc = [_dot(qh[h], ct[h].astype(BF16)) for h in heads]
        dmat = [jnp.where(causal, bcol[h] - brow[h] + irow[h], NEG) for h in heads]
        inter = [bcol[h] + mprev[h] for h in heads]
        mt = [jnp.maximum(jnp.max(dmat[h], axis=-1, keepdims=True), inter[h]) for h in heads]
        a = [jnp.exp(dmat[h] - mt[h]) * qk[h] for h in heads]
        dec = [jnp.exp(inter[h] - mt[h]) for h in heads]
        num = [_dot(a[h].astype(BF16), vh[h]) + dec[h] * qc[h] for h in heads]
        den = [jnp.sum(a[h], axis=-1, keepdims=True)
               + dec[h] * jnp.sum(qh[h].astype(F32) * nrow[h], axis=-1, keepdims=True) for h in heads]
        blast = [bcol[h][lc - 1:lc, :] for h in heads]
        mnew = [jnp.maximum(blast[h] + mprev[h], jnp.max(blast[h] - brow[h] + irow[h], axis=-1, keepdims=True))
                for h in heads]
        wprev = [jnp.exp(blast[h] + mprev[h] - mnew[h]) for h in heads]
        kwt = [kh[h].astype(F32) * jnp.exp(blast[h] - bcol[h] + icol[h] - mnew[h]) for h in heads]
        for h in heads:
            ct_scr[h] = wprev[h] * ct[h] + lax.dot_general(kwt[h].astype(BF16), vh[h], _TN,
                                                           preferred_element_type=F32)
            n_scr[h] = wprev[h] * nrow[h] + jnp.sum(kwt[h], axis=0, keepdims=True)
            m_scr[h] = jnp.broadcast_to(mnew[h], (1, LANES))
        hm = [num[h] / jnp.maximum(jnp.abs(den[h]), jnp.exp(-mt[h])) * _sigmoid(og_ref[lo:hi, hsl[h]].astype(F32))
              for h in heads]
        for h in heads:
            y_ref[lo:hi, hsl[h]] = _rms(hm[h], ghn_ref[:, hsl[h]]).astype(BF16)


def _mlstm(qkb, vb, ob, sm, smt, wconv, bconv, bifc, bifr, ghn, b, s):
    nb = s // ML_BLOCK
    row = lambda w: pl.BlockSpec((ML_BLOCK, w), lambda bi, j: (bi * nb + j, 0))
    return pl.pallas_call(
        _mlstm_kernel,
        out_shape=jax.ShapeDtypeStruct((b * s, ML_WIDTH), BF16),
        grid=(b, nb),
        in_specs=[row(2 * ML_WIDTH), row(ML_WIDTH), row(ML_WIDTH), row(LANES),
                  pl.BlockSpec((LANES, ML_BLOCK), lambda bi, j: (0, bi * nb + j)),
                  _const_spec(wconv.shape), _const_spec(bconv.shape), _const_spec(bifc.shape),
                  _const_spec(bifr.shape), _const_spec(ghn.shape)],
        out_specs=row(ML_WIDTH),
        scratch_shapes=[pltpu.VMEM((ML_BLOCK + 8, 2 * ML_WIDTH), F32),
                        pltpu.VMEM((ML_HEADS, ML_HD, ML_HD), F32),
                        pltpu.VMEM((ML_HEADS, 1, ML_HD), F32),
                        pltpu.VMEM((ML_HEADS, 1, LANES), F32)],
        compiler_params=pltpu.CompilerParams(dimension_semantics=("arbitrary", "arbitrary"),
                                             vmem_limit_bytes=VMEM_LIMIT),
        name="mlstm",
    )(qkb, vb, ob, sm, smt, wconv, bconv, bifc, bifr, ghn)


RT_BUCKET = N_EXPERTS
RT_RANK = N_EXPERTS + 1
RT_WLO = N_EXPERTS + 2
RT_WHI = N_EXPERTS + 3
N_BUCKETS = N_GROUPS * 6
X_ROWS = D_MODEL // LANES
REC = 2 * X_ROWS
DMA_UNROLL = 8
DISPATCH_SUB = 32


def _merge_kernel(ya_ref, yb_ref, mg_ref, x_ref, wpa_ref, wpb_ref, wout_ref, gffn_ref, wr_ref, br_ref,
                  x1_ref, h2_ref, slab_ref, hist_ref):
    halves = [slice(i * (TD // 2), (i + 1) * (TD // 2)) for i in range(2)]
    pa = [_dot(ya_ref[hs, :], wpa_ref[...]) for hs in halves]
    pb = [_dot(yb_ref[hs, :], wpb_ref[...]) for hs in halves]
    mixed = [(_sigmoid(mg_ref[hs, 0:D_MODEL].astype(F32)) * pa[i]
              + _sigmoid(mg_ref[hs, D_MODEL:2 * D_MODEL].astype(F32)) * pb[i]).astype(BF16)
             for i, hs in enumerate(halves)]
    x1 = [x_ref[hs, :] + _dot(mixed[i], wout_ref[...]) for i, hs in enumerate(halves)]
    h2 = [_rms(x1[i], gffn_ref[...]) for i in range(2)]
    for i, hs in enumerate(halves):
        x1_ref[hs, :] = x1[i]
        h2_ref[hs, :] = h2[i]

    h_hi = [h.astype(BF16) for h in h2]
    h_lo = [(h2[i] - h_hi[i].astype(F32)).astype(BF16) for i in range(2)]
    r_hi = [_dot(h, wr_ref[...]) for h in h_hi]
    logit = jnp.concatenate([r_hi[i][:, 0:LANES] + r_hi[i][:, LANES:2 * LANES] + _dot(h_lo[i], wr_ref[:, 0:LANES])
                             for i in range(2)], axis=0) + br_ref[...]
    lane = lax.broadcasted_iota(jnp.int32, logit.shape, 1)
    big = jnp.int32(LANES)
    gmask = (lane >= N_EXPERTS) & (lane < N_EXPERTS + N_GROUPS)
    gl = jnp.where(gmask, logit, NEG)
    gmax = jnp.max(gl, axis=-1, keepdims=True)
    gidx = jnp.min(jnp.where(gmask & (gl == gmax), lane, big), axis=-1, keepdims=True) - N_EXPERTS
    pg_sel = 1.0 / jnp.sum(jnp.where(gmask, jnp.exp(gl - gmax), 0.0), axis=-1, keepdims=True)
    emask = (lane < N_EXPERTS) & (lax.shift_right_logical(lane, 2) == gidx)
    el = jnp.where(emask, logit, NEG)
    e1 = jnp.max(el, axis=-1, keepdims=True)
    i1 = jnp.min(jnp.where(emask & (el == e1), lane, big), axis=-1, keepdims=True)
    emask2 = emask & (lane != i1)
    el2 = jnp.where(emask2, logit, NEG)
    e2 = jnp.max(el2, axis=-1, keepdims=True)
    i2 = jnp.min(jnp.where(emask2 & (el2 == e2), lane, big), axis=-1, keepdims=True)
    x21 = jnp.exp(e2 - e1)
    w1 = pg_sel / (1.0 + x21)
    w2 = pg_sel * x21 / (1.0 + x21)
    first_lo = i1 < i2
    e_lo = jnp.where(first_lo, i1, i2) - EXP_PER_GROUP * gidx
    e_hi = jnp.where(first_lo, i2, i1) - EXP_PER_GROUP * gidx
    pair = lax.shift_right_logical(e_lo * (2 * EXP_PER_GROUP - 1 - e_lo), 1) + (e_hi - e_lo - 1)
    bucket = 6 * gidx + pair
    member = lane == bucket
    onehot = jnp.where(member, 1.0, 0.0)
    rr = lax.broadcasted_iota(jnp.int32, (TD, TD), 0)
    cc = lax.broadcasted_iota(jnp.int32, (TD, TD), 1)
    earlier = _dot((rr > cc).astype(BF16), onehot.astype(BF16))
    rank = jnp.sum(jnp.where(member, earlier, 0.0), axis=-1, keepdims=True)
    slab = jnp.where(lane == i1, w1, 0.0) + jnp.where(lane == i2, w2, 0.0)
    slab = jnp.where(lane == RT_BUCKET, bucket.astype(F32), slab)
    slab = jnp.where(lane == RT_RANK, rank, slab)
    slab = jnp.where(lane == RT_WLO, jnp.where(first_lo, w1, w2), slab)
    slab = jnp.where(lane == RT_WHI, jnp.where(first_lo, w2, w1), slab)
    slab_ref[...] = slab
    hist_ref[...] = jnp.broadcast_to(jnp.sum(onehot, axis=0, keepdims=True), (8, LANES))


def _merge(ya, yb, mg, x2, wpa, wpb, wout, gffn, wr, br):
    t = x2.shape[0]
    row = lambda w: pl.BlockSpec((TD, w), lambda i: (i, 0))
    return pl.pallas_call(
        _merge_kernel,
        out_shape=[jax.ShapeDtypeStruct((t, D_MODEL), F32),
                   jax.ShapeDtypeStruct((t, D_MODEL), F32),
                   jax.ShapeDtypeStruct((t, LANES), F32),
                   jax.ShapeDtypeStruct((t // TD * 8, LANES), F32)],
        grid=(t // TD,),
        in_specs=[row(NSA_HEADS * NSA_HD), row(ML_WIDTH), row(2 * D_MODEL), row(D_MODEL)]
                 + [_const_spec(a.shape) for a in (wpa, wpb, wout, gffn, wr, br)],
        out_specs=[row(D_MODEL), row(D_MODEL), row(LANES),
                   pl.BlockSpec((8, LANES), lambda i: (i, 0))],
        compiler_params=pltpu.CompilerParams(dimension_semantics=("arbitrary",), vmem_limit_bytes=VMEM_LIMIT),
        name="merge",
    )(ya, yb, mg, x2, wpa, wpb, wout, gffn, wr, br)


def _rec_copy(src_ref, src_tok, dst_ref, dst_tok, sem, rows):
    src = src_ref.at[pl.ds(pl.multiple_of(src_tok * rows, rows), rows), :]
    dst = dst_ref.at[pl.ds(pl.multiple_of(dst_tok * rows, rows), rows), :]
    return pltpu.make_async_copy(src, dst, sem)


def _token_copies(n, make, wait=False):
    def body(g, carry):
        for u in range(DMA_UNROLL):
            cp = make(g * DMA_UNROLL + u)
            if wait:
                cp.wait()
            else:
                cp.start(priority=u % 2)
        return carry
    lax.fori_loop(0, n // DMA_UNROLL, body, 0)


def _dispatch_kernel(pos_ref, tail_ref, h2_ref, slab_ref, out_ref, stage_scr, zero_scr, sem, zsem):
    i = pl.program_id(0)
    slot = lax.rem(i, 2)
    n_tiles = out_ref.shape[0] // (TM * REC)
    n_used = tail_ref[2 * N_BUCKETS]

    def zero_copy(first_slot):
        start = pl.multiple_of(first_slot * REC, TM * REC)
        return pltpu.make_async_copy(zero_scr, out_ref.at[pl.ds(start, TM * REC), :], zsem)

    @pl.when(i == 0)
    def _():
        zero_scr[...] = jnp.zeros_like(zero_scr)
        for phase in ("start", "wait"):
            for b in range(N_BUCKETS):
                @pl.when(tail_ref[N_BUCKETS + b] > 0)
                def _():
                    getattr(zero_copy(tail_ref[b]), phase)()

                @pl.when(n_used + b < n_tiles)
                def _():
                    getattr(zero_copy((n_used + b) * TM), phase)()

    stage = stage_scr.at[slot]
    base = i * TM
    for t0 in range(0, TM, DISPATCH_SUB):
        for j in range(REC):
            if j < X_ROWS:
                rows = h2_ref[t0:t0 + DISPATCH_SUB, j * LANES:(j + 1) * LANES]
            elif j == X_ROWS:
                rows = slab_ref[t0:t0 + DISPATCH_SUB, :]
            else:
                rows = jnp.zeros((DISPATCH_SUB, LANES), F32)
            stage[pl.ds(t0 * REC + j, DISPATCH_SUB, stride=REC), :] = rows
        for r in range(t0, t0 + DISPATCH_SUB):
            _rec_copy(stage, r, out_ref, pos_ref[base + r], sem.at[slot], REC).start(priority=r % 2)

    def drain(which):
        _token_copies(TM, lambda r: _rec_copy(stage_scr.at[which], 0, out_ref, 0, sem.at[which], REC), wait=True)

    @pl.when(i > 0)
    def _():
        drain(1 - slot)

    @pl.when(i == pl.num_programs(0) - 1)
    def _():
        drain(slot)


def _dispatch(pos, tail, h2, slab, n_slots):
    t = h2.shape[0]
    return pl.pallas_call(
        _dispatch_kernel,
        out_shape=jax.ShapeDtypeStruct((n_slots * REC, LANES), F32),
        grid_spec=pltpu.PrefetchScalarGridSpec(
            num_scalar_prefetch=2,
            grid=(t // TM,),
            in_specs=[pl.BlockSpec((TM, D_MODEL), lambda i, pos_r, tail_r: (i, 0)),
                      pl.BlockSpec((TM, LANES), lambda i, pos_r, tail_r: (i, 0))],
            out_specs=pl.BlockSpec(memory_space=pl.ANY),
            scratch_shapes=[pltpu.VMEM((2, TM * REC, LANES), F32), pltpu.VMEM((TM * REC, LANES), F32),
                            pltpu.SemaphoreType.DMA((2,)), pltpu.SemaphoreType.DMA(())],
        ),
        compiler_params=pltpu.CompilerParams(dimension_semantics=("arbitrary",), vmem_limit_bytes=VMEM_LIMIT,
                                             has_side_effects=True),
        name="dispatch",
    )(pos, tail, h2, slab)


MOE_TILES = 2


def _moe_kernel(te_ref, nu_ref, hx_ref, w13_ref, w2_ref, y_ref):
    step = pl.program_id(0)
    n_tiles = pl.num_programs(0) * MOE_TILES
    n_used = nu_ref[0]

    @pl.when(step * MOE_TILES < n_used)
    def _():
        subs = range(MOE_TILES)
        hs = [jnp.concatenate([hx_ref[pl.ds(sub * TM * REC + j, TM, stride=REC), :] for j in range(X_ROWS)],
                              axis=1).astype(BF16) for sub in subs]
        slabs = [hx_ref[pl.ds(sub * TM * REC + X_ROWS, TM, stride=REC), :] for sub in subs]
        ys = [None] * MOE_TILES
        for side, lane in ((0, RT_WLO), (1, RT_WHI)):
            es = [te_ref[side * n_tiles + step * MOE_TILES + sub] for sub in subs]
            up = [_dot(hs[sub], w13_ref[es[sub]]) for sub in subs]
            act = [(up[sub][:, 0:D_EXPERT] * _sigmoid(up[sub][:, 0:D_EXPERT]) * up[sub][:, D_EXPERT:2 * D_EXPERT]
                    * slabs[sub][:, lane:lane + 1]).astype(BF16) for sub in subs]
            for sub in subs:
                part = _dot(act[sub], w2_ref[es[sub]])
                ys[sub] = part if ys[sub] is None else ys[sub] + part
        for sub in subs:
            for j in range(X_ROWS):
                y_ref[pl.ds(sub * TM * X_ROWS + j, TM, stride=X_ROWS), :] = ys[sub][:, j * LANES:(j + 1) * LANES]

    @pl.when(step * MOE_TILES >= n_used)
    def _():
        y_ref[...] = jnp.zeros_like(y_ref)


def _moe(tile_e, n_used, hx_sorted, w13, w2):
    n_tiles = hx_sorted.shape[0] // (TM * REC)
    rows = MOE_TILES * TM
    last = lambda nu: (nu[0] - 1) // MOE_TILES
    return pl.pallas_call(
        _moe_kernel,
        out_shape=jax.ShapeDtypeStruct((n_tiles * TM * X_ROWS, LANES), F32),
        grid_spec=pltpu.PrefetchScalarGridSpec(
            num_scalar_prefetch=2,
            grid=(n_tiles // MOE_TILES,),
            in_specs=[pl.BlockSpec((rows * REC, LANES), lambda k, te, nu: (jnp.minimum(k, last(nu)), 0)),
                      pl.BlockSpec(w13.shape, lambda k, te, nu: (0, 0, 0), pipeline_mode=pl.Buffered(1)),
                      pl.BlockSpec(w2.shape, lambda k, te, nu: (0, 0, 0), pipeline_mode=pl.Buffered(1))],
            out_specs=pl.BlockSpec((rows * X_ROWS, LANES), lambda k, te, nu: (k, 0)),
        ),
        compiler_params=pltpu.CompilerParams(dimension_semantics=("arbitrary",), vmem_limit_bytes=VMEM_LIMIT),
        name="moe",
    )(tile_e, n_used, hx_sorted, w13, w2)


def _combine_kernel(pos_ref, y_ref, x1_ref, p_ref, gple_ref, wpg_ref, wpp_ref, gfin_ref, o_ref, ybuf, sem):
    i = pl.program_id(0)
    slot = lax.rem(i, 2)

    def gather(tile, which):
        _token_copies(TD, lambda r: _rec_copy(y_ref, pos_ref[tile * TD + r], ybuf.at[which], r, sem.at[which],
                                              X_ROWS))

    @pl.when(i == 0)
    def _():
        gather(0, 0)

    @pl.when(i + 1 < pl.num_programs(0))
    def _():
        gather(i + 1, 1 - slot)

    _token_copies(TD, lambda r: _rec_copy(y_ref, 0, ybuf.at[slot], 0, sem.at[slot], X_ROWS), wait=True)
    yb = ybuf.at[slot]
    x2 = x1_ref[...] + jnp.concatenate([yb[pl.ds(j, TD, stride=X_ROWS), :] for j in range(X_ROWS)], axis=1)
    h3 = _rms(x2, gple_ref[...]).astype(BF16)
    x3 = x2 + _sigmoid(_dot(h3, wpg_ref[...])) * _dot(p_ref[...].astype(BF16), wpp_ref[...])
    o_ref[...] = _rms(x3, gfin_ref[...])


def _combine(pos, y_sorted, x1, p2, gple, wpg, wpp, gfin):
    t = x1.shape[0]
    row = lambda w: pl.BlockSpec((TD, w), lambda i, pos_r: (i, 0))
    const = lambda a: pl.BlockSpec(a.shape, lambda i, pos_r: (0,) * a.ndim, pipeline_mode=pl.Buffered(1))
    return pl.pallas_call(
        _combine_kernel,
        out_shape=jax.ShapeDtypeStruct((t, D_MODEL), F32),
        grid_spec=pltpu.PrefetchScalarGridSpec(
            num_scalar_prefetch=1,
            grid=(t // TD,),
            in_specs=[pl.BlockSpec(memory_space=pl.ANY), row(D_MODEL), row(PLE_DIM),
                      const(gple), const(wpg), const(wpp), const(gfin)],
            out_specs=row(D_MODEL),
            scratch_shapes=[pltpu.VMEM((2, TD * X_ROWS, LANES), F32), pltpu.SemaphoreType.DMA((2,))],
        ),
        compiler_params=pltpu.CompilerParams(dimension_semantics=("arbitrary",), vmem_limit_bytes=VMEM_LIMIT),
        name="combine",
    )(pos, y_sorted, x1, p2, gple, wpg, wpp, gfin)


def _routing_tables(slab, hist8):
    t = slab.shape[0]
    nt = t // TD
    n_tiles = t // TM + N_BUCKETS
    hist = hist8.reshape(nt, 8, LANES)[:, 0, :]
    counts = jnp.sum(hist, axis=0)
    padded = jnp.ceil(counts / TM) * TM
    ends = jnp.cumsum(padded)
    first = (ends - padded)[None, :] + jnp.cumsum(hist, axis=0) - hist
    lane = jnp.arange(LANES, dtype=F32)[None, :]
    mine = lane == slab[:, RT_BUCKET:RT_BUCKET + 1]
    pos = jnp.sum(jnp.where(mine, jnp.repeat(first, TD, axis=0), 0.0), axis=1) + slab[:, RT_RANK]
    starts = jnp.arange(n_tiles, dtype=F32) * TM
    tile_bucket = jnp.minimum(jnp.sum(ends[None, :N_BUCKETS] <= starts[:, None], axis=1), N_BUCKETS - 1)
    group, pair = tile_bucket // 6, tile_bucket % 6
    e_lo = EXP_PER_GROUP * group + jnp.array([0, 0, 0, 1, 1, 2], jnp.int32)[pair]
    e_hi = EXP_PER_GROUP * group + jnp.array([1, 2, 3, 2, 3, 3], jnp.int32)[pair]
    tile_e = jnp.concatenate([e_lo, e_hi]).astype(jnp.int32)
    n_used = (ends[N_BUCKETS - 1] / TM).astype(jnp.int32).reshape(1)
    tail = jnp.concatenate([(ends - TM)[:N_BUCKETS], padded[:N_BUCKETS], n_used.astype(F32)]).astype(jnp.int32)
    return pos.astype(jnp.int32), tile_e, n_used, tail, n_tiles * TM


def _pack_inproj_weights(w):
    d = w.shape[0]
    qw = NSA_HEADS * NSA_HD
    kvw = NSA_KV * NSA_HD
    o = 0
    wq = w[:, o:o + qw]; o += qw
    wkc = w[:, o:o + kvw]; o += kvw
    wvc = w[:, o:o + kvw]; o += kvw
    wks = w[:, o:o + kvw]; o += kvw
    wvs = w[:, o:o + kvw]; o += kvw
    wkw = w[:, o:o + kvw]; o += kvw
    wvw = w[:, o:o + kvw]; o += kvw
    wga = w[:, o:o + 3 * NSA_HEADS]; o += 3 * NSA_HEADS
    wqkb = w[:, o:o + 2 * ML_WIDTH]; o += 2 * ML_WIDTH
    wvb = w[:, o:o + ML_WIDTH]; o += ML_WIDTH
    wob = w[:, o:o + ML_WIDTH]; o += ML_WIDTH
    wif = w[:, o:o + 2 * ML_HEADS]; o += 2 * ML_HEADS
    wmg = w[:, o:o + 2 * D_MODEL]
    wsm = jnp.concatenate([wga, wif, jnp.zeros((d, LANES - 3 * NSA_HEADS - 2 * ML_HEADS), w.dtype)], axis=1)
    wcat = jnp.concatenate([wq, wkc, wvc, wks, wkw, wsm, wqkb, wvb, wob, wmg], axis=1).astype(BF16)
    wtr = jnp.concatenate([wvs, wvw, wsm], axis=1).T.astype(BF16)
    return wcat, wtr


def _rope_lane_tables(positions):
    inv = ROPE_THETA ** (-jnp.arange(0, ROPE_DIM, 2, dtype=F32) / ROPE_DIM)
    ang = positions.astype(F32).reshape(-1, 1) * inv[None, :]
    cos, sin = jnp.cos(ang), jnp.sin(ang)
    half = ROPE_DIM // 2
    d = jnp.arange(LANES) % NSA_HD
    cos_l, sin_l = jnp.tile(cos, (1, LANES // half)), jnp.tile(sin, (1, LANES // half))
    rc = jnp.where(d < ROPE_DIM, cos_l, 1.0)
    rp = jnp.where((d >= half) & (d < ROPE_DIM), sin_l, 0.0)
    rm = jnp.where(d < half, -sin_l, 0.0)
    return rc, rp, rm


def _pack_compress_weights(w1, w2, pe):
    half = CMP_LEN // 2
    w1r = w1.reshape(2, half, NSA_HD, CMP_HIDDEN)
    outs = []
    for part in range(2):
        wb = w1r[part].astype(BF16)
        zb = jnp.zeros_like(wb)
        wp = jnp.stack([jnp.stack([wb, zb], axis=2), jnp.stack([zb, wb], axis=2)], axis=1)
        outs.append(wp.reshape(half * NSA_KV * NSA_HD, NSA_KV * CMP_HIDDEN))
    pe8 = jnp.broadcast_to(pe.reshape(1, CMP_LEN * NSA_HD), (8, CMP_LEN * NSA_HD)).astype(BF16)
    return outs[0], outs[1], pe8, w1.astype(BF16)


def _stages(x, p, positions, g_mix, w_in, b_if, w_ck1, w_ck2, pe_ck, w_cv1, w_cv2, pe_cv, w_conv, b_conv, g_hn, w_pa, w_pb, w_out, g_ffn, w_rg, b_rg, w_re, b_re, w_e13, w_e2, g_ple, w_pg, w_pp, g_final):
    b, s, d = x.shape
    t = b * s
    rc, rp, rm = _rope_lane_tables(positions)
    assert w_in.shape[0] == 1, "the final norm is fused into the layer's last kernel: single-layer problem only"
    for i in range(w_in.shape[0]):
        x2 = x.reshape(t, d)
        wcat, wtr = _pack_inproj_weights(w_in[i])
        (qpad, kc_tok, vc_tok, ks, kw, vst, vwt, sm, smt, qkb, vb, ob, mg) = _inproj(
            x2, g_mix[i].reshape(1, d), wcat, wtr, rc, rp, rm)
        wka, wkb, pek, w1k = _pack_compress_weights(w_ck1[i], w_ck2[i], pe_ck[i])
        wva, wvb, pev, w1v = _pack_compress_weights(w_cv1[i], w_cv2[i], pe_cv[i])
        zpad = jnp.zeros((CMP_HIDDEN, NSA_HD), F32)
        w2k = jnp.stack([jnp.concatenate([w_ck2[i], zpad], axis=1),
                         jnp.concatenate([zpad, w_ck2[i]], axis=1)]).astype(BF16)
        w2vt = w_cv2[i].T.astype(BF16)
        nrow = s // CMP_STRIDE
        rk = kc_tok.reshape(b, nrow, CMP_STRIDE * LANES)
        rv = vc_tok.reshape(b, nrow, CMP_STRIDE * LANES)
        kcb, vct = _compress(rk, rv, wka, wkb, wva, wvb, pek, pev, w1k, w1v, w2k, w2vt)
        ya = _nsa(qpad, kcb, vct, ks, kw, vst, vwt, smt, b, s)
        bif = b_if[i].astype(F32)
        bifc = jnp.zeros((1, LANES), F32).at[0, SM_I:SM_I + 2 * ML_HEADS].set(bif)
        bifr = bifc.reshape(LANES, 1)
        yb = _mlstm(qkb, vb, ob, sm, smt, w_conv[i], b_conv[i].reshape(1, -1), bifc, bifr,
                    g_hn[i].reshape(1, -1), b, s)
        wr = jnp.concatenate([w_re[i], w_rg[i], jnp.zeros((d, LANES - N_EXPERTS - N_GROUPS), F32)], axis=1)
        wr_hi = wr.astype(BF16)
        wr = jnp.concatenate([wr_hi, (wr - wr_hi.astype(F32)).astype(BF16)], axis=1)
        br =jnp.concatenate([b_re[i], b_rg[i], jnp.zeros((LANES - N_EXPERTS - N_GROUPS,), F32)]).reshape(1, LANES)
        x1, h2, slab, hist8 = _merge(ya, yb, mg, x2, w_pa[i].astype(BF16), w_pb[i].astype(BF16),
                                     w_out[i].astype(BF16), g_ffn[i].reshape(1, d), wr, br)
        pos, tile_e, n_used, tail, n_slots = _routing_tables(slab, hist8)
        hx_sorted = _dispatch(pos, tail, h2, slab, n_slots)
        y_sorted = _moe(tile_e, n_used, hx_sorted, w_e13[i].astype(BF16), w_e2[i].astype(BF16))
        out = _combine(pos, y_sorted, x1, p[i].reshape(t, PLE_DIM), g_ple[i].reshape(1, d), w_pg[i].astype(BF16),
                       w_pp[i].astype(BF16), g_final.reshape(1, d))
        x = out.reshape(b, s, d)
    return dict(out=x, qpad=qpad, ks=ks, kcb=kcb, vct=vct, y_a=ya, y_b=yb, x1=x1, pos=pos)


def kernel(x, p, positions, g_mix, w_in, b_if, w_ck1, w_ck2, pe_ck, w_cv1, w_cv2, pe_cv, w_conv, b_conv, g_hn, w_pa, w_pb, w_out, g_ffn, w_rg, b_rg, w_re, b_re, w_e13, w_e2, g_ple, w_pg, w_pp, g_final):
    return _stages(x, p, positions, g_mix, w_in, b_if, w_ck1, w_ck2, pe_ck, w_cv1, w_cv2, pe_cv, w_conv, b_conv, g_hn,
                   w_pa, w_pb, w_out, g_ffn, w_rg, b_rg, w_re, b_re, w_e13, w_e2, g_ple, w_pg, w_pp, g_final)["out"]
```

```python
import math

import numpy as np
import jax
import jax.numpy as jnp
from jax import lax
from jax.experimental import pallas as pl
from jax.experimental.pallas import tpu as pltpu

F32 = jnp.float32
BF16 = jnp.bfloat16

EPS = 1e-6
NEG = -1e30

D_MODEL = 1024
PLE_DIM = 256
NSA_HEADS = 8
NSA_KV = 2
NSA_HPG = NSA_HEADS // NSA_KV
NSA_HD = 64
CMP_LEN = 32
CMP_STRIDE = 16
CMP_HIDDEN = 256
SEL_BLOCK = 64
SEL_TOPK = 16
SEL_FORCE = 1000.0
WINDOW = 512
ROPE_THETA = 500000.0
ROPE_DIM = NSA_HD // 4
ML_HEADS = 4
ML_HD = 128
ML_WIDTH = ML_HEADS * ML_HD
CONV_W = 4
N_GROUPS = 4
EXP_PER_GROUP = 4
N_EXPERTS = N_GROUPS * EXP_PER_GROUP
D_EXPERT = 256

LANES = 128
QT = 128
KC = 128
SEL_GROUP = 512
VT_PAD = 16
VT_ROWS = NSA_HD + VT_PAD
ML_CHUNK = 128
ML_BLOCK = 256
TD = 512
TM = 256
VMEM_LIMIT = 56 * 1024 * 1024

_NT = (((1,), (1,)), ((), ()))
_TN = (((0,), (0,)), ((), ()))

SM_GATE = 0
SM_I = 3 * NSA_HEADS
SM_F = SM_I + ML_HEADS


def _dot(a, b):
    return jnp.dot(a, b, preferred_element_type=F32)


def _dot_nt(a, b):
    return lax.dot_general(a, b, _NT, preferred_element_type=F32)


def _split3(x):
    hi = x.astype(BF16)
    r1 = x - hi.astype(F32)
    mid = r1.astype(BF16)
    lo = (r1 - mid.astype(F32)).astype(BF16)
    return hi, mid, lo


def _rms(x, g):
    return x * lax.rsqrt(jnp.mean(x * x, axis=-1, keepdims=True) + EPS) * g


def _sigmoid(x):
    return 0.5 + 0.5 * jnp.tanh(0.5 * x)


def _const_spec(shape):
    nd = len(shape)
    return pl.BlockSpec(shape, lambda *_: (0,) * nd, pipeline_mode=pl.Buffered(1))


_C_Q = 0
_C_KC = _C_Q + NSA_HEADS * NSA_HD
_C_VC = _C_KC + LANES
_C_KS = _C_VC + LANES
_C_KW = _C_KS + LANES
_C_SM = _C_KW + LANES
_C_QKB = _C_SM + LANES
_C_VB = _C_QKB + 2 * ML_WIDTH
_C_OB = _C_VB + ML_WIDTH
_C_MG = _C_OB + ML_WIDTH
_C_END = _C_MG + 2 * D_MODEL


def _inproj_kernel(x_ref, g_ref, w_ref, wt_ref, cs_ref, spread_ref, unrot_ref,
                   q_ref, kc_ref, vc_ref, ks_ref, kw_ref, vst_ref, vwt_ref, sm_ref, smt_ref,
                   qkb_ref, vb_ref, ob_ref, mg_ref):
    hn = _rms(x_ref[...], g_ref[...]).astype(BF16)
    tables = sum(_dot(part, spread_ref[...]) for part in _split3(cs_ref[...]))
    rc = tables[:, 0:LANES] + unrot_ref[...]
    rp = tables[:, LANES:2 * LANES]
    rm = tables[:, 2 * LANES:3 * LANES]

    def rope(z):
        return z * rc + pltpu.roll(z, 8, 1) * rp + pltpu.roll(z, LANES - 8, 1) * rm

    scale = NSA_HD ** -0.5 * math.log2(math.e)
    for h in range(NSA_HEADS * NSA_HD // LANES):
        z = _dot(hn, w_ref[:, _C_Q + h * LANES:_C_Q + (h + 1) * LANES])
        q_ref[:, h * LANES:(h + 1) * LANES] = (rope(z) * scale).astype(BF16)
    kc_ref[...] = rope(_dot(hn, w_ref[:, _C_KC:_C_KC + LANES])).astype(BF16)
    vc_ref[...] = _dot(hn, w_ref[:, _C_VC:_C_VC + LANES]).astype(BF16)
    ks_ref[...] = rope(_dot(hn, w_ref[:, _C_KS:_C_KS + LANES])).astype(BF16)
    kw_ref[...] = rope(_dot(hn, w_ref[:, _C_KW:_C_KW + LANES])).astype(BF16)
    sm_ref[...] = _dot(hn, w_ref[:, _C_SM:_C_SM + LANES])
    for c0 in range(0, 2 * ML_WIDTH, 512):
        qkb_ref[:, c0:c0 + 512] = _dot(hn, w_ref[:, _C_QKB + c0:_C_QKB + c0 + 512]).astype(BF16)
    vb_ref[...] = _dot(hn, w_ref[:, _C_VB:_C_VB + ML_WIDTH]).astype(BF16)
    ob_ref[...] = _dot(hn, w_ref[:, _C_OB:_C_OB + ML_WIDTH]).astype(BF16)
    for c0 in range(0, 2 * D_MODEL, 512):
        mg_ref[:, c0:c0 + 512] = _dot(hn, w_ref[:, _C_MG + c0:_C_MG + c0 + 512]).astype(BF16)
    zt = _dot_nt(wt_ref[...], hn)
    ones_rows = (lax.broadcasted_iota(jnp.int32, (VT_PAD, KC), 0) == 0).astype(BF16)
    for i in range(TD // KC):
        for ref, r0 in ((vst_ref, 0), (vwt_ref, LANES)):
            zc = zt[r0:r0 + LANES, i * KC:(i + 1) * KC].astype(BF16)
            ref[i] = jnp.concatenate([piece for g in range(NSA_KV)
                                      for piece in (zc[g * NSA_HD:(g + 1) * NSA_HD, :], ones_rows)], axis=0)
    smt_ref[...] = zt[2 * LANES:3 * LANES, :]


def _inproj(x2, g_mix, wcat, wtr, cs, spread, unrot):
    t = x2.shape[0]
    row = lambda w: pl.BlockSpec((TD, w), lambda i: (i, 0))
    out_shape = [
        jax.ShapeDtypeStruct((t, NSA_HEADS * NSA_HD), BF16),
        jax.ShapeDtypeStruct((t, LANES), BF16),
        jax.ShapeDtypeStruct((t, LANES), BF16),
        jax.ShapeDtypeStruct((t, LANES), BF16),
        jax.ShapeDtypeStruct((t, LANES), BF16),
        jax.ShapeDtypeStruct((t // KC, NSA_KV * VT_ROWS, KC), BF16),
        jax.ShapeDtypeStruct((t // KC, NSA_KV * VT_ROWS, KC), BF16),
        jax.ShapeDtypeStruct((t, LANES), F32),
        jax.ShapeDtypeStruct((LANES, t), F32),
        jax.ShapeDtypeStruct((t, 2 * ML_WIDTH), BF16),
        jax.ShapeDtypeStruct((t, ML_WIDTH), BF16),
        jax.ShapeDtypeStruct((t, ML_WIDTH), BF16),
        jax.ShapeDtypeStruct((t, 2 * D_MODEL), BF16),
    ]
    chunk3 = pl.BlockSpec((TD // KC, NSA_KV * VT_ROWS, KC), lambda i: (i, 0, 0))
    out_specs = [row(NSA_HEADS * NSA_HD), row(LANES), row(LANES), row(LANES), row(LANES), chunk3, chunk3,
                 row(LANES), pl.BlockSpec((LANES, TD), lambda i: (0, i)),
                 row(2 * ML_WIDTH), row(ML_WIDTH), row(ML_WIDTH), row(2 * D_MODEL)]
    return pl.pallas_call(
        _inproj_kernel,
        out_shape=out_shape,
        grid=(t // TD,),
        in_specs=[row(D_MODEL), _const_spec((1, D_MODEL)), _const_spec((D_MODEL, _C_END)),
                  _const_spec((3 * LANES, D_MODEL)), row(cs.shape[1]), _const_spec(spread.shape),
                  _const_spec(unrot.shape)],
        out_specs=out_specs,
        compiler_params=pltpu.CompilerParams(dimension_semantics=("arbitrary",), vmem_limit_bytes=VMEM_LIMIT),
        name="inproj",
    )(x2, g_mix, wcat, wtr, cs, spread, unrot)


def _gelu_tanh(x):
    return 0.5 * x * (1.0 + jnp.tanh(math.sqrt(2.0 / math.pi) * (x + 0.044715 * x * x * x)))


def _compress_kernel(rk_ref, rv_ref, wka_ref, wkb_ref, wva_ref, wvb_ref, pek_ref, pev_ref,
                     w1k_ref, w1v_ref, w2k_ref, w2vt_ref, kc_ref, vct_ref):
    nrow = rk_ref.shape[0]

    def hidden(r_ref, wa_ref, wb_ref, pe_ref, w1_ref):
        r = r_ref[...]
        ha = _dot(r, wa_ref[...])
        hb = _dot(r, wb_ref[...])
        hb = pltpu.roll(hb, nrow - 1, 0)
        c = _dot(pe_ref[...], w1_ref[...])[0:1, :]
        return [_gelu_tanh(ha[:, g * CMP_HIDDEN:(g + 1) * CMP_HIDDEN] + hb[:, g * CMP_HIDDEN:(g + 1) * CMP_HIDDEN] + c).astype(BF16)
                for g in range(NSA_KV)]

    ak = hidden(rk_ref, wka_ref, wkb_ref, pek_ref, w1k_ref)
    kc_ref[...] = (_dot(ak[0], w2k_ref[0]) + _dot(ak[1], w2k_ref[1])).astype(BF16)
    av = hidden(rv_ref, wva_ref, wvb_ref, pev_ref, w1v_ref)
    for g in range(NSA_KV):
        vct_ref[g * NSA_HD:(g + 1) * NSA_HD, :] = _dot_nt(w2vt_ref[...], av[g]).astype(BF16)


def _compress(rk, rv, wka, wkb, wva, wvb, pek, pev, w1k, w1v, w2k, w2vt):
    b, nrow, width = rk.shape
    blk = pl.BlockSpec((None, nrow, width), lambda i: (i, 0, 0))
    return pl.pallas_call(
        _compress_kernel,
        out_shape=[jax.ShapeDtypeStruct((b, nrow, LANES), BF16),
                   jax.ShapeDtypeStruct((b, LANES, nrow), BF16)],
        grid=(b,),
        in_specs=[blk, blk] + [_const_spec(a.shape) for a in (wka, wkb, wva, wvb, pek, pev, w1k, w1v, w2k, w2vt)],
        out_specs=[pl.BlockSpec((None, nrow, LANES), lambda i: (i, 0, 0)),
                   pl.BlockSpec((None, LANES, nrow), lambda i: (i, 0, 0))],
        compiler_params=pltpu.CompilerParams(dimension_semantics=("arbitrary",), vmem_limit_bytes=VMEM_LIMIT),
        name="compress",
    )(rk, rv, wka, wkb, wva, wvb, pek, pev, w1k, w1v, w2k, w2vt)


def _nsa_kernel(q_ref, kc_ref, vct_ref, ks_ref, kw_ref, vst_ref, vwt_ref, smt_ref, o_ref, bias_scr, sx_scr, sy_scr):
    c = pl.program_id(1)
    t0 = c * QT
    ncmp = kc_ref.shape[0]
    nsel = bias_scr.shape[0]
    nw = WINDOW // KC + 1
    gw = NSA_HPG * QT
    width = NSA_KV * gw

    def per_group(x):
        return [x[:, g * gw:(g + 1) * gw] for g in range(NSA_KV)]

    def pv(vt, p):
        rows = vt.shape[0] // NSA_KV
        pb = p.astype(BF16)
        return jnp.concatenate([_dot(vt[g * rows:(g + 1) * rows, :], pg) for g, pg in enumerate(per_group(pb))],
                               axis=1)

    def normalised(acc):
        return acc[0:NSA_HD, :] / acc[NSA_HD:NSA_HD + 1, :]

    low_half = lax.broadcasted_iota(jnp.int32, (1, LANES), 1) < NSA_HD
    q_heads = []
    for h in range(NSA_HEADS):
        pair = q_ref[:, (h // 2) * LANES:(h // 2 + 1) * LANES].astype(F32)
        want_low = h // NSA_HPG == 0
        if (h % 2 == 0) != want_low:
            pair = pltpu.roll(pair, NSA_HD, 1)
        q_heads.append(jnp.where(low_half if want_low else ~low_half, pair, 0.0).astype(BF16))
    qs = jnp.concatenate(q_heads, axis=0)
    u_row = lax.broadcasted_iota(jnp.int32, (1, width), 1) % QT
    t_row = t0 + u_row
    r_kc = lax.broadcasted_iota(jnp.int32, (KC, 1), 0)

    n_grp = ks_ref.shape[0] // SEL_GROUP
    n_full = lax.shift_right_logical(t0, int(math.log2(SEL_GROUP)))

    def qk_group(j):
        return _dot_nt(ks_ref[pl.ds(pl.multiple_of(j * SEL_GROUP, SEL_GROUP), SEL_GROUP), :], qs)

    sc = _dot_nt(kc_ref[...], qs)

    w_slabs, w_chunks = [], []
    for i in range(nw):
        jj = c - (nw - 1) + i
        jc = jnp.maximum(jj, 0)
        si = _dot_nt(kw_ref[pl.ds(pl.multiple_of(jc * KC, KC), KC), :], qs)
        if i == 0:
            keep = (r_kc > u_row) & (jj >= 0)
        elif i == nw - 1:
            keep = r_kc <= u_row
        else:
            keep = jj >= 0
        w_slabs.append(jnp.where(keep, si, NEG))
        w_chunks.append(jc)

    n_col = lax.broadcasted_iota(jnp.int32, (ncmp, 1), 0)
    cmask = (CMP_STRIDE * n_col + (CMP_LEN - 1) <= t_row) & (n_col < ncmp - 1)
    s = jnp.where(cmask, sc, NEG)
    m = jnp.max(s, axis=0, keepdims=True)
    e = jnp.exp2(s - m)
    anyv = (t_row >= CMP_LEN - 1).astype(F32)
    p = e * (anyv / jnp.sum(e, axis=0, keepdims=True))
    o_cmp = pv(vct_ref[...], p)

    psums = []
    for pg in per_group(p):
        acc_p = pg[:, 0:QT]
        for h in range(1, NSA_HPG):
            acc_p = acc_p + pg[:, h * QT:(h + 1) * QT]
        psums.append(acc_p)
    psum = jnp.concatenate(psums, axis=1)
    nq2 = NSA_KV * QT
    s_col = lax.broadcasted_iota(jnp.int32, (nsel, 1), 0)
    n_lane = lax.broadcasted_iota(jnp.int32, (1, ncmp), 1)
    ov = ((CMP_STRIDE * n_lane < SEL_BLOCK * (s_col + 1)) & (CMP_STRIDE * n_lane + (CMP_LEN - 1) >= SEL_BLOCK * s_col)
          ).astype(BF16)
    imp = sum(_dot(ov, part) for part in _split3(psum))

    s_diag = qk_group(n_full)
    sx_scr[...] = qk_group(0)

    mxw = w_slabs[0]
    for sl in w_slabs[1:]:
        mxw = jnp.maximum(mxw, sl)
    mw = jnp.max(mxw, axis=0, keepdims=True)
    acc_w = jnp.zeros((VT_ROWS, width), F32)
    for sl, jc in zip(w_slabs, w_chunks):
        acc_w = acc_w + pv(vwt_ref[jc], jnp.exp2(sl - mw))
    o_win = normalised(acc_w)

    t1 = t0 + lax.broadcasted_iota(jnp.int32, (1, nq2), 1) % QT
    cur = lax.shift_right_logical(t1, 6)
    forced = (s_col == 0) | (s_col == cur) | (s_col == cur - 1)
    valid = SEL_BLOCK * s_col <= t1
    val = jnp.where(valid, jnp.where(forced, imp + SEL_FORCE, imp), NEG)
    sub = 8
    r_sub = lax.broadcasted_iota(jnp.int32, (sub, 1), 0)
    blocks = [val[r * sub:(r + 1) * sub, :] for r in range(nsel // sub)]
    ranks = [jnp.zeros((sub, nq2), F32) for _ in blocks]
    for i in range(nsel):
        vi = val[i:i + 1, :]
        for r, blk in enumerate(blocks):
            if i < r * sub:
                beats = vi >= blk
            elif i >= (r + 1) * sub:
                beats = vi > blk
            else:
                beats = (vi > blk) | ((vi == blk) & (r_sub > i - r * sub))
            ranks[r] = ranks[r] + jnp.where(beats, 1.0, 0.0)
    bias = jnp.where(jnp.concatenate(ranks, axis=0) < SEL_TOPK, 0.0, NEG).astype(F32)
    bias_scr[...] = jnp.concatenate([bias[:, g * QT:(g + 1) * QT] for g in range(NSA_KV) for _ in range(NSA_HPG)],
                                    axis=1)

    r_blk = lax.broadcasted_iota(jnp.int32, (SEL_BLOCK, 1), 0)
    blk_per_grp = SEL_GROUP // SEL_BLOCK
    chunk_per_grp = SEL_GROUP // KC
    blk_per_chunk = KC // SEL_BLOCK

    def sel_update(j, sj, carry, causal=False):
        m_o, acc = carry
        brows, keeps = [], []
        for i in range(blk_per_grp):
            brows.append(bias_scr[pl.ds(blk_per_grp * j + i, 1), :])
            keeps.append(j * SEL_GROUP + i * SEL_BLOCK + r_blk <= t_row if causal else None)

        def block(i):
            return sj[i * SEL_BLOCK:(i + 1) * SEL_BLOCK, :]

        mx = None
        for i in range(blk_per_grp):
            sl = block(i) + brows[i]
            if causal:
                sl = jnp.where(keeps[i], sl, NEG)
            mx = sl if mx is None else jnp.maximum(mx, sl)
        m_n = jnp.maximum(m_o, jnp.max(mx, axis=0, keepdims=True))
        a = jnp.exp2(m_o - m_n)
        acc = a * acc
        for ci in range(chunk_per_grp):
            parts = []
            for i in range(blk_per_chunk * ci, blk_per_chunk * (ci + 1)):
                arg = block(i) + (brows[i] - m_n)
                parts.append(jnp.exp2(jnp.where(keeps[i], arg, NEG) if causal else arg))
            acc = acc + pv(vst_ref[chunk_per_grp * j + ci], jnp.concatenate(parts, axis=0))
        return m_n, acc

    empty = (jnp.full((1, width), NEG, F32), jnp.zeros((VT_ROWS, width), F32))
    seeded = sel_update(n_full, s_diag, empty, causal=True)

    def pair_body(jp, carry):
        ja, jb = 2 * jp, 2 * jp + 1
        sy_scr[...] = qk_group(jb)
        carry = sel_update(ja, sx_scr, carry)
        sx_scr[...] = qk_group(jnp.minimum(ja + 2, n_grp - 1))
        return sel_update(jb, sy_scr, carry)

    n_pairs = lax.shift_right_logical(n_full, 1)
    carry = lax.fori_loop(0, n_pairs, pair_body, seeded)
    _, acc_s = lax.cond(n_full - 2 * n_pairs == 1, lambda cr: sel_update(n_full - 1, sx_scr, cr), lambda cr: cr,
                        carry)
    o_sel = normalised(acc_s)

    def gate_row(br):
        rows = [smt_ref[SM_GATE + 3 * h + br:SM_GATE + 3 * h + br + 1, :] for h in range(NSA_HEADS)]
        return _sigmoid(jnp.concatenate(rows, axis=1))

    o_t = gate_row(0) * o_cmp + gate_row(1) * o_sel + gate_row(2) * o_win
    for pr in range(NSA_HEADS // 2):
        xp = jnp.concatenate([o_t[:, (2 * pr) * QT:(2 * pr + 1) * QT], o_t[:, (2 * pr + 1) * QT:(2 * pr + 2) * QT]], axis=0)
        o_ref[:, pr * LANES:(pr + 1) * LANES] = xp.T.astype(BF16)


def _nsa(qpad, kcb, vct, ks, kw, vst, vwt, smt, b, s):
    nq = s // QT
    ncmp = kcb.shape[1]
    return pl.pallas_call(
        _nsa_kernel,
        out_shape=jax.ShapeDtypeStruct((b * s, NSA_HEADS * NSA_HD), BF16),
        grid=(b, nq),
        in_specs=[
            pl.BlockSpec((QT, NSA_HEADS * NSA_HD), lambda bi, c: (bi * nq + c, 0)),
            pl.BlockSpec((None, ncmp, LANES), lambda bi, c: (bi, 0, 0)),
            pl.BlockSpec((None, NSA_KV * NSA_HD, ncmp), lambda bi, c: (bi, 0, 0)),
            pl.BlockSpec((s, LANES), lambda bi, c: (bi, 0)),
            pl.BlockSpec((s, LANES), lambda bi, c: (bi, 0)),
            pl.BlockSpec((s // KC, NSA_KV * VT_ROWS, KC), lambda bi, c: (bi, 0, 0)),
            pl.BlockSpec((s // KC, NSA_KV * VT_ROWS, KC), lambda bi, c: (bi, 0, 0)),
            pl.BlockSpec((LANES, QT), lambda bi, c: (0, bi * nq + c)),
        ],
        out_specs=pl.BlockSpec((QT, NSA_HEADS * NSA_HD), lambda bi, c: (bi * nq + c, 0)),
        scratch_shapes=[pltpu.VMEM((s // SEL_BLOCK, NSA_HEADS * QT), F32),
                        pltpu.VMEM((SEL_GROUP, NSA_HEADS * QT), F32),
                        pltpu.VMEM((SEL_GROUP, NSA_HEADS * QT), F32)],
        compiler_params=pltpu.CompilerParams(dimension_semantics=("arbitrary", "arbitrary"),
                                             vmem_limit_bytes=VMEM_LIMIT),
        name="nsa",
    )(qpad, kcb, vct, ks, kw, vst, vwt, smt)


def _log_sigmoid(x):
    return jnp.minimum(x, 0.0) - jnp.log(1.0 + jnp.exp(-jnp.abs(x)))


def _mlstm_kernel(qk_ref, v_ref, og_ref, sm_ref, smt_ref, wc_ref, bc_ref, bifc_ref, bifr_ref, ghn_ref,
                  y_ref, ext_scr, ct_scr, n_scr, m_scr):
    lc = ML_CHUNK

    @pl.when(pl.program_id(1) == 0)
    def _():
        ext_scr[0:8, :] = jnp.zeros((8, 2 * ML_WIDTH), F32)
        ct_scr[...] = jnp.zeros_like(ct_scr)
        n_scr[...] = jnp.zeros_like(n_scr)
        m_scr[...] = jnp.zeros_like(m_scr)

    ext_scr[8:8 + ML_BLOCK, :] = qk_ref[...].astype(F32)
    y = bc_ref[...]
    for j in range(CONV_W):
        y = y + wc_ref[j:j + 1, :] * ext_scr[pl.ds(8 - (CONV_W - 1) + j, ML_BLOCK), :]
    ext_scr[0:8, :] = ext_scr[ML_BLOCK:ML_BLOCK + 8, :]
    qkc = y * _sigmoid(y)
    q_all = qkc[:, 0:ML_WIDTH].astype(BF16)
    k_all = (qkc[:, ML_WIDTH:2 * ML_WIDTH] * (ML_HD ** -0.5)).astype(BF16)

    ifc = sm_ref[...] + bifc_ref[...]
    ifr = smt_ref[...] + bifr_ref[...]
    lfc = _log_sigmoid(ifc)
    lfr = _log_sigmoid(ifr)
    rr = lax.broadcasted_iota(jnp.int32, (lc, lc), 0)
    cc = lax.broadcasted_iota(jnp.int32, (lc, lc), 1)
    causal = rr >= cc
    tri_l = causal.astype(F32)
    tri_u = (rr <= cc).astype(F32)

    for ci in range(ML_BLOCK // lc):
        lo, hi = ci * lc, (ci + 1) * lc
        bc_all = jnp.dot(tri_l, lfc[lo:hi, :], preferred_element_type=F32, precision=lax.Precision.HIGHEST)
        br_all = jnp.dot(lfr[:, lo:hi], tri_u, preferred_element_type=F32, precision=lax.Precision.HIGHEST)
        heads = range(ML_HEADS)
        hsl = [slice(h * ML_HD, (h + 1) * ML_HD) for h in heads]
        bcol = [bc_all[:, SM_F + h:SM_F + h + 1] for h in heads]
        brow = [br_all[SM_F + h:SM_F + h + 1, :] for h in heads]
        icol = [ifc[lo:hi, SM_I + h:SM_I + h + 1] for h in heads]
        irow = [ifr[SM_I + h:SM_I + h + 1, lo:hi] for h in heads]
        mprev = [m_scr[h][:, 0:1] for h in heads]
        qh = [q_all[lo:hi, hsl[h]] for h in heads]
        kh = [k_all[lo:hi, hsl[h]] for h in heads]
        vh = [v_ref[lo:hi, hsl[h]] for h in heads]
        ct = [ct_scr[h] for h in heads]
        nrow = [n_scr[h] for h in heads]
        qk = [_dot_nt(qh[h], kh[h]) for h in heads]
        qc = [_dot(qh[h], ct[h].astype(BF16)) for h in heads]
        dmat = [jnp.where(causal, bcol[h] - brow[h] + irow[h], NEG) for h in heads]
        inter = [bcol[h] + mprev[h] for h in heads]
        mt = [jnp.maximum(jnp.max(dmat[h], axis=-1, keepdims=True), inter[h]) for h in heads]
        a = [jnp.exp(dmat[h] - mt[h]) * qk[h] for h in heads]
        dec = [jnp.exp(inter[h] - mt[h]) for h in heads]
        num = [_dot(a[h].astype(BF16), vh[h]) + dec[h] * qc[h] for h in heads]
        den = [jnp.sum(a[h], axis=-1, keepdims=True)
               + dec[h] * jnp.sum(qh[h].astype(F32) * nrow[h], axis=-1, keepdims=True) for h in heads]
        blast = [bcol[h][lc - 1:lc, :] for h in heads]
        mnew = [jnp.maximum(blast[h] + mprev[h], jnp.max(blast[h] - brow[h] + irow[h], axis=-1, keepdims=True))
                for h in heads]
        wprev = [jnp.exp(blast[h] + mprev[h] - mnew[h]) for h in heads]
        kwt = [kh[h].astype(F32) * jnp.exp(blast[h] - bcol[h] + icol[h] - mnew[h]) for h in heads]
        for h in heads:
            ct_scr[h] = wprev[h] * ct[h] + lax.dot_general(kwt[h].astype(BF16), vh[h], _TN,
                                                           preferred_element_type=F32)
            n_scr[h] = wprev[h] * nrow[h] + jnp.sum(kwt[h], axis=0, keepdims=True)
            m_scr[h] = jnp.broadcast_to(mnew[h], (1, LANES))
        hm = [num[h] / jnp.maximum(jnp.abs(den[h]), jnp.exp(-mt[h])) * _sigmoid(og_ref[lo:hi, hsl[h]].astype(F32))
              for h in heads]
        for h in heads:
            y_ref[lo:hi, hsl[h]] = _rms(hm[h], ghn_ref[:, hsl[h]]).astype(BF16)


def _mlstm(qkb, vb, ob, sm, smt, wconv, bconv, bifc, bifr, ghn, b, s):
    nb = s // ML_BLOCK
    row = lambda w: pl.BlockSpec((ML_BLOCK, w), lambda bi, j: (bi * nb + j, 0))
    return pl.pallas_call(
        _mlstm_kernel,
        out_shape=jax.ShapeDtypeStruct((b * s, ML_WIDTH), BF16),
        grid=(b, nb),
        in_specs=[row(2 * ML_WIDTH), row(ML_WIDTH), row(ML_WIDTH), row(LANES),
                  pl.BlockSpec((LANES, ML_BLOCK), lambda bi, j: (0, bi * nb + j)),
                  _const_spec(wconv.shape), _const_spec(bconv.shape), _const_spec(bifc.shape),
                  _const_spec(bifr.shape), _const_spec(ghn.shape)],
        out_specs=row(ML_WIDTH),
        scratch_shapes=[pltpu.VMEM((ML_BLOCK + 8, 2 * ML_WIDTH), F32),
                        pltpu.VMEM((ML_HEADS, ML_HD, ML_HD), F32),
                        pltpu.VMEM((ML_HEADS, 1, ML_HD), F32),
                        pltpu.VMEM((ML_HEADS, 1, LANES), F32)],
        compiler_params=pltpu.CompilerParams(dimension_semantics=("arbitrary", "arbitrary"),
                                             vmem_limit_bytes=VMEM_LIMIT),
        name="mlstm",
    )(qkb, vb, ob, sm, smt, wconv, bconv, bifc, bifr, ghn)


RT_BUCKET = N_EXPERTS
RT_RANK = N_EXPERTS + 1
RT_WLO = N_EXPERTS + 2
RT_WHI = N_EXPERTS + 3
N_BUCKETS = N_GROUPS * 6
X_ROWS = D_MODEL // LANES
REC = 2 * X_ROWS
DMA_UNROLL = 8
DISPATCH_SUB = 32
COMBINE_PARTS = 2


def _merge_kernel(ya_ref, yb_ref, mg_ref, x_ref, wpa_ref, wpb_ref, wout_ref, gffn_ref, wr_ref, br_ref,
                  x1_ref, h2_ref, slab_ref, hist_ref):
    halves = [slice(i * (TD // 2), (i + 1) * (TD // 2)) for i in range(2)]
    pa = [_dot(ya_ref[hs, :], wpa_ref[...]) for hs in halves]
    pb = [_dot(yb_ref[hs, :], wpb_ref[...]) for hs in halves]
    mixed = [(_sigmoid(mg_ref[hs, 0:D_MODEL].astype(F32)) * pa[i]
              + _sigmoid(mg_ref[hs, D_MODEL:2 * D_MODEL].astype(F32)) * pb[i]).astype(BF16)
             for i, hs in enumerate(halves)]
    x1 = [x_ref[hs, :] + _dot(mixed[i], wout_ref[...]) for i, hs in enumerate(halves)]
    h2 = [_rms(x1[i], gffn_ref[...]) for i in range(2)]
    for i, hs in enumerate(halves):
        x1_ref[hs, :] = x1[i]
        h2_ref[hs, :] = h2[i]

    h_hi = [h.astype(BF16) for h in h2]
    h_lo = [(h2[i] - h_hi[i].astype(F32)).astype(BF16) for i in range(2)]
    r_hi = [_dot(h, wr_ref[...]) for h in h_hi]
    logit = jnp.concatenate([r_hi[i][:, 0:LANES] + r_hi[i][:, LANES:2 * LANES] + _dot(h_lo[i], wr_ref[:, 0:LANES])
                             for i in range(2)], axis=0) + br_ref[...]
    lane = lax.broadcasted_iota(jnp.int32, logit.shape, 1)
    big = jnp.int32(LANES)
    gmask = (lane >= N_EXPERTS) & (lane < N_EXPERTS + N_GROUPS)
    gl = jnp.where(gmask, logit, NEG)
    gmax = jnp.max(gl, axis=-1, keepdims=True)
    gidx = jnp.min(jnp.where(gmask & (gl == gmax), lane, big), axis=-1, keepdims=True) - N_EXPERTS
    pg_sel = 1.0 / jnp.sum(jnp.where(gmask, jnp.exp(gl - gmax), 0.0), axis=-1, keepdims=True)
    emask = (lane < N_EXPERTS) & (lax.shift_right_logical(lane, 2) == gidx)
    el = jnp.where(emask, logit, NEG)
    e1 = jnp.max(el, axis=-1, keepdims=True)
    i1 = jnp.min(jnp.where(emask & (el == e1), lane, big), axis=-1, keepdims=True)
    emask2 = emask & (lane != i1)
    el2 = jnp.where(emask2, logit, NEG)
    e2 = jnp.max(el2, axis=-1, keepdims=True)
    i2 = jnp.min(jnp.where(emask2 & (el2 == e2), lane, big), axis=-1, keepdims=True)
    x21 = jnp.exp(e2 - e1)
    w1 = pg_sel / (1.0 + x21)
    w2 = pg_sel * x21 / (1.0 + x21)
    first_lo = i1 < i2
    e_lo = jnp.where(first_lo, i1, i2) - EXP_PER_GROUP * gidx
    e_hi = jnp.where(first_lo, i2, i1) - EXP_PER_GROUP * gidx
    pair = lax.shift_right_logical(e_lo * (2 * EXP_PER_GROUP - 1 - e_lo), 1) + (e_hi - e_lo - 1)
    bucket = 6 * gidx + pair
    member = lane == bucket
    onehot = jnp.where(member, 1.0, 0.0)
    rr = lax.broadcasted_iota(jnp.int32, (TD, TD), 0)
    cc = lax.broadcasted_iota(jnp.int32, (TD, TD), 1)
    earlier = _dot((rr > cc).astype(BF16), onehot.astype(BF16))
    rank = jnp.sum(jnp.where(member, earlier, 0.0), axis=-1, keepdims=True)
    slab = jnp.where(lane == i1, w1, 0.0) + jnp.where(lane == i2, w2, 0.0)
    slab = jnp.where(lane == RT_BUCKET, bucket.astype(F32), slab)
    slab = jnp.where(lane == RT_RANK, rank, slab)
    slab = jnp.where(lane == RT_WLO, jnp.where(first_lo, w1, w2), slab)
    slab = jnp.where(lane == RT_WHI, jnp.where(first_lo, w2, w1), slab)
    slab_ref[...] = slab
    hist_ref[...] = jnp.broadcast_to(jnp.sum(onehot, axis=0, keepdims=True), (8, LANES))


def _merge(ya, yb, mg, x2, wpa, wpb, wout, gffn, wr, br):
    t = x2.shape[0]
    row = lambda w: pl.BlockSpec((TD, w), lambda i: (i, 0))
    return pl.pallas_call(
        _merge_kernel,
        out_shape=[jax.ShapeDtypeStruct((t, D_MODEL), F32),
                   jax.ShapeDtypeStruct((t, D_MODEL), F32),
                   jax.ShapeDtypeStruct((t, LANES), F32),
                   jax.ShapeDtypeStruct((t // TD * 8, LANES), F32)],
        grid=(t // TD,),
        in_specs=[row(NSA_HEADS * NSA_HD), row(ML_WIDTH), row(2 * D_MODEL), row(D_MODEL)]
                 + [_const_spec(a.shape) for a in (wpa, wpb, wout, gffn, wr, br)],
        out_specs=[row(D_MODEL), row(D_MODEL), row(LANES),
                   pl.BlockSpec((8, LANES), lambda i: (i, 0))],
        compiler_params=pltpu.CompilerParams(dimension_semantics=("arbitrary",), vmem_limit_bytes=VMEM_LIMIT),
        name="merge",
    )(ya, yb, mg, x2, wpa, wpb, wout, gffn, wr, br)


def _rec_copy(src_ref, src_tok, dst_ref, dst_tok, sem, rows):
    src = src_ref.at[pl.ds(pl.multiple_of(src_tok * rows, rows), rows), :]
    dst = dst_ref.at[pl.ds(pl.multiple_of(dst_tok * rows, rows), rows), :]
    return pltpu.make_async_copy(src, dst, sem)


def _token_copies(n, make, wait=False):
    def body(g, carry):
        for u in range(DMA_UNROLL):
            cp = make(g * DMA_UNROLL + u)
            if wait:
                cp.wait()
            else:
                cp.start(priority=u % 2)
        return carry
    lax.fori_loop(0, n // DMA_UNROLL, body, 0)


def _dispatch_kernel(pos_ref, tail_ref, h2_ref, slab_ref, out_ref, stage_scr, zero_scr, sem, zsem):
    i = pl.program_id(0)
    slot = lax.rem(i, 2)
    n_tiles = out_ref.shape[0] // (TM * REC)
    n_used = tail_ref[2 * N_BUCKETS]

    def zero_copy(first_slot):
        start = pl.multiple_of(first_slot * REC, TM * REC)
        return pltpu.make_async_copy(zero_scr, out_ref.at[pl.ds(start, TM * REC), :], zsem)

    @pl.when(i == 0)
    def _():
        zero_scr[...] = jnp.zeros_like(zero_scr)
        for phase in ("start", "wait"):
            for b in range(N_BUCKETS):
                @pl.when(tail_ref[N_BUCKETS + b] > 0)
                def _():
                    getattr(zero_copy(tail_ref[b]), phase)()

                @pl.when(n_used + b < n_tiles)
                def _():
                    getattr(zero_copy((n_used + b) * TM), phase)()

    stage = stage_scr.at[slot]
    base = i * TM
    for t0 in range(0, TM, DISPATCH_SUB):
        for j in range(REC):
            if j < X_ROWS:
                rows = h2_ref[t0:t0 + DISPATCH_SUB, j * LANES:(j + 1) * LANES]
            elif j == X_ROWS:
                rows = slab_ref[t0:t0 + DISPATCH_SUB, :]
            else:
                rows = jnp.zeros((DISPATCH_SUB, LANES), F32)
            stage[pl.ds(t0 * REC + j, DISPATCH_SUB, stride=REC), :] = rows
        for r in range(t0, t0 + DISPATCH_SUB):
            _rec_copy(stage, r, out_ref, pos_ref[base + r], sem.at[slot], REC).start(priority=r % 2)

    def drain(which):
        _token_copies(TM, lambda r: _rec_copy(stage_scr.at[which], 0, out_ref, 0, sem.at[which], REC), wait=True)

    @pl.when(i > 0)
    def _():
        drain(1 - slot)

    @pl.when(i == pl.num_programs(0) - 1)
    def _():
        drain(slot)


def _dispatch(pos, tail, h2, slab, n_slots):
    t = h2.shape[0]
    return pl.pallas_call(
        _dispatch_kernel,
        out_shape=jax.ShapeDtypeStruct((n_slots * REC, LANES), F32),
        grid_spec=pltpu.PrefetchScalarGridSpec(
            num_scalar_prefetch=2,
            grid=(t // TM,),
            in_specs=[pl.BlockSpec((TM, D_MODEL), lambda i, pos_r, tail_r: (i, 0)),
                      pl.BlockSpec((TM, LANES), lambda i, pos_r, tail_r: (i, 0))],
            out_specs=pl.BlockSpec(memory_space=pl.ANY),
            scratch_shapes=[pltpu.VMEM((2, TM * REC, LANES), F32), pltpu.VMEM((TM * REC, LANES), F32),
                            pltpu.SemaphoreType.DMA((2,)), pltpu.SemaphoreType.DMA(())],
        ),
        compiler_params=pltpu.CompilerParams(dimension_semantics=("arbitrary",), vmem_limit_bytes=VMEM_LIMIT,
                                             has_side_effects=True),
        name="dispatch",
    )(pos, tail, h2, slab)


MOE_TILES = 2


def _moe_kernel(te_ref, nu_ref, hx_ref, w13_ref, w2_ref, y_ref):
    step = pl.program_id(0)
    n_tiles = pl.num_programs(0) * MOE_TILES
    n_used = nu_ref[0]

    @pl.when(step * MOE_TILES < n_used)
    def _():
        subs = range(MOE_TILES)
        hs = [jnp.concatenate([hx_ref[pl.ds(sub * TM * REC + j, TM, stride=REC), :] for j in range(X_ROWS)],
                              axis=1).astype(BF16) for sub in subs]
        slabs = [hx_ref[pl.ds(sub * TM * REC + X_ROWS, TM, stride=REC), :] for sub in subs]
        ys = [None] * MOE_TILES
        for side, lane in ((0, RT_WLO), (1, RT_WHI)):
            es = [te_ref[side * n_tiles + step * MOE_TILES + sub] for sub in subs]
            up = [_dot(hs[sub], w13_ref[es[sub]]) for sub in subs]
            act = [(up[sub][:, 0:D_EXPERT] * _sigmoid(up[sub][:, 0:D_EXPERT]) * up[sub][:, D_EXPERT:2 * D_EXPERT]
                    * slabs[sub][:, lane:lane + 1]).astype(BF16) for sub in subs]
            for sub in subs:
                part = _dot(act[sub], w2_ref[es[sub]])
                ys[sub] = part if ys[sub] is None else ys[sub] + part
        for sub in subs:
            for j in range(X_ROWS):
                y_ref[pl.ds(sub * TM * X_ROWS + j, TM, stride=X_ROWS), :] = ys[sub][:, j * LANES:(j + 1) * LANES]

    @pl.when(step * MOE_TILES >= n_used)
    def _():
        y_ref[...] = jnp.zeros_like(y_ref)


def _moe(tile_e, n_used, hx_sorted, w13, w2):
    n_tiles = hx_sorted.shape[0] // (TM * REC)
    rows = MOE_TILES * TM
    last = lambda nu: (nu[0] - 1) // MOE_TILES
    return pl.pallas_call(
        _moe_kernel,
        out_shape=jax.ShapeDtypeStruct((n_tiles * TM * X_ROWS, LANES), F32),
        grid_spec=pltpu.PrefetchScalarGridSpec(
            num_scalar_prefetch=2,
            grid=(n_tiles // MOE_TILES,),
            in_specs=[pl.BlockSpec((rows * REC, LANES), lambda k, te, nu: (jnp.minimum(k, last(nu)), 0)),
                      pl.BlockSpec(w13.shape, lambda k, te, nu: (0, 0, 0), pipeline_mode=pl.Buffered(1)),
                      pl.BlockSpec(w2.shape, lambda k, te, nu: (0, 0, 0), pipeline_mode=pl.Buffered(1))],
            out_specs=pl.BlockSpec((rows * X_ROWS, LANES), lambda k, te, nu: (k, 0)),
        ),
        compiler_params=pltpu.CompilerParams(dimension_semantics=("arbitrary",), vmem_limit_bytes=VMEM_LIMIT),
        name="moe",
    )(tile_e, n_used, hx_sorted, w13, w2)


def _combine_kernel(pos_ref, y_ref, x1_ref, p_ref, gple_ref, wpg_ref, wpp_ref, gfin_ref, o_ref, ybuf, sem):
    i = pl.program_id(0)
    slot = lax.rem(i, 2)

    last = pl.num_programs(0) - 1

    def drain(which):
        _token_copies(TD, lambda r: _rec_copy(y_ref, 0, ybuf.at[which], 0, sem.at[which], X_ROWS), wait=True)

    @pl.when(i == 0)
    def _():
        _token_copies(TD, lambda r: _rec_copy(y_ref, pos_ref[r], ybuf.at[0], r, sem.at[0], X_ROWS))

    drain(slot)
    nxt = jnp.minimum(i + 1, last) * TD
    yb = ybuf.at[slot]
    part = TD // COMBINE_PARTS
    for q in range(COMBINE_PARTS):
        rows = slice(q * part, (q + 1) * part)
        y = jnp.concatenate([yb[pl.ds(q * part * X_ROWS + j, part, stride=X_ROWS), :] for j in range(X_ROWS)],
                            axis=1)
        x2 = x1_ref[rows, :] + y
        h3 = _rms(x2, gple_ref[...]).astype(BF16)
        x3 = x2 + _sigmoid(_dot(h3, wpg_ref[...])) * _dot(p_ref[rows, :].astype(BF16), wpp_ref[...])
        o_ref[rows, :] = _rms(x3, gfin_ref[...])
        for r in range(q * part, (q + 1) * part):
            _rec_copy(y_ref, pos_ref[nxt + r], ybuf.at[1 - slot], r, sem.at[1 - slot], X_ROWS).start(
                priority=r % 2)

    @pl.when(i == last)
    def _():
        drain(1 - slot)


def _combine(pos, y_sorted, x1, p2, gple, wpg, wpp, gfin):
    t = x1.shape[0]
    row = lambda w: pl.BlockSpec((TD, w), lambda i, pos_r: (i, 0))
    const = lambda a: pl.BlockSpec(a.shape, lambda i, pos_r: (0,) * a.ndim, pipeline_mode=pl.Buffered(1))
    return pl.pallas_call(
        _combine_kernel,
        out_shape=jax.ShapeDtypeStruct((t, D_MODEL), F32),
        grid_spec=pltpu.PrefetchScalarGridSpec(
            num_scalar_prefetch=1,
            grid=(t // TD,),
            in_specs=[pl.BlockSpec(memory_space=pl.ANY), row(D_MODEL), row(PLE_DIM),
                      const(gple), const(wpg), const(wpp), const(gfin)],
            out_specs=row(D_MODEL),
            scratch_shapes=[pltpu.VMEM((2, TD * X_ROWS, LANES), F32), pltpu.SemaphoreType.DMA((2,))],
        ),
        compiler_params=pltpu.CompilerParams(dimension_semantics=("arbitrary",), vmem_limit_bytes=VMEM_LIMIT),
        name="combine",
    )(pos, y_sorted, x1, p2, gple, wpg, wpp, gfin)


def _routing_tables(slab, hist8):
    t = slab.shape[0]
    nt = t // TD
    n_tiles = t // TM + N_BUCKETS
    hist = hist8.reshape(nt, 8, LANES)[:, 0, :]
    counts = jnp.sum(hist, axis=0)
    padded = jnp.ceil(counts / TM) * TM
    ends = jnp.cumsum(padded)
    first = (ends - padded)[None, :] + jnp.cumsum(hist, axis=0) - hist
    lane = jnp.arange(N_BUCKETS, dtype=F32)[None, :]
    mine = lane == slab[:, RT_BUCKET:RT_BUCKET + 1]
    pos = jnp.sum(jnp.where(mine, jnp.repeat(first[:, :N_BUCKETS], TD, axis=0), 0.0), axis=1) + slab[:, RT_RANK]
    starts = jnp.arange(n_tiles, dtype=F32) * TM
    tile_bucket = jnp.minimum(jnp.sum(ends[None, :N_BUCKETS] <= starts[:, None], axis=1), N_BUCKETS - 1)
    group, pair = tile_bucket // 6, tile_bucket % 6
    e_lo = EXP_PER_GROUP * group + jnp.array([0, 0, 0, 1, 1, 2], jnp.int32)[pair]
    e_hi = EXP_PER_GROUP * group + jnp.array([1, 2, 3, 2, 3, 3], jnp.int32)[pair]
    tile_e = jnp.concatenate([e_lo, e_hi]).astype(jnp.int32)
    n_used = (ends[N_BUCKETS - 1] / TM).astype(jnp.int32).reshape(1)
    tail = jnp.concatenate([(ends - TM)[:N_BUCKETS], padded[:N_BUCKETS], n_used.astype(F32)]).astype(jnp.int32)
    return pos.astype(jnp.int32), tile_e, n_used, tail, n_tiles * TM


def _pack_inproj_weights(w):
    d = w.shape[0]
    qw = NSA_HEADS * NSA_HD
    kvw = NSA_KV * NSA_HD
    o = 0
    wq = w[:, o:o + qw]; o += qw
    wkc = w[:, o:o + kvw]; o += kvw
    wvc = w[:, o:o + kvw]; o += kvw
    wks = w[:, o:o + kvw]; o += kvw
    wvs = w[:, o:o + kvw]; o += kvw
    wkw = w[:, o:o + kvw]; o += kvw
    wvw = w[:, o:o + kvw]; o += kvw
    wga = w[:, o:o + 3 * NSA_HEADS]; o += 3 * NSA_HEADS
    wqkb = w[:, o:o + 2 * ML_WIDTH]; o += 2 * ML_WIDTH
    wvb = w[:, o:o + ML_WIDTH]; o += ML_WIDTH
    wob = w[:, o:o + ML_WIDTH]; o += ML_WIDTH
    wif = w[:, o:o + 2 * ML_HEADS]; o += 2 * ML_HEADS
    wmg = w[:, o:o + 2 * D_MODEL]
    wsm = jnp.concatenate([wga, wif, jnp.zeros((d, LANES - 3 * NSA_HEADS - 2 * ML_HEADS), w.dtype)], axis=1)
    wcat = jnp.concatenate([wq, wkc, wvc, wks, wkw, wsm, wqkb, wvb, wob, wmg], axis=1).astype(BF16)
    wtr = jnp.concatenate([wvs, wvw, wsm], axis=1).T.astype(BF16)
    return wcat, wtr


def _rope_tables(positions):
    half = ROPE_DIM // 2
    inv = ROPE_THETA ** (-jnp.arange(0, ROPE_DIM, 2, dtype=F32) / ROPE_DIM)
    ang = positions.astype(F32).reshape(-1, 1) * inv[None, :]
    cs = jnp.concatenate([jnp.cos(ang), jnp.sin(ang)], axis=1)
    d = np.arange(LANES) % NSA_HD
    spread = np.zeros((2 * half, 3 * LANES), np.float32)
    lanes = np.arange(LANES)
    rot = d < ROPE_DIM
    spread[(d % half)[rot], lanes[rot]] = 1.0
    hi = (d >= half) & rot
    spread[half + (d % half)[hi], LANES + lanes[hi]] = 1.0
    lo = d < half
    spread[half + (d % half)[lo], 2 * LANES + lanes[lo]] = -1.0
    unrotated = (~rot).astype(np.float32).reshape(1, LANES)
    return cs, jnp.asarray(spread, BF16), jnp.asarray(unrotated)


def _pack_compress_weights(w1, w2, pe):
    half = CMP_LEN // 2
    w1r = w1.reshape(2, half, NSA_HD, CMP_HIDDEN)
    outs = []
    for part in range(2):
        wb = w1r[part].astype(BF16)
        zb = jnp.zeros_like(wb)
        wp = jnp.stack([jnp.stack([wb, zb], axis=2), jnp.stack([zb, wb], axis=2)], axis=1)
        outs.append(wp.reshape(half * NSA_KV * NSA_HD, NSA_KV * CMP_HIDDEN))
    pe8 = jnp.broadcast_to(pe.reshape(1, CMP_LEN * NSA_HD), (8, CMP_LEN * NSA_HD)).astype(BF16)
    return outs[0], outs[1], pe8, w1.astype(BF16)


def _stages(x, p, positions, g_mix, w_in, b_if, w_ck1, w_ck2, pe_ck, w_cv1, w_cv2, pe_cv, w_conv, b_conv, g_hn, w_pa, w_pb, w_out, g_ffn, w_rg, b_rg, w_re, b_re, w_e13, w_e2, g_ple, w_pg, w_pp, g_final):
    b, s, d = x.shape
    t = b * s
    cs, spread, unrot = _rope_tables(positions)
    assert w_in.shape[0] == 1, "the final norm is fused into the layer's last kernel: single-layer problem only"
    for i in range(w_in.shape[0]):
        x2 = x.reshape(t, d)
        wcat, wtr = _pack_inproj_weights(w_in[i])
        (qpad, kc_tok, vc_tok, ks, kw, vst, vwt, sm, smt, qkb, vb, ob, mg) = _inproj(
            x2, g_mix[i].reshape(1, d), wcat, wtr, cs, spread, unrot)
        wka, wkb, pek, w1k = _pack_compress_weights(w_ck1[i], w_ck2[i], pe_ck[i])
        wva, wvb, pev, w1v = _pack_compress_weights(w_cv1[i], w_cv2[i], pe_cv[i])
        zpad = jnp.zeros((CMP_HIDDEN, NSA_HD), F32)
        w2k = jnp.stack([jnp.concatenate([w_ck2[i], zpad], axis=1),
                         jnp.concatenate([zpad, w_ck2[i]], axis=1)]).astype(BF16)
        w2vt = w_cv2[i].T.astype(BF16)
        nrow = s // CMP_STRIDE
        rk = kc_tok.reshape(b, nrow, CMP_STRIDE * LANES)
        rv = vc_tok.reshape(b, nrow, CMP_STRIDE * LANES)
        kcb, vct = _compress(rk, rv, wka, wkb, wva, wvb, pek, pev, w1k, w1v, w2k, w2vt)
        ya = _nsa(qpad, kcb, vct, ks, kw, vst, vwt, smt, b, s)
        bif = b_if[i].astype(F32)
        bifc = jnp.zeros((1, LANES), F32).at[0, SM_I:SM_I + 2 * ML_HEADS].set(bif)
        bifr = bifc.reshape(LANES, 1)
        yb = _mlstm(qkb, vb, ob, sm, smt, w_conv[i], b_conv[i].reshape(1, -1), bifc, bifr,
                    g_hn[i].reshape(1, -1), b, s)
        wr = jnp.concatenate([w_re[i], w_rg[i], jnp.zeros((d, LANES - N_EXPERTS - N_GROUPS), F32)], axis=1)
        wr_hi = wr.astype(BF16)
        wr = jnp.concatenate([wr_hi, (wr - wr_hi.astype(F32)).astype(BF16)], axis=1)
        br =jnp.concatenate([b_re[i], b_rg[i], jnp.zeros((LANES - N_EXPERTS - N_GROUPS,), F32)]).reshape(1, LANES)
        x1, h2, slab, hist8 = _merge(ya, yb, mg, x2, w_pa[i].astype(BF16), w_pb[i].astype(BF16),
                                     w_out[i].astype(BF16), g_ffn[i].reshape(1, d), wr, br)
        pos, tile_e, n_used, tail, n_slots = _routing_tables(slab, hist8)
        hx_sorted = _dispatch(pos, tail, h2, slab, n_slots)
        y_sorted = _moe(tile_e, n_used, hx_sorted, w_e13[i].astype(BF16), w_e2[i].astype(BF16))
        out = _combine(pos, y_sorted, x1, p[i].reshape(t, PLE_DIM), g_ple[i].reshape(1, d), w_pg[i].astype(BF16),
                       w_pp[i].astype(BF16), g_final.reshape(1, d))
        x = out.reshape(b, s, d)
    return dict(out=x, qpad=qpad, ks=ks, kcb=kcb, vct=vct, y_a=ya, y_b=yb, x1=x1, pos=pos)


def kernel(x, p, positions, g_mix, w_in, b_if, w_ck1, w_ck2, pe_ck, w_cv1, w_cv2, pe_cv, w_conv, b_conv, g_hn, w_pa, w_pb, w_out, g_ffn, w_rg, b_rg, w_re, b_re, w_e13, w_e2, g_ple, w_pg, w_pp, g_final):
    return _stages(x, p, positions, g_mix, w_in, b_if, w_ck1, w_ck2, pe_ck, w_cv1, w_cv2, pe_cv, w_conv, b_conv, g_hn,
                   w_pa, w_pb, w_out, g_ffn, w_rg, b_rg, w_re, b_re, w_e13, w_e2, g_ple, w_pg, w_pp, g_final)["out"]
```

```python
import math

import numpy as np
import jax
import jax.numpy as jnp
from jax import lax
from jax.experimental import pallas as pl
from jax.experimental.pallas import tpu as pltpu

F32 = jnp.float32
BF16 = jnp.bfloat16

EPS = 1e-6
NEG = -1e30

D_MODEL = 1024
PLE_DIM = 256
NSA_HEADS = 8
NSA_KV = 2
NSA_HPG = NSA_HEADS // NSA_KV
NSA_HD = 64
CMP_LEN = 32
CMP_STRIDE = 16
CMP_HIDDEN = 256
SEL_BLOCK = 64
SEL_TOPK = 16
SEL_FORCE = 1000.0
WINDOW = 512
ROPE_THETA = 500000.0
ROPE_DIM = NSA_HD // 4
ML_HEADS = 4
ML_HD = 128
ML_WIDTH = ML_HEADS * ML_HD
CONV_W = 4
N_GROUPS = 4
EXP_PER_GROUP = 4
N_EXPERTS = N_GROUPS * EXP_PER_GROUP
D_EXPERT = 256

LANES = 128
QT = 128
KC = 128
SEL_GROUP = 512
VT_PAD = 16
VT_ROWS = NSA_HD + VT_PAD
ML_CHUNK = 128
ML_BLOCK = 256
TD = 512
TM = 256
VMEM_LIMIT = 56 * 1024 * 1024

_NT = (((1,), (1,)), ((), ()))
_TN = (((0,), (0,)), ((), ()))

SM_GATE = 0
SM_I = 3 * NSA_HEADS
SM_F = SM_I + ML_HEADS


def _dot(a, b):
    return jnp.dot(a, b, preferred_element_type=F32)


def _dot_nt(a, b):
    return lax.dot_general(a, b, _NT, preferred_element_type=F32)


def _split3(x):
    hi = x.astype(BF16)
    r1 = x - hi.astype(F32)
    mid = r1.astype(BF16)
    lo = (r1 - mid.astype(F32)).astype(BF16)
    return hi, mid, lo


def _rms(x, g):
    return x * lax.rsqrt(jnp.mean(x * x, axis=-1, keepdims=True) + EPS) * g


def _sigmoid(x):
    return 0.5 + 0.5 * jnp.tanh(0.5 * x)


def _const_spec(shape):
    nd = len(shape)
    return pl.BlockSpec(shape, lambda *_: (0,) * nd, pipeline_mode=pl.Buffered(1))


_C_Q = 0
_C_KC = _C_Q + NSA_HEADS * NSA_HD
_C_VC = _C_KC + LANES
_C_KS = _C_VC + LANES
_C_KW = _C_KS + LANES
_C_SM = _C_KW + LANES
_C_QKB = _C_SM + LANES
_C_VB = _C_QKB + 2 * ML_WIDTH
_C_OB = _C_VB + ML_WIDTH
_C_MG = _C_OB + ML_WIDTH
_C_END = _C_MG + 2 * D_MODEL


def _inproj_kernel(x_ref, g_ref, w_ref, wt_ref, cs_ref, spread_ref, unrot_ref,
                   q_ref, kc_ref, vc_ref, ks_ref, kw_ref, vst_ref, vwt_ref, sm_ref, smt_ref,
                   qkb_ref, vb_ref, ob_ref, mg_ref):
    hn = _rms(x_ref[...], g_ref[...]).astype(BF16)
    tables = sum(_dot(part, spread_ref[...]) for part in _split3(cs_ref[...]))
    rc = tables[:, 0:LANES] + unrot_ref[...]
    rp = tables[:, LANES:2 * LANES]
    rm = tables[:, 2 * LANES:3 * LANES]

    def rope(z):
        return z * rc + pltpu.roll(z, 8, 1) * rp + pltpu.roll(z, LANES - 8, 1) * rm

    scale = NSA_HD ** -0.5 * math.log2(math.e)
    for h in range(NSA_HEADS * NSA_HD // LANES):
        z = _dot(hn, w_ref[:, _C_Q + h * LANES:_C_Q + (h + 1) * LANES])
        q_ref[:, h * LANES:(h + 1) * LANES] = (rope(z) * scale).astype(BF16)
    kc_ref[...] = rope(_dot(hn, w_ref[:, _C_KC:_C_KC + LANES])).astype(BF16)
    vc_ref[...] = _dot(hn, w_ref[:, _C_VC:_C_VC + LANES]).astype(BF16)
    ks_ref[...] = rope(_dot(hn, w_ref[:, _C_KS:_C_KS + LANES])).astype(BF16)
    kw_ref[...] = rope(_dot(hn, w_ref[:, _C_KW:_C_KW + LANES])).astype(BF16)
    sm_ref[...] = _dot(hn, w_ref[:, _C_SM:_C_SM + LANES])
    for c0 in range(0, 2 * ML_WIDTH, 512):
        qkb_ref[:, c0:c0 + 512] = _dot(hn, w_ref[:, _C_QKB + c0:_C_QKB + c0 + 512]).astype(BF16)
    vb_ref[...] = _dot(hn, w_ref[:, _C_VB:_C_VB + ML_WIDTH]).astype(BF16)
    ob_ref[...] = _dot(hn, w_ref[:, _C_OB:_C_OB + ML_WIDTH]).astype(BF16)
    for c0 in range(0, 2 * D_MODEL, 512):
        mg_ref[:, c0:c0 + 512] = _dot(hn, w_ref[:, _C_MG + c0:_C_MG + c0 + 512]).astype(BF16)
    zt = _dot_nt(wt_ref[...], hn)
    ones_rows = (lax.broadcasted_iota(jnp.int32, (VT_PAD, KC), 0) == 0).astype(BF16)
    for i in range(TD // KC):
        for ref, r0 in ((vst_ref, 0), (vwt_ref, LANES)):
            zc = zt[r0:r0 + LANES, i * KC:(i + 1) * KC].astype(BF16)
            ref[i] = jnp.concatenate([piece for g in range(NSA_KV)
                                      for piece in (zc[g * NSA_HD:(g + 1) * NSA_HD, :], ones_rows)], axis=0)
    smt_ref[...] = zt[2 * LANES:3 * LANES, :]


def _inproj(x2, g_mix, wcat, wtr, cs, spread, unrot):
    t = x2.shape[0]
    row = lambda w: pl.BlockSpec((TD, w), lambda i: (i, 0))
    out_shape = [
        jax.ShapeDtypeStruct((t, NSA_HEADS * NSA_HD), BF16),
        jax.ShapeDtypeStruct((t, LANES), BF16),
        jax.ShapeDtypeStruct((t, LANES), BF16),
        jax.ShapeDtypeStruct((t, LANES), BF16),
        jax.ShapeDtypeStruct((t, LANES), BF16),
        jax.ShapeDtypeStruct((t // KC, NSA_KV * VT_ROWS, KC), BF16),
        jax.ShapeDtypeStruct((t // KC, NSA_KV * VT_ROWS, KC), BF16),
        jax.ShapeDtypeStruct((t, LANES), F32),
        jax.ShapeDtypeStruct((LANES, t), F32),
        jax.ShapeDtypeStruct((t, 2 * ML_WIDTH), BF16),
        jax.ShapeDtypeStruct((t, ML_WIDTH), BF16),
        jax.ShapeDtypeStruct((t, ML_WIDTH), BF16),
        jax.ShapeDtypeStruct((t, 2 * D_MODEL), BF16),
    ]
    chunk3 = pl.BlockSpec((TD // KC, NSA_KV * VT_ROWS, KC), lambda i: (i, 0, 0))
    out_specs = [row(NSA_HEADS * NSA_HD), row(LANES), row(LANES), row(LANES), row(LANES), chunk3, chunk3,
                 row(LANES), pl.BlockSpec((LANES, TD), lambda i: (0, i)),
                 row(2 * ML_WIDTH), row(ML_WIDTH), row(ML_WIDTH), row(2 * D_MODEL)]
    return pl.pallas_call(
        _inproj_kernel,
        out_shape=out_shape,
        grid=(t // TD,),
        in_specs=[row(D_MODEL), _const_spec((1, D_MODEL)), _const_spec((D_MODEL, _C_END)),
                  _const_spec((3 * LANES, D_MODEL)), row(cs.shape[1]), _const_spec(spread.shape),
                  _const_spec(unrot.shape)],
        out_specs=out_specs,
        compiler_params=pltpu.CompilerParams(dimension_semantics=("arbitrary",), vmem_limit_bytes=VMEM_LIMIT),
        name="inproj",
    )(x2, g_mix, wcat, wtr, cs, spread, unrot)


def _gelu_tanh(x):
    return 0.5 * x * (1.0 + jnp.tanh(math.sqrt(2.0 / math.pi) * (x + 0.044715 * x * x * x)))


def _compress_kernel(rk_ref, rv_ref, wka_ref, wkb_ref, wva_ref, wvb_ref, pek_ref, pev_ref,
                     w1k_ref, w1v_ref, w2k_ref, w2vt_ref, kc_ref, vct_ref):
    nrow = rk_ref.shape[0]

    def hidden(r_ref, wa_ref, wb_ref, pe_ref, w1_ref):
        r = r_ref[...]
        ha = _dot(r, wa_ref[...])
        hb = _dot(r, wb_ref[...])
        hb = pltpu.roll(hb, nrow - 1, 0)
        c = _dot(pe_ref[...], w1_ref[...])[0:1, :]
        return [_gelu_tanh(ha[:, g * CMP_HIDDEN:(g + 1) * CMP_HIDDEN] + hb[:, g * CMP_HIDDEN:(g + 1) * CMP_HIDDEN] + c).astype(BF16)
                for g in range(NSA_KV)]

    ak = hidden(rk_ref, wka_ref, wkb_ref, pek_ref, w1k_ref)
    kc_ref[...] = (_dot(ak[0], w2k_ref[0]) + _dot(ak[1], w2k_ref[1])).astype(BF16)
    av = hidden(rv_ref, wva_ref, wvb_ref, pev_ref, w1v_ref)
    for g in range(NSA_KV):
        vct_ref[g * NSA_HD:(g + 1) * NSA_HD, :] = _dot_nt(w2vt_ref[...], av[g]).astype(BF16)


def _compress(rk, rv, wka, wkb, wva, wvb, pek, pev, w1k, w1v, w2k, w2vt):
    b, nrow, width = rk.shape
    blk = pl.BlockSpec((None, nrow, width), lambda i: (i, 0, 0))
    return pl.pallas_call(
        _compress_kernel,
        out_shape=[jax.ShapeDtypeStruct((b, nrow, LANES), BF16),
                   jax.ShapeDtypeStruct((b, LANES, nrow), BF16)],
        grid=(b,),
        in_specs=[blk, blk] + [_const_spec(a.shape) for a in (wka, wkb, wva, wvb, pek, pev, w1k, w1v, w2k, w2vt)],
        out_specs=[pl.BlockSpec((None, nrow, LANES), lambda i: (i, 0, 0)),
                   pl.BlockSpec((None, LANES, nrow), lambda i: (i, 0, 0))],
        compiler_params=pltpu.CompilerParams(dimension_semantics=("arbitrary",), vmem_limit_bytes=VMEM_LIMIT),
        name="compress",
    )(rk, rv, wka, wkb, wva, wvb, pek, pev, w1k, w1v, w2k, w2vt)


def _nsa_kernel(q_ref, kc_ref, vct_ref, ks_ref, kw_ref, vst_ref, vwt_ref, smt_ref, o_ref, bias_scr, sx_scr, sy_scr):
    c = pl.program_id(1)
    t0 = c * QT
    ncmp = kc_ref.shape[0]
    nsel = bias_scr.shape[0]
    nw = WINDOW // KC + 1
    gw = NSA_HPG * QT
    width = NSA_KV * gw

    def per_group(x):
        return [x[:, g * gw:(g + 1) * gw] for g in range(NSA_KV)]

    def pv(vt, p):
        rows = vt.shape[0] // NSA_KV
        pb = p.astype(BF16)
        return jnp.concatenate([_dot(vt[g * rows:(g + 1) * rows, :], pg) for g, pg in enumerate(per_group(pb))],
                               axis=1)

    def normalised(acc):
        return acc[0:NSA_HD, :] / acc[NSA_HD:NSA_HD + 1, :]

    low_half = lax.broadcasted_iota(jnp.int32, (1, LANES), 1) < NSA_HD
    q_heads = []
    for h in range(NSA_HEADS):
        pair = q_ref[:, (h // 2) * LANES:(h // 2 + 1) * LANES].astype(F32)
        want_low = h // NSA_HPG == 0
        if (h % 2 == 0) != want_low:
            pair = pltpu.roll(pair, NSA_HD, 1)
        q_heads.append(jnp.where(low_half if want_low else ~low_half, pair, 0.0).astype(BF16))
    qs = jnp.concatenate(q_heads, axis=0)
    u_row = lax.broadcasted_iota(jnp.int32, (1, width), 1) % QT
    t_row = t0 + u_row
    r_kc = lax.broadcasted_iota(jnp.int32, (KC, 1), 0)

    n_grp = ks_ref.shape[0] // SEL_GROUP
    n_full = lax.shift_right_logical(t0, int(math.log2(SEL_GROUP)))

    def qk_group(j):
        return _dot_nt(ks_ref[pl.ds(pl.multiple_of(j * SEL_GROUP, SEL_GROUP), SEL_GROUP), :], qs)

    sc = _dot_nt(kc_ref[...], qs)

    w_slabs, w_chunks = [], []
    for i in range(nw):
        jj = c - (nw - 1) + i
        jc = jnp.maximum(jj, 0)
        si = _dot_nt(kw_ref[pl.ds(pl.multiple_of(jc * KC, KC), KC), :], qs)
        if i == 0:
            keep = (r_kc > u_row) & (jj >= 0)
        elif i == nw - 1:
            keep = r_kc <= u_row
        else:
            keep = jj >= 0
        w_slabs.append(jnp.where(keep, si, NEG))
        w_chunks.append(jc)

    n_col = lax.broadcasted_iota(jnp.int32, (ncmp, 1), 0)
    cmask = (CMP_STRIDE * n_col + (CMP_LEN - 1) <= t_row) & (n_col < ncmp - 1)
    s = jnp.where(cmask, sc, NEG)
    m = jnp.max(s, axis=0, keepdims=True)
    e = jnp.exp2(s - m)
    anyv = (t_row >= CMP_LEN - 1).astype(F32)
    p = e * (anyv / jnp.sum(e, axis=0, keepdims=True))
    o_cmp = pv(vct_ref[...], p)

    psums = []
    for pg in per_group(p):
        acc_p = pg[:, 0:QT]
        for h in range(1, NSA_HPG):
            acc_p = acc_p + pg[:, h * QT:(h + 1) * QT]
        psums.append(acc_p)
    psum = jnp.concatenate(psums, axis=1)
    nq2 = NSA_KV * QT
    s_col = lax.broadcasted_iota(jnp.int32, (nsel, 1), 0)
    n_lane = lax.broadcasted_iota(jnp.int32, (1, ncmp), 1)
    ov = ((CMP_STRIDE * n_lane < SEL_BLOCK * (s_col + 1)) & (CMP_STRIDE * n_lane + (CMP_LEN - 1) >= SEL_BLOCK * s_col)
          ).astype(BF16)
    imp = sum(_dot(ov, part) for part in _split3(psum))

    s_diag = qk_group(n_full)
    sx_scr[...] = qk_group(0)

    mxw = w_slabs[0]
    for sl in w_slabs[1:]:
        mxw = jnp.maximum(mxw, sl)
    mw = jnp.max(mxw, axis=0, keepdims=True)
    acc_w = jnp.zeros((VT_ROWS, width), F32)
    for sl, jc in zip(w_slabs, w_chunks):
        acc_w = acc_w + pv(vwt_ref[jc], jnp.exp2(sl - mw))
    o_win = normalised(acc_w)

    t1 = t0 + lax.broadcasted_iota(jnp.int32, (1, nq2), 1) % QT
    cur = lax.shift_right_logical(t1, 6)
    forced = (s_col == 0) | (s_col == cur) | (s_col == cur - 1)
    valid = SEL_BLOCK * s_col <= t1
    val = jnp.where(valid, jnp.where(forced, imp + SEL_FORCE, imp), NEG)
    sub = 8
    r_sub = lax.broadcasted_iota(jnp.int32, (sub, 1), 0)
    blocks = [val[r * sub:(r + 1) * sub, :] for r in range(nsel // sub)]
    ranks = [jnp.zeros((sub, nq2), F32) for _ in blocks]
    for i in range(nsel):
        vi = val[i:i + 1, :]
        for r, blk in enumerate(blocks):
            if i < r * sub:
                beats = vi >= blk
            elif i >= (r + 1) * sub:
                beats = vi > blk
            else:
                beats = (vi > blk) | ((vi == blk) & (r_sub > i - r * sub))
            ranks[r] = ranks[r] + jnp.where(beats, 1.0, 0.0)
    bias = jnp.where(jnp.concatenate(ranks, axis=0) < SEL_TOPK, 0.0, NEG).astype(F32)
    bias_scr[...] = jnp.concatenate([bias[:, g * QT:(g + 1) * QT] for g in range(NSA_KV) for _ in range(NSA_HPG)],
                                    axis=1)

    r_blk = lax.broadcasted_iota(jnp.int32, (SEL_BLOCK, 1), 0)
    blk_per_grp = SEL_GROUP // SEL_BLOCK
    chunk_per_grp = SEL_GROUP // KC
    blk_per_chunk = KC // SEL_BLOCK

    def sel_update(j, sj, carry, causal=False):
        m_o, acc = carry
        brows, keeps = [], []
        for i in range(blk_per_grp):
            brows.append(bias_scr[pl.ds(blk_per_grp * j + i, 1), :])
            keeps.append(j * SEL_GROUP + i * SEL_BLOCK + r_blk <= t_row if causal else None)

        def block(i):
            return sj[i * SEL_BLOCK:(i + 1) * SEL_BLOCK, :]

        mx = None
        for i in range(blk_per_grp):
            sl = block(i) + brows[i]
            if causal:
                sl = jnp.where(keeps[i], sl, NEG)
            mx = sl if mx is None else jnp.maximum(mx, sl)
        m_n = jnp.maximum(m_o, jnp.max(mx, axis=0, keepdims=True))
        a = jnp.exp2(m_o - m_n)
        acc = a * acc
        for ci in range(chunk_per_grp):
            parts = []
            for i in range(blk_per_chunk * ci, blk_per_chunk * (ci + 1)):
                arg = block(i) + (brows[i] - m_n)
                parts.append(jnp.exp2(jnp.where(keeps[i], arg, NEG) if causal else arg))
            acc = acc + pv(vst_ref[chunk_per_grp * j + ci], jnp.concatenate(parts, axis=0))
        return m_n, acc

    empty = (jnp.full((1, width), NEG, F32), jnp.zeros((VT_ROWS, width), F32))
    seeded = sel_update(n_full, s_diag, empty, causal=True)

    def pair_body(jp, carry):
        ja, jb = 2 * jp, 2 * jp + 1
        sy_scr[...] = qk_group(jb)
        carry = sel_update(ja, sx_scr, carry)
        sx_scr[...] = qk_group(jnp.minimum(ja + 2, n_grp - 1))
        return sel_update(jb, sy_scr, carry)

    n_pairs = lax.shift_right_logical(n_full, 1)
    carry = lax.fori_loop(0, n_pairs, pair_body, seeded)
    _, acc_s = lax.cond(n_full - 2 * n_pairs == 1, lambda cr: sel_update(n_full - 1, sx_scr, cr), lambda cr: cr,
                        carry)
    o_sel = normalised(acc_s)

    def gate_row(br):
        rows = [smt_ref[SM_GATE + 3 * h + br:SM_GATE + 3 * h + br + 1, :] for h in range(NSA_HEADS)]
        return _sigmoid(jnp.concatenate(rows, axis=1))

    o_t = gate_row(0) * o_cmp + gate_row(1) * o_sel + gate_row(2) * o_win
    for pr in range(NSA_HEADS // 2):
        xp = jnp.concatenate([o_t[:, (2 * pr) * QT:(2 * pr + 1) * QT], o_t[:, (2 * pr + 1) * QT:(2 * pr + 2) * QT]], axis=0)
        o_ref[:, pr * LANES:(pr + 1) * LANES] = xp.T.astype(BF16)


def _nsa(qpad, kcb, vct, ks, kw, vst, vwt, smt, b, s):
    nq = s // QT
    ncmp = kcb.shape[1]
    return pl.pallas_call(
        _nsa_kernel,
        out_shape=jax.ShapeDtypeStruct((b * s, NSA_HEADS * NSA_HD), BF16),
        grid=(b, nq),
        in_specs=[
            pl.BlockSpec((QT, NSA_HEADS * NSA_HD), lambda bi, c: (bi * nq + c, 0)),
            pl.BlockSpec((None, ncmp, LANES), lambda bi, c: (bi, 0, 0)),
            pl.BlockSpec((None, NSA_KV * NSA_HD, ncmp), lambda bi, c: (bi, 0, 0)),
            pl.BlockSpec((s, LANES), lambda bi, c: (bi, 0)),
            pl.BlockSpec((s, LANES), lambda bi, c: (bi, 0)),
            pl.BlockSpec((s // KC, NSA_KV * VT_ROWS, KC), lambda bi, c: (bi, 0, 0)),
            pl.BlockSpec((s // KC, NSA_KV * VT_ROWS, KC), lambda bi, c: (bi, 0, 0)),
            pl.BlockSpec((LANES, QT), lambda bi, c: (0, bi * nq + c)),
        ],
        out_specs=pl.BlockSpec((QT, NSA_HEADS * NSA_HD), lambda bi, c: (bi * nq + c, 0)),
        scratch_shapes=[pltpu.VMEM((s // SEL_BLOCK, NSA_HEADS * QT), F32),
                        pltpu.VMEM((SEL_GROUP, NSA_HEADS * QT), F32),
                        pltpu.VMEM((SEL_GROUP, NSA_HEADS * QT), F32)],
        compiler_params=pltpu.CompilerParams(dimension_semantics=("arbitrary", "arbitrary"),
                                             vmem_limit_bytes=VMEM_LIMIT),
        name="nsa",
    )(qpad, kcb, vct, ks, kw, vst, vwt, smt)


def _log_sigmoid(x):
    return jnp.minimum(x, 0.0) - jnp.log(1.0 + jnp.exp(-jnp.abs(x)))


def _mlstm_kernel(qk_ref, v_ref, og_ref, sm_ref, smt_ref, wc_ref, bc_ref, bifc_ref, bifr_ref, ghn_ref,
                  y_ref, tail_scr, ct_scr, n_scr, m_scr):
    lc = ML_CHUNK

    @pl.when(pl.program_id(1) == 0)
    def _():
        tail_scr[...] = jnp.zeros_like(tail_scr)
        ct_scr[...] = jnp.zeros_like(ct_scr)
        n_scr[...] = jnp.zeros_like(n_scr)
        m_scr[...] = jnp.zeros_like(m_scr)

    u = qk_ref[...]
    tail = tail_scr[...]
    rr8 = lax.broadcasted_iota(jnp.int32, (8, 1), 0)
    sr = lax.broadcasted_iota(jnp.int32, (ML_BLOCK, ML_BLOCK), 0)
    sc_ = lax.broadcasted_iota(jnp.int32, (ML_BLOCK, ML_BLOCK), 1)
    y = bc_ref[...] + wc_ref[CONV_W - 1:CONV_W, :] * u.astype(F32)
    for k in range(1, CONV_W):
        down = _dot((sr - sc_ == k).astype(BF16), u)
        head = jnp.where(rr8 < k, pltpu.roll(tail, k, 0), down[0:8, :])
        y = y + wc_ref[CONV_W - 1 - k:CONV_W - k, :] * jnp.concatenate([head, down[8:, :]], axis=0)
    tail_scr[...] = u[ML_BLOCK - 8:ML_BLOCK, :].astype(F32)
    qkc = y * _sigmoid(y)
    q_all = qkc[:, 0:ML_WIDTH].astype(BF16)
    k_all = (qkc[:, ML_WIDTH:2 * ML_WIDTH] * (ML_HD ** -0.5)).astype(BF16)

    ifc = sm_ref[...] + bifc_ref[...]
    ifr = smt_ref[...] + bifr_ref[...]
    lfc = _log_sigmoid(ifc)
    lfr = _log_sigmoid(ifr)
    rr = lax.broadcasted_iota(jnp.int32, (lc, lc), 0)
    cc = lax.broadcasted_iota(jnp.int32, (lc, lc), 1)
    causal = rr >= cc
    tri_l = causal.astype(F32)
    tri_u = (rr <= cc).astype(F32)

    for ci in range(ML_BLOCK // lc):
        lo, hi = ci * lc, (ci + 1) * lc
        bc_all = jnp.dot(tri_l, lfc[lo:hi, :], preferred_element_type=F32, precision=lax.Precision.HIGHEST)
        br_all = jnp.dot(lfr[:, lo:hi], tri_u, preferred_element_type=F32, precision=lax.Precision.HIGHEST)
        heads = range(ML_HEADS)
        hsl = [slice(h * ML_HD, (h + 1) * ML_HD) for h in heads]
        bcol = [bc_all[:, SM_F + h:SM_F + h + 1] for h in heads]
        brow = [br_all[SM_F + h:SM_F + h + 1, :] for h in heads]
        icol = [ifc[lo:hi, SM_I + h:SM_I + h + 1] for h in heads]
        irow = [ifr[SM_I + h:SM_I + h + 1, lo:hi] for h in heads]
        mprev = [m_scr[h][:, 0:1] for h in heads]
        qh = [q_all[lo:hi, hsl[h]] for h in heads]
        kh = [k_all[lo:hi, hsl[h]] for h in heads]
        vh = [v_ref[lo:hi, hsl[h]] for h in heads]
        ct = [ct_scr[h] for h in heads]
        nrow = [n_scr[h] for h in heads]
        qk = [_dot_nt(qh[h], kh[h]) for h in heads]
        qc = [_dot(qh[h], ct[h].astype(BF16)) for h in heads]
        dmat = [jnp.where(causal, bcol[h] - brow[h] + irow[h], NEG) for h in heads]
        inter = [bcol[h] + mprev[h] for h in heads]
        mt = [jnp.maximum(jnp.max(dmat[h], axis=-1, keepdims=True), inter[h]) for h in heads]
        a = [jnp.exp(dmat[h] - mt[h]) * qk[h] for h in heads]
        dec = [jnp.exp(inter[h] - mt[h]) for h in heads]
        num = [_dot(a[h].astype(BF16), vh[h]) + dec[h] * qc[h] for h in heads]
        den = [jnp.sum(a[h], axis=-1, keepdims=True)
               + dec[h] * jnp.sum(qh[h].astype(F32) * nrow[h], axis=-1, keepdims=True) for h in heads]
        blast = [bcol[h][lc - 1:lc, :] for h in heads]
        mnew = [jnp.maximum(blast[h] + mprev[h], jnp.max(blast[h] - brow[h] + irow[h], axis=-1, keepdims=True))
                for h in heads]
        wprev = [jnp.exp(blast[h] + mprev[h] - mnew[h]) for h in heads]
        kwt = [kh[h].astype(F32) * jnp.exp(blast[h] - bcol[h] + icol[h] - mnew[h]) for h in heads]
        for h in heads:
            ct_scr[h] = wprev[h] * ct[h] + lax.dot_general(kwt[h].astype(BF16), vh[h], _TN,
                                                           preferred_element_type=F32)
            n_scr[h] = wprev[h] * nrow[h] + jnp.sum(kwt[h], axis=0, keepdims=True)
            m_scr[h] = jnp.broadcast_to(mnew[h], (1, LANES))
        hm = [num[h] / jnp.maximum(jnp.abs(den[h]), jnp.exp(-mt[h])) * _sigmoid(og_ref[lo:hi, hsl[h]].astype(F32))
              for h in heads]
        for h in heads:
            y_ref[lo:hi, hsl[h]] = _rms(hm[h], ghn_ref[:, hsl[h]]).astype(BF16)


def _mlstm(qkb, vb, ob, sm, smt, wconv, bconv, bifc, bifr, ghn, b, s):
    nb = s // ML_BLOCK
    row = lambda w: pl.BlockSpec((ML_BLOCK, w), lambda bi, j: (bi * nb + j, 0))
    return pl.pallas_call(
        _mlstm_kernel,
        out_shape=jax.ShapeDtypeStruct((b * s, ML_WIDTH), BF16),
        grid=(b, nb),
        in_specs=[row(2 * ML_WIDTH), row(ML_WIDTH), row(ML_WIDTH), row(LANES),
                  pl.BlockSpec((LANES, ML_BLOCK), lambda bi, j: (0, bi * nb + j)),
                  _const_spec(wconv.shape), _const_spec(bconv.shape), _const_spec(bifc.shape),
                  _const_spec(bifr.shape), _const_spec(ghn.shape)],
        out_specs=row(ML_WIDTH),
        scratch_shapes=[pltpu.VMEM((8, 2 * ML_WIDTH), F32),
                        pltpu.VMEM((ML_HEADS, ML_HD, ML_HD), F32),
                        pltpu.VMEM((ML_HEADS, 1, ML_HD), F32),
                        pltpu.VMEM((ML_HEADS, 1, LANES), F32)],
        compiler_params=pltpu.CompilerParams(dimension_semantics=("arbitrary", "arbitrary"),
                                             vmem_limit_bytes=VMEM_LIMIT),
        name="mlstm",
    )(qkb, vb, ob, sm, smt, wconv, bconv, bifc, bifr, ghn)


RT_BUCKET = N_EXPERTS
RT_RANK = N_EXPERTS + 1
RT_WLO = N_EXPERTS + 2
RT_WHI = N_EXPERTS + 3
N_BUCKETS = N_GROUPS * 6
X_ROWS = D_MODEL // LANES
REC = 2 * X_ROWS
DMA_UNROLL = 8
DISPATCH_SUB = 32
COMBINE_PARTS = 2


def _merge_kernel(ya_ref, yb_ref, mg_ref, x_ref, wpa_ref, wpb_ref, wout_ref, gffn_ref, wr_ref, br_ref,
                  x1_ref, h2_ref, slab_ref, hist_ref):
    halves = [slice(i * (TD // 2), (i + 1) * (TD // 2)) for i in range(2)]
    pa = [_dot(ya_ref[hs, :], wpa_ref[...]) for hs in halves]
    pb = [_dot(yb_ref[hs, :], wpb_ref[...]) for hs in halves]
    mixed = [(_sigmoid(mg_ref[hs, 0:D_MODEL].astype(F32)) * pa[i]
              + _sigmoid(mg_ref[hs, D_MODEL:2 * D_MODEL].astype(F32)) * pb[i]).astype(BF16)
             for i, hs in enumerate(halves)]
    x1 = [x_ref[hs, :] + _dot(mixed[i], wout_ref[...]) for i, hs in enumerate(halves)]
    h2 = [_rms(x1[i], gffn_ref[...]) for i in range(2)]
    for i, hs in enumerate(halves):
        x1_ref[hs, :] = x1[i]
        h2_ref[hs, :] = h2[i]

    h_hi = [h.astype(BF16) for h in h2]
    h_lo = [(h2[i] - h_hi[i].astype(F32)).astype(BF16) for i in range(2)]
    r_hi = [_dot(h, wr_ref[...]) for h in h_hi]
    logit = jnp.concatenate([r_hi[i][:, 0:LANES] + r_hi[i][:, LANES:2 * LANES] + _dot(h_lo[i], wr_ref[:, 0:LANES])
                             for i in range(2)], axis=0) + br_ref[...]
    lane = lax.broadcasted_iota(jnp.int32, logit.shape, 1)
    big = jnp.int32(LANES)
    gmask = (lane >= N_EXPERTS) & (lane < N_EXPERTS + N_GROUPS)
    gl = jnp.where(gmask, logit, NEG)
    gmax = jnp.max(gl, axis=-1, keepdims=True)
    gidx = jnp.min(jnp.where(gmask & (gl == gmax), lane, big), axis=-1, keepdims=True) - N_EXPERTS
    pg_sel = 1.0 / jnp.sum(jnp.where(gmask, jnp.exp(gl - gmax), 0.0), axis=-1, keepdims=True)
    emask = (lane < N_EXPERTS) & (lax.shift_right_logical(lane, 2) == gidx)
    el = jnp.where(emask, logit, NEG)
    e1 = jnp.max(el, axis=-1, keepdims=True)
    i1 = jnp.min(jnp.where(emask & (el == e1), lane, big), axis=-1, keepdims=True)
    emask2 = emask & (lane != i1)
    el2 = jnp.where(emask2, logit, NEG)
    e2 = jnp.max(el2, axis=-1, keepdims=True)
    i2 = jnp.min(jnp.where(emask2 & (el2 == e2), lane, big), axis=-1, keepdims=True)
    x21 = jnp.exp(e2 - e1)
    w1 = pg_sel / (1.0 + x21)
    w2 = pg_sel * x21 / (1.0 + x21)
    first_lo = i1 < i2
    e_lo = jnp.where(first_lo, i1, i2) - EXP_PER_GROUP * gidx
    e_hi = jnp.where(first_lo, i2, i1) - EXP_PER_GROUP * gidx
    pair = lax.shift_right_logical(e_lo * (2 * EXP_PER_GROUP - 1 - e_lo), 1) + (e_hi - e_lo - 1)
    bucket = 6 * gidx + pair
    member = lane == bucket
    onehot = jnp.where(member, 1.0, 0.0)
    rr = lax.broadcasted_iota(jnp.int32, (TD, TD), 0)
    cc = lax.broadcasted_iota(jnp.int32, (TD, TD), 1)
    earlier = _dot((rr > cc).astype(BF16), onehot.astype(BF16))
    rank = jnp.sum(jnp.where(member, earlier, 0.0), axis=-1, keepdims=True)
    slab = jnp.where(lane == i1, w1, 0.0) + jnp.where(lane == i2, w2, 0.0)
    slab = jnp.where(lane == RT_BUCKET, bucket.astype(F32), slab)
    slab = jnp.where(lane == RT_RANK, rank, slab)
    slab = jnp.where(lane == RT_WLO, jnp.where(first_lo, w1, w2), slab)
    slab = jnp.where(lane == RT_WHI, jnp.where(first_lo, w2, w1), slab)
    slab_ref[...] = slab
    hist_ref[...] = jnp.broadcast_to(jnp.sum(onehot, axis=0, keepdims=True), (8, LANES))


def _merge(ya, yb, mg, x2, wpa, wpb, wout, gffn, wr, br):
    t = x2.shape[0]
    row = lambda w: pl.BlockSpec((TD, w), lambda i: (i, 0))
    return pl.pallas_call(
        _merge_kernel,
        out_shape=[jax.ShapeDtypeStruct((t, D_MODEL), F32),
                   jax.ShapeDtypeStruct((t, D_MODEL), F32),
                   jax.ShapeDtypeStruct((t, LANES), F32),
                   jax.ShapeDtypeStruct((t // TD * 8, LANES), F32)],
        grid=(t // TD,),
        in_specs=[row(NSA_HEADS * NSA_HD), row(ML_WIDTH), row(2 * D_MODEL), row(D_MODEL)]
                 + [_const_spec(a.shape) for a in (wpa, wpb, wout, gffn, wr, br)],
        out_specs=[row(D_MODEL), row(D_MODEL), row(LANES),
                   pl.BlockSpec((8, LANES), lambda i: (i, 0))],
        compiler_params=pltpu.CompilerParams(dimension_semantics=("arbitrary",), vmem_limit_bytes=VMEM_LIMIT),
        name="merge",
    )(ya, yb, mg, x2, wpa, wpb, wout, gffn, wr, br)


def _rec_copy(src_ref, src_tok, dst_ref, dst_tok, sem, rows):
    src = src_ref.at[pl.ds(pl.multiple_of(src_tok * rows, rows), rows), :]
    dst = dst_ref.at[pl.ds(pl.multiple_of(dst_tok * rows, rows), rows), :]
    return pltpu.make_async_copy(src, dst, sem)


def _token_copies(n, make, wait=False):
    def body(g, carry):
        for u in range(DMA_UNROLL):
            cp = make(g * DMA_UNROLL + u)
            if wait:
                cp.wait()
            else:
                cp.start(priority=u % 2)
        return carry
    lax.fori_loop(0, n // DMA_UNROLL, body, 0)


def _dispatch_kernel(pos_ref, tail_ref, h2_ref, slab_ref, out_ref, stage_scr, zero_scr, sem, zsem):
    i = pl.program_id(0)
    slot = lax.rem(i, 2)
    n_tiles = out_ref.shape[0] // (TM * REC)
    n_used = tail_ref[2 * N_BUCKETS]

    def zero_copy(first_slot):
        start = pl.multiple_of(first_slot * REC, TM * REC)
        return pltpu.make_async_copy(zero_scr, out_ref.at[pl.ds(start, TM * REC), :], zsem)

    @pl.when(i == 0)
    def _():
        zero_scr[...] = jnp.zeros_like(zero_scr)
        for phase in ("start", "wait"):
            for b in range(N_BUCKETS):
                @pl.when(tail_ref[N_BUCKETS + b] > 0)
                def _():
                    getattr(zero_copy(tail_ref[b]), phase)()

                @pl.when(n_used + b < n_tiles)
                def _():
                    getattr(zero_copy((n_used + b) * TM), phase)()

    stage = stage_scr.at[slot]
    base = i * TM
    for t0 in range(0, TM, DISPATCH_SUB):
        for j in range(REC):
            if j < X_ROWS:
                rows = h2_ref[t0:t0 + DISPATCH_SUB, j * LANES:(j + 1) * LANES]
            elif j == X_ROWS:
                rows = slab_ref[t0:t0 + DISPATCH_SUB, :]
            else:
                rows = jnp.zeros((DISPATCH_SUB, LANES), F32)
            stage[pl.ds(t0 * REC + j, DISPATCH_SUB, stride=REC), :] = rows
        for r in range(t0, t0 + DISPATCH_SUB):
            _rec_copy(stage, r, out_ref, pos_ref[base + r], sem.at[slot], REC).start(priority=r % 2)

    def drain(which):
        _token_copies(TM, lambda r: _rec_copy(stage_scr.at[which], 0, out_ref, 0, sem.at[which], REC), wait=True)

    @pl.when(i > 0)
    def _():
        drain(1 - slot)

    @pl.when(i == pl.num_programs(0) - 1)
    def _():
        drain(slot)


def _dispatch(pos, tail, h2, slab, n_slots):
    t = h2.shape[0]
    return pl.pallas_call(
        _dispatch_kernel,
        out_shape=jax.ShapeDtypeStruct((n_slots * REC, LANES), F32),
        grid_spec=pltpu.PrefetchScalarGridSpec(
            num_scalar_prefetch=2,
            grid=(t // TM,),
            in_specs=[pl.BlockSpec((TM, D_MODEL), lambda i, pos_r, tail_r: (i, 0)),
                      pl.BlockSpec((TM, LANES), lambda i, pos_r, tail_r: (i, 0))],
            out_specs=pl.BlockSpec(memory_space=pl.ANY),
            scratch_shapes=[pltpu.VMEM((2, TM * REC, LANES), F32), pltpu.VMEM((TM * REC, LANES), F32),
                            pltpu.SemaphoreType.DMA((2,)), pltpu.SemaphoreType.DMA(())],
        ),
        compiler_params=pltpu.CompilerParams(dimension_semantics=("arbitrary",), vmem_limit_bytes=VMEM_LIMIT,
                                             has_side_effects=True),
        name="dispatch",
    )(pos, tail, h2, slab)


MOE_TILES = 2


def _moe_kernel(te_ref, nu_ref, hx_ref, w13_ref, w2_ref, y_ref):
    step = pl.program_id(0)
    n_tiles = pl.num_programs(0) * MOE_TILES
    n_used = nu_ref[0]

    @pl.when(step * MOE_TILES < n_used)
    def _():
        subs = range(MOE_TILES)
        hs = [jnp.concatenate([hx_ref[pl.ds(sub * TM * REC + j, TM, stride=REC), :] for j in range(X_ROWS)],
                              axis=1).astype(BF16) for sub in subs]
        slabs = [hx_ref[pl.ds(sub * TM * REC + X_ROWS, TM, stride=REC), :] for sub in subs]
        ys = [None] * MOE_TILES
        for side, lane in ((0, RT_WLO), (1, RT_WHI)):
            es = [te_ref[side * n_tiles + step * MOE_TILES + sub] for sub in subs]
            up = [_dot(hs[sub], w13_ref[es[sub]]) for sub in subs]
            act = [(up[sub][:, 0:D_EXPERT] * _sigmoid(up[sub][:, 0:D_EXPERT]) * up[sub][:, D_EXPERT:2 * D_EXPERT]
                    * slabs[sub][:, lane:lane + 1]).astype(BF16) for sub in subs]
            for sub in subs:
                part = _dot(act[sub], w2_ref[es[sub]])
                ys[sub] = part if ys[sub] is None else ys[sub] + part
        for sub in subs:
            for j in range(X_ROWS):
                y_ref[pl.ds(sub * TM * X_ROWS + j, TM, stride=X_ROWS), :] = ys[sub][:, j * LANES:(j + 1) * LANES]

    @pl.when(step * MOE_TILES >= n_used)
    def _():
        y_ref[...] = jnp.zeros_like(y_ref)


def _moe(tile_e, n_used, hx_sorted, w13, w2):
    n_tiles = hx_sorted.shape[0] // (TM * REC)
    rows = MOE_TILES * TM
    last = lambda nu: (nu[0] - 1) // MOE_TILES
    return pl.pallas_call(
        _moe_kernel,
        out_shape=jax.ShapeDtypeStruct((n_tiles * TM * X_ROWS, LANES), F32),
        grid_spec=pltpu.PrefetchScalarGridSpec(
            num_scalar_prefetch=2,
            grid=(n_tiles // MOE_TILES,),
            in_specs=[pl.BlockSpec((rows * REC, LANES), lambda k, te, nu: (jnp.minimum(k, last(nu)), 0)),
                      pl.BlockSpec(w13.shape, lambda k, te, nu: (0, 0, 0), pipeline_mode=pl.Buffered(1)),
                      pl.BlockSpec(w2.shape, lambda k, te, nu: (0, 0, 0), pipeline_mode=pl.Buffered(1))],
            out_specs=pl.BlockSpec((rows * X_ROWS, LANES), lambda k, te, nu: (k, 0)),
        ),
        compiler_params=pltpu.CompilerParams(dimension_semantics=("arbitrary",), vmem_limit_bytes=VMEM_LIMIT),
        name="moe",
    )(tile_e, n_used, hx_sorted, w13, w2)


def _combine_kernel(pos_ref, y_ref, x1_ref, p_ref, gple_ref, wpg_ref, wpp_ref, gfin_ref, o_ref, ybuf, sem):
    i = pl.program_id(0)
    slot = lax.rem(i, 2)

    last = pl.num_programs(0) - 1

    def drain(which):
        _token_copies(TD, lambda r: _rec_copy(y_ref, 0, ybuf.at[which], 0, sem.at[which], X_ROWS), wait=True)

    @pl.when(i == 0)
    def _():
        _token_copies(TD, lambda r: _rec_copy(y_ref, pos_ref[r], ybuf.at[0], r, sem.at[0], X_ROWS))

    drain(slot)
    nxt = jnp.minimum(i + 1, last) * TD
    yb = ybuf.at[slot]
    part = TD // COMBINE_PARTS
    for q in range(COMBINE_PARTS):
        rows = slice(q * part, (q + 1) * part)
        y = jnp.concatenate([yb[pl.ds(q * part * X_ROWS + j, part, stride=X_ROWS), :] for j in range(X_ROWS)],
                            axis=1)
        x2 = x1_ref[rows, :] + y
        h3 = _rms(x2, gple_ref[...]).astype(BF16)
        x3 = x2 + _sigmoid(_dot(h3, wpg_ref[...])) * _dot(p_ref[rows, :].astype(BF16), wpp_ref[...])
        o_ref[rows, :] = _rms(x3, gfin_ref[...])
        for r in range(q * part, (q + 1) * part):
            _rec_copy(y_ref, pos_ref[nxt + r], ybuf.at[1 - slot], r, sem.at[1 - slot], X_ROWS).start(
                priority=r % 2)

    @pl.when(i == last)
    def _():
        drain(1 - slot)


def _combine(pos, y_sorted, x1, p2, gple, wpg, wpp, gfin):
    t = x1.shape[0]
    row = lambda w: pl.BlockSpec((TD, w), lambda i, pos_r: (i, 0))
    const = lambda a: pl.BlockSpec(a.shape, lambda i, pos_r: (0,) * a.ndim, pipeline_mode=pl.Buffered(1))
    return pl.pallas_call(
        _combine_kernel,
        out_shape=jax.ShapeDtypeStruct((t, D_MODEL), F32),
        grid_spec=pltpu.PrefetchScalarGridSpec(
            num_scalar_prefetch=1,
            grid=(t // TD,),
            in_specs=[pl.BlockSpec(memory_space=pl.ANY), row(D_MODEL), row(PLE_DIM),
                      const(gple), const(wpg), const(wpp), const(gfin)],
            out_specs=row(D_MODEL),
            scratch_shapes=[pltpu.VMEM((2, TD * X_ROWS, LANES), F32), pltpu.SemaphoreType.DMA((2,))],
        ),
        compiler_params=pltpu.CompilerParams(dimension_semantics=("arbitrary",), vmem_limit_bytes=VMEM_LIMIT),
        name="combine",
    )(pos, y_sorted, x1, p2, gple, wpg, wpp, gfin)


def _routing_tables(slab, hist8):
    t = slab.shape[0]
    nt = t // TD
    n_tiles = t // TM + N_BUCKETS
    hist = hist8.reshape(nt, 8, LANES)[:, 0, :]
    counts = jnp.sum(hist, axis=0)
    padded = jnp.ceil(counts / TM) * TM
    ends = jnp.cumsum(padded)
    first = (ends - padded)[None, :] + jnp.cumsum(hist, axis=0) - hist
    lane = jnp.arange(N_BUCKETS, dtype=F32)[None, :]
    mine = lane == slab[:, RT_BUCKET:RT_BUCKET + 1]
    pos = jnp.sum(jnp.where(mine, jnp.repeat(first[:, :N_BUCKETS], TD, axis=0), 0.0), axis=1) + slab[:, RT_RANK]
    starts = jnp.arange(n_tiles, dtype=F32) * TM
    tile_bucket = jnp.minimum(jnp.sum(ends[None, :N_BUCKETS] <= starts[:, None], axis=1), N_BUCKETS - 1)
    group, pair = tile_bucket // 6, tile_bucket % 6
    e_lo = EXP_PER_GROUP * group + jnp.array([0, 0, 0, 1, 1, 2], jnp.int32)[pair]
    e_hi = EXP_PER_GROUP * group + jnp.array([1, 2, 3, 2, 3, 3], jnp.int32)[pair]
    tile_e = jnp.concatenate([e_lo, e_hi]).astype(jnp.int32)
    n_used = (ends[N_BUCKETS - 1] / TM).astype(jnp.int32).reshape(1)
    tail = jnp.concatenate([(ends - TM)[:N_BUCKETS], padded[:N_BUCKETS], n_used.astype(F32)]).astype(jnp.int32)
    return pos.astype(jnp.int32), tile_e, n_used, tail, n_tiles * TM


def _pack_inproj_weights(w):
    d = w.shape[0]
    qw = NSA_HEADS * NSA_HD
    kvw = NSA_KV * NSA_HD
    o = 0
    wq = w[:, o:o + qw]; o += qw
    wkc = w[:, o:o + kvw]; o += kvw
    wvc = w[:, o:o + kvw]; o += kvw
    wks = w[:, o:o + kvw]; o += kvw
    wvs = w[:, o:o + kvw]; o += kvw
    wkw = w[:, o:o + kvw]; o += kvw
    wvw = w[:, o:o + kvw]; o += kvw
    wga = w[:, o:o + 3 * NSA_HEADS]; o += 3 * NSA_HEADS
    wqkb = w[:, o:o + 2 * ML_WIDTH]; o += 2 * ML_WIDTH
    wvb = w[:, o:o + ML_WIDTH]; o += ML_WIDTH
    wob = w[:, o:o + ML_WIDTH]; o += ML_WIDTH
    wif = w[:, o:o + 2 * ML_HEADS]; o += 2 * ML_HEADS
    wmg = w[:, o:o + 2 * D_MODEL]
    wsm = jnp.concatenate([wga, wif, jnp.zeros((d, LANES - 3 * NSA_HEADS - 2 * ML_HEADS), w.dtype)], axis=1)
    wcat = jnp.concatenate([wq, wkc, wvc, wks, wkw, wsm, wqkb, wvb, wob, wmg], axis=1).astype(BF16)
    wtr = jnp.concatenate([wvs, wvw, wsm], axis=1).T.astype(BF16)
    return wcat, wtr


def _rope_tables(positions):
    half = ROPE_DIM // 2
    inv = ROPE_THETA ** (-jnp.arange(0, ROPE_DIM, 2, dtype=F32) / ROPE_DIM)
    ang = positions.astype(F32).reshape(-1, 1) * inv[None, :]
    cs = jnp.concatenate([jnp.cos(ang), jnp.sin(ang)], axis=1)
    d = np.arange(LANES) % NSA_HD
    spread = np.zeros((2 * half, 3 * LANES), np.float32)
    lanes = np.arange(LANES)
    rot = d < ROPE_DIM
    spread[(d % half)[rot], lanes[rot]] = 1.0
    hi = (d >= half) & rot
    spread[half + (d % half)[hi], LANES + lanes[hi]] = 1.0
    lo = d < half
    spread[half + (d % half)[lo], 2 * LANES + lanes[lo]] = -1.0
    unrotated = (~rot).astype(np.float32).reshape(1, LANES)
    return cs, jnp.asarray(spread, BF16), jnp.asarray(unrotated)


def _pack_compress_weights(w1, w2, pe):
    half = CMP_LEN // 2
    w1r = w1.reshape(2, half, NSA_HD, CMP_HIDDEN)
    outs = []
    for part in range(2):
        wb = w1r[part].astype(BF16)
        zb = jnp.zeros_like(wb)
        wp = jnp.stack([jnp.stack([wb, zb], axis=2), jnp.stack([zb, wb], axis=2)], axis=1)
        outs.append(wp.reshape(half * NSA_KV * NSA_HD, NSA_KV * CMP_HIDDEN))
    pe8 = jnp.broadcast_to(pe.reshape(1, CMP_LEN * NSA_HD), (8, CMP_LEN * NSA_HD)).astype(BF16)
    return outs[0], outs[1], pe8, w1.astype(BF16)


def _stages(x, p, positions, g_mix, w_in, b_if, w_ck1, w_ck2, pe_ck, w_cv1, w_cv2, pe_cv, w_conv, b_conv, g_hn, w_pa, w_pb, w_out, g_ffn, w_rg, b_rg, w_re, b_re, w_e13, w_e2, g_ple, w_pg, w_pp, g_final):
    b, s, d = x.shape
    t = b * s
    cs, spread, unrot = _rope_tables(positions)
    assert w_in.shape[0] == 1, "the final norm is fused into the layer's last kernel: single-layer problem only"
    for i in range(w_in.shape[0]):
        x2 = x.reshape(t, d)
        wcat, wtr = _pack_inproj_weights(w_in[i])
        (qpad, kc_tok, vc_tok, ks, kw, vst, vwt, sm, smt, qkb, vb, ob, mg) = _inproj(
            x2, g_mix[i].reshape(1, d), wcat, wtr, cs, spread, unrot)
        wka, wkb, pek, w1k = _pack_compress_weights(w_ck1[i], w_ck2[i], pe_ck[i])
        wva, wvb, pev, w1v = _pack_compress_weights(w_cv1[i], w_cv2[i], pe_cv[i])
        zpad = jnp.zeros((CMP_HIDDEN, NSA_HD), F32)
        w2k = jnp.stack([jnp.concatenate([w_ck2[i], zpad], axis=1),
                         jnp.concatenate([zpad, w_ck2[i]], axis=1)]).astype(BF16)
        w2vt = w_cv2[i].T.astype(BF16)
        nrow = s // CMP_STRIDE
        rk = kc_tok.reshape(b, nrow, CMP_STRIDE * LANES)
        rv = vc_tok.reshape(b, nrow, CMP_STRIDE * LANES)
        kcb, vct = _compress(rk, rv, wka, wkb, wva, wvb, pek, pev, w1k, w1v, w2k, w2vt)
        ya = _nsa(qpad, kcb, vct, ks, kw, vst, vwt, smt, b, s)
        bif = b_if[i].astype(F32)
        bifc = jnp.zeros((1, LANES), F32).at[0, SM_I:SM_I + 2 * ML_HEADS].set(bif)
        bifr = bifc.reshape(LANES, 1)
        yb = _mlstm(qkb, vb, ob, sm, smt, w_conv[i], b_conv[i].reshape(1, -1), bifc, bifr,
                    g_hn[i].reshape(1, -1), b, s)
        wr = jnp.concatenate([w_re[i], w_rg[i], jnp.zeros((d, LANES - N_EXPERTS - N_GROUPS), F32)], axis=1)
        wr_hi = wr.astype(BF16)
        wr = jnp.concatenate([wr_hi, (wr - wr_hi.astype(F32)).astype(BF16)], axis=1)
        br =jnp.concatenate([b_re[i], b_rg[i], jnp.zeros((LANES - N_EXPERTS - N_GROUPS,), F32)]).reshape(1, LANES)
        x1, h2, slab, hist8 = _merge(ya, yb, mg, x2, w_pa[i].astype(BF16), w_pb[i].astype(BF16),
                                     w_out[i].astype(BF16), g_ffn[i].reshape(1, d), wr, br)
        pos, tile_e, n_used, tail, n_slots = _routing_tables(slab, hist8)
        hx_sorted = _dispatch(pos, tail, h2, slab, n_slots)
        y_sorted = _moe(tile_e, n_used, hx_sorted, w_e13[i].astype(BF16), w_e2[i].astype(BF16))
        out = _combine(pos, y_sorted, x1, p[i].reshape(t, PLE_DIM), g_ple[i].reshape(1, d), w_pg[i].astype(BF16),
                       w_pp[i].astype(BF16), g_final.reshape(1, d))
        x = out.reshape(b, s, d)
    return dict(out=x, qpad=qpad, ks=ks, kcb=kcb, vct=vct, y_a=ya, y_b=yb, x1=x1, pos=pos)


def kernel(x, p, positions, g_mix, w_in, b_if, w_ck1, w_ck2, pe_ck, w_cv1, w_cv2, pe_cv, w_conv, b_conv, g_hn, w_pa, w_pb, w_out, g_ffn, w_rg, b_rg, w_re, b_re, w_e13, w_e2, g_ple, w_pg, w_pp, g_final):
    return _stages(x, p, positions, g_mix, w_in, b_if, w_ck1, w_ck2, pe_ck, w_cv1, w_cv2, pe_cv, w_conv, b_conv, g_hn,
                   w_pa, w_pb, w_out, g_ffn, w_rg, b_rg, w_re, b_re, w_e13, w_e2, g_ple, w_pg, w_pp, g_final)["out"]
```

```python
import math

import numpy as np
import jax
import jax.numpy as jnp
from jax import lax
from jax.experimental import pallas as pl
from jax.experimental.pallas import tpu as pltpu

F32 = jnp.float32
BF16 = jnp.bfloat16

EPS = 1e-6
NEG = -1e30

D_MODEL = 1024
PLE_DIM = 256
NSA_HEADS = 8
NSA_KV = 2
NSA_HPG = NSA_HEADS // NSA_KV
NSA_HD = 64
CMP_LEN = 32
CMP_STRIDE = 16
CMP_HIDDEN = 256
SEL_BLOCK = 64
SEL_TOPK = 16
SEL_FORCE = 1000.0
WINDOW = 512
ROPE_THETA = 500000.0
ROPE_DIM = NSA_HD // 4
ML_HEADS = 4
ML_HD = 128
ML_WIDTH = ML_HEADS * ML_HD
CONV_W = 4
N_GROUPS = 4
EXP_PER_GROUP = 4
N_EXPERTS = N_GROUPS * EXP_PER_GROUP
D_EXPERT = 256

LANES = 128
SUBLANES = 8
QT = 128
KC = 128
SEL_GROUP = 512
VT_PAD = 16
VT_ROWS = NSA_HD + VT_PAD
ML_CHUNK = 128
ML_BLOCK = 256
TD = 512
TM = 256
VMEM_LIMIT = 56 * 1024 * 1024

_NT = (((1,), (1,)), ((), ()))
_TN = (((0,), (0,)), ((), ()))

SM_GATE = 0
SM_I = 3 * NSA_HEADS
SM_F = SM_I + ML_HEADS


def _dot(a, b):
    return jnp.dot(a, b, preferred_element_type=F32)


def _dot_nt(a, b):
    return lax.dot_general(a, b, _NT, preferred_element_type=F32)


def _split3(x):
    hi = x.astype(BF16)
    r1 = x - hi.astype(F32)
    mid = r1.astype(BF16)
    lo = (r1 - mid.astype(F32)).astype(BF16)
    return hi, mid, lo


def _rms(x, g):
    return x * lax.rsqrt(jnp.mean(x * x, axis=-1, keepdims=True) + EPS) * g


def _sigmoid(x):
    return 0.5 + 0.5 * jnp.tanh(0.5 * x)


def _const_spec(shape):
    nd = len(shape)
    return pl.BlockSpec(shape, lambda *_: (0,) * nd, pipeline_mode=pl.Buffered(1))


_C_Q = 0
_C_KC = _C_Q + NSA_HEADS * NSA_HD
_C_VC = _C_KC + LANES
_C_KS = _C_VC + LANES
_C_KW = _C_KS + LANES
_C_SM = _C_KW + LANES
_C_QKB = _C_SM + LANES
_C_VB = _C_QKB + 2 * ML_WIDTH
_C_OB = _C_VB + ML_WIDTH
_C_MG = _C_OB + ML_WIDTH
_C_END = _C_MG + 2 * D_MODEL


def _inproj_kernel(x_ref, g_ref, w_ref, wt_ref, cs_ref, spread_ref, unrot_ref,
                   q_ref, kc_ref, vc_ref, ks_ref, kw_ref, vst_ref, vwt_ref, sm_ref, smt_ref,
                   qkb_ref, vb_ref, ob_ref, mg_ref):
    hn = _rms(x_ref[...], g_ref[...]).astype(BF16)
    tables = sum(_dot(part, spread_ref[...]) for part in _split3(cs_ref[...]))
    rc = tables[:, 0:LANES] + unrot_ref[...]
    rp = tables[:, LANES:2 * LANES]
    rm = tables[:, 2 * LANES:3 * LANES]

    def rope(z):
        half = ROPE_DIM // 2
        return z * rc + pltpu.roll(z, half, 1) * rp + pltpu.roll(z, LANES - half, 1) * rm

    scale = NSA_HD ** -0.5 * math.log2(math.e)
    for h in range(NSA_HEADS * NSA_HD // LANES):
        z = _dot(hn, w_ref[:, _C_Q + h * LANES:_C_Q + (h + 1) * LANES])
        q_ref[:, h * LANES:(h + 1) * LANES] = (rope(z) * scale).astype(BF16)
    kc_ref[...] = rope(_dot(hn, w_ref[:, _C_KC:_C_KC + LANES])).astype(BF16)
    vc_ref[...] = _dot(hn, w_ref[:, _C_VC:_C_VC + LANES]).astype(BF16)
    ks_ref[...] = rope(_dot(hn, w_ref[:, _C_KS:_C_KS + LANES])).astype(BF16)
    kw_ref[...] = rope(_dot(hn, w_ref[:, _C_KW:_C_KW + LANES])).astype(BF16)
    sm_ref[...] = _dot(hn, w_ref[:, _C_SM:_C_SM + LANES])
    for c0 in range(0, 2 * ML_WIDTH, 512):
        qkb_ref[:, c0:c0 + 512] = _dot(hn, w_ref[:, _C_QKB + c0:_C_QKB + c0 + 512]).astype(BF16)
    vb_ref[...] = _dot(hn, w_ref[:, _C_VB:_C_VB + ML_WIDTH]).astype(BF16)
    ob_ref[...] = _dot(hn, w_ref[:, _C_OB:_C_OB + ML_WIDTH]).astype(BF16)
    for c0 in range(0, 2 * D_MODEL, 512):
        mg_ref[:, c0:c0 + 512] = _dot(hn, w_ref[:, _C_MG + c0:_C_MG + c0 + 512]).astype(BF16)
    zt = _dot_nt(wt_ref[...], hn)
    ones_rows = (lax.broadcasted_iota(jnp.int32, (VT_PAD, KC), 0) == 0).astype(BF16)
    for i in range(TD // KC):
        for ref, r0 in ((vst_ref, 0), (vwt_ref, LANES)):
            zc = zt[r0:r0 + LANES, i * KC:(i + 1) * KC].astype(BF16)
            ref[i] = jnp.concatenate([piece for g in range(NSA_KV)
                                      for piece in (zc[g * NSA_HD:(g + 1) * NSA_HD, :], ones_rows)], axis=0)
    smt_ref[...] = zt[2 * LANES:3 * LANES, :]


def _inproj(x2, g_mix, wcat, wtr, cs, spread, unrot):
    t = x2.shape[0]
    row = lambda w: pl.BlockSpec((TD, w), lambda i: (i, 0))
    out_shape = [
        jax.ShapeDtypeStruct((t, NSA_HEADS * NSA_HD), BF16),
        jax.ShapeDtypeStruct((t, LANES), BF16),
        jax.ShapeDtypeStruct((t, LANES), BF16),
        jax.ShapeDtypeStruct((t, LANES), BF16),
        jax.ShapeDtypeStruct((t, LANES), BF16),
        jax.ShapeDtypeStruct((t // KC, NSA_KV * VT_ROWS, KC), BF16),
        jax.ShapeDtypeStruct((t // KC, NSA_KV * VT_ROWS, KC), BF16),
        jax.ShapeDtypeStruct((t, LANES), F32),
        jax.ShapeDtypeStruct((LANES, t), F32),
        jax.ShapeDtypeStruct((t, 2 * ML_WIDTH), BF16),
        jax.ShapeDtypeStruct((t, ML_WIDTH), BF16),
        jax.ShapeDtypeStruct((t, ML_WIDTH), BF16),
        jax.ShapeDtypeStruct((t, 2 * D_MODEL), BF16),
    ]
    chunk3 = pl.BlockSpec((TD // KC, NSA_KV * VT_ROWS, KC), lambda i: (i, 0, 0))
    out_specs = [row(NSA_HEADS * NSA_HD), row(LANES), row(LANES), row(LANES), row(LANES), chunk3, chunk3,
                 row(LANES), pl.BlockSpec((LANES, TD), lambda i: (0, i)),
                 row(2 * ML_WIDTH), row(ML_WIDTH), row(ML_WIDTH), row(2 * D_MODEL)]
    return pl.pallas_call(
        _inproj_kernel,
        out_shape=out_shape,
        grid=(t // TD,),
        in_specs=[row(D_MODEL), _const_spec((1, D_MODEL)), _const_spec((D_MODEL, _C_END)),
                  _const_spec((3 * LANES, D_MODEL)), row(cs.shape[1]), _const_spec(spread.shape),
                  _const_spec(unrot.shape)],
        out_specs=out_specs,
        compiler_params=pltpu.CompilerParams(dimension_semantics=("arbitrary",), vmem_limit_bytes=VMEM_LIMIT),
        name="inproj",
    )(x2, g_mix, wcat, wtr, cs, spread, unrot)


def _gelu_tanh(x):
    return 0.5 * x * (1.0 + jnp.tanh(math.sqrt(2.0 / math.pi) * (x + 0.044715 * x * x * x)))


def _compress_kernel(rk_ref, rv_ref, wka_ref, wkb_ref, wva_ref, wvb_ref, pek_ref, pev_ref,
                     w1k_ref, w1v_ref, w2k_ref, w2vt_ref, kc_ref, vct_ref):
    nrow = rk_ref.shape[0]

    def hidden(r_ref, wa_ref, wb_ref, pe_ref, w1_ref):
        r = r_ref[...]
        ha = _dot(r, wa_ref[...])
        hb = _dot(r, wb_ref[...])
        hb = pltpu.roll(hb, nrow - 1, 0)
        c = _dot(pe_ref[...], w1_ref[...])[0:1, :]
        return [_gelu_tanh(ha[:, g * CMP_HIDDEN:(g + 1) * CMP_HIDDEN] + hb[:, g * CMP_HIDDEN:(g + 1) * CMP_HIDDEN] + c).astype(BF16)
                for g in range(NSA_KV)]

    ak = hidden(rk_ref, wka_ref, wkb_ref, pek_ref, w1k_ref)
    kc_ref[...] = (_dot(ak[0], w2k_ref[0]) + _dot(ak[1], w2k_ref[1])).astype(BF16)
    av = hidden(rv_ref, wva_ref, wvb_ref, pev_ref, w1v_ref)
    for g in range(NSA_KV):
        vct_ref[g * NSA_HD:(g + 1) * NSA_HD, :] = _dot_nt(w2vt_ref[...], av[g]).astype(BF16)


def _compress(rk, rv, wka, wkb, wva, wvb, pek, pev, w1k, w1v, w2k, w2vt):
    b, nrow, width = rk.shape
    blk = pl.BlockSpec((None, nrow, width), lambda i: (i, 0, 0))
    return pl.pallas_call(
        _compress_kernel,
        out_shape=[jax.ShapeDtypeStruct((b, nrow, LANES), BF16),
                   jax.ShapeDtypeStruct((b, LANES, nrow), BF16)],
        grid=(b,),
        in_specs=[blk, blk] + [_const_spec(a.shape) for a in (wka, wkb, wva, wvb, pek, pev, w1k, w1v, w2k, w2vt)],
        out_specs=[pl.BlockSpec((None, nrow, LANES), lambda i: (i, 0, 0)),
                   pl.BlockSpec((None, LANES, nrow), lambda i: (i, 0, 0))],
        compiler_params=pltpu.CompilerParams(dimension_semantics=("arbitrary",), vmem_limit_bytes=VMEM_LIMIT),
        name="compress",
    )(rk, rv, wka, wkb, wva, wvb, pek, pev, w1k, w1v, w2k, w2vt)


def _nsa_kernel(q_ref, kc_ref, vct_ref, ks_ref, kw_ref, vst_ref, vwt_ref, smt_ref, o_ref, bias_scr, sx_scr, sy_scr):
    c = pl.program_id(1)
    t0 = c * QT
    ncmp = kc_ref.shape[0]
    nsel = bias_scr.shape[0]
    nw = WINDOW // KC + 1
    gw = NSA_HPG * QT
    width = NSA_KV * gw

    def per_group(x):
        return [x[:, g * gw:(g + 1) * gw] for g in range(NSA_KV)]

    def pv(vt, p):
        rows = vt.shape[0] // NSA_KV
        pb = p.astype(BF16)
        return jnp.concatenate([_dot(vt[g * rows:(g + 1) * rows, :], pg) for g, pg in enumerate(per_group(pb))],
                               axis=1)

    def normalised(acc):
        return acc[0:NSA_HD, :] / acc[NSA_HD:NSA_HD + 1, :]

    low_half = lax.broadcasted_iota(jnp.int32, (1, LANES), 1) < NSA_HD
    q_heads = []
    for h in range(NSA_HEADS):
        pair = q_ref[:, (h // 2) * LANES:(h // 2 + 1) * LANES].astype(F32)
        want_low = h // NSA_HPG == 0
        if (h % 2 == 0) != want_low:
            pair = pltpu.roll(pair, NSA_HD, 1)
        q_heads.append(jnp.where(low_half if want_low else ~low_half, pair, 0.0).astype(BF16))
    qs = jnp.concatenate(q_heads, axis=0)
    u_row = lax.broadcasted_iota(jnp.int32, (1, width), 1) % QT
    t_row = t0 + u_row
    r_kc = lax.broadcasted_iota(jnp.int32, (KC, 1), 0)

    n_grp = ks_ref.shape[0] // SEL_GROUP
    n_full = lax.shift_right_logical(t0, int(math.log2(SEL_GROUP)))

    def qk_group(j):
        return _dot_nt(ks_ref[pl.ds(pl.multiple_of(j * SEL_GROUP, SEL_GROUP), SEL_GROUP), :], qs)

    sc = _dot_nt(kc_ref[...], qs)

    w_slabs, w_chunks = [], []
    for i in range(nw):
        jj = c - (nw - 1) + i
        jc = jnp.maximum(jj, 0)
        si = _dot_nt(kw_ref[pl.ds(pl.multiple_of(jc * KC, KC), KC), :], qs)
        if i == 0:
            keep = jnp.where(jj >= 0, r_kc, -1) > u_row
        elif i == nw - 1:
            keep = r_kc <= u_row
        else:
            keep = jj >= 0
        w_slabs.append(jnp.where(keep, si, NEG))
        w_chunks.append(jc)

    n_col = lax.broadcasted_iota(jnp.int32, (ncmp, 1), 0)
    last_tok = jnp.where(n_col < ncmp - 1, CMP_STRIDE * n_col + (CMP_LEN - 1), jnp.iinfo(jnp.int32).max)
    s = jnp.where(last_tok <= t_row, sc, NEG)
    m = jnp.max(s, axis=0, keepdims=True)
    e = jnp.exp2(s - m)
    anyv = (t_row >= CMP_LEN - 1).astype(F32)
    p = e * (anyv / jnp.sum(e, axis=0, keepdims=True))
    o_cmp = pv(vct_ref[...], p)

    psums = []
    for pg in per_group(p):
        acc_p = pg[:, 0:QT]
        for h in range(1, NSA_HPG):
            acc_p = acc_p + pg[:, h * QT:(h + 1) * QT]
        psums.append(acc_p)
    psum = jnp.concatenate(psums, axis=1)
    nq2 = NSA_KV * QT
    s_col = lax.broadcasted_iota(jnp.int32, (nsel, 1), 0)
    n_lane = lax.broadcasted_iota(jnp.int32, (1, ncmp), 1)
    ov = ((CMP_STRIDE * n_lane < SEL_BLOCK * (s_col + 1)) & (CMP_STRIDE * n_lane + (CMP_LEN - 1) >= SEL_BLOCK * s_col)
          ).astype(BF16)
    imp = sum(_dot(ov, part) for part in _split3(psum))

    s_diag = qk_group(n_full)
    s_own = _dot_nt(ks_ref[pl.ds(pl.multiple_of(t0, KC), KC), :], qs)
    sx_scr[...] = qk_group(0)

    mxw = w_slabs[0]
    for sl in w_slabs[1:]:
        mxw = jnp.maximum(mxw, sl)
    mw = jnp.max(mxw, axis=0, keepdims=True)
    acc_w = jnp.zeros((VT_ROWS, width), F32)
    for sl, jc in zip(w_slabs, w_chunks):
        acc_w = acc_w + pv(vwt_ref[jc], jnp.exp2(sl - mw))
    o_win = normalised(acc_w)

    t1 = t0 + lax.broadcasted_iota(jnp.int32, (1, nq2), 1) % QT
    cur = lax.shift_right_logical(t1, 6)
    forced = (s_col == 0) | (s_col == cur) | (s_col == cur - 1)
    valid = SEL_BLOCK * s_col <= t1
    val = jnp.where(valid, jnp.where(forced, imp + SEL_FORCE, imp), NEG)
    sub = 8
    r_sub = lax.broadcasted_iota(jnp.int32, (sub, 1), 0)
    blocks = [val[r * sub:(r + 1) * sub, :] for r in range(nsel // sub)]
    ranks = [jnp.zeros((sub, nq2), F32) for _ in blocks]
    for i in range(nsel):
        vi = val[i:i + 1, :]
        for r, blk in enumerate(blocks):
            if i < r * sub:
                beats = vi >= blk
            elif i >= (r + 1) * sub:
                beats = vi > blk
            else:
                beats = (vi > blk) | ((vi == blk) & (r_sub > i - r * sub))
            ranks[r] = ranks[r] + jnp.where(beats, 1.0, 0.0)
    bias = jnp.where(jnp.concatenate(ranks, axis=0) < SEL_TOPK, 0.0, NEG).astype(F32)
    bias_scr[...] = jnp.concatenate([bias[:, g * QT:(g + 1) * QT] for g in range(NSA_KV) for _ in range(NSA_HPG)],
                                    axis=1)

    blk_per_grp = SEL_GROUP // SEL_BLOCK
    chunk_per_grp = SEL_GROUP // KC
    blk_per_chunk = KC // SEL_BLOCK

    def sel_update(j, sj, carry):
        m_o, acc = carry
        brows = [bias_scr[pl.ds(blk_per_grp * j + i, 1), :] for i in range(blk_per_grp)]

        def block(i):
            return sj[i * SEL_BLOCK:(i + 1) * SEL_BLOCK, :]

        mx = None
        for i in range(blk_per_grp):
            sl = block(i) + brows[i]
            mx = sl if mx is None else jnp.maximum(mx, sl)
        m_n = jnp.maximum(m_o, jnp.max(mx, axis=0, keepdims=True))
        a = jnp.exp2(m_o - m_n)
        acc = a * acc
        for ci in range(chunk_per_grp):
            parts = [jnp.exp2(block(i) + (brows[i] - m_n))
                     for i in range(blk_per_chunk * ci, blk_per_chunk * (ci + 1))]
            acc = acc + pv(vst_ref[chunk_per_grp * j + ci], jnp.concatenate(parts, axis=0))
        return m_n, acc

    def seed_state():
        first_own = blk_per_chunk * (c % chunk_per_grp)
        brows = [bias_scr[pl.ds(blk_per_grp * n_full + i, 1), :] + jnp.where(i < first_own, 0.0, NEG)
                 for i in range(blk_per_grp)]
        own = jnp.where(r_kc <= u_row, s_own, NEG)
        mx = jnp.maximum(own[0:SEL_BLOCK, :], own[SEL_BLOCK:KC, :])
        for i in range(blk_per_grp):
            mx = jnp.maximum(mx, s_diag[i * SEL_BLOCK:(i + 1) * SEL_BLOCK, :] + brows[i])
        m_n = jnp.max(mx, axis=0, keepdims=True)
        acc = pv(vst_ref[c], jnp.exp2(own - m_n))
        for ci in range(chunk_per_grp):
            parts = [jnp.exp2(s_diag[i * SEL_BLOCK:(i + 1) * SEL_BLOCK, :] + (brows[i] - m_n))
                     for i in range(blk_per_chunk * ci, blk_per_chunk * (ci + 1))]
            acc = acc + pv(vst_ref[chunk_per_grp * n_full + ci], jnp.concatenate(parts, axis=0))
        return m_n, acc

    seeded = seed_state()

    def pair_body(jp, carry):
        ja, jb = 2 * jp, 2 * jp + 1
        sy_scr[...] = qk_group(jb)
        carry = sel_update(ja, sx_scr, carry)
        sx_scr[...] = qk_group(jnp.minimum(ja + 2, n_grp - 1))
        return sel_update(jb, sy_scr, carry)

    n_pairs = lax.shift_right_logical(n_full, 1)
    carry = lax.fori_loop(0, n_pairs, pair_body, seeded)
    _, acc_s = lax.cond(n_full - 2 * n_pairs == 1, lambda cr: sel_update(n_full - 1, sx_scr, cr), lambda cr: cr,
                        carry)
    o_sel = normalised(acc_s)

    def gate_row(br):
        rows = [smt_ref[SM_GATE + 3 * h + br:SM_GATE + 3 * h + br + 1, :] for h in range(NSA_HEADS)]
        return _sigmoid(jnp.concatenate(rows, axis=1))

    o_t = gate_row(0) * o_cmp + gate_row(1) * o_sel + gate_row(2) * o_win
    for pr in range(NSA_HEADS // 2):
        xp = jnp.concatenate([o_t[:, (2 * pr) * QT:(2 * pr + 1) * QT], o_t[:, (2 * pr + 1) * QT:(2 * pr + 2) * QT]], axis=0)
        o_ref[:, pr * LANES:(pr + 1) * LANES] = xp.T.astype(BF16)


def _nsa(qpad, kcb, vct, ks, kw, vst, vwt, smt, b, s):
    nq = s // QT
    ncmp = kcb.shape[1]
    return pl.pallas_call(
        _nsa_kernel,
        out_shape=jax.ShapeDtypeStruct((b * s, NSA_HEADS * NSA_HD), BF16),
        grid=(b, nq),
        in_specs=[
            pl.BlockSpec((QT, NSA_HEADS * NSA_HD), lambda bi, c: (bi * nq + c, 0)),
            pl.BlockSpec((None, ncmp, LANES), lambda bi, c: (bi, 0, 0)),
            pl.BlockSpec((None, NSA_KV * NSA_HD, ncmp), lambda bi, c: (bi, 0, 0)),
            pl.BlockSpec((s, LANES), lambda bi, c: (bi, 0)),
            pl.BlockSpec((s, LANES), lambda bi, c: (bi, 0)),
            pl.BlockSpec((s // KC, NSA_KV * VT_ROWS, KC), lambda bi, c: (bi, 0, 0)),
            pl.BlockSpec((s // KC, NSA_KV * VT_ROWS, KC), lambda bi, c: (bi, 0, 0)),
            pl.BlockSpec((LANES, QT), lambda bi, c: (0, bi * nq + c)),
        ],
        out_specs=pl.BlockSpec((QT, NSA_HEADS * NSA_HD), lambda bi, c: (bi * nq + c, 0)),
        scratch_shapes=[pltpu.VMEM((s // SEL_BLOCK, NSA_HEADS * QT), F32),
                        pltpu.VMEM((SEL_GROUP, NSA_HEADS * QT), F32),
                        pltpu.VMEM((SEL_GROUP, NSA_HEADS * QT), F32)],
        compiler_params=pltpu.CompilerParams(dimension_semantics=("arbitrary", "arbitrary"),
                                             vmem_limit_bytes=VMEM_LIMIT),
        name="nsa",
    )(qpad, kcb, vct, ks, kw, vst, vwt, smt)


def _log_sigmoid(x):
    return jnp.minimum(x, 0.0) - jnp.log(1.0 + jnp.exp(-jnp.abs(x)))


def _mlstm_kernel(qk_ref, v_ref, og_ref, sm_ref, smt_ref, wc_ref, bc_ref, bifc_ref, bifr_ref, ghn_ref,
                  y_ref, tail_scr, ct_scr, n_scr, m_scr):
    lc = ML_CHUNK

    @pl.when(pl.program_id(1) == 0)
    def _():
        tail_scr[...] = jnp.zeros_like(tail_scr)
        ct_scr[...] = jnp.zeros_like(ct_scr)
        n_scr[...] = jnp.zeros_like(n_scr)
        m_scr[...] = jnp.zeros_like(m_scr)

    u = qk_ref[...]
    tail = tail_scr[...]
    rr8 = lax.broadcasted_iota(jnp.int32, (SUBLANES, 1), 0)
    sr = lax.broadcasted_iota(jnp.int32, (ML_BLOCK, ML_BLOCK), 0)
    sc_ = lax.broadcasted_iota(jnp.int32, (ML_BLOCK, ML_BLOCK), 1)
    y = bc_ref[...] + wc_ref[CONV_W - 1:CONV_W, :] * u.astype(F32)
    for k in range(1, CONV_W):
        down = _dot((sr - sc_ == k).astype(BF16), u)
        head = jnp.where(rr8 < k, pltpu.roll(tail, k, 0), down[0:SUBLANES, :])
        y = y + wc_ref[CONV_W - 1 - k:CONV_W - k, :] * jnp.concatenate([head, down[SUBLANES:, :]], axis=0)
    tail_scr[...] = u[ML_BLOCK - SUBLANES:ML_BLOCK, :].astype(F32)
    qkc = y * _sigmoid(y)
    q_all = qkc[:, 0:ML_WIDTH].astype(BF16)
    k_all = (qkc[:, ML_WIDTH:2 * ML_WIDTH] * (ML_HD ** -0.5)).astype(BF16)

    ifc = sm_ref[...] + bifc_ref[...]
    ifr = smt_ref[...] + bifr_ref[...]
    lfc = _log_sigmoid(ifc)
    lfr = _log_sigmoid(ifr)
    rr = lax.broadcasted_iota(jnp.int32, (lc, lc), 0)
    cc = lax.broadcasted_iota(jnp.int32, (lc, lc), 1)
    causal = rr >= cc
    tri_l = causal.astype(F32)
    tri_u = (rr <= cc).astype(F32)

    for ci in range(ML_BLOCK // lc):
        lo, hi = ci * lc, (ci + 1) * lc
        bc_all = jnp.dot(tri_l, lfc[lo:hi, :], preferred_element_type=F32, precision=lax.Precision.HIGHEST)
        br_all = jnp.dot(lfr[:, lo:hi], tri_u, preferred_element_type=F32, precision=lax.Precision.HIGHEST)
        heads = range(ML_HEADS)
        hsl = [slice(h * ML_HD, (h + 1) * ML_HD) for h in heads]
        bcol = [bc_all[:, SM_F + h:SM_F + h + 1] for h in heads]
        brow = [br_all[SM_F + h:SM_F + h + 1, :] for h in heads]
        icol = [ifc[lo:hi, SM_I + h:SM_I + h + 1] for h in heads]
        irow = [ifr[SM_I + h:SM_I + h + 1, lo:hi] for h in heads]
        mprev = [m_scr[h][:, 0:1] for h in heads]
        qh = [q_all[lo:hi, hsl[h]] for h in heads]
        kh = [k_all[lo:hi, hsl[h]] for h in heads]
        vh = [v_ref[lo:hi, hsl[h]] for h in heads]
        ct = [ct_scr[h] for h in heads]
        nrow = [n_scr[h] for h in heads]
        qk = [_dot_nt(qh[h], kh[h]) for h in heads]
        qc = [_dot(qh[h], ct[h].astype(BF16)) for h in heads]
        dmat = [jnp.where(causal, bcol[h] - brow[h] + irow[h], NEG) for h in heads]
        inter = [bcol[h] + mprev[h] for h in heads]
        mt = [jnp.maximum(jnp.max(dmat[h], axis=-1, keepdims=True), inter[h]) for h in heads]
        a = [jnp.exp(dmat[h] - mt[h]) * qk[h] for h in heads]
        dec = [jnp.exp(inter[h] - mt[h]) for h in heads]
        num = [_dot(a[h].astype(BF16), vh[h]) + dec[h] * qc[h] for h in heads]
        den = [jnp.sum(a[h], axis=-1, keepdims=True)
               + dec[h] * jnp.sum(qh[h].astype(F32) * nrow[h], axis=-1, keepdims=True) for h in heads]
        blast = [bcol[h][lc - 1:lc, :] for h in heads]
        mnew = [jnp.maximum(blast[h] + mprev[h], jnp.max(blast[h] - brow[h] + irow[h], axis=-1, keepdims=True))
                for h in heads]
        wprev = [jnp.exp(blast[h] + mprev[h] - mnew[h]) for h in heads]
        kwt = [kh[h].astype(F32) * jnp.exp(blast[h] - bcol[h] + icol[h] - mnew[h]) for h in heads]
        for h in heads:
            ct_scr[h] = wprev[h] * ct[h] + lax.dot_general(kwt[h].astype(BF16), vh[h], _TN,
                                                           preferred_element_type=F32)
            n_scr[h] = wprev[h] * nrow[h] + jnp.sum(kwt[h], axis=0, keepdims=True)
            m_scr[h] = jnp.broadcast_to(mnew[h], (1, LANES))
        hm = [num[h] / jnp.maximum(jnp.abs(den[h]), jnp.exp(-mt[h])) * _sigmoid(og_ref[lo:hi, hsl[h]].astype(F32))
              for h in heads]
        for h in heads:
            y_ref[lo:hi, hsl[h]] = _rms(hm[h], ghn_ref[:, hsl[h]]).astype(BF16)


def _mlstm(qkb, vb, ob, sm, smt, wconv, bconv, bifc, bifr, ghn, b, s):
    nb = s // ML_BLOCK
    row = lambda w: pl.BlockSpec((ML_BLOCK, w), lambda bi, j: (bi * nb + j, 0))
    return pl.pallas_call(
        _mlstm_kernel,
        out_shape=jax.ShapeDtypeStruct((b * s, ML_WIDTH), BF16),
        grid=(b, nb),
        in_specs=[row(2 * ML_WIDTH), row(ML_WIDTH), row(ML_WIDTH), row(LANES),
                  pl.BlockSpec((LANES, ML_BLOCK), lambda bi, j: (0, bi * nb + j)),
                  _const_spec(wconv.shape), _const_spec(bconv.shape), _const_spec(bifc.shape),
                  _const_spec(bifr.shape), _const_spec(ghn.shape)],
        out_specs=row(ML_WIDTH),
        scratch_shapes=[pltpu.VMEM((SUBLANES, 2 * ML_WIDTH), F32),
                        pltpu.VMEM((ML_HEADS, ML_HD, ML_HD), F32),
                        pltpu.VMEM((ML_HEADS, 1, ML_HD), F32),
                        pltpu.VMEM((ML_HEADS, 1, LANES), F32)],
        compiler_params=pltpu.CompilerParams(dimension_semantics=("arbitrary", "arbitrary"),
                                             vmem_limit_bytes=VMEM_LIMIT),
        name="mlstm",
    )(qkb, vb, ob, sm, smt, wconv, bconv, bifc, bifr, ghn)


RT_BUCKET = N_EXPERTS
RT_RANK = N_EXPERTS + 1
RT_WLO = N_EXPERTS + 2
RT_WHI = N_EXPERTS + 3
N_BUCKETS = N_GROUPS * 6
X_ROWS = D_MODEL // LANES
REC = 2 * X_ROWS
DMA_UNROLL = 8
DISPATCH_SUB = 32
COMBINE_PARTS = 2


def _merge_kernel(ya_ref, yb_ref, mg_ref, x_ref, wpa_ref, wpb_ref, wout_ref, gffn_ref, wr_ref, br_ref,
                  x1_ref, h2_ref, slab_ref, hist_ref):
    halves = [slice(i * (TD // 2), (i + 1) * (TD // 2)) for i in range(2)]
    pa = [_dot(ya_ref[hs, :], wpa_ref[...]) for hs in halves]
    pb = [_dot(yb_ref[hs, :], wpb_ref[...]) for hs in halves]
    mixed = [(_sigmoid(mg_ref[hs, 0:D_MODEL].astype(F32)) * pa[i]
              + _sigmoid(mg_ref[hs, D_MODEL:2 * D_MODEL].astype(F32)) * pb[i]).astype(BF16)
             for i, hs in enumerate(halves)]
    x1 = [x_ref[hs, :] + _dot(mixed[i], wout_ref[...]) for i, hs in enumerate(halves)]
    h2 = [_rms(x1[i], gffn_ref[...]) for i in range(2)]
    for i, hs in enumerate(halves):
        x1_ref[hs, :] = x1[i]
        h2_ref[hs, :] = h2[i]

    h_hi = [h.astype(BF16) for h in h2]
    h_lo = [(h2[i] - h_hi[i].astype(F32)).astype(BF16) for i in range(2)]
    r_hi = [_dot(h, wr_ref[...]) for h in h_hi]
    logit = jnp.concatenate([r_hi[i][:, 0:LANES] + r_hi[i][:, LANES:2 * LANES] + _dot(h_lo[i], wr_ref[:, 0:LANES])
                             for i in range(2)], axis=0) + br_ref[...]
    lane = lax.broadcasted_iota(jnp.int32, logit.shape, 1)
    big = jnp.int32(LANES)
    gmask = (lane >= N_EXPERTS) & (lane < N_EXPERTS + N_GROUPS)
    gl = jnp.where(gmask, logit, NEG)
    gmax = jnp.max(gl, axis=-1, keepdims=True)
    gidx = jnp.min(jnp.where(gmask & (gl == gmax), lane, big), axis=-1, keepdims=True) - N_EXPERTS
    pg_sel = 1.0 / jnp.sum(jnp.where(gmask, jnp.exp(gl - gmax), 0.0), axis=-1, keepdims=True)
    emask = (lane < N_EXPERTS) & (lax.shift_right_logical(lane, 2) == gidx)
    el = jnp.where(emask, logit, NEG)
    e1 = jnp.max(el, axis=-1, keepdims=True)
    i1 = jnp.min(jnp.where(emask & (el == e1), lane, big), axis=-1, keepdims=True)
    emask2 = emask & (lane != i1)
    el2 = jnp.where(emask2, logit, NEG)
    e2 = jnp.max(el2, axis=-1, keepdims=True)
    i2 = jnp.min(jnp.where(emask2 & (el2 == e2), lane, big), axis=-1, keepdims=True)
    x21 = jnp.exp(e2 - e1)
    w1 = pg_sel / (1.0 + x21)
    w2 = pg_sel * x21 / (1.0 + x21)
    first_lo = i1 < i2
    e_lo = jnp.where(first_lo, i1, i2) - EXP_PER_GROUP * gidx
    e_hi = jnp.where(first_lo, i2, i1) - EXP_PER_GROUP * gidx
    pair = lax.shift_right_logical(e_lo * (2 * EXP_PER_GROUP - 1 - e_lo), 1) + (e_hi - e_lo - 1)
    bucket = 6 * gidx + pair
    member = lane == bucket
    onehot = jnp.where(member, 1.0, 0.0)
    rr = lax.broadcasted_iota(jnp.int32, (TD, TD), 0)
    cc = lax.broadcasted_iota(jnp.int32, (TD, TD), 1)
    earlier = _dot((rr > cc).astype(BF16), onehot.astype(BF16))
    rank = jnp.sum(jnp.where(member, earlier, 0.0), axis=-1, keepdims=True)
    slab = jnp.where(lane == i1, w1, 0.0) + jnp.where(lane == i2, w2, 0.0)
    slab = jnp.where(lane == RT_BUCKET, bucket.astype(F32), slab)
    slab = jnp.where(lane == RT_RANK, rank, slab)
    slab = jnp.where(lane == RT_WLO, jnp.where(first_lo, w1, w2), slab)
    slab = jnp.where(lane == RT_WHI, jnp.where(first_lo, w2, w1), slab)
    slab_ref[...] = slab
    hist_ref[...] = jnp.broadcast_to(jnp.sum(onehot, axis=0, keepdims=True), (SUBLANES, LANES))


def _merge(ya, yb, mg, x2, wpa, wpb, wout, gffn, wr, br):
    t = x2.shape[0]
    row = lambda w: pl.BlockSpec((TD, w), lambda i: (i, 0))
    return pl.pallas_call(
        _merge_kernel,
        out_shape=[jax.ShapeDtypeStruct((t, D_MODEL), F32),
                   jax.ShapeDtypeStruct((t, D_MODEL), F32),
                   jax.ShapeDtypeStruct((t, LANES), F32),
                   jax.ShapeDtypeStruct((t // TD * SUBLANES, LANES), F32)],
        grid=(t // TD,),
        in_specs=[row(NSA_HEADS * NSA_HD), row(ML_WIDTH), row(2 * D_MODEL), row(D_MODEL)]
                 + [_const_spec(a.shape) for a in (wpa, wpb, wout, gffn, wr, br)],
        out_specs=[row(D_MODEL), row(D_MODEL), row(LANES),
                   pl.BlockSpec((SUBLANES, LANES), lambda i: (i, 0))],
        compiler_params=pltpu.CompilerParams(dimension_semantics=("arbitrary",), vmem_limit_bytes=VMEM_LIMIT),
        name="merge",
    )(ya, yb, mg, x2, wpa, wpb, wout, gffn, wr, br)


def _rec_copy(src_ref, src_tok, dst_ref, dst_tok, sem, rows):
    src = src_ref.at[pl.ds(pl.multiple_of(src_tok * rows, rows), rows), :]
    dst = dst_ref.at[pl.ds(pl.multiple_of(dst_tok * rows, rows), rows), :]
    return pltpu.make_async_copy(src, dst, sem)


def _token_copies(n, make, wait=False):
    def body(g, carry):
        for u in range(DMA_UNROLL):
            cp = make(g * DMA_UNROLL + u)
            if wait:
                cp.wait()
            else:
                cp.start(priority=u % 2)
        return carry
    lax.fori_loop(0, n // DMA_UNROLL, body, 0)


def _dispatch_kernel(pos_ref, tail_ref, h2_ref, slab_ref, out_ref, stage_scr, zero_scr, sem, zsem):
    i = pl.program_id(0)
    slot = lax.rem(i, 2)
    n_tiles = out_ref.shape[0] // (TM * REC)
    n_used = tail_ref[2 * N_BUCKETS]

    def zero_copy(first_slot):
        start = pl.multiple_of(first_slot * REC, TM * REC)
        return pltpu.make_async_copy(zero_scr, out_ref.at[pl.ds(start, TM * REC), :], zsem)

    @pl.when(i == 0)
    def _():
        zero_scr[...] = jnp.zeros_like(zero_scr)
        for phase in ("start", "wait"):
            for b in range(N_BUCKETS):
                @pl.when(tail_ref[N_BUCKETS + b] > 0)
                def _():
                    getattr(zero_copy(tail_ref[b]), phase)()

                @pl.when(n_used + b < n_tiles)
                def _():
                    getattr(zero_copy((n_used + b) * TM), phase)()

    stage = stage_scr.at[slot]
    base = i * TM
    for t0 in range(0, TM, DISPATCH_SUB):
        for j in range(REC):
            if j < X_ROWS:
                rows = h2_ref[t0:t0 + DISPATCH_SUB, j * LANES:(j + 1) * LANES]
            elif j == X_ROWS:
                rows = slab_ref[t0:t0 + DISPATCH_SUB, :]
            else:
                rows = jnp.zeros((DISPATCH_SUB, LANES), F32)
            stage[pl.ds(t0 * REC + j, DISPATCH_SUB, stride=REC), :] = rows
        for r in range(t0, t0 + DISPATCH_SUB):
            _rec_copy(stage, r, out_ref, pos_ref[base + r], sem.at[slot], REC).start(priority=r % 2)

    def drain(which):
        _token_copies(TM, lambda r: _rec_copy(stage_scr.at[which], 0, out_ref, 0, sem.at[which], REC), wait=True)

    @pl.when(i > 0)
    def _():
        drain(1 - slot)

    @pl.when(i == pl.num_programs(0) - 1)
    def _():
        drain(slot)


def _dispatch(pos, tail, h2, slab, n_slots):
    t = h2.shape[0]
    return pl.pallas_call(
        _dispatch_kernel,
        out_shape=jax.ShapeDtypeStruct((n_slots * REC, LANES), F32),
        grid_spec=pltpu.PrefetchScalarGridSpec(
            num_scalar_prefetch=2,
            grid=(t // TM,),
            in_specs=[pl.BlockSpec((TM, D_MODEL), lambda i, pos_r, tail_r: (i, 0)),
                      pl.BlockSpec((TM, LANES), lambda i, pos_r, tail_r: (i, 0))],
            out_specs=pl.BlockSpec(memory_space=pl.ANY),
            scratch_shapes=[pltpu.VMEM((2, TM * REC, LANES), F32), pltpu.VMEM((TM * REC, LANES), F32),
                            pltpu.SemaphoreType.DMA((2,)), pltpu.SemaphoreType.DMA(())],
        ),
        compiler_params=pltpu.CompilerParams(dimension_semantics=("arbitrary",), vmem_limit_bytes=VMEM_LIMIT,
                                             has_side_effects=True),
        name="dispatch",
    )(pos, tail, h2, slab)


MOE_TILES = 2


def _moe_kernel(te_ref, nu_ref, hx_ref, w13_ref, w2_ref, y_ref):
    step = pl.program_id(0)
    n_tiles = pl.num_programs(0) * MOE_TILES
    n_used = nu_ref[0]

    @pl.when(step * MOE_TILES < n_used)
    def _():
        subs = range(MOE_TILES)
        hs = [jnp.concatenate([hx_ref[pl.ds(sub * TM * REC + j, TM, stride=REC), :] for j in range(X_ROWS)],
                              axis=1).astype(BF16) for sub in subs]
        slabs = [hx_ref[pl.ds(sub * TM * REC + X_ROWS, TM, stride=REC), :] for sub in subs]
        ys = [None] * MOE_TILES
        for side, lane in ((0, RT_WLO), (1, RT_WHI)):
            es = [te_ref[side * n_tiles + step * MOE_TILES + sub] for sub in subs]
            up = [_dot(hs[sub], w13_ref[es[sub]]) for sub in subs]
            act = [(up[sub][:, 0:D_EXPERT] * _sigmoid(up[sub][:, 0:D_EXPERT]) * up[sub][:, D_EXPERT:2 * D_EXPERT]
                    * slabs[sub][:, lane:lane + 1]).astype(BF16) for sub in subs]
            for sub in subs:
                part = _dot(act[sub], w2_ref[es[sub]])
                ys[sub] = part if ys[sub] is None else ys[sub] + part
        for sub in subs:
            for j in range(X_ROWS):
                y_ref[pl.ds(sub * TM * X_ROWS + j, TM, stride=X_ROWS), :] = ys[sub][:, j * LANES:(j + 1) * LANES]

    @pl.when(step * MOE_TILES >= n_used)
    def _():
        y_ref[...] = jnp.zeros_like(y_ref)


def _moe(tile_e, n_used, hx_sorted, w13, w2):
    n_tiles = hx_sorted.shape[0] // (TM * REC)
    rows = MOE_TILES * TM
    last = lambda nu: (nu[0] - 1) // MOE_TILES
    return pl.pallas_call(
        _moe_kernel,
        out_shape=jax.ShapeDtypeStruct((n_tiles * TM * X_ROWS, LANES), F32),
        grid_spec=pltpu.PrefetchScalarGridSpec(
            num_scalar_prefetch=2,
            grid=(n_tiles // MOE_TILES,),
            in_specs=[pl.BlockSpec((rows * REC, LANES), lambda k, te, nu: (jnp.minimum(k, last(nu)), 0)),
                      pl.BlockSpec(w13.shape, lambda k, te, nu: (0, 0, 0), pipeline_mode=pl.Buffered(1)),
                      pl.BlockSpec(w2.shape, lambda k, te, nu: (0, 0, 0), pipeline_mode=pl.Buffered(1))],
            out_specs=pl.BlockSpec((rows * X_ROWS, LANES), lambda k, te, nu: (k, 0)),
        ),
        compiler_params=pltpu.CompilerParams(dimension_semantics=("arbitrary",), vmem_limit_bytes=VMEM_LIMIT),
        name="moe",
    )(tile_e, n_used, hx_sorted, w13, w2)


def _combine_kernel(pos_ref, y_ref, x1_ref, p_ref, gple_ref, wpg_ref, wpp_ref, gfin_ref, o_ref, ybuf, sem):
    i = pl.program_id(0)
    slot = lax.rem(i, 2)

    last = pl.num_programs(0) - 1

    def drain(which):
        _token_copies(TD, lambda r: _rec_copy(y_ref, 0, ybuf.at[which], 0, sem.at[which], X_ROWS), wait=True)

    @pl.when(i == 0)
    def _():
        _token_copies(TD, lambda r: _rec_copy(y_ref, pos_ref[r], ybuf.at[0], r, sem.at[0], X_ROWS))

    drain(slot)
    nxt = jnp.minimum(i + 1, last) * TD
    yb = ybuf.at[slot]
    part = TD // COMBINE_PARTS
    for q in range(COMBINE_PARTS):
        rows = slice(q * part, (q + 1) * part)
        y = jnp.concatenate([yb[pl.ds(q * part * X_ROWS + j, part, stride=X_ROWS), :] for j in range(X_ROWS)],
                            axis=1)
        x2 = x1_ref[rows, :] + y
        h3 = _rms(x2, gple_ref[...]).astype(BF16)
        x3 = x2 + _sigmoid(_dot(h3, wpg_ref[...])) * _dot(p_ref[rows, :].astype(BF16), wpp_ref[...])
        o_ref[rows, :] = _rms(x3, gfin_ref[...])
        for r in range(q * part, (q + 1) * part):
            _rec_copy(y_ref, pos_ref[nxt + r], ybuf.at[1 - slot], r, sem.at[1 - slot], X_ROWS).start(
                priority=r % 2)

    @pl.when(i == last)
    def _():
        drain(1 - slot)


def _combine(pos, y_sorted, x1, p2, gple, wpg, wpp, gfin):
    t = x1.shape[0]
    row = lambda w: pl.BlockSpec((TD, w), lambda i, pos_r: (i, 0))
    const = lambda a: pl.BlockSpec(a.shape, lambda i, pos_r: (0,) * a.ndim, pipeline_mode=pl.Buffered(1))
    return pl.pallas_call(
        _combine_kernel,
        out_shape=jax.ShapeDtypeStruct((t, D_MODEL), F32),
        grid_spec=pltpu.PrefetchScalarGridSpec(
            num_scalar_prefetch=1,
            grid=(t // TD,),
            in_specs=[pl.BlockSpec(memory_space=pl.ANY), row(D_MODEL), row(PLE_DIM),
                      const(gple), const(wpg), const(wpp), const(gfin)],
            out_specs=row(D_MODEL),
            scratch_shapes=[pltpu.VMEM((2, TD * X_ROWS, LANES), F32), pltpu.SemaphoreType.DMA((2,))],
        ),
        compiler_params=pltpu.CompilerParams(dimension_semantics=("arbitrary",), vmem_limit_bytes=VMEM_LIMIT),
        name="combine",
    )(pos, y_sorted, x1, p2, gple, wpg, wpp, gfin)


def _routing_tables(slab, hist8):
    t = slab.shape[0]
    nt = t // TD
    n_tiles = t // TM + N_BUCKETS
    hist = hist8.reshape(nt, SUBLANES, LANES)[:, 0, :]
    counts = jnp.sum(hist, axis=0)
    padded = jnp.ceil(counts / TM) * TM
    ends = jnp.cumsum(padded)
    first = (ends - padded)[None, :] + jnp.cumsum(hist, axis=0) - hist
    lane = jnp.arange(N_BUCKETS, dtype=F32)[None, :]
    mine = lane == slab[:, RT_BUCKET:RT_BUCKET + 1]
    pos = jnp.sum(jnp.where(mine, jnp.repeat(first[:, :N_BUCKETS], TD, axis=0), 0.0), axis=1) + slab[:, RT_RANK]
    starts = jnp.arange(n_tiles, dtype=F32) * TM
    tile_bucket = jnp.minimum(jnp.sum(ends[None, :N_BUCKETS] <= starts[:, None], axis=1), N_BUCKETS - 1)
    group, pair = tile_bucket // 6, tile_bucket % 6
    e_lo = EXP_PER_GROUP * group + jnp.array([0, 0, 0, 1, 1, 2], jnp.int32)[pair]
    e_hi = EXP_PER_GROUP * group + jnp.array([1, 2, 3, 2, 3, 3], jnp.int32)[pair]
    tile_e = jnp.concatenate([e_lo, e_hi]).astype(jnp.int32)
    n_used = (ends[N_BUCKETS - 1] / TM).astype(jnp.int32).reshape(1)
    tail = jnp.concatenate([(ends - TM)[:N_BUCKETS], padded[:N_BUCKETS], n_used.astype(F32)]).astype(jnp.int32)
    return pos.astype(jnp.int32), tile_e, n_used, tail, n_tiles * TM


def _pack_inproj_weights(w):
    d = w.shape[0]
    qw = NSA_HEADS * NSA_HD
    kvw = NSA_KV * NSA_HD
    o = 0
    wq = w[:, o:o + qw]; o += qw
    wkc = w[:, o:o + kvw]; o += kvw
    wvc = w[:, o:o + kvw]; o += kvw
    wks = w[:, o:o + kvw]; o += kvw
    wvs = w[:, o:o + kvw]; o += kvw
    wkw = w[:, o:o + kvw]; o += kvw
    wvw = w[:, o:o + kvw]; o += kvw
    wga = w[:, o:o + 3 * NSA_HEADS]; o += 3 * NSA_HEADS
    wqkb = w[:, o:o + 2 * ML_WIDTH]; o += 2 * ML_WIDTH
    wvb = w[:, o:o + ML_WIDTH]; o += ML_WIDTH
    wob = w[:, o:o + ML_WIDTH]; o += ML_WIDTH
    wif = w[:, o:o + 2 * ML_HEADS]; o += 2 * ML_HEADS
    wmg = w[:, o:o + 2 * D_MODEL]
    wsm = jnp.concatenate([wga, wif, jnp.zeros((d, LANES - 3 * NSA_HEADS - 2 * ML_HEADS), w.dtype)], axis=1)
    wcat = jnp.concatenate([wq, wkc, wvc, wks, wkw, wsm, wqkb, wvb, wob, wmg], axis=1).astype(BF16)
    wtr = jnp.concatenate([wvs, wvw, wsm], axis=1).T.astype(BF16)
    return wcat, wtr


def _rope_tables(positions):
    half = ROPE_DIM // 2
    inv = ROPE_THETA ** (-jnp.arange(0, ROPE_DIM, 2, dtype=F32) / ROPE_DIM)
    ang = positions.astype(F32).reshape(-1, 1) * inv[None, :]
    cs = jnp.concatenate([jnp.cos(ang), jnp.sin(ang)], axis=1)
    d = np.arange(LANES) % NSA_HD
    spread = np.zeros((2 * half, 3 * LANES), np.float32)
    lanes = np.arange(LANES)
    rot = d < ROPE_DIM
    spread[(d % half)[rot], lanes[rot]] = 1.0
    hi = (d >= half) & rot
    spread[half + (d % half)[hi], LANES + lanes[hi]] = 1.0
    lo = d < half
    spread[half + (d % half)[lo], 2 * LANES + lanes[lo]] = -1.0
    unrotated = (~rot).astype(np.float32).reshape(1, LANES)
    return cs, jnp.asarray(spread, BF16), jnp.asarray(unrotated)


def _pack_compress_weights(w1, w2, pe):
    half = CMP_LEN // 2
    w1r = w1.reshape(2, half, NSA_HD, CMP_HIDDEN)
    outs = []
    for part in range(2):
        wb = w1r[part].astype(BF16)
        zb = jnp.zeros_like(wb)
        wp = jnp.stack([jnp.stack([wb, zb], axis=2), jnp.stack([zb, wb], axis=2)], axis=1)
        outs.append(wp.reshape(half * NSA_KV * NSA_HD, NSA_KV * CMP_HIDDEN))
    pe8 = jnp.broadcast_to(pe.reshape(1, CMP_LEN * NSA_HD), (SUBLANES, CMP_LEN * NSA_HD)).astype(BF16)
    return outs[0], outs[1], pe8, w1.astype(BF16)


def _stages(x, p, positions, g_mix, w_in, b_if, w_ck1, w_ck2, pe_ck, w_cv1, w_cv2, pe_cv, w_conv, b_conv, g_hn, w_pa, w_pb, w_out, g_ffn, w_rg, b_rg, w_re, b_re, w_e13, w_e2, g_ple, w_pg, w_pp, g_final):
    b, s, d = x.shape
    t = b * s
    cs, spread, unrot = _rope_tables(positions)
    assert w_in.shape[0] == 1, "the final norm is fused into the layer's last kernel: single-layer problem only"
    for i in range(w_in.shape[0]):
        x2 = x.reshape(t, d)
        wcat, wtr = _pack_inproj_weights(w_in[i])
        (qpad, kc_tok, vc_tok, ks, kw, vst, vwt, sm, smt, qkb, vb, ob, mg) = _inproj(
            x2, g_mix[i].reshape(1, d), wcat, wtr, cs, spread, unrot)
        wka, wkb, pek, w1k = _pack_compress_weights(w_ck1[i], w_ck2[i], pe_ck[i])
        wva, wvb, pev, w1v = _pack_compress_weights(w_cv1[i], w_cv2[i], pe_cv[i])
        zpad = jnp.zeros((CMP_HIDDEN, NSA_HD), F32)
        w2k = jnp.stack([jnp.concatenate([w_ck2[i], zpad], axis=1),
                         jnp.concatenate([zpad, w_ck2[i]], axis=1)]).astype(BF16)
        w2vt = w_cv2[i].T.astype(BF16)
        nrow = s // CMP_STRIDE
        rk = kc_tok.reshape(b, nrow, CMP_STRIDE * LANES)
        rv = vc_tok.reshape(b, nrow, CMP_STRIDE * LANES)
        kcb, vct = _compress(rk, rv, wka, wkb, wva, wvb, pek, pev, w1k, w1v, w2k, w2vt)
        ya = _nsa(qpad, kcb, vct, ks, kw, vst, vwt, smt, b, s)
        bif = b_if[i].astype(F32)
        bifc = jnp.zeros((1, LANES), F32).at[0, SM_I:SM_I + 2 * ML_HEADS].set(bif)
        bifr = bifc.reshape(LANES, 1)
        yb = _mlstm(qkb, vb, ob, sm, smt, w_conv[i], b_conv[i].reshape(1, -1), bifc, bifr,
                    g_hn[i].reshape(1, -1), b, s)
        wr = jnp.concatenate([w_re[i], w_rg[i], jnp.zeros((d, LANES - N_EXPERTS - N_GROUPS), F32)], axis=1)
        wr_hi = wr.astype(BF16)
        wr = jnp.concatenate([wr_hi, (wr - wr_hi.astype(F32)).astype(BF16)], axis=1)
        br =jnp.concatenate([b_re[i], b_rg[i], jnp.zeros((LANES - N_EXPERTS - N_GROUPS,), F32)]).reshape(1, LANES)
        x1, h2, slab, hist8 = _merge(ya, yb, mg, x2, w_pa[i].astype(BF16), w_pb[i].astype(BF16),
                                     w_out[i].astype(BF16), g_ffn[i].reshape(1, d), wr, br)
        pos, tile_e, n_used, tail, n_slots = _routing_tables(slab, hist8)
        hx_sorted = _dispatch(pos, tail, h2, slab, n_slots)
        y_sorted = _moe(tile_e, n_used, hx_sorted, w_e13[i].astype(BF16), w_e2[i].astype(BF16))
        out = _combine(pos, y_sorted, x1, p[i].reshape(t, PLE_DIM), g_ple[i].reshape(1, d), w_pg[i].astype(BF16),
                       w_pp[i].astype(BF16), g_final.reshape(1, d))
        x = out.reshape(b, s, d)
    return dict(out=x, qpad=qpad, ks=ks, kcb=kcb, vct=vct, y_a=ya, y_b=yb, x1=x1, pos=pos)


def kernel(x, p, positions, g_mix, w_in, b_if, w_ck1, w_ck2, pe_ck, w_cv1, w_cv2, pe_cv, w_conv, b_conv, g_hn, w_pa, w_pb, w_out, g_ffn, w_rg, b_rg, w_re, b_re, w_e13, w_e2, g_ple, w_pg, w_pp, g_final):
    return _stages(x, p, positions, g_mix, w_in, b_if, w_ck1, w_ck2, pe_ck, w_cv1, w_cv2, pe_cv, w_conv, b_conv, g_hn,
                   w_pa, w_pb, w_out, g_ffn, w_rg, b_rg, w_re, b_re, w_e13, w_e2, g_ple, w_pg, w_pp, g_final)["out"]
```

```python
import math

import numpy as np
import jax
import jax.numpy as jnp
from jax import lax
from jax.experimental import pallas as pl
from jax.experimental.pallas import tpu as pltpu

F32 = jnp.float32
BF16 = jnp.bfloat16

EPS = 1e-6
NEG = -1e30

D_MODEL = 1024
PLE_DIM = 256
NSA_HEADS = 8
NSA_KV = 2
NSA_HPG = NSA_HEADS // NSA_KV
NSA_HD = 64
CMP_LEN = 32
CMP_STRIDE = 16
CMP_HIDDEN = 256
SEL_BLOCK = 64
SEL_TOPK = 16
SEL_FORCE = 1000.0
WINDOW = 512
ROPE_THETA = 500000.0
ROPE_DIM = NSA_HD // 4
ML_HEADS = 4
ML_HD = 128
ML_WIDTH = ML_HEADS * ML_HD
CONV_W = 4
N_GROUPS = 4
EXP_PER_GROUP = 4
N_EXPERTS = N_GROUPS * EXP_PER_GROUP
D_EXPERT = 256

LANES = 128
SUBLANES = 8
QT = 128
KC = 128
SEL_GROUP = 512
VT_PAD = 16
VT_ROWS = NSA_HD + VT_PAD
ML_CHUNK = 128
ML_BLOCK = 256
TD = 512
TM = 256
VMEM_LIMIT = 56 * 1024 * 1024

_NT = (((1,), (1,)), ((), ()))
_TN = (((0,), (0,)), ((), ()))

SM_GATE = 0
SM_I = 3 * NSA_HEADS
SM_F = SM_I + ML_HEADS


def _dot(a, b):
    return jnp.dot(a, b, preferred_element_type=F32)


def _dot_nt(a, b):
    return lax.dot_general(a, b, _NT, preferred_element_type=F32)


def _split3(x):
    hi = x.astype(BF16)
    r1 = x - hi.astype(F32)
    mid = r1.astype(BF16)
    lo = (r1 - mid.astype(F32)).astype(BF16)
    return hi, mid, lo


def _rms(x, g):
    return x * lax.rsqrt(jnp.mean(x * x, axis=-1, keepdims=True) + EPS) * g


def _sigmoid(x):
    return 0.5 + 0.5 * jnp.tanh(0.5 * x)


def _const_spec(shape):
    nd = len(shape)
    return pl.BlockSpec(shape, lambda *_: (0,) * nd, pipeline_mode=pl.Buffered(1))


_C_Q = 0
_C_KC = _C_Q + NSA_HEADS * NSA_HD
_C_VC = _C_KC + LANES
_C_KS = _C_VC + LANES
_C_KW = _C_KS + LANES
_C_SM = _C_KW + LANES
_C_QKB = _C_SM + LANES
_C_VB = _C_QKB + 2 * ML_WIDTH
_C_OB = _C_VB + ML_WIDTH
_C_MG = _C_OB + ML_WIDTH
_C_END = _C_MG + 2 * D_MODEL


def _inproj_kernel(x_ref, g_ref, w_ref, wt_ref, cs_ref, spread_ref, unrot_ref,
                   q_ref, kc_ref, vc_ref, ks_ref, kw_ref, vst_ref, vwt_ref, sm_ref, smt_ref,
                   qkb_ref, vb_ref, ob_ref, mg_ref):
    hn = _rms(x_ref[...], g_ref[...]).astype(BF16)
    tables = sum(_dot(part, spread_ref[...]) for part in _split3(cs_ref[...]))
    rc = tables[:, 0:LANES] + unrot_ref[...]
    rp = tables[:, LANES:2 * LANES]
    rm = tables[:, 2 * LANES:3 * LANES]

    def rope(z):
        half = ROPE_DIM // 2
        return z * rc + pltpu.roll(z, half, 1) * rp + pltpu.roll(z, LANES - half, 1) * rm

    scale = NSA_HD ** -0.5 * math.log2(math.e)
    for h in range(NSA_HEADS * NSA_HD // LANES):
        z = _dot(hn, w_ref[:, _C_Q + h * LANES:_C_Q + (h + 1) * LANES])
        q_ref[:, h * LANES:(h + 1) * LANES] = (rope(z) * scale).astype(BF16)
    kc_ref[...] = rope(_dot(hn, w_ref[:, _C_KC:_C_KC + LANES])).astype(BF16)
    vc_ref[...] = _dot(hn, w_ref[:, _C_VC:_C_VC + LANES]).astype(BF16)
    ks_ref[...] = rope(_dot(hn, w_ref[:, _C_KS:_C_KS + LANES])).astype(BF16)
    kw_ref[...] = rope(_dot(hn, w_ref[:, _C_KW:_C_KW + LANES])).astype(BF16)
    sm_ref[...] = _dot(hn, w_ref[:, _C_SM:_C_SM + LANES])
    for c0 in range(0, 2 * ML_WIDTH, 512):
        qkb_ref[:, c0:c0 + 512] = _dot(hn, w_ref[:, _C_QKB + c0:_C_QKB + c0 + 512]).astype(BF16)
    vb_ref[...] = _dot(hn, w_ref[:, _C_VB:_C_VB + ML_WIDTH]).astype(BF16)
    ob_ref[...] = _dot(hn, w_ref[:, _C_OB:_C_OB + ML_WIDTH]).astype(BF16)
    for c0 in range(0, 2 * D_MODEL, 512):
        mg_ref[:, c0:c0 + 512] = _dot(hn, w_ref[:, _C_MG + c0:_C_MG + c0 + 512]).astype(BF16)
    zt = _dot_nt(wt_ref[...], hn)
    ones_rows = (lax.broadcasted_iota(jnp.int32, (VT_PAD, KC), 0) == 0).astype(BF16)
    for i in range(TD // KC):
        for ref, r0 in ((vst_ref, 0), (vwt_ref, LANES)):
            zc = zt[r0:r0 + LANES, i * KC:(i + 1) * KC].astype(BF16)
            ref[i] = jnp.concatenate([piece for g in range(NSA_KV)
                                      for piece in (zc[g * NSA_HD:(g + 1) * NSA_HD, :], ones_rows)], axis=0)
    smt_ref[...] = zt[2 * LANES:3 * LANES, :]


def _inproj(x2, g_mix, wcat, wtr, cs, spread, unrot):
    t = x2.shape[0]
    row = lambda w: pl.BlockSpec((TD, w), lambda i: (i, 0))
    out_shape = [
        jax.ShapeDtypeStruct((t, NSA_HEADS * NSA_HD), BF16),
        jax.ShapeDtypeStruct((t, LANES), BF16),
        jax.ShapeDtypeStruct((t, LANES), BF16),
        jax.ShapeDtypeStruct((t, LANES), BF16),
        jax.ShapeDtypeStruct((t, LANES), BF16),
        jax.ShapeDtypeStruct((t // KC, NSA_KV * VT_ROWS, KC), BF16),
        jax.ShapeDtypeStruct((t // KC, NSA_KV * VT_ROWS, KC), BF16),
        jax.ShapeDtypeStruct((t, LANES), F32),
        jax.ShapeDtypeStruct((LANES, t), F32),
        jax.ShapeDtypeStruct((t, 2 * ML_WIDTH), BF16),
        jax.ShapeDtypeStruct((t, ML_WIDTH), BF16),
        jax.ShapeDtypeStruct((t, ML_WIDTH), BF16),
        jax.ShapeDtypeStruct((t, 2 * D_MODEL), BF16),
    ]
    chunk3 = pl.BlockSpec((TD // KC, NSA_KV * VT_ROWS, KC), lambda i: (i, 0, 0))
    out_specs = [row(NSA_HEADS * NSA_HD), row(LANES), row(LANES), row(LANES), row(LANES), chunk3, chunk3,
                 row(LANES), pl.BlockSpec((LANES, TD), lambda i: (0, i)),
                 row(2 * ML_WIDTH), row(ML_WIDTH), row(ML_WIDTH), row(2 * D_MODEL)]
    return pl.pallas_call(
        _inproj_kernel,
        out_shape=out_shape,
        grid=(t // TD,),
        in_specs=[row(D_MODEL), _const_spec((1, D_MODEL)), _const_spec((D_MODEL, _C_END)),
                  _const_spec((3 * LANES, D_MODEL)), row(cs.shape[1]), _const_spec(spread.shape),
                  _const_spec(unrot.shape)],
        out_specs=out_specs,
        compiler_params=pltpu.CompilerParams(dimension_semantics=("arbitrary",), vmem_limit_bytes=VMEM_LIMIT),
        name="inproj",
    )(x2, g_mix, wcat, wtr, cs, spread, unrot)


def _gelu_tanh(x):
    return 0.5 * x * (1.0 + jnp.tanh(math.sqrt(2.0 / math.pi) * (x + 0.044715 * x * x * x)))


def _compress_kernel(rk_ref, rv_ref, wka_ref, wkb_ref, wva_ref, wvb_ref, pek_ref, pev_ref,
                     w1k_ref, w1v_ref, w2k_ref, w2vt_ref, kc_ref, vct_ref):
    nrow = rk_ref.shape[0]

    def hidden(r_ref, wa_ref, wb_ref, pe_ref, w1_ref):
        r = r_ref[...]
        ha = _dot(r, wa_ref[...])
        hb = _dot(r, wb_ref[...])
        hb = pltpu.roll(hb, nrow - 1, 0)
        c = _dot(pe_ref[...], w1_ref[...])[0:1, :]
        return [_gelu_tanh(ha[:, g * CMP_HIDDEN:(g + 1) * CMP_HIDDEN] + hb[:, g * CMP_HIDDEN:(g + 1) * CMP_HIDDEN] + c).astype(BF16)
                for g in range(NSA_KV)]

    ak = hidden(rk_ref, wka_ref, wkb_ref, pek_ref, w1k_ref)
    kc_ref[...] = (_dot(ak[0], w2k_ref[0]) + _dot(ak[1], w2k_ref[1])).astype(BF16)
    av = hidden(rv_ref, wva_ref, wvb_ref, pev_ref, w1v_ref)
    for g in range(NSA_KV):
        vct_ref[g * NSA_HD:(g + 1) * NSA_HD, :] = _dot_nt(w2vt_ref[...], av[g]).astype(BF16)


def _compress(rk, rv, wka, wkb, wva, wvb, pek, pev, w1k, w1v, w2k, w2vt):
    b, nrow, width = rk.shape
    blk = pl.BlockSpec((None, nrow, width), lambda i: (i, 0, 0))
    return pl.pallas_call(
        _compress_kernel,
        out_shape=[jax.ShapeDtypeStruct((b, nrow, LANES), BF16),
                   jax.ShapeDtypeStruct((b, LANES, nrow), BF16)],
        grid=(b,),
        in_specs=[blk, blk] + [_const_spec(a.shape) for a in (wka, wkb, wva, wvb, pek, pev, w1k, w1v, w2k, w2vt)],
        out_specs=[pl.BlockSpec((None, nrow, LANES), lambda i: (i, 0, 0)),
                   pl.BlockSpec((None, LANES, nrow), lambda i: (i, 0, 0))],
        compiler_params=pltpu.CompilerParams(dimension_semantics=("arbitrary",), vmem_limit_bytes=VMEM_LIMIT),
        name="compress",
    )(rk, rv, wka, wkb, wva, wvb, pek, pev, w1k, w1v, w2k, w2vt)


def _nsa_kernel(q_ref, kc_ref, vct_ref, ks_ref, kw_ref, vst_ref, vwt_ref, smt_ref, o_ref, bias_scr, sx_scr, sy_scr):
    c = pl.program_id(1)
    t0 = c * QT
    ncmp = kc_ref.shape[0]
    nsel = bias_scr.shape[0]
    nw = WINDOW // KC + 1
    gw = NSA_HPG * QT
    width = NSA_KV * gw

    def per_group(x):
        return [x[:, g * gw:(g + 1) * gw] for g in range(NSA_KV)]

    def pv(vt, p):
        rows = vt.shape[0] // NSA_KV
        pb = p.astype(BF16)
        return jnp.concatenate([_dot(vt[g * rows:(g + 1) * rows, :], pg) for g, pg in enumerate(per_group(pb))],
                               axis=1)

    def normalised(acc):
        return acc[0:NSA_HD, :] / acc[NSA_HD:NSA_HD + 1, :]

    low_half = lax.broadcasted_iota(jnp.int32, (1, LANES), 1) < NSA_HD
    q_heads = []
    for h in range(NSA_HEADS):
        pair = q_ref[:, (h // 2) * LANES:(h // 2 + 1) * LANES].astype(F32)
        want_low = h // NSA_HPG == 0
        if (h % 2 == 0) != want_low:
            pair = pltpu.roll(pair, NSA_HD, 1)
        q_heads.append(jnp.where(low_half if want_low else ~low_half, pair, 0.0).astype(BF16))
    qs = jnp.concatenate(q_heads, axis=0)
    u_row = lax.broadcasted_iota(jnp.int32, (1, width), 1) % QT
    t_row = t0 + u_row
    r_kc = lax.broadcasted_iota(jnp.int32, (KC, 1), 0)

    n_grp = ks_ref.shape[0] // SEL_GROUP
    n_full = lax.shift_right_logical(t0, int(math.log2(SEL_GROUP)))

    def qk_group(j):
        return _dot_nt(ks_ref[pl.ds(pl.multiple_of(j * SEL_GROUP, SEL_GROUP), SEL_GROUP), :], qs)

    sc = _dot_nt(kc_ref[...], qs)

    w_slabs, w_chunks = [], []
    for i in range(nw):
        jj = c - (nw - 1) + i
        jc = jnp.maximum(jj, 0)
        si = _dot_nt(kw_ref[pl.ds(pl.multiple_of(jc * KC, KC), KC), :], qs)
        if i == 0:
            keep = jnp.where(jj >= 0, r_kc, -1) > u_row
        elif i == nw - 1:
            keep = r_kc <= u_row
        else:
            keep = jj >= 0
        w_slabs.append(jnp.where(keep, si, NEG))
        w_chunks.append(jc)

    n_col = lax.broadcasted_iota(jnp.int32, (ncmp, 1), 0)
    last_tok = jnp.where(n_col < ncmp - 1, CMP_STRIDE * n_col + (CMP_LEN - 1), jnp.iinfo(jnp.int32).max)
    s = jnp.where(last_tok <= t_row, sc, NEG)
    m = jnp.max(s, axis=0, keepdims=True)
    e = jnp.exp2(s - m)
    anyv = (t_row >= CMP_LEN - 1).astype(F32)
    p = e * (anyv / jnp.sum(e, axis=0, keepdims=True))
    o_cmp = pv(vct_ref[...], p)

    psums = []
    for pg in per_group(p):
        acc_p = pg[:, 0:QT]
        for h in range(1, NSA_HPG):
            acc_p = acc_p + pg[:, h * QT:(h + 1) * QT]
        psums.append(acc_p)
    psum = jnp.concatenate(psums, axis=1)
    nq2 = NSA_KV * QT
    s_col = lax.broadcasted_iota(jnp.int32, (nsel, 1), 0)
    n_lane = lax.broadcasted_iota(jnp.int32, (1, ncmp), 1)
    ov = ((CMP_STRIDE * n_lane < SEL_BLOCK * (s_col + 1)) & (CMP_STRIDE * n_lane + (CMP_LEN - 1) >= SEL_BLOCK * s_col)
          ).astype(BF16)
    imp = sum(_dot(ov, part) for part in _split3(psum))

    s_diag = qk_group(n_full)
    s_own = _dot_nt(ks_ref[pl.ds(pl.multiple_of(t0, KC), KC), :], qs)
    sx_scr[...] = qk_group(0)

    mxw = w_slabs[0]
    for sl in w_slabs[1:]:
        mxw = jnp.maximum(mxw, sl)
    mw = jnp.max(mxw, axis=0, keepdims=True)
    acc_w = jnp.zeros((VT_ROWS, width), F32)
    for sl, jc in zip(w_slabs, w_chunks):
        acc_w = acc_w + pv(vwt_ref[jc], jnp.exp2(sl - mw))
    o_win = normalised(acc_w)

    t1 = t0 + lax.broadcasted_iota(jnp.int32, (1, nq2), 1) % QT
    cur = lax.shift_right_logical(t1, 6)
    forced = (s_col == 0) | (s_col == cur) | (s_col == cur - 1)
    valid = SEL_BLOCK * s_col <= t1
    val = jnp.where(valid, jnp.where(forced, imp + SEL_FORCE, imp), NEG)
    sub = 8
    r_sub = lax.broadcasted_iota(jnp.int32, (sub, 1), 0)
    blocks = [val[r * sub:(r + 1) * sub, :] for r in range(nsel // sub)]
    ranks = [jnp.zeros((sub, nq2), F32) for _ in blocks]
    for i in range(nsel):
        vi = val[i:i + 1, :]
        for r, blk in enumerate(blocks):
            if i < r * sub:
                beats = vi >= blk
            elif i >= (r + 1) * sub:
                beats = vi > blk
            else:
                beats = (vi > blk) | ((vi == blk) & (r_sub > i - r * sub))
            ranks[r] = ranks[r] + jnp.where(beats, 1.0, 0.0)
    bias = jnp.where(jnp.concatenate(ranks, axis=0) < SEL_TOPK, 0.0, NEG).astype(F32)
    bias_scr[...] = jnp.concatenate([bias[:, g * QT:(g + 1) * QT] for g in range(NSA_KV) for _ in range(NSA_HPG)],
                                    axis=1)

    blk_per_grp = SEL_GROUP // SEL_BLOCK
    chunk_per_grp = SEL_GROUP // KC
    blk_per_chunk = KC // SEL_BLOCK

    def sel_update(j, sj, carry):
        m_o, acc = carry
        brows = [bias_scr[pl.ds(blk_per_grp * j + i, 1), :] for i in range(blk_per_grp)]

        def block(i):
            return sj[i * SEL_BLOCK:(i + 1) * SEL_BLOCK, :]

        mx = None
        for i in range(blk_per_grp):
            sl = block(i) + brows[i]
            mx = sl if mx is None else jnp.maximum(mx, sl)
        m_n = jnp.maximum(m_o, jnp.max(mx, axis=0, keepdims=True))
        a = jnp.exp2(m_o - m_n)
        acc = a * acc
        for ci in range(chunk_per_grp):
            parts = [jnp.exp2(block(i) + (brows[i] - m_n))
                     for i in range(blk_per_chunk * ci, blk_per_chunk * (ci + 1))]
            acc = acc + pv(vst_ref[chunk_per_grp * j + ci], jnp.concatenate(parts, axis=0))
        return m_n, acc

    def seed_state():
        first_own = blk_per_chunk * (c % chunk_per_grp)
        brows = [bias_scr[pl.ds(blk_per_grp * n_full + i, 1), :] + jnp.where(i < first_own, 0.0, NEG)
                 for i in range(blk_per_grp)]
        own = jnp.where(r_kc <= u_row, s_own, NEG)
        mx = jnp.maximum(own[0:SEL_BLOCK, :], own[SEL_BLOCK:KC, :])
        for i in range(blk_per_grp):
            mx = jnp.maximum(mx, s_diag[i * SEL_BLOCK:(i + 1) * SEL_BLOCK, :] + brows[i])
        m_n = jnp.max(mx, axis=0, keepdims=True)
        acc = pv(vst_ref[c], jnp.exp2(own - m_n))
        for ci in range(chunk_per_grp):
            parts = [jnp.exp2(s_diag[i * SEL_BLOCK:(i + 1) * SEL_BLOCK, :] + (brows[i] - m_n))
                     for i in range(blk_per_chunk * ci, blk_per_chunk * (ci + 1))]
            acc = acc + pv(vst_ref[chunk_per_grp * n_full + ci], jnp.concatenate(parts, axis=0))
        return m_n, acc

    seeded = seed_state()

    def pair_body(jp, carry):
        ja, jb = 2 * jp, 2 * jp + 1
        sy_scr[...] = qk_group(jb)
        carry = sel_update(ja, sx_scr, carry)
        sx_scr[...] = qk_group(jnp.minimum(ja + 2, n_grp - 1))
        return sel_update(jb, sy_scr, carry)

    n_pairs = lax.shift_right_logical(n_full, 1)
    carry = lax.fori_loop(0, n_pairs, pair_body, seeded)
    _, acc_s = lax.cond(n_full - 2 * n_pairs == 1, lambda cr: sel_update(n_full - 1, sx_scr, cr), lambda cr: cr,
                        carry)
    o_sel = normalised(acc_s)

    def gate_row(br):
        rows = [smt_ref[SM_GATE + 3 * h + br:SM_GATE + 3 * h + br + 1, :] for h in range(NSA_HEADS)]
        return _sigmoid(jnp.concatenate(rows, axis=1))

    o_t = gate_row(0) * o_cmp + gate_row(1) * o_sel + gate_row(2) * o_win
    for pr in range(NSA_HEADS // 2):
        xp = jnp.concatenate([o_t[:, (2 * pr) * QT:(2 * pr + 1) * QT], o_t[:, (2 * pr + 1) * QT:(2 * pr + 2) * QT]], axis=0)
        o_ref[:, pr * LANES:(pr + 1) * LANES] = xp.T.astype(BF16)


def _nsa(qpad, kcb, vct, ks, kw, vst, vwt, smt, b, s):
    nq = s // QT
    ncmp = kcb.shape[1]
    return pl.pallas_call(
        _nsa_kernel,
        out_shape=jax.ShapeDtypeStruct((b * s, NSA_HEADS * NSA_HD), BF16),
        grid=(b, nq),
        in_specs=[
            pl.BlockSpec((QT, NSA_HEADS * NSA_HD), lambda bi, c: (bi * nq + c, 0)),
            pl.BlockSpec((None, ncmp, LANES), lambda bi, c: (bi, 0, 0)),
            pl.BlockSpec((None, NSA_KV * NSA_HD, ncmp), lambda bi, c: (bi, 0, 0)),
            pl.BlockSpec((s, LANES), lambda bi, c: (bi, 0)),
            pl.BlockSpec((s, LANES), lambda bi, c: (bi, 0)),
            pl.BlockSpec((s // KC, NSA_KV * VT_ROWS, KC), lambda bi, c: (bi, 0, 0)),
            pl.BlockSpec((s // KC, NSA_KV * VT_ROWS, KC), lambda bi, c: (bi, 0, 0)),
            pl.BlockSpec((LANES, QT), lambda bi, c: (0, bi * nq + c)),
        ],
        out_specs=pl.BlockSpec((QT, NSA_HEADS * NSA_HD), lambda bi, c: (bi * nq + c, 0)),
        scratch_shapes=[pltpu.VMEM((s // SEL_BLOCK, NSA_HEADS * QT), F32),
                        pltpu.VMEM((SEL_GROUP, NSA_HEADS * QT), F32),
                        pltpu.VMEM((SEL_GROUP, NSA_HEADS * QT), F32)],
        compiler_params=pltpu.CompilerParams(dimension_semantics=("arbitrary", "arbitrary"),
                                             vmem_limit_bytes=VMEM_LIMIT),
        name="nsa",
    )(qpad, kcb, vct, ks, kw, vst, vwt, smt)


def _log_sigmoid(x):
    return jnp.minimum(x, 0.0) - jnp.log(1.0 + jnp.exp(-jnp.abs(x)))


def _mlstm_kernel(qk_ref, v_ref, og_ref, sm_ref, smt_ref, wc_ref, bc_ref, bifc_ref, bifr_ref, ghn_ref,
                  y_ref, tail_scr, ct_scr, n_scr, m_scr):
    lc = ML_CHUNK

    @pl.when(pl.program_id(1) == 0)
    def _():
        tail_scr[...] = jnp.zeros_like(tail_scr)
        ct_scr[...] = jnp.zeros_like(ct_scr)
        n_scr[...] = jnp.zeros_like(n_scr)
        m_scr[...] = jnp.zeros_like(m_scr)

    u = qk_ref[...]
    tail = tail_scr[...]
    rr8 = lax.broadcasted_iota(jnp.int32, (SUBLANES, 1), 0)
    sr = lax.broadcasted_iota(jnp.int32, (ML_BLOCK, ML_BLOCK), 0)
    sc_ = lax.broadcasted_iota(jnp.int32, (ML_BLOCK, ML_BLOCK), 1)
    y = bc_ref[...] + wc_ref[CONV_W - 1:CONV_W, :] * u.astype(F32)
    for k in range(1, CONV_W):
        down = _dot((sr - sc_ == k).astype(BF16), u)
        head = jnp.where(rr8 < k, pltpu.roll(tail, k, 0), down[0:SUBLANES, :])
        y = y + wc_ref[CONV_W - 1 - k:CONV_W - k, :] * jnp.concatenate([head, down[SUBLANES:, :]], axis=0)
    tail_scr[...] = u[ML_BLOCK - SUBLANES:ML_BLOCK, :].astype(F32)
    qkc = y * _sigmoid(y)
    q_all = qkc[:, 0:ML_WIDTH].astype(BF16)
    k_all = (qkc[:, ML_WIDTH:2 * ML_WIDTH] * (ML_HD ** -0.5)).astype(BF16)

    ifc = sm_ref[...] + bifc_ref[...]
    ifr = smt_ref[...] + bifr_ref[...]
    lfc = _log_sigmoid(ifc)
    lfr = _log_sigmoid(ifr)
    rr = lax.broadcasted_iota(jnp.int32, (lc, lc), 0)
    cc = lax.broadcasted_iota(jnp.int32, (lc, lc), 1)
    causal = rr >= cc
    tri_l = causal.astype(F32)
    tri_u = (rr <= cc).astype(F32)

    for ci in range(ML_BLOCK // lc):
        lo, hi = ci * lc, (ci + 1) * lc
        bc_all = jnp.dot(tri_l, lfc[lo:hi, :], preferred_element_type=F32, precision=lax.Precision.HIGHEST)
        br_all = jnp.dot(lfr[:, lo:hi], tri_u, preferred_element_type=F32, precision=lax.Precision.HIGHEST)
        heads = range(ML_HEADS)
        hsl = [slice(h * ML_HD, (h + 1) * ML_HD) for h in heads]
        bcol = [bc_all[:, SM_F + h:SM_F + h + 1] for h in heads]
        brow = [br_all[SM_F + h:SM_F + h + 1, :] for h in heads]
        icol = [ifc[lo:hi, SM_I + h:SM_I + h + 1] for h in heads]
        irow = [ifr[SM_I + h:SM_I + h + 1, lo:hi] for h in heads]
        mprev = [m_scr[h][:, 0:1] for h in heads]
        qh = [q_all[lo:hi, hsl[h]] for h in heads]
        kh = [k_all[lo:hi, hsl[h]] for h in heads]
        vh = [v_ref[lo:hi, hsl[h]] for h in heads]
        ct = [ct_scr[h] for h in heads]
        nrow = [n_scr[h] for h in heads]
        qk = [_dot_nt(qh[h], kh[h]) for h in heads]
        qc = [_dot(qh[h], ct[h].astype(BF16)) for h in heads]
        dmat = [jnp.where(causal, bcol[h] - brow[h] + irow[h], NEG) for h in heads]
        inter = [bcol[h] + mprev[h] for h in heads]
        mt = [jnp.maximum(jnp.max(dmat[h], axis=-1, keepdims=True), inter[h]) for h in heads]
        a = [jnp.exp(dmat[h] - mt[h]) * qk[h] for h in heads]
        dec = [jnp.exp(inter[h] - mt[h]) for h in heads]
        num = [_dot(a[h].astype(BF16), vh[h]) + dec[h] * qc[h] for h in heads]
        den = [jnp.sum(a[h], axis=-1, keepdims=True)
               + dec[h] * jnp.sum(qh[h].astype(F32) * nrow[h], axis=-1, keepdims=True) for h in heads]
        blast = [bcol[h][lc - 1:lc, :] for h in heads]
        mnew = [jnp.maximum(blast[h] + mprev[h], jnp.max(blast[h] - brow[h] + irow[h], axis=-1, keepdims=True))
                for h in heads]
        wprev = [jnp.exp(blast[h] + mprev[h] - mnew[h]) for h in heads]
        kwt = [kh[h].astype(F32) * jnp.exp(blast[h] - bcol[h] + icol[h] - mnew[h]) for h in heads]
        for h in heads:
            ct_scr[h] = wprev[h] * ct[h] + lax.dot_general(kwt[h].astype(BF16), vh[h], _TN,
                                                           preferred_element_type=F32)
            n_scr[h] = wprev[h] * nrow[h] + jnp.sum(kwt[h], axis=0, keepdims=True)
            m_scr[h] = jnp.broadcast_to(mnew[h], (1, LANES))
        hm = [num[h] / jnp.maximum(jnp.abs(den[h]), jnp.exp(-mt[h])) * _sigmoid(og_ref[lo:hi, hsl[h]].astype(F32))
              for h in heads]
        for h in heads:
            y_ref[lo:hi, hsl[h]] = _rms(hm[h], ghn_ref[:, hsl[h]]).astype(BF16)


def _mlstm(qkb, vb, ob, sm, smt, wconv, bconv, bifc, bifr, ghn, b, s):
    nb = s // ML_BLOCK
    row = lambda w: pl.BlockSpec((ML_BLOCK, w), lambda bi, j: (bi * nb + j, 0))
    return pl.pallas_call(
        _mlstm_kernel,
        out_shape=jax.ShapeDtypeStruct((b * s, ML_WIDTH), BF16),
        grid=(b, nb),
        in_specs=[row(2 * ML_WIDTH), row(ML_WIDTH), row(ML_WIDTH), row(LANES),
                  pl.BlockSpec((LANES, ML_BLOCK), lambda bi, j: (0, bi * nb + j)),
                  _const_spec(wconv.shape), _const_spec(bconv.shape), _const_spec(bifc.shape),
                  _const_spec(bifr.shape), _const_spec(ghn.shape)],
        out_specs=row(ML_WIDTH),
        scratch_shapes=[pltpu.VMEM((SUBLANES, 2 * ML_WIDTH), F32),
                        pltpu.VMEM((ML_HEADS, ML_HD, ML_HD), F32),
                        pltpu.VMEM((ML_HEADS, 1, ML_HD), F32),
                        pltpu.VMEM((ML_HEADS, 1, LANES), F32)],
        compiler_params=pltpu.CompilerParams(dimension_semantics=("arbitrary", "arbitrary"),
                                             vmem_limit_bytes=VMEM_LIMIT),
        name="mlstm",
    )(qkb, vb, ob, sm, smt, wconv, bconv, bifc, bifr, ghn)


RT_BUCKET = N_EXPERTS
RT_RANK = N_EXPERTS + 1
RT_WLO = N_EXPERTS + 2
RT_WHI = N_EXPERTS + 3
RT_LPOS = N_EXPERTS + 4
N_BUCKETS = N_GROUPS * 6
X_ROWS = D_MODEL // LANES
REC = 2 * X_ROWS
DMA_UNROLL = 8
DISPATCH_SUB = 32
COMBINE_PARTS = 2


def _merge_kernel(ya_ref, yb_ref, mg_ref, x_ref, wpa_ref, wpb_ref, wout_ref, gffn_ref, wr_ref, br_ref,
                  x1_ref, h2_ref, slab_ref, hist_ref):
    halves = [slice(i * (TD // 2), (i + 1) * (TD // 2)) for i in range(2)]
    pa = [_dot(ya_ref[hs, :], wpa_ref[...]) for hs in halves]
    pb = [_dot(yb_ref[hs, :], wpb_ref[...]) for hs in halves]
    mixed = [(_sigmoid(mg_ref[hs, 0:D_MODEL].astype(F32)) * pa[i]
              + _sigmoid(mg_ref[hs, D_MODEL:2 * D_MODEL].astype(F32)) * pb[i]).astype(BF16)
             for i, hs in enumerate(halves)]
    x1 = [x_ref[hs, :] + _dot(mixed[i], wout_ref[...]) for i, hs in enumerate(halves)]
    h2 = [_rms(x1[i], gffn_ref[...]) for i in range(2)]
    for i, hs in enumerate(halves):
        x1_ref[hs, :] = x1[i]
        h2_ref[hs, :] = h2[i]

    h_hi = [h.astype(BF16) for h in h2]
    h_lo = [(h2[i] - h_hi[i].astype(F32)).astype(BF16) for i in range(2)]
    r_hi = [_dot(h, wr_ref[...]) for h in h_hi]
    logit = jnp.concatenate([r_hi[i][:, 0:LANES] + r_hi[i][:, LANES:2 * LANES] + _dot(h_lo[i], wr_ref[:, 0:LANES])
                             for i in range(2)], axis=0) + br_ref[...]
    lane = lax.broadcasted_iota(jnp.int32, logit.shape, 1)
    big = jnp.int32(LANES)
    gmask = (lane >= N_EXPERTS) & (lane < N_EXPERTS + N_GROUPS)
    gl = jnp.where(gmask, logit, NEG)
    gmax = jnp.max(gl, axis=-1, keepdims=True)
    gidx = jnp.min(jnp.where(gmask & (gl == gmax), lane, big), axis=-1, keepdims=True) - N_EXPERTS
    pg_sel = 1.0 / jnp.sum(jnp.where(gmask, jnp.exp(gl - gmax), 0.0), axis=-1, keepdims=True)
    emask = (lane < N_EXPERTS) & (lax.shift_right_logical(lane, 2) == gidx)
    el = jnp.where(emask, logit, NEG)
    e1 = jnp.max(el, axis=-1, keepdims=True)
    i1 = jnp.min(jnp.where(emask & (el == e1), lane, big), axis=-1, keepdims=True)
    emask2 = emask & (lane != i1)
    el2 = jnp.where(emask2, logit, NEG)
    e2 = jnp.max(el2, axis=-1, keepdims=True)
    i2 = jnp.min(jnp.where(emask2 & (el2 == e2), lane, big), axis=-1, keepdims=True)
    x21 = jnp.exp(e2 - e1)
    w1 = pg_sel / (1.0 + x21)
    w2 = pg_sel * x21 / (1.0 + x21)
    first_lo = i1 < i2
    e_lo = jnp.where(first_lo, i1, i2) - EXP_PER_GROUP * gidx
    e_hi = jnp.where(first_lo, i2, i1) - EXP_PER_GROUP * gidx
    pair = lax.shift_right_logical(e_lo * (2 * EXP_PER_GROUP - 1 - e_lo), 1) + (e_hi - e_lo - 1)
    bucket = 6 * gidx + pair
    member = lane == bucket
    onehot = jnp.where(member, 1.0, 0.0)
    rr = lax.broadcasted_iota(jnp.int32, (TD, TD), 0)
    cc = lax.broadcasted_iota(jnp.int32, (TD, TD), 1)
    earlier = _dot((rr > cc).astype(BF16), onehot.astype(BF16))
    rank = jnp.sum(jnp.where(member, earlier, 0.0), axis=-1, keepdims=True)
    slab = jnp.where(lane == i1, w1, 0.0) + jnp.where(lane == i2, w2, 0.0)
    slab = jnp.where(lane == RT_BUCKET, bucket.astype(F32), slab)
    slab = jnp.where(lane == RT_RANK, rank, slab)
    slab = jnp.where(lane == RT_WLO, jnp.where(first_lo, w1, w2), slab)
    slab = jnp.where(lane == RT_WHI, jnp.where(first_lo, w2, w1), slab)
    hist8 = jnp.broadcast_to(jnp.sum(onehot, axis=0, keepdims=True), (SUBLANES, LANES))
    kk = lax.broadcasted_iota(jnp.int32, (LANES, LANES), 0)
    ll = lax.broadcasted_iota(jnp.int32, (LANES, LANES), 1)
    before = sum(_dot(part, (kk < ll).astype(BF16)) for part in _split3(hist8))[0:1, :]
    lpos = jnp.sum(jnp.where(member, before, 0.0), axis=-1, keepdims=True) + rank
    slab = jnp.where(lane == RT_LPOS, lpos, slab)
    slab_ref[...] = slab
    hist_ref[...] = hist8


def _merge(ya, yb, mg, x2, wpa, wpb, wout, gffn, wr, br):
    t = x2.shape[0]
    row = lambda w: pl.BlockSpec((TD, w), lambda i: (i, 0))
    return pl.pallas_call(
        _merge_kernel,
        out_shape=[jax.ShapeDtypeStruct((t, D_MODEL), F32),
                   jax.ShapeDtypeStruct((t, D_MODEL), F32),
                   jax.ShapeDtypeStruct((t, LANES), F32),
                   jax.ShapeDtypeStruct((t // TD * SUBLANES, LANES), F32)],
        grid=(t // TD,),
        in_specs=[row(NSA_HEADS * NSA_HD), row(ML_WIDTH), row(2 * D_MODEL), row(D_MODEL)]
                 + [_const_spec(a.shape) for a in (wpa, wpb, wout, gffn, wr, br)],
        out_specs=[row(D_MODEL), row(D_MODEL), row(LANES),
                   pl.BlockSpec((SUBLANES, LANES), lambda i: (i, 0))],
        compiler_params=pltpu.CompilerParams(dimension_semantics=("arbitrary",), vmem_limit_bytes=VMEM_LIMIT),
        name="merge",
    )(ya, yb, mg, x2, wpa, wpb, wout, gffn, wr, br)


def _rec_copy(src_ref, src_tok, dst_ref, dst_tok, sem, rows):
    src = src_ref.at[pl.ds(pl.multiple_of(src_tok * rows, rows), rows), :]
    dst = dst_ref.at[pl.ds(pl.multiple_of(dst_tok * rows, rows), rows), :]
    return pltpu.make_async_copy(src, dst, sem)


def _token_copies(n, make, wait=False):
    def body(g, carry):
        for u in range(DMA_UNROLL):
            cp = make(g * DMA_UNROLL + u)
            if wait:
                cp.wait()
            else:
                cp.start(priority=u % 2)
        return carry
    lax.fori_loop(0, n // DMA_UNROLL, body, 0)


def _dispatch_kernel(run_ref, tail_ref, h2_ref, slab_ref, out_ref, stage_scr, zero_scr, sem, zsem):
    i = pl.program_id(0)
    n_steps = pl.num_programs(0)
    slot = lax.rem(i, 2)
    n_tiles = out_ref.shape[0] // (TM * REC)
    n_used = tail_ref[2 * N_BUCKETS]

    def zero_copy(first_slot):
        start = pl.multiple_of(first_slot * REC, TM * REC)
        return pltpu.make_async_copy(zero_scr, out_ref.at[pl.ds(start, TM * REC), :], zsem)

    @pl.when(i == 0)
    def _():
        zero_scr[...] = jnp.zeros_like(zero_scr)
        for phase in ("start", "wait"):
            for b in range(N_BUCKETS):
                @pl.when(tail_ref[N_BUCKETS + b] > 0)
                def _():
                    getattr(zero_copy(tail_ref[b]), phase)()

                @pl.when(n_used + b < n_tiles)
                def _():
                    getattr(zero_copy((n_used + b) * TM), phase)()

    slab = slab_ref[...]
    perm = (slab[:, RT_LPOS:RT_LPOS + 1] == lax.broadcasted_iota(jnp.int32, (1, TD), 1).astype(F32)).astype(BF16)
    h_sorted = lax.dot_general(perm, h2_ref[...].astype(BF16), _TN, preferred_element_type=F32)
    s_sorted = sum(lax.dot_general(perm, part, _TN, preferred_element_type=F32) for part in _split3(slab))
    stage = stage_scr.at[slot]
    for j in range(REC):
        if j < X_ROWS:
            rows = h_sorted[:, j * LANES:(j + 1) * LANES]
        elif j == X_ROWS:
            rows = s_sorted
        else:
            rows = jnp.zeros((TD, LANES), F32)
        stage[pl.ds(j, TD, stride=REC), :] = rows

    def bucket_copies(tile, which, wait):
        src_buf = stage_scr.at[which]
        for b in range(N_BUCKETS):
            n = run_ref[tile * N_BUCKETS + b]
            src0 = run_ref[(n_steps + tile) * N_BUCKETS + b]
            dst0 = run_ref[(2 * n_steps + tile) * N_BUCKETS + b]
            done = 0
            for k in reversed(range(TD.bit_length())):
                size = 1 << k
                if size > TD:
                    continue

                @pl.when((n & size) != 0)
                def _(size=size, done=done):
                    src = src_buf.at[pl.ds(pl.multiple_of((src0 + done) * REC, REC), size * REC), :]
                    dst = out_ref.at[pl.ds(pl.multiple_of((dst0 + done) * REC, REC), size * REC), :]
                    cp = pltpu.make_async_copy(src, dst, sem.at[which])
                    cp.wait() if wait else cp.start()
                done = done + (n & size)

    bucket_copies(i, slot, wait=False)

    @pl.when(i > 0)
    def _():
        bucket_copies(i - 1, 1 - slot, wait=True)

    @pl.when(i == n_steps - 1)
    def _():
        bucket_copies(i, slot, wait=True)


def _dispatch(runs, tail, h2, slab, n_slots):
    t = h2.shape[0]
    return pl.pallas_call(
        _dispatch_kernel,
        out_shape=jax.ShapeDtypeStruct((n_slots * REC, LANES), F32),
        grid_spec=pltpu.PrefetchScalarGridSpec(
            num_scalar_prefetch=2,
            grid=(t // TD,),
            in_specs=[pl.BlockSpec((TD, D_MODEL), lambda i, run_r, tail_r: (i, 0)),
                      pl.BlockSpec((TD, LANES), lambda i, run_r, tail_r: (i, 0))],
            out_specs=pl.BlockSpec(memory_space=pl.ANY),
            scratch_shapes=[pltpu.VMEM((2, TD * REC, LANES), F32), pltpu.VMEM((TM * REC, LANES), F32),
                            pltpu.SemaphoreType.DMA((2,)), pltpu.SemaphoreType.DMA(())],
        ),
        compiler_params=pltpu.CompilerParams(dimension_semantics=("arbitrary",), vmem_limit_bytes=VMEM_LIMIT,
                                             has_side_effects=True),
        name="dispatch",
    )(runs, tail, h2, slab)


MOE_TILES = 2


def _moe_kernel(te_ref, nu_ref, hx_ref, w13_ref, w2_ref, y_ref):
    step = pl.program_id(0)
    n_tiles = pl.num_programs(0) * MOE_TILES
    n_used = nu_ref[0]

    @pl.when(step * MOE_TILES < n_used)
    def _():
        subs = range(MOE_TILES)
        hs = [jnp.concatenate([hx_ref[pl.ds(sub * TM * REC + j, TM, stride=REC), :] for j in range(X_ROWS)],
                              axis=1).astype(BF16) for sub in subs]
        slabs = [hx_ref[pl.ds(sub * TM * REC + X_ROWS, TM, stride=REC), :] for sub in subs]
        ys = [None] * MOE_TILES
        for side, lane in ((0, RT_WLO), (1, RT_WHI)):
            es = [te_ref[side * n_tiles + step * MOE_TILES + sub] for sub in subs]
            up = [_dot(hs[sub], w13_ref[es[sub]]) for sub in subs]
            act = [(up[sub][:, 0:D_EXPERT] * _sigmoid(up[sub][:, 0:D_EXPERT]) * up[sub][:, D_EXPERT:2 * D_EXPERT]
                    * slabs[sub][:, lane:lane + 1]).astype(BF16) for sub in subs]
            for sub in subs:
                part = _dot(act[sub], w2_ref[es[sub]])
                ys[sub] = part if ys[sub] is None else ys[sub] + part
        for sub in subs:
            for j in range(X_ROWS):
                y_ref[pl.ds(sub * TM * X_ROWS + j, TM, stride=X_ROWS), :] = ys[sub][:, j * LANES:(j + 1) * LANES]

    @pl.when(step * MOE_TILES >= n_used)
    def _():
        y_ref[...] = jnp.zeros_like(y_ref)


def _moe(tile_e, n_used, hx_sorted, w13, w2):
    n_tiles = hx_sorted.shape[0] // (TM * REC)
    rows = MOE_TILES * TM
    last = lambda nu: (nu[0] - 1) // MOE_TILES
    return pl.pallas_call(
        _moe_kernel,
        out_shape=jax.ShapeDtypeStruct((n_tiles * TM * X_ROWS, LANES), F32),
        grid_spec=pltpu.PrefetchScalarGridSpec(
            num_scalar_prefetch=2,
            grid=(n_tiles // MOE_TILES,),
            in_specs=[pl.BlockSpec((rows * REC, LANES), lambda k, te, nu: (jnp.minimum(k, last(nu)), 0)),
                      pl.BlockSpec(w13.shape, lambda k, te, nu: (0, 0, 0), pipeline_mode=pl.Buffered(1)),
                      pl.BlockSpec(w2.shape, lambda k, te, nu: (0, 0, 0), pipeline_mode=pl.Buffered(1))],
            out_specs=pl.BlockSpec((rows * X_ROWS, LANES), lambda k, te, nu: (k, 0)),
        ),
        compiler_params=pltpu.CompilerParams(dimension_semantics=("arbitrary",), vmem_limit_bytes=VMEM_LIMIT),
        name="moe",
    )(tile_e, n_used, hx_sorted, w13, w2)


def _combine_kernel(pos_ref, y_ref, x1_ref, p_ref, gple_ref, wpg_ref, wpp_ref, gfin_ref, o_ref, ybuf, sem):
    i = pl.program_id(0)
    slot = lax.rem(i, 2)

    last = pl.num_programs(0) - 1

    def drain(which):
        _token_copies(TD, lambda r: _rec_copy(y_ref, 0, ybuf.at[which], 0, sem.at[which], X_ROWS), wait=True)

    @pl.when(i == 0)
    def _():
        _token_copies(TD, lambda r: _rec_copy(y_ref, pos_ref[r], ybuf.at[0], r, sem.at[0], X_ROWS))

    drain(slot)
    nxt = jnp.minimum(i + 1, last) * TD
    yb = ybuf.at[slot]
    part = TD // COMBINE_PARTS
    for q in range(COMBINE_PARTS):
        rows = slice(q * part, (q + 1) * part)
        y = jnp.concatenate([yb[pl.ds(q * part * X_ROWS + j, part, stride=X_ROWS), :] for j in range(X_ROWS)],
                            axis=1)
        x2 = x1_ref[rows, :] + y
        h3 = _rms(x2, gple_ref[...]).astype(BF16)
        x3 = x2 + _sigmoid(_dot(h3, wpg_ref[...])) * _dot(p_ref[rows, :].astype(BF16), wpp_ref[...])
        o_ref[rows, :] = _rms(x3, gfin_ref[...])
        for r in range(q * part, (q + 1) * part):
            _rec_copy(y_ref, pos_ref[nxt + r], ybuf.at[1 - slot], r, sem.at[1 - slot], X_ROWS).start(
                priority=r % 2)

    @pl.when(i == last)
    def _():
        drain(1 - slot)


def _combine(pos, y_sorted, x1, p2, gple, wpg, wpp, gfin):
    t = x1.shape[0]
    row = lambda w: pl.BlockSpec((TD, w), lambda i, pos_r: (i, 0))
    const = lambda a: pl.BlockSpec(a.shape, lambda i, pos_r: (0,) * a.ndim, pipeline_mode=pl.Buffered(1))
    return pl.pallas_call(
        _combine_kernel,
        out_shape=jax.ShapeDtypeStruct((t, D_MODEL), F32),
        grid_spec=pltpu.PrefetchScalarGridSpec(
            num_scalar_prefetch=1,
            grid=(t // TD,),
            in_specs=[pl.BlockSpec(memory_space=pl.ANY), row(D_MODEL), row(PLE_DIM),
                      const(gple), const(wpg), const(wpp), const(gfin)],
            out_specs=row(D_MODEL),
            scratch_shapes=[pltpu.VMEM((2, TD * X_ROWS, LANES), F32), pltpu.SemaphoreType.DMA((2,))],
        ),
        compiler_params=pltpu.CompilerParams(dimension_semantics=("arbitrary",), vmem_limit_bytes=VMEM_LIMIT),
        name="combine",
    )(pos, y_sorted, x1, p2, gple, wpg, wpp, gfin)


def _routing_tables(slab, hist8):
    t = slab.shape[0]
    nt = t // TD
    n_tiles = t // TM + N_BUCKETS
    hist = hist8.reshape(nt, SUBLANES, LANES)[:, 0, :]
    counts = jnp.sum(hist, axis=0)
    padded = jnp.ceil(counts / TM) * TM
    ends = jnp.cumsum(padded)
    first = (ends - padded)[None, :] + jnp.cumsum(hist, axis=0) - hist
    lane = jnp.arange(N_BUCKETS, dtype=F32)[None, :]
    mine = lane == slab[:, RT_BUCKET:RT_BUCKET + 1]
    pos = jnp.sum(jnp.where(mine, jnp.repeat(first[:, :N_BUCKETS], TD, axis=0), 0.0), axis=1) + slab[:, RT_RANK]
    starts = jnp.arange(n_tiles, dtype=F32) * TM
    tile_bucket = jnp.minimum(jnp.sum(ends[None, :N_BUCKETS] <= starts[:, None], axis=1), N_BUCKETS - 1)
    group, pair = tile_bucket // 6, tile_bucket % 6
    e_lo = EXP_PER_GROUP * group + jnp.array([0, 0, 0, 1, 1, 2], jnp.int32)[pair]
    e_hi = EXP_PER_GROUP * group + jnp.array([1, 2, 3, 2, 3, 3], jnp.int32)[pair]
    tile_e = jnp.concatenate([e_lo, e_hi]).astype(jnp.int32)
    n_used = (ends[N_BUCKETS - 1] / TM).astype(jnp.int32).reshape(1)
    tail = jnp.concatenate([(ends - TM)[:N_BUCKETS], padded[:N_BUCKETS], n_used.astype(F32)]).astype(jnp.int32)
    local = jnp.cumsum(hist, axis=1) - hist
    runs = jnp.concatenate([a[:, :N_BUCKETS].reshape(-1) for a in (hist, local, first)]).astype(jnp.int32)
    return pos.astype(jnp.int32), runs, tile_e, n_used, tail, n_tiles * TM


def _pack_inproj_weights(w):
    d = w.shape[0]
    qw = NSA_HEADS * NSA_HD
    kvw = NSA_KV * NSA_HD
    o = 0
    wq = w[:, o:o + qw]; o += qw
    wkc = w[:, o:o + kvw]; o += kvw
    wvc = w[:, o:o + kvw]; o += kvw
    wks = w[:, o:o + kvw]; o += kvw
    wvs = w[:, o:o + kvw]; o += kvw
    wkw = w[:, o:o + kvw]; o += kvw
    wvw = w[:, o:o + kvw]; o += kvw
    wga = w[:, o:o + 3 * NSA_HEADS]; o += 3 * NSA_HEADS
    wqkb = w[:, o:o + 2 * ML_WIDTH]; o += 2 * ML_WIDTH
    wvb = w[:, o:o + ML_WIDTH]; o += ML_WIDTH
    wob = w[:, o:o + ML_WIDTH]; o += ML_WIDTH
    wif = w[:, o:o + 2 * ML_HEADS]; o += 2 * ML_HEADS
    wmg = w[:, o:o + 2 * D_MODEL]
    wsm = jnp.concatenate([wga, wif, jnp.zeros((d, LANES - 3 * NSA_HEADS - 2 * ML_HEADS), w.dtype)], axis=1)
    wcat = jnp.concatenate([wq, wkc, wvc, wks, wkw, wsm, wqkb, wvb, wob, wmg], axis=1).astype(BF16)
    wtr = jnp.concatenate([wvs, wvw, wsm], axis=1).T.astype(BF16)
    return wcat, wtr


def _rope_tables(positions):
    half = ROPE_DIM // 2
    inv = ROPE_THETA ** (-jnp.arange(0, ROPE_DIM, 2, dtype=F32) / ROPE_DIM)
    ang = positions.astype(F32).reshape(-1, 1) * inv[None, :]
    cs = jnp.concatenate([jnp.cos(ang), jnp.sin(ang)], axis=1)
    d = np.arange(LANES) % NSA_HD
    spread = np.zeros((2 * half, 3 * LANES), np.float32)
    lanes = np.arange(LANES)
    rot = d < ROPE_DIM
    spread[(d % half)[rot], lanes[rot]] = 1.0
    hi = (d >= half) & rot
    spread[half + (d % half)[hi], LANES + lanes[hi]] = 1.0
    lo = d < half
    spread[half + (d % half)[lo], 2 * LANES + lanes[lo]] = -1.0
    unrotated = (~rot).astype(np.float32).reshape(1, LANES)
    return cs, jnp.asarray(spread, BF16), jnp.asarray(unrotated)


def _pack_compress_weights(w1, w2, pe):
    half = CMP_LEN // 2
    w1r = w1.reshape(2, half, NSA_HD, CMP_HIDDEN)
    outs = []
    for part in range(2):
        wb = w1r[part].astype(BF16)
        zb = jnp.zeros_like(wb)
        wp = jnp.stack([jnp.stack([wb, zb], axis=2), jnp.stack([zb, wb], axis=2)], axis=1)
        outs.append(wp.reshape(half * NSA_KV * NSA_HD, NSA_KV * CMP_HIDDEN))
    pe8 = jnp.broadcast_to(pe.reshape(1, CMP_LEN * NSA_HD), (SUBLANES, CMP_LEN * NSA_HD)).astype(BF16)
    return outs[0], outs[1], pe8, w1.astype(BF16)


def _stages(x, p, positions, g_mix, w_in, b_if, w_ck1, w_ck2, pe_ck, w_cv1, w_cv2, pe_cv, w_conv, b_conv, g_hn, w_pa, w_pb, w_out, g_ffn, w_rg, b_rg, w_re, b_re, w_e13, w_e2, g_ple, w_pg, w_pp, g_final):
    b, s, d = x.shape
    t = b * s
    cs, spread, unrot = _rope_tables(positions)
    assert w_in.shape[0] == 1, "the final norm is fused into the layer's last kernel: single-layer problem only"
    for i in range(w_in.shape[0]):
        x2 = x.reshape(t, d)
        wcat, wtr = _pack_inproj_weights(w_in[i])
        (qpad, kc_tok, vc_tok, ks, kw, vst, vwt, sm, smt, qkb, vb, ob, mg) = _inproj(
            x2, g_mix[i].reshape(1, d), wcat, wtr, cs, spread, unrot)
        wka, wkb, pek, w1k = _pack_compress_weights(w_ck1[i], w_ck2[i], pe_ck[i])
        wva, wvb, pev, w1v = _pack_compress_weights(w_cv1[i], w_cv2[i], pe_cv[i])
        zpad = jnp.zeros((CMP_HIDDEN, NSA_HD), F32)
        w2k = jnp.stack([jnp.concatenate([w_ck2[i], zpad], axis=1),
                         jnp.concatenate([zpad, w_ck2[i]], axis=1)]).astype(BF16)
        w2vt = w_cv2[i].T.astype(BF16)
        nrow = s // CMP_STRIDE
        rk = kc_tok.reshape(b, nrow, CMP_STRIDE * LANES)
        rv = vc_tok.reshape(b, nrow, CMP_STRIDE * LANES)
        kcb, vct = _compress(rk, rv, wka, wkb, wva, wvb, pek, pev, w1k, w1v, w2k, w2vt)
        ya = _nsa(qpad, kcb, vct, ks, kw, vst, vwt, smt, b, s)
        bif = b_if[i].astype(F32)
        bifc = jnp.zeros((1, LANES), F32).at[0, SM_I:SM_I + 2 * ML_HEADS].set(bif)
        bifr = bifc.reshape(LANES, 1)
        yb = _mlstm(qkb, vb, ob, sm, smt, w_conv[i], b_conv[i].reshape(1, -1), bifc, bifr,
                    g_hn[i].reshape(1, -1), b, s)
        wr = jnp.concatenate([w_re[i], w_rg[i], jnp.zeros((d, LANES - N_EXPERTS - N_GROUPS), F32)], axis=1)
        wr_hi = wr.astype(BF16)
        wr = jnp.concatenate([wr_hi, (wr - wr_hi.astype(F32)).astype(BF16)], axis=1)
        br =jnp.concatenate([b_re[i], b_rg[i], jnp.zeros((LANES - N_EXPERTS - N_GROUPS,), F32)]).reshape(1, LANES)
        x1, h2, slab, hist8 = _merge(ya, yb, mg, x2, w_pa[i].astype(BF16), w_pb[i].astype(BF16),
                                     w_out[i].astype(BF16), g_ffn[i].reshape(1, d), wr, br)
        pos, runs, tile_e, n_used, tail, n_slots = _routing_tables(slab, hist8)
        hx_sorted = _dispatch(runs, tail, h2, slab, n_slots)
        y_sorted = _moe(tile_e, n_used, hx_sorted, w_e13[i].astype(BF16), w_e2[i].astype(BF16))
        out = _combine(pos, y_sorted, x1, p[i].reshape(t, PLE_DIM), g_ple[i].reshape(1, d), w_pg[i].astype(BF16),
                       w_pp[i].astype(BF16), g_final.reshape(1, d))
        x = out.reshape(b, s, d)
    return dict(out=x, qpad=qpad, ks=ks, kcb=kcb, vct=vct, y_a=ya, y_b=yb, x1=x1, pos=pos)


def kernel(x, p, positions, g_mix, w_in, b_if, w_ck1, w_ck2, pe_ck, w_cv1, w_cv2, pe_cv, w_conv, b_conv, g_hn, w_pa, w_pb, w_out, g_ffn, w_rg, b_rg, w_re, b_re, w_e13, w_e2, g_ple, w_pg, w_pp, g_final):
    return _stages(x, p, positions, g_mix, w_in, b_if, w_ck1, w_ck2, pe_ck, w_cv1, w_cv2, pe_cv, w_conv, b_conv, g_hn,
                   w_pa, w_pb, w_out, g_ffn, w_rg, b_rg, w_re, b_re, w_e13, w_e2, g_ple, w_pg, w_pp, g_final)["out"]
```

```python
import math

import numpy as np
import jax
import jax.numpy as jnp
from jax import lax
from jax.experimental import pallas as pl
from jax.experimental.pallas import tpu as pltpu

F32 = jnp.float32
BF16 = jnp.bfloat16

EPS = 1e-6
NEG = -1e30

D_MODEL = 1024
PLE_DIM = 256
NSA_HEADS = 8
NSA_KV = 2
NSA_HPG = NSA_HEADS // NSA_KV
NSA_HD = 64
CMP_LEN = 32
CMP_STRIDE = 16
CMP_HIDDEN = 256
SEL_BLOCK = 64
SEL_TOPK = 16
SEL_FORCE = 1000.0
WINDOW = 512
ROPE_THETA = 500000.0
ROPE_DIM = NSA_HD // 4
ML_HEADS = 4
ML_HD = 128
ML_WIDTH = ML_HEADS * ML_HD
CONV_W = 4
N_GROUPS = 4
EXP_PER_GROUP = 4
N_EXPERTS = N_GROUPS * EXP_PER_GROUP
D_EXPERT = 256

LANES = 128
SUBLANES = 8
QT = 128
KC = 128
SEL_GROUP = 512
VT_PAD = 16
VT_ROWS = NSA_HD + VT_PAD
ML_CHUNK = 128
ML_BLOCK = 256
TD = 512
TM = 256
VMEM_LIMIT = 56 * 1024 * 1024

_NT = (((1,), (1,)), ((), ()))
_TN = (((0,), (0,)), ((), ()))

SM_GATE = 0
SM_I = 3 * NSA_HEADS
SM_F = SM_I + ML_HEADS


def _dot(a, b):
    return jnp.dot(a, b, preferred_element_type=F32)


def _dot_nt(a, b):
    return lax.dot_general(a, b, _NT, preferred_element_type=F32)


def _split3(x):
    hi = x.astype(BF16)
    r1 = x - hi.astype(F32)
    mid = r1.astype(BF16)
    lo = (r1 - mid.astype(F32)).astype(BF16)
    return hi, mid, lo


def _rms(x, g):
    return x * lax.rsqrt(jnp.mean(x * x, axis=-1, keepdims=True) + EPS) * g


def _sigmoid(x):
    return 0.5 + 0.5 * jnp.tanh(0.5 * x)


def _const_spec(shape):
    nd = len(shape)
    return pl.BlockSpec(shape, lambda *_: (0,) * nd, pipeline_mode=pl.Buffered(1))


_C_Q = 0
_C_KC = _C_Q + NSA_HEADS * NSA_HD
_C_VC = _C_KC + LANES
_C_KS = _C_VC + LANES
_C_KW = _C_KS + LANES
_C_SM = _C_KW + LANES
_C_QKB = _C_SM + LANES
_C_VB = _C_QKB + 2 * ML_WIDTH
_C_OB = _C_VB + ML_WIDTH
_C_MG = _C_OB + ML_WIDTH
_C_END = _C_MG + 2 * D_MODEL


def _inproj_kernel(x_ref, g_ref, w_ref, wt_ref, cs_ref, spread_ref, unrot_ref,
                   q_ref, kc_ref, vc_ref, ks_ref, kw_ref, vst_ref, vwt_ref, sm_ref, smt_ref,
                   qkb_ref, vb_ref, ob_ref, mg_ref):
    hn = _rms(x_ref[...], g_ref[...]).astype(BF16)
    tables = sum(_dot(part, spread_ref[...]) for part in _split3(cs_ref[...]))
    rc = tables[:, 0:LANES] + unrot_ref[...]
    rp = tables[:, LANES:2 * LANES]
    rm = tables[:, 2 * LANES:3 * LANES]

    def rope(z):
        half = ROPE_DIM // 2
        return z * rc + pltpu.roll(z, half, 1) * rp + pltpu.roll(z, LANES - half, 1) * rm

    scale = NSA_HD ** -0.5 * math.log2(math.e)
    for h in range(NSA_HEADS * NSA_HD // LANES):
        z = _dot(hn, w_ref[:, _C_Q + h * LANES:_C_Q + (h + 1) * LANES])
        q_ref[:, h * LANES:(h + 1) * LANES] = (rope(z) * scale).astype(BF16)
    kc_ref[...] = rope(_dot(hn, w_ref[:, _C_KC:_C_KC + LANES])).astype(BF16)
    vc_ref[...] = _dot(hn, w_ref[:, _C_VC:_C_VC + LANES]).astype(BF16)
    ks_ref[...] = rope(_dot(hn, w_ref[:, _C_KS:_C_KS + LANES])).astype(BF16)
    kw_ref[...] = rope(_dot(hn, w_ref[:, _C_KW:_C_KW + LANES])).astype(BF16)
    sm_ref[...] = _dot(hn, w_ref[:, _C_SM:_C_SM + LANES])
    for c0 in range(0, 2 * ML_WIDTH, 512):
        qkb_ref[:, c0:c0 + 512] = _dot(hn, w_ref[:, _C_QKB + c0:_C_QKB + c0 + 512]).astype(BF16)
    vb_ref[...] = _dot(hn, w_ref[:, _C_VB:_C_VB + ML_WIDTH]).astype(BF16)
    ob_ref[...] = _dot(hn, w_ref[:, _C_OB:_C_OB + ML_WIDTH]).astype(BF16)
    for c0 in range(0, 2 * D_MODEL, 512):
        mg_ref[:, c0:c0 + 512] = _dot(hn, w_ref[:, _C_MG + c0:_C_MG + c0 + 512]).astype(BF16)
    zt = _dot_nt(wt_ref[...], hn)
    ones_rows = (lax.broadcasted_iota(jnp.int32, (VT_PAD, KC), 0) == 0).astype(BF16)
    for i in range(TD // KC):
        for ref, r0 in ((vst_ref, 0), (vwt_ref, LANES)):
            zc = zt[r0:r0 + LANES, i * KC:(i + 1) * KC].astype(BF16)
            ref[i] = jnp.concatenate([piece for g in range(NSA_KV)
                                      for piece in (zc[g * NSA_HD:(g + 1) * NSA_HD, :], ones_rows)], axis=0)
    smt_ref[...] = zt[2 * LANES:3 * LANES, :]


def _inproj(x2, g_mix, wcat, wtr, cs, spread, unrot):
    t = x2.shape[0]
    row = lambda w: pl.BlockSpec((TD, w), lambda i: (i, 0))
    out_shape = [
        jax.ShapeDtypeStruct((t, NSA_HEADS * NSA_HD), BF16),
        jax.ShapeDtypeStruct((t, LANES), BF16),
        jax.ShapeDtypeStruct((t, LANES), BF16),
        jax.ShapeDtypeStruct((t, LANES), BF16),
        jax.ShapeDtypeStruct((t, LANES), BF16),
        jax.ShapeDtypeStruct((t // KC, NSA_KV * VT_ROWS, KC), BF16),
        jax.ShapeDtypeStruct((t // KC, NSA_KV * VT_ROWS, KC), BF16),
        jax.ShapeDtypeStruct((t, LANES), F32),
        jax.ShapeDtypeStruct((LANES, t), F32),
        jax.ShapeDtypeStruct((t, 2 * ML_WIDTH), BF16),
        jax.ShapeDtypeStruct((t, ML_WIDTH), BF16),
        jax.ShapeDtypeStruct((t, ML_WIDTH), BF16),
        jax.ShapeDtypeStruct((t, 2 * D_MODEL), BF16),
    ]
    chunk3 = pl.BlockSpec((TD // KC, NSA_KV * VT_ROWS, KC), lambda i: (i, 0, 0))
    out_specs = [row(NSA_HEADS * NSA_HD), row(LANES), row(LANES), row(LANES), row(LANES), chunk3, chunk3,
                 row(LANES), pl.BlockSpec((LANES, TD), lambda i: (0, i)),
                 row(2 * ML_WIDTH), row(ML_WIDTH), row(ML_WIDTH), row(2 * D_MODEL)]
    return pl.pallas_call(
        _inproj_kernel,
        out_shape=out_shape,
        grid=(t // TD,),
        in_specs=[row(D_MODEL), _const_spec((1, D_MODEL)), _const_spec((D_MODEL, _C_END)),
                  _const_spec((3 * LANES, D_MODEL)), row(cs.shape[1]), _const_spec(spread.shape),
                  _const_spec(unrot.shape)],
        out_specs=out_specs,
        compiler_params=pltpu.CompilerParams(dimension_semantics=("arbitrary",), vmem_limit_bytes=VMEM_LIMIT),
        name="inproj",
    )(x2, g_mix, wcat, wtr, cs, spread, unrot)


def _gelu_tanh(x):
    return 0.5 * x * (1.0 + jnp.tanh(math.sqrt(2.0 / math.pi) * (x + 0.044715 * x * x * x)))


def _compress_kernel(rk_ref, rv_ref, wka_ref, wkb_ref, wva_ref, wvb_ref, pek_ref, pev_ref,
                     w1k_ref, w1v_ref, w2k_ref, w2vt_ref, kc_ref, vct_ref):
    nrow = rk_ref.shape[0]

    def hidden(r_ref, wa_ref, wb_ref, pe_ref, w1_ref):
        r = r_ref[...]
        ha = _dot(r, wa_ref[...])
        hb = _dot(r, wb_ref[...])
        hb = pltpu.roll(hb, nrow - 1, 0)
        c = _dot(pe_ref[...], w1_ref[...])[0:1, :]
        return [_gelu_tanh(ha[:, g * CMP_HIDDEN:(g + 1) * CMP_HIDDEN] + hb[:, g * CMP_HIDDEN:(g + 1) * CMP_HIDDEN] + c).astype(BF16)
                for g in range(NSA_KV)]

    ak = hidden(rk_ref, wka_ref, wkb_ref, pek_ref, w1k_ref)
    kc_ref[...] = (_dot(ak[0], w2k_ref[0]) + _dot(ak[1], w2k_ref[1])).astype(BF16)
    av = hidden(rv_ref, wva_ref, wvb_ref, pev_ref, w1v_ref)
    for g in range(NSA_KV):
        vct_ref[g * NSA_HD:(g + 1) * NSA_HD, :] = _dot_nt(w2vt_ref[...], av[g]).astype(BF16)


def _compress(rk, rv, wka, wkb, wva, wvb, pek, pev, w1k, w1v, w2k, w2vt):
    b, nrow, width = rk.shape
    blk = pl.BlockSpec((None, nrow, width), lambda i: (i, 0, 0))
    return pl.pallas_call(
        _compress_kernel,
        out_shape=[jax.ShapeDtypeStruct((b, nrow, LANES), BF16),
                   jax.ShapeDtypeStruct((b, LANES, nrow), BF16)],
        grid=(b,),
        in_specs=[blk, blk] + [_const_spec(a.shape) for a in (wka, wkb, wva, wvb, pek, pev, w1k, w1v, w2k, w2vt)],
        out_specs=[pl.BlockSpec((None, nrow, LANES), lambda i: (i, 0, 0)),
                   pl.BlockSpec((None, LANES, nrow), lambda i: (i, 0, 0))],
        compiler_params=pltpu.CompilerParams(dimension_semantics=("arbitrary",), vmem_limit_bytes=VMEM_LIMIT),
        name="compress",
    )(rk, rv, wka, wkb, wva, wvb, pek, pev, w1k, w1v, w2k, w2vt)


def _nsa_kernel(q_ref, kc_ref, vct_ref, ks_ref, kw_ref, vst_ref, vwt_ref, smt_ref, o_ref, bias_scr, sx_scr, sy_scr):
    c = pl.program_id(1)
    t0 = c * QT
    ncmp = kc_ref.shape[0]
    nsel = bias_scr.shape[0]
    nw = WINDOW // KC + 1
    gw = NSA_HPG * QT
    width = NSA_KV * gw

    def per_group(x):
        return [x[:, g * gw:(g + 1) * gw] for g in range(NSA_KV)]

    def pv(vt, p):
        rows = vt.shape[0] // NSA_KV
        pb = p.astype(BF16)
        return jnp.concatenate([_dot(vt[g * rows:(g + 1) * rows, :], pg) for g, pg in enumerate(per_group(pb))],
                               axis=1)

    def normalised(acc):
        return acc[0:NSA_HD, :] / acc[NSA_HD:NSA_HD + 1, :]

    low_half = lax.broadcasted_iota(jnp.int32, (1, LANES), 1) < NSA_HD
    q_heads = []
    for h in range(NSA_HEADS):
        pair = q_ref[:, (h // 2) * LANES:(h // 2 + 1) * LANES].astype(F32)
        want_low = h // NSA_HPG == 0
        if (h % 2 == 0) != want_low:
            pair = pltpu.roll(pair, NSA_HD, 1)
        q_heads.append(jnp.where(low_half if want_low else ~low_half, pair, 0.0).astype(BF16))
    qs = jnp.concatenate(q_heads, axis=0)
    u_row = lax.broadcasted_iota(jnp.int32, (1, width), 1) % QT
    t_row = t0 + u_row
    r_kc = lax.broadcasted_iota(jnp.int32, (KC, 1), 0)

    n_grp = ks_ref.shape[0] // SEL_GROUP
    n_full = lax.shift_right_logical(t0, int(math.log2(SEL_GROUP)))

    def qk_group(j):
        return _dot_nt(ks_ref[pl.ds(pl.multiple_of(j * SEL_GROUP, SEL_GROUP), SEL_GROUP), :], qs)

    sc = _dot_nt(kc_ref[...], qs)

    w_slabs, w_chunks = [], []
    for i in range(nw):
        jj = c - (nw - 1) + i
        jc = jnp.maximum(jj, 0)
        si = _dot_nt(kw_ref[pl.ds(pl.multiple_of(jc * KC, KC), KC), :], qs)
        if i == 0:
            keep = jnp.where(jj >= 0, r_kc, -1) > u_row
        elif i == nw - 1:
            keep = r_kc <= u_row
        else:
            keep = jj >= 0
        w_slabs.append(jnp.where(keep, si, NEG))
        w_chunks.append(jc)

    n_col = lax.broadcasted_iota(jnp.int32, (ncmp, 1), 0)
    last_tok = jnp.where(n_col < ncmp - 1, CMP_STRIDE * n_col + (CMP_LEN - 1), jnp.iinfo(jnp.int32).max)
    s = jnp.where(last_tok <= t_row, sc, NEG)
    m = jnp.max(s, axis=0, keepdims=True)
    e = jnp.exp2(s - m)
    anyv = (t_row >= CMP_LEN - 1).astype(F32)
    p = e * (anyv / jnp.sum(e, axis=0, keepdims=True))
    o_cmp = pv(vct_ref[...], p)

    psums = []
    for pg in per_group(p):
        acc_p = pg[:, 0:QT]
        for h in range(1, NSA_HPG):
            acc_p = acc_p + pg[:, h * QT:(h + 1) * QT]
        psums.append(acc_p)
    psum = jnp.concatenate(psums, axis=1)
    nq2 = NSA_KV * QT
    s_col = lax.broadcasted_iota(jnp.int32, (nsel, 1), 0)
    n_lane = lax.broadcasted_iota(jnp.int32, (1, ncmp), 1)
    ov = ((CMP_STRIDE * n_lane < SEL_BLOCK * (s_col + 1)) & (CMP_STRIDE * n_lane + (CMP_LEN - 1) >= SEL_BLOCK * s_col)
          ).astype(BF16)
    imp = sum(_dot(ov, part) for part in _split3(psum))

    s_diag = qk_group(n_full)
    s_own = _dot_nt(ks_ref[pl.ds(pl.multiple_of(t0, KC), KC), :], qs)
    sx_scr[...] = qk_group(0)

    mxw = w_slabs[0]
    for sl in w_slabs[1:]:
        mxw = jnp.maximum(mxw, sl)
    mw = jnp.max(mxw, axis=0, keepdims=True)
    acc_w = jnp.zeros((VT_ROWS, width), F32)
    for sl, jc in zip(w_slabs, w_chunks):
        acc_w = acc_w + pv(vwt_ref[jc], jnp.exp2(sl - mw))
    o_win = normalised(acc_w)

    t1 = t0 + lax.broadcasted_iota(jnp.int32, (1, nq2), 1) % QT
    cur = lax.shift_right_logical(t1, 6)
    forced = (s_col == 0) | (s_col == cur) | (s_col == cur - 1)
    valid = SEL_BLOCK * s_col <= t1
    val = jnp.where(valid, jnp.where(forced, imp + SEL_FORCE, imp), NEG)
    sub = 8
    r_sub = lax.broadcasted_iota(jnp.int32, (sub, 1), 0)
    blocks = [val[r * sub:(r + 1) * sub, :] for r in range(nsel // sub)]
    ranks = [jnp.zeros((sub, nq2), F32) for _ in blocks]
    for i in range(nsel):
        vi = val[i:i + 1, :]
        for r, blk in enumerate(blocks):
            if i < r * sub:
                beats = vi >= blk
            elif i >= (r + 1) * sub:
                beats = vi > blk
            else:
                beats = (vi > blk) | ((vi == blk) & (r_sub > i - r * sub))
            ranks[r] = ranks[r] + jnp.where(beats, 1.0, 0.0)
    bias = jnp.where(jnp.concatenate(ranks, axis=0) < SEL_TOPK, 0.0, NEG).astype(F32)
    bias_scr[...] = jnp.concatenate([bias[:, g * QT:(g + 1) * QT] for g in range(NSA_KV) for _ in range(NSA_HPG)],
                                    axis=1)

    blk_per_grp = SEL_GROUP // SEL_BLOCK
    chunk_per_grp = SEL_GROUP // KC
    blk_per_chunk = KC // SEL_BLOCK

    def sel_update(j, sj, carry):
        m_o, acc = carry
        brows = [bias_scr[pl.ds(blk_per_grp * j + i, 1), :] for i in range(blk_per_grp)]

        def block(i):
            return sj[i * SEL_BLOCK:(i + 1) * SEL_BLOCK, :]

        mx = None
        for i in range(blk_per_grp):
            sl = block(i) + brows[i]
            mx = sl if mx is None else jnp.maximum(mx, sl)
        m_n = jnp.maximum(m_o, jnp.max(mx, axis=0, keepdims=True))
        a = jnp.exp2(m_o - m_n)
        acc = a * acc
        for ci in range(chunk_per_grp):
            parts = [jnp.exp2(block(i) + (brows[i] - m_n))
                     for i in range(blk_per_chunk * ci, blk_per_chunk * (ci + 1))]
            acc = acc + pv(vst_ref[chunk_per_grp * j + ci], jnp.concatenate(parts, axis=0))
        return m_n, acc

    def seed_state():
        first_own = blk_per_chunk * (c % chunk_per_grp)
        brows = [bias_scr[pl.ds(blk_per_grp * n_full + i, 1), :] + jnp.where(i < first_own, 0.0, NEG)
                 for i in range(blk_per_grp)]
        own = jnp.where(r_kc <= u_row, s_own, NEG)
        mx = jnp.maximum(own[0:SEL_BLOCK, :], own[SEL_BLOCK:KC, :])
        for i in range(blk_per_grp):
            mx = jnp.maximum(mx, s_diag[i * SEL_BLOCK:(i + 1) * SEL_BLOCK, :] + brows[i])
        m_n = jnp.max(mx, axis=0, keepdims=True)
        acc = pv(vst_ref[c], jnp.exp2(own - m_n))
        for ci in range(chunk_per_grp):
            parts = [jnp.exp2(s_diag[i * SEL_BLOCK:(i + 1) * SEL_BLOCK, :] + (brows[i] - m_n))
                     for i in range(blk_per_chunk * ci, blk_per_chunk * (ci + 1))]
            acc = acc + pv(vst_ref[chunk_per_grp * n_full + ci], jnp.concatenate(parts, axis=0))
        return m_n, acc

    seeded = seed_state()

    def pair_body(jp, carry):
        ja, jb = 2 * jp, 2 * jp + 1
        sy_scr[...] = qk_group(jb)
        carry = sel_update(ja, sx_scr, carry)
        sx_scr[...] = qk_group(jnp.minimum(ja + 2, n_grp - 1))
        return sel_update(jb, sy_scr, carry)

    n_pairs = lax.shift_right_logical(n_full, 1)
    carry = lax.fori_loop(0, n_pairs, pair_body, seeded)
    _, acc_s = lax.cond(n_full - 2 * n_pairs == 1, lambda cr: sel_update(n_full - 1, sx_scr, cr), lambda cr: cr,
                        carry)
    o_sel = normalised(acc_s)

    def gate_row(br):
        rows = [smt_ref[SM_GATE + 3 * h + br:SM_GATE + 3 * h + br + 1, :] for h in range(NSA_HEADS)]
        return _sigmoid(jnp.concatenate(rows, axis=1))

    o_t = gate_row(0) * o_cmp + gate_row(1) * o_sel + gate_row(2) * o_win
    for pr in range(NSA_HEADS // 2):
        xp = jnp.concatenate([o_t[:, (2 * pr) * QT:(2 * pr + 1) * QT], o_t[:, (2 * pr + 1) * QT:(2 * pr + 2) * QT]], axis=0)
        o_ref[:, pr * LANES:(pr + 1) * LANES] = xp.T.astype(BF16)


def _nsa(qpad, kcb, vct, ks, kw, vst, vwt, smt, b, s):
    nq = s // QT
    ncmp = kcb.shape[1]
    return pl.pallas_call(
        _nsa_kernel,
        out_shape=jax.ShapeDtypeStruct((b * s, NSA_HEADS * NSA_HD), BF16),
        grid=(b, nq),
        in_specs=[
            pl.BlockSpec((QT, NSA_HEADS * NSA_HD), lambda bi, c: (bi * nq + c, 0)),
            pl.BlockSpec((None, ncmp, LANES), lambda bi, c: (bi, 0, 0)),
            pl.BlockSpec((None, NSA_KV * NSA_HD, ncmp), lambda bi, c: (bi, 0, 0)),
            pl.BlockSpec((s, LANES), lambda bi, c: (bi, 0)),
            pl.BlockSpec((s, LANES), lambda bi, c: (bi, 0)),
            pl.BlockSpec((s // KC, NSA_KV * VT_ROWS, KC), lambda bi, c: (bi, 0, 0)),
            pl.BlockSpec((s // KC, NSA_KV * VT_ROWS, KC), lambda bi, c: (bi, 0, 0)),
            pl.BlockSpec((LANES, QT), lambda bi, c: (0, bi * nq + c)),
        ],
        out_specs=pl.BlockSpec((QT, NSA_HEADS * NSA_HD), lambda bi, c: (bi * nq + c, 0)),
        scratch_shapes=[pltpu.VMEM((s // SEL_BLOCK, NSA_HEADS * QT), F32),
                        pltpu.VMEM((SEL_GROUP, NSA_HEADS * QT), F32),
                        pltpu.VMEM((SEL_GROUP, NSA_HEADS * QT), F32)],
        compiler_params=pltpu.CompilerParams(dimension_semantics=("arbitrary", "arbitrary"),
                                             vmem_limit_bytes=VMEM_LIMIT),
        name="nsa",
    )(qpad, kcb, vct, ks, kw, vst, vwt, smt)


def _log_sigmoid(x):
    return jnp.minimum(x, 0.0) - jnp.log(1.0 + jnp.exp(-jnp.abs(x)))


def _mlstm_kernel(qk_ref, v_ref, og_ref, sm_ref, smt_ref, wc_ref, bc_ref, bifc_ref, bifr_ref, ghn_ref,
                  y_ref, tail_scr, ct_scr, n_scr, m_scr):
    lc = ML_CHUNK

    @pl.when(pl.program_id(1) == 0)
    def _():
        tail_scr[...] = jnp.zeros_like(tail_scr)
        ct_scr[...] = jnp.zeros_like(ct_scr)
        n_scr[...] = jnp.zeros_like(n_scr)
        m_scr[...] = jnp.zeros_like(m_scr)

    u = qk_ref[...]
    tail = tail_scr[...]
    rr8 = lax.broadcasted_iota(jnp.int32, (SUBLANES, 1), 0)
    sr = lax.broadcasted_iota(jnp.int32, (ML_BLOCK, ML_BLOCK), 0)
    sc_ = lax.broadcasted_iota(jnp.int32, (ML_BLOCK, ML_BLOCK), 1)
    y = bc_ref[...] + wc_ref[CONV_W - 1:CONV_W, :] * u.astype(F32)
    for k in range(1, CONV_W):
        down = _dot((sr - sc_ == k).astype(BF16), u)
        head = jnp.where(rr8 < k, pltpu.roll(tail, k, 0), down[0:SUBLANES, :])
        y = y + wc_ref[CONV_W - 1 - k:CONV_W - k, :] * jnp.concatenate([head, down[SUBLANES:, :]], axis=0)
    tail_scr[...] = u[ML_BLOCK - SUBLANES:ML_BLOCK, :].astype(F32)
    qkc = y * _sigmoid(y)
    q_all = qkc[:, 0:ML_WIDTH].astype(BF16)
    k_all = (qkc[:, ML_WIDTH:2 * ML_WIDTH] * (ML_HD ** -0.5)).astype(BF16)

    ifc = sm_ref[...] + bifc_ref[...]
    ifr = smt_ref[...] + bifr_ref[...]
    lfc = _log_sigmoid(ifc)
    lfr = _log_sigmoid(ifr)
    rr = lax.broadcasted_iota(jnp.int32, (lc, lc), 0)
    cc = lax.broadcasted_iota(jnp.int32, (lc, lc), 1)
    causal = rr >= cc
    tri_l = causal.astype(F32)
    tri_u = (rr <= cc).astype(F32)

    for ci in range(ML_BLOCK // lc):
        lo, hi = ci * lc, (ci + 1) * lc
        bc_all = jnp.dot(tri_l, lfc[lo:hi, :], preferred_element_type=F32, precision=lax.Precision.HIGHEST)
        br_all = jnp.dot(lfr[:, lo:hi], tri_u, preferred_element_type=F32, precision=lax.Precision.HIGHEST)
        heads = range(ML_HEADS)
        hsl = [slice(h * ML_HD, (h + 1) * ML_HD) for h in heads]
        bcol = [bc_all[:, SM_F + h:SM_F + h + 1] for h in heads]
        brow = [br_all[SM_F + h:SM_F + h + 1, :] for h in heads]
        icol = [ifc[lo:hi, SM_I + h:SM_I + h + 1] for h in heads]
        irow = [ifr[SM_I + h:SM_I + h + 1, lo:hi] for h in heads]
        mprev = [m_scr[h][:, 0:1] for h in heads]
        qh = [q_all[lo:hi, hsl[h]] for h in heads]
        kh = [k_all[lo:hi, hsl[h]] for h in heads]
        vh = [v_ref[lo:hi, hsl[h]] for h in heads]
        ct = [ct_scr[h] for h in heads]
        nrow = [n_scr[h] for h in heads]
        qk = [_dot_nt(qh[h], kh[h]) for h in heads]
        qc = [_dot(qh[h], ct[h].astype(BF16)) for h in heads]
        dmat = [jnp.where(causal, bcol[h] - brow[h] + irow[h], NEG) for h in heads]
        inter = [bcol[h] + mprev[h] for h in heads]
        mt = [jnp.maximum(jnp.max(dmat[h], axis=-1, keepdims=True), inter[h]) for h in heads]
        a = [jnp.exp(dmat[h] - mt[h]) * qk[h] for h in heads]
        dec = [jnp.exp(inter[h] - mt[h]) for h in heads]
        num = [_dot(a[h].astype(BF16), vh[h]) + dec[h] * qc[h] for h in heads]
        den = [jnp.sum(a[h], axis=-1, keepdims=True)
               + dec[h] * jnp.sum(qh[h].astype(F32) * nrow[h], axis=-1, keepdims=True) for h in heads]
        blast = [bcol[h][lc - 1:lc, :] for h in heads]
        mnew = [jnp.maximum(blast[h] + mprev[h], jnp.max(blast[h] - brow[h] + irow[h], axis=-1, keepdims=True))
                for h in heads]
        wprev = [jnp.exp(blast[h] + mprev[h] - mnew[h]) for h in heads]
        kwt = [kh[h].astype(F32) * jnp.exp(blast[h] - bcol[h] + icol[h] - mnew[h]) for h in heads]
        for h in heads:
            ct_scr[h] = wprev[h] * ct[h] + lax.dot_general(kwt[h].astype(BF16), vh[h], _TN,
                                                           preferred_element_type=F32)
            n_scr[h] = wprev[h] * nrow[h] + jnp.sum(kwt[h], axis=0, keepdims=True)
            m_scr[h] = jnp.broadcast_to(mnew[h], (1, LANES))
        hm = [num[h] / jnp.maximum(jnp.abs(den[h]), jnp.exp(-mt[h])) * _sigmoid(og_ref[lo:hi, hsl[h]].astype(F32))
              for h in heads]
        for h in heads:
            y_ref[lo:hi, hsl[h]] = _rms(hm[h], ghn_ref[:, hsl[h]]).astype(BF16)


def _mlstm(qkb, vb, ob, sm, smt, wconv, bconv, bifc, bifr, ghn, b, s):
    nb = s // ML_BLOCK
    row = lambda w: pl.BlockSpec((ML_BLOCK, w), lambda bi, j: (bi * nb + j, 0))
    return pl.pallas_call(
        _mlstm_kernel,
        out_shape=jax.ShapeDtypeStruct((b * s, ML_WIDTH), BF16),
        grid=(b, nb),
        in_specs=[row(2 * ML_WIDTH), row(ML_WIDTH), row(ML_WIDTH), row(LANES),
                  pl.BlockSpec((LANES, ML_BLOCK), lambda bi, j: (0, bi * nb + j)),
                  _const_spec(wconv.shape), _const_spec(bconv.shape), _const_spec(bifc.shape),
                  _const_spec(bifr.shape), _const_spec(ghn.shape)],
        out_specs=row(ML_WIDTH),
        scratch_shapes=[pltpu.VMEM((SUBLANES, 2 * ML_WIDTH), F32),
                        pltpu.VMEM((ML_HEADS, ML_HD, ML_HD), F32),
                        pltpu.VMEM((ML_HEADS, 1, ML_HD), F32),
                        pltpu.VMEM((ML_HEADS, 1, LANES), F32)],
        compiler_params=pltpu.CompilerParams(dimension_semantics=("arbitrary", "arbitrary"),
                                             vmem_limit_bytes=VMEM_LIMIT),
        name="mlstm",
    )(qkb, vb, ob, sm, smt, wconv, bconv, bifc, bifr, ghn)


RT_BUCKET = N_EXPERTS
RT_RANK = N_EXPERTS + 1
RT_WLO = N_EXPERTS + 2
RT_WHI = N_EXPERTS + 3
RT_LPOS = N_EXPERTS + 4
N_BUCKETS = N_GROUPS * 6
X_ROWS = D_MODEL // LANES
DMA_UNROLL = 8
RUN_CHUNK = SUBLANES
MAX_CHUNKS = TD // RUN_CHUNK + N_BUCKETS
SORT_ROWS = MAX_CHUNKS * RUN_CHUNK
COMBINE_PARTS = 2


def _merge_kernel(ya_ref, yb_ref, mg_ref, x_ref, wpa_ref, wpb_ref, wout_ref, gffn_ref, wr_ref, br_ref,
                  x1_ref, h2_ref, slab_ref, hist_ref):
    halves = [slice(i * (TD // 2), (i + 1) * (TD // 2)) for i in range(2)]
    pa = [_dot(ya_ref[hs, :], wpa_ref[...]) for hs in halves]
    pb = [_dot(yb_ref[hs, :], wpb_ref[...]) for hs in halves]
    mixed = [(_sigmoid(mg_ref[hs, 0:D_MODEL].astype(F32)) * pa[i]
              + _sigmoid(mg_ref[hs, D_MODEL:2 * D_MODEL].astype(F32)) * pb[i]).astype(BF16)
             for i, hs in enumerate(halves)]
    x1 = [x_ref[hs, :] + _dot(mixed[i], wout_ref[...]) for i, hs in enumerate(halves)]
    h2 = [_rms(x1[i], gffn_ref[...]) for i in range(2)]
    for i, hs in enumerate(halves):
        x1_ref[hs, :] = x1[i]
        h2_ref[hs, :] = h2[i]

    h_hi = [h.astype(BF16) for h in h2]
    h_lo = [(h2[i] - h_hi[i].astype(F32)).astype(BF16) for i in range(2)]
    r_hi = [_dot(h, wr_ref[...]) for h in h_hi]
    logit = jnp.concatenate([r_hi[i][:, 0:LANES] + r_hi[i][:, LANES:2 * LANES] + _dot(h_lo[i], wr_ref[:, 0:LANES])
                             for i in range(2)], axis=0) + br_ref[...]
    lane = lax.broadcasted_iota(jnp.int32, logit.shape, 1)
    big = jnp.int32(LANES)
    gmask = (lane >= N_EXPERTS) & (lane < N_EXPERTS + N_GROUPS)
    gl = jnp.where(gmask, logit, NEG)
    gmax = jnp.max(gl, axis=-1, keepdims=True)
    gidx = jnp.min(jnp.where(gmask & (gl == gmax), lane, big), axis=-1, keepdims=True) - N_EXPERTS
    pg_sel = 1.0 / jnp.sum(jnp.where(gmask, jnp.exp(gl - gmax), 0.0), axis=-1, keepdims=True)
    emask = (lane < N_EXPERTS) & (lax.shift_right_logical(lane, 2) == gidx)
    el = jnp.where(emask, logit, NEG)
    e1 = jnp.max(el, axis=-1, keepdims=True)
    i1 = jnp.min(jnp.where(emask & (el == e1), lane, big), axis=-1, keepdims=True)
    emask2 = emask & (lane != i1)
    el2 = jnp.where(emask2, logit, NEG)
    e2 = jnp.max(el2, axis=-1, keepdims=True)
    i2 = jnp.min(jnp.where(emask2 & (el2 == e2), lane, big), axis=-1, keepdims=True)
    x21 = jnp.exp(e2 - e1)
    w1 = pg_sel / (1.0 + x21)
    w2 = pg_sel * x21 / (1.0 + x21)
    first_lo = i1 < i2
    e_lo = jnp.where(first_lo, i1, i2) - EXP_PER_GROUP * gidx
    e_hi = jnp.where(first_lo, i2, i1) - EXP_PER_GROUP * gidx
    pair = lax.shift_right_logical(e_lo * (2 * EXP_PER_GROUP - 1 - e_lo), 1) + (e_hi - e_lo - 1)
    bucket = 6 * gidx + pair
    member = lane == bucket
    onehot = jnp.where(member, 1.0, 0.0)
    rr = lax.broadcasted_iota(jnp.int32, (TD, TD), 0)
    cc = lax.broadcasted_iota(jnp.int32, (TD, TD), 1)
    earlier = _dot((rr > cc).astype(BF16), onehot.astype(BF16))
    rank = jnp.sum(jnp.where(member, earlier, 0.0), axis=-1, keepdims=True)
    slab = jnp.where(lane == i1, w1, 0.0) + jnp.where(lane == i2, w2, 0.0)
    slab = jnp.where(lane == RT_BUCKET, bucket.astype(F32), slab)
    slab = jnp.where(lane == RT_RANK, rank, slab)
    slab = jnp.where(lane == RT_WLO, jnp.where(first_lo, w1, w2), slab)
    slab = jnp.where(lane == RT_WHI, jnp.where(first_lo, w2, w1), slab)
    hist8 = jnp.broadcast_to(jnp.sum(onehot, axis=0, keepdims=True), (SUBLANES, LANES))
    kk = lax.broadcasted_iota(jnp.int32, (LANES, LANES), 0)
    ll = lax.broadcasted_iota(jnp.int32, (LANES, LANES), 1)
    runlen = jnp.ceil(hist8 / RUN_CHUNK) * RUN_CHUNK
    before = sum(_dot(part, (kk < ll).astype(BF16)) for part in _split3(runlen))[0:1, :]
    lpos = jnp.sum(jnp.where(member, before, 0.0), axis=-1, keepdims=True) + rank
    slab = jnp.where(lane == RT_LPOS, lpos, slab)
    slab_ref[...] = slab
    hist_ref[...] = hist8


def _merge(ya, yb, mg, x2, wpa, wpb, wout, gffn, wr, br):
    t = x2.shape[0]
    row = lambda w: pl.BlockSpec((TD, w), lambda i: (i, 0))
    return pl.pallas_call(
        _merge_kernel,
        out_shape=[jax.ShapeDtypeStruct((t, D_MODEL), F32),
                   jax.ShapeDtypeStruct((t, D_MODEL), F32),
                   jax.ShapeDtypeStruct((t, LANES), F32),
                   jax.ShapeDtypeStruct((t // TD * SUBLANES, LANES), F32)],
        grid=(t // TD,),
        in_specs=[row(NSA_HEADS * NSA_HD), row(ML_WIDTH), row(2 * D_MODEL), row(D_MODEL)]
                 + [_const_spec(a.shape) for a in (wpa, wpb, wout, gffn, wr, br)],
        out_specs=[row(D_MODEL), row(D_MODEL), row(LANES),
                   pl.BlockSpec((SUBLANES, LANES), lambda i: (i, 0))],
        compiler_params=pltpu.CompilerParams(dimension_semantics=("arbitrary",), vmem_limit_bytes=VMEM_LIMIT),
        name="merge",
    )(ya, yb, mg, x2, wpa, wpb, wout, gffn, wr, br)


def _rec_copy(src_ref, src_tok, dst_ref, dst_tok, sem, rows):
    src = src_ref.at[pl.ds(pl.multiple_of(src_tok * rows, rows), rows), :]
    dst = dst_ref.at[pl.ds(pl.multiple_of(dst_tok * rows, rows), rows), :]
    return pltpu.make_async_copy(src, dst, sem)


def _token_copies(n, make, wait=False):
    def body(g, carry):
        for u in range(DMA_UNROLL):
            cp = make(g * DMA_UNROLL + u)
            if wait:
                cp.wait()
            else:
                cp.start(priority=u % 2)
        return carry
    lax.fori_loop(0, n // DMA_UNROLL, body, 0)


def _dispatch_kernel(ch_ref, tail_ref, h2_ref, slab_ref, hx_ref, sx_ref, h_stage, s_stage, h_zero, s_zero, sem, zsem):
    i = pl.program_id(0)
    n_steps = pl.num_programs(0)
    slot = lax.rem(i, 2)
    n_tiles = hx_ref.shape[0] // TM
    n_used = tail_ref[2 * N_BUCKETS]

    def zero_copies(first_slot):
        rows = pl.ds(pl.multiple_of(first_slot, TM), TM)
        return (pltpu.make_async_copy(h_zero, hx_ref.at[rows, :], zsem),
                pltpu.make_async_copy(s_zero, sx_ref.at[rows, :], zsem))

    @pl.when(i == 0)
    def _():
        h_zero[...] = jnp.zeros_like(h_zero)
        s_zero[...] = jnp.zeros_like(s_zero)
        for phase in ("start", "wait"):
            for b in range(N_BUCKETS):
                @pl.when(tail_ref[N_BUCKETS + b] > 0)
                def _():
                    for cp in zero_copies(tail_ref[b]):
                        getattr(cp, phase)()

                @pl.when(n_used + b < n_tiles)
                def _():
                    for cp in zero_copies((n_used + b) * TM):
                        getattr(cp, phase)()

    slab = slab_ref[...]
    perm = (slab[:, RT_LPOS:RT_LPOS + 1]
            == lax.broadcasted_iota(jnp.int32, (1, SORT_ROWS), 1).astype(F32)).astype(BF16)
    h_stage[slot] = lax.dot_general(perm, h2_ref[...].astype(BF16), _TN, preferred_element_type=F32)
    s_stage[slot] = sum(lax.dot_general(perm, part, _TN, preferred_element_type=F32) for part in _split3(slab))

    def chunk_copies(tile, which, wait):
        src_at = n_steps + tile * MAX_CHUNKS
        dst_at = n_steps + (n_steps + tile) * MAX_CHUNKS

        def body(q, carry):
            src = pl.ds(pl.multiple_of(ch_ref[src_at + q], RUN_CHUNK), RUN_CHUNK)
            dst = pl.ds(pl.multiple_of(ch_ref[dst_at + q], RUN_CHUNK), RUN_CHUNK)
            for cp in (pltpu.make_async_copy(h_stage.at[which, src, :], hx_ref.at[dst, :], sem.at[which]),
                       pltpu.make_async_copy(s_stage.at[which, src, :], sx_ref.at[dst, :], sem.at[which])):
                cp.wait() if wait else cp.start()
            return carry
        lax.fori_loop(0, ch_ref[tile], body, 0)

    chunk_copies(i, slot, wait=False)

    @pl.when(i > 0)
    def _():
        chunk_copies(i - 1, 1 - slot, wait=True)

    @pl.when(i == n_steps - 1)
    def _():
        chunk_copies(i, slot, wait=True)


def _dispatch(chunks, tail, h2, slab, n_slots):
    t = h2.shape[0]
    return pl.pallas_call(
        _dispatch_kernel,
        out_shape=(jax.ShapeDtypeStruct((n_slots, D_MODEL), F32), jax.ShapeDtypeStruct((n_slots, LANES), F32)),
        grid_spec=pltpu.PrefetchScalarGridSpec(
            num_scalar_prefetch=2,
            grid=(t // TD,),
            in_specs=[pl.BlockSpec((TD, D_MODEL), lambda i, ch_r, tail_r: (i, 0)),
                      pl.BlockSpec((TD, LANES), lambda i, ch_r, tail_r: (i, 0))],
            out_specs=(pl.BlockSpec(memory_space=pl.ANY), pl.BlockSpec(memory_space=pl.ANY)),
            scratch_shapes=[pltpu.VMEM((2, SORT_ROWS, D_MODEL), F32), pltpu.VMEM((2, SORT_ROWS, LANES), F32),
                            pltpu.VMEM((TM, D_MODEL), F32), pltpu.VMEM((TM, LANES), F32),
                            pltpu.SemaphoreType.DMA((2,)), pltpu.SemaphoreType.DMA(())],
        ),
        compiler_params=pltpu.CompilerParams(dimension_semantics=("arbitrary",), vmem_limit_bytes=VMEM_LIMIT,
                                             has_side_effects=True),
        name="dispatch",
    )(chunks, tail, h2, slab)


MOE_TILES = 2


def _moe_kernel(te_ref, nu_ref, hx_ref, sx_ref, w13_ref, w2_ref, y_ref):
    step = pl.program_id(0)
    n_tiles = pl.num_programs(0) * MOE_TILES
    n_used = nu_ref[0]

    @pl.when(step * MOE_TILES < n_used)
    def _():
        subs = range(MOE_TILES)
        hs = [hx_ref[sub * TM:(sub + 1) * TM, :].astype(BF16) for sub in subs]
        slabs = [sx_ref[sub * TM:(sub + 1) * TM, :] for sub in subs]
        ys = [None] * MOE_TILES
        for side, lane in ((0, RT_WLO), (1, RT_WHI)):
            es = [te_ref[side * n_tiles + step * MOE_TILES + sub] for sub in subs]
            up = [_dot(hs[sub], w13_ref[es[sub]]) for sub in subs]
            act = [(up[sub][:, 0:D_EXPERT] * _sigmoid(up[sub][:, 0:D_EXPERT]) * up[sub][:, D_EXPERT:2 * D_EXPERT]
                    * slabs[sub][:, lane:lane + 1]).astype(BF16) for sub in subs]
            for sub in subs:
                part = _dot(act[sub], w2_ref[es[sub]])
                ys[sub] = part if ys[sub] is None else ys[sub] + part
        for sub in subs:
            for j in range(X_ROWS):
                y_ref[pl.ds(sub * TM * X_ROWS + j, TM, stride=X_ROWS), :] = ys[sub][:, j * LANES:(j + 1) * LANES]

    @pl.when(step * MOE_TILES >= n_used)
    def _():
        y_ref[...] = jnp.zeros_like(y_ref)


def _moe(tile_e, n_used, hx_sorted, slab_sorted, w13, w2):
    n_tiles = hx_sorted.shape[0] // TM
    rows = MOE_TILES * TM
    last = lambda nu: (nu[0] - 1) // MOE_TILES
    return pl.pallas_call(
        _moe_kernel,
        out_shape=jax.ShapeDtypeStruct((n_tiles * TM * X_ROWS, LANES), F32),
        grid_spec=pltpu.PrefetchScalarGridSpec(
            num_scalar_prefetch=2,
            grid=(n_tiles // MOE_TILES,),
            in_specs=[pl.BlockSpec((rows, D_MODEL), lambda k, te, nu: (jnp.minimum(k, last(nu)), 0)),
                      pl.BlockSpec((rows, LANES), lambda k, te, nu: (jnp.minimum(k, last(nu)), 0)),
                      pl.BlockSpec(w13.shape, lambda k, te, nu: (0, 0, 0), pipeline_mode=pl.Buffered(1)),
                      pl.BlockSpec(w2.shape, lambda k, te, nu: (0, 0, 0), pipeline_mode=pl.Buffered(1))],
            out_specs=pl.BlockSpec((rows * X_ROWS, LANES), lambda k, te, nu: (k, 0)),
        ),
        compiler_params=pltpu.CompilerParams(dimension_semantics=("arbitrary",), vmem_limit_bytes=VMEM_LIMIT),
        name="moe",
    )(tile_e, n_used, hx_sorted, slab_sorted, w13, w2)


def _combine_kernel(pos_ref, y_ref, x1_ref, p_ref, gple_ref, wpg_ref, wpp_ref, gfin_ref, o_ref, ybuf, sem):
    i = pl.program_id(0)
    slot = lax.rem(i, 2)

    last = pl.num_programs(0) - 1

    def drain(which):
        _token_copies(TD, lambda r: _rec_copy(y_ref, 0, ybuf.at[which], 0, sem.at[which], X_ROWS), wait=True)

    @pl.when(i == 0)
    def _():
        _token_copies(TD, lambda r: _rec_copy(y_ref, pos_ref[r], ybuf.at[0], r, sem.at[0], X_ROWS))

    drain(slot)
    nxt = jnp.minimum(i + 1, last) * TD
    yb = ybuf.at[slot]
    part = TD // COMBINE_PARTS
    for q in range(COMBINE_PARTS):
        rows = slice(q * part, (q + 1) * part)
        y = jnp.concatenate([yb[pl.ds(q * part * X_ROWS + j, part, stride=X_ROWS), :] for j in range(X_ROWS)],
                            axis=1)
        x2 = x1_ref[rows, :] + y
        h3 = _rms(x2, gple_ref[...]).astype(BF16)
        x3 = x2 + _sigmoid(_dot(h3, wpg_ref[...])) * _dot(p_ref[rows, :].astype(BF16), wpp_ref[...])
        o_ref[rows, :] = _rms(x3, gfin_ref[...])
        for r in range(q * part, (q + 1) * part):
            _rec_copy(y_ref, pos_ref[nxt + r], ybuf.at[1 - slot], r, sem.at[1 - slot], X_ROWS).start(
                priority=r % 2)

    @pl.when(i == last)
    def _():
        drain(1 - slot)


def _combine(pos, y_sorted, x1, p2, gple, wpg, wpp, gfin):
    t = x1.shape[0]
    row = lambda w: pl.BlockSpec((TD, w), lambda i, pos_r: (i, 0))
    const = lambda a: pl.BlockSpec(a.shape, lambda i, pos_r: (0,) * a.ndim, pipeline_mode=pl.Buffered(1))
    return pl.pallas_call(
        _combine_kernel,
        out_shape=jax.ShapeDtypeStruct((t, D_MODEL), F32),
        grid_spec=pltpu.PrefetchScalarGridSpec(
            num_scalar_prefetch=1,
            grid=(t // TD,),
            in_specs=[pl.BlockSpec(memory_space=pl.ANY), row(D_MODEL), row(PLE_DIM),
                      const(gple), const(wpg), const(wpp), const(gfin)],
            out_specs=row(D_MODEL),
            scratch_shapes=[pltpu.VMEM((2, TD * X_ROWS, LANES), F32), pltpu.SemaphoreType.DMA((2,))],
        ),
        compiler_params=pltpu.CompilerParams(dimension_semantics=("arbitrary",), vmem_limit_bytes=VMEM_LIMIT),
        name="combine",
    )(pos, y_sorted, x1, p2, gple, wpg, wpp, gfin)


def _routing_tables(slab, hist8):
    t = slab.shape[0]
    nt = t // TD
    n_tiles = t // TM + N_BUCKETS + -(-nt * N_BUCKETS * (RUN_CHUNK - 1) // TM)
    n_tiles += n_tiles % MOE_TILES
    hist = hist8.reshape(nt, SUBLANES, LANES)[:, 0, :]
    runlen = jnp.ceil(hist / RUN_CHUNK) * RUN_CHUNK
    counts = jnp.sum(runlen, axis=0)
    padded = jnp.ceil(counts / TM) * TM
    ends = jnp.cumsum(padded)
    first = (ends - padded)[None, :] + jnp.cumsum(runlen, axis=0) - runlen
    lane = jnp.arange(N_BUCKETS, dtype=F32)[None, :]
    mine = lane == slab[:, RT_BUCKET:RT_BUCKET + 1]
    pos = jnp.sum(jnp.where(mine, jnp.repeat(first[:, :N_BUCKETS], TD, axis=0), 0.0), axis=1) + slab[:, RT_RANK]
    starts = jnp.arange(n_tiles, dtype=F32) * TM
    tile_bucket = jnp.minimum(jnp.sum(ends[None, :N_BUCKETS] <= starts[:, None], axis=1), N_BUCKETS - 1)
    group, pair = tile_bucket // 6, tile_bucket % 6
    e_lo = EXP_PER_GROUP * group + jnp.array([0, 0, 0, 1, 1, 2], jnp.int32)[pair]
    e_hi = EXP_PER_GROUP * group + jnp.array([1, 2, 3, 2, 3, 3], jnp.int32)[pair]
    tile_e = jnp.concatenate([e_lo, e_hi]).astype(jnp.int32)
    n_used = (ends[N_BUCKETS - 1] / TM).astype(jnp.int32).reshape(1)
    tail = jnp.concatenate([(ends - TM)[:N_BUCKETS], padded[:N_BUCKETS], n_used.astype(F32)]).astype(jnp.int32)
    nb = N_BUCKETS
    nch = runlen[:, :nb] / RUN_CHUNK
    cum = jnp.cumsum(nch, axis=1)
    q = jnp.arange(MAX_CHUNKS, dtype=F32)[None, :, None]
    of_b = jnp.arange(nb)[None, None, :] == jnp.minimum(jnp.sum(cum[:, None, :] <= q, axis=2), nb - 1)[..., None]
    pick = lambda a: jnp.sum(jnp.where(of_b, a[:, None, :nb], 0.0), axis=2)
    within = RUN_CHUNK * (q[..., 0] - pick(cum - nch))
    local = jnp.cumsum(runlen, axis=1) - runlen
    chunks = jnp.concatenate([cum[:, nb - 1], (pick(local) + within).reshape(-1),
                              (pick(first) + within).reshape(-1)]).astype(jnp.int32)
    return pos.astype(jnp.int32), chunks, tile_e, n_used, tail, n_tiles * TM


def _pack_inproj_weights(w):
    d = w.shape[0]
    qw = NSA_HEADS * NSA_HD
    kvw = NSA_KV * NSA_HD
    o = 0
    wq = w[:, o:o + qw]; o += qw
    wkc = w[:, o:o + kvw]; o += kvw
    wvc = w[:, o:o + kvw]; o += kvw
    wks = w[:, o:o + kvw]; o += kvw
    wvs = w[:, o:o + kvw]; o += kvw
    wkw = w[:, o:o + kvw]; o += kvw
    wvw = w[:, o:o + kvw]; o += kvw
    wga = w[:, o:o + 3 * NSA_HEADS]; o += 3 * NSA_HEADS
    wqkb = w[:, o:o + 2 * ML_WIDTH]; o += 2 * ML_WIDTH
    wvb = w[:, o:o + ML_WIDTH]; o += ML_WIDTH
    wob = w[:, o:o + ML_WIDTH]; o += ML_WIDTH
    wif = w[:, o:o + 2 * ML_HEADS]; o += 2 * ML_HEADS
    wmg = w[:, o:o + 2 * D_MODEL]
    wsm = jnp.concatenate([wga, wif, jnp.zeros((d, LANES - 3 * NSA_HEADS - 2 * ML_HEADS), w.dtype)], axis=1)
    wcat = jnp.concatenate([wq, wkc, wvc, wks, wkw, wsm, wqkb, wvb, wob, wmg], axis=1).astype(BF16)
    wtr = jnp.concatenate([wvs, wvw, wsm], axis=1).T.astype(BF16)
    return wcat, wtr


def _rope_tables(positions):
    half = ROPE_DIM // 2
    inv = ROPE_THETA ** (-jnp.arange(0, ROPE_DIM, 2, dtype=F32) / ROPE_DIM)
    ang = positions.astype(F32).reshape(-1, 1) * inv[None, :]
    cs = jnp.concatenate([jnp.cos(ang), jnp.sin(ang)], axis=1)
    d = np.arange(LANES) % NSA_HD
    spread = np.zeros((2 * half, 3 * LANES), np.float32)
    lanes = np.arange(LANES)
    rot = d < ROPE_DIM
    spread[(d % half)[rot], lanes[rot]] = 1.0
    hi = (d >= half) & rot
    spread[half + (d % half)[hi], LANES + lanes[hi]] = 1.0
    lo = d < half
    spread[half + (d % half)[lo], 2 * LANES + lanes[lo]] = -1.0
    unrotated = (~rot).astype(np.float32).reshape(1, LANES)
    return cs, jnp.asarray(spread, BF16), jnp.asarray(unrotated)


def _pack_compress_weights(w1, w2, pe):
    half = CMP_LEN // 2
    w1r = w1.reshape(2, half, NSA_HD, CMP_HIDDEN)
    outs = []
    for part in range(2):
        wb = w1r[part].astype(BF16)
        zb = jnp.zeros_like(wb)
        wp = jnp.stack([jnp.stack([wb, zb], axis=2), jnp.stack([zb, wb], axis=2)], axis=1)
        outs.append(wp.reshape(half * NSA_KV * NSA_HD, NSA_KV * CMP_HIDDEN))
    pe8 = jnp.broadcast_to(pe.reshape(1, CMP_LEN * NSA_HD), (SUBLANES, CMP_LEN * NSA_HD)).astype(BF16)
    return outs[0], outs[1], pe8, w1.astype(BF16)


def _stages(x, p, positions, g_mix, w_in, b_if, w_ck1, w_ck2, pe_ck, w_cv1, w_cv2, pe_cv, w_conv, b_conv, g_hn, w_pa, w_pb, w_out, g_ffn, w_rg, b_rg, w_re, b_re, w_e13, w_e2, g_ple, w_pg, w_pp, g_final):
    b, s, d = x.shape
    t = b * s
    cs, spread, unrot = _rope_tables(positions)
    assert w_in.shape[0] == 1, "the final norm is fused into the layer's last kernel: single-layer problem only"
    for i in range(w_in.shape[0]):
        x2 = x.reshape(t, d)
        wcat, wtr = _pack_inproj_weights(w_in[i])
        (qpad, kc_tok, vc_tok, ks, kw, vst, vwt, sm, smt, qkb, vb, ob, mg) = _inproj(
            x2, g_mix[i].reshape(1, d), wcat, wtr, cs, spread, unrot)
        wka, wkb, pek, w1k = _pack_compress_weights(w_ck1[i], w_ck2[i], pe_ck[i])
        wva, wvb, pev, w1v = _pack_compress_weights(w_cv1[i], w_cv2[i], pe_cv[i])
        zpad = jnp.zeros((CMP_HIDDEN, NSA_HD), F32)
        w2k = jnp.stack([jnp.concatenate([w_ck2[i], zpad], axis=1),
                         jnp.concatenate([zpad, w_ck2[i]], axis=1)]).astype(BF16)
        w2vt = w_cv2[i].T.astype(BF16)
        nrow = s // CMP_STRIDE
        rk = kc_tok.reshape(b, nrow, CMP_STRIDE * LANES)
        rv = vc_tok.reshape(b, nrow, CMP_STRIDE * LANES)
        kcb, vct = _compress(rk, rv, wka, wkb, wva, wvb, pek, pev, w1k, w1v, w2k, w2vt)
        ya = _nsa(qpad, kcb, vct, ks, kw, vst, vwt, smt, b, s)
        bif = b_if[i].astype(F32)
        bifc = jnp.zeros((1, LANES), F32).at[0, SM_I:SM_I + 2 * ML_HEADS].set(bif)
        bifr = bifc.reshape(LANES, 1)
        yb = _mlstm(qkb, vb, ob, sm, smt, w_conv[i], b_conv[i].reshape(1, -1), bifc, bifr,
                    g_hn[i].reshape(1, -1), b, s)
        wr = jnp.concatenate([w_re[i], w_rg[i], jnp.zeros((d, LANES - N_EXPERTS - N_GROUPS), F32)], axis=1)
        wr_hi = wr.astype(BF16)
        wr = jnp.concatenate([wr_hi, (wr - wr_hi.astype(F32)).astype(BF16)], axis=1)
        br =jnp.concatenate([b_re[i], b_rg[i], jnp.zeros((LANES - N_EXPERTS - N_GROUPS,), F32)]).reshape(1, LANES)
        x1, h2, slab, hist8 = _merge(ya, yb, mg, x2, w_pa[i].astype(BF16), w_pb[i].astype(BF16),
                                     w_out[i].astype(BF16), g_ffn[i].reshape(1, d), wr, br)
        pos, chunks, tile_e, n_used, tail, n_slots = _routing_tables(slab, hist8)
        hx_sorted, slab_sorted = _dispatch(chunks, tail, h2, slab, n_slots)
        y_sorted = _moe(tile_e, n_used, hx_sorted, slab_sorted, w_e13[i].astype(BF16), w_e2[i].astype(BF16))
        out = _combine(pos, y_sorted, x1, p[i].reshape(t, PLE_DIM), g_ple[i].reshape(1, d), w_pg[i].astype(BF16),
                       w_pp[i].astype(BF16), g_final.reshape(1, d))
        x = out.reshape(b, s, d)
    return dict(out=x, qpad=qpad, ks=ks, kcb=kcb, vct=vct, y_a=ya, y_b=yb, x1=x1, pos=pos)


def kernel(x, p, positions, g_mix, w_in, b_if, w_ck1, w_ck2, pe_ck, w_cv1, w_cv2, pe_cv, w_conv, b_conv, g_hn, w_pa, w_pb, w_out, g_ffn, w_rg, b_rg, w_re, b_re, w_e13, w_e2, g_ple, w_pg, w_pp, g_final):
    return _stages(x, p, positions, g_mix, w_in, b_if, w_ck1, w_ck2, pe_ck, w_cv1, w_cv2, pe_cv, w_conv, b_conv, g_hn,
                   w_pa, w_pb, w_out, g_ffn, w_rg, b_rg, w_re, b_re, w_e13, w_e2, g_ple, w_pg, w_pp, g_final)["out"]
```

```python
import functools
import math

import numpy as np
import jax
import jax.numpy as jnp
from jax import lax
from jax.experimental import pallas as pl
from jax.experimental.pallas import tpu as pltpu

F32 = jnp.float32
BF16 = jnp.bfloat16

EPS = 1e-6
NEG = -1e30

D_MODEL = 1024
PLE_DIM = 256
NSA_HEADS = 8
NSA_KV = 2
NSA_HPG = NSA_HEADS // NSA_KV
NSA_HD = 64
CMP_LEN = 32
CMP_STRIDE = 16
CMP_HIDDEN = 256
SEL_BLOCK = 64
SEL_TOPK = 16
SEL_FORCE = 1000.0
WINDOW = 512
ROPE_THETA = 500000.0
ROPE_DIM = NSA_HD // 4
ML_HEADS = 4
ML_HD = 128
ML_WIDTH = ML_HEADS * ML_HD
CONV_W = 4
N_GROUPS = 4
EXP_PER_GROUP = 4
N_EXPERTS = N_GROUPS * EXP_PER_GROUP
D_EXPERT = 256

LANES = 128
SUBLANES = 8
QT = 128
KC = 128
SEL_GROUP = 512
VT_PAD = 16
VT_ROWS = NSA_HD + VT_PAD
ML_CHUNK = 128
ML_BLOCK = 256
TD = 512
TM = 256
VMEM_LIMIT = 56 * 1024 * 1024

_NT = (((1,), (1,)), ((), ()))
_TN = (((0,), (0,)), ((), ()))

SM_GATE = 0
SM_I = 3 * NSA_HEADS
SM_F = SM_I + ML_HEADS


def _dot(a, b):
    return jnp.dot(a, b, preferred_element_type=F32)


def _dot_nt(a, b):
    return lax.dot_general(a, b, _NT, preferred_element_type=F32)


def _split3(x):
    hi = x.astype(BF16)
    r1 = x - hi.astype(F32)
    mid = r1.astype(BF16)
    lo = (r1 - mid.astype(F32)).astype(BF16)
    return hi, mid, lo


def _rms(x, g):
    return x * lax.rsqrt(jnp.mean(x * x, axis=-1, keepdims=True) + EPS) * g


def _sigmoid(x):
    return 0.5 + 0.5 * jnp.tanh(0.5 * x)


def _const_spec(shape):
    nd = len(shape)
    return pl.BlockSpec(shape, lambda *_: (0,) * nd, pipeline_mode=pl.Buffered(1))


_C_Q = 0
_C_KC = _C_Q + NSA_HEADS * NSA_HD
_C_VC = _C_KC + LANES
_C_KS = _C_VC + LANES
_C_KW = _C_KS + LANES
_C_SM = _C_KW + LANES
_C_QKB = _C_SM + LANES
_C_VB = _C_QKB + 2 * ML_WIDTH
_C_OB = _C_VB + ML_WIDTH
_C_MG = _C_OB + ML_WIDTH
_C_END = _C_MG + 2 * D_MODEL


def _inproj_kernel(x_ref, g_ref, w_ref, wt_ref, cs_ref, spread_ref, unrot_ref,
                   q_ref, kc_ref, vc_ref, ks_ref, kw_ref, vst_ref, vwt_ref, sm_ref, smt_ref,
                   qkb_ref, vb_ref, ob_ref, mg_ref):
    hn = _rms(x_ref[...], g_ref[...]).astype(BF16)
    tables = sum(_dot(part, spread_ref[...]) for part in _split3(cs_ref[...]))
    rc = tables[:, 0:LANES] + unrot_ref[...]
    rp = tables[:, LANES:2 * LANES]
    rm = tables[:, 2 * LANES:3 * LANES]

    def rope(z):
        half = ROPE_DIM // 2
        return z * rc + pltpu.roll(z, half, 1) * rp + pltpu.roll(z, LANES - half, 1) * rm

    scale = NSA_HD ** -0.5 * math.log2(math.e)
    for h in range(NSA_HEADS * NSA_HD // LANES):
        z = _dot(hn, w_ref[:, _C_Q + h * LANES:_C_Q + (h + 1) * LANES])
        q_ref[:, h * LANES:(h + 1) * LANES] = (rope(z) * scale).astype(BF16)
    kc_ref[...] = rope(_dot(hn, w_ref[:, _C_KC:_C_KC + LANES])).astype(BF16)
    vc_ref[...] = _dot(hn, w_ref[:, _C_VC:_C_VC + LANES]).astype(BF16)
    ks_ref[...] = rope(_dot(hn, w_ref[:, _C_KS:_C_KS + LANES])).astype(BF16)
    kw_ref[...] = rope(_dot(hn, w_ref[:, _C_KW:_C_KW + LANES])).astype(BF16)
    sm_ref[...] = _dot(hn, w_ref[:, _C_SM:_C_SM + LANES])
    for c0 in range(0, 2 * ML_WIDTH, 512):
        qkb_ref[:, c0:c0 + 512] = _dot(hn, w_ref[:, _C_QKB + c0:_C_QKB + c0 + 512]).astype(BF16)
    vb_ref[...] = _dot(hn, w_ref[:, _C_VB:_C_VB + ML_WIDTH]).astype(BF16)
    ob_ref[...] = _dot(hn, w_ref[:, _C_OB:_C_OB + ML_WIDTH]).astype(BF16)
    for c0 in range(0, 2 * D_MODEL, 512):
        mg_ref[:, c0:c0 + 512] = _dot(hn, w_ref[:, _C_MG + c0:_C_MG + c0 + 512]).astype(BF16)
    zt = _dot_nt(wt_ref[...], hn)
    ones_rows = (lax.broadcasted_iota(jnp.int32, (VT_PAD, KC), 0) == 0).astype(BF16)
    for i in range(TD // KC):
        for ref, r0 in ((vst_ref, 0), (vwt_ref, LANES)):
            zc = zt[r0:r0 + LANES, i * KC:(i + 1) * KC].astype(BF16)
            ref[i] = jnp.concatenate([piece for g in range(NSA_KV)
                                      for piece in (zc[g * NSA_HD:(g + 1) * NSA_HD, :], ones_rows)], axis=0)
    smt_ref[...] = zt[2 * LANES:3 * LANES, :]


def _inproj(x2, g_mix, wcat, wtr, cs, spread, unrot):
    t = x2.shape[0]
    row = lambda w: pl.BlockSpec((TD, w), lambda i: (i, 0))
    out_shape = [
        jax.ShapeDtypeStruct((t, NSA_HEADS * NSA_HD), BF16),
        jax.ShapeDtypeStruct((t, LANES), BF16),
        jax.ShapeDtypeStruct((t, LANES), BF16),
        jax.ShapeDtypeStruct((t, LANES), BF16),
        jax.ShapeDtypeStruct((t, LANES), BF16),
        jax.ShapeDtypeStruct((t // KC, NSA_KV * VT_ROWS, KC), BF16),
        jax.ShapeDtypeStruct((t // KC, NSA_KV * VT_ROWS, KC), BF16),
        jax.ShapeDtypeStruct((t, LANES), F32),
        jax.ShapeDtypeStruct((LANES, t), F32),
        jax.ShapeDtypeStruct((t, 2 * ML_WIDTH), BF16),
        jax.ShapeDtypeStruct((t, ML_WIDTH), BF16),
        jax.ShapeDtypeStruct((t, ML_WIDTH), BF16),
        jax.ShapeDtypeStruct((t, 2 * D_MODEL), BF16),
    ]
    chunk3 = pl.BlockSpec((TD // KC, NSA_KV * VT_ROWS, KC), lambda i: (i, 0, 0))
    out_specs = [row(NSA_HEADS * NSA_HD), row(LANES), row(LANES), row(LANES), row(LANES), chunk3, chunk3,
                 row(LANES), pl.BlockSpec((LANES, TD), lambda i: (0, i)),
                 row(2 * ML_WIDTH), row(ML_WIDTH), row(ML_WIDTH), row(2 * D_MODEL)]
    return pl.pallas_call(
        _inproj_kernel,
        out_shape=out_shape,
        grid=(t // TD,),
        in_specs=[row(D_MODEL), _const_spec((1, D_MODEL)), _const_spec((D_MODEL, _C_END)),
                  _const_spec((3 * LANES, D_MODEL)), row(cs.shape[1]), _const_spec(spread.shape),
                  _const_spec(unrot.shape)],
        out_specs=out_specs,
        compiler_params=pltpu.CompilerParams(dimension_semantics=("arbitrary",), vmem_limit_bytes=VMEM_LIMIT),
        name="inproj",
    )(x2, g_mix, wcat, wtr, cs, spread, unrot)


def _gelu_tanh(x):
    return 0.5 * x * (1.0 + jnp.tanh(math.sqrt(2.0 / math.pi) * (x + 0.044715 * x * x * x)))


def _compress_kernel(rk_ref, rv_ref, wka_ref, wkb_ref, wva_ref, wvb_ref, pek_ref, pev_ref,
                     w1k_ref, w1v_ref, w2k_ref, w2vt_ref, kc_ref, vct_ref):
    nrow = rk_ref.shape[0]

    def hidden(r_ref, wa_ref, wb_ref, pe_ref, w1_ref):
        r = r_ref[...]
        ha = _dot(r, wa_ref[...])
        hb = _dot(r, wb_ref[...])
        hb = pltpu.roll(hb, nrow - 1, 0)
        c = _dot(pe_ref[...], w1_ref[...])[0:1, :]
        return [_gelu_tanh(ha[:, g * CMP_HIDDEN:(g + 1) * CMP_HIDDEN] + hb[:, g * CMP_HIDDEN:(g + 1) * CMP_HIDDEN] + c).astype(BF16)
                for g in range(NSA_KV)]

    ak = hidden(rk_ref, wka_ref, wkb_ref, pek_ref, w1k_ref)
    kc_ref[...] = (_dot(ak[0], w2k_ref[0]) + _dot(ak[1], w2k_ref[1])).astype(BF16)
    av = hidden(rv_ref, wva_ref, wvb_ref, pev_ref, w1v_ref)
    for g in range(NSA_KV):
        vct_ref[g * NSA_HD:(g + 1) * NSA_HD, :] = _dot_nt(w2vt_ref[...], av[g]).astype(BF16)


def _compress(rk, rv, wka, wkb, wva, wvb, pek, pev, w1k, w1v, w2k, w2vt):
    b, nrow, width = rk.shape
    blk = pl.BlockSpec((None, nrow, width), lambda i: (i, 0, 0))
    return pl.pallas_call(
        _compress_kernel,
        out_shape=[jax.ShapeDtypeStruct((b, nrow, LANES), BF16),
                   jax.ShapeDtypeStruct((b, LANES, nrow), BF16)],
        grid=(b,),
        in_specs=[blk, blk] + [_const_spec(a.shape) for a in (wka, wkb, wva, wvb, pek, pev, w1k, w1v, w2k, w2vt)],
        out_specs=[pl.BlockSpec((None, nrow, LANES), lambda i: (i, 0, 0)),
                   pl.BlockSpec((None, LANES, nrow), lambda i: (i, 0, 0))],
        compiler_params=pltpu.CompilerParams(dimension_semantics=("arbitrary",), vmem_limit_bytes=VMEM_LIMIT),
        name="compress",
    )(rk, rv, wka, wkb, wva, wvb, pek, pev, w1k, w1v, w2k, w2vt)


def _nsa_kernel(q_ref, kc_ref, vct_ref, ks_ref, kw_ref, vst_ref, vwt_ref, smt_ref, o_ref, bias_scr, sx_scr, sy_scr):
    c = pl.program_id(1)
    t0 = c * QT
    ncmp = kc_ref.shape[0]
    nsel = bias_scr.shape[0]
    nw = WINDOW // KC + 1
    gw = NSA_HPG * QT
    width = NSA_KV * gw

    def per_group(x):
        return [x[:, g * gw:(g + 1) * gw] for g in range(NSA_KV)]

    def pv(vt, p):
        rows = vt.shape[0] // NSA_KV
        pb = p.astype(BF16)
        return jnp.concatenate([_dot(vt[g * rows:(g + 1) * rows, :], pg) for g, pg in enumerate(per_group(pb))],
                               axis=1)

    def normalised(acc):
        return acc[0:NSA_HD, :] / acc[NSA_HD:NSA_HD + 1, :]

    low_half = lax.broadcasted_iota(jnp.int32, (1, LANES), 1) < NSA_HD
    q_heads = []
    for h in range(NSA_HEADS):
        pair = q_ref[:, (h // 2) * LANES:(h // 2 + 1) * LANES].astype(F32)
        want_low = h // NSA_HPG == 0
        if (h % 2 == 0) != want_low:
            pair = pltpu.roll(pair, NSA_HD, 1)
        q_heads.append(jnp.where(low_half if want_low else ~low_half, pair, 0.0).astype(BF16))
    qs = jnp.concatenate(q_heads, axis=0)
    u_row = lax.broadcasted_iota(jnp.int32, (1, width), 1) % QT
    t_row = t0 + u_row
    r_kc = lax.broadcasted_iota(jnp.int32, (KC, 1), 0)

    n_grp = ks_ref.shape[0] // SEL_GROUP
    n_full = lax.shift_right_logical(t0, int(math.log2(SEL_GROUP)))

    def qk_group(j):
        return _dot_nt(ks_ref[pl.ds(pl.multiple_of(j * SEL_GROUP, SEL_GROUP), SEL_GROUP), :], qs)

    sc = _dot_nt(kc_ref[...], qs)

    w_slabs, w_chunks = [], []
    for i in range(nw):
        jj = c - (nw - 1) + i
        jc = jnp.maximum(jj, 0)
        si = _dot_nt(kw_ref[pl.ds(pl.multiple_of(jc * KC, KC), KC), :], qs)
        if i == 0:
            keep = jnp.where(jj >= 0, r_kc, -1) > u_row
        elif i == nw - 1:
            keep = r_kc <= u_row
        else:
            keep = jj >= 0
        w_slabs.append(jnp.where(keep, si, NEG))
        w_chunks.append(jc)

    n_col = lax.broadcasted_iota(jnp.int32, (ncmp, 1), 0)
    last_tok = jnp.where(n_col < ncmp - 1, CMP_STRIDE * n_col + (CMP_LEN - 1), jnp.iinfo(jnp.int32).max)
    s = jnp.where(last_tok <= t_row, sc, NEG)
    m = jnp.max(s, axis=0, keepdims=True)
    e = jnp.exp2(s - m)
    anyv = (t_row >= CMP_LEN - 1).astype(F32)
    p = e * (anyv / jnp.sum(e, axis=0, keepdims=True))
    o_cmp = pv(vct_ref[...], p)

    psums = []
    for pg in per_group(p):
        acc_p = pg[:, 0:QT]
        for h in range(1, NSA_HPG):
            acc_p = acc_p + pg[:, h * QT:(h + 1) * QT]
        psums.append(acc_p)
    psum = jnp.concatenate(psums, axis=1)
    nq2 = NSA_KV * QT
    s_col = lax.broadcasted_iota(jnp.int32, (nsel, 1), 0)
    n_lane = lax.broadcasted_iota(jnp.int32, (1, ncmp), 1)
    ov = ((CMP_STRIDE * n_lane < SEL_BLOCK * (s_col + 1)) & (CMP_STRIDE * n_lane + (CMP_LEN - 1) >= SEL_BLOCK * s_col)
          ).astype(BF16)
    imp = sum(_dot(ov, part) for part in _split3(psum))

    s_diag = qk_group(n_full)
    s_own = _dot_nt(ks_ref[pl.ds(pl.multiple_of(t0, KC), KC), :], qs)
    sx_scr[...] = qk_group(0)

    mxw = w_slabs[0]
    for sl in w_slabs[1:]:
        mxw = jnp.maximum(mxw, sl)
    mw = jnp.max(mxw, axis=0, keepdims=True)
    acc_w = jnp.zeros((VT_ROWS, width), F32)
    for sl, jc in zip(w_slabs, w_chunks):
        acc_w = acc_w + pv(vwt_ref[jc], jnp.exp2(sl - mw))
    o_win = normalised(acc_w)

    t1 = t0 + lax.broadcasted_iota(jnp.int32, (1, nq2), 1) % QT
    cur = lax.shift_right_logical(t1, 6)
    forced = (s_col == 0) | (s_col == cur) | (s_col == cur - 1)
    valid = SEL_BLOCK * s_col <= t1
    val = jnp.where(valid, jnp.where(forced, imp + SEL_FORCE, imp), NEG)
    sub = 8
    r_sub = lax.broadcasted_iota(jnp.int32, (sub, 1), 0)
    blocks = [val[r * sub:(r + 1) * sub, :] for r in range(nsel // sub)]
    ranks = [jnp.zeros((sub, nq2), F32) for _ in blocks]
    for i in range(nsel):
        vi = val[i:i + 1, :]
        for r, blk in enumerate(blocks):
            if i < r * sub:
                beats = vi >= blk
            elif i >= (r + 1) * sub:
                beats = vi > blk
            else:
                beats = (vi > blk) | ((vi == blk) & (r_sub > i - r * sub))
            ranks[r] = ranks[r] + jnp.where(beats, 1.0, 0.0)
    bias = jnp.where(jnp.concatenate(ranks, axis=0) < SEL_TOPK, 0.0, NEG).astype(F32)
    bias_scr[...] = jnp.concatenate([bias[:, g * QT:(g + 1) * QT] for g in range(NSA_KV) for _ in range(NSA_HPG)],
                                    axis=1)

    blk_per_grp = SEL_GROUP // SEL_BLOCK
    chunk_per_grp = SEL_GROUP // KC
    blk_per_chunk = KC // SEL_BLOCK

    def sel_update(j, sj, carry):
        m_o, acc = carry
        brows = [bias_scr[pl.ds(blk_per_grp * j + i, 1), :] for i in range(blk_per_grp)]

        def block(i):
            return sj[i * SEL_BLOCK:(i + 1) * SEL_BLOCK, :]

        mx = None
        for i in range(blk_per_grp):
            sl = block(i) + brows[i]
            mx = sl if mx is None else jnp.maximum(mx, sl)
        m_n = jnp.maximum(m_o, jnp.max(mx, axis=0, keepdims=True))
        a = jnp.exp2(m_o - m_n)
        acc = a * acc
        for ci in range(chunk_per_grp):
            parts = [jnp.exp2(block(i) + (brows[i] - m_n))
                     for i in range(blk_per_chunk * ci, blk_per_chunk * (ci + 1))]
            acc = acc + pv(vst_ref[chunk_per_grp * j + ci], jnp.concatenate(parts, axis=0))
        return m_n, acc

    def seed_state():
        first_own = blk_per_chunk * (c % chunk_per_grp)
        brows = [bias_scr[pl.ds(blk_per_grp * n_full + i, 1), :] + jnp.where(i < first_own, 0.0, NEG)
                 for i in range(blk_per_grp)]
        own = jnp.where(r_kc <= u_row, s_own, NEG)
        mx = jnp.maximum(own[0:SEL_BLOCK, :], own[SEL_BLOCK:KC, :])
        for i in range(blk_per_grp):
            mx = jnp.maximum(mx, s_diag[i * SEL_BLOCK:(i + 1) * SEL_BLOCK, :] + brows[i])
        m_n = jnp.max(mx, axis=0, keepdims=True)
        acc = pv(vst_ref[c], jnp.exp2(own - m_n))
        for ci in range(chunk_per_grp):
            parts = [jnp.exp2(s_diag[i * SEL_BLOCK:(i + 1) * SEL_BLOCK, :] + (brows[i] - m_n))
                     for i in range(blk_per_chunk * ci, blk_per_chunk * (ci + 1))]
            acc = acc + pv(vst_ref[chunk_per_grp * n_full + ci], jnp.concatenate(parts, axis=0))
        return m_n, acc

    seeded = seed_state()

    def pair_body(jp, carry):
        ja, jb = 2 * jp, 2 * jp + 1
        sy_scr[...] = qk_group(jb)
        carry = sel_update(ja, sx_scr, carry)
        sx_scr[...] = qk_group(jnp.minimum(ja + 2, n_grp - 1))
        return sel_update(jb, sy_scr, carry)

    n_pairs = lax.shift_right_logical(n_full, 1)
    carry = lax.fori_loop(0, n_pairs, pair_body, seeded)
    _, acc_s = lax.cond(n_full - 2 * n_pairs == 1, lambda cr: sel_update(n_full - 1, sx_scr, cr), lambda cr: cr,
                        carry)
    o_sel = normalised(acc_s)

    def gate_row(br):
        rows = [smt_ref[SM_GATE + 3 * h + br:SM_GATE + 3 * h + br + 1, :] for h in range(NSA_HEADS)]
        return _sigmoid(jnp.concatenate(rows, axis=1))

    o_t = gate_row(0) * o_cmp + gate_row(1) * o_sel + gate_row(2) * o_win
    for pr in range(NSA_HEADS // 2):
        xp = jnp.concatenate([o_t[:, (2 * pr) * QT:(2 * pr + 1) * QT], o_t[:, (2 * pr + 1) * QT:(2 * pr + 2) * QT]], axis=0)
        o_ref[:, pr * LANES:(pr + 1) * LANES] = xp.T.astype(BF16)


def _nsa(qpad, kcb, vct, ks, kw, vst, vwt, smt, b, s):
    nq = s // QT
    ncmp = kcb.shape[1]
    return pl.pallas_call(
        _nsa_kernel,
        out_shape=jax.ShapeDtypeStruct((b * s, NSA_HEADS * NSA_HD), BF16),
        grid=(b, nq),
        in_specs=[
            pl.BlockSpec((QT, NSA_HEADS * NSA_HD), lambda bi, c: (bi * nq + c, 0)),
            pl.BlockSpec((None, ncmp, LANES), lambda bi, c: (bi, 0, 0)),
            pl.BlockSpec((None, NSA_KV * NSA_HD, ncmp), lambda bi, c: (bi, 0, 0)),
            pl.BlockSpec((s, LANES), lambda bi, c: (bi, 0)),
            pl.BlockSpec((s, LANES), lambda bi, c: (bi, 0)),
            pl.BlockSpec((s // KC, NSA_KV * VT_ROWS, KC), lambda bi, c: (bi, 0, 0)),
            pl.BlockSpec((s // KC, NSA_KV * VT_ROWS, KC), lambda bi, c: (bi, 0, 0)),
            pl.BlockSpec((LANES, QT), lambda bi, c: (0, bi * nq + c)),
        ],
        out_specs=pl.BlockSpec((QT, NSA_HEADS * NSA_HD), lambda bi, c: (bi * nq + c, 0)),
        scratch_shapes=[pltpu.VMEM((s // SEL_BLOCK, NSA_HEADS * QT), F32),
                        pltpu.VMEM((SEL_GROUP, NSA_HEADS * QT), F32),
                        pltpu.VMEM((SEL_GROUP, NSA_HEADS * QT), F32)],
        compiler_params=pltpu.CompilerParams(dimension_semantics=("arbitrary", "arbitrary"),
                                             vmem_limit_bytes=VMEM_LIMIT),
        name="nsa",
    )(qpad, kcb, vct, ks, kw, vst, vwt, smt)


def _log_sigmoid(x):
    return jnp.minimum(x, 0.0) - jnp.log(1.0 + jnp.exp(-jnp.abs(x)))


def _mlstm_kernel(qk_ref, v_ref, og_ref, sm_ref, smt_ref, wc_ref, bc_ref, bifc_ref, bifr_ref, ghn_ref,
                  y_ref, tail_scr, ct_scr, n_scr, m_scr):
    lc = ML_CHUNK

    @pl.when(pl.program_id(1) == 0)
    def _():
        tail_scr[...] = jnp.zeros_like(tail_scr)
        ct_scr[...] = jnp.zeros_like(ct_scr)
        n_scr[...] = jnp.zeros_like(n_scr)
        m_scr[...] = jnp.zeros_like(m_scr)

    u = qk_ref[...]
    tail = tail_scr[...]
    rr8 = lax.broadcasted_iota(jnp.int32, (SUBLANES, 1), 0)
    sr = lax.broadcasted_iota(jnp.int32, (ML_BLOCK, ML_BLOCK), 0)
    sc_ = lax.broadcasted_iota(jnp.int32, (ML_BLOCK, ML_BLOCK), 1)
    y = bc_ref[...] + wc_ref[CONV_W - 1:CONV_W, :] * u.astype(F32)
    for k in range(1, CONV_W):
        down = _dot((sr - sc_ == k).astype(BF16), u)
        head = jnp.where(rr8 < k, pltpu.roll(tail, k, 0), down[0:SUBLANES, :])
        y = y + wc_ref[CONV_W - 1 - k:CONV_W - k, :] * jnp.concatenate([head, down[SUBLANES:, :]], axis=0)
    tail_scr[...] = u[ML_BLOCK - SUBLANES:ML_BLOCK, :].astype(F32)
    qkc = y * _sigmoid(y)
    q_all = qkc[:, 0:ML_WIDTH].astype(BF16)
    k_all = (qkc[:, ML_WIDTH:2 * ML_WIDTH] * (ML_HD ** -0.5)).astype(BF16)

    ifc = sm_ref[...] + bifc_ref[...]
    ifr = smt_ref[...] + bifr_ref[...]
    lfc = _log_sigmoid(ifc)
    lfr = _log_sigmoid(ifr)
    rr = lax.broadcasted_iota(jnp.int32, (lc, lc), 0)
    cc = lax.broadcasted_iota(jnp.int32, (lc, lc), 1)
    causal = rr >= cc
    tri_l = causal.astype(F32)
    tri_u = (rr <= cc).astype(F32)

    for ci in range(ML_BLOCK // lc):
        lo, hi = ci * lc, (ci + 1) * lc
        bc_all = jnp.dot(tri_l, lfc[lo:hi, :], preferred_element_type=F32, precision=lax.Precision.HIGHEST)
        br_all = jnp.dot(lfr[:, lo:hi], tri_u, preferred_element_type=F32, precision=lax.Precision.HIGHEST)
        heads = range(ML_HEADS)
        hsl = [slice(h * ML_HD, (h + 1) * ML_HD) for h in heads]
        bcol = [bc_all[:, SM_F + h:SM_F + h + 1] for h in heads]
        brow = [br_all[SM_F + h:SM_F + h + 1, :] for h in heads]
        icol = [ifc[lo:hi, SM_I + h:SM_I + h + 1] for h in heads]
        irow = [ifr[SM_I + h:SM_I + h + 1, lo:hi] for h in heads]
        mprev = [m_scr[h][:, 0:1] for h in heads]
        qh = [q_all[lo:hi, hsl[h]] for h in heads]
        kh = [k_all[lo:hi, hsl[h]] for h in heads]
        vh = [v_ref[lo:hi, hsl[h]] for h in heads]
        ct = [ct_scr[h] for h in heads]
        nrow = [n_scr[h] for h in heads]
        qk = [_dot_nt(qh[h], kh[h]) for h in heads]
        qc = [_dot(qh[h], ct[h].astype(BF16)) for h in heads]
        dmat = [jnp.where(causal, bcol[h] - brow[h] + irow[h], NEG) for h in heads]
        inter = [bcol[h] + mprev[h] for h in heads]
        mt = [jnp.maximum(jnp.max(dmat[h], axis=-1, keepdims=True), inter[h]) for h in heads]
        a = [jnp.exp(dmat[h] - mt[h]) * qk[h] for h in heads]
        dec = [jnp.exp(inter[h] - mt[h]) for h in heads]
        num = [_dot(a[h].astype(BF16), vh[h]) + dec[h] * qc[h] for h in heads]
        den = [jnp.sum(a[h], axis=-1, keepdims=True)
               + dec[h] * jnp.sum(qh[h].astype(F32) * nrow[h], axis=-1, keepdims=True) for h in heads]
        blast = [bcol[h][lc - 1:lc, :] for h in heads]
        mnew = [jnp.maximum(blast[h] + mprev[h], jnp.max(blast[h] - brow[h] + irow[h], axis=-1, keepdims=True))
                for h in heads]
        wprev = [jnp.exp(blast[h] + mprev[h] - mnew[h]) for h in heads]
        kwt = [kh[h].astype(F32) * jnp.exp(blast[h] - bcol[h] + icol[h] - mnew[h]) for h in heads]
        for h in heads:
            ct_scr[h] = wprev[h] * ct[h] + lax.dot_general(kwt[h].astype(BF16), vh[h], _TN,
                                                           preferred_element_type=F32)
            n_scr[h] = wprev[h] * nrow[h] + jnp.sum(kwt[h], axis=0, keepdims=True)
            m_scr[h] = jnp.broadcast_to(mnew[h], (1, LANES))
        hm = [num[h] / jnp.maximum(jnp.abs(den[h]), jnp.exp(-mt[h])) * _sigmoid(og_ref[lo:hi, hsl[h]].astype(F32))
              for h in heads]
        for h in heads:
            y_ref[lo:hi, hsl[h]] = _rms(hm[h], ghn_ref[:, hsl[h]]).astype(BF16)


def _mlstm(qkb, vb, ob, sm, smt, wconv, bconv, bifc, bifr, ghn, b, s):
    nb = s // ML_BLOCK
    row = lambda w: pl.BlockSpec((ML_BLOCK, w), lambda bi, j: (bi * nb + j, 0))
    return pl.pallas_call(
        _mlstm_kernel,
        out_shape=jax.ShapeDtypeStruct((b * s, ML_WIDTH), BF16),
        grid=(b, nb),
        in_specs=[row(2 * ML_WIDTH), row(ML_WIDTH), row(ML_WIDTH), row(LANES),
                  pl.BlockSpec((LANES, ML_BLOCK), lambda bi, j: (0, bi * nb + j)),
                  _const_spec(wconv.shape), _const_spec(bconv.shape), _const_spec(bifc.shape),
                  _const_spec(bifr.shape), _const_spec(ghn.shape)],
        out_specs=row(ML_WIDTH),
        scratch_shapes=[pltpu.VMEM((SUBLANES, 2 * ML_WIDTH), F32),
                        pltpu.VMEM((ML_HEADS, ML_HD, ML_HD), F32),
                        pltpu.VMEM((ML_HEADS, 1, ML_HD), F32),
                        pltpu.VMEM((ML_HEADS, 1, LANES), F32)],
        compiler_params=pltpu.CompilerParams(dimension_semantics=("arbitrary", "arbitrary"),
                                             vmem_limit_bytes=VMEM_LIMIT),
        name="mlstm",
    )(qkb, vb, ob, sm, smt, wconv, bconv, bifc, bifr, ghn)


RT_BUCKET = N_EXPERTS
RT_RANK = N_EXPERTS + 1
RT_WLO = N_EXPERTS + 2
RT_WHI = N_EXPERTS + 3
RT_LPOS = N_EXPERTS + 4
N_BUCKETS = N_GROUPS * 6
X_ROWS = D_MODEL // LANES
DMA_UNROLL = 8
RUN_CHUNK = SUBLANES
MAX_CHUNKS = TD // RUN_CHUNK + N_BUCKETS
SORT_ROWS = MAX_CHUNKS * RUN_CHUNK
COMBINE_PARTS = 2


def _merge_kernel(ya_ref, yb_ref, mg_ref, x_ref, wpa_ref, wpb_ref, wout_ref, gffn_ref, wr_ref, br_ref,
                  x1_ref, h2_ref, slab_ref, hist_ref):
    halves = [slice(i * (TD // 2), (i + 1) * (TD // 2)) for i in range(2)]
    pa = [_dot(ya_ref[hs, :], wpa_ref[...]) for hs in halves]
    pb = [_dot(yb_ref[hs, :], wpb_ref[...]) for hs in halves]
    mixed = [(_sigmoid(mg_ref[hs, 0:D_MODEL].astype(F32)) * pa[i]
              + _sigmoid(mg_ref[hs, D_MODEL:2 * D_MODEL].astype(F32)) * pb[i]).astype(BF16)
             for i, hs in enumerate(halves)]
    x1 = [x_ref[hs, :] + _dot(mixed[i], wout_ref[...]) for i, hs in enumerate(halves)]
    h2 = [_rms(x1[i], gffn_ref[...]) for i in range(2)]
    for i, hs in enumerate(halves):
        x1_ref[hs, :] = x1[i]
        h2_ref[hs, :] = h2[i]

    h_hi = [h.astype(BF16) for h in h2]
    h_lo = [(h2[i] - h_hi[i].astype(F32)).astype(BF16) for i in range(2)]
    r_hi = [_dot(h, wr_ref[...]) for h in h_hi]
    logit = jnp.concatenate([r_hi[i][:, 0:LANES] + r_hi[i][:, LANES:2 * LANES] + _dot(h_lo[i], wr_ref[:, 0:LANES])
                             for i in range(2)], axis=0) + br_ref[...]
    lane = lax.broadcasted_iota(jnp.int32, logit.shape, 1)
    big = jnp.int32(LANES)
    gmask = (lane >= N_EXPERTS) & (lane < N_EXPERTS + N_GROUPS)
    gl = jnp.where(gmask, logit, NEG)
    gmax = jnp.max(gl, axis=-1, keepdims=True)
    gidx = jnp.min(jnp.where(gmask & (gl == gmax), lane, big), axis=-1, keepdims=True) - N_EXPERTS
    pg_sel = 1.0 / jnp.sum(jnp.where(gmask, jnp.exp(gl - gmax), 0.0), axis=-1, keepdims=True)
    emask = (lane < N_EXPERTS) & (lax.shift_right_logical(lane, 2) == gidx)
    el = jnp.where(emask, logit, NEG)
    e1 = jnp.max(el, axis=-1, keepdims=True)
    i1 = jnp.min(jnp.where(emask & (el == e1), lane, big), axis=-1, keepdims=True)
    emask2 = emask & (lane != i1)
    el2 = jnp.where(emask2, logit, NEG)
    e2 = jnp.max(el2, axis=-1, keepdims=True)
    i2 = jnp.min(jnp.where(emask2 & (el2 == e2), lane, big), axis=-1, keepdims=True)
    x21 = jnp.exp(e2 - e1)
    w1 = pg_sel / (1.0 + x21)
    w2 = pg_sel * x21 / (1.0 + x21)
    first_lo = i1 < i2
    e_lo = jnp.where(first_lo, i1, i2) - EXP_PER_GROUP * gidx
    e_hi = jnp.where(first_lo, i2, i1) - EXP_PER_GROUP * gidx
    pair = lax.shift_right_logical(e_lo * (2 * EXP_PER_GROUP - 1 - e_lo), 1) + (e_hi - e_lo - 1)
    bucket = 6 * gidx + pair
    member = lane == bucket
    onehot = jnp.where(member, 1.0, 0.0)
    rr = lax.broadcasted_iota(jnp.int32, (TD, TD), 0)
    cc = lax.broadcasted_iota(jnp.int32, (TD, TD), 1)
    earlier = _dot((rr > cc).astype(BF16), onehot.astype(BF16))
    rank = jnp.sum(jnp.where(member, earlier, 0.0), axis=-1, keepdims=True)
    slab = jnp.where(lane == i1, w1, 0.0) + jnp.where(lane == i2, w2, 0.0)
    slab = jnp.where(lane == RT_BUCKET, bucket.astype(F32), slab)
    slab = jnp.where(lane == RT_RANK, rank, slab)
    slab = jnp.where(lane == RT_WLO, jnp.where(first_lo, w1, w2), slab)
    slab = jnp.where(lane == RT_WHI, jnp.where(first_lo, w2, w1), slab)
    hist8 = jnp.broadcast_to(jnp.sum(onehot, axis=0, keepdims=True), (SUBLANES, LANES))
    kk = lax.broadcasted_iota(jnp.int32, (LANES, LANES), 0)
    ll = lax.broadcasted_iota(jnp.int32, (LANES, LANES), 1)
    runlen = jnp.ceil(hist8 / RUN_CHUNK) * RUN_CHUNK
    before = sum(_dot(part, (kk < ll).astype(BF16)) for part in _split3(runlen))[0:1, :]
    lpos = jnp.sum(jnp.where(member, before, 0.0), axis=-1, keepdims=True) + rank
    slab = jnp.where(lane == RT_LPOS, lpos, slab)
    slab_ref[...] = slab
    hist_ref[...] = hist8


def _merge(ya, yb, mg, x2, wpa, wpb, wout, gffn, wr, br):
    t = x2.shape[0]
    row = lambda w: pl.BlockSpec((TD, w), lambda i: (i, 0))
    return pl.pallas_call(
        _merge_kernel,
        out_shape=[jax.ShapeDtypeStruct((t, D_MODEL), F32),
                   jax.ShapeDtypeStruct((t, D_MODEL), F32),
                   jax.ShapeDtypeStruct((t, LANES), F32),
                   jax.ShapeDtypeStruct((t // TD * SUBLANES, LANES), F32)],
        grid=(t // TD,),
        in_specs=[row(NSA_HEADS * NSA_HD), row(ML_WIDTH), row(2 * D_MODEL), row(D_MODEL)]
                 + [_const_spec(a.shape) for a in (wpa, wpb, wout, gffn, wr, br)],
        out_specs=[row(D_MODEL), row(D_MODEL), row(LANES),
                   pl.BlockSpec((SUBLANES, LANES), lambda i: (i, 0))],
        compiler_params=pltpu.CompilerParams(dimension_semantics=("arbitrary",), vmem_limit_bytes=VMEM_LIMIT),
        name="merge",
    )(ya, yb, mg, x2, wpa, wpb, wout, gffn, wr, br)


def _rec_copy(src_ref, src_tok, dst_ref, dst_tok, sem, rows):
    src = src_ref.at[pl.ds(pl.multiple_of(src_tok * rows, rows), rows), :]
    dst = dst_ref.at[pl.ds(pl.multiple_of(dst_tok * rows, rows), rows), :]
    return pltpu.make_async_copy(src, dst, sem)


def _token_copies(n, make, wait=False):
    def body(g, carry):
        for u in range(DMA_UNROLL):
            cp = make(g * DMA_UNROLL + u)
            if wait:
                cp.wait()
            else:
                cp.start(priority=u % 2)
        return carry
    lax.fori_loop(0, n // DMA_UNROLL, body, 0)


def _dispatch_kernel(ch_ref, tail_ref, h2_ref, slab_ref, hx_ref, sx_ref, h_stage, s_stage, h_zero, s_zero, sem, zsem):
    i = pl.program_id(0)
    n_steps = pl.num_programs(0)
    slot = lax.rem(i, 2)
    n_tiles = hx_ref.shape[0] // TM
    n_used = tail_ref[2 * N_BUCKETS]

    def zero_copies(first_slot, n):
        rows = pl.ds(pl.multiple_of(first_slot, n), n)
        return (pltpu.make_async_copy(h_zero.at[0:n, :], hx_ref.at[rows, :], zsem),
                pltpu.make_async_copy(s_zero.at[0:n, :], sx_ref.at[rows, :], zsem))

    def zero_fill(wait):
        def chunk(b, q, carry):
            for cp in zero_copies(tail_ref[b] + q * RUN_CHUNK, RUN_CHUNK):
                cp.wait() if wait else cp.start()
            return carry

        def tile(k, carry):
            for cp in zero_copies(k * TM, TM):
                cp.wait() if wait else cp.start()
            return carry
        for b in range(N_BUCKETS):
            lax.fori_loop(0, tail_ref[N_BUCKETS + b], functools.partial(chunk, b), 0)
        lax.fori_loop(n_used, n_tiles, tile, 0)

    @pl.when(i == 0)
    def _():
        h_zero[...] = jnp.zeros_like(h_zero)
        s_zero[...] = jnp.zeros_like(s_zero)
        zero_fill(wait=False)

    slab = slab_ref[...]
    perm = (slab[:, RT_LPOS:RT_LPOS + 1]
            == lax.broadcasted_iota(jnp.int32, (1, SORT_ROWS), 1).astype(F32)).astype(BF16)
    h_stage[slot] = lax.dot_general(perm, h2_ref[...].astype(BF16), _TN, preferred_element_type=F32)
    s_stage[slot] = sum(lax.dot_general(perm, part, _TN, preferred_element_type=F32) for part in _split3(slab))

    def chunk_copies(tile, which, wait):
        src_at = n_steps + tile * MAX_CHUNKS
        dst_at = n_steps + (n_steps + tile) * MAX_CHUNKS

        def body(q, carry):
            src = pl.ds(pl.multiple_of(ch_ref[src_at + q], RUN_CHUNK), RUN_CHUNK)
            dst = pl.ds(pl.multiple_of(ch_ref[dst_at + q], RUN_CHUNK), RUN_CHUNK)
            for cp in (pltpu.make_async_copy(h_stage.at[which, src, :], hx_ref.at[dst, :], sem.at[which]),
                       pltpu.make_async_copy(s_stage.at[which, src, :], sx_ref.at[dst, :], sem.at[which])):
                cp.wait() if wait else cp.start()
            return carry
        lax.fori_loop(0, ch_ref[tile], body, 0)

    chunk_copies(i, slot, wait=False)

    @pl.when(i > 0)
    def _():
        chunk_copies(i - 1, 1 - slot, wait=True)

    @pl.when(i == n_steps - 1)
    def _():
        chunk_copies(i, slot, wait=True)
        zero_fill(wait=True)


def _dispatch(chunks, tail, h2, slab, n_slots):
    t = h2.shape[0]
    return pl.pallas_call(
        _dispatch_kernel,
        out_shape=(jax.ShapeDtypeStruct((n_slots, D_MODEL), F32), jax.ShapeDtypeStruct((n_slots, LANES), F32)),
        grid_spec=pltpu.PrefetchScalarGridSpec(
            num_scalar_prefetch=2,
            grid=(t // TD,),
            in_specs=[pl.BlockSpec((TD, D_MODEL), lambda i, ch_r, tail_r: (i, 0)),
                      pl.BlockSpec((TD, LANES), lambda i, ch_r, tail_r: (i, 0))],
            out_specs=(pl.BlockSpec(memory_space=pl.ANY), pl.BlockSpec(memory_space=pl.ANY)),
            scratch_shapes=[pltpu.VMEM((2, SORT_ROWS, D_MODEL), F32), pltpu.VMEM((2, SORT_ROWS, LANES), F32),
                            pltpu.VMEM((TM, D_MODEL), F32), pltpu.VMEM((TM, LANES), F32),
                            pltpu.SemaphoreType.DMA((2,)), pltpu.SemaphoreType.DMA(())],
        ),
        compiler_params=pltpu.CompilerParams(dimension_semantics=("arbitrary",), vmem_limit_bytes=VMEM_LIMIT,
                                             has_side_effects=True),
        name="dispatch",
    )(chunks, tail, h2, slab)


MOE_TILES = 2


def _moe_kernel(te_ref, nu_ref, hx_ref, sx_ref, w13_ref, w2_ref, y_ref):
    step = pl.program_id(0)
    n_tiles = pl.num_programs(0) * MOE_TILES
    n_used = nu_ref[0]

    @pl.when(step * MOE_TILES < n_used)
    def _():
        subs = range(MOE_TILES)
        hs = [hx_ref[sub * TM:(sub + 1) * TM, :].astype(BF16) for sub in subs]
        slabs = [sx_ref[sub * TM:(sub + 1) * TM, :] for sub in subs]
        ys = [None] * MOE_TILES
        for side, lane in ((0, RT_WLO), (1, RT_WHI)):
            es = [te_ref[side * n_tiles + step * MOE_TILES + sub] for sub in subs]
            up = [_dot(hs[sub], w13_ref[es[sub]]) for sub in subs]
            act = [(up[sub][:, 0:D_EXPERT] * _sigmoid(up[sub][:, 0:D_EXPERT]) * up[sub][:, D_EXPERT:2 * D_EXPERT]
                    * slabs[sub][:, lane:lane + 1]).astype(BF16) for sub in subs]
            for sub in subs:
                part = _dot(act[sub], w2_ref[es[sub]])
                ys[sub] = part if ys[sub] is None else ys[sub] + part
        for sub in subs:
            for j in range(X_ROWS):
                y_ref[pl.ds(sub * TM * X_ROWS + j, TM, stride=X_ROWS), :] = ys[sub][:, j * LANES:(j + 1) * LANES]

    @pl.when(step * MOE_TILES >= n_used)
    def _():
        y_ref[...] = jnp.zeros_like(y_ref)


def _moe(tile_e, n_used, hx_sorted, slab_sorted, w13, w2):
    n_tiles = hx_sorted.shape[0] // TM
    rows = MOE_TILES * TM
    last = lambda nu: (nu[0] - 1) // MOE_TILES
    return pl.pallas_call(
        _moe_kernel,
        out_shape=jax.ShapeDtypeStruct((n_tiles * TM * X_ROWS, LANES), F32),
        grid_spec=pltpu.PrefetchScalarGridSpec(
            num_scalar_prefetch=2,
            grid=(n_tiles // MOE_TILES,),
            in_specs=[pl.BlockSpec((rows, D_MODEL), lambda k, te, nu: (jnp.minimum(k, last(nu)), 0)),
                      pl.BlockSpec((rows, LANES), lambda k, te, nu: (jnp.minimum(k, last(nu)), 0)),
                      pl.BlockSpec(w13.shape, lambda k, te, nu: (0, 0, 0), pipeline_mode=pl.Buffered(1)),
                      pl.BlockSpec(w2.shape, lambda k, te, nu: (0, 0, 0), pipeline_mode=pl.Buffered(1))],
            out_specs=pl.BlockSpec((rows * X_ROWS, LANES), lambda k, te, nu: (k, 0)),
        ),
        compiler_params=pltpu.CompilerParams(dimension_semantics=("arbitrary",), vmem_limit_bytes=VMEM_LIMIT),
        name="moe",
    )(tile_e, n_used, hx_sorted, slab_sorted, w13, w2)


def _combine_kernel(pos_ref, y_ref, x1_ref, p_ref, gple_ref, wpg_ref, wpp_ref, gfin_ref, o_ref, ybuf, sem):
    i = pl.program_id(0)
    slot = lax.rem(i, 2)

    last = pl.num_programs(0) - 1

    def drain(which):
        _token_copies(TD, lambda r: _rec_copy(y_ref, 0, ybuf.at[which], 0, sem.at[which], X_ROWS), wait=True)

    @pl.when(i == 0)
    def _():
        _token_copies(TD, lambda r: _rec_copy(y_ref, pos_ref[r], ybuf.at[0], r, sem.at[0], X_ROWS))

    drain(slot)
    nxt = jnp.minimum(i + 1, last) * TD
    yb = ybuf.at[slot]
    part = TD // COMBINE_PARTS
    for q in range(COMBINE_PARTS):
        rows = slice(q * part, (q + 1) * part)
        y = jnp.concatenate([yb[pl.ds(q * part * X_ROWS + j, part, stride=X_ROWS), :] for j in range(X_ROWS)],
                            axis=1)
        x2 = x1_ref[rows, :] + y
        h3 = _rms(x2, gple_ref[...]).astype(BF16)
        x3 = x2 + _sigmoid(_dot(h3, wpg_ref[...])) * _dot(p_ref[rows, :].astype(BF16), wpp_ref[...])
        o_ref[rows, :] = _rms(x3, gfin_ref[...])
        for r in range(q * part, (q + 1) * part):
            _rec_copy(y_ref, pos_ref[nxt + r], ybuf.at[1 - slot], r, sem.at[1 - slot], X_ROWS).start(
                priority=r % 2)

    @pl.when(i == last)
    def _():
        drain(1 - slot)


def _combine(pos, y_sorted, x1, p2, gple, wpg, wpp, gfin):
    t = x1.shape[0]
    row = lambda w: pl.BlockSpec((TD, w), lambda i, pos_r: (i, 0))
    const = lambda a: pl.BlockSpec(a.shape, lambda i, pos_r: (0,) * a.ndim, pipeline_mode=pl.Buffered(1))
    return pl.pallas_call(
        _combine_kernel,
        out_shape=jax.ShapeDtypeStruct((t, D_MODEL), F32),
        grid_spec=pltpu.PrefetchScalarGridSpec(
            num_scalar_prefetch=1,
            grid=(t // TD,),
            in_specs=[pl.BlockSpec(memory_space=pl.ANY), row(D_MODEL), row(PLE_DIM),
                      const(gple), const(wpg), const(wpp), const(gfin)],
            out_specs=row(D_MODEL),
            scratch_shapes=[pltpu.VMEM((2, TD * X_ROWS, LANES), F32), pltpu.SemaphoreType.DMA((2,))],
        ),
        compiler_params=pltpu.CompilerParams(dimension_semantics=("arbitrary",), vmem_limit_bytes=VMEM_LIMIT),
        name="combine",
    )(pos, y_sorted, x1, p2, gple, wpg, wpp, gfin)


def _routing_tables(slab, hist8):
    t = slab.shape[0]
    nt = t // TD
    n_tiles = t // TM + N_BUCKETS + -(-nt * N_BUCKETS * (RUN_CHUNK - 1) // TM)
    n_tiles += n_tiles % MOE_TILES
    hist = hist8.reshape(nt, SUBLANES, LANES)[:, 0, :]
    runlen = jnp.ceil(hist / RUN_CHUNK) * RUN_CHUNK
    counts = jnp.sum(runlen, axis=0)
    padded = jnp.ceil(counts / TM) * TM
    ends = jnp.cumsum(padded)
    first = (ends - padded)[None, :] + jnp.cumsum(runlen, axis=0) - runlen
    lane = jnp.arange(N_BUCKETS, dtype=F32)[None, :]
    mine = lane == slab[:, RT_BUCKET:RT_BUCKET + 1]
    pos = jnp.sum(jnp.where(mine, jnp.repeat(first[:, :N_BUCKETS], TD, axis=0), 0.0), axis=1) + slab[:, RT_RANK]
    starts = jnp.arange(n_tiles, dtype=F32) * TM
    tile_bucket = jnp.minimum(jnp.sum(ends[None, :N_BUCKETS] <= starts[:, None], axis=1), N_BUCKETS - 1)
    group, pair = tile_bucket // 6, tile_bucket % 6
    e_lo = EXP_PER_GROUP * group + jnp.array([0, 0, 0, 1, 1, 2], jnp.int32)[pair]
    e_hi = EXP_PER_GROUP * group + jnp.array([1, 2, 3, 2, 3, 3], jnp.int32)[pair]
    tile_e = jnp.concatenate([e_lo, e_hi]).astype(jnp.int32)
    n_used = (ends[N_BUCKETS - 1] / TM).astype(jnp.int32).reshape(1)
    tail = jnp.concatenate([(ends - padded + counts)[:N_BUCKETS], ((padded - counts) / RUN_CHUNK)[:N_BUCKETS],
                            n_used.astype(F32)]).astype(jnp.int32)
    nb = N_BUCKETS
    nch = runlen[:, :nb] / RUN_CHUNK
    cum = jnp.cumsum(nch, axis=1)
    q = jnp.arange(MAX_CHUNKS, dtype=F32)[None, :, None]
    of_b = jnp.arange(nb)[None, None, :] == jnp.minimum(jnp.sum(cum[:, None, :] <= q, axis=2), nb - 1)[..., None]
    pick = lambda a: jnp.sum(jnp.where(of_b, a[:, None, :nb], 0.0), axis=2)
    within = RUN_CHUNK * (q[..., 0] - pick(cum - nch))
    local = jnp.cumsum(runlen, axis=1) - runlen
    chunks = jnp.concatenate([cum[:, nb - 1], (pick(local) + within).reshape(-1),
                              (pick(first) + within).reshape(-1)]).astype(jnp.int32)
    return pos.astype(jnp.int32), chunks, tile_e, n_used, tail, n_tiles * TM


def _pack_inproj_weights(w):
    d = w.shape[0]
    qw = NSA_HEADS * NSA_HD
    kvw = NSA_KV * NSA_HD
    o = 0
    wq = w[:, o:o + qw]; o += qw
    wkc = w[:, o:o + kvw]; o += kvw
    wvc = w[:, o:o + kvw]; o += kvw
    wks = w[:, o:o + kvw]; o += kvw
    wvs = w[:, o:o + kvw]; o += kvw
    wkw = w[:, o:o + kvw]; o += kvw
    wvw = w[:, o:o + kvw]; o += kvw
    wga = w[:, o:o + 3 * NSA_HEADS]; o += 3 * NSA_HEADS
    wqkb = w[:, o:o + 2 * ML_WIDTH]; o += 2 * ML_WIDTH
    wvb = w[:, o:o + ML_WIDTH]; o += ML_WIDTH
    wob = w[:, o:o + ML_WIDTH]; o += ML_WIDTH
    wif = w[:, o:o + 2 * ML_HEADS]; o += 2 * ML_HEADS
    wmg = w[:, o:o + 2 * D_MODEL]
    wsm = jnp.concatenate([wga, wif, jnp.zeros((d, LANES - 3 * NSA_HEADS - 2 * ML_HEADS), w.dtype)], axis=1)
    wcat = jnp.concatenate([wq, wkc, wvc, wks, wkw, wsm, wqkb, wvb, wob, wmg], axis=1).astype(BF16)
    wtr = jnp.concatenate([wvs, wvw, wsm], axis=1).T.astype(BF16)
    return wcat, wtr


def _rope_tables(positions):
    half = ROPE_DIM // 2
    inv = ROPE_THETA ** (-jnp.arange(0, ROPE_DIM, 2, dtype=F32) / ROPE_DIM)
    ang = positions.astype(F32).reshape(-1, 1) * inv[None, :]
    cs = jnp.concatenate([jnp.cos(ang), jnp.sin(ang)], axis=1)
    d = np.arange(LANES) % NSA_HD
    spread = np.zeros((2 * half, 3 * LANES), np.float32)
    lanes = np.arange(LANES)
    rot = d < ROPE_DIM
    spread[(d % half)[rot], lanes[rot]] = 1.0
    hi = (d >= half) & rot
    spread[half + (d % half)[hi], LANES + lanes[hi]] = 1.0
    lo = d < half
    spread[half + (d % half)[lo], 2 * LANES + lanes[lo]] = -1.0
    unrotated = (~rot).astype(np.float32).reshape(1, LANES)
    return cs, jnp.asarray(spread, BF16), jnp.asarray(unrotated)


def _pack_compress_weights(w1, w2, pe):
    half = CMP_LEN // 2
    w1r = w1.reshape(2, half, NSA_HD, CMP_HIDDEN)
    outs = []
    for part in range(2):
        wb = w1r[part].astype(BF16)
        zb = jnp.zeros_like(wb)
        wp = jnp.stack([jnp.stack([wb, zb], axis=2), jnp.stack([zb, wb], axis=2)], axis=1)
        outs.append(wp.reshape(half * NSA_KV * NSA_HD, NSA_KV * CMP_HIDDEN))
    pe8 = jnp.broadcast_to(pe.reshape(1, CMP_LEN * NSA_HD), (SUBLANES, CMP_LEN * NSA_HD)).astype(BF16)
    return outs[0], outs[1], pe8, w1.astype(BF16)


def _stages(x, p, positions, g_mix, w_in, b_if, w_ck1, w_ck2, pe_ck, w_cv1, w_cv2, pe_cv, w_conv, b_conv, g_hn, w_pa, w_pb, w_out, g_ffn, w_rg, b_rg, w_re, b_re, w_e13, w_e2, g_ple, w_pg, w_pp, g_final):
    b, s, d = x.shape
    t = b * s
    cs, spread, unrot = _rope_tables(positions)
    assert w_in.shape[0] == 1, "the final norm is fused into the layer's last kernel: single-layer problem only"
    for i in range(w_in.shape[0]):
        x2 = x.reshape(t, d)
        wcat, wtr = _pack_inproj_weights(w_in[i])
        (qpad, kc_tok, vc_tok, ks, kw, vst, vwt, sm, smt, qkb, vb, ob, mg) = _inproj(
            x2, g_mix[i].reshape(1, d), wcat, wtr, cs, spread, unrot)
        wka, wkb, pek, w1k = _pack_compress_weights(w_ck1[i], w_ck2[i], pe_ck[i])
        wva, wvb, pev, w1v = _pack_compress_weights(w_cv1[i], w_cv2[i], pe_cv[i])
        zpad = jnp.zeros((CMP_HIDDEN, NSA_HD), F32)
        w2k = jnp.stack([jnp.concatenate([w_ck2[i], zpad], axis=1),
                         jnp.concatenate([zpad, w_ck2[i]], axis=1)]).astype(BF16)
        w2vt = w_cv2[i].T.astype(BF16)
        nrow = s // CMP_STRIDE
        rk = kc_tok.reshape(b, nrow, CMP_STRIDE * LANES)
        rv = vc_tok.reshape(b, nrow, CMP_STRIDE * LANES)
        kcb, vct = _compress(rk, rv, wka, wkb, wva, wvb, pek, pev, w1k, w1v, w2k, w2vt)
        ya = _nsa(qpad, kcb, vct, ks, kw, vst, vwt, smt, b, s)
        bif = b_if[i].astype(F32)
        bifc = jnp.zeros((1, LANES), F32).at[0, SM_I:SM_I + 2 * ML_HEADS].set(bif)
        bifr = bifc.reshape(LANES, 1)
        yb = _mlstm(qkb, vb, ob, sm, smt, w_conv[i], b_conv[i].reshape(1, -1), bifc, bifr,
                    g_hn[i].reshape(1, -1), b, s)
        wr = jnp.concatenate([w_re[i], w_rg[i], jnp.zeros((d, LANES - N_EXPERTS - N_GROUPS), F32)], axis=1)
        wr_hi = wr.astype(BF16)
        wr = jnp.concatenate([wr_hi, (wr - wr_hi.astype(F32)).astype(BF16)], axis=1)
        br =jnp.concatenate([b_re[i], b_rg[i], jnp.zeros((LANES - N_EXPERTS - N_GROUPS,), F32)]).reshape(1, LANES)
        x1, h2, slab, hist8 = _merge(ya, yb, mg, x2, w_pa[i].astype(BF16), w_pb[i].astype(BF16),
                                     w_out[i].astype(BF16), g_ffn[i].reshape(1, d), wr, br)
        pos, chunks, tile_e, n_used, tail, n_slots = _routing_tables(slab, hist8)
        hx_sorted, slab_sorted = _dispatch(chunks, tail, h2, slab, n_slots)
        y_sorted = _moe(tile_e, n_used, hx_sorted, slab_sorted, w_e13[i].astype(BF16), w_e2[i].astype(BF16))
        out = _combine(pos, y_sorted, x1, p[i].reshape(t, PLE_DIM), g_ple[i].reshape(1, d), w_pg[i].astype(BF16),
                       w_pp[i].astype(BF16), g_final.reshape(1, d))
        x = out.reshape(b, s, d)
    return dict(out=x, qpad=qpad, ks=ks, kcb=kcb, vct=vct, y_a=ya, y_b=yb, x1=x1, pos=pos)


def kernel(x, p, positions, g_mix, w_in, b_if, w_ck1, w_ck2, pe_ck, w_cv1, w_cv2, pe_cv, w_conv, b_conv, g_hn, w_pa, w_pb, w_out, g_ffn, w_rg, b_rg, w_re, b_re, w_e13, w_e2, g_ple, w_pg, w_pp, g_final):
    return _stages(x, p, positions, g_mix, w_in, b_if, w_ck1, w_ck2, pe_ck, w_cv1, w_cv2, pe_cv, w_conv, b_conv, g_hn,
                   w_pa, w_pb, w_out, g_ffn, w_rg, b_rg, w_re, b_re, w_e13, w_e2, g_ple, w_pg, w_pp, g_final)["out"]
```

```python
import functools
import math

import numpy as np
import jax
import jax.numpy as jnp
from jax import lax
from jax.experimental import pallas as pl
from jax.experimental.pallas import tpu as pltpu

F32 = jnp.float32
BF16 = jnp.bfloat16

EPS = 1e-6
NEG = -1e30

D_MODEL = 1024
PLE_DIM = 256
NSA_HEADS = 8
NSA_KV = 2
NSA_HPG = NSA_HEADS // NSA_KV
NSA_HD = 64
CMP_LEN = 32
CMP_STRIDE = 16
CMP_HIDDEN = 256
SEL_BLOCK = 64
SEL_TOPK = 16
SEL_FORCE = 1000.0
WINDOW = 512
ROPE_THETA = 500000.0
ROPE_DIM = NSA_HD // 4
ML_HEADS = 4
ML_HD = 128
ML_WIDTH = ML_HEADS * ML_HD
CONV_W = 4
N_GROUPS = 4
EXP_PER_GROUP = 4
N_EXPERTS = N_GROUPS * EXP_PER_GROUP
D_EXPERT = 256

LANES = 128
SUBLANES = 8
QT = 128
KC = 128
SEL_GROUP = 512
VT_PAD = 16
VT_ROWS = NSA_HD + VT_PAD
ML_CHUNK = 128
ML_BLOCK = 256
TD = 512
TM = 256
VMEM_LIMIT = 56 * 1024 * 1024

_NT = (((1,), (1,)), ((), ()))
_TN = (((0,), (0,)), ((), ()))

SM_GATE = 0
SM_I = 3 * NSA_HEADS
SM_F = SM_I + ML_HEADS


def _dot(a, b):
    return jnp.dot(a, b, preferred_element_type=F32)


def _dot_nt(a, b):
    return lax.dot_general(a, b, _NT, preferred_element_type=F32)


def _split3(x):
    hi = x.astype(BF16)
    r1 = x - hi.astype(F32)
    mid = r1.astype(BF16)
    lo = (r1 - mid.astype(F32)).astype(BF16)
    return hi, mid, lo


def _rms(x, g):
    return x * lax.rsqrt(jnp.mean(x * x, axis=-1, keepdims=True) + EPS) * g


def _sigmoid(x):
    return 0.5 + 0.5 * jnp.tanh(0.5 * x)


def _const_spec(shape):
    nd = len(shape)
    return pl.BlockSpec(shape, lambda *_: (0,) * nd, pipeline_mode=pl.Buffered(1))


_C_Q = 0
_C_KC = _C_Q + NSA_HEADS * NSA_HD
_C_VC = _C_KC + LANES
_C_KS = _C_VC + LANES
_C_KW = _C_KS + LANES
_C_SM = _C_KW + LANES
_C_QKB = _C_SM + LANES
_C_VB = _C_QKB + 2 * ML_WIDTH
_C_OB = _C_VB + ML_WIDTH
_C_MG = _C_OB + ML_WIDTH
_C_END = _C_MG + 2 * D_MODEL


def _inproj_kernel(x_ref, g_ref, w_ref, wt_ref, cs_ref, spread_ref, unrot_ref,
                   q_ref, kc_ref, vc_ref, ks_ref, kw_ref, vst_ref, vwt_ref, sm_ref, smt_ref,
                   qkb_ref, vb_ref, ob_ref, mg_ref):
    hn = _rms(x_ref[...], g_ref[...]).astype(BF16)
    tables = sum(_dot(part, spread_ref[...]) for part in _split3(cs_ref[...]))
    rc = tables[:, 0:LANES] + unrot_ref[...]
    rp = tables[:, LANES:2 * LANES]
    rm = tables[:, 2 * LANES:3 * LANES]

    def rope(z):
        half = ROPE_DIM // 2
        return z * rc + pltpu.roll(z, half, 1) * rp + pltpu.roll(z, LANES - half, 1) * rm

    scale = NSA_HD ** -0.5 * math.log2(math.e)
    for h in range(NSA_HEADS * NSA_HD // LANES):
        z = _dot(hn, w_ref[:, _C_Q + h * LANES:_C_Q + (h + 1) * LANES])
        q_ref[:, h * LANES:(h + 1) * LANES] = (rope(z) * scale).astype(BF16)
    kc_ref[...] = rope(_dot(hn, w_ref[:, _C_KC:_C_KC + LANES])).astype(BF16)
    vc_ref[...] = _dot(hn, w_ref[:, _C_VC:_C_VC + LANES]).astype(BF16)
    ks_ref[...] = rope(_dot(hn, w_ref[:, _C_KS:_C_KS + LANES])).astype(BF16)
    kw_ref[...] = rope(_dot(hn, w_ref[:, _C_KW:_C_KW + LANES])).astype(BF16)
    sm_ref[...] = _dot(hn, w_ref[:, _C_SM:_C_SM + LANES])
    for c0 in range(0, 2 * ML_WIDTH, 512):
        qkb_ref[:, c0:c0 + 512] = _dot(hn, w_ref[:, _C_QKB + c0:_C_QKB + c0 + 512]).astype(BF16)
    vb_ref[...] = _dot(hn, w_ref[:, _C_VB:_C_VB + ML_WIDTH]).astype(BF16)
    ob_ref[...] = _dot(hn, w_ref[:, _C_OB:_C_OB + ML_WIDTH]).astype(BF16)
    for c0 in range(0, 2 * D_MODEL, 512):
        mg_ref[:, c0:c0 + 512] = _dot(hn, w_ref[:, _C_MG + c0:_C_MG + c0 + 512]).astype(BF16)
    zt = _dot_nt(wt_ref[...], hn)
    ones_rows = (lax.broadcasted_iota(jnp.int32, (VT_PAD, KC), 0) == 0).astype(BF16)
    for i in range(TD // KC):
        for ref, r0 in ((vst_ref, 0), (vwt_ref, LANES)):
            zc = zt[r0:r0 + LANES, i * KC:(i + 1) * KC].astype(BF16)
            ref[i] = jnp.concatenate([piece for g in range(NSA_KV)
                                      for piece in (zc[g * NSA_HD:(g + 1) * NSA_HD, :], ones_rows)], axis=0)
    smt_ref[...] = zt[2 * LANES:3 * LANES, :]


def _inproj(x2, g_mix, wcat, wtr, cs, spread, unrot):
    t = x2.shape[0]
    row = lambda w: pl.BlockSpec((TD, w), lambda i: (i, 0))
    out_shape = [
        jax.ShapeDtypeStruct((t, NSA_HEADS * NSA_HD), BF16),
        jax.ShapeDtypeStruct((t, LANES), BF16),
        jax.ShapeDtypeStruct((t, LANES), BF16),
        jax.ShapeDtypeStruct((t, LANES), BF16),
        jax.ShapeDtypeStruct((t, LANES), BF16),
        jax.ShapeDtypeStruct((t // KC, NSA_KV * VT_ROWS, KC), BF16),
        jax.ShapeDtypeStruct((t // KC, NSA_KV * VT_ROWS, KC), BF16),
        jax.ShapeDtypeStruct((t, LANES), F32),
        jax.ShapeDtypeStruct((LANES, t), F32),
        jax.ShapeDtypeStruct((t, 2 * ML_WIDTH), BF16),
        jax.ShapeDtypeStruct((t, ML_WIDTH), BF16),
        jax.ShapeDtypeStruct((t, ML_WIDTH), BF16),
        jax.ShapeDtypeStruct((t, 2 * D_MODEL), BF16),
    ]
    chunk3 = pl.BlockSpec((TD // KC, NSA_KV * VT_ROWS, KC), lambda i: (i, 0, 0))
    out_specs = [row(NSA_HEADS * NSA_HD), row(LANES), row(LANES), row(LANES), row(LANES), chunk3, chunk3,
                 row(LANES), pl.BlockSpec((LANES, TD), lambda i: (0, i)),
                 row(2 * ML_WIDTH), row(ML_WIDTH), row(ML_WIDTH), row(2 * D_MODEL)]
    return pl.pallas_call(
        _inproj_kernel,
        out_shape=out_shape,
        grid=(t // TD,),
        in_specs=[row(D_MODEL), _const_spec((1, D_MODEL)), _const_spec((D_MODEL, _C_END)),
                  _const_spec((3 * LANES, D_MODEL)), row(cs.shape[1]), _const_spec(spread.shape),
                  _const_spec(unrot.shape)],
        out_specs=out_specs,
        compiler_params=pltpu.CompilerParams(dimension_semantics=("arbitrary",), vmem_limit_bytes=VMEM_LIMIT),
        name="inproj",
    )(x2, g_mix, wcat, wtr, cs, spread, unrot)


def _gelu_tanh(x):
    return 0.5 * x * (1.0 + jnp.tanh(math.sqrt(2.0 / math.pi) * (x + 0.044715 * x * x * x)))


def _compress_kernel(rk_ref, rv_ref, wka_ref, wkb_ref, wva_ref, wvb_ref, pek_ref, pev_ref,
                     w1k_ref, w1v_ref, w2k_ref, w2vt_ref, kc_ref, vct_ref):
    nrow = rk_ref.shape[0]

    def hidden(r_ref, wa_ref, wb_ref, pe_ref, w1_ref):
        r = r_ref[...]
        ha = _dot(r, wa_ref[...])
        hb = _dot(r, wb_ref[...])
        hb = pltpu.roll(hb, nrow - 1, 0)
        c = _dot(pe_ref[...], w1_ref[...])[0:1, :]
        return [_gelu_tanh(ha[:, g * CMP_HIDDEN:(g + 1) * CMP_HIDDEN] + hb[:, g * CMP_HIDDEN:(g + 1) * CMP_HIDDEN] + c).astype(BF16)
                for g in range(NSA_KV)]

    ak = hidden(rk_ref, wka_ref, wkb_ref, pek_ref, w1k_ref)
    kc_ref[...] = (_dot(ak[0], w2k_ref[0]) + _dot(ak[1], w2k_ref[1])).astype(BF16)
    av = hidden(rv_ref, wva_ref, wvb_ref, pev_ref, w1v_ref)
    for g in range(NSA_KV):
        vct_ref[g * NSA_HD:(g + 1) * NSA_HD, :] = _dot_nt(w2vt_ref[...], av[g]).astype(BF16)


def _compress(rk, rv, wka, wkb, wva, wvb, pek, pev, w1k, w1v, w2k, w2vt):
    b, nrow, width = rk.shape
    blk = pl.BlockSpec((None, nrow, width), lambda i: (i, 0, 0))
    return pl.pallas_call(
        _compress_kernel,
        out_shape=[jax.ShapeDtypeStruct((b, nrow, LANES), BF16),
                   jax.ShapeDtypeStruct((b, LANES, nrow), BF16)],
        grid=(b,),
        in_specs=[blk, blk] + [_const_spec(a.shape) for a in (wka, wkb, wva, wvb, pek, pev, w1k, w1v, w2k, w2vt)],
        out_specs=[pl.BlockSpec((None, nrow, LANES), lambda i: (i, 0, 0)),
                   pl.BlockSpec((None, LANES, nrow), lambda i: (i, 0, 0))],
        compiler_params=pltpu.CompilerParams(dimension_semantics=("arbitrary",), vmem_limit_bytes=VMEM_LIMIT),
        name="compress",
    )(rk, rv, wka, wkb, wva, wvb, pek, pev, w1k, w1v, w2k, w2vt)


def _nsa_kernel(q_ref, kc_ref, vct_ref, ks_ref, kw_ref, vst_ref, vwt_ref, smt_ref, o_ref, bias_scr, sx_scr, sy_scr):
    c = pl.program_id(1)
    t0 = c * QT
    ncmp = kc_ref.shape[0]
    nsel = bias_scr.shape[0]
    nw = WINDOW // KC + 1
    gw = NSA_HPG * QT
    width = NSA_KV * gw

    def per_group(x):
        return [x[:, g * gw:(g + 1) * gw] for g in range(NSA_KV)]

    def pv(vt, p):
        rows = vt.shape[0] // NSA_KV
        pb = p.astype(BF16)
        return jnp.concatenate([_dot(vt[g * rows:(g + 1) * rows, :], pg) for g, pg in enumerate(per_group(pb))],
                               axis=1)

    def normalised(acc):
        return acc[0:NSA_HD, :] / acc[NSA_HD:NSA_HD + 1, :]

    low_half = lax.broadcasted_iota(jnp.int32, (1, LANES), 1) < NSA_HD
    q_heads = []
    for h in range(NSA_HEADS):
        pair = q_ref[:, (h // 2) * LANES:(h // 2 + 1) * LANES].astype(F32)
        want_low = h // NSA_HPG == 0
        if (h % 2 == 0) != want_low:
            pair = pltpu.roll(pair, NSA_HD, 1)
        q_heads.append(jnp.where(low_half if want_low else ~low_half, pair, 0.0).astype(BF16))
    qs = jnp.concatenate(q_heads, axis=0)
    u_row = lax.broadcasted_iota(jnp.int32, (1, width), 1) % QT
    t_row = t0 + u_row
    r_kc = lax.broadcasted_iota(jnp.int32, (KC, 1), 0)

    n_grp = ks_ref.shape[0] // SEL_GROUP
    n_full = lax.shift_right_logical(t0, int(math.log2(SEL_GROUP)))

    def qk_group(j):
        return _dot_nt(ks_ref[pl.ds(pl.multiple_of(j * SEL_GROUP, SEL_GROUP), SEL_GROUP), :], qs)

    sc = _dot_nt(kc_ref[...], qs)

    w_slabs, w_chunks = [], []
    for i in range(nw):
        jj = c - (nw - 1) + i
        jc = jnp.maximum(jj, 0)
        si = _dot_nt(kw_ref[pl.ds(pl.multiple_of(jc * KC, KC), KC), :], qs)
        if i == 0:
            keep = jnp.where(jj >= 0, r_kc, -1) > u_row
        elif i == nw - 1:
            keep = r_kc <= u_row
        else:
            keep = jj >= 0
        w_slabs.append(jnp.where(keep, si, NEG))
        w_chunks.append(jc)

    n_col = lax.broadcasted_iota(jnp.int32, (ncmp, 1), 0)
    last_tok = jnp.where(n_col < ncmp - 1, CMP_STRIDE * n_col + (CMP_LEN - 1), jnp.iinfo(jnp.int32).max)
    s = jnp.where(last_tok <= t_row, sc, NEG)
    m = jnp.max(s, axis=0, keepdims=True)
    e = jnp.exp2(s - m)
    anyv = (t_row >= CMP_LEN - 1).astype(F32)
    p = e * (anyv / jnp.sum(e, axis=0, keepdims=True))
    o_cmp = pv(vct_ref[...], p)

    psums = []
    for pg in per_group(p):
        acc_p = pg[:, 0:QT]
        for h in range(1, NSA_HPG):
            acc_p = acc_p + pg[:, h * QT:(h + 1) * QT]
        psums.append(acc_p)
    psum = jnp.concatenate(psums, axis=1)
    nq2 = NSA_KV * QT
    s_col = lax.broadcasted_iota(jnp.int32, (nsel, 1), 0)
    n_lane = lax.broadcasted_iota(jnp.int32, (1, ncmp), 1)
    ov = ((CMP_STRIDE * n_lane < SEL_BLOCK * (s_col + 1)) & (CMP_STRIDE * n_lane + (CMP_LEN - 1) >= SEL_BLOCK * s_col)
          ).astype(BF16)
    imp = sum(_dot(ov, part) for part in _split3(psum))

    s_diag = qk_group(n_full)
    s_own = _dot_nt(ks_ref[pl.ds(pl.multiple_of(t0, KC), KC), :], qs)
    sx_scr[...] = qk_group(0)

    mxw = w_slabs[0]
    for sl in w_slabs[1:]:
        mxw = jnp.maximum(mxw, sl)
    mw = jnp.max(mxw, axis=0, keepdims=True)
    acc_w = jnp.zeros((VT_ROWS, width), F32)
    for sl, jc in zip(w_slabs, w_chunks):
        acc_w = acc_w + pv(vwt_ref[jc], jnp.exp2(sl - mw))
    o_win = normalised(acc_w)

    t1 = t0 + lax.broadcasted_iota(jnp.int32, (1, nq2), 1) % QT
    cur = lax.shift_right_logical(t1, 6)
    forced = (s_col == 0) | (s_col == cur) | (s_col == cur - 1)
    valid = SEL_BLOCK * s_col <= t1
    val = jnp.where(valid, jnp.where(forced, imp + SEL_FORCE, imp), NEG)
    sub = 8
    r_sub = lax.broadcasted_iota(jnp.int32, (sub, 1), 0)
    blocks = [val[r * sub:(r + 1) * sub, :] for r in range(nsel // sub)]
    ranks = [jnp.zeros((sub, nq2), F32) for _ in blocks]
    for i in range(nsel):
        vi = val[i:i + 1, :]
        for r, blk in enumerate(blocks):
            if i < r * sub:
                beats = vi >= blk
            elif i >= (r + 1) * sub:
                beats = vi > blk
            else:
                beats = (vi > blk) | ((vi == blk) & (r_sub > i - r * sub))
            ranks[r] = ranks[r] + jnp.where(beats, 1.0, 0.0)
    bias = jnp.where(jnp.concatenate(ranks, axis=0) < SEL_TOPK, 0.0, NEG).astype(F32)
    bias_scr[...] = jnp.concatenate([bias[:, g * QT:(g + 1) * QT] for g in range(NSA_KV) for _ in range(NSA_HPG)],
                                    axis=1)

    blk_per_grp = SEL_GROUP // SEL_BLOCK
    chunk_per_grp = SEL_GROUP // KC
    blk_per_chunk = KC // SEL_BLOCK

    def sel_update(j, sj, carry):
        m_o, acc = carry
        brows = [bias_scr[pl.ds(blk_per_grp * j + i, 1), :] for i in range(blk_per_grp)]

        def block(i):
            return sj[i * SEL_BLOCK:(i + 1) * SEL_BLOCK, :]

        mx = None
        for i in range(blk_per_grp):
            sl = block(i) + brows[i]
            mx = sl if mx is None else jnp.maximum(mx, sl)
        m_n = jnp.maximum(m_o, jnp.max(mx, axis=0, keepdims=True))
        a = jnp.exp2(m_o - m_n)
        acc = a * acc
        for ci in range(chunk_per_grp):
            parts = [jnp.exp2(block(i) + (brows[i] - m_n))
                     for i in range(blk_per_chunk * ci, blk_per_chunk * (ci + 1))]
            acc = acc + pv(vst_ref[chunk_per_grp * j + ci], jnp.concatenate(parts, axis=0))
        return m_n, acc

    def seed_state():
        first_own = blk_per_chunk * (c % chunk_per_grp)
        brows = [bias_scr[pl.ds(blk_per_grp * n_full + i, 1), :] + jnp.where(i < first_own, 0.0, NEG)
                 for i in range(blk_per_grp)]
        own = jnp.where(r_kc <= u_row, s_own, NEG)
        mx = jnp.maximum(own[0:SEL_BLOCK, :], own[SEL_BLOCK:KC, :])
        for i in range(blk_per_grp):
            mx = jnp.maximum(mx, s_diag[i * SEL_BLOCK:(i + 1) * SEL_BLOCK, :] + brows[i])
        m_n = jnp.max(mx, axis=0, keepdims=True)
        acc = pv(vst_ref[c], jnp.exp2(own - m_n))
        for ci in range(chunk_per_grp):
            parts = [jnp.exp2(s_diag[i * SEL_BLOCK:(i + 1) * SEL_BLOCK, :] + (brows[i] - m_n))
                     for i in range(blk_per_chunk * ci, blk_per_chunk * (ci + 1))]
            acc = acc + pv(vst_ref[chunk_per_grp * n_full + ci], jnp.concatenate(parts, axis=0))
        return m_n, acc

    seeded = seed_state()

    def pair_body(jp, carry):
        ja, jb = 2 * jp, 2 * jp + 1
        sy_scr[...] = qk_group(jb)
        carry = sel_update(ja, sx_scr, carry)
        sx_scr[...] = qk_group(jnp.minimum(ja + 2, n_grp - 1))
        return sel_update(jb, sy_scr, carry)

    n_pairs = lax.shift_right_logical(n_full, 1)
    carry = lax.fori_loop(0, n_pairs, pair_body, seeded)
    _, acc_s = lax.cond(n_full - 2 * n_pairs == 1, lambda cr: sel_update(n_full - 1, sx_scr, cr), lambda cr: cr,
                        carry)
    o_sel = normalised(acc_s)

    def gate_row(br):
        rows = [smt_ref[SM_GATE + 3 * h + br:SM_GATE + 3 * h + br + 1, :] for h in range(NSA_HEADS)]
        return _sigmoid(jnp.concatenate(rows, axis=1))

    o_t = gate_row(0) * o_cmp + gate_row(1) * o_sel + gate_row(2) * o_win
    for pr in range(NSA_HEADS // 2):
        xp = jnp.concatenate([o_t[:, (2 * pr) * QT:(2 * pr + 1) * QT], o_t[:, (2 * pr + 1) * QT:(2 * pr + 2) * QT]], axis=0)
        o_ref[:, pr * LANES:(pr + 1) * LANES] = xp.T.astype(BF16)


def _nsa(qpad, kcb, vct, ks, kw, vst, vwt, smt, b, s):
    nq = s // QT
    ncmp = kcb.shape[1]
    return pl.pallas_call(
        _nsa_kernel,
        out_shape=jax.ShapeDtypeStruct((b * s, NSA_HEADS * NSA_HD), BF16),
        grid=(b, nq),
        in_specs=[
            pl.BlockSpec((QT, NSA_HEADS * NSA_HD), lambda bi, c: (bi * nq + c, 0)),
            pl.BlockSpec((None, ncmp, LANES), lambda bi, c: (bi, 0, 0)),
            pl.BlockSpec((None, NSA_KV * NSA_HD, ncmp), lambda bi, c: (bi, 0, 0)),
            pl.BlockSpec((s, LANES), lambda bi, c: (bi, 0)),
            pl.BlockSpec((s, LANES), lambda bi, c: (bi, 0)),
            pl.BlockSpec((s // KC, NSA_KV * VT_ROWS, KC), lambda bi, c: (bi, 0, 0)),
            pl.BlockSpec((s // KC, NSA_KV * VT_ROWS, KC), lambda bi, c: (bi, 0, 0)),
            pl.BlockSpec((LANES, QT), lambda bi, c: (0, bi * nq + c)),
        ],
        out_specs=pl.BlockSpec((QT, NSA_HEADS * NSA_HD), lambda bi, c: (bi * nq + c, 0)),
        scratch_shapes=[pltpu.VMEM((s // SEL_BLOCK, NSA_HEADS * QT), F32),
                        pltpu.VMEM((SEL_GROUP, NSA_HEADS * QT), F32),
                        pltpu.VMEM((SEL_GROUP, NSA_HEADS * QT), F32)],
        compiler_params=pltpu.CompilerParams(dimension_semantics=("arbitrary", "arbitrary"),
                                             vmem_limit_bytes=VMEM_LIMIT),
        name="nsa",
    )(qpad, kcb, vct, ks, kw, vst, vwt, smt)


def _log_sigmoid(x):
    return jnp.minimum(x, 0.0) - jnp.log(1.0 + jnp.exp(-jnp.abs(x)))


def _mlstm_kernel(qk_ref, v_ref, og_ref, sm_ref, smt_ref, wc_ref, bc_ref, bifc_ref, bifr_ref, ghn_ref,
                  y_ref, tail_scr, ct_scr, n_scr, m_scr):
    lc = ML_CHUNK

    @pl.when(pl.program_id(1) == 0)
    def _():
        tail_scr[...] = jnp.zeros_like(tail_scr)
        ct_scr[...] = jnp.zeros_like(ct_scr)
        n_scr[...] = jnp.zeros_like(n_scr)
        m_scr[...] = jnp.zeros_like(m_scr)

    u = qk_ref[...]
    tail = tail_scr[...]
    rr8 = lax.broadcasted_iota(jnp.int32, (SUBLANES, 1), 0)
    sr = lax.broadcasted_iota(jnp.int32, (ML_BLOCK, ML_BLOCK), 0)
    sc_ = lax.broadcasted_iota(jnp.int32, (ML_BLOCK, ML_BLOCK), 1)
    y = bc_ref[...] + wc_ref[CONV_W - 1:CONV_W, :] * u.astype(F32)
    for k in range(1, CONV_W):
        down = _dot((sr - sc_ == k).astype(BF16), u)
        head = jnp.where(rr8 < k, pltpu.roll(tail, k, 0), down[0:SUBLANES, :])
        y = y + wc_ref[CONV_W - 1 - k:CONV_W - k, :] * jnp.concatenate([head, down[SUBLANES:, :]], axis=0)
    tail_scr[...] = u[ML_BLOCK - SUBLANES:ML_BLOCK, :].astype(F32)
    qkc = y * _sigmoid(y)
    q_all = qkc[:, 0:ML_WIDTH].astype(BF16)
    k_all = (qkc[:, ML_WIDTH:2 * ML_WIDTH] * (ML_HD ** -0.5)).astype(BF16)

    ifc = sm_ref[...] + bifc_ref[...]
    ifr = smt_ref[...] + bifr_ref[...]
    lfc = _log_sigmoid(ifc)
    lfr = _log_sigmoid(ifr)
    rr = lax.broadcasted_iota(jnp.int32, (lc, lc), 0)
    cc = lax.broadcasted_iota(jnp.int32, (lc, lc), 1)
    causal = rr >= cc
    tri_l = causal.astype(F32)
    tri_u = (rr <= cc).astype(F32)

    for ci in range(ML_BLOCK // lc):
        lo, hi = ci * lc, (ci + 1) * lc
        bc_all = jnp.dot(tri_l, lfc[lo:hi, :], preferred_element_type=F32, precision=lax.Precision.HIGHEST)
        br_all = jnp.dot(lfr[:, lo:hi], tri_u, preferred_element_type=F32, precision=lax.Precision.HIGHEST)
        heads = range(ML_HEADS)
        hsl = [slice(h * ML_HD, (h + 1) * ML_HD) for h in heads]
        bcol = [bc_all[:, SM_F + h:SM_F + h + 1] for h in heads]
        brow = [br_all[SM_F + h:SM_F + h + 1, :] for h in heads]
        icol = [ifc[lo:hi, SM_I + h:SM_I + h + 1] for h in heads]
        irow = [ifr[SM_I + h:SM_I + h + 1, lo:hi] for h in heads]
        mprev = [m_scr[h][:, 0:1] for h in heads]
        qh = [q_all[lo:hi, hsl[h]] for h in heads]
        kh = [k_all[lo:hi, hsl[h]] for h in heads]
        vh = [v_ref[lo:hi, hsl[h]] for h in heads]
        ct = [ct_scr[h] for h in heads]
        nrow = [n_scr[h] for h in heads]
        qk = [_dot_nt(qh[h], kh[h]) for h in heads]
        qc = [_dot(qh[h], ct[h].astype(BF16)) for h in heads]
        dmat = [jnp.where(causal, bcol[h] - brow[h] + irow[h], NEG) for h in heads]
        inter = [bcol[h] + mprev[h] for h in heads]
        mt = [jnp.maximum(jnp.max(dmat[h], axis=-1, keepdims=True), inter[h]) for h in heads]
        a = [jnp.exp(dmat[h] - mt[h]) * qk[h] for h in heads]
        dec = [jnp.exp(inter[h] - mt[h]) for h in heads]
        num = [_dot(a[h].astype(BF16), vh[h]) + dec[h] * qc[h] for h in heads]
        den = [jnp.sum(a[h], axis=-1, keepdims=True)
               + dec[h] * jnp.sum(qh[h].astype(F32) * nrow[h], axis=-1, keepdims=True) for h in heads]
        blast = [bcol[h][lc - 1:lc, :] for h in heads]
        mnew = [jnp.maximum(blast[h] + mprev[h], jnp.max(blast[h] - brow[h] + irow[h], axis=-1, keepdims=True))
                for h in heads]
        wprev = [jnp.exp(blast[h] + mprev[h] - mnew[h]) for h in heads]
        kwt = [kh[h].astype(F32) * jnp.exp(blast[h] - bcol[h] + icol[h] - mnew[h]) for h in heads]
        for h in heads:
            ct_scr[h] = wprev[h] * ct[h] + lax.dot_general(kwt[h].astype(BF16), vh[h], _TN,
                                                           preferred_element_type=F32)
            n_scr[h] = wprev[h] * nrow[h] + jnp.sum(kwt[h], axis=0, keepdims=True)
            m_scr[h] = jnp.broadcast_to(mnew[h], (1, LANES))
        hm = [num[h] / jnp.maximum(jnp.abs(den[h]), jnp.exp(-mt[h])) * _sigmoid(og_ref[lo:hi, hsl[h]].astype(F32))
              for h in heads]
        for h in heads:
            y_ref[lo:hi, hsl[h]] = _rms(hm[h], ghn_ref[:, hsl[h]]).astype(BF16)


def _mlstm(qkb, vb, ob, sm, smt, wconv, bconv, bifc, bifr, ghn, b, s):
    nb = s // ML_BLOCK
    row = lambda w: pl.BlockSpec((ML_BLOCK, w), lambda bi, j: (bi * nb + j, 0))
    return pl.pallas_call(
        _mlstm_kernel,
        out_shape=jax.ShapeDtypeStruct((b * s, ML_WIDTH), BF16),
        grid=(b, nb),
        in_specs=[row(2 * ML_WIDTH), row(ML_WIDTH), row(ML_WIDTH), row(LANES),
                  pl.BlockSpec((LANES, ML_BLOCK), lambda bi, j: (0, bi * nb + j)),
                  _const_spec(wconv.shape), _const_spec(bconv.shape), _const_spec(bifc.shape),
                  _const_spec(bifr.shape), _const_spec(ghn.shape)],
        out_specs=row(ML_WIDTH),
        scratch_shapes=[pltpu.VMEM((SUBLANES, 2 * ML_WIDTH), F32),
                        pltpu.VMEM((ML_HEADS, ML_HD, ML_HD), F32),
                        pltpu.VMEM((ML_HEADS, 1, ML_HD), F32),
                        pltpu.VMEM((ML_HEADS, 1, LANES), F32)],
        compiler_params=pltpu.CompilerParams(dimension_semantics=("arbitrary", "arbitrary"),
                                             vmem_limit_bytes=VMEM_LIMIT),
        name="mlstm",
    )(qkb, vb, ob, sm, smt, wconv, bconv, bifc, bifr, ghn)


RT_BUCKET = N_EXPERTS
RT_RANK = N_EXPERTS + 1
RT_WLO = N_EXPERTS + 2
RT_WHI = N_EXPERTS + 3
RT_LPOS = N_EXPERTS + 4
N_BUCKETS = N_GROUPS * 6
X_ROWS = D_MODEL // LANES
DMA_UNROLL = 8
RUN_CHUNK = SUBLANES
MAX_CHUNKS = TD // RUN_CHUNK + N_BUCKETS
SORT_ROWS = MAX_CHUNKS * RUN_CHUNK
COMBINE_PARTS = 2


def _merge_kernel(ya_ref, yb_ref, mg_ref, x_ref, wpa_ref, wpb_ref, wout_ref, gffn_ref, wr_ref, br_ref,
                  x1_ref, h2_ref, slab_ref, hist_ref):
    halves = [slice(i * (TD // 2), (i + 1) * (TD // 2)) for i in range(2)]
    pa = [_dot(ya_ref[hs, :], wpa_ref[...]) for hs in halves]
    pb = [_dot(yb_ref[hs, :], wpb_ref[...]) for hs in halves]
    mixed = [(_sigmoid(mg_ref[hs, 0:D_MODEL].astype(F32)) * pa[i]
              + _sigmoid(mg_ref[hs, D_MODEL:2 * D_MODEL].astype(F32)) * pb[i]).astype(BF16)
             for i, hs in enumerate(halves)]
    x1 = [x_ref[hs, :] + _dot(mixed[i], wout_ref[...]) for i, hs in enumerate(halves)]
    h2 = [_rms(x1[i], gffn_ref[...]) for i in range(2)]
    h_hi = [h.astype(BF16) for h in h2]
    for i, hs in enumerate(halves):
        x1_ref[hs, :] = x1[i]
        h2_ref[hs, :] = h_hi[i]

    h_lo = [(h2[i] - h_hi[i].astype(F32)).astype(BF16) for i in range(2)]
    r_hi = [_dot(h, wr_ref[...]) for h in h_hi]
    logit = jnp.concatenate([r_hi[i][:, 0:LANES] + r_hi[i][:, LANES:2 * LANES] + _dot(h_lo[i], wr_ref[:, 0:LANES])
                             for i in range(2)], axis=0) + br_ref[...]
    lane = lax.broadcasted_iota(jnp.int32, logit.shape, 1)
    big = jnp.int32(LANES)
    gmask = (lane >= N_EXPERTS) & (lane < N_EXPERTS + N_GROUPS)
    gl = jnp.where(gmask, logit, NEG)
    gmax = jnp.max(gl, axis=-1, keepdims=True)
    gidx = jnp.min(jnp.where(gmask & (gl == gmax), lane, big), axis=-1, keepdims=True) - N_EXPERTS
    pg_sel = 1.0 / jnp.sum(jnp.where(gmask, jnp.exp(gl - gmax), 0.0), axis=-1, keepdims=True)
    emask = (lane < N_EXPERTS) & (lax.shift_right_logical(lane, 2) == gidx)
    el = jnp.where(emask, logit, NEG)
    e1 = jnp.max(el, axis=-1, keepdims=True)
    i1 = jnp.min(jnp.where(emask & (el == e1), lane, big), axis=-1, keepdims=True)
    emask2 = emask & (lane != i1)
    el2 = jnp.where(emask2, logit, NEG)
    e2 = jnp.max(el2, axis=-1, keepdims=True)
    i2 = jnp.min(jnp.where(emask2 & (el2 == e2), lane, big), axis=-1, keepdims=True)
    x21 = jnp.exp(e2 - e1)
    w1 = pg_sel / (1.0 + x21)
    w2 = pg_sel * x21 / (1.0 + x21)
    first_lo = i1 < i2
    e_lo = jnp.where(first_lo, i1, i2) - EXP_PER_GROUP * gidx
    e_hi = jnp.where(first_lo, i2, i1) - EXP_PER_GROUP * gidx
    pair = lax.shift_right_logical(e_lo * (2 * EXP_PER_GROUP - 1 - e_lo), 1) + (e_hi - e_lo - 1)
    bucket = 6 * gidx + pair
    member = lane == bucket
    onehot = jnp.where(member, 1.0, 0.0)
    rr = lax.broadcasted_iota(jnp.int32, (TD, TD), 0)
    cc = lax.broadcasted_iota(jnp.int32, (TD, TD), 1)
    earlier = _dot((rr > cc).astype(BF16), onehot.astype(BF16))
    rank = jnp.sum(jnp.where(member, earlier, 0.0), axis=-1, keepdims=True)
    slab = jnp.where(lane == i1, w1, 0.0) + jnp.where(lane == i2, w2, 0.0)
    slab = jnp.where(lane == RT_BUCKET, bucket.astype(F32), slab)
    slab = jnp.where(lane == RT_RANK, rank, slab)
    slab = jnp.where(lane == RT_WLO, jnp.where(first_lo, w1, w2), slab)
    slab = jnp.where(lane == RT_WHI, jnp.where(first_lo, w2, w1), slab)
    hist8 = jnp.broadcast_to(jnp.sum(onehot, axis=0, keepdims=True), (SUBLANES, LANES))
    kk = lax.broadcasted_iota(jnp.int32, (LANES, LANES), 0)
    ll = lax.broadcasted_iota(jnp.int32, (LANES, LANES), 1)
    runlen = jnp.ceil(hist8 / RUN_CHUNK) * RUN_CHUNK
    before = sum(_dot(part, (kk < ll).astype(BF16)) for part in _split3(runlen))[0:1, :]
    lpos = jnp.sum(jnp.where(member, before, 0.0), axis=-1, keepdims=True) + rank
    slab = jnp.where(lane == RT_LPOS, lpos, slab)
    slab_ref[...] = slab
    hist_ref[...] = hist8


def _merge(ya, yb, mg, x2, wpa, wpb, wout, gffn, wr, br):
    t = x2.shape[0]
    row = lambda w: pl.BlockSpec((TD, w), lambda i: (i, 0))
    return pl.pallas_call(
        _merge_kernel,
        out_shape=[jax.ShapeDtypeStruct((t, D_MODEL), F32),
                   jax.ShapeDtypeStruct((t, D_MODEL), BF16),
                   jax.ShapeDtypeStruct((t, LANES), F32),
                   jax.ShapeDtypeStruct((t // TD * SUBLANES, LANES), F32)],
        grid=(t // TD,),
        in_specs=[row(NSA_HEADS * NSA_HD), row(ML_WIDTH), row(2 * D_MODEL), row(D_MODEL)]
                 + [_const_spec(a.shape) for a in (wpa, wpb, wout, gffn, wr, br)],
        out_specs=[row(D_MODEL), row(D_MODEL), row(LANES),
                   pl.BlockSpec((SUBLANES, LANES), lambda i: (i, 0))],
        compiler_params=pltpu.CompilerParams(dimension_semantics=("arbitrary",), vmem_limit_bytes=VMEM_LIMIT),
        name="merge",
    )(ya, yb, mg, x2, wpa, wpb, wout, gffn, wr, br)


def _rec_copy(src_ref, src_tok, dst_ref, dst_tok, sem, rows):
    src = src_ref.at[pl.ds(pl.multiple_of(src_tok * rows, rows), rows), :]
    dst = dst_ref.at[pl.ds(pl.multiple_of(dst_tok * rows, rows), rows), :]
    return pltpu.make_async_copy(src, dst, sem)


def _token_copies(n, make, wait=False):
    def body(g, carry):
        for u in range(DMA_UNROLL):
            cp = make(g * DMA_UNROLL + u)
            if wait:
                cp.wait()
            else:
                cp.start(priority=u % 2)
        return carry
    lax.fori_loop(0, n // DMA_UNROLL, body, 0)


def _dispatch_kernel(ch_ref, tail_ref, h2_ref, slab_ref, hx_ref, sx_ref, h_stage, s_stage, h_zero, s_zero, sem, zsem):
    i = pl.program_id(0)
    n_steps = pl.num_programs(0)
    slot = lax.rem(i, 2)
    n_tiles = hx_ref.shape[0] // TM
    n_used = tail_ref[2 * N_BUCKETS]

    def zero_copies(first_slot, n):
        rows = pl.ds(pl.multiple_of(first_slot, n), n)
        return (pltpu.make_async_copy(h_zero.at[0:n, :], hx_ref.at[rows, :], zsem),
                pltpu.make_async_copy(s_zero.at[0:n, :], sx_ref.at[rows, :], zsem))

    def zero_fill(wait):
        def chunk(b, q, carry):
            for cp in zero_copies(tail_ref[b] + q * RUN_CHUNK, RUN_CHUNK):
                cp.wait() if wait else cp.start()
            return carry

        def tile(k, carry):
            for cp in zero_copies(k * TM, TM):
                cp.wait() if wait else cp.start()
            return carry
        for b in range(N_BUCKETS):
            lax.fori_loop(0, tail_ref[N_BUCKETS + b], functools.partial(chunk, b), 0)
        lax.fori_loop(n_used, n_tiles, tile, 0)

    @pl.when(i == 0)
    def _():
        h_zero[...] = jnp.zeros_like(h_zero)
        s_zero[...] = jnp.zeros_like(s_zero)
        zero_fill(wait=False)

    slab = slab_ref[...]
    perm = (slab[:, RT_LPOS:RT_LPOS + 1]
            == lax.broadcasted_iota(jnp.int32, (1, SORT_ROWS), 1).astype(F32)).astype(BF16)
    h_stage[slot] = lax.dot_general(perm, h2_ref[...], _TN, preferred_element_type=F32)
    s_stage[slot] = sum(lax.dot_general(perm, part, _TN, preferred_element_type=F32) for part in _split3(slab))

    def chunk_copies(tile, which, wait):
        src_at = n_steps + tile * MAX_CHUNKS
        dst_at = n_steps + (n_steps + tile) * MAX_CHUNKS

        def body(q, carry):
            src = pl.ds(pl.multiple_of(ch_ref[src_at + q], RUN_CHUNK), RUN_CHUNK)
            dst = pl.ds(pl.multiple_of(ch_ref[dst_at + q], RUN_CHUNK), RUN_CHUNK)
            for cp in (pltpu.make_async_copy(h_stage.at[which, src, :], hx_ref.at[dst, :], sem.at[which]),
                       pltpu.make_async_copy(s_stage.at[which, src, :], sx_ref.at[dst, :], sem.at[which])):
                cp.wait() if wait else cp.start()
            return carry
        lax.fori_loop(0, ch_ref[tile], body, 0)

    chunk_copies(i, slot, wait=False)

    @pl.when(i > 0)
    def _():
        chunk_copies(i - 1, 1 - slot, wait=True)

    @pl.when(i == n_steps - 1)
    def _():
        chunk_copies(i, slot, wait=True)
        zero_fill(wait=True)


def _dispatch(chunks, tail, h2, slab, n_slots):
    t = h2.shape[0]
    return pl.pallas_call(
        _dispatch_kernel,
        out_shape=(jax.ShapeDtypeStruct((n_slots, D_MODEL), F32), jax.ShapeDtypeStruct((n_slots, LANES), F32)),
        grid_spec=pltpu.PrefetchScalarGridSpec(
            num_scalar_prefetch=2,
            grid=(t // TD,),
            in_specs=[pl.BlockSpec((TD, D_MODEL), lambda i, ch_r, tail_r: (i, 0)),
                      pl.BlockSpec((TD, LANES), lambda i, ch_r, tail_r: (i, 0))],
            out_specs=(pl.BlockSpec(memory_space=pl.ANY), pl.BlockSpec(memory_space=pl.ANY)),
            scratch_shapes=[pltpu.VMEM((2, SORT_ROWS, D_MODEL), F32), pltpu.VMEM((2, SORT_ROWS, LANES), F32),
                            pltpu.VMEM((TM, D_MODEL), F32), pltpu.VMEM((TM, LANES), F32),
                            pltpu.SemaphoreType.DMA((2,)), pltpu.SemaphoreType.DMA(())],
        ),
        compiler_params=pltpu.CompilerParams(dimension_semantics=("arbitrary",), vmem_limit_bytes=VMEM_LIMIT,
                                             has_side_effects=True),
        name="dispatch",
    )(chunks, tail, h2, slab)


MOE_TILES = 2


def _moe_kernel(te_ref, nu_ref, hx_ref, sx_ref, w13_ref, w2_ref, y_ref):
    step = pl.program_id(0)
    n_tiles = pl.num_programs(0) * MOE_TILES
    n_used = nu_ref[0]

    @pl.when(step * MOE_TILES < n_used)
    def _():
        subs = range(MOE_TILES)
        hs = [hx_ref[sub * TM:(sub + 1) * TM, :].astype(BF16) for sub in subs]
        slabs = [sx_ref[sub * TM:(sub + 1) * TM, :] for sub in subs]
        ys = [None] * MOE_TILES
        for side, lane in ((0, RT_WLO), (1, RT_WHI)):
            es = [te_ref[side * n_tiles + step * MOE_TILES + sub] for sub in subs]
            up = [_dot(hs[sub], w13_ref[es[sub]]) for sub in subs]
            act = [(up[sub][:, 0:D_EXPERT] * _sigmoid(up[sub][:, 0:D_EXPERT]) * up[sub][:, D_EXPERT:2 * D_EXPERT]
                    * slabs[sub][:, lane:lane + 1]).astype(BF16) for sub in subs]
            for sub in subs:
                part = _dot(act[sub], w2_ref[es[sub]])
                ys[sub] = part if ys[sub] is None else ys[sub] + part
        for sub in subs:
            for j in range(X_ROWS):
                y_ref[pl.ds(sub * TM * X_ROWS + j, TM, stride=X_ROWS), :] = ys[sub][:, j * LANES:(j + 1) * LANES]

    @pl.when(step * MOE_TILES >= n_used)
    def _():
        y_ref[...] = jnp.zeros_like(y_ref)


def _moe(tile_e, n_used, hx_sorted, slab_sorted, w13, w2):
    n_tiles = hx_sorted.shape[0] // TM
    rows = MOE_TILES * TM
    last = lambda nu: (nu[0] - 1) // MOE_TILES
    return pl.pallas_call(
        _moe_kernel,
        out_shape=jax.ShapeDtypeStruct((n_tiles * TM * X_ROWS, LANES), F32),
        grid_spec=pltpu.PrefetchScalarGridSpec(
            num_scalar_prefetch=2,
            grid=(n_tiles // MOE_TILES,),
            in_specs=[pl.BlockSpec((rows, D_MODEL), lambda k, te, nu: (jnp.minimum(k, last(nu)), 0)),
                      pl.BlockSpec((rows, LANES), lambda k, te, nu: (jnp.minimum(k, last(nu)), 0)),
                      pl.BlockSpec(w13.shape, lambda k, te, nu: (0, 0, 0), pipeline_mode=pl.Buffered(1)),
                      pl.BlockSpec(w2.shape, lambda k, te, nu: (0, 0, 0), pipeline_mode=pl.Buffered(1))],
            out_specs=pl.BlockSpec((rows * X_ROWS, LANES), lambda k, te, nu: (k, 0)),
        ),
        compiler_params=pltpu.CompilerParams(dimension_semantics=("arbitrary",), vmem_limit_bytes=VMEM_LIMIT),
        name="moe",
    )(tile_e, n_used, hx_sorted, slab_sorted, w13, w2)


def _combine_kernel(pos_ref, y_ref, x1_ref, p_ref, gple_ref, wpg_ref, wpp_ref, gfin_ref, o_ref, ybuf, sem):
    i = pl.program_id(0)
    slot = lax.rem(i, 2)

    last = pl.num_programs(0) - 1

    def drain(which):
        _token_copies(TD, lambda r: _rec_copy(y_ref, 0, ybuf.at[which], 0, sem.at[which], X_ROWS), wait=True)

    @pl.when(i == 0)
    def _():
        _token_copies(TD, lambda r: _rec_copy(y_ref, pos_ref[r], ybuf.at[0], r, sem.at[0], X_ROWS))

    drain(slot)
    nxt = jnp.minimum(i + 1, last) * TD
    yb = ybuf.at[slot]
    part = TD // COMBINE_PARTS
    for q in range(COMBINE_PARTS):
        rows = slice(q * part, (q + 1) * part)
        y = jnp.concatenate([yb[pl.ds(q * part * X_ROWS + j, part, stride=X_ROWS), :] for j in range(X_ROWS)],
                            axis=1)
        x2 = x1_ref[rows, :] + y
        h3 = _rms(x2, gple_ref[...]).astype(BF16)
        x3 = x2 + _sigmoid(_dot(h3, wpg_ref[...])) * _dot(p_ref[rows, :].astype(BF16), wpp_ref[...])
        o_ref[rows, :] = _rms(x3, gfin_ref[...])
        for r in range(q * part, (q + 1) * part):
            _rec_copy(y_ref, pos_ref[nxt + r], ybuf.at[1 - slot], r, sem.at[1 - slot], X_ROWS).start(
                priority=r % 2)

    @pl.when(i == last)
    def _():
        drain(1 - slot)


def _combine(pos, y_sorted, x1, p2, gple, wpg, wpp, gfin):
    t = x1.shape[0]
    row = lambda w: pl.BlockSpec((TD, w), lambda i, pos_r: (i, 0))
    const = lambda a: pl.BlockSpec(a.shape, lambda i, pos_r: (0,) * a.ndim, pipeline_mode=pl.Buffered(1))
    return pl.pallas_call(
        _combine_kernel,
        out_shape=jax.ShapeDtypeStruct((t, D_MODEL), F32),
        grid_spec=pltpu.PrefetchScalarGridSpec(
            num_scalar_prefetch=1,
            grid=(t // TD,),
            in_specs=[pl.BlockSpec(memory_space=pl.ANY), row(D_MODEL), row(PLE_DIM),
                      const(gple), const(wpg), const(wpp), const(gfin)],
            out_specs=row(D_MODEL),
            scratch_shapes=[pltpu.VMEM((2, TD * X_ROWS, LANES), F32), pltpu.SemaphoreType.DMA((2,))],
        ),
        compiler_params=pltpu.CompilerParams(dimension_semantics=("arbitrary",), vmem_limit_bytes=VMEM_LIMIT),
        name="combine",
    )(pos, y_sorted, x1, p2, gple, wpg, wpp, gfin)


def _routing_tables(slab, hist8):
    t = slab.shape[0]
    nt = t // TD
    n_tiles = t // TM + N_BUCKETS + -(-nt * N_BUCKETS * (RUN_CHUNK - 1) // TM)
    n_tiles += n_tiles % MOE_TILES
    hist = hist8.reshape(nt, SUBLANES, LANES)[:, 0, :]
    runlen = jnp.ceil(hist / RUN_CHUNK) * RUN_CHUNK
    counts = jnp.sum(runlen, axis=0)
    padded = jnp.ceil(counts / TM) * TM
    ends = jnp.cumsum(padded)
    first = (ends - padded)[None, :] + jnp.cumsum(runlen, axis=0) - runlen
    lane = jnp.arange(N_BUCKETS, dtype=F32)[None, :]
    mine = lane == slab[:, RT_BUCKET:RT_BUCKET + 1]
    pos = jnp.sum(jnp.where(mine, jnp.repeat(first[:, :N_BUCKETS], TD, axis=0), 0.0), axis=1) + slab[:, RT_RANK]
    starts = jnp.arange(n_tiles, dtype=F32) * TM
    tile_bucket = jnp.minimum(jnp.sum(ends[None, :N_BUCKETS] <= starts[:, None], axis=1), N_BUCKETS - 1)
    group, pair = tile_bucket // 6, tile_bucket % 6
    e_lo = EXP_PER_GROUP * group + jnp.array([0, 0, 0, 1, 1, 2], jnp.int32)[pair]
    e_hi = EXP_PER_GROUP * group + jnp.array([1, 2, 3, 2, 3, 3], jnp.int32)[pair]
    tile_e = jnp.concatenate([e_lo, e_hi]).astype(jnp.int32)
    n_used = (ends[N_BUCKETS - 1] / TM).astype(jnp.int32).reshape(1)
    tail = jnp.concatenate([(ends - padded + counts)[:N_BUCKETS], ((padded - counts) / RUN_CHUNK)[:N_BUCKETS],
                            n_used.astype(F32)]).astype(jnp.int32)
    nb = N_BUCKETS
    nch = runlen[:, :nb] / RUN_CHUNK
    cum = jnp.cumsum(nch, axis=1)
    q = jnp.arange(MAX_CHUNKS, dtype=F32)[None, :, None]
    of_b = jnp.arange(nb)[None, None, :] == jnp.minimum(jnp.sum(cum[:, None, :] <= q, axis=2), nb - 1)[..., None]
    pick = lambda a: jnp.sum(jnp.where(of_b, a[:, None, :nb], 0.0), axis=2)
    within = RUN_CHUNK * (q[..., 0] - pick(cum - nch))
    local = jnp.cumsum(runlen, axis=1) - runlen
    chunks = jnp.concatenate([cum[:, nb - 1], (pick(local) + within).reshape(-1),
                              (pick(first) + within).reshape(-1)]).astype(jnp.int32)
    return pos.astype(jnp.int32), chunks, tile_e, n_used, tail, n_tiles * TM


def _pack_inproj_weights(w):
    d = w.shape[0]
    qw = NSA_HEADS * NSA_HD
    kvw = NSA_KV * NSA_HD
    o = 0
    wq = w[:, o:o + qw]; o += qw
    wkc = w[:, o:o + kvw]; o += kvw
    wvc = w[:, o:o + kvw]; o += kvw
    wks = w[:, o:o + kvw]; o += kvw
    wvs = w[:, o:o + kvw]; o += kvw
    wkw = w[:, o:o + kvw]; o += kvw
    wvw = w[:, o:o + kvw]; o += kvw
    wga = w[:, o:o + 3 * NSA_HEADS]; o += 3 * NSA_HEADS
    wqkb = w[:, o:o + 2 * ML_WIDTH]; o += 2 * ML_WIDTH
    wvb = w[:, o:o + ML_WIDTH]; o += ML_WIDTH
    wob = w[:, o:o + ML_WIDTH]; o += ML_WIDTH
    wif = w[:, o:o + 2 * ML_HEADS]; o += 2 * ML_HEADS
    wmg = w[:, o:o + 2 * D_MODEL]
    wsm = jnp.concatenate([wga, wif, jnp.zeros((d, LANES - 3 * NSA_HEADS - 2 * ML_HEADS), w.dtype)], axis=1)
    wcat = jnp.concatenate([wq, wkc, wvc, wks, wkw, wsm, wqkb, wvb, wob, wmg], axis=1).astype(BF16)
    wtr = jnp.concatenate([wvs, wvw, wsm], axis=1).T.astype(BF16)
    return wcat, wtr


def _rope_tables(positions):
    half = ROPE_DIM // 2
    inv = ROPE_THETA ** (-jnp.arange(0, ROPE_DIM, 2, dtype=F32) / ROPE_DIM)
    ang = positions.astype(F32).reshape(-1, 1) * inv[None, :]
    cs = jnp.concatenate([jnp.cos(ang), jnp.sin(ang)], axis=1)
    d = np.arange(LANES) % NSA_HD
    spread = np.zeros((2 * half, 3 * LANES), np.float32)
    lanes = np.arange(LANES)
    rot = d < ROPE_DIM
    spread[(d % half)[rot], lanes[rot]] = 1.0
    hi = (d >= half) & rot
    spread[half + (d % half)[hi], LANES + lanes[hi]] = 1.0
    lo = d < half
    spread[half + (d % half)[lo], 2 * LANES + lanes[lo]] = -1.0
    unrotated = (~rot).astype(np.float32).reshape(1, LANES)
    return cs, jnp.asarray(spread, BF16), jnp.asarray(unrotated)


def _pack_compress_weights(w1, w2, pe):
    half = CMP_LEN // 2
    w1r = w1.reshape(2, half, NSA_HD, CMP_HIDDEN)
    outs = []
    for part in range(2):
        wb = w1r[part].astype(BF16)
        zb = jnp.zeros_like(wb)
        wp = jnp.stack([jnp.stack([wb, zb], axis=2), jnp.stack([zb, wb], axis=2)], axis=1)
        outs.append(wp.reshape(half * NSA_KV * NSA_HD, NSA_KV * CMP_HIDDEN))
    pe8 = jnp.broadcast_to(pe.reshape(1, CMP_LEN * NSA_HD), (SUBLANES, CMP_LEN * NSA_HD)).astype(BF16)
    return outs[0], outs[1], pe8, w1.astype(BF16)


def _stages(x, p, positions, g_mix, w_in, b_if, w_ck1, w_ck2, pe_ck, w_cv1, w_cv2, pe_cv, w_conv, b_conv, g_hn, w_pa, w_pb, w_out, g_ffn, w_rg, b_rg, w_re, b_re, w_e13, w_e2, g_ple, w_pg, w_pp, g_final):
    b, s, d = x.shape
    t = b * s
    cs, spread, unrot = _rope_tables(positions)
    assert w_in.shape[0] == 1, "the final norm is fused into the layer's last kernel: single-layer problem only"
    for i in range(w_in.shape[0]):
        x2 = x.reshape(t, d)
        wcat, wtr = _pack_inproj_weights(w_in[i])
        (qpad, kc_tok, vc_tok, ks, kw, vst, vwt, sm, smt, qkb, vb, ob, mg) = _inproj(
            x2, g_mix[i].reshape(1, d), wcat, wtr, cs, spread, unrot)
        wka, wkb, pek, w1k = _pack_compress_weights(w_ck1[i], w_ck2[i], pe_ck[i])
        wva, wvb, pev, w1v = _pack_compress_weights(w_cv1[i], w_cv2[i], pe_cv[i])
        zpad = jnp.zeros((CMP_HIDDEN, NSA_HD), F32)
        w2k = jnp.stack([jnp.concatenate([w_ck2[i], zpad], axis=1),
                         jnp.concatenate([zpad, w_ck2[i]], axis=1)]).astype(BF16)
        w2vt = w_cv2[i].T.astype(BF16)
        nrow = s // CMP_STRIDE
        rk = kc_tok.reshape(b, nrow, CMP_STRIDE * LANES)
        rv = vc_tok.reshape(b, nrow, CMP_STRIDE * LANES)
        kcb, vct = _compress(rk, rv, wka, wkb, wva, wvb, pek, pev, w1k, w1v, w2k, w2vt)
        ya = _nsa(qpad, kcb, vct, ks, kw, vst, vwt, smt, b, s)
        bif = b_if[i].astype(F32)
        bifc = jnp.zeros((1, LANES), F32).at[0, SM_I:SM_I + 2 * ML_HEADS].set(bif)
        bifr = bifc.reshape(LANES, 1)
        yb = _mlstm(qkb, vb, ob, sm, smt, w_conv[i], b_conv[i].reshape(1, -1), bifc, bifr,
                    g_hn[i].reshape(1, -1), b, s)
        wr = jnp.concatenate([w_re[i], w_rg[i], jnp.zeros((d, LANES - N_EXPERTS - N_GROUPS), F32)], axis=1)
        wr_hi = wr.astype(BF16)
        wr = jnp.concatenate([wr_hi, (wr - wr_hi.astype(F32)).astype(BF16)], axis=1)
        br =jnp.concatenate([b_re[i], b_rg[i], jnp.zeros((LANES - N_EXPERTS - N_GROUPS,), F32)]).reshape(1, LANES)
        x1, h2, slab, hist8 = _merge(ya, yb, mg, x2, w_pa[i].astype(BF16), w_pb[i].astype(BF16),
                                     w_out[i].astype(BF16), g_ffn[i].reshape(1, d), wr, br)
        pos, chunks, tile_e, n_used, tail, n_slots = _routing_tables(slab, hist8)
        hx_sorted, slab_sorted = _dispatch(chunks, tail, h2, slab, n_slots)
        y_sorted = _moe(tile_e, n_used, hx_sorted, slab_sorted, w_e13[i].astype(BF16), w_e2[i].astype(BF16))
        out = _combine(pos, y_sorted, x1, p[i].reshape(t, PLE_DIM), g_ple[i].reshape(1, d), w_pg[i].astype(BF16),
                       w_pp[i].astype(BF16), g_final.reshape(1, d))
        x = out.reshape(b, s, d)
    return dict(out=x, qpad=qpad, ks=ks, kcb=kcb, vct=vct, y_a=ya, y_b=yb, x1=x1, pos=pos)


def kernel(x, p, positions, g_mix, w_in, b_if, w_ck1, w_ck2, pe_ck, w_cv1, w_cv2, pe_cv, w_conv, b_conv, g_hn, w_pa, w_pb, w_out, g_ffn, w_rg, b_rg, w_re, b_re, w_e13, w_e2, g_ple, w_pg, w_pp, g_final):
    return _stages(x, p, positions, g_mix, w_in, b_if, w_ck1, w_ck2, pe_ck, w_cv1, w_cv2, pe_cv, w_conv, b_conv, g_hn,
                   w_pa, w_pb, w_out, g_ffn, w_rg, b_rg, w_re, b_re, w_e13, w_e2, g_ple, w_pg, w_pp, g_final)["out"]
```

```python
import functools
import math

import numpy as np
import jax
import jax.numpy as jnp
from jax import lax
from jax.experimental import pallas as pl
from jax.experimental.pallas import tpu as pltpu

F32 = jnp.float32
BF16 = jnp.bfloat16

EPS = 1e-6
NEG = -1e30

D_MODEL = 1024
PLE_DIM = 256
NSA_HEADS = 8
NSA_KV = 2
NSA_HPG = NSA_HEADS // NSA_KV
NSA_HD = 64
CMP_LEN = 32
CMP_STRIDE = 16
CMP_HIDDEN = 256
SEL_BLOCK = 64
SEL_TOPK = 16
SEL_FORCE = 1000.0
WINDOW = 512
ROPE_THETA = 500000.0
ROPE_DIM = NSA_HD // 4
ML_HEADS = 4
ML_HD = 128
ML_WIDTH = ML_HEADS * ML_HD
CONV_W = 4
N_GROUPS = 4
EXP_PER_GROUP = 4
N_EXPERTS = N_GROUPS * EXP_PER_GROUP
D_EXPERT = 256

LANES = 128
SUBLANES = 8
QT = 128
KC = 128
SEL_GROUP = 512
VT_PAD = 16
VT_ROWS = NSA_HD + VT_PAD
ML_CHUNK = 128
ML_BLOCK = 256
TD = 512
TM = 256
VMEM_LIMIT = 56 * 1024 * 1024

_NT = (((1,), (1,)), ((), ()))
_TN = (((0,), (0,)), ((), ()))

SM_GATE = 0
SM_I = 3 * NSA_HEADS
SM_F = SM_I + ML_HEADS


def _dot(a, b):
    return jnp.dot(a, b, preferred_element_type=F32)


def _dot_nt(a, b):
    return lax.dot_general(a, b, _NT, preferred_element_type=F32)


def _split3(x):
    hi = x.astype(BF16)
    r1 = x - hi.astype(F32)
    mid = r1.astype(BF16)
    lo = (r1 - mid.astype(F32)).astype(BF16)
    return hi, mid, lo


def _rms(x, g):
    return x * lax.rsqrt(jnp.mean(x * x, axis=-1, keepdims=True) + EPS) * g


def _sigmoid(x):
    return 0.5 + 0.5 * jnp.tanh(0.5 * x)


def _const_spec(shape):
    nd = len(shape)
    return pl.BlockSpec(shape, lambda *_: (0,) * nd, pipeline_mode=pl.Buffered(1))


_C_Q = 0
_C_KC = _C_Q + NSA_HEADS * NSA_HD
_C_VC = _C_KC + LANES
_C_KS = _C_VC + LANES
_C_KW = _C_KS + LANES
_C_SM = _C_KW + LANES
_C_QKB = _C_SM + LANES
_C_VB = _C_QKB + 2 * ML_WIDTH
_C_OB = _C_VB + ML_WIDTH
_C_MG = _C_OB + ML_WIDTH
_C_END = _C_MG + 2 * D_MODEL


def _inproj_kernel(x_ref, g_ref, w_ref, wt_ref, cs_ref, spread_ref, unrot_ref,
                   q_ref, kc_ref, vc_ref, ks_ref, kw_ref, vst_ref, vwt_ref, sm_ref, smt_ref,
                   qkb_ref, vb_ref, ob_ref, mg_ref):
    hn = _rms(x_ref[...], g_ref[...]).astype(BF16)
    tables = sum(_dot(part, spread_ref[...]) for part in _split3(cs_ref[...]))
    rc = tables[:, 0:LANES] + unrot_ref[...]
    rp = tables[:, LANES:2 * LANES]
    rm = tables[:, 2 * LANES:3 * LANES]

    def rope(z):
        half = ROPE_DIM // 2
        return z * rc + pltpu.roll(z, half, 1) * rp + pltpu.roll(z, LANES - half, 1) * rm

    scale = NSA_HD ** -0.5 * math.log2(math.e)
    for h in range(NSA_HEADS * NSA_HD // LANES):
        z = _dot(hn, w_ref[:, _C_Q + h * LANES:_C_Q + (h + 1) * LANES])
        q_ref[:, h * LANES:(h + 1) * LANES] = (rope(z) * scale).astype(BF16)
    kc_ref[...] = rope(_dot(hn, w_ref[:, _C_KC:_C_KC + LANES])).astype(BF16)
    vc_ref[...] = _dot(hn, w_ref[:, _C_VC:_C_VC + LANES]).astype(BF16)
    ks_ref[...] = rope(_dot(hn, w_ref[:, _C_KS:_C_KS + LANES])).astype(BF16)
    kw_ref[...] = rope(_dot(hn, w_ref[:, _C_KW:_C_KW + LANES])).astype(BF16)
    sm_ref[...] = _dot(hn, w_ref[:, _C_SM:_C_SM + LANES])
    for c0 in range(0, 2 * ML_WIDTH, 512):
        qkb_ref[:, c0:c0 + 512] = _dot(hn, w_ref[:, _C_QKB + c0:_C_QKB + c0 + 512]).astype(BF16)
    vb_ref[...] = _dot(hn, w_ref[:, _C_VB:_C_VB + ML_WIDTH]).astype(BF16)
    ob_ref[...] = _dot(hn, w_ref[:, _C_OB:_C_OB + ML_WIDTH]).astype(BF16)
    for c0 in range(0, 2 * D_MODEL, 512):
        mg_ref[:, c0:c0 + 512] = _dot(hn, w_ref[:, _C_MG + c0:_C_MG + c0 + 512]).astype(BF16)
    zt = _dot_nt(wt_ref[...], hn)
    ones_rows = (lax.broadcasted_iota(jnp.int32, (VT_PAD, KC), 0) == 0).astype(BF16)
    for i in range(TD // KC):
        for ref, r0 in ((vst_ref, 0), (vwt_ref, LANES)):
            zc = zt[r0:r0 + LANES, i * KC:(i + 1) * KC].astype(BF16)
            ref[i] = jnp.concatenate([piece for g in range(NSA_KV)
                                      for piece in (zc[g * NSA_HD:(g + 1) * NSA_HD, :], ones_rows)], axis=0)
    smt_ref[...] = zt[2 * LANES:3 * LANES, :]


def _inproj(x2, g_mix, wcat, wtr, cs, spread, unrot):
    t = x2.shape[0]
    row = lambda w: pl.BlockSpec((TD, w), lambda i: (i, 0))
    out_shape = [
        jax.ShapeDtypeStruct((t, NSA_HEADS * NSA_HD), BF16),
        jax.ShapeDtypeStruct((t, LANES), BF16),
        jax.ShapeDtypeStruct((t, LANES), BF16),
        jax.ShapeDtypeStruct((t, LANES), BF16),
        jax.ShapeDtypeStruct((t, LANES), BF16),
        jax.ShapeDtypeStruct((t // KC, NSA_KV * VT_ROWS, KC), BF16),
        jax.ShapeDtypeStruct((t // KC, NSA_KV * VT_ROWS, KC), BF16),
        jax.ShapeDtypeStruct((t, LANES), F32),
        jax.ShapeDtypeStruct((LANES, t), F32),
        jax.ShapeDtypeStruct((t, 2 * ML_WIDTH), BF16),
        jax.ShapeDtypeStruct((t, ML_WIDTH), BF16),
        jax.ShapeDtypeStruct((t, ML_WIDTH), BF16),
        jax.ShapeDtypeStruct((t, 2 * D_MODEL), BF16),
    ]
    chunk3 = pl.BlockSpec((TD // KC, NSA_KV * VT_ROWS, KC), lambda i: (i, 0, 0))
    out_specs = [row(NSA_HEADS * NSA_HD), row(LANES), row(LANES), row(LANES), row(LANES), chunk3, chunk3,
                 row(LANES), pl.BlockSpec((LANES, TD), lambda i: (0, i)),
                 row(2 * ML_WIDTH), row(ML_WIDTH), row(ML_WIDTH), row(2 * D_MODEL)]
    return pl.pallas_call(
        _inproj_kernel,
        out_shape=out_shape,
        grid=(t // TD,),
        in_specs=[row(D_MODEL), _const_spec((1, D_MODEL)), _const_spec((D_MODEL, _C_END)),
                  _const_spec((3 * LANES, D_MODEL)), row(cs.shape[1]), _const_spec(spread.shape),
                  _const_spec(unrot.shape)],
        out_specs=out_specs,
        compiler_params=pltpu.CompilerParams(dimension_semantics=("arbitrary",), vmem_limit_bytes=VMEM_LIMIT),
        name="inproj",
    )(x2, g_mix, wcat, wtr, cs, spread, unrot)


def _gelu_tanh(x):
    return 0.5 * x * (1.0 + jnp.tanh(math.sqrt(2.0 / math.pi) * (x + 0.044715 * x * x * x)))


def _compress_kernel(rk_ref, rv_ref, wka_ref, wkb_ref, wva_ref, wvb_ref, pek_ref, pev_ref,
                     w1k_ref, w1v_ref, w2k_ref, w2vt_ref, kc_ref, vct_ref):
    nrow = rk_ref.shape[0]

    def hidden(r_ref, wa_ref, wb_ref, pe_ref, w1_ref):
        r = r_ref[...]
        ha = _dot(r, wa_ref[...])
        hb = _dot(r, wb_ref[...])
        hb = pltpu.roll(hb, nrow - 1, 0)
        c = _dot(pe_ref[...], w1_ref[...])[0:1, :]
        return [_gelu_tanh(ha[:, g * CMP_HIDDEN:(g + 1) * CMP_HIDDEN] + hb[:, g * CMP_HIDDEN:(g + 1) * CMP_HIDDEN] + c).astype(BF16)
                for g in range(NSA_KV)]

    ak = hidden(rk_ref, wka_ref, wkb_ref, pek_ref, w1k_ref)
    kc_ref[...] = (_dot(ak[0], w2k_ref[0]) + _dot(ak[1], w2k_ref[1])).astype(BF16)
    av = hidden(rv_ref, wva_ref, wvb_ref, pev_ref, w1v_ref)
    for g in range(NSA_KV):
        vct_ref[g * NSA_HD:(g + 1) * NSA_HD, :] = _dot_nt(w2vt_ref[...], av[g]).astype(BF16)


def _compress(rk, rv, wka, wkb, wva, wvb, pek, pev, w1k, w1v, w2k, w2vt):
    b, nrow, width = rk.shape
    blk = pl.BlockSpec((None, nrow, width), lambda i: (i, 0, 0))
    return pl.pallas_call(
        _compress_kernel,
        out_shape=[jax.ShapeDtypeStruct((b, nrow, LANES), BF16),
                   jax.ShapeDtypeStruct((b, LANES, nrow), BF16)],
        grid=(b,),
        in_specs=[blk, blk] + [_const_spec(a.shape) for a in (wka, wkb, wva, wvb, pek, pev, w1k, w1v, w2k, w2vt)],
        out_specs=[pl.BlockSpec((None, nrow, LANES), lambda i: (i, 0, 0)),
                   pl.BlockSpec((None, LANES, nrow), lambda i: (i, 0, 0))],
        compiler_params=pltpu.CompilerParams(dimension_semantics=("arbitrary",), vmem_limit_bytes=VMEM_LIMIT),
        name="compress",
    )(rk, rv, wka, wkb, wva, wvb, pek, pev, w1k, w1v, w2k, w2vt)


def _nsa_kernel(q_ref, kc_ref, vct_ref, ks_ref, kw_ref, vst_ref, vwt_ref, smt_ref, o_ref, bias_scr, sx_scr, sy_scr):
    c = pl.program_id(1)
    t0 = c * QT
    ncmp = kc_ref.shape[0]
    nsel = bias_scr.shape[0]
    nw = WINDOW // KC + 1
    gw = NSA_HPG * QT
    width = NSA_KV * gw

    def per_group(x):
        return [x[:, g * gw:(g + 1) * gw] for g in range(NSA_KV)]

    def pv(vt, p):
        rows = vt.shape[0] // NSA_KV
        pb = p.astype(BF16)
        return jnp.concatenate([_dot(vt[g * rows:(g + 1) * rows, :], pg) for g, pg in enumerate(per_group(pb))],
                               axis=1)

    def normalised(acc):
        return acc[0:NSA_HD, :] / acc[NSA_HD:NSA_HD + 1, :]

    low_half = lax.broadcasted_iota(jnp.int32, (1, LANES), 1) < NSA_HD
    q_heads = []
    for h in range(NSA_HEADS):
        pair = q_ref[:, (h // 2) * LANES:(h // 2 + 1) * LANES].astype(F32)
        want_low = h // NSA_HPG == 0
        if (h % 2 == 0) != want_low:
            pair = pltpu.roll(pair, NSA_HD, 1)
        q_heads.append(jnp.where(low_half if want_low else ~low_half, pair, 0.0).astype(BF16))
    qs = jnp.concatenate(q_heads, axis=0)
    u_row = lax.broadcasted_iota(jnp.int32, (1, width), 1) % QT
    t_row = t0 + u_row
    r_kc = lax.broadcasted_iota(jnp.int32, (KC, 1), 0)

    n_grp = ks_ref.shape[0] // SEL_GROUP
    n_full = lax.shift_right_logical(t0, int(math.log2(SEL_GROUP)))

    def qk_group(j):
        return _dot_nt(ks_ref[pl.ds(pl.multiple_of(j * SEL_GROUP, SEL_GROUP), SEL_GROUP), :], qs)

    sc = _dot_nt(kc_ref[...], qs)

    w_slabs, w_chunks = [], []
    for i in range(nw):
        jj = c - (nw - 1) + i
        jc = jnp.maximum(jj, 0)
        si = _dot_nt(kw_ref[pl.ds(pl.multiple_of(jc * KC, KC), KC), :], qs)
        if i == 0:
            keep = jnp.where(jj >= 0, r_kc, -1) > u_row
        elif i == nw - 1:
            keep = r_kc <= u_row
        else:
            keep = jj >= 0
        w_slabs.append(jnp.where(keep, si, NEG))
        w_chunks.append(jc)

    n_col = lax.broadcasted_iota(jnp.int32, (ncmp, 1), 0)
    last_tok = jnp.where(n_col < ncmp - 1, CMP_STRIDE * n_col + (CMP_LEN - 1), jnp.iinfo(jnp.int32).max)
    s = jnp.where(last_tok <= t_row, sc, NEG)
    m = jnp.max(s, axis=0, keepdims=True)
    e = jnp.exp2(s - m)
    anyv = (t_row >= CMP_LEN - 1).astype(F32)
    p = e * (anyv / jnp.sum(e, axis=0, keepdims=True))
    o_cmp = pv(vct_ref[...], p)

    psums = []
    for pg in per_group(p):
        acc_p = pg[:, 0:QT]
        for h in range(1, NSA_HPG):
            acc_p = acc_p + pg[:, h * QT:(h + 1) * QT]
        psums.append(acc_p)
    psum = jnp.concatenate(psums, axis=1)
    nq2 = NSA_KV * QT
    s_col = lax.broadcasted_iota(jnp.int32, (nsel, 1), 0)
    n_lane = lax.broadcasted_iota(jnp.int32, (1, ncmp), 1)
    ov = ((CMP_STRIDE * n_lane < SEL_BLOCK * (s_col + 1)) & (CMP_STRIDE * n_lane + (CMP_LEN - 1) >= SEL_BLOCK * s_col)
          ).astype(BF16)
    imp = sum(_dot(ov, part) for part in _split3(psum))

    s_diag = qk_group(n_full)
    s_own = _dot_nt(ks_ref[pl.ds(pl.multiple_of(t0, KC), KC), :], qs)
    sx_scr[...] = qk_group(0)

    mxw = w_slabs[0]
    for sl in w_slabs[1:]:
        mxw = jnp.maximum(mxw, sl)
    mw = jnp.max(mxw, axis=0, keepdims=True)
    acc_w = jnp.zeros((VT_ROWS, width), F32)
    for sl, jc in zip(w_slabs, w_chunks):
        acc_w = acc_w + pv(vwt_ref[jc], jnp.exp2(sl - mw))
    o_win = normalised(acc_w)

    t1 = t0 + lax.broadcasted_iota(jnp.int32, (1, nq2), 1) % QT
    cur = lax.shift_right_logical(t1, 6)
    forced = (s_col == 0) | (s_col == cur) | (s_col == cur - 1)
    valid = SEL_BLOCK * s_col <= t1
    val = jnp.where(valid, jnp.where(forced, imp + SEL_FORCE, imp), NEG)
    sub = 8
    r_sub = lax.broadcasted_iota(jnp.int32, (sub, 1), 0)
    blocks = [val[r * sub:(r + 1) * sub, :] for r in range(nsel // sub)]
    ranks = [jnp.zeros((sub, nq2), F32) for _ in blocks]
    for i in range(nsel):
        vi = val[i:i + 1, :]
        for r, blk in enumerate(blocks):
            if i < r * sub:
                beats = vi >= blk
            elif i >= (r + 1) * sub:
                beats = vi > blk
            else:
                beats = (vi > blk) | ((vi == blk) & (r_sub > i - r * sub))
            ranks[r] = ranks[r] + jnp.where(beats, 1.0, 0.0)
    bias = jnp.where(jnp.concatenate(ranks, axis=0) < SEL_TOPK, 0.0, NEG).astype(F32)
    bias_scr[...] = jnp.concatenate([bias[:, g * QT:(g + 1) * QT] for g in range(NSA_KV) for _ in range(NSA_HPG)],
                                    axis=1)

    blk_per_grp = SEL_GROUP // SEL_BLOCK
    chunk_per_grp = SEL_GROUP // KC
    blk_per_chunk = KC // SEL_BLOCK

    def sel_update(j, sj, carry):
        m_o, acc = carry
        brows = [bias_scr[pl.ds(blk_per_grp * j + i, 1), :] for i in range(blk_per_grp)]

        def block(i):
            return sj[i * SEL_BLOCK:(i + 1) * SEL_BLOCK, :]

        mx = None
        for i in range(blk_per_grp):
            sl = block(i) + brows[i]
            mx = sl if mx is None else jnp.maximum(mx, sl)
        m_n = jnp.maximum(m_o, jnp.max(mx, axis=0, keepdims=True))
        a = jnp.exp2(m_o - m_n)
        acc = a * acc
        for ci in range(chunk_per_grp):
            parts = [jnp.exp2(block(i) + (brows[i] - m_n))
                     for i in range(blk_per_chunk * ci, blk_per_chunk * (ci + 1))]
            acc = acc + pv(vst_ref[chunk_per_grp * j + ci], jnp.concatenate(parts, axis=0))
        return m_n, acc

    def seed_state():
        first_own = blk_per_chunk * (c % chunk_per_grp)
        brows = [bias_scr[pl.ds(blk_per_grp * n_full + i, 1), :] + jnp.where(i < first_own, 0.0, NEG)
                 for i in range(blk_per_grp)]
        own = jnp.where(r_kc <= u_row, s_own, NEG)
        mx = jnp.maximum(own[0:SEL_BLOCK, :], own[SEL_BLOCK:KC, :])
        for i in range(blk_per_grp):
            mx = jnp.maximum(mx, s_diag[i * SEL_BLOCK:(i + 1) * SEL_BLOCK, :] + brows[i])
        m_n = jnp.max(mx, axis=0, keepdims=True)
        acc = pv(vst_ref[c], jnp.exp2(own - m_n))
        for ci in range(chunk_per_grp):
            parts = [jnp.exp2(s_diag[i * SEL_BLOCK:(i + 1) * SEL_BLOCK, :] + (brows[i] - m_n))
                     for i in range(blk_per_chunk * ci, blk_per_chunk * (ci + 1))]
            acc = acc + pv(vst_ref[chunk_per_grp * n_full + ci], jnp.concatenate(parts, axis=0))
        return m_n, acc

    seeded = seed_state()

    def pair_body(jp, carry):
        ja, jb = 2 * jp, 2 * jp + 1
        sy_scr[...] = qk_group(jb)
        carry = sel_update(ja, sx_scr, carry)
        sx_scr[...] = qk_group(jnp.minimum(ja + 2, n_grp - 1))
        return sel_update(jb, sy_scr, carry)

    n_pairs = lax.shift_right_logical(n_full, 1)
    carry = lax.fori_loop(0, n_pairs, pair_body, seeded)
    _, acc_s = lax.cond(n_full - 2 * n_pairs == 1, lambda cr: sel_update(n_full - 1, sx_scr, cr), lambda cr: cr,
                        carry)
    o_sel = normalised(acc_s)

    def gate_row(br):
        rows = [smt_ref[SM_GATE + 3 * h + br:SM_GATE + 3 * h + br + 1, :] for h in range(NSA_HEADS)]
        return _sigmoid(jnp.concatenate(rows, axis=1))

    o_t = gate_row(0) * o_cmp + gate_row(1) * o_sel + gate_row(2) * o_win
    for pr in range(NSA_HEADS // 2):
        xp = jnp.concatenate([o_t[:, (2 * pr) * QT:(2 * pr + 1) * QT], o_t[:, (2 * pr + 1) * QT:(2 * pr + 2) * QT]], axis=0)
        o_ref[:, pr * LANES:(pr + 1) * LANES] = xp.T.astype(BF16)


def _nsa(qpad, kcb, vct, ks, kw, vst, vwt, smt, b, s):
    nq = s // QT
    ncmp = kcb.shape[1]
    return pl.pallas_call(
        _nsa_kernel,
        out_shape=jax.ShapeDtypeStruct((b * s, NSA_HEADS * NSA_HD), BF16),
        grid=(b, nq),
        in_specs=[
            pl.BlockSpec((QT, NSA_HEADS * NSA_HD), lambda bi, c: (bi * nq + c, 0)),
            pl.BlockSpec((None, ncmp, LANES), lambda bi, c: (bi, 0, 0)),
            pl.BlockSpec((None, NSA_KV * NSA_HD, ncmp), lambda bi, c: (bi, 0, 0)),
            pl.BlockSpec((s, LANES), lambda bi, c: (bi, 0)),
            pl.BlockSpec((s, LANES), lambda bi, c: (bi, 0)),
            pl.BlockSpec((s // KC, NSA_KV * VT_ROWS, KC), lambda bi, c: (bi, 0, 0)),
            pl.BlockSpec((s // KC, NSA_KV * VT_ROWS, KC), lambda bi, c: (bi, 0, 0)),
            pl.BlockSpec((LANES, QT), lambda bi, c: (0, bi * nq + c)),
        ],
        out_specs=pl.BlockSpec((QT, NSA_HEADS * NSA_HD), lambda bi, c: (bi * nq + c, 0)),
        scratch_shapes=[pltpu.VMEM((s // SEL_BLOCK, NSA_HEADS * QT), F32),
                        pltpu.VMEM((SEL_GROUP, NSA_HEADS * QT), F32),
                        pltpu.VMEM((SEL_GROUP, NSA_HEADS * QT), F32)],
        compiler_params=pltpu.CompilerParams(dimension_semantics=("arbitrary", "arbitrary"),
                                             vmem_limit_bytes=VMEM_LIMIT),
        name="nsa",
    )(qpad, kcb, vct, ks, kw, vst, vwt, smt)


def _log_sigmoid(x):
    return jnp.minimum(x, 0.0) - jnp.log(1.0 + jnp.exp(-jnp.abs(x)))


def _mlstm_kernel(qk_ref, v_ref, og_ref, sm_ref, smt_ref, wc_ref, bc_ref, bifc_ref, bifr_ref, ghn_ref,
                  y_ref, tail_scr, ct_scr, n_scr, m_scr):
    lc = ML_CHUNK

    @pl.when(pl.program_id(1) == 0)
    def _():
        tail_scr[...] = jnp.zeros_like(tail_scr)
        ct_scr[...] = jnp.zeros_like(ct_scr)
        n_scr[...] = jnp.zeros_like(n_scr)
        m_scr[...] = jnp.zeros_like(m_scr)

    u = qk_ref[...]
    tail = tail_scr[...]
    rr8 = lax.broadcasted_iota(jnp.int32, (SUBLANES, 1), 0)
    sr = lax.broadcasted_iota(jnp.int32, (ML_BLOCK, ML_BLOCK), 0)
    sc_ = lax.broadcasted_iota(jnp.int32, (ML_BLOCK, ML_BLOCK), 1)
    y = bc_ref[...] + wc_ref[CONV_W - 1:CONV_W, :] * u.astype(F32)
    for k in range(1, CONV_W):
        down = _dot((sr - sc_ == k).astype(BF16), u)
        head = jnp.where(rr8 < k, pltpu.roll(tail, k, 0), down[0:SUBLANES, :])
        y = y + wc_ref[CONV_W - 1 - k:CONV_W - k, :] * jnp.concatenate([head, down[SUBLANES:, :]], axis=0)
    tail_scr[...] = u[ML_BLOCK - SUBLANES:ML_BLOCK, :].astype(F32)
    qkc = y * _sigmoid(y)
    q_all = qkc[:, 0:ML_WIDTH].astype(BF16)
    k_all = (qkc[:, ML_WIDTH:2 * ML_WIDTH] * (ML_HD ** -0.5)).astype(BF16)

    ifc = sm_ref[...] + bifc_ref[...]
    ifr = smt_ref[...] + bifr_ref[...]
    lfc = _log_sigmoid(ifc)
    lfr = _log_sigmoid(ifr)
    rr = lax.broadcasted_iota(jnp.int32, (lc, lc), 0)
    cc = lax.broadcasted_iota(jnp.int32, (lc, lc), 1)
    causal = rr >= cc
    tri_l = causal.astype(F32)
    tri_u = (rr <= cc).astype(F32)

    for ci in range(ML_BLOCK // lc):
        lo, hi = ci * lc, (ci + 1) * lc
        bc_all = jnp.dot(tri_l, lfc[lo:hi, :], preferred_element_type=F32, precision=lax.Precision.HIGHEST)
        br_all = jnp.dot(lfr[:, lo:hi], tri_u, preferred_element_type=F32, precision=lax.Precision.HIGHEST)
        heads = range(ML_HEADS)
        hsl = [slice(h * ML_HD, (h + 1) * ML_HD) for h in heads]
        bcol = [bc_all[:, SM_F + h:SM_F + h + 1] for h in heads]
        brow = [br_all[SM_F + h:SM_F + h + 1, :] for h in heads]
        icol = [ifc[lo:hi, SM_I + h:SM_I + h + 1] for h in heads]
        irow = [ifr[SM_I + h:SM_I + h + 1, lo:hi] for h in heads]
        mprev = [m_scr[h][:, 0:1] for h in heads]
        qh = [q_all[lo:hi, hsl[h]] for h in heads]
        kh = [k_all[lo:hi, hsl[h]] for h in heads]
        vh = [v_ref[lo:hi, hsl[h]] for h in heads]
        ct = [ct_scr[h] for h in heads]
        nrow = [n_scr[h] for h in heads]
        qk = [_dot_nt(qh[h], kh[h]) for h in heads]
        qc = [_dot(qh[h], ct[h].astype(BF16)) for h in heads]
        dmat = [jnp.where(causal, bcol[h] - brow[h] + irow[h], NEG) for h in heads]
        inter = [bcol[h] + mprev[h] for h in heads]
        mt = [jnp.maximum(jnp.max(dmat[h], axis=-1, keepdims=True), inter[h]) for h in heads]
        a = [jnp.exp(dmat[h] - mt[h]) * qk[h] for h in heads]
        dec = [jnp.exp(inter[h] - mt[h]) for h in heads]
        num = [_dot(a[h].astype(BF16), vh[h]) + dec[h] * qc[h] for h in heads]
        den = [jnp.sum(a[h], axis=-1, keepdims=True)
               + dec[h] * jnp.sum(qh[h].astype(F32) * nrow[h], axis=-1, keepdims=True) for h in heads]
        blast = [bcol[h][lc - 1:lc, :] for h in heads]
        mnew = [jnp.maximum(blast[h] + mprev[h], jnp.max(blast[h] - brow[h] + irow[h], axis=-1, keepdims=True))
                for h in heads]
        wprev = [jnp.exp(blast[h] + mprev[h] - mnew[h]) for h in heads]
        kwt = [kh[h].astype(F32) * jnp.exp(blast[h] - bcol[h] + icol[h] - mnew[h]) for h in heads]
        for h in heads:
            ct_scr[h] = wprev[h] * ct[h] + lax.dot_general(kwt[h].astype(BF16), vh[h], _TN,
                                                           preferred_element_type=F32)
            n_scr[h] = wprev[h] * nrow[h] + jnp.sum(kwt[h], axis=0, keepdims=True)
            m_scr[h] = jnp.broadcast_to(mnew[h], (1, LANES))
        hm = [num[h] / jnp.maximum(jnp.abs(den[h]), jnp.exp(-mt[h])) * _sigmoid(og_ref[lo:hi, hsl[h]].astype(F32))
              for h in heads]
        for h in heads:
            y_ref[lo:hi, hsl[h]] = _rms(hm[h], ghn_ref[:, hsl[h]]).astype(BF16)


def _mlstm(qkb, vb, ob, sm, smt, wconv, bconv, bifc, bifr, ghn, b, s):
    nb = s // ML_BLOCK
    row = lambda w: pl.BlockSpec((ML_BLOCK, w), lambda bi, j: (bi * nb + j, 0))
    return pl.pallas_call(
        _mlstm_kernel,
        out_shape=jax.ShapeDtypeStruct((b * s, ML_WIDTH), BF16),
        grid=(b, nb),
        in_specs=[row(2 * ML_WIDTH), row(ML_WIDTH), row(ML_WIDTH), row(LANES),
                  pl.BlockSpec((LANES, ML_BLOCK), lambda bi, j: (0, bi * nb + j)),
                  _const_spec(wconv.shape), _const_spec(bconv.shape), _const_spec(bifc.shape),
                  _const_spec(bifr.shape), _const_spec(ghn.shape)],
        out_specs=row(ML_WIDTH),
        scratch_shapes=[pltpu.VMEM((SUBLANES, 2 * ML_WIDTH), F32),
                        pltpu.VMEM((ML_HEADS, ML_HD, ML_HD), F32),
                        pltpu.VMEM((ML_HEADS, 1, ML_HD), F32),
                        pltpu.VMEM((ML_HEADS, 1, LANES), F32)],
        compiler_params=pltpu.CompilerParams(dimension_semantics=("arbitrary", "arbitrary"),
                                             vmem_limit_bytes=VMEM_LIMIT),
        name="mlstm",
    )(qkb, vb, ob, sm, smt, wconv, bconv, bifc, bifr, ghn)


RT_BUCKET = N_EXPERTS
RT_RANK = N_EXPERTS + 1
RT_WLO = N_EXPERTS + 2
RT_WHI = N_EXPERTS + 3
RT_LPOS = N_EXPERTS + 4
N_BUCKETS = N_GROUPS * 6
X_ROWS = D_MODEL // LANES
DMA_UNROLL = 8
RUN_CHUNK = SUBLANES
MAX_CHUNKS = TD // RUN_CHUNK + N_BUCKETS
SORT_ROWS = MAX_CHUNKS * RUN_CHUNK
COMBINE_PARTS = 2


def _merge_kernel(ya_ref, yb_ref, mg_ref, x_ref, wpa_ref, wpb_ref, wout_ref, gffn_ref, wr_ref, br_ref,
                  x1_ref, h2_ref, slab_ref, hist_ref):
    halves = [slice(i * (TD // 2), (i + 1) * (TD // 2)) for i in range(2)]
    pa = [_dot(ya_ref[hs, :], wpa_ref[...]) for hs in halves]
    pb = [_dot(yb_ref[hs, :], wpb_ref[...]) for hs in halves]
    mixed = [(_sigmoid(mg_ref[hs, 0:D_MODEL].astype(F32)) * pa[i]
              + _sigmoid(mg_ref[hs, D_MODEL:2 * D_MODEL].astype(F32)) * pb[i]).astype(BF16)
             for i, hs in enumerate(halves)]
    x1 = [x_ref[hs, :] + _dot(mixed[i], wout_ref[...]) for i, hs in enumerate(halves)]
    h2 = [_rms(x1[i], gffn_ref[...]) for i in range(2)]
    h_hi = [h.astype(BF16) for h in h2]
    for i, hs in enumerate(halves):
        x1_ref[hs, :] = x1[i]
        h2_ref[hs, :] = h_hi[i]

    h_lo = [(h2[i] - h_hi[i].astype(F32)).astype(BF16) for i in range(2)]
    r_hi = [_dot(h, wr_ref[...]) for h in h_hi]
    logit = jnp.concatenate([r_hi[i][:, 0:LANES] + r_hi[i][:, LANES:2 * LANES] + _dot(h_lo[i], wr_ref[:, 0:LANES])
                             for i in range(2)], axis=0) + br_ref[...]
    lane = lax.broadcasted_iota(jnp.int32, logit.shape, 1)
    big = jnp.int32(LANES)
    gmask = (lane >= N_EXPERTS) & (lane < N_EXPERTS + N_GROUPS)
    gl = jnp.where(gmask, logit, NEG)
    gmax = jnp.max(gl, axis=-1, keepdims=True)
    gidx = jnp.min(jnp.where(gmask & (gl == gmax), lane, big), axis=-1, keepdims=True) - N_EXPERTS
    pg_sel = 1.0 / jnp.sum(jnp.where(gmask, jnp.exp(gl - gmax), 0.0), axis=-1, keepdims=True)
    emask = (lane < N_EXPERTS) & (lax.shift_right_logical(lane, 2) == gidx)
    el = jnp.where(emask, logit, NEG)
    e1 = jnp.max(el, axis=-1, keepdims=True)
    i1 = jnp.min(jnp.where(emask & (el == e1), lane, big), axis=-1, keepdims=True)
    emask2 = emask & (lane != i1)
    el2 = jnp.where(emask2, logit, NEG)
    e2 = jnp.max(el2, axis=-1, keepdims=True)
    i2 = jnp.min(jnp.where(emask2 & (el2 == e2), lane, big), axis=-1, keepdims=True)
    x21 = jnp.exp(e2 - e1)
    w1 = pg_sel / (1.0 + x21)
    w2 = pg_sel * x21 / (1.0 + x21)
    first_lo = i1 < i2
    e_lo = jnp.where(first_lo, i1, i2) - EXP_PER_GROUP * gidx
    e_hi = jnp.where(first_lo, i2, i1) - EXP_PER_GROUP * gidx
    pair = lax.shift_right_logical(e_lo * (2 * EXP_PER_GROUP - 1 - e_lo), 1) + (e_hi - e_lo - 1)
    bucket = 6 * gidx + pair
    member = lane == bucket
    onehot = jnp.where(member, 1.0, 0.0)
    rr = lax.broadcasted_iota(jnp.int32, (TD, TD), 0)
    cc = lax.broadcasted_iota(jnp.int32, (TD, TD), 1)
    earlier = _dot((rr > cc).astype(BF16), onehot.astype(BF16))
    rank = jnp.sum(jnp.where(member, earlier, 0.0), axis=-1, keepdims=True)
    slab = jnp.where(lane == i1, w1, 0.0) + jnp.where(lane == i2, w2, 0.0)
    slab = jnp.where(lane == RT_BUCKET, bucket.astype(F32), slab)
    slab = jnp.where(lane == RT_RANK, rank, slab)
    slab = jnp.where(lane == RT_WLO, jnp.where(first_lo, w1, w2), slab)
    slab = jnp.where(lane == RT_WHI, jnp.where(first_lo, w2, w1), slab)
    hist8 = jnp.broadcast_to(jnp.sum(onehot, axis=0, keepdims=True), (SUBLANES, LANES))
    kk = lax.broadcasted_iota(jnp.int32, (LANES, LANES), 0)
    ll = lax.broadcasted_iota(jnp.int32, (LANES, LANES), 1)
    runlen = jnp.ceil(hist8 / RUN_CHUNK) * RUN_CHUNK
    before = sum(_dot(part, (kk < ll).astype(BF16)) for part in _split3(runlen))[0:1, :]
    lpos = jnp.sum(jnp.where(member, before, 0.0), axis=-1, keepdims=True) + rank
    slab = jnp.where(lane == RT_LPOS, lpos, slab)
    slab_ref[...] = slab
    hist_ref[...] = hist8


def _merge(ya, yb, mg, x2, wpa, wpb, wout, gffn, wr, br):
    t = x2.shape[0]
    row = lambda w: pl.BlockSpec((TD, w), lambda i: (i, 0))
    return pl.pallas_call(
        _merge_kernel,
        out_shape=[jax.ShapeDtypeStruct((t, D_MODEL), F32),
                   jax.ShapeDtypeStruct((t, D_MODEL), BF16),
                   jax.ShapeDtypeStruct((t, LANES), F32),
                   jax.ShapeDtypeStruct((t // TD * SUBLANES, LANES), F32)],
        grid=(t // TD,),
        in_specs=[row(NSA_HEADS * NSA_HD), row(ML_WIDTH), row(2 * D_MODEL), row(D_MODEL)]
                 + [_const_spec(a.shape) for a in (wpa, wpb, wout, gffn, wr, br)],
        out_specs=[row(D_MODEL), row(D_MODEL), row(LANES),
                   pl.BlockSpec((SUBLANES, LANES), lambda i: (i, 0))],
        compiler_params=pltpu.CompilerParams(dimension_semantics=("arbitrary",), vmem_limit_bytes=VMEM_LIMIT),
        name="merge",
    )(ya, yb, mg, x2, wpa, wpb, wout, gffn, wr, br)


def _rec_copy(src_ref, src_tok, dst_ref, dst_tok, sem, rows):
    src = src_ref.at[pl.ds(pl.multiple_of(src_tok * rows, rows), rows), :]
    dst = dst_ref.at[pl.ds(pl.multiple_of(dst_tok * rows, rows), rows), :]
    return pltpu.make_async_copy(src, dst, sem)


def _token_copies(n, make, wait=False):
    def body(g, carry):
        for u in range(DMA_UNROLL):
            cp = make(g * DMA_UNROLL + u)
            if wait:
                cp.wait()
            else:
                cp.start(priority=u % 2)
        return carry
    lax.fori_loop(0, n // DMA_UNROLL, body, 0)


def _dispatch_kernel(ch_ref, tail_ref, h2_ref, slab_ref, hx_ref, stage, zero_scr, sem, zsem):
    i = pl.program_id(0)
    n_steps = pl.num_programs(0)
    slot = lax.rem(i, 2)
    n_tiles = hx_ref.shape[0] // TM
    n_used = tail_ref[2 * N_BUCKETS]

    def zero_copies(first_slot, n):
        rows = pl.ds(pl.multiple_of(first_slot, n), n)
        return (pltpu.make_async_copy(zero_scr.at[0:n, :], hx_ref.at[rows, :], zsem),)

    def zero_fill(wait):
        def chunk(b, q, carry):
            for cp in zero_copies(tail_ref[b] + q * RUN_CHUNK, RUN_CHUNK):
                cp.wait() if wait else cp.start()
            return carry

        def tile(k, carry):
            for cp in zero_copies(k * TM, TM):
                cp.wait() if wait else cp.start()
            return carry
        for b in range(N_BUCKETS):
            lax.fori_loop(0, tail_ref[N_BUCKETS + b], functools.partial(chunk, b), 0)
        lax.fori_loop(n_used, n_tiles, tile, 0)

    @pl.when(i == 0)
    def _():
        zero_scr[...] = jnp.zeros_like(zero_scr)
        zero_fill(wait=False)

    slab = slab_ref[...]
    perm = (slab[:, RT_LPOS:RT_LPOS + 1]
            == lax.broadcasted_iota(jnp.int32, (1, SORT_ROWS), 1).astype(F32)).astype(BF16)
    stage[slot, :, 0:D_MODEL] = lax.dot_general(perm, h2_ref[...], _TN, preferred_element_type=F32)
    stage[slot, :, D_MODEL:] = sum(lax.dot_general(perm, part, _TN, preferred_element_type=F32)
                                   for part in _split3(slab))

    def chunk_copies(tile, which, wait):
        src_at = n_steps + tile * MAX_CHUNKS
        dst_at = n_steps + (n_steps + tile) * MAX_CHUNKS

        def body(q, carry):
            src = pl.ds(pl.multiple_of(ch_ref[src_at + q], RUN_CHUNK), RUN_CHUNK)
            dst = pl.ds(pl.multiple_of(ch_ref[dst_at + q], RUN_CHUNK), RUN_CHUNK)
            cp = pltpu.make_async_copy(stage.at[which, src, :], hx_ref.at[dst, :], sem.at[which])
            cp.wait() if wait else cp.start()
            return carry
        lax.fori_loop(0, ch_ref[tile], body, 0)

    chunk_copies(i, slot, wait=False)

    @pl.when(i > 0)
    def _():
        chunk_copies(i - 1, 1 - slot, wait=True)

    @pl.when(i == n_steps - 1)
    def _():
        chunk_copies(i, slot, wait=True)
        zero_fill(wait=True)


def _dispatch(chunks, tail, h2, slab, n_slots):
    t = h2.shape[0]
    return pl.pallas_call(
        _dispatch_kernel,
        out_shape=jax.ShapeDtypeStruct((n_slots, D_MODEL + LANES), F32),
        grid_spec=pltpu.PrefetchScalarGridSpec(
            num_scalar_prefetch=2,
            grid=(t // TD,),
            in_specs=[pl.BlockSpec((TD, D_MODEL), lambda i, ch_r, tail_r: (i, 0)),
                      pl.BlockSpec((TD, LANES), lambda i, ch_r, tail_r: (i, 0))],
            out_specs=pl.BlockSpec(memory_space=pl.ANY),
            scratch_shapes=[pltpu.VMEM((2, SORT_ROWS, D_MODEL + LANES), F32), pltpu.VMEM((TM, D_MODEL + LANES), F32),
                            pltpu.SemaphoreType.DMA((2,)), pltpu.SemaphoreType.DMA(())],
        ),
        compiler_params=pltpu.CompilerParams(dimension_semantics=("arbitrary",), vmem_limit_bytes=VMEM_LIMIT,
                                             has_side_effects=True),
        name="dispatch",
    )(chunks, tail, h2, slab)


MOE_TILES = 2


def _moe_kernel(te_ref, nu_ref, hx_ref, w13_ref, w2_ref, y_ref):
    step = pl.program_id(0)
    n_tiles = pl.num_programs(0) * MOE_TILES
    n_used = nu_ref[0]

    @pl.when(step * MOE_TILES < n_used)
    def _():
        subs = range(MOE_TILES)
        hs = [hx_ref[sub * TM:(sub + 1) * TM, 0:D_MODEL].astype(BF16) for sub in subs]
        slabs = [hx_ref[sub * TM:(sub + 1) * TM, D_MODEL:] for sub in subs]
        ys = [None] * MOE_TILES
        for side, lane in ((0, RT_WLO), (1, RT_WHI)):
            es = [te_ref[side * n_tiles + step * MOE_TILES + sub] for sub in subs]
            up = [_dot(hs[sub], w13_ref[es[sub]]) for sub in subs]
            act = [(up[sub][:, 0:D_EXPERT] * _sigmoid(up[sub][:, 0:D_EXPERT]) * up[sub][:, D_EXPERT:2 * D_EXPERT]
                    * slabs[sub][:, lane:lane + 1]).astype(BF16) for sub in subs]
            for sub in subs:
                part = _dot(act[sub], w2_ref[es[sub]])
                ys[sub] = part if ys[sub] is None else ys[sub] + part
        for sub in subs:
            for j in range(X_ROWS):
                y_ref[pl.ds(sub * TM * X_ROWS + j, TM, stride=X_ROWS), :] = ys[sub][:, j * LANES:(j + 1) * LANES]

    @pl.when(step * MOE_TILES >= n_used)
    def _():
        y_ref[...] = jnp.zeros_like(y_ref)


def _moe(tile_e, n_used, hx_sorted, w13, w2):
    n_tiles = hx_sorted.shape[0] // TM
    rows = MOE_TILES * TM
    last = lambda nu: (nu[0] - 1) // MOE_TILES
    return pl.pallas_call(
        _moe_kernel,
        out_shape=jax.ShapeDtypeStruct((n_tiles * TM * X_ROWS, LANES), F32),
        grid_spec=pltpu.PrefetchScalarGridSpec(
            num_scalar_prefetch=2,
            grid=(n_tiles // MOE_TILES,),
            in_specs=[pl.BlockSpec((rows, D_MODEL + LANES), lambda k, te, nu: (jnp.minimum(k, last(nu)), 0)),
                      pl.BlockSpec(w13.shape, lambda k, te, nu: (0, 0, 0), pipeline_mode=pl.Buffered(1)),
                      pl.BlockSpec(w2.shape, lambda k, te, nu: (0, 0, 0), pipeline_mode=pl.Buffered(1))],
            out_specs=pl.BlockSpec((rows * X_ROWS, LANES), lambda k, te, nu: (k, 0)),
        ),
        compiler_params=pltpu.CompilerParams(dimension_semantics=("arbitrary",), vmem_limit_bytes=VMEM_LIMIT),
        name="moe",
    )(tile_e, n_used, hx_sorted, w13, w2)


def _combine_kernel(pos_ref, y_ref, x1_ref, p_ref, gple_ref, wpg_ref, wpp_ref, gfin_ref, o_ref, ybuf, sem):
    i = pl.program_id(0)
    slot = lax.rem(i, 2)

    last = pl.num_programs(0) - 1

    def drain(which):
        _token_copies(TD, lambda r: _rec_copy(y_ref, 0, ybuf.at[which], 0, sem.at[which], X_ROWS), wait=True)

    @pl.when(i == 0)
    def _():
        _token_copies(TD, lambda r: _rec_copy(y_ref, pos_ref[r], ybuf.at[0], r, sem.at[0], X_ROWS))

    drain(slot)
    nxt = jnp.minimum(i + 1, last) * TD
    yb = ybuf.at[slot]
    part = TD // COMBINE_PARTS
    for q in range(COMBINE_PARTS):
        rows = slice(q * part, (q + 1) * part)
        y = jnp.concatenate([yb[pl.ds(q * part * X_ROWS + j, part, stride=X_ROWS), :] for j in range(X_ROWS)],
                            axis=1)
        x2 = x1_ref[rows, :] + y
        h3 = _rms(x2, gple_ref[...]).astype(BF16)
        x3 = x2 + _sigmoid(_dot(h3, wpg_ref[...])) * _dot(p_ref[rows, :].astype(BF16), wpp_ref[...])
        o_ref[rows, :] = _rms(x3, gfin_ref[...])
        for r in range(q * part, (q + 1) * part):
            _rec_copy(y_ref, pos_ref[nxt + r], ybuf.at[1 - slot], r, sem.at[1 - slot], X_ROWS).start(
                priority=r % 2)

    @pl.when(i == last)
    def _():
        drain(1 - slot)


def _combine(pos, y_sorted, x1, p2, gple, wpg, wpp, gfin):
    t = x1.shape[0]
    row = lambda w: pl.BlockSpec((TD, w), lambda i, pos_r: (i, 0))
    const = lambda a: pl.BlockSpec(a.shape, lambda i, pos_r: (0,) * a.ndim, pipeline_mode=pl.Buffered(1))
    return pl.pallas_call(
        _combine_kernel,
        out_shape=jax.ShapeDtypeStruct((t, D_MODEL), F32),
        grid_spec=pltpu.PrefetchScalarGridSpec(
            num_scalar_prefetch=1,
            grid=(t // TD,),
            in_specs=[pl.BlockSpec(memory_space=pl.ANY), row(D_MODEL), row(PLE_DIM),
                      const(gple), const(wpg), const(wpp), const(gfin)],
            out_specs=row(D_MODEL),
            scratch_shapes=[pltpu.VMEM((2, TD * X_ROWS, LANES), F32), pltpu.SemaphoreType.DMA((2,))],
        ),
        compiler_params=pltpu.CompilerParams(dimension_semantics=("arbitrary",), vmem_limit_bytes=VMEM_LIMIT),
        name="combine",
    )(pos, y_sorted, x1, p2, gple, wpg, wpp, gfin)


def _routing_tables(slab, hist8):
    t = slab.shape[0]
    nt = t // TD
    n_tiles = t // TM + N_BUCKETS + -(-nt * N_BUCKETS * (RUN_CHUNK - 1) // TM)
    n_tiles += n_tiles % MOE_TILES
    hist = hist8.reshape(nt, SUBLANES, LANES)[:, 0, :]
    runlen = jnp.ceil(hist / RUN_CHUNK) * RUN_CHUNK
    counts = jnp.sum(runlen, axis=0)
    padded = jnp.ceil(counts / TM) * TM
    ends = jnp.cumsum(padded)
    first = (ends - padded)[None, :] + jnp.cumsum(runlen, axis=0) - runlen
    lane = jnp.arange(N_BUCKETS, dtype=F32)[None, :]
    mine = lane == slab[:, RT_BUCKET:RT_BUCKET + 1]
    pos = jnp.sum(jnp.where(mine, jnp.repeat(first[:, :N_BUCKETS], TD, axis=0), 0.0), axis=1) + slab[:, RT_RANK]
    starts = jnp.arange(n_tiles, dtype=F32) * TM
    tile_bucket = jnp.minimum(jnp.sum(ends[None, :N_BUCKETS] <= starts[:, None], axis=1), N_BUCKETS - 1)
    group, pair = tile_bucket // 6, tile_bucket % 6
    e_lo = EXP_PER_GROUP * group + jnp.array([0, 0, 0, 1, 1, 2], jnp.int32)[pair]
    e_hi = EXP_PER_GROUP * group + jnp.array([1, 2, 3, 2, 3, 3], jnp.int32)[pair]
    tile_e = jnp.concatenate([e_lo, e_hi]).astype(jnp.int32)
    n_used = (ends[N_BUCKETS - 1] / TM).astype(jnp.int32).reshape(1)
    tail = jnp.concatenate([(ends - padded + counts)[:N_BUCKETS], ((padded - counts) / RUN_CHUNK)[:N_BUCKETS],
                            n_used.astype(F32)]).astype(jnp.int32)
    nb = N_BUCKETS
    nch = runlen[:, :nb] / RUN_CHUNK
    cum = jnp.cumsum(nch, axis=1)
    q = jnp.arange(MAX_CHUNKS, dtype=F32)[None, :, None]
    of_b = jnp.arange(nb)[None, None, :] == jnp.minimum(jnp.sum(cum[:, None, :] <= q, axis=2), nb - 1)[..., None]
    pick = lambda a: jnp.sum(jnp.where(of_b, a[:, None, :nb], 0.0), axis=2)
    within = RUN_CHUNK * (q[..., 0] - pick(cum - nch))
    local = jnp.cumsum(runlen, axis=1) - runlen
    chunks = jnp.concatenate([cum[:, nb - 1], (pick(local) + within).reshape(-1),
                              (pick(first) + within).reshape(-1)]).astype(jnp.int32)
    return pos.astype(jnp.int32), chunks, tile_e, n_used, tail, n_tiles * TM


def _pack_inproj_weights(w):
    d = w.shape[0]
    qw = NSA_HEADS * NSA_HD
    kvw = NSA_KV * NSA_HD
    o = 0
    wq = w[:, o:o + qw]; o += qw
    wkc = w[:, o:o + kvw]; o += kvw
    wvc = w[:, o:o + kvw]; o += kvw
    wks = w[:, o:o + kvw]; o += kvw
    wvs = w[:, o:o + kvw]; o += kvw
    wkw = w[:, o:o + kvw]; o += kvw
    wvw = w[:, o:o + kvw]; o += kvw
    wga = w[:, o:o + 3 * NSA_HEADS]; o += 3 * NSA_HEADS
    wqkb = w[:, o:o + 2 * ML_WIDTH]; o += 2 * ML_WIDTH
    wvb = w[:, o:o + ML_WIDTH]; o += ML_WIDTH
    wob = w[:, o:o + ML_WIDTH]; o += ML_WIDTH
    wif = w[:, o:o + 2 * ML_HEADS]; o += 2 * ML_HEADS
    wmg = w[:, o:o + 2 * D_MODEL]
    wsm = jnp.concatenate([wga, wif, jnp.zeros((d, LANES - 3 * NSA_HEADS - 2 * ML_HEADS), w.dtype)], axis=1)
    wcat = jnp.concatenate([wq, wkc, wvc, wks, wkw, wsm, wqkb, wvb, wob, wmg], axis=1).astype(BF16)
    wtr = jnp.concatenate([wvs, wvw, wsm], axis=1).T.astype(BF16)
    return wcat, wtr


def _rope_tables(positions):
    half = ROPE_DIM // 2
    inv = ROPE_THETA ** (-jnp.arange(0, ROPE_DIM, 2, dtype=F32) / ROPE_DIM)
    ang = positions.astype(F32).reshape(-1, 1) * inv[None, :]
    cs = jnp.concatenate([jnp.cos(ang), jnp.sin(ang)], axis=1)
    d = np.arange(LANES) % NSA_HD
    spread = np.zeros((2 * half, 3 * LANES), np.float32)
    lanes = np.arange(LANES)
    rot = d < ROPE_DIM
    spread[(d % half)[rot], lanes[rot]] = 1.0
    hi = (d >= half) & rot
    spread[half + (d % half)[hi], LANES + lanes[hi]] = 1.0
    lo = d < half
    spread[half + (d % half)[lo], 2 * LANES + lanes[lo]] = -1.0
    unrotated = (~rot).astype(np.float32).reshape(1, LANES)
    return cs, jnp.asarray(spread, BF16), jnp.asarray(unrotated)


def _pack_compress_weights(w1, w2, pe):
    half = CMP_LEN // 2
    w1r = w1.reshape(2, half, NSA_HD, CMP_HIDDEN)
    outs = []
    for part in range(2):
        wb = w1r[part].astype(BF16)
        zb = jnp.zeros_like(wb)
        wp = jnp.stack([jnp.stack([wb, zb], axis=2), jnp.stack([zb, wb], axis=2)], axis=1)
        outs.append(wp.reshape(half * NSA_KV * NSA_HD, NSA_KV * CMP_HIDDEN))
    pe8 = jnp.broadcast_to(pe.reshape(1, CMP_LEN * NSA_HD), (SUBLANES, CMP_LEN * NSA_HD)).astype(BF16)
    return outs[0], outs[1], pe8, w1.astype(BF16)


def _stages(x, p, positions, g_mix, w_in, b_if, w_ck1, w_ck2, pe_ck, w_cv1, w_cv2, pe_cv, w_conv, b_conv, g_hn, w_pa, w_pb, w_out, g_ffn, w_rg, b_rg, w_re, b_re, w_e13, w_e2, g_ple, w_pg, w_pp, g_final):
    b, s, d = x.shape
    t = b * s
    cs, spread, unrot = _rope_tables(positions)
    assert w_in.shape[0] == 1, "the final norm is fused into the layer's last kernel: single-layer problem only"
    for i in range(w_in.shape[0]):
        x2 = x.reshape(t, d)
        wcat, wtr = _pack_inproj_weights(w_in[i])
        (qpad, kc_tok, vc_tok, ks, kw, vst, vwt, sm, smt, qkb, vb, ob, mg) = _inproj(
            x2, g_mix[i].reshape(1, d), wcat, wtr, cs, spread, unrot)
        wka, wkb, pek, w1k = _pack_compress_weights(w_ck1[i], w_ck2[i], pe_ck[i])
        wva, wvb, pev, w1v = _pack_compress_weights(w_cv1[i], w_cv2[i], pe_cv[i])
        zpad = jnp.zeros((CMP_HIDDEN, NSA_HD), F32)
        w2k = jnp.stack([jnp.concatenate([w_ck2[i], zpad], axis=1),
                         jnp.concatenate([zpad, w_ck2[i]], axis=1)]).astype(BF16)
        w2vt = w_cv2[i].T.astype(BF16)
        nrow = s // CMP_STRIDE
        rk = kc_tok.reshape(b, nrow, CMP_STRIDE * LANES)
        rv = vc_tok.reshape(b, nrow, CMP_STRIDE * LANES)
        kcb, vct = _compress(rk, rv, wka, wkb, wva, wvb, pek, pev, w1k, w1v, w2k, w2vt)
        ya = _nsa(qpad, kcb, vct, ks, kw, vst, vwt, smt, b, s)
        bif = b_if[i].astype(F32)
        bifc = jnp.zeros((1, LANES), F32).at[0, SM_I:SM_I + 2 * ML_HEADS].set(bif)
        bifr = bifc.reshape(LANES, 1)
        yb = _mlstm(qkb, vb, ob, sm, smt, w_conv[i], b_conv[i].reshape(1, -1), bifc, bifr,
                    g_hn[i].reshape(1, -1), b, s)
        wr = jnp.concatenate([w_re[i], w_rg[i], jnp.zeros((d, LANES - N_EXPERTS - N_GROUPS), F32)], axis=1)
        wr_hi = wr.astype(BF16)
        wr = jnp.concatenate([wr_hi, (wr - wr_hi.astype(F32)).astype(BF16)], axis=1)
        br =jnp.concatenate([b_re[i], b_rg[i], jnp.zeros((LANES - N_EXPERTS - N_GROUPS,), F32)]).reshape(1, LANES)
        x1, h2, slab, hist8 = _merge(ya, yb, mg, x2, w_pa[i].astype(BF16), w_pb[i].astype(BF16),
                                     w_out[i].astype(BF16), g_ffn[i].reshape(1, d), wr, br)
        pos, chunks, tile_e, n_used, tail, n_slots = _routing_tables(slab, hist8)
        hx_sorted = _dispatch(chunks, tail, h2, slab, n_slots)
        y_sorted = _moe(tile_e, n_used, hx_sorted, w_e13[i].astype(BF16), w_e2[i].astype(BF16))
        out = _combine(pos, y_sorted, x1, p[i].reshape(t, PLE_DIM), g_ple[i].reshape(1, d), w_pg[i].astype(BF16),
                       w_pp[i].astype(BF16), g_final.reshape(1, d))
        x = out.reshape(b, s, d)
    return dict(out=x, qpad=qpad, ks=ks, kcb=kcb, vct=vct, y_a=ya, y_b=yb, x1=x1, pos=pos)


def kernel(x, p, positions, g_mix, w_in, b_if, w_ck1, w_ck2, pe_ck, w_cv1, w_cv2, pe_cv, w_conv, b_conv, g_hn, w_pa, w_pb, w_out, g_ffn, w_rg, b_rg, w_re, b_re, w_e13, w_e2, g_ple, w_pg, w_pp, g_final):
    return _stages(x, p, positions, g_mix, w_in, b_if, w_ck1, w_ck2, pe_ck, w_cv1, w_cv2, pe_cv, w_conv, b_conv, g_hn,
                   w_pa, w_pb, w_out, g_ffn, w_rg, b_rg, w_re, b_re, w_e13, w_e2, g_ple, w_pg, w_pp, g_final)["out"]
```

```python
import functools
import math

import numpy as np
import jax
import jax.numpy as jnp
from jax import lax
from jax.experimental import pallas as pl
from jax.experimental.pallas import tpu as pltpu

F32 = jnp.float32
BF16 = jnp.bfloat16

EPS = 1e-6
NEG = -1e30

D_MODEL = 1024
PLE_DIM = 256
NSA_HEADS = 8
NSA_KV = 2
NSA_HPG = NSA_HEADS // NSA_KV
NSA_HD = 64
CMP_LEN = 32
CMP_STRIDE = 16
CMP_HIDDEN = 256
SEL_BLOCK = 64
SEL_TOPK = 16
SEL_FORCE = 1000.0
WINDOW = 512
ROPE_THETA = 500000.0
ROPE_DIM = NSA_HD // 4
ML_HEADS = 4
ML_HD = 128
ML_WIDTH = ML_HEADS * ML_HD
CONV_W = 4
N_GROUPS = 4
EXP_PER_GROUP = 4
N_EXPERTS = N_GROUPS * EXP_PER_GROUP
D_EXPERT = 256

LANES = 128
SUBLANES = 8
QT = 128
KC = 128
SEL_GROUP = 512
VT_PAD = 16
VT_ROWS = NSA_HD + VT_PAD
ML_CHUNK = 128
ML_BLOCK = 256
TD = 512
TM = 256
VMEM_LIMIT = 56 * 1024 * 1024

_NT = (((1,), (1,)), ((), ()))
_TN = (((0,), (0,)), ((), ()))

SM_GATE = 0
SM_I = 3 * NSA_HEADS
SM_F = SM_I + ML_HEADS


def _dot(a, b):
    return jnp.dot(a, b, preferred_element_type=F32)


def _dot_nt(a, b):
    return lax.dot_general(a, b, _NT, preferred_element_type=F32)


def _split3(x):
    hi = x.astype(BF16)
    r1 = x - hi.astype(F32)
    mid = r1.astype(BF16)
    lo = (r1 - mid.astype(F32)).astype(BF16)
    return hi, mid, lo


def _rms(x, g):
    return x * lax.rsqrt(jnp.mean(x * x, axis=-1, keepdims=True) + EPS) * g


def _sigmoid(x):
    return 0.5 + 0.5 * jnp.tanh(0.5 * x)


def _const_spec(shape):
    nd = len(shape)
    return pl.BlockSpec(shape, lambda *_: (0,) * nd, pipeline_mode=pl.Buffered(1))


_C_Q = 0
_C_KC = _C_Q + NSA_HEADS * NSA_HD
_C_VC = _C_KC + LANES
_C_KS = _C_VC + LANES
_C_KW = _C_KS + LANES
_C_SM = _C_KW + LANES
_C_QKB = _C_SM + LANES
_C_VB = _C_QKB + 2 * ML_WIDTH
_C_OB = _C_VB + ML_WIDTH
_C_MG = _C_OB + ML_WIDTH
_C_END = _C_MG + 2 * D_MODEL


def _inproj_kernel(x_ref, g_ref, w_ref, wt_ref, cs_ref, spread_ref, unrot_ref,
                   q_ref, kc_ref, vc_ref, ks_ref, kw_ref, vst_ref, vwt_ref, sm_ref, smt_ref,
                   qkb_ref, vb_ref, ob_ref, mg_ref):
    hn = _rms(x_ref[...], g_ref[...]).astype(BF16)
    tables = sum(_dot(part, spread_ref[...]) for part in _split3(cs_ref[...]))
    rc = tables[:, 0:LANES] + unrot_ref[...]
    rp = tables[:, LANES:2 * LANES]
    rm = tables[:, 2 * LANES:3 * LANES]

    def rope(z):
        half = ROPE_DIM // 2
        return z * rc + pltpu.roll(z, half, 1) * rp + pltpu.roll(z, LANES - half, 1) * rm

    scale = NSA_HD ** -0.5 * math.log2(math.e)
    for h in range(NSA_HEADS * NSA_HD // LANES):
        z = _dot(hn, w_ref[:, _C_Q + h * LANES:_C_Q + (h + 1) * LANES])
        q_ref[:, h * LANES:(h + 1) * LANES] = (rope(z) * scale).astype(BF16)
    kc_ref[...] = rope(_dot(hn, w_ref[:, _C_KC:_C_KC + LANES])).astype(BF16)
    vc_ref[...] = _dot(hn, w_ref[:, _C_VC:_C_VC + LANES]).astype(BF16)
    ks_ref[...] = rope(_dot(hn, w_ref[:, _C_KS:_C_KS + LANES])).astype(BF16)
    kw_ref[...] = rope(_dot(hn, w_ref[:, _C_KW:_C_KW + LANES])).astype(BF16)
    sm_ref[...] = _dot(hn, w_ref[:, _C_SM:_C_SM + LANES])
    for c0 in range(0, 2 * ML_WIDTH, 512):
        qkb_ref[:, c0:c0 + 512] = _dot(hn, w_ref[:, _C_QKB + c0:_C_QKB + c0 + 512]).astype(BF16)
    vb_ref[...] = _dot(hn, w_ref[:, _C_VB:_C_VB + ML_WIDTH]).astype(BF16)
    ob_ref[...] = _dot(hn, w_ref[:, _C_OB:_C_OB + ML_WIDTH]).astype(BF16)
    for c0 in range(0, 2 * D_MODEL, 512):
        mg_ref[:, c0:c0 + 512] = _dot(hn, w_ref[:, _C_MG + c0:_C_MG + c0 + 512]).astype(BF16)
    zt = _dot_nt(wt_ref[...], hn)
    ones_rows = (lax.broadcasted_iota(jnp.int32, (VT_PAD, KC), 0) == 0).astype(BF16)
    for i in range(TD // KC):
        for ref, r0 in ((vst_ref, 0), (vwt_ref, LANES)):
            zc = zt[r0:r0 + LANES, i * KC:(i + 1) * KC].astype(BF16)
            ref[i] = jnp.concatenate([piece for g in range(NSA_KV)
                                      for piece in (zc[g * NSA_HD:(g + 1) * NSA_HD, :], ones_rows)], axis=0)
    smt_ref[...] = zt[2 * LANES:3 * LANES, :]


def _inproj(x2, g_mix, wcat, wtr, cs, spread, unrot):
    t = x2.shape[0]
    row = lambda w: pl.BlockSpec((TD, w), lambda i: (i, 0))
    out_shape = [
        jax.ShapeDtypeStruct((t, NSA_HEADS * NSA_HD), BF16),
        jax.ShapeDtypeStruct((t, LANES), BF16),
        jax.ShapeDtypeStruct((t, LANES), BF16),
        jax.ShapeDtypeStruct((t, LANES), BF16),
        jax.ShapeDtypeStruct((t, LANES), BF16),
        jax.ShapeDtypeStruct((t // KC, NSA_KV * VT_ROWS, KC), BF16),
        jax.ShapeDtypeStruct((t // KC, NSA_KV * VT_ROWS, KC), BF16),
        jax.ShapeDtypeStruct((t, LANES), F32),
        jax.ShapeDtypeStruct((LANES, t), F32),
        jax.ShapeDtypeStruct((t, 2 * ML_WIDTH), BF16),
        jax.ShapeDtypeStruct((t, ML_WIDTH), BF16),
        jax.ShapeDtypeStruct((t, ML_WIDTH), BF16),
        jax.ShapeDtypeStruct((t, 2 * D_MODEL), BF16),
    ]
    chunk3 = pl.BlockSpec((TD // KC, NSA_KV * VT_ROWS, KC), lambda i: (i, 0, 0))
    out_specs = [row(NSA_HEADS * NSA_HD), row(LANES), row(LANES), row(LANES), row(LANES), chunk3, chunk3,
                 row(LANES), pl.BlockSpec((LANES, TD), lambda i: (0, i)),
                 row(2 * ML_WIDTH), row(ML_WIDTH), row(ML_WIDTH), row(2 * D_MODEL)]
    return pl.pallas_call(
        _inproj_kernel,
        out_shape=out_shape,
        grid=(t // TD,),
        in_specs=[row(D_MODEL), _const_spec((1, D_MODEL)), _const_spec((D_MODEL, _C_END)),
                  _const_spec((3 * LANES, D_MODEL)), row(cs.shape[1]), _const_spec(spread.shape),
                  _const_spec(unrot.shape)],
        out_specs=out_specs,
        compiler_params=pltpu.CompilerParams(dimension_semantics=("arbitrary",), vmem_limit_bytes=VMEM_LIMIT),
        name="inproj",
    )(x2, g_mix, wcat, wtr, cs, spread, unrot)


def _gelu_tanh(x):
    return 0.5 * x * (1.0 + jnp.tanh(math.sqrt(2.0 / math.pi) * (x + 0.044715 * x * x * x)))


def _compress_kernel(rk_ref, rv_ref, wka_ref, wkb_ref, wva_ref, wvb_ref, pek_ref, pev_ref,
                     w1k_ref, w1v_ref, w2k_ref, w2vt_ref, kc_ref, vct_ref):
    nrow = rk_ref.shape[0]

    def hidden(r_ref, wa_ref, wb_ref, pe_ref, w1_ref):
        r = r_ref[...]
        ha = _dot(r, wa_ref[...])
        hb = _dot(r, wb_ref[...])
        hb = pltpu.roll(hb, nrow - 1, 0)
        c = _dot(pe_ref[...], w1_ref[...])[0:1, :]
        return [_gelu_tanh(ha[:, g * CMP_HIDDEN:(g + 1) * CMP_HIDDEN] + hb[:, g * CMP_HIDDEN:(g + 1) * CMP_HIDDEN] + c).astype(BF16)
                for g in range(NSA_KV)]

    ak = hidden(rk_ref, wka_ref, wkb_ref, pek_ref, w1k_ref)
    kc_ref[...] = (_dot(ak[0], w2k_ref[0]) + _dot(ak[1], w2k_ref[1])).astype(BF16)
    av = hidden(rv_ref, wva_ref, wvb_ref, pev_ref, w1v_ref)
    for g in range(NSA_KV):
        vct_ref[g * NSA_HD:(g + 1) * NSA_HD, :] = _dot_nt(w2vt_ref[...], av[g]).astype(BF16)


def _compress(rk, rv, wka, wkb, wva, wvb, pek, pev, w1k, w1v, w2k, w2vt):
    b, nrow, width = rk.shape
    blk = pl.BlockSpec((None, nrow, width), lambda i: (i, 0, 0))
    return pl.pallas_call(
        _compress_kernel,
        out_shape=[jax.ShapeDtypeStruct((b, nrow, LANES), BF16),
                   jax.ShapeDtypeStruct((b, LANES, nrow), BF16)],
        grid=(b,),
        in_specs=[blk, blk] + [_const_spec(a.shape) for a in (wka, wkb, wva, wvb, pek, pev, w1k, w1v, w2k, w2vt)],
        out_specs=[pl.BlockSpec((None, nrow, LANES), lambda i: (i, 0, 0)),
                   pl.BlockSpec((None, LANES, nrow), lambda i: (i, 0, 0))],
        compiler_params=pltpu.CompilerParams(dimension_semantics=("arbitrary",), vmem_limit_bytes=VMEM_LIMIT),
        name="compress",
    )(rk, rv, wka, wkb, wva, wvb, pek, pev, w1k, w1v, w2k, w2vt)


def _nsa_kernel(q_ref, kc_ref, vct_ref, ks_ref, kw_ref, vst_ref, vwt_ref, smt_ref, o_ref, bias_scr, sx_scr, sy_scr):
    c = pl.program_id(1)
    t0 = c * QT
    ncmp = kc_ref.shape[0]
    nsel = bias_scr.shape[0]
    nw = WINDOW // KC + 1
    gw = NSA_HPG * QT
    width = NSA_KV * gw

    def per_group(x):
        return [x[:, g * gw:(g + 1) * gw] for g in range(NSA_KV)]

    def pv(vt, p):
        rows = vt.shape[0] // NSA_KV
        pb = p.astype(BF16)
        return jnp.concatenate([_dot(vt[g * rows:(g + 1) * rows, :], pg) for g, pg in enumerate(per_group(pb))],
                               axis=1)

    def normalised(acc):
        return acc[0:NSA_HD, :] / acc[NSA_HD:NSA_HD + 1, :]

    low_half = lax.broadcasted_iota(jnp.int32, (1, LANES), 1) < NSA_HD
    q_heads = []
    for h in range(NSA_HEADS):
        pair = q_ref[:, (h // 2) * LANES:(h // 2 + 1) * LANES].astype(F32)
        want_low = h // NSA_HPG == 0
        if (h % 2 == 0) != want_low:
            pair = pltpu.roll(pair, NSA_HD, 1)
        q_heads.append(jnp.where(low_half if want_low else ~low_half, pair, 0.0).astype(BF16))
    qs = jnp.concatenate(q_heads, axis=0)
    u_row = lax.broadcasted_iota(jnp.int32, (1, width), 1) % QT
    t_row = t0 + u_row
    r_kc = lax.broadcasted_iota(jnp.int32, (KC, 1), 0)

    n_grp = ks_ref.shape[0] // SEL_GROUP
    n_full = lax.shift_right_logical(t0, int(math.log2(SEL_GROUP)))

    def qk_group(j):
        return _dot_nt(ks_ref[pl.ds(pl.multiple_of(j * SEL_GROUP, SEL_GROUP), SEL_GROUP), :], qs)

    sc = _dot_nt(kc_ref[...], qs)

    w_slabs, w_chunks = [], []
    for i in range(nw):
        jj = c - (nw - 1) + i
        jc = jnp.maximum(jj, 0)
        si = _dot_nt(kw_ref[pl.ds(pl.multiple_of(jc * KC, KC), KC), :], qs)
        if i == 0:
            keep = jnp.where(jj >= 0, r_kc, -1) > u_row
        elif i == nw - 1:
            keep = r_kc <= u_row
        else:
            keep = jj >= 0
        w_slabs.append(jnp.where(keep, si, NEG))
        w_chunks.append(jc)

    n_col = lax.broadcasted_iota(jnp.int32, (ncmp, 1), 0)
    last_tok = jnp.where(n_col < ncmp - 1, CMP_STRIDE * n_col + (CMP_LEN - 1), jnp.iinfo(jnp.int32).max)
    s = jnp.where(last_tok <= t_row, sc, NEG)
    m = jnp.max(s, axis=0, keepdims=True)
    e = jnp.exp2(s - m)
    anyv = (t_row >= CMP_LEN - 1).astype(F32)
    p = e * (anyv / jnp.sum(e, axis=0, keepdims=True))
    o_cmp = pv(vct_ref[...], p)

    psums = []
    for pg in per_group(p):
        acc_p = pg[:, 0:QT]
        for h in range(1, NSA_HPG):
            acc_p = acc_p + pg[:, h * QT:(h + 1) * QT]
        psums.append(acc_p)
    psum = jnp.concatenate(psums, axis=1)
    nq2 = NSA_KV * QT
    s_col = lax.broadcasted_iota(jnp.int32, (nsel, 1), 0)
    n_lane = lax.broadcasted_iota(jnp.int32, (1, ncmp), 1)
    ov = ((CMP_STRIDE * n_lane < SEL_BLOCK * (s_col + 1)) & (CMP_STRIDE * n_lane + (CMP_LEN - 1) >= SEL_BLOCK * s_col)
          ).astype(BF16)
    imp = sum(_dot(ov, part) for part in _split3(psum))

    s_diag = qk_group(n_full)
    s_own = _dot_nt(ks_ref[pl.ds(pl.multiple_of(t0, KC), KC), :], qs)
    sx_scr[...] = qk_group(0)

    mxw = w_slabs[0]
    for sl in w_slabs[1:]:
        mxw = jnp.maximum(mxw, sl)
    mw = jnp.max(mxw, axis=0, keepdims=True)
    acc_w = jnp.zeros((VT_ROWS, width), F32)
    for sl, jc in zip(w_slabs, w_chunks):
        acc_w = acc_w + pv(vwt_ref[jc], jnp.exp2(sl - mw))
    o_win = normalised(acc_w)

    t1 = t0 + lax.broadcasted_iota(jnp.int32, (1, nq2), 1) % QT
    cur = lax.shift_right_logical(t1, 6)
    forced = (s_col == 0) | (s_col == cur) | (s_col == cur - 1)
    valid = SEL_BLOCK * s_col <= t1
    val = jnp.where(valid, jnp.where(forced, imp + SEL_FORCE, imp), NEG)
    sub = 8
    r_sub = lax.broadcasted_iota(jnp.int32, (sub, 1), 0)
    blocks = [val[r * sub:(r + 1) * sub, :] for r in range(nsel // sub)]
    ranks = [jnp.zeros((sub, nq2), F32) for _ in blocks]
    for i in range(nsel):
        vi = val[i:i + 1, :]
        for r, blk in enumerate(blocks):
            if i < r * sub:
                beats = vi >= blk
            elif i >= (r + 1) * sub:
                beats = vi > blk
            else:
                beats = (vi > blk) | ((vi == blk) & (r_sub > i - r * sub))
            ranks[r] = ranks[r] + jnp.where(beats, 1.0, 0.0)
    bias = jnp.where(jnp.concatenate(ranks, axis=0) < SEL_TOPK, 0.0, NEG).astype(F32)
    bias_scr[...] = jnp.concatenate([bias[:, g * QT:(g + 1) * QT] for g in range(NSA_KV) for _ in range(NSA_HPG)],
                                    axis=1)

    blk_per_grp = SEL_GROUP // SEL_BLOCK
    chunk_per_grp = SEL_GROUP // KC
    blk_per_chunk = KC // SEL_BLOCK

    def sel_update(j, sj, carry):
        m_o, acc = carry
        brows = [bias_scr[pl.ds(blk_per_grp * j + i, 1), :] for i in range(blk_per_grp)]

        def block(i):
            return sj[i * SEL_BLOCK:(i + 1) * SEL_BLOCK, :]

        mx = None
        for i in range(blk_per_grp):
            sl = block(i) + brows[i]
            mx = sl if mx is None else jnp.maximum(mx, sl)
        m_n = jnp.maximum(m_o, jnp.max(mx, axis=0, keepdims=True))
        a = jnp.exp2(m_o - m_n)
        acc = a * acc
        for ci in range(chunk_per_grp):
            parts = [jnp.exp2(block(i) + (brows[i] - m_n))
                     for i in range(blk_per_chunk * ci, blk_per_chunk * (ci + 1))]
            acc = acc + pv(vst_ref[chunk_per_grp * j + ci], jnp.concatenate(parts, axis=0))
        return m_n, acc

    def seed_state():
        first_own = blk_per_chunk * (c % chunk_per_grp)
        brows = [bias_scr[pl.ds(blk_per_grp * n_full + i, 1), :] + jnp.where(i < first_own, 0.0, NEG)
                 for i in range(blk_per_grp)]
        own = jnp.where(r_kc <= u_row, s_own, NEG)
        mx = jnp.maximum(own[0:SEL_BLOCK, :], own[SEL_BLOCK:KC, :])
        for i in range(blk_per_grp):
            mx = jnp.maximum(mx, s_diag[i * SEL_BLOCK:(i + 1) * SEL_BLOCK, :] + brows[i])
        m_n = jnp.max(mx, axis=0, keepdims=True)
        acc = pv(vst_ref[c], jnp.exp2(own - m_n))
        for ci in range(chunk_per_grp):
            parts = [jnp.exp2(s_diag[i * SEL_BLOCK:(i + 1) * SEL_BLOCK, :] + (brows[i] - m_n))
                     for i in range(blk_per_chunk * ci, blk_per_chunk * (ci + 1))]
            acc = acc + pv(vst_ref[chunk_per_grp * n_full + ci], jnp.concatenate(parts, axis=0))
        return m_n, acc

    seeded = seed_state()

    def pair_body(jp, carry):
        ja, jb = 2 * jp, 2 * jp + 1
        sy_scr[...] = qk_group(jb)
        carry = sel_update(ja, sx_scr, carry)
        sx_scr[...] = qk_group(jnp.minimum(ja + 2, n_grp - 1))
        return sel_update(jb, sy_scr, carry)

    n_pairs = lax.shift_right_logical(n_full, 1)
    carry = lax.fori_loop(0, n_pairs, pair_body, seeded)
    _, acc_s = lax.cond(n_full - 2 * n_pairs == 1, lambda cr: sel_update(n_full - 1, sx_scr, cr), lambda cr: cr,
                        carry)
    o_sel = normalised(acc_s)

    def gate_row(br):
        rows = [smt_ref[SM_GATE + 3 * h + br:SM_GATE + 3 * h + br + 1, :] for h in range(NSA_HEADS)]
        return _sigmoid(jnp.concatenate(rows, axis=1))

    o_t = gate_row(0) * o_cmp + gate_row(1) * o_sel + gate_row(2) * o_win
    for pr in range(NSA_HEADS // 2):
        xp = jnp.concatenate([o_t[:, (2 * pr) * QT:(2 * pr + 1) * QT], o_t[:, (2 * pr + 1) * QT:(2 * pr + 2) * QT]], axis=0)
        o_ref[:, pr * LANES:(pr + 1) * LANES] = xp.T.astype(BF16)


def _nsa(qpad, kcb, vct, ks, kw, vst, vwt, smt, b, s):
    nq = s // QT
    ncmp = kcb.shape[1]
    return pl.pallas_call(
        _nsa_kernel,
        out_shape=jax.ShapeDtypeStruct((b * s, NSA_HEADS * NSA_HD), BF16),
        grid=(b, nq),
        in_specs=[
            pl.BlockSpec((QT, NSA_HEADS * NSA_HD), lambda bi, c: (bi * nq + c, 0)),
            pl.BlockSpec((None, ncmp, LANES), lambda bi, c: (bi, 0, 0)),
            pl.BlockSpec((None, NSA_KV * NSA_HD, ncmp), lambda bi, c: (bi, 0, 0)),
            pl.BlockSpec((s, LANES), lambda bi, c: (bi, 0)),
            pl.BlockSpec((s, LANES), lambda bi, c: (bi, 0)),
            pl.BlockSpec((s // KC, NSA_KV * VT_ROWS, KC), lambda bi, c: (bi, 0, 0)),
            pl.BlockSpec((s // KC, NSA_KV * VT_ROWS, KC), lambda bi, c: (bi, 0, 0)),
            pl.BlockSpec((LANES, QT), lambda bi, c: (0, bi * nq + c)),
        ],
        out_specs=pl.BlockSpec((QT, NSA_HEADS * NSA_HD), lambda bi, c: (bi * nq + c, 0)),
        scratch_shapes=[pltpu.VMEM((s // SEL_BLOCK, NSA_HEADS * QT), F32),
                        pltpu.VMEM((SEL_GROUP, NSA_HEADS * QT), F32),
                        pltpu.VMEM((SEL_GROUP, NSA_HEADS * QT), F32)],
        compiler_params=pltpu.CompilerParams(dimension_semantics=("arbitrary", "arbitrary"),
                                             vmem_limit_bytes=VMEM_LIMIT),
        name="nsa",
    )(qpad, kcb, vct, ks, kw, vst, vwt, smt)


def _log_sigmoid(x):
    return jnp.minimum(x, 0.0) - jnp.log(1.0 + jnp.exp(-jnp.abs(x)))


def _mlstm_kernel(qk_ref, v_ref, og_ref, sm_ref, smt_ref, wc_ref, bc_ref, bifc_ref, bifr_ref, ghn_ref,
                  y_ref, tail_scr, ct_scr, n_scr, m_scr):
    lc = ML_CHUNK

    @pl.when(pl.program_id(1) == 0)
    def _():
        tail_scr[...] = jnp.zeros_like(tail_scr)
        ct_scr[...] = jnp.zeros_like(ct_scr)
        n_scr[...] = jnp.zeros_like(n_scr)
        m_scr[...] = jnp.zeros_like(m_scr)

    u = qk_ref[...]
    tail = tail_scr[...]
    rr8 = lax.broadcasted_iota(jnp.int32, (SUBLANES, 1), 0)
    sr = lax.broadcasted_iota(jnp.int32, (ML_BLOCK, ML_BLOCK), 0)
    sc_ = lax.broadcasted_iota(jnp.int32, (ML_BLOCK, ML_BLOCK), 1)
    y = bc_ref[...] + wc_ref[CONV_W - 1:CONV_W, :] * u.astype(F32)
    for k in range(1, CONV_W):
        down = _dot((sr - sc_ == k).astype(BF16), u)
        head = jnp.where(rr8 < k, pltpu.roll(tail, k, 0), down[0:SUBLANES, :])
        y = y + wc_ref[CONV_W - 1 - k:CONV_W - k, :] * jnp.concatenate([head, down[SUBLANES:, :]], axis=0)
    tail_scr[...] = u[ML_BLOCK - SUBLANES:ML_BLOCK, :].astype(F32)
    qkc = y * _sigmoid(y)
    q_all = qkc[:, 0:ML_WIDTH].astype(BF16)
    k_all = (qkc[:, ML_WIDTH:2 * ML_WIDTH] * (ML_HD ** -0.5)).astype(BF16)

    ifc = sm_ref[...] + bifc_ref[...]
    ifr = smt_ref[...] + bifr_ref[...]
    lfc = _log_sigmoid(ifc)
    lfr = _log_sigmoid(ifr)
    rr = lax.broadcasted_iota(jnp.int32, (lc, lc), 0)
    cc = lax.broadcasted_iota(jnp.int32, (lc, lc), 1)
    causal = rr >= cc
    tri_l = causal.astype(F32)
    tri_u = (rr <= cc).astype(F32)

    for ci in range(ML_BLOCK // lc):
        lo, hi = ci * lc, (ci + 1) * lc
        bc_all = jnp.dot(tri_l, lfc[lo:hi, :], preferred_element_type=F32, precision=lax.Precision.HIGHEST)
        br_all = jnp.dot(lfr[:, lo:hi], tri_u, preferred_element_type=F32, precision=lax.Precision.HIGHEST)
        heads = range(ML_HEADS)
        hsl = [slice(h * ML_HD, (h + 1) * ML_HD) for h in heads]
        bcol = [bc_all[:, SM_F + h:SM_F + h + 1] for h in heads]
        brow = [br_all[SM_F + h:SM_F + h + 1, :] for h in heads]
        icol = [ifc[lo:hi, SM_I + h:SM_I + h + 1] for h in heads]
        irow = [ifr[SM_I + h:SM_I + h + 1, lo:hi] for h in heads]
        mprev = [m_scr[h][:, 0:1] for h in heads]
        qh = [q_all[lo:hi, hsl[h]] for h in heads]
        kh = [k_all[lo:hi, hsl[h]] for h in heads]
        vh = [v_ref[lo:hi, hsl[h]] for h in heads]
        ct = [ct_scr[h] for h in heads]
        nrow = [n_scr[h] for h in heads]
        qk = [_dot_nt(qh[h], kh[h]) for h in heads]
        qc = [_dot(qh[h], ct[h].astype(BF16)) for h in heads]
        dmat = [jnp.where(causal, bcol[h] - brow[h] + irow[h], NEG) for h in heads]
        inter = [bcol[h] + mprev[h] for h in heads]
        mt = [jnp.maximum(jnp.max(dmat[h], axis=-1, keepdims=True), inter[h]) for h in heads]
        a = [jnp.exp(dmat[h] - mt[h]) * qk[h] for h in heads]
        dec = [jnp.exp(inter[h] - mt[h]) for h in heads]
        num = [_dot(a[h].astype(BF16), vh[h]) + dec[h] * qc[h] for h in heads]
        den = [jnp.sum(a[h], axis=-1, keepdims=True)
               + dec[h] * jnp.sum(qh[h].astype(F32) * nrow[h], axis=-1, keepdims=True) for h in heads]
        blast = [bcol[h][lc - 1:lc, :] for h in heads]
        mnew = [jnp.maximum(blast[h] + mprev[h], jnp.max(blast[h] - brow[h] + irow[h], axis=-1, keepdims=True))
                for h in heads]
        wprev = [jnp.exp(blast[h] + mprev[h] - mnew[h]) for h in heads]
        kwt = [kh[h].astype(F32) * jnp.exp(blast[h] - bcol[h] + icol[h] - mnew[h]) for h in heads]
        for h in heads:
            ct_scr[h] = wprev[h] * ct[h] + lax.dot_general(kwt[h].astype(BF16), vh[h], _TN,
                                                           preferred_element_type=F32)
            n_scr[h] = wprev[h] * nrow[h] + jnp.sum(kwt[h], axis=0, keepdims=True)
            m_scr[h] = jnp.broadcast_to(mnew[h], (1, LANES))
        hm = [num[h] / jnp.maximum(jnp.abs(den[h]), jnp.exp(-mt[h])) * _sigmoid(og_ref[lo:hi, hsl[h]].astype(F32))
              for h in heads]
        for h in heads:
            y_ref[lo:hi, hsl[h]] = _rms(hm[h], ghn_ref[:, hsl[h]]).astype(BF16)


def _mlstm(qkb, vb, ob, sm, smt, wconv, bconv, bifc, bifr, ghn, b, s):
    nb = s // ML_BLOCK
    row = lambda w: pl.BlockSpec((ML_BLOCK, w), lambda bi, j: (bi * nb + j, 0))
    return pl.pallas_call(
        _mlstm_kernel,
        out_shape=jax.ShapeDtypeStruct((b * s, ML_WIDTH), BF16),
        grid=(b, nb),
        in_specs=[row(2 * ML_WIDTH), row(ML_WIDTH), row(ML_WIDTH), row(LANES),
                  pl.BlockSpec((LANES, ML_BLOCK), lambda bi, j: (0, bi * nb + j)),
                  _const_spec(wconv.shape), _const_spec(bconv.shape), _const_spec(bifc.shape),
                  _const_spec(bifr.shape), _const_spec(ghn.shape)],
        out_specs=row(ML_WIDTH),
        scratch_shapes=[pltpu.VMEM((SUBLANES, 2 * ML_WIDTH), F32),
                        pltpu.VMEM((ML_HEADS, ML_HD, ML_HD), F32),
                        pltpu.VMEM((ML_HEADS, 1, ML_HD), F32),
                        pltpu.VMEM((ML_HEADS, 1, LANES), F32)],
        compiler_params=pltpu.CompilerParams(dimension_semantics=("arbitrary", "arbitrary"),
                                             vmem_limit_bytes=VMEM_LIMIT),
        name="mlstm",
    )(qkb, vb, ob, sm, smt, wconv, bconv, bifc, bifr, ghn)


RT_BUCKET = N_EXPERTS
RT_RANK = N_EXPERTS + 1
RT_WLO = N_EXPERTS + 2
RT_WHI = N_EXPERTS + 3
RT_LPOS = N_EXPERTS + 4
N_BUCKETS = N_GROUPS * 6
X_ROWS = D_MODEL // LANES
DMA_UNROLL = 8
RUN_CHUNK = SUBLANES
MAX_CHUNKS = TD // RUN_CHUNK + N_BUCKETS
SORT_ROWS = MAX_CHUNKS * RUN_CHUNK
WAIT_GROUP = 8
COMBINE_PARTS = 2


def _merge_kernel(ya_ref, yb_ref, mg_ref, x_ref, wpa_ref, wpb_ref, wout_ref, gffn_ref, wr_ref, br_ref,
                  x1_ref, h2_ref, slab_ref, hist_ref):
    halves = [slice(i * (TD // 2), (i + 1) * (TD // 2)) for i in range(2)]
    pa = [_dot(ya_ref[hs, :], wpa_ref[...]) for hs in halves]
    pb = [_dot(yb_ref[hs, :], wpb_ref[...]) for hs in halves]
    mixed = [(_sigmoid(mg_ref[hs, 0:D_MODEL].astype(F32)) * pa[i]
              + _sigmoid(mg_ref[hs, D_MODEL:2 * D_MODEL].astype(F32)) * pb[i]).astype(BF16)
             for i, hs in enumerate(halves)]
    x1 = [x_ref[hs, :] + _dot(mixed[i], wout_ref[...]) for i, hs in enumerate(halves)]
    h2 = [_rms(x1[i], gffn_ref[...]) for i in range(2)]
    h_hi = [h.astype(BF16) for h in h2]
    for i, hs in enumerate(halves):
        x1_ref[hs, :] = x1[i]
        h2_ref[hs, :] = h_hi[i]

    h_lo = [(h2[i] - h_hi[i].astype(F32)).astype(BF16) for i in range(2)]
    r_hi = [_dot(h, wr_ref[...]) for h in h_hi]
    logit = jnp.concatenate([r_hi[i][:, 0:LANES] + r_hi[i][:, LANES:2 * LANES] + _dot(h_lo[i], wr_ref[:, 0:LANES])
                             for i in range(2)], axis=0) + br_ref[...]
    lane = lax.broadcasted_iota(jnp.int32, logit.shape, 1)
    big = jnp.int32(LANES)
    gmask = (lane >= N_EXPERTS) & (lane < N_EXPERTS + N_GROUPS)
    gl = jnp.where(gmask, logit, NEG)
    gmax = jnp.max(gl, axis=-1, keepdims=True)
    gidx = jnp.min(jnp.where(gmask & (gl == gmax), lane, big), axis=-1, keepdims=True) - N_EXPERTS
    pg_sel = 1.0 / jnp.sum(jnp.where(gmask, jnp.exp(gl - gmax), 0.0), axis=-1, keepdims=True)
    emask = (lane < N_EXPERTS) & (lax.shift_right_logical(lane, 2) == gidx)
    el = jnp.where(emask, logit, NEG)
    e1 = jnp.max(el, axis=-1, keepdims=True)
    i1 = jnp.min(jnp.where(emask & (el == e1), lane, big), axis=-1, keepdims=True)
    emask2 = emask & (lane != i1)
    el2 = jnp.where(emask2, logit, NEG)
    e2 = jnp.max(el2, axis=-1, keepdims=True)
    i2 = jnp.min(jnp.where(emask2 & (el2 == e2), lane, big), axis=-1, keepdims=True)
    x21 = jnp.exp(e2 - e1)
    w1 = pg_sel / (1.0 + x21)
    w2 = pg_sel * x21 / (1.0 + x21)
    first_lo = i1 < i2
    e_lo = jnp.where(first_lo, i1, i2) - EXP_PER_GROUP * gidx
    e_hi = jnp.where(first_lo, i2, i1) - EXP_PER_GROUP * gidx
    pair = lax.shift_right_logical(e_lo * (2 * EXP_PER_GROUP - 1 - e_lo), 1) + (e_hi - e_lo - 1)
    bucket = 6 * gidx + pair
    member = lane == bucket
    onehot = jnp.where(member, 1.0, 0.0)
    rr = lax.broadcasted_iota(jnp.int32, (TD, TD), 0)
    cc = lax.broadcasted_iota(jnp.int32, (TD, TD), 1)
    earlier = _dot((rr > cc).astype(BF16), onehot.astype(BF16))
    rank = jnp.sum(jnp.where(member, earlier, 0.0), axis=-1, keepdims=True)
    slab = jnp.where(lane == i1, w1, 0.0) + jnp.where(lane == i2, w2, 0.0)
    slab = jnp.where(lane == RT_BUCKET, bucket.astype(F32), slab)
    slab = jnp.where(lane == RT_RANK, rank, slab)
    slab = jnp.where(lane == RT_WLO, jnp.where(first_lo, w1, w2), slab)
    slab = jnp.where(lane == RT_WHI, jnp.where(first_lo, w2, w1), slab)
    hist8 = jnp.broadcast_to(jnp.sum(onehot, axis=0, keepdims=True), (SUBLANES, LANES))
    kk = lax.broadcasted_iota(jnp.int32, (LANES, LANES), 0)
    ll = lax.broadcasted_iota(jnp.int32, (LANES, LANES), 1)
    runlen = jnp.ceil(hist8 / RUN_CHUNK) * RUN_CHUNK
    before = sum(_dot(part, (kk < ll).astype(BF16)) for part in _split3(runlen))[0:1, :]
    lpos = jnp.sum(jnp.where(member, before, 0.0), axis=-1, keepdims=True) + rank
    slab = jnp.where(lane == RT_LPOS, lpos, slab)
    slab_ref[...] = slab
    hist_ref[...] = hist8


def _merge(ya, yb, mg, x2, wpa, wpb, wout, gffn, wr, br):
    t = x2.shape[0]
    row = lambda w: pl.BlockSpec((TD, w), lambda i: (i, 0))
    return pl.pallas_call(
        _merge_kernel,
        out_shape=[jax.ShapeDtypeStruct((t, D_MODEL), F32),
                   jax.ShapeDtypeStruct((t, D_MODEL), BF16),
                   jax.ShapeDtypeStruct((t, LANES), F32),
                   jax.ShapeDtypeStruct((t // TD * SUBLANES, LANES), F32)],
        grid=(t // TD,),
        in_specs=[row(NSA_HEADS * NSA_HD), row(ML_WIDTH), row(2 * D_MODEL), row(D_MODEL)]
                 + [_const_spec(a.shape) for a in (wpa, wpb, wout, gffn, wr, br)],
        out_specs=[row(D_MODEL), row(D_MODEL), row(LANES),
                   pl.BlockSpec((SUBLANES, LANES), lambda i: (i, 0))],
        compiler_params=pltpu.CompilerParams(dimension_semantics=("arbitrary",), vmem_limit_bytes=VMEM_LIMIT),
        name="merge",
    )(ya, yb, mg, x2, wpa, wpb, wout, gffn, wr, br)


def _rec_copy(src_ref, src_tok, dst_ref, dst_tok, sem, rows):
    src = src_ref.at[pl.ds(pl.multiple_of(src_tok * rows, rows), rows), :]
    dst = dst_ref.at[pl.ds(pl.multiple_of(dst_tok * rows, rows), rows), :]
    return pltpu.make_async_copy(src, dst, sem)


def _token_copies(n, make, wait=False):
    def body(g, carry):
        for u in range(DMA_UNROLL):
            cp = make(g * DMA_UNROLL + u)
            if wait:
                cp.wait()
            else:
                cp.start(priority=u % 2)
        return carry
    lax.fori_loop(0, n // DMA_UNROLL, body, 0)


def _dispatch_kernel(ch_ref, tail_ref, h2_ref, slab_ref, hx_ref, stage, zero_scr, sem, zsem):
    i = pl.program_id(0)
    n_steps = pl.num_programs(0)
    slot = lax.rem(i, 2)
    n_tiles = hx_ref.shape[0] // TM
    n_used = tail_ref[2 * N_BUCKETS]

    def zero_copies(first_slot, n):
        rows = pl.ds(pl.multiple_of(first_slot, n), n)
        return (pltpu.make_async_copy(zero_scr.at[0:n, :], hx_ref.at[rows, :], zsem),)

    def zero_fill(wait):
        def chunk(b, q, carry):
            for cp in zero_copies(tail_ref[b] + q * RUN_CHUNK, RUN_CHUNK):
                cp.wait() if wait else cp.start()
            return carry

        def tile(k, carry):
            for cp in zero_copies(k * TM, TM):
                cp.wait() if wait else cp.start()
            return carry
        for b in range(N_BUCKETS):
            lax.fori_loop(0, tail_ref[N_BUCKETS + b], functools.partial(chunk, b), 0)
        lax.fori_loop(n_used, n_tiles, tile, 0)

    @pl.when(i == 0)
    def _():
        zero_scr[...] = jnp.zeros_like(zero_scr)
        zero_fill(wait=False)

    slab = slab_ref[...]
    perm = (slab[:, RT_LPOS:RT_LPOS + 1]
            == lax.broadcasted_iota(jnp.int32, (1, SORT_ROWS), 1).astype(F32)).astype(BF16)
    stage[slot, :, 0:D_MODEL] = lax.dot_general(perm, h2_ref[...], _TN, preferred_element_type=F32)
    stage[slot, :, D_MODEL:] = sum(lax.dot_general(perm, part, _TN, preferred_element_type=F32)
                                   for part in _split3(slab))

    def chunk_copies(tile, which, wait):
        src_at = n_steps + tile * MAX_CHUNKS
        dst_at = n_steps + (n_steps + tile) * MAX_CHUNKS

        def body(q, carry):
            src = pl.ds(pl.multiple_of(ch_ref[src_at + q], RUN_CHUNK), RUN_CHUNK)
            dst = pl.ds(pl.multiple_of(ch_ref[dst_at + q], RUN_CHUNK), RUN_CHUNK)
            cp = pltpu.make_async_copy(stage.at[which, src, :], hx_ref.at[dst, :], sem.at[which])
            cp.wait() if wait else cp.start()
            return carry

        def group_wait(g, carry):
            rows = slice(0, WAIT_GROUP * RUN_CHUNK)
            pltpu.make_async_copy(stage.at[which, rows, :], hx_ref.at[rows, :], sem.at[which]).wait()
            return carry
        n = ch_ref[tile]
        grouped = 0
        if wait:
            grouped = n // WAIT_GROUP * WAIT_GROUP
            lax.fori_loop(0, n // WAIT_GROUP, group_wait, 0)
        lax.fori_loop(grouped, n, body, 0)

    chunk_copies(i, slot, wait=False)

    @pl.when(i > 0)
    def _():
        chunk_copies(i - 1, 1 - slot, wait=True)

    @pl.when(i == n_steps - 1)
    def _():
        chunk_copies(i, slot, wait=True)
        zero_fill(wait=True)


def _dispatch(chunks, tail, h2, slab, n_slots):
    t = h2.shape[0]
    return pl.pallas_call(
        _dispatch_kernel,
        out_shape=jax.ShapeDtypeStruct((n_slots, D_MODEL + LANES), F32),
        grid_spec=pltpu.PrefetchScalarGridSpec(
            num_scalar_prefetch=2,
            grid=(t // TD,),
            in_specs=[pl.BlockSpec((TD, D_MODEL), lambda i, ch_r, tail_r: (i, 0)),
                      pl.BlockSpec((TD, LANES), lambda i, ch_r, tail_r: (i, 0))],
            out_specs=pl.BlockSpec(memory_space=pl.ANY),
            scratch_shapes=[pltpu.VMEM((2, SORT_ROWS, D_MODEL + LANES), F32), pltpu.VMEM((TM, D_MODEL + LANES), F32),
                            pltpu.SemaphoreType.DMA((2,)), pltpu.SemaphoreType.DMA(())],
        ),
        compiler_params=pltpu.CompilerParams(dimension_semantics=("arbitrary",), vmem_limit_bytes=VMEM_LIMIT,
                                             has_side_effects=True),
        name="dispatch",
    )(chunks, tail, h2, slab)


MOE_TILES = 2


def _moe_kernel(te_ref, nu_ref, hx_ref, w13_ref, w2_ref, y_ref):
    step = pl.program_id(0)
    n_tiles = pl.num_programs(0) * MOE_TILES
    n_used = nu_ref[0]

    @pl.when(step * MOE_TILES < n_used)
    def _():
        subs = range(MOE_TILES)
        hs = [hx_ref[sub * TM:(sub + 1) * TM, 0:D_MODEL].astype(BF16) for sub in subs]
        slabs = [hx_ref[sub * TM:(sub + 1) * TM, D_MODEL:] for sub in subs]
        ys = [None] * MOE_TILES
        for side, lane in ((0, RT_WLO), (1, RT_WHI)):
            es = [te_ref[side * n_tiles + step * MOE_TILES + sub] for sub in subs]
            up = [_dot(hs[sub], w13_ref[es[sub]]) for sub in subs]
            act = [(up[sub][:, 0:D_EXPERT] * _sigmoid(up[sub][:, 0:D_EXPERT]) * up[sub][:, D_EXPERT:2 * D_EXPERT]
                    * slabs[sub][:, lane:lane + 1]).astype(BF16) for sub in subs]
            for sub in subs:
                part = _dot(act[sub], w2_ref[es[sub]])
                ys[sub] = part if ys[sub] is None else ys[sub] + part
        for sub in subs:
            for j in range(X_ROWS):
                y_ref[pl.ds(sub * TM * X_ROWS + j, TM, stride=X_ROWS), :] = ys[sub][:, j * LANES:(j + 1) * LANES]

    @pl.when(step * MOE_TILES >= n_used)
    def _():
        y_ref[...] = jnp.zeros_like(y_ref)


def _moe(tile_e, n_used, hx_sorted, w13, w2):
    n_tiles = hx_sorted.shape[0] // TM
    rows = MOE_TILES * TM
    last = lambda nu: (nu[0] - 1) // MOE_TILES
    return pl.pallas_call(
        _moe_kernel,
        out_shape=jax.ShapeDtypeStruct((n_tiles * TM * X_ROWS, LANES), F32),
        grid_spec=pltpu.PrefetchScalarGridSpec(
            num_scalar_prefetch=2,
            grid=(n_tiles // MOE_TILES,),
            in_specs=[pl.BlockSpec((rows, D_MODEL + LANES), lambda k, te, nu: (jnp.minimum(k, last(nu)), 0)),
                      pl.BlockSpec(w13.shape, lambda k, te, nu: (0, 0, 0), pipeline_mode=pl.Buffered(1)),
                      pl.BlockSpec(w2.shape, lambda k, te, nu: (0, 0, 0), pipeline_mode=pl.Buffered(1))],
            out_specs=pl.BlockSpec((rows * X_ROWS, LANES), lambda k, te, nu: (k, 0)),
        ),
        compiler_params=pltpu.CompilerParams(dimension_semantics=("arbitrary",), vmem_limit_bytes=VMEM_LIMIT),
        name="moe",
    )(tile_e, n_used, hx_sorted, w13, w2)


def _combine_kernel(pos_ref, y_ref, x1_ref, p_ref, gple_ref, wpg_ref, wpp_ref, gfin_ref, o_ref, ybuf, sem):
    i = pl.program_id(0)
    slot = lax.rem(i, 2)

    last = pl.num_programs(0) - 1

    def drain(which):
        _token_copies(TD, lambda r: _rec_copy(y_ref, 0, ybuf.at[which], 0, sem.at[which], X_ROWS), wait=True)

    @pl.when(i == 0)
    def _():
        _token_copies(TD, lambda r: _rec_copy(y_ref, pos_ref[r], ybuf.at[0], r, sem.at[0], X_ROWS))

    drain(slot)
    nxt = jnp.minimum(i + 1, last) * TD
    yb = ybuf.at[slot]
    part = TD // COMBINE_PARTS
    for q in range(COMBINE_PARTS):
        rows = slice(q * part, (q + 1) * part)
        y = jnp.concatenate([yb[pl.ds(q * part * X_ROWS + j, part, stride=X_ROWS), :] for j in range(X_ROWS)],
                            axis=1)
        x2 = x1_ref[rows, :] + y
        h3 = _rms(x2, gple_ref[...]).astype(BF16)
        x3 = x2 + _sigmoid(_dot(h3, wpg_ref[...])) * _dot(p_ref[rows, :].astype(BF16), wpp_ref[...])
        o_ref[rows, :] = _rms(x3, gfin_ref[...])
        for r in range(q * part, (q + 1) * part):
            _rec_copy(y_ref, pos_ref[nxt + r], ybuf.at[1 - slot], r, sem.at[1 - slot], X_ROWS).start(
                priority=r % 2)

    @pl.when(i == last)
    def _():
        drain(1 - slot)


def _combine(pos, y_sorted, x1, p2, gple, wpg, wpp, gfin):
    t = x1.shape[0]
    row = lambda w: pl.BlockSpec((TD, w), lambda i, pos_r: (i, 0))
    const = lambda a: pl.BlockSpec(a.shape, lambda i, pos_r: (0,) * a.ndim, pipeline_mode=pl.Buffered(1))
    return pl.pallas_call(
        _combine_kernel,
        out_shape=jax.ShapeDtypeStruct((t, D_MODEL), F32),
        grid_spec=pltpu.PrefetchScalarGridSpec(
            num_scalar_prefetch=1,
            grid=(t // TD,),
            in_specs=[pl.BlockSpec(memory_space=pl.ANY), row(D_MODEL), row(PLE_DIM),
                      const(gple), const(wpg), const(wpp), const(gfin)],
            out_specs=row(D_MODEL),
            scratch_shapes=[pltpu.VMEM((2, TD * X_ROWS, LANES), F32), pltpu.SemaphoreType.DMA((2,))],
        ),
        compiler_params=pltpu.CompilerParams(dimension_semantics=("arbitrary",), vmem_limit_bytes=VMEM_LIMIT),
        name="combine",
    )(pos, y_sorted, x1, p2, gple, wpg, wpp, gfin)


def _routing_tables(slab, hist8):
    t = slab.shape[0]
    nt = t // TD
    n_tiles = t // TM + N_BUCKETS + -(-nt * N_BUCKETS * (RUN_CHUNK - 1) // TM)
    n_tiles += n_tiles % MOE_TILES
    hist = hist8.reshape(nt, SUBLANES, LANES)[:, 0, :]
    runlen = jnp.ceil(hist / RUN_CHUNK) * RUN_CHUNK
    counts = jnp.sum(runlen, axis=0)
    padded = jnp.ceil(counts / TM) * TM
    ends = jnp.cumsum(padded)
    first = (ends - padded)[None, :] + jnp.cumsum(runlen, axis=0) - runlen
    lane = jnp.arange(N_BUCKETS, dtype=F32)[None, :]
    mine = lane == slab[:, RT_BUCKET:RT_BUCKET + 1]
    pos = jnp.sum(jnp.where(mine, jnp.repeat(first[:, :N_BUCKETS], TD, axis=0), 0.0), axis=1) + slab[:, RT_RANK]
    starts = jnp.arange(n_tiles, dtype=F32) * TM
    tile_bucket = jnp.minimum(jnp.sum(ends[None, :N_BUCKETS] <= starts[:, None], axis=1), N_BUCKETS - 1)
    group, pair = tile_bucket // 6, tile_bucket % 6
    e_lo = EXP_PER_GROUP * group + jnp.array([0, 0, 0, 1, 1, 2], jnp.int32)[pair]
    e_hi = EXP_PER_GROUP * group + jnp.array([1, 2, 3, 2, 3, 3], jnp.int32)[pair]
    tile_e = jnp.concatenate([e_lo, e_hi]).astype(jnp.int32)
    n_used = (ends[N_BUCKETS - 1] / TM).astype(jnp.int32).reshape(1)
    tail = jnp.concatenate([(ends - padded + counts)[:N_BUCKETS], ((padded - counts) / RUN_CHUNK)[:N_BUCKETS],
                            n_used.astype(F32)]).astype(jnp.int32)
    nb = N_BUCKETS
    nch = runlen[:, :nb] / RUN_CHUNK
    cum = jnp.cumsum(nch, axis=1)
    q = jnp.arange(MAX_CHUNKS, dtype=F32)[None, :, None]
    of_b = jnp.arange(nb)[None, None, :] == jnp.minimum(jnp.sum(cum[:, None, :] <= q, axis=2), nb - 1)[..., None]
    pick = lambda a: jnp.sum(jnp.where(of_b, a[:, None, :nb], 0.0), axis=2)
    within = RUN_CHUNK * (q[..., 0] - pick(cum - nch))
    local = jnp.cumsum(runlen, axis=1) - runlen
    chunks = jnp.concatenate([cum[:, nb - 1], (pick(local) + within).reshape(-1),
                              (pick(first) + within).reshape(-1)]).astype(jnp.int32)
    return pos.astype(jnp.int32), chunks, tile_e, n_used, tail, n_tiles * TM


def _pack_inproj_weights(w):
    d = w.shape[0]
    qw = NSA_HEADS * NSA_HD
    kvw = NSA_KV * NSA_HD
    o = 0
    wq = w[:, o:o + qw]; o += qw
    wkc = w[:, o:o + kvw]; o += kvw
    wvc = w[:, o:o + kvw]; o += kvw
    wks = w[:, o:o + kvw]; o += kvw
    wvs = w[:, o:o + kvw]; o += kvw
    wkw = w[:, o:o + kvw]; o += kvw
    wvw = w[:, o:o + kvw]; o += kvw
    wga = w[:, o:o + 3 * NSA_HEADS]; o += 3 * NSA_HEADS
    wqkb = w[:, o:o + 2 * ML_WIDTH]; o += 2 * ML_WIDTH
    wvb = w[:, o:o + ML_WIDTH]; o += ML_WIDTH
    wob = w[:, o:o + ML_WIDTH]; o += ML_WIDTH
    wif = w[:, o:o + 2 * ML_HEADS]; o += 2 * ML_HEADS
    wmg = w[:, o:o + 2 * D_MODEL]
    wsm = jnp.concatenate([wga, wif, jnp.zeros((d, LANES - 3 * NSA_HEADS - 2 * ML_HEADS), w.dtype)], axis=1)
    wcat = jnp.concatenate([wq, wkc, wvc, wks, wkw, wsm, wqkb, wvb, wob, wmg], axis=1).astype(BF16)
    wtr = jnp.concatenate([wvs, wvw, wsm], axis=1).T.astype(BF16)
    return wcat, wtr


def _rope_tables(positions):
    half = ROPE_DIM // 2
    inv = ROPE_THETA ** (-jnp.arange(0, ROPE_DIM, 2, dtype=F32) / ROPE_DIM)
    ang = positions.astype(F32).reshape(-1, 1) * inv[None, :]
    cs = jnp.concatenate([jnp.cos(ang), jnp.sin(ang)], axis=1)
    d = np.arange(LANES) % NSA_HD
    spread = np.zeros((2 * half, 3 * LANES), np.float32)
    lanes = np.arange(LANES)
    rot = d < ROPE_DIM
    spread[(d % half)[rot], lanes[rot]] = 1.0
    hi = (d >= half) & rot
    spread[half + (d % half)[hi], LANES + lanes[hi]] = 1.0
    lo = d < half
    spread[half + (d % half)[lo], 2 * LANES + lanes[lo]] = -1.0
    unrotated = (~rot).astype(np.float32).reshape(1, LANES)
    return cs, jnp.asarray(spread, BF16), jnp.asarray(unrotated)


def _pack_compress_weights(w1, w2, pe):
    half = CMP_LEN // 2
    w1r = w1.reshape(2, half, NSA_HD, CMP_HIDDEN)
    outs = []
    for part in range(2):
        wb = w1r[part].astype(BF16)
        zb = jnp.zeros_like(wb)
        wp = jnp.stack([jnp.stack([wb, zb], axis=2), jnp.stack([zb, wb], axis=2)], axis=1)
        outs.append(wp.reshape(half * NSA_KV * NSA_HD, NSA_KV * CMP_HIDDEN))
    pe8 = jnp.broadcast_to(pe.reshape(1, CMP_LEN * NSA_HD), (SUBLANES, CMP_LEN * NSA_HD)).astype(BF16)
    return outs[0], outs[1], pe8, w1.astype(BF16)


def _stages(x, p, positions, g_mix, w_in, b_if, w_ck1, w_ck2, pe_ck, w_cv1, w_cv2, pe_cv, w_conv, b_conv, g_hn, w_pa, w_pb, w_out, g_ffn, w_rg, b_rg, w_re, b_re, w_e13, w_e2, g_ple, w_pg, w_pp, g_final):
    b, s, d = x.shape
    t = b * s
    cs, spread, unrot = _rope_tables(positions)
    assert w_in.shape[0] == 1, "the final norm is fused into the layer's last kernel: single-layer problem only"
    for i in range(w_in.shape[0]):
        x2 = x.reshape(t, d)
        wcat, wtr = _pack_inproj_weights(w_in[i])
        (qpad, kc_tok, vc_tok, ks, kw, vst, vwt, sm, smt, qkb, vb, ob, mg) = _inproj(
            x2, g_mix[i].reshape(1, d), wcat, wtr, cs, spread, unrot)
        wka, wkb, pek, w1k = _pack_compress_weights(w_ck1[i], w_ck2[i], pe_ck[i])
        wva, wvb, pev, w1v = _pack_compress_weights(w_cv1[i], w_cv2[i], pe_cv[i])
        zpad = jnp.zeros((CMP_HIDDEN, NSA_HD), F32)
        w2k = jnp.stack([jnp.concatenate([w_ck2[i], zpad], axis=1),
                         jnp.concatenate([zpad, w_ck2[i]], axis=1)]).astype(BF16)
        w2vt = w_cv2[i].T.astype(BF16)
        nrow = s // CMP_STRIDE
        rk = kc_tok.reshape(b, nrow, CMP_STRIDE * LANES)
        rv = vc_tok.reshape(b, nrow, CMP_STRIDE * LANES)
        kcb, vct = _compress(rk, rv, wka, wkb, wva, wvb, pek, pev, w1k, w1v, w2k, w2vt)
        ya = _nsa(qpad, kcb, vct, ks, kw, vst, vwt, smt, b, s)
        bif = b_if[i].astype(F32)
        bifc = jnp.zeros((1, LANES), F32).at[0, SM_I:SM_I + 2 * ML_HEADS].set(bif)
        bifr = bifc.reshape(LANES, 1)
        yb = _mlstm(qkb, vb, ob, sm, smt, w_conv[i], b_conv[i].reshape(1, -1), bifc, bifr,
                    g_hn[i].reshape(1, -1), b, s)
        wr = jnp.concatenate([w_re[i], w_rg[i], jnp.zeros((d, LANES - N_EXPERTS - N_GROUPS), F32)], axis=1)
        wr_hi = wr.astype(BF16)
        wr = jnp.concatenate([wr_hi, (wr - wr_hi.astype(F32)).astype(BF16)], axis=1)
        br =jnp.concatenate([b_re[i], b_rg[i], jnp.zeros((LANES - N_EXPERTS - N_GROUPS,), F32)]).reshape(1, LANES)
        x1, h2, slab, hist8 = _merge(ya, yb, mg, x2, w_pa[i].astype(BF16), w_pb[i].astype(BF16),
                                     w_out[i].astype(BF16), g_ffn[i].reshape(1, d), wr, br)
        pos, chunks, tile_e, n_used, tail, n_slots = _routing_tables(slab, hist8)
        hx_sorted = _dispatch(chunks, tail, h2, slab, n_slots)
        y_sorted = _moe(tile_e, n_used, hx_sorted, w_e13[i].astype(BF16), w_e2[i].astype(BF16))
        out = _combine(pos, y_sorted, x1, p[i].reshape(t, PLE_DIM), g_ple[i].reshape(1, d), w_pg[i].astype(BF16),
                       w_pp[i].astype(BF16), g_final.reshape(1, d))
        x = out.reshape(b, s, d)
    return dict(out=x, qpad=qpad, ks=ks, kcb=kcb, vct=vct, y_a=ya, y_b=yb, x1=x1, pos=pos)


def kernel(x, p, positions, g_mix, w_in, b_if, w_ck1, w_ck2, pe_ck, w_cv1, w_cv2, pe_cv, w_conv, b_conv, g_hn, w_pa, w_pb, w_out, g_ffn, w_rg, b_rg, w_re, b_re, w_e13, w_e2, g_ple, w_pg, w_pp, g_final):
    return _stages(x, p, positions, g_mix, w_in, b_if, w_ck1, w_ck2, pe_ck, w_cv1, w_cv2, pe_cv, w_conv, b_conv, g_hn,
                   w_pa, w_pb, w_out, g_ffn, w_rg, b_rg, w_re, b_re, w_e13, w_e2, g_ple, w_pg, w_pp, g_final)["out"]
```

```python
import functools
import math

import numpy as np
import jax
import jax.numpy as jnp
from jax import lax
from jax.experimental import pallas as pl
from jax.experimental.pallas import tpu as pltpu

F32 = jnp.float32
BF16 = jnp.bfloat16

EPS = 1e-6
NEG = -1e30

D_MODEL = 1024
PLE_DIM = 256
NSA_HEADS = 8
NSA_KV = 2
NSA_HPG = NSA_HEADS // NSA_KV
NSA_HD = 64
CMP_LEN = 32
CMP_STRIDE = 16
CMP_HIDDEN = 256
SEL_BLOCK = 64
SEL_TOPK = 16
SEL_FORCE = 1000.0
WINDOW = 512
ROPE_THETA = 500000.0
ROPE_DIM = NSA_HD // 4
ML_HEADS = 4
ML_HD = 128
ML_WIDTH = ML_HEADS * ML_HD
CONV_W = 4
N_GROUPS = 4
EXP_PER_GROUP = 4
N_EXPERTS = N_GROUPS * EXP_PER_GROUP
D_EXPERT = 256

LANES = 128
SUBLANES = 8
QT = 128
KC = 128
SEL_GROUP = 512
VT_PAD = 16
VT_ROWS = NSA_HD + VT_PAD
ML_CHUNK = 128
ML_BLOCK = 256
TD = 512
TM = 256
VMEM_LIMIT = 56 * 1024 * 1024

_NT = (((1,), (1,)), ((), ()))
_TN = (((0,), (0,)), ((), ()))

SM_GATE = 0
SM_I = 3 * NSA_HEADS
SM_F = SM_I + ML_HEADS


def _dot(a, b):
    return jnp.dot(a, b, preferred_element_type=F32)


def _dot_nt(a, b):
    return lax.dot_general(a, b, _NT, preferred_element_type=F32)


def _split3(x):
    hi = x.astype(BF16)
    r1 = x - hi.astype(F32)
    mid = r1.astype(BF16)
    lo = (r1 - mid.astype(F32)).astype(BF16)
    return hi, mid, lo


def _rms(x, g):
    return x * lax.rsqrt(jnp.mean(x * x, axis=-1, keepdims=True) + EPS) * g


def _sigmoid(x):
    return 0.5 + 0.5 * jnp.tanh(0.5 * x)


def _const_spec(shape):
    nd = len(shape)
    return pl.BlockSpec(shape, lambda *_: (0,) * nd, pipeline_mode=pl.Buffered(1))


_C_Q = 0
_C_KC = _C_Q + NSA_HEADS * NSA_HD
_C_VC = _C_KC + LANES
_C_KS = _C_VC + LANES
_C_KW = _C_KS + LANES
_C_SM = _C_KW + LANES
_C_QKB = _C_SM + LANES
_C_VB = _C_QKB + 2 * ML_WIDTH
_C_OB = _C_VB + ML_WIDTH
_C_MG = _C_OB + ML_WIDTH
_C_END = _C_MG + 2 * D_MODEL


def _inproj_kernel(x_ref, g_ref, w_ref, wt_ref, cs_ref, spread_ref, unrot_ref,
                   q_ref, kc_ref, vc_ref, ks_ref, kw_ref, vst_ref, vwt_ref, sm_ref, smt_ref,
                   qkb_ref, vb_ref, ob_ref, mg_ref):
    hn = _rms(x_ref[...], g_ref[...]).astype(BF16)
    tables = sum(_dot(part, spread_ref[...]) for part in _split3(cs_ref[...]))
    rc = tables[:, 0:LANES] + unrot_ref[...]
    rp = tables[:, LANES:2 * LANES]
    rm = tables[:, 2 * LANES:3 * LANES]

    def rope(z):
        half = ROPE_DIM // 2
        return z * rc + pltpu.roll(z, half, 1) * rp + pltpu.roll(z, LANES - half, 1) * rm

    scale = NSA_HD ** -0.5 * math.log2(math.e)
    for h in range(NSA_HEADS * NSA_HD // LANES):
        z = _dot(hn, w_ref[:, _C_Q + h * LANES:_C_Q + (h + 1) * LANES])
        q_ref[:, h * LANES:(h + 1) * LANES] = (rope(z) * scale).astype(BF16)
    kc_ref[...] = rope(_dot(hn, w_ref[:, _C_KC:_C_KC + LANES])).astype(BF16)
    vc_ref[...] = _dot(hn, w_ref[:, _C_VC:_C_VC + LANES]).astype(BF16)
    ks_ref[...] = rope(_dot(hn, w_ref[:, _C_KS:_C_KS + LANES])).astype(BF16)
    kw_ref[...] = rope(_dot(hn, w_ref[:, _C_KW:_C_KW + LANES])).astype(BF16)
    sm_ref[...] = _dot(hn, w_ref[:, _C_SM:_C_SM + LANES])
    for c0 in range(0, 2 * ML_WIDTH, 512):
        qkb_ref[:, c0:c0 + 512] = _dot(hn, w_ref[:, _C_QKB + c0:_C_QKB + c0 + 512]).astype(BF16)
    vb_ref[...] = _dot(hn, w_ref[:, _C_VB:_C_VB + ML_WIDTH]).astype(BF16)
    ob_ref[...] = _dot(hn, w_ref[:, _C_OB:_C_OB + ML_WIDTH]).astype(BF16)
    for c0 in range(0, 2 * D_MODEL, 512):
        mg_ref[:, c0:c0 + 512] = _dot(hn, w_ref[:, _C_MG + c0:_C_MG + c0 + 512]).astype(BF16)
    zt = _dot_nt(wt_ref[...], hn)
    ones_rows = (lax.broadcasted_iota(jnp.int32, (VT_PAD, KC), 0) == 0).astype(BF16)
    for i in range(TD // KC):
        for ref, r0 in ((vst_ref, 0), (vwt_ref, LANES)):
            zc = zt[r0:r0 + LANES, i * KC:(i + 1) * KC].astype(BF16)
            ref[i] = jnp.concatenate([piece for g in range(NSA_KV)
                                      for piece in (zc[g * NSA_HD:(g + 1) * NSA_HD, :], ones_rows)], axis=0)
    smt_ref[...] = zt[2 * LANES:3 * LANES, :]


def _inproj(x2, g_mix, wcat, wtr, cs, spread, unrot):
    t = x2.shape[0]
    row = lambda w: pl.BlockSpec((TD, w), lambda i: (i, 0))
    out_shape = [
        jax.ShapeDtypeStruct((t, NSA_HEADS * NSA_HD), BF16),
        jax.ShapeDtypeStruct((t, LANES), BF16),
        jax.ShapeDtypeStruct((t, LANES), BF16),
        jax.ShapeDtypeStruct((t, LANES), BF16),
        jax.ShapeDtypeStruct((t, LANES), BF16),
        jax.ShapeDtypeStruct((t // KC, NSA_KV * VT_ROWS, KC), BF16),
        jax.ShapeDtypeStruct((t // KC, NSA_KV * VT_ROWS, KC), BF16),
        jax.ShapeDtypeStruct((t, LANES), F32),
        jax.ShapeDtypeStruct((LANES, t), F32),
        jax.ShapeDtypeStruct((t, 2 * ML_WIDTH), BF16),
        jax.ShapeDtypeStruct((t, ML_WIDTH), BF16),
        jax.ShapeDtypeStruct((t, ML_WIDTH), BF16),
        jax.ShapeDtypeStruct((t, 2 * D_MODEL), BF16),
    ]
    chunk3 = pl.BlockSpec((TD // KC, NSA_KV * VT_ROWS, KC), lambda i: (i, 0, 0))
    out_specs = [row(NSA_HEADS * NSA_HD), row(LANES), row(LANES), row(LANES), row(LANES), chunk3, chunk3,
                 row(LANES), pl.BlockSpec((LANES, TD), lambda i: (0, i)),
                 row(2 * ML_WIDTH), row(ML_WIDTH), row(ML_WIDTH), row(2 * D_MODEL)]
    return pl.pallas_call(
        _inproj_kernel,
        out_shape=out_shape,
        grid=(t // TD,),
        in_specs=[row(D_MODEL), _const_spec((1, D_MODEL)), _const_spec((D_MODEL, _C_END)),
                  _const_spec((3 * LANES, D_MODEL)), row(cs.shape[1]), _const_spec(spread.shape),
                  _const_spec(unrot.shape)],
        out_specs=out_specs,
        compiler_params=pltpu.CompilerParams(dimension_semantics=("arbitrary",), vmem_limit_bytes=VMEM_LIMIT),
        name="inproj",
    )(x2, g_mix, wcat, wtr, cs, spread, unrot)


def _gelu_tanh(x):
    return 0.5 * x * (1.0 + jnp.tanh(math.sqrt(2.0 / math.pi) * (x + 0.044715 * x * x * x)))


def _compress_kernel(rk_ref, rv_ref, wka_ref, wkb_ref, wva_ref, wvb_ref, pek_ref, pev_ref,
                     w1k_ref, w1v_ref, w2k_ref, w2vt_ref, kc_ref, vct_ref):
    nrow = rk_ref.shape[0]

    def hidden(r_ref, wa_ref, wb_ref, pe_ref, w1_ref):
        r = r_ref[...]
        ha = _dot(r, wa_ref[...])
        hb = _dot(r, wb_ref[...])
        hb = pltpu.roll(hb, nrow - 1, 0)
        c = _dot(pe_ref[...], w1_ref[...])[0:1, :]
        return [_gelu_tanh(ha[:, g * CMP_HIDDEN:(g + 1) * CMP_HIDDEN] + hb[:, g * CMP_HIDDEN:(g + 1) * CMP_HIDDEN] + c).astype(BF16)
                for g in range(NSA_KV)]

    ak = hidden(rk_ref, wka_ref, wkb_ref, pek_ref, w1k_ref)
    kc_ref[...] = (_dot(ak[0], w2k_ref[0]) + _dot(ak[1], w2k_ref[1])).astype(BF16)
    av = hidden(rv_ref, wva_ref, wvb_ref, pev_ref, w1v_ref)
    for g in range(NSA_KV):
        vct_ref[g * NSA_HD:(g + 1) * NSA_HD, :] = _dot_nt(w2vt_ref[...], av[g]).astype(BF16)


def _compress(rk, rv, wka, wkb, wva, wvb, pek, pev, w1k, w1v, w2k, w2vt):
    b, nrow, width = rk.shape
    blk = pl.BlockSpec((None, nrow, width), lambda i: (i, 0, 0))
    return pl.pallas_call(
        _compress_kernel,
        out_shape=[jax.ShapeDtypeStruct((b, nrow, LANES), BF16),
                   jax.ShapeDtypeStruct((b, LANES, nrow), BF16)],
        grid=(b,),
        in_specs=[blk, blk] + [_const_spec(a.shape) for a in (wka, wkb, wva, wvb, pek, pev, w1k, w1v, w2k, w2vt)],
        out_specs=[pl.BlockSpec((None, nrow, LANES), lambda i: (i, 0, 0)),
                   pl.BlockSpec((None, LANES, nrow), lambda i: (i, 0, 0))],
        compiler_params=pltpu.CompilerParams(dimension_semantics=("arbitrary",), vmem_limit_bytes=VMEM_LIMIT),
        name="compress",
    )(rk, rv, wka, wkb, wva, wvb, pek, pev, w1k, w1v, w2k, w2vt)


def _nsa_kernel(q_ref, kc_ref, vct_ref, ks_ref, kw_ref, vst_ref, vwt_ref, smt_ref, o_ref, bias_scr, sx_scr, sy_scr):
    c = pl.program_id(1)
    t0 = c * QT
    ncmp = kc_ref.shape[0]
    nsel = bias_scr.shape[0]
    nw = WINDOW // KC + 1
    gw = NSA_HPG * QT
    width = NSA_KV * gw

    def per_group(x):
        return [x[:, g * gw:(g + 1) * gw] for g in range(NSA_KV)]

    def pv(vt, p):
        rows = vt.shape[0] // NSA_KV
        pb = p.astype(BF16)
        return jnp.concatenate([_dot(vt[g * rows:(g + 1) * rows, :], pg) for g, pg in enumerate(per_group(pb))],
                               axis=1)

    def normalised(acc):
        return acc[0:NSA_HD, :] / acc[NSA_HD:NSA_HD + 1, :]

    low_half = lax.broadcasted_iota(jnp.int32, (1, LANES), 1) < NSA_HD
    q_heads = []
    for h in range(NSA_HEADS):
        pair = q_ref[:, (h // 2) * LANES:(h // 2 + 1) * LANES].astype(F32)
        want_low = h // NSA_HPG == 0
        if (h % 2 == 0) != want_low:
            pair = pltpu.roll(pair, NSA_HD, 1)
        q_heads.append(jnp.where(low_half if want_low else ~low_half, pair, 0.0).astype(BF16))
    qs = jnp.concatenate(q_heads, axis=0)
    u_row = lax.broadcasted_iota(jnp.int32, (1, width), 1) % QT
    t_row = t0 + u_row
    r_kc = lax.broadcasted_iota(jnp.int32, (KC, 1), 0)

    n_grp = ks_ref.shape[0] // SEL_GROUP
    n_full = lax.shift_right_logical(t0, int(math.log2(SEL_GROUP)))

    def qk_group(j):
        return _dot_nt(ks_ref[pl.ds(pl.multiple_of(j * SEL_GROUP, SEL_GROUP), SEL_GROUP), :], qs)

    sc = _dot_nt(kc_ref[...], qs)

    w_slabs, w_chunks = [], []
    for i in range(nw):
        jj = c - (nw - 1) + i
        jc = jnp.maximum(jj, 0)
        si = _dot_nt(kw_ref[pl.ds(pl.multiple_of(jc * KC, KC), KC), :], qs)
        if i == 0:
            keep = jnp.where(jj >= 0, r_kc, -1) > u_row
        elif i == nw - 1:
            keep = r_kc <= u_row
        else:
            keep = jj >= 0
        w_slabs.append(jnp.where(keep, si, NEG))
        w_chunks.append(jc)

    n_col = lax.broadcasted_iota(jnp.int32, (ncmp, 1), 0)
    last_tok = jnp.where(n_col < ncmp - 1, CMP_STRIDE * n_col + (CMP_LEN - 1), jnp.iinfo(jnp.int32).max)
    s = jnp.where(last_tok <= t_row, sc, NEG)
    m = jnp.max(s, axis=0, keepdims=True)
    e = jnp.exp2(s - m)
    anyv = (t_row >= CMP_LEN - 1).astype(F32)
    p = e * (anyv / jnp.sum(e, axis=0, keepdims=True))
    o_cmp = pv(vct_ref[...], p)

    psums = []
    for pg in per_group(p):
        acc_p = pg[:, 0:QT]
        for h in range(1, NSA_HPG):
            acc_p = acc_p + pg[:, h * QT:(h + 1) * QT]
        psums.append(acc_p)
    psum = jnp.concatenate(psums, axis=1)
    nq2 = NSA_KV * QT
    s_col = lax.broadcasted_iota(jnp.int32, (nsel, 1), 0)
    n_lane = lax.broadcasted_iota(jnp.int32, (1, ncmp), 1)
    ov = ((CMP_STRIDE * n_lane < SEL_BLOCK * (s_col + 1)) & (CMP_STRIDE * n_lane + (CMP_LEN - 1) >= SEL_BLOCK * s_col)
          ).astype(BF16)
    imp = sum(_dot(ov, part) for part in _split3(psum))

    s_diag = qk_group(n_full)
    s_own = _dot_nt(ks_ref[pl.ds(pl.multiple_of(t0, KC), KC), :], qs)
    sx_scr[...] = qk_group(0)

    mxw = w_slabs[0]
    for sl in w_slabs[1:]:
        mxw = jnp.maximum(mxw, sl)
    mw = jnp.max(mxw, axis=0, keepdims=True)
    acc_w = jnp.zeros((VT_ROWS, width), F32)
    for sl, jc in zip(w_slabs, w_chunks):
        acc_w = acc_w + pv(vwt_ref[jc], jnp.exp2(sl - mw))
    o_win = normalised(acc_w)

    t1 = t0 + lax.broadcasted_iota(jnp.int32, (1, nq2), 1) % QT
    cur = lax.shift_right_logical(t1, 6)
    forced = (s_col == 0) | (s_col == cur) | (s_col == cur - 1)
    valid = SEL_BLOCK * s_col <= t1
    val = jnp.where(valid, jnp.where(forced, imp + SEL_FORCE, imp), NEG)
    sub = 8
    r_sub = lax.broadcasted_iota(jnp.int32, (sub, 1), 0)
    blocks = [val[r * sub:(r + 1) * sub, :] for r in range(nsel // sub)]
    ranks = [jnp.zeros((sub, nq2), F32) for _ in blocks]
    for i in range(nsel):
        vi = val[i:i + 1, :]
        for r, blk in enumerate(blocks):
            if i < r * sub:
                beats = vi >= blk
            elif i >= (r + 1) * sub:
                beats = vi > blk
            else:
                beats = (vi > blk) | ((vi == blk) & (r_sub > i - r * sub))
            ranks[r] = ranks[r] + jnp.where(beats, 1.0, 0.0)
    bias = jnp.where(jnp.concatenate(ranks, axis=0) < SEL_TOPK, 0.0, NEG).astype(F32)
    bias_scr[...] = jnp.concatenate([bias[:, g * QT:(g + 1) * QT] for g in range(NSA_KV) for _ in range(NSA_HPG)],
                                    axis=1)

    blk_per_grp = SEL_GROUP // SEL_BLOCK
    chunk_per_grp = SEL_GROUP // KC
    blk_per_chunk = KC // SEL_BLOCK

    def sel_update(j, sj, carry):
        m_o, acc = carry
        brows = [bias_scr[pl.ds(blk_per_grp * j + i, 1), :] for i in range(blk_per_grp)]

        def block(i):
            return sj[i * SEL_BLOCK:(i + 1) * SEL_BLOCK, :]

        mx = None
        for i in range(blk_per_grp):
            sl = block(i) + brows[i]
            mx = sl if mx is None else jnp.maximum(mx, sl)
        m_n = jnp.maximum(m_o, jnp.max(mx, axis=0, keepdims=True))
        a = jnp.exp2(m_o - m_n)
        acc = a * acc
        for ci in range(chunk_per_grp):
            parts = [jnp.exp2(block(i) + (brows[i] - m_n))
                     for i in range(blk_per_chunk * ci, blk_per_chunk * (ci + 1))]
            acc = acc + pv(vst_ref[chunk_per_grp * j + ci], jnp.concatenate(parts, axis=0))
        return m_n, acc

    def seed_state():
        first_own = blk_per_chunk * (c % chunk_per_grp)
        brows = [bias_scr[pl.ds(blk_per_grp * n_full + i, 1), :] + jnp.where(i < first_own, 0.0, NEG)
                 for i in range(blk_per_grp)]
        own = jnp.where(r_kc <= u_row, s_own, NEG)
        mx = jnp.maximum(own[0:SEL_BLOCK, :], own[SEL_BLOCK:KC, :])
        for i in range(blk_per_grp):
            mx = jnp.maximum(mx, s_diag[i * SEL_BLOCK:(i + 1) * SEL_BLOCK, :] + brows[i])
        m_n = jnp.max(mx, axis=0, keepdims=True)
        acc = pv(vst_ref[c], jnp.exp2(own - m_n))
        for ci in range(chunk_per_grp):
            parts = [jnp.exp2(s_diag[i * SEL_BLOCK:(i + 1) * SEL_BLOCK, :] + (brows[i] - m_n))
                     for i in range(blk_per_chunk * ci, blk_per_chunk * (ci + 1))]
            acc = acc + pv(vst_ref[chunk_per_grp * n_full + ci], jnp.concatenate(parts, axis=0))
        return m_n, acc

    seeded = seed_state()

    def pair_body(jp, carry):
        ja, jb = 2 * jp, 2 * jp + 1
        sy_scr[...] = qk_group(jb)
        carry = sel_update(ja, sx_scr, carry)
        sx_scr[...] = qk_group(jnp.minimum(ja + 2, n_grp - 1))
        return sel_update(jb, sy_scr, carry)

    n_pairs = lax.shift_right_logical(n_full, 1)
    carry = lax.fori_loop(0, n_pairs, pair_body, seeded)
    _, acc_s = lax.cond(n_full - 2 * n_pairs == 1, lambda cr: sel_update(n_full - 1, sx_scr, cr), lambda cr: cr,
                        carry)
    o_sel = normalised(acc_s)

    def gate_row(br):
        rows = [smt_ref[SM_GATE + 3 * h + br:SM_GATE + 3 * h + br + 1, :] for h in range(NSA_HEADS)]
        return _sigmoid(jnp.concatenate(rows, axis=1))

    o_t = gate_row(0) * o_cmp + gate_row(1) * o_sel + gate_row(2) * o_win
    for pr in range(NSA_HEADS // 2):
        xp = jnp.concatenate([o_t[:, (2 * pr) * QT:(2 * pr + 1) * QT], o_t[:, (2 * pr + 1) * QT:(2 * pr + 2) * QT]], axis=0)
        o_ref[:, pr * LANES:(pr + 1) * LANES] = xp.T.astype(BF16)


def _nsa(qpad, kcb, vct, ks, kw, vst, vwt, smt, b, s):
    nq = s // QT
    ncmp = kcb.shape[1]
    return pl.pallas_call(
        _nsa_kernel,
        out_shape=jax.ShapeDtypeStruct((b * s, NSA_HEADS * NSA_HD), BF16),
        grid=(b, nq),
        in_specs=[
            pl.BlockSpec((QT, NSA_HEADS * NSA_HD), lambda bi, c: (bi * nq + c, 0)),
            pl.BlockSpec((None, ncmp, LANES), lambda bi, c: (bi, 0, 0)),
            pl.BlockSpec((None, NSA_KV * NSA_HD, ncmp), lambda bi, c: (bi, 0, 0)),
            pl.BlockSpec((s, LANES), lambda bi, c: (bi, 0)),
            pl.BlockSpec((s, LANES), lambda bi, c: (bi, 0)),
            pl.BlockSpec((s // KC, NSA_KV * VT_ROWS, KC), lambda bi, c: (bi, 0, 0)),
            pl.BlockSpec((s // KC, NSA_KV * VT_ROWS, KC), lambda bi, c: (bi, 0, 0)),
            pl.BlockSpec((LANES, QT), lambda bi, c: (0, bi * nq + c)),
        ],
        out_specs=pl.BlockSpec((QT, NSA_HEADS * NSA_HD), lambda bi, c: (bi * nq + c, 0)),
        scratch_shapes=[pltpu.VMEM((s // SEL_BLOCK, NSA_HEADS * QT), F32),
                        pltpu.VMEM((SEL_GROUP, NSA_HEADS * QT), F32),
                        pltpu.VMEM((SEL_GROUP, NSA_HEADS * QT), F32)],
        compiler_params=pltpu.CompilerParams(dimension_semantics=("arbitrary", "arbitrary"),
                                             vmem_limit_bytes=VMEM_LIMIT),
        name="nsa",
    )(qpad, kcb, vct, ks, kw, vst, vwt, smt)


def _log_sigmoid(x):
    return jnp.minimum(x, 0.0) - jnp.log(1.0 + jnp.exp(-jnp.abs(x)))


def _mlstm_kernel(qk_ref, v_ref, og_ref, sm_ref, smt_ref, wc_ref, bc_ref, bifc_ref, bifr_ref, ghn_ref,
                  y_ref, tail_scr, ct_scr, n_scr, m_scr):
    lc = ML_CHUNK

    @pl.when(pl.program_id(1) == 0)
    def _():
        tail_scr[...] = jnp.zeros_like(tail_scr)
        ct_scr[...] = jnp.zeros_like(ct_scr)
        n_scr[...] = jnp.zeros_like(n_scr)
        m_scr[...] = jnp.zeros_like(m_scr)

    u = qk_ref[...]
    tail = tail_scr[...]
    rr8 = lax.broadcasted_iota(jnp.int32, (SUBLANES, 1), 0)
    sr = lax.broadcasted_iota(jnp.int32, (ML_BLOCK, ML_BLOCK), 0)
    sc_ = lax.broadcasted_iota(jnp.int32, (ML_BLOCK, ML_BLOCK), 1)
    y = bc_ref[...] + wc_ref[CONV_W - 1:CONV_W, :] * u.astype(F32)
    for k in range(1, CONV_W):
        down = _dot((sr - sc_ == k).astype(BF16), u)
        head = jnp.where(rr8 < k, pltpu.roll(tail, k, 0), down[0:SUBLANES, :])
        y = y + wc_ref[CONV_W - 1 - k:CONV_W - k, :] * jnp.concatenate([head, down[SUBLANES:, :]], axis=0)
    tail_scr[...] = u[ML_BLOCK - SUBLANES:ML_BLOCK, :].astype(F32)
    qkc = y * _sigmoid(y)
    q_all = qkc[:, 0:ML_WIDTH].astype(BF16)
    k_all = (qkc[:, ML_WIDTH:2 * ML_WIDTH] * (ML_HD ** -0.5)).astype(BF16)

    ifc = sm_ref[...] + bifc_ref[...]
    ifr = smt_ref[...] + bifr_ref[...]
    lfc = _log_sigmoid(ifc)
    lfr = _log_sigmoid(ifr)
    rr = lax.broadcasted_iota(jnp.int32, (lc, lc), 0)
    cc = lax.broadcasted_iota(jnp.int32, (lc, lc), 1)
    causal = rr >= cc
    tri_l = causal.astype(F32)
    tri_u = (rr <= cc).astype(F32)

    for ci in range(ML_BLOCK // lc):
        lo, hi = ci * lc, (ci + 1) * lc
        bc_all = jnp.dot(tri_l, lfc[lo:hi, :], preferred_element_type=F32, precision=lax.Precision.HIGHEST)
        br_all = jnp.dot(lfr[:, lo:hi], tri_u, preferred_element_type=F32, precision=lax.Precision.HIGHEST)
        heads = range(ML_HEADS)
        hsl = [slice(h * ML_HD, (h + 1) * ML_HD) for h in heads]
        bcol = [bc_all[:, SM_F + h:SM_F + h + 1] for h in heads]
        brow = [br_all[SM_F + h:SM_F + h + 1, :] for h in heads]
        icol = [ifc[lo:hi, SM_I + h:SM_I + h + 1] for h in heads]
        irow = [ifr[SM_I + h:SM_I + h + 1, lo:hi] for h in heads]
        mprev = [m_scr[h][:, 0:1] for h in heads]
        qh = [q_all[lo:hi, hsl[h]] for h in heads]
        kh = [k_all[lo:hi, hsl[h]] for h in heads]
        vh = [v_ref[lo:hi, hsl[h]] for h in heads]
        ct = [ct_scr[h] for h in heads]
        nrow = [n_scr[h] for h in heads]
        qk = [_dot_nt(qh[h], kh[h]) for h in heads]
        qc = [_dot(qh[h], ct[h].astype(BF16)) for h in heads]
        dmat = [jnp.where(causal, bcol[h] - brow[h] + irow[h], NEG) for h in heads]
        inter = [bcol[h] + mprev[h] for h in heads]
        mt = [jnp.maximum(jnp.max(dmat[h], axis=-1, keepdims=True), inter[h]) for h in heads]
        a = [jnp.exp(dmat[h] - mt[h]) * qk[h] for h in heads]
        dec = [jnp.exp(inter[h] - mt[h]) for h in heads]
        num = [_dot(a[h].astype(BF16), vh[h]) + dec[h] * qc[h] for h in heads]
        den = [jnp.sum(a[h], axis=-1, keepdims=True)
               + dec[h] * jnp.sum(qh[h].astype(F32) * nrow[h], axis=-1, keepdims=True) for h in heads]
        blast = [bcol[h][lc - 1:lc, :] for h in heads]
        mnew = [jnp.maximum(blast[h] + mprev[h], jnp.max(blast[h] - brow[h] + irow[h], axis=-1, keepdims=True))
                for h in heads]
        wprev = [jnp.exp(blast[h] + mprev[h] - mnew[h]) for h in heads]
        kwt = [kh[h].astype(F32) * jnp.exp(blast[h] - bcol[h] + icol[h] - mnew[h]) for h in heads]
        for h in heads:
            ct_scr[h] = wprev[h] * ct[h] + lax.dot_general(kwt[h].astype(BF16), vh[h], _TN,
                                                           preferred_element_type=F32)
            n_scr[h] = wprev[h] * nrow[h] + jnp.sum(kwt[h], axis=0, keepdims=True)
            m_scr[h] = jnp.broadcast_to(mnew[h], (1, LANES))
        hm = [num[h] / jnp.maximum(jnp.abs(den[h]), jnp.exp(-mt[h])) * _sigmoid(og_ref[lo:hi, hsl[h]].astype(F32))
              for h in heads]
        for h in heads:
            y_ref[lo:hi, hsl[h]] = _rms(hm[h], ghn_ref[:, hsl[h]]).astype(BF16)


def _mlstm(qkb, vb, ob, sm, smt, wconv, bconv, bifc, bifr, ghn, b, s):
    nb = s // ML_BLOCK
    row = lambda w: pl.BlockSpec((ML_BLOCK, w), lambda bi, j: (bi * nb + j, 0))
    return pl.pallas_call(
        _mlstm_kernel,
        out_shape=jax.ShapeDtypeStruct((b * s, ML_WIDTH), BF16),
        grid=(b, nb),
        in_specs=[row(2 * ML_WIDTH), row(ML_WIDTH), row(ML_WIDTH), row(LANES),
                  pl.BlockSpec((LANES, ML_BLOCK), lambda bi, j: (0, bi * nb + j)),
                  _const_spec(wconv.shape), _const_spec(bconv.shape), _const_spec(bifc.shape),
                  _const_spec(bifr.shape), _const_spec(ghn.shape)],
        out_specs=row(ML_WIDTH),
        scratch_shapes=[pltpu.VMEM((SUBLANES, 2 * ML_WIDTH), F32),
                        pltpu.VMEM((ML_HEADS, ML_HD, ML_HD), F32),
                        pltpu.VMEM((ML_HEADS, 1, ML_HD), F32),
                        pltpu.VMEM((ML_HEADS, 1, LANES), F32)],
        compiler_params=pltpu.CompilerParams(dimension_semantics=("arbitrary", "arbitrary"),
                                             vmem_limit_bytes=VMEM_LIMIT),
        name="mlstm",
    )(qkb, vb, ob, sm, smt, wconv, bconv, bifc, bifr, ghn)


RT_BUCKET = N_EXPERTS
RT_RANK = N_EXPERTS + 1
RT_WLO = N_EXPERTS + 2
RT_WHI = N_EXPERTS + 3
RT_LPOS = N_EXPERTS + 4
N_BUCKETS = N_GROUPS * 6
X_ROWS = D_MODEL // LANES
DMA_UNROLL = 8
RUN_CHUNK = SUBLANES
MAX_CHUNKS = TD // RUN_CHUNK + N_BUCKETS
SORT_ROWS = MAX_CHUNKS * RUN_CHUNK
WAIT_GROUP = 8
TERM_LANES = 32
COMBINE_PARTS = 2


def _merge_kernel(ya_ref, yb_ref, mg_ref, x_ref, wpa_ref, wpb_ref, wout_ref, gffn_ref, wr_ref, br_ref,
                  x1_ref, h2_ref, slab_ref, hist_ref):
    halves = [slice(i * (TD // 2), (i + 1) * (TD // 2)) for i in range(2)]
    pa = [_dot(ya_ref[hs, :], wpa_ref[...]) for hs in halves]
    pb = [_dot(yb_ref[hs, :], wpb_ref[...]) for hs in halves]
    mixed = [(_sigmoid(mg_ref[hs, 0:D_MODEL].astype(F32)) * pa[i]
              + _sigmoid(mg_ref[hs, D_MODEL:2 * D_MODEL].astype(F32)) * pb[i]).astype(BF16)
             for i, hs in enumerate(halves)]
    x1 = [x_ref[hs, :] + _dot(mixed[i], wout_ref[...]) for i, hs in enumerate(halves)]
    h2 = [_rms(x1[i], gffn_ref[...]) for i in range(2)]
    h_hi = [h.astype(BF16) for h in h2]
    for i, hs in enumerate(halves):
        x1_ref[hs, :] = x1[i]
        h2_ref[hs, :] = h_hi[i]

    h_lo = [(h2[i] - h_hi[i].astype(F32)).astype(BF16) for i in range(2)]
    r_hi = [_dot(h, wr_ref[...]) for h in h_hi]
    logit = jnp.concatenate([r_hi[i][:, 0:LANES] + r_hi[i][:, LANES:2 * LANES] + _dot(h_lo[i], wr_ref[:, 0:LANES])
                             for i in range(2)], axis=0) + br_ref[...]
    lane = lax.broadcasted_iota(jnp.int32, logit.shape, 1)
    big = jnp.int32(LANES)
    gmask = (lane >= N_EXPERTS) & (lane < N_EXPERTS + N_GROUPS)
    gl = jnp.where(gmask, logit, NEG)
    gmax = jnp.max(gl, axis=-1, keepdims=True)
    gidx = jnp.min(jnp.where(gmask & (gl == gmax), lane, big), axis=-1, keepdims=True) - N_EXPERTS
    pg_sel = 1.0 / jnp.sum(jnp.where(gmask, jnp.exp(gl - gmax), 0.0), axis=-1, keepdims=True)
    emask = (lane < N_EXPERTS) & (lax.shift_right_logical(lane, 2) == gidx)
    el = jnp.where(emask, logit, NEG)
    e1 = jnp.max(el, axis=-1, keepdims=True)
    i1 = jnp.min(jnp.where(emask & (el == e1), lane, big), axis=-1, keepdims=True)
    emask2 = emask & (lane != i1)
    el2 = jnp.where(emask2, logit, NEG)
    e2 = jnp.max(el2, axis=-1, keepdims=True)
    i2 = jnp.min(jnp.where(emask2 & (el2 == e2), lane, big), axis=-1, keepdims=True)
    x21 = jnp.exp(e2 - e1)
    w1 = pg_sel / (1.0 + x21)
    w2 = pg_sel * x21 / (1.0 + x21)
    first_lo = i1 < i2
    e_lo = jnp.where(first_lo, i1, i2) - EXP_PER_GROUP * gidx
    e_hi = jnp.where(first_lo, i2, i1) - EXP_PER_GROUP * gidx
    pair = lax.shift_right_logical(e_lo * (2 * EXP_PER_GROUP - 1 - e_lo), 1) + (e_hi - e_lo - 1)
    bucket = 6 * gidx + pair
    member = lane == bucket
    onehot = jnp.where(member, 1.0, 0.0)
    rr = lax.broadcasted_iota(jnp.int32, (TD, TD), 0)
    cc = lax.broadcasted_iota(jnp.int32, (TD, TD), 1)
    earlier = _dot((rr > cc).astype(BF16), onehot.astype(BF16))
    rank = jnp.sum(jnp.where(member, earlier, 0.0), axis=-1, keepdims=True)
    slab = jnp.where(lane == i1, w1, 0.0) + jnp.where(lane == i2, w2, 0.0)
    slab = jnp.where(lane == RT_BUCKET, bucket.astype(F32), slab)
    slab = jnp.where(lane == RT_RANK, rank, slab)
    slab = jnp.where(lane == RT_WLO, jnp.where(first_lo, w1, w2), slab)
    slab = jnp.where(lane == RT_WHI, jnp.where(first_lo, w2, w1), slab)
    hist8 = jnp.broadcast_to(jnp.sum(onehot, axis=0, keepdims=True), (SUBLANES, LANES))
    kk = lax.broadcasted_iota(jnp.int32, (LANES, LANES), 0)
    ll = lax.broadcasted_iota(jnp.int32, (LANES, LANES), 1)
    runlen = jnp.ceil(hist8 / RUN_CHUNK) * RUN_CHUNK
    before = sum(_dot(part, (kk < ll).astype(BF16)) for part in _split3(runlen))[0:1, :]
    lpos = jnp.sum(jnp.where(member, before, 0.0), axis=-1, keepdims=True) + rank
    slab = jnp.where(lane == RT_LPOS, lpos, slab)
    slab_ref[...] = slab
    hist_ref[...] = hist8


def _merge(ya, yb, mg, x2, wpa, wpb, wout, gffn, wr, br):
    t = x2.shape[0]
    row = lambda w: pl.BlockSpec((TD, w), lambda i: (i, 0))
    return pl.pallas_call(
        _merge_kernel,
        out_shape=[jax.ShapeDtypeStruct((t, D_MODEL), F32),
                   jax.ShapeDtypeStruct((t, D_MODEL), BF16),
                   jax.ShapeDtypeStruct((t, LANES), F32),
                   jax.ShapeDtypeStruct((t // TD * SUBLANES, LANES), F32)],
        grid=(t // TD,),
        in_specs=[row(NSA_HEADS * NSA_HD), row(ML_WIDTH), row(2 * D_MODEL), row(D_MODEL)]
                 + [_const_spec(a.shape) for a in (wpa, wpb, wout, gffn, wr, br)],
        out_specs=[row(D_MODEL), row(D_MODEL), row(LANES),
                   pl.BlockSpec((SUBLANES, LANES), lambda i: (i, 0))],
        compiler_params=pltpu.CompilerParams(dimension_semantics=("arbitrary",), vmem_limit_bytes=VMEM_LIMIT),
        name="merge",
    )(ya, yb, mg, x2, wpa, wpb, wout, gffn, wr, br)


def _rec_copy(src_ref, src_tok, dst_ref, dst_tok, sem, rows):
    src = src_ref.at[pl.ds(pl.multiple_of(src_tok * rows, rows), rows), :]
    dst = dst_ref.at[pl.ds(pl.multiple_of(dst_tok * rows, rows), rows), :]
    return pltpu.make_async_copy(src, dst, sem)


def _token_copies(n, make):
    def body(g, carry):
        for u in range(DMA_UNROLL):
            make(g * DMA_UNROLL + u).start(priority=u % 2)
        return carry
    lax.fori_loop(0, n // DMA_UNROLL, body, 0)


def _dispatch_kernel(ch_ref, tail_ref, h2_ref, slab_ref, hx_ref, stage, zero_scr, sem, zsem):
    i = pl.program_id(0)
    n_steps = pl.num_programs(0)
    slot = lax.rem(i, 2)
    n_tiles = hx_ref.shape[0] // TM
    n_used = tail_ref[2 * N_BUCKETS]

    def zero_copies(first_slot, n):
        rows = pl.ds(pl.multiple_of(first_slot, n), n)
        return (pltpu.make_async_copy(zero_scr.at[0:n, :], hx_ref.at[rows, :], zsem),)

    def zero_fill(wait):
        def chunk(b, q, carry):
            for cp in zero_copies(tail_ref[b] + q * RUN_CHUNK, RUN_CHUNK):
                cp.wait() if wait else cp.start()
            return carry

        def tile(k, carry):
            for cp in zero_copies(k * TM, TM):
                cp.wait() if wait else cp.start()
            return carry
        for b in range(N_BUCKETS):
            lax.fori_loop(0, tail_ref[N_BUCKETS + b], functools.partial(chunk, b), 0)
        lax.fori_loop(n_used, n_tiles, tile, 0)

    @pl.when(i == 0)
    def _():
        zero_scr[...] = jnp.zeros_like(zero_scr)
        zero_fill(wait=False)

    slab = slab_ref[...]
    perm = (slab[:, RT_LPOS:RT_LPOS + 1]
            == lax.broadcasted_iota(jnp.int32, (1, SORT_ROWS), 1).astype(F32)).astype(BF16)
    stage[slot, :, 0:D_MODEL] = lax.dot_general(perm, h2_ref[...], _TN, preferred_element_type=F32)
    lane = lax.broadcasted_iota(jnp.int32, (1, LANES), 1)
    hi, mid, lo = _split3(jnp.where((lane == RT_WLO) | (lane == RT_WHI), slab, 0.0))
    terms = (hi.astype(F32) + pltpu.roll(mid.astype(F32), TERM_LANES, 1)
             + pltpu.roll(lo.astype(F32), 2 * TERM_LANES, 1)).astype(BF16)
    stage[slot, :, D_MODEL:] = lax.dot_general(perm, terms, _TN, preferred_element_type=F32)

    def chunk_copies(tile, which, wait):
        src_at = n_steps + tile * MAX_CHUNKS
        dst_at = n_steps + (n_steps + tile) * MAX_CHUNKS

        def body(q, carry):
            src = pl.ds(pl.multiple_of(ch_ref[src_at + q], RUN_CHUNK), RUN_CHUNK)
            dst = pl.ds(pl.multiple_of(ch_ref[dst_at + q], RUN_CHUNK), RUN_CHUNK)
            cp = pltpu.make_async_copy(stage.at[which, src, :], hx_ref.at[dst, :], sem.at[which])
            cp.wait() if wait else cp.start()
            return carry

        def group_wait(g, carry):
            rows = slice(0, WAIT_GROUP * RUN_CHUNK)
            pltpu.make_async_copy(stage.at[which, rows, :], hx_ref.at[rows, :], sem.at[which]).wait()
            return carry
        n = ch_ref[tile]
        grouped = 0
        if wait:
            grouped = n // WAIT_GROUP * WAIT_GROUP
            lax.fori_loop(0, n // WAIT_GROUP, group_wait, 0)
        lax.fori_loop(grouped, n, body, 0)

    chunk_copies(i, slot, wait=False)

    @pl.when(i > 0)
    def _():
        chunk_copies(i - 1, 1 - slot, wait=True)

    @pl.when(i == n_steps - 1)
    def _():
        chunk_copies(i, slot, wait=True)
        zero_fill(wait=True)


def _dispatch(chunks, tail, h2, slab, n_slots):
    t = h2.shape[0]
    return pl.pallas_call(
        _dispatch_kernel,
        out_shape=jax.ShapeDtypeStruct((n_slots, D_MODEL + LANES), F32),
        grid_spec=pltpu.PrefetchScalarGridSpec(
            num_scalar_prefetch=2,
            grid=(t // TD,),
            in_specs=[pl.BlockSpec((TD, D_MODEL), lambda i, ch_r, tail_r: (i, 0)),
                      pl.BlockSpec((TD, LANES), lambda i, ch_r, tail_r: (i, 0))],
            out_specs=pl.BlockSpec(memory_space=pl.ANY),
            scratch_shapes=[pltpu.VMEM((2, SORT_ROWS, D_MODEL + LANES), F32), pltpu.VMEM((TM, D_MODEL + LANES), F32),
                            pltpu.SemaphoreType.DMA((2,)), pltpu.SemaphoreType.DMA(())],
        ),
        compiler_params=pltpu.CompilerParams(dimension_semantics=("arbitrary",), vmem_limit_bytes=VMEM_LIMIT,
                                             has_side_effects=True),
        name="dispatch",
    )(chunks, tail, h2, slab)


MOE_TILES = 2


def _moe_kernel(te_ref, nu_ref, hx_ref, w13_ref, w2_ref, y_ref):
    step = pl.program_id(0)
    n_tiles = pl.num_programs(0) * MOE_TILES
    n_used = nu_ref[0]

    @pl.when(step * MOE_TILES < n_used)
    def _():
        subs = range(MOE_TILES)
        hs = [hx_ref[sub * TM:(sub + 1) * TM, 0:D_MODEL].astype(BF16) for sub in subs]
        slabs = [hx_ref[sub * TM:(sub + 1) * TM, D_MODEL:] for sub in subs]
        ys = [None] * MOE_TILES
        weight = lambda s, lane: sum(s[:, lane + k * TERM_LANES:lane + k * TERM_LANES + 1] for k in range(3))
        for side, lane in ((0, RT_WLO), (1, RT_WHI)):
            es = [te_ref[side * n_tiles + step * MOE_TILES + sub] for sub in subs]
            up = [_dot(hs[sub], w13_ref[es[sub]]) for sub in subs]
            act = [(up[sub][:, 0:D_EXPERT] * _sigmoid(up[sub][:, 0:D_EXPERT]) * up[sub][:, D_EXPERT:2 * D_EXPERT]
                    * weight(slabs[sub], lane)).astype(BF16) for sub in subs]
            for sub in subs:
                part = _dot(act[sub], w2_ref[es[sub]])
                ys[sub] = part if ys[sub] is None else ys[sub] + part
        for sub in subs:
            for j in range(X_ROWS):
                y_ref[pl.ds(sub * TM * X_ROWS + j, TM, stride=X_ROWS), :] = ys[sub][:, j * LANES:(j + 1) * LANES]

    @pl.when(step * MOE_TILES >= n_used)
    def _():
        y_ref[...] = jnp.zeros_like(y_ref)


def _moe(tile_e, n_used, hx_sorted, w13, w2):
    n_tiles = hx_sorted.shape[0] // TM
    rows = MOE_TILES * TM
    last = lambda nu: (nu[0] - 1) // MOE_TILES
    return pl.pallas_call(
        _moe_kernel,
        out_shape=jax.ShapeDtypeStruct((n_tiles * TM * X_ROWS, LANES), F32),
        grid_spec=pltpu.PrefetchScalarGridSpec(
            num_scalar_prefetch=2,
            grid=(n_tiles // MOE_TILES,),
            in_specs=[pl.BlockSpec((rows, D_MODEL + LANES), lambda k, te, nu: (jnp.minimum(k, last(nu)), 0)),
                      pl.BlockSpec(w13.shape, lambda k, te, nu: (0, 0, 0), pipeline_mode=pl.Buffered(1)),
                      pl.BlockSpec(w2.shape, lambda k, te, nu: (0, 0, 0), pipeline_mode=pl.Buffered(1))],
            out_specs=pl.BlockSpec((rows * X_ROWS, LANES), lambda k, te, nu: (k, 0)),
        ),
        compiler_params=pltpu.CompilerParams(dimension_semantics=("arbitrary",), vmem_limit_bytes=VMEM_LIMIT),
        name="moe",
    )(tile_e, n_used, hx_sorted, w13, w2)


def _combine_kernel(pos_ref, y_ref, x1_ref, p_ref, gple_ref, wpg_ref, wpp_ref, gfin_ref, o_ref, ybuf, sem):
    i = pl.program_id(0)
    slot = lax.rem(i, 2)

    last = pl.num_programs(0) - 1

    def drain(which):
        pltpu.make_async_copy(y_ref.at[0:TD * X_ROWS, :], ybuf.at[which], sem.at[which]).wait()

    @pl.when(i == 0)
    def _():
        _token_copies(TD, lambda r: _rec_copy(y_ref, pos_ref[r], ybuf.at[0], r, sem.at[0], X_ROWS))

    drain(slot)
    nxt = jnp.minimum(i + 1, last) * TD
    yb = ybuf.at[slot]
    part = TD // COMBINE_PARTS
    for q in range(COMBINE_PARTS):
        rows = slice(q * part, (q + 1) * part)
        y = jnp.concatenate([yb[pl.ds(q * part * X_ROWS + j, part, stride=X_ROWS), :] for j in range(X_ROWS)],
                            axis=1)
        x2 = x1_ref[rows, :] + y
        h3 = _rms(x2, gple_ref[...]).astype(BF16)
        x3 = x2 + _sigmoid(_dot(h3, wpg_ref[...])) * _dot(p_ref[rows, :].astype(BF16), wpp_ref[...])
        o_ref[rows, :] = _rms(x3, gfin_ref[...])
        for r in range(q * part, (q + 1) * part):
            _rec_copy(y_ref, pos_ref[nxt + r], ybuf.at[1 - slot], r, sem.at[1 - slot], X_ROWS).start(
                priority=r % 2)

    @pl.when(i == last)
    def _():
        drain(1 - slot)


def _combine(pos, y_sorted, x1, p2, gple, wpg, wpp, gfin):
    t = x1.shape[0]
    row = lambda w: pl.BlockSpec((TD, w), lambda i, pos_r: (i, 0))
    const = lambda a: pl.BlockSpec(a.shape, lambda i, pos_r: (0,) * a.ndim, pipeline_mode=pl.Buffered(1))
    return pl.pallas_call(
        _combine_kernel,
        out_shape=jax.ShapeDtypeStruct((t, D_MODEL), F32),
        grid_spec=pltpu.PrefetchScalarGridSpec(
            num_scalar_prefetch=1,
            grid=(t // TD,),
            in_specs=[pl.BlockSpec(memory_space=pl.ANY), row(D_MODEL), row(PLE_DIM),
                      const(gple), const(wpg), const(wpp), const(gfin)],
            out_specs=row(D_MODEL),
            scratch_shapes=[pltpu.VMEM((2, TD * X_ROWS, LANES), F32), pltpu.SemaphoreType.DMA((2,))],
        ),
        compiler_params=pltpu.CompilerParams(dimension_semantics=("arbitrary",), vmem_limit_bytes=VMEM_LIMIT),
        name="combine",
    )(pos, y_sorted, x1, p2, gple, wpg, wpp, gfin)


def _routing_tables(slab, hist8):
    t = slab.shape[0]
    nt = t // TD
    n_tiles = t // TM + N_BUCKETS + -(-nt * N_BUCKETS * (RUN_CHUNK - 1) // TM)
    n_tiles += n_tiles % MOE_TILES
    hist = hist8.reshape(nt, SUBLANES, LANES)[:, 0, :]
    runlen = jnp.ceil(hist / RUN_CHUNK) * RUN_CHUNK
    counts = jnp.sum(runlen, axis=0)
    padded = jnp.ceil(counts / TM) * TM
    ends = jnp.cumsum(padded)
    first = (ends - padded)[None, :] + jnp.cumsum(runlen, axis=0) - runlen
    lane = jnp.arange(N_BUCKETS, dtype=F32)[None, :]
    mine = lane == slab[:, RT_BUCKET:RT_BUCKET + 1]
    pos = jnp.sum(jnp.where(mine, jnp.repeat(first[:, :N_BUCKETS], TD, axis=0), 0.0), axis=1) + slab[:, RT_RANK]
    starts = jnp.arange(n_tiles, dtype=F32) * TM
    tile_bucket = jnp.minimum(jnp.sum(ends[None, :N_BUCKETS] <= starts[:, None], axis=1), N_BUCKETS - 1)
    group, pair = tile_bucket // 6, tile_bucket % 6
    e_lo = EXP_PER_GROUP * group + jnp.array([0, 0, 0, 1, 1, 2], jnp.int32)[pair]
    e_hi = EXP_PER_GROUP * group + jnp.array([1, 2, 3, 2, 3, 3], jnp.int32)[pair]
    tile_e = jnp.concatenate([e_lo, e_hi]).astype(jnp.int32)
    n_used = (ends[N_BUCKETS - 1] / TM).astype(jnp.int32).reshape(1)
    tail = jnp.concatenate([(ends - padded + counts)[:N_BUCKETS], ((padded - counts) / RUN_CHUNK)[:N_BUCKETS],
                            n_used.astype(F32)]).astype(jnp.int32)
    nb = N_BUCKETS
    nch = runlen[:, :nb] / RUN_CHUNK
    cum = jnp.cumsum(nch, axis=1)
    q = jnp.arange(MAX_CHUNKS, dtype=F32)[None, :, None]
    of_b = jnp.arange(nb)[None, None, :] == jnp.minimum(jnp.sum(cum[:, None, :] <= q, axis=2), nb - 1)[..., None]
    pick = lambda a: jnp.sum(jnp.where(of_b, a[:, None, :nb], 0.0), axis=2)
    within = RUN_CHUNK * (q[..., 0] - pick(cum - nch))
    local = jnp.cumsum(runlen, axis=1) - runlen
    chunks = jnp.concatenate([cum[:, nb - 1], (pick(local) + within).reshape(-1),
                              (pick(first) + within).reshape(-1)]).astype(jnp.int32)
    return pos.astype(jnp.int32), chunks, tile_e, n_used, tail, n_tiles * TM


def _pack_inproj_weights(w):
    d = w.shape[0]
    qw = NSA_HEADS * NSA_HD
    kvw = NSA_KV * NSA_HD
    o = 0
    wq = w[:, o:o + qw]; o += qw
    wkc = w[:, o:o + kvw]; o += kvw
    wvc = w[:, o:o + kvw]; o += kvw
    wks = w[:, o:o + kvw]; o += kvw
    wvs = w[:, o:o + kvw]; o += kvw
    wkw = w[:, o:o + kvw]; o += kvw
    wvw = w[:, o:o + kvw]; o += kvw
    wga = w[:, o:o + 3 * NSA_HEADS]; o += 3 * NSA_HEADS
    wqkb = w[:, o:o + 2 * ML_WIDTH]; o += 2 * ML_WIDTH
    wvb = w[:, o:o + ML_WIDTH]; o += ML_WIDTH
    wob = w[:, o:o + ML_WIDTH]; o += ML_WIDTH
    wif = w[:, o:o + 2 * ML_HEADS]; o += 2 * ML_HEADS
    wmg = w[:, o:o + 2 * D_MODEL]
    wsm = jnp.concatenate([wga, wif, jnp.zeros((d, LANES - 3 * NSA_HEADS - 2 * ML_HEADS), w.dtype)], axis=1)
    wcat = jnp.concatenate([wq, wkc, wvc, wks, wkw, wsm, wqkb, wvb, wob, wmg], axis=1).astype(BF16)
    wtr = jnp.concatenate([wvs, wvw, wsm], axis=1).T.astype(BF16)
    return wcat, wtr


def _rope_tables(positions):
    half = ROPE_DIM // 2
    inv = ROPE_THETA ** (-jnp.arange(0, ROPE_DIM, 2, dtype=F32) / ROPE_DIM)
    ang = positions.astype(F32).reshape(-1, 1) * inv[None, :]
    cs = jnp.concatenate([jnp.cos(ang), jnp.sin(ang)], axis=1)
    d = np.arange(LANES) % NSA_HD
    spread = np.zeros((2 * half, 3 * LANES), np.float32)
    lanes = np.arange(LANES)
    rot = d < ROPE_DIM
    spread[(d % half)[rot], lanes[rot]] = 1.0
    hi = (d >= half) & rot
    spread[half + (d % half)[hi], LANES + lanes[hi]] = 1.0
    lo = d < half
    spread[half + (d % half)[lo], 2 * LANES + lanes[lo]] = -1.0
    unrotated = (~rot).astype(np.float32).reshape(1, LANES)
    return cs, jnp.asarray(spread, BF16), jnp.asarray(unrotated)


def _pack_compress_weights(w1, w2, pe):
    half = CMP_LEN // 2
    w1r = w1.reshape(2, half, NSA_HD, CMP_HIDDEN)
    outs = []
    for part in range(2):
        wb = w1r[part].astype(BF16)
        zb = jnp.zeros_like(wb)
        wp = jnp.stack([jnp.stack([wb, zb], axis=2), jnp.stack([zb, wb], axis=2)], axis=1)
        outs.append(wp.reshape(half * NSA_KV * NSA_HD, NSA_KV * CMP_HIDDEN))
    pe8 = jnp.broadcast_to(pe.reshape(1, CMP_LEN * NSA_HD), (SUBLANES, CMP_LEN * NSA_HD)).astype(BF16)
    return outs[0], outs[1], pe8, w1.astype(BF16)


def _stages(x, p, positions, g_mix, w_in, b_if, w_ck1, w_ck2, pe_ck, w_cv1, w_cv2, pe_cv, w_conv, b_conv, g_hn, w_pa, w_pb, w_out, g_ffn, w_rg, b_rg, w_re, b_re, w_e13, w_e2, g_ple, w_pg, w_pp, g_final):
    b, s, d = x.shape
    t = b * s
    cs, spread, unrot = _rope_tables(positions)
    assert w_in.shape[0] == 1, "the final norm is fused into the layer's last kernel: single-layer problem only"
    for i in range(w_in.shape[0]):
        x2 = x.reshape(t, d)
        wcat, wtr = _pack_inproj_weights(w_in[i])
        (qpad, kc_tok, vc_tok, ks, kw, vst, vwt, sm, smt, qkb, vb, ob, mg) = _inproj(
            x2, g_mix[i].reshape(1, d), wcat, wtr, cs, spread, unrot)
        wka, wkb, pek, w1k = _pack_compress_weights(w_ck1[i], w_ck2[i], pe_ck[i])
        wva, wvb, pev, w1v = _pack_compress_weights(w_cv1[i], w_cv2[i], pe_cv[i])
        zpad = jnp.zeros((CMP_HIDDEN, NSA_HD), F32)
        w2k = jnp.stack([jnp.concatenate([w_ck2[i], zpad], axis=1),
                         jnp.concatenate([zpad, w_ck2[i]], axis=1)]).astype(BF16)
        w2vt = w_cv2[i].T.astype(BF16)
        nrow = s // CMP_STRIDE
        rk = kc_tok.reshape(b, nrow, CMP_STRIDE * LANES)
        rv = vc_tok.reshape(b, nrow, CMP_STRIDE * LANES)
        kcb, vct = _compress(rk, rv, wka, wkb, wva, wvb, pek, pev, w1k, w1v, w2k, w2vt)
        ya = _nsa(qpad, kcb, vct, ks, kw, vst, vwt, smt, b, s)
        bif = b_if[i].astype(F32)
        bifc = jnp.zeros((1, LANES), F32).at[0, SM_I:SM_I + 2 * ML_HEADS].set(bif)
        bifr = bifc.reshape(LANES, 1)
        yb = _mlstm(qkb, vb, ob, sm, smt, w_conv[i], b_conv[i].reshape(1, -1), bifc, bifr,
                    g_hn[i].reshape(1, -1), b, s)
        wr = jnp.concatenate([w_re[i], w_rg[i], jnp.zeros((d, LANES - N_EXPERTS - N_GROUPS), F32)], axis=1)
        wr_hi = wr.astype(BF16)
        wr = jnp.concatenate([wr_hi, (wr - wr_hi.astype(F32)).astype(BF16)], axis=1)
        br =jnp.concatenate([b_re[i], b_rg[i], jnp.zeros((LANES - N_EXPERTS - N_GROUPS,), F32)]).reshape(1, LANES)
        x1, h2, slab, hist8 = _merge(ya, yb, mg, x2, w_pa[i].astype(BF16), w_pb[i].astype(BF16),
                                     w_out[i].astype(BF16), g_ffn[i].reshape(1, d), wr, br)
        pos, chunks, tile_e, n_used, tail, n_slots = _routing_tables(slab, hist8)
        hx_sorted = _dispatch(chunks, tail, h2, slab, n_slots)
        y_sorted = _moe(tile_e, n_used, hx_sorted, w_e13[i].astype(BF16), w_e2[i].astype(BF16))
        out = _combine(pos, y_sorted, x1, p[i].reshape(t, PLE_DIM), g_ple[i].reshape(1, d), w_pg[i].astype(BF16),
                       w_pp[i].astype(BF16), g_final.reshape(1, d))
        x = out.reshape(b, s, d)
    return dict(out=x, qpad=qpad, ks=ks, kcb=kcb, vct=vct, y_a=ya, y_b=yb, x1=x1, pos=pos)


def kernel(x, p, positions, g_mix, w_in, b_if, w_ck1, w_ck2, pe_ck, w_cv1, w_cv2, pe_cv, w_conv, b_conv, g_hn, w_pa, w_pb, w_out, g_ffn, w_rg, b_rg, w_re, b_re, w_e13, w_e2, g_ple, w_pg, w_pp, g_final):
    return _stages(x, p, positions, g_mix, w_in, b_if, w_ck1, w_ck2, pe_ck, w_cv1, w_cv2, pe_cv, w_conv, b_conv, g_hn,
                   w_pa, w_pb, w_out, g_ffn, w_rg, b_rg, w_re, b_re, w_e13, w_e2, g_ple, w_pg, w_pp, g_final)["out"]
```

```python
import math

import numpy as np
import jax
import jax.numpy as jnp
from jax import lax
from jax.experimental import pallas as pl
from jax.experimental.pallas import tpu as pltpu

F32 = jnp.float32
BF16 = jnp.bfloat16

EPS = 1e-6
NEG = -1e30

D_MODEL = 1024
PLE_DIM = 256
NSA_HEADS = 8
NSA_KV = 2
NSA_HPG = NSA_HEADS // NSA_KV
NSA_HD = 64
CMP_LEN = 32
CMP_STRIDE = 16
CMP_HIDDEN = 256
SEL_BLOCK = 64
SEL_TOPK = 16
SEL_FORCE = 1000.0
WINDOW = 512
ROPE_THETA = 500000.0
ROPE_DIM = NSA_HD // 4
ML_HEADS = 4
ML_HD = 128
ML_WIDTH = ML_HEADS * ML_HD
CONV_W = 4
N_GROUPS = 4
EXP_PER_GROUP = 4
N_EXPERTS = N_GROUPS * EXP_PER_GROUP
D_EXPERT = 256

LANES = 128
SUBLANES = 8
QT = 128
KC = 128
SEL_GROUP = 512
VT_PAD = 16
VT_ROWS = NSA_HD + VT_PAD
ML_CHUNK = 128
ML_BLOCK = 256
TD = 512
TM = 256
VMEM_LIMIT = 56 * 1024 * 1024

_NT = (((1,), (1,)), ((), ()))
_TN = (((0,), (0,)), ((), ()))

SM_GATE = 0
SM_I = 3 * NSA_HEADS
SM_F = SM_I + ML_HEADS


def _dot(a, b):
    return jnp.dot(a, b, preferred_element_type=F32)


def _dot_nt(a, b):
    return lax.dot_general(a, b, _NT, preferred_element_type=F32)


def _split3(x):
    hi = x.astype(BF16)
    r1 = x - hi.astype(F32)
    mid = r1.astype(BF16)
    lo = (r1 - mid.astype(F32)).astype(BF16)
    return hi, mid, lo


def _rms(x, g):
    return x * lax.rsqrt(jnp.mean(x * x, axis=-1, keepdims=True) + EPS) * g


def _sigmoid(x):
    return 0.5 + 0.5 * jnp.tanh(0.5 * x)


def _const_spec(shape):
    nd = len(shape)
    return pl.BlockSpec(shape, lambda *_: (0,) * nd, pipeline_mode=pl.Buffered(1))


_C_Q = 0
_C_KC = _C_Q + NSA_HEADS * NSA_HD
_C_VC = _C_KC + LANES
_C_KS = _C_VC + LANES
_C_KW = _C_KS + LANES
_C_SM = _C_KW + LANES
_C_QKB = _C_SM + LANES
_C_VB = _C_QKB + 2 * ML_WIDTH
_C_OB = _C_VB + ML_WIDTH
_C_MG = _C_OB + ML_WIDTH
_C_END = _C_MG + 2 * D_MODEL


def _inproj_kernel(x_ref, g_ref, w_ref, wt_ref, cs_ref, spread_ref, unrot_ref,
                   q_ref, kc_ref, vc_ref, ks_ref, kw_ref, vst_ref, vwt_ref, sm_ref, smt_ref,
                   qkb_ref, vb_ref, ob_ref, mg_ref):
    hn = _rms(x_ref[...], g_ref[...]).astype(BF16)
    tables = sum(_dot(part, spread_ref[...]) for part in _split3(cs_ref[...]))
    rc = tables[:, 0:LANES] + unrot_ref[...]
    rp = tables[:, LANES:2 * LANES]
    rm = tables[:, 2 * LANES:3 * LANES]

    def rope(z):
        half = ROPE_DIM // 2
        return z * rc + pltpu.roll(z, half, 1) * rp + pltpu.roll(z, LANES - half, 1) * rm

    scale = NSA_HD ** -0.5 * math.log2(math.e)
    for h in range(NSA_HEADS * NSA_HD // LANES):
        z = _dot(hn, w_ref[:, _C_Q + h * LANES:_C_Q + (h + 1) * LANES])
        q_ref[:, h * LANES:(h + 1) * LANES] = (rope(z) * scale).astype(BF16)
    kc_ref[...] = rope(_dot(hn, w_ref[:, _C_KC:_C_KC + LANES])).astype(BF16)
    vc_ref[...] = _dot(hn, w_ref[:, _C_VC:_C_VC + LANES]).astype(BF16)
    ks_ref[...] = rope(_dot(hn, w_ref[:, _C_KS:_C_KS + LANES])).astype(BF16)
    kw_ref[...] = rope(_dot(hn, w_ref[:, _C_KW:_C_KW + LANES])).astype(BF16)
    sm_ref[...] = _dot(hn, w_ref[:, _C_SM:_C_SM + LANES])
    for c0 in range(0, 2 * ML_WIDTH, 512):
        qkb_ref[:, c0:c0 + 512] = _dot(hn, w_ref[:, _C_QKB + c0:_C_QKB + c0 + 512]).astype(BF16)
    vb_ref[...] = _dot(hn, w_ref[:, _C_VB:_C_VB + ML_WIDTH]).astype(BF16)
    ob_ref[...] = _dot(hn, w_ref[:, _C_OB:_C_OB + ML_WIDTH]).astype(BF16)
    for c0 in range(0, 2 * D_MODEL, 512):
        mg_ref[:, c0:c0 + 512] = _dot(hn, w_ref[:, _C_MG + c0:_C_MG + c0 + 512]).astype(BF16)
    zt = _dot_nt(wt_ref[...], hn)
    ones_rows = (lax.broadcasted_iota(jnp.int32, (VT_PAD, KC), 0) == 0).astype(BF16)
    for i in range(TD // KC):
        for ref, r0 in ((vst_ref, 0), (vwt_ref, LANES)):
            zc = zt[r0:r0 + LANES, i * KC:(i + 1) * KC].astype(BF16)
            ref[i] = jnp.concatenate([piece for g in range(NSA_KV)
                                      for piece in (zc[g * NSA_HD:(g + 1) * NSA_HD, :], ones_rows)], axis=0)
    smt_ref[...] = zt[2 * LANES:3 * LANES, :]


def _inproj(x2, g_mix, wcat, wtr, cs, spread, unrot):
    t = x2.shape[0]
    row = lambda w: pl.BlockSpec((TD, w), lambda i: (i, 0))
    out_shape = [
        jax.ShapeDtypeStruct((t, NSA_HEADS * NSA_HD), BF16),
        jax.ShapeDtypeStruct((t, LANES), BF16),
        jax.ShapeDtypeStruct((t, LANES), BF16),
        jax.ShapeDtypeStruct((t, LANES), BF16),
        jax.ShapeDtypeStruct((t, LANES), BF16),
        jax.ShapeDtypeStruct((t // KC, NSA_KV * VT_ROWS, KC), BF16),
        jax.ShapeDtypeStruct((t // KC, NSA_KV * VT_ROWS, KC), BF16),
        jax.ShapeDtypeStruct((t, LANES), F32),
        jax.ShapeDtypeStruct((LANES, t), F32),
        jax.ShapeDtypeStruct((t, 2 * ML_WIDTH), BF16),
        jax.ShapeDtypeStruct((t, ML_WIDTH), BF16),
        jax.ShapeDtypeStruct((t, ML_WIDTH), BF16),
        jax.ShapeDtypeStruct((t, 2 * D_MODEL), BF16),
    ]
    chunk3 = pl.BlockSpec((TD // KC, NSA_KV * VT_ROWS, KC), lambda i: (i, 0, 0))
    out_specs = [row(NSA_HEADS * NSA_HD), row(LANES), row(LANES), row(LANES), row(LANES), chunk3, chunk3,
                 row(LANES), pl.BlockSpec((LANES, TD), lambda i: (0, i)),
                 row(2 * ML_WIDTH), row(ML_WIDTH), row(ML_WIDTH), row(2 * D_MODEL)]
    return pl.pallas_call(
        _inproj_kernel,
        out_shape=out_shape,
        grid=(t // TD,),
        in_specs=[row(D_MODEL), _const_spec((1, D_MODEL)), _const_spec((D_MODEL, _C_END)),
                  _const_spec((3 * LANES, D_MODEL)), row(cs.shape[1]), _const_spec(spread.shape),
                  _const_spec(unrot.shape)],
        out_specs=out_specs,
        compiler_params=pltpu.CompilerParams(dimension_semantics=("arbitrary",), vmem_limit_bytes=VMEM_LIMIT),
        name="inproj",
    )(x2, g_mix, wcat, wtr, cs, spread, unrot)


def _gelu_tanh(x):
    return 0.5 * x * (1.0 + jnp.tanh(math.sqrt(2.0 / math.pi) * (x + 0.044715 * x * x * x)))


def _compress_kernel(rk_ref, rv_ref, wka_ref, wkb_ref, wva_ref, wvb_ref, pek_ref, pev_ref,
                     w1k_ref, w1v_ref, w2k_ref, w2vt_ref, kc_ref, vct_ref):
    nrow = rk_ref.shape[0]

    def hidden(r_ref, wa_ref, wb_ref, pe_ref, w1_ref):
        r = r_ref[...]
        ha = _dot(r, wa_ref[...])
        hb = _dot(r, wb_ref[...])
        hb = pltpu.roll(hb, nrow - 1, 0)
        c = _dot(pe_ref[...], w1_ref[...])[0:1, :]
        return [_gelu_tanh(ha[:, g * CMP_HIDDEN:(g + 1) * CMP_HIDDEN] + hb[:, g * CMP_HIDDEN:(g + 1) * CMP_HIDDEN] + c).astype(BF16)
                for g in range(NSA_KV)]

    ak = hidden(rk_ref, wka_ref, wkb_ref, pek_ref, w1k_ref)
    kc_ref[...] = (_dot(ak[0], w2k_ref[0]) + _dot(ak[1], w2k_ref[1])).astype(BF16)
    av = hidden(rv_ref, wva_ref, wvb_ref, pev_ref, w1v_ref)
    for g in range(NSA_KV):
        vct_ref[g * NSA_HD:(g + 1) * NSA_HD, :] = _dot_nt(w2vt_ref[...], av[g]).astype(BF16)


def _compress(rk, rv, wka, wkb, wva, wvb, pek, pev, w1k, w1v, w2k, w2vt):
    b, nrow, width = rk.shape
    blk = pl.BlockSpec((None, nrow, width), lambda i: (i, 0, 0))
    return pl.pallas_call(
        _compress_kernel,
        out_shape=[jax.ShapeDtypeStruct((b, nrow, LANES), BF16),
                   jax.ShapeDtypeStruct((b, LANES, nrow), BF16)],
        grid=(b,),
        in_specs=[blk, blk] + [_const_spec(a.shape) for a in (wka, wkb, wva, wvb, pek, pev, w1k, w1v, w2k, w2vt)],
        out_specs=[pl.BlockSpec((None, nrow, LANES), lambda i: (i, 0, 0)),
                   pl.BlockSpec((None, LANES, nrow), lambda i: (i, 0, 0))],
        compiler_params=pltpu.CompilerParams(dimension_semantics=("arbitrary",), vmem_limit_bytes=VMEM_LIMIT),
        name="compress",
    )(rk, rv, wka, wkb, wva, wvb, pek, pev, w1k, w1v, w2k, w2vt)


def _nsa_kernel(q_ref, kc_ref, vct_ref, ks_ref, kw_ref, vst_ref, vwt_ref, smt_ref, o_ref, bias_scr, sx_scr, sy_scr):
    c = pl.program_id(1)
    t0 = c * QT
    ncmp = kc_ref.shape[0]
    nsel = bias_scr.shape[0]
    nw = WINDOW // KC + 1
    gw = NSA_HPG * QT
    width = NSA_KV * gw

    def per_group(x):
        return [x[:, g * gw:(g + 1) * gw] for g in range(NSA_KV)]

    def pv(vt, p):
        rows = vt.shape[0] // NSA_KV
        pb = p.astype(BF16)
        return jnp.concatenate([_dot(vt[g * rows:(g + 1) * rows, :], pg) for g, pg in enumerate(per_group(pb))],
                               axis=1)

    def normalised(acc):
        return acc[0:NSA_HD, :] / acc[NSA_HD:NSA_HD + 1, :]

    low_half = lax.broadcasted_iota(jnp.int32, (1, LANES), 1) < NSA_HD
    q_heads = []
    for h in range(NSA_HEADS):
        pair = q_ref[:, (h // 2) * LANES:(h // 2 + 1) * LANES].astype(F32)
        want_low = h // NSA_HPG == 0
        if (h % 2 == 0) != want_low:
            pair = pltpu.roll(pair, NSA_HD, 1)
        q_heads.append(jnp.where(low_half if want_low else ~low_half, pair, 0.0).astype(BF16))
    qs = jnp.concatenate(q_heads, axis=0)
    u_row = lax.broadcasted_iota(jnp.int32, (1, width), 1) % QT
    t_row = t0 + u_row
    r_kc = lax.broadcasted_iota(jnp.int32, (KC, 1), 0)

    n_grp = ks_ref.shape[0] // SEL_GROUP
    n_full = lax.shift_right_logical(t0, int(math.log2(SEL_GROUP)))

    def qk_group(j):
        return _dot_nt(ks_ref[pl.ds(pl.multiple_of(j * SEL_GROUP, SEL_GROUP), SEL_GROUP), :], qs)

    sc = _dot_nt(kc_ref[...], qs)

    w_slabs, w_chunks = [], []
    for i in range(nw):
        jj = c - (nw - 1) + i
        jc = jnp.maximum(jj, 0)
        si = _dot_nt(kw_ref[pl.ds(pl.multiple_of(jc * KC, KC), KC), :], qs)
        if i == 0:
            keep = jnp.where(jj >= 0, r_kc, -1) > u_row
        elif i == nw - 1:
            keep = r_kc <= u_row
        else:
            keep = jj >= 0
        w_slabs.append(jnp.where(keep, si, NEG))
        w_chunks.append(jc)

    n_col = lax.broadcasted_iota(jnp.int32, (ncmp, 1), 0)
    last_tok = jnp.where(n_col < ncmp - 1, CMP_STRIDE * n_col + (CMP_LEN - 1), jnp.iinfo(jnp.int32).max)
    s = jnp.where(last_tok <= t_row, sc, NEG)
    m = jnp.max(s, axis=0, keepdims=True)
    e = jnp.exp2(s - m)
    anyv = (t_row >= CMP_LEN - 1).astype(F32)
    p = e * (anyv / jnp.sum(e, axis=0, keepdims=True))
    o_cmp = pv(vct_ref[...], p)

    psums = []
    for pg in per_group(p):
        acc_p = pg[:, 0:QT]
        for h in range(1, NSA_HPG):
            acc_p = acc_p + pg[:, h * QT:(h + 1) * QT]
        psums.append(acc_p)
    psum = jnp.concatenate(psums, axis=1)
    nq2 = NSA_KV * QT
    s_col = lax.broadcasted_iota(jnp.int32, (nsel, 1), 0)
    n_lane = lax.broadcasted_iota(jnp.int32, (1, ncmp), 1)
    ov = ((CMP_STRIDE * n_lane < SEL_BLOCK * (s_col + 1)) & (CMP_STRIDE * n_lane + (CMP_LEN - 1) >= SEL_BLOCK * s_col)
          ).astype(BF16)
    imp = sum(_dot(ov, part) for part in _split3(psum))

    s_diag = qk_group(n_full)
    s_own = _dot_nt(ks_ref[pl.ds(pl.multiple_of(t0, KC), KC), :], qs)
    sx_scr[...] = qk_group(0)

    mxw = w_slabs[0]
    for sl in w_slabs[1:]:
        mxw = jnp.maximum(mxw, sl)
    mw = jnp.max(mxw, axis=0, keepdims=True)
    acc_w = jnp.zeros((VT_ROWS, width), F32)
    for sl, jc in zip(w_slabs, w_chunks):
        acc_w = acc_w + pv(vwt_ref[jc], jnp.exp2(sl - mw))
    o_win = normalised(acc_w)

    t1 = t0 + lax.broadcasted_iota(jnp.int32, (1, nq2), 1) % QT
    cur = lax.shift_right_logical(t1, 6)
    forced = (s_col == 0) | (s_col == cur) | (s_col == cur - 1)
    valid = SEL_BLOCK * s_col <= t1
    val = jnp.where(valid, jnp.where(forced, imp + SEL_FORCE, imp), NEG)
    sub = 8
    r_sub = lax.broadcasted_iota(jnp.int32, (sub, 1), 0)
    blocks = [val[r * sub:(r + 1) * sub, :] for r in range(nsel // sub)]
    ranks = [jnp.zeros((sub, nq2), F32) for _ in blocks]
    for i in range(nsel):
        vi = val[i:i + 1, :]
        for r, blk in enumerate(blocks):
            if i < r * sub:
                beats = vi >= blk
            elif i >= (r + 1) * sub:
                beats = vi > blk
            else:
                beats = (vi > blk) | ((vi == blk) & (r_sub > i - r * sub))
            ranks[r] = ranks[r] + jnp.where(beats, 1.0, 0.0)
    bias = jnp.where(jnp.concatenate(ranks, axis=0) < SEL_TOPK, 0.0, NEG).astype(F32)
    bias_scr[...] = jnp.concatenate([bias[:, g * QT:(g + 1) * QT] for g in range(NSA_KV) for _ in range(NSA_HPG)],
                                    axis=1)

    blk_per_grp = SEL_GROUP // SEL_BLOCK
    chunk_per_grp = SEL_GROUP // KC
    blk_per_chunk = KC // SEL_BLOCK

    def sel_update(j, sj, carry):
        m_o, acc = carry
        brows = [bias_scr[pl.ds(blk_per_grp * j + i, 1), :] for i in range(blk_per_grp)]

        def block(i):
            return sj[i * SEL_BLOCK:(i + 1) * SEL_BLOCK, :]

        mx = None
        for i in range(blk_per_grp):
            sl = block(i) + brows[i]
            mx = sl if mx is None else jnp.maximum(mx, sl)
        m_n = jnp.maximum(m_o, jnp.max(mx, axis=0, keepdims=True))
        a = jnp.exp2(m_o - m_n)
        acc = a * acc
        for ci in range(chunk_per_grp):
            parts = [jnp.exp2(block(i) + (brows[i] - m_n))
                     for i in range(blk_per_chunk * ci, blk_per_chunk * (ci + 1))]
            acc = acc + pv(vst_ref[chunk_per_grp * j + ci], jnp.concatenate(parts, axis=0))
        return m_n, acc

    def seed_state():
        first_own = blk_per_chunk * (c % chunk_per_grp)
        brows = [bias_scr[pl.ds(blk_per_grp * n_full + i, 1), :] + jnp.where(i < first_own, 0.0, NEG)
                 for i in range(blk_per_grp)]
        own = jnp.where(r_kc <= u_row, s_own, NEG)
        mx = jnp.maximum(own[0:SEL_BLOCK, :], own[SEL_BLOCK:KC, :])
        for i in range(blk_per_grp):
            mx = jnp.maximum(mx, s_diag[i * SEL_BLOCK:(i + 1) * SEL_BLOCK, :] + brows[i])
        m_n = jnp.max(mx, axis=0, keepdims=True)
        acc = pv(vst_ref[c], jnp.exp2(own - m_n))
        for ci in range(chunk_per_grp):
            parts = [jnp.exp2(s_diag[i * SEL_BLOCK:(i + 1) * SEL_BLOCK, :] + (brows[i] - m_n))
                     for i in range(blk_per_chunk * ci, blk_per_chunk * (ci + 1))]
            acc = acc + pv(vst_ref[chunk_per_grp * n_full + ci], jnp.concatenate(parts, axis=0))
        return m_n, acc

    seeded = seed_state()

    def pair_body(jp, carry):
        ja, jb = 2 * jp, 2 * jp + 1
        sy_scr[...] = qk_group(jb)
        carry = sel_update(ja, sx_scr, carry)
        sx_scr[...] = qk_group(jnp.minimum(ja + 2, n_grp - 1))
        return sel_update(jb, sy_scr, carry)

    n_pairs = lax.shift_right_logical(n_full, 1)
    carry = lax.fori_loop(0, n_pairs, pair_body, seeded)
    _, acc_s = lax.cond(n_full - 2 * n_pairs == 1, lambda cr: sel_update(n_full - 1, sx_scr, cr), lambda cr: cr,
                        carry)
    o_sel = normalised(acc_s)

    def gate_row(br):
        rows = [smt_ref[SM_GATE + 3 * h + br:SM_GATE + 3 * h + br + 1, :] for h in range(NSA_HEADS)]
        return _sigmoid(jnp.concatenate(rows, axis=1))

    o_t = gate_row(0) * o_cmp + gate_row(1) * o_sel + gate_row(2) * o_win
    for pr in range(NSA_HEADS // 2):
        xp = jnp.concatenate([o_t[:, (2 * pr) * QT:(2 * pr + 1) * QT], o_t[:, (2 * pr + 1) * QT:(2 * pr + 2) * QT]], axis=0)
        o_ref[:, pr * LANES:(pr + 1) * LANES] = xp.T.astype(BF16)


def _nsa(qpad, kcb, vct, ks, kw, vst, vwt, smt, b, s):
    nq = s // QT
    ncmp = kcb.shape[1]
    return pl.pallas_call(
        _nsa_kernel,
        out_shape=jax.ShapeDtypeStruct((b * s, NSA_HEADS * NSA_HD), BF16),
        grid=(b, nq),
        in_specs=[
            pl.BlockSpec((QT, NSA_HEADS * NSA_HD), lambda bi, c: (bi * nq + c, 0)),
            pl.BlockSpec((None, ncmp, LANES), lambda bi, c: (bi, 0, 0)),
            pl.BlockSpec((None, NSA_KV * NSA_HD, ncmp), lambda bi, c: (bi, 0, 0)),
            pl.BlockSpec((s, LANES), lambda bi, c: (bi, 0)),
            pl.BlockSpec((s, LANES), lambda bi, c: (bi, 0)),
            pl.BlockSpec((s // KC, NSA_KV * VT_ROWS, KC), lambda bi, c: (bi, 0, 0)),
            pl.BlockSpec((s // KC, NSA_KV * VT_ROWS, KC), lambda bi, c: (bi, 0, 0)),
            pl.BlockSpec((LANES, QT), lambda bi, c: (0, bi * nq + c)),
        ],
        out_specs=pl.BlockSpec((QT, NSA_HEADS * NSA_HD), lambda bi, c: (bi * nq + c, 0)),
        scratch_shapes=[pltpu.VMEM((s // SEL_BLOCK, NSA_HEADS * QT), F32),
                        pltpu.VMEM((SEL_GROUP, NSA_HEADS * QT), F32),
                        pltpu.VMEM((SEL_GROUP, NSA_HEADS * QT), F32)],
        compiler_params=pltpu.CompilerParams(dimension_semantics=("arbitrary", "arbitrary"),
                                             vmem_limit_bytes=VMEM_LIMIT),
        name="nsa",
    )(qpad, kcb, vct, ks, kw, vst, vwt, smt)


def _log_sigmoid(x):
    return jnp.minimum(x, 0.0) - jnp.log(1.0 + jnp.exp(-jnp.abs(x)))


def _mlstm_kernel(qk_ref, v_ref, og_ref, sm_ref, smt_ref, wc_ref, bc_ref, bifc_ref, bifr_ref, ghn_ref,
                  y_ref, tail_scr, ct_scr, n_scr, m_scr):
    lc = ML_CHUNK

    @pl.when(pl.program_id(1) == 0)
    def _():
        tail_scr[...] = jnp.zeros_like(tail_scr)
        ct_scr[...] = jnp.zeros_like(ct_scr)
        n_scr[...] = jnp.zeros_like(n_scr)
        m_scr[...] = jnp.zeros_like(m_scr)

    u = qk_ref[...]
    tail = tail_scr[...]
    rr8 = lax.broadcasted_iota(jnp.int32, (SUBLANES, 1), 0)
    sr = lax.broadcasted_iota(jnp.int32, (ML_BLOCK, ML_BLOCK), 0)
    sc_ = lax.broadcasted_iota(jnp.int32, (ML_BLOCK, ML_BLOCK), 1)
    y = bc_ref[...] + wc_ref[CONV_W - 1:CONV_W, :] * u.astype(F32)
    for k in range(1, CONV_W):
        down = _dot((sr - sc_ == k).astype(BF16), u)
        head = jnp.where(rr8 < k, pltpu.roll(tail, k, 0), down[0:SUBLANES, :])
        y = y + wc_ref[CONV_W - 1 - k:CONV_W - k, :] * jnp.concatenate([head, down[SUBLANES:, :]], axis=0)
    tail_scr[...] = u[ML_BLOCK - SUBLANES:ML_BLOCK, :].astype(F32)
    qkc = y * _sigmoid(y)
    q_all = qkc[:, 0:ML_WIDTH].astype(BF16)
    k_all = (qkc[:, ML_WIDTH:2 * ML_WIDTH] * (ML_HD ** -0.5)).astype(BF16)

    ifc = sm_ref[...] + bifc_ref[...]
    ifr = smt_ref[...] + bifr_ref[...]
    lfc = _log_sigmoid(ifc)
    lfr = _log_sigmoid(ifr)
    rr = lax.broadcasted_iota(jnp.int32, (lc, lc), 0)
    cc = lax.broadcasted_iota(jnp.int32, (lc, lc), 1)
    causal = rr >= cc
    tri_l = causal.astype(F32)
    tri_u = (rr <= cc).astype(F32)

    for ci in range(ML_BLOCK // lc):
        lo, hi = ci * lc, (ci + 1) * lc
        bc_all = jnp.dot(tri_l, lfc[lo:hi, :], preferred_element_type=F32, precision=lax.Precision.HIGHEST)
        br_all = jnp.dot(lfr[:, lo:hi], tri_u, preferred_element_type=F32, precision=lax.Precision.HIGHEST)
        heads = range(ML_HEADS)
        hsl = [slice(h * ML_HD, (h + 1) * ML_HD) for h in heads]
        bcol = [bc_all[:, SM_F + h:SM_F + h + 1] for h in heads]
        brow = [br_all[SM_F + h:SM_F + h + 1, :] for h in heads]
        icol = [ifc[lo:hi, SM_I + h:SM_I + h + 1] for h in heads]
        irow = [ifr[SM_I + h:SM_I + h + 1, lo:hi] for h in heads]
        mprev = [m_scr[h][:, 0:1] for h in heads]
        qh = [q_all[lo:hi, hsl[h]] for h in heads]
        kh = [k_all[lo:hi, hsl[h]] for h in heads]
        vh = [v_ref[lo:hi, hsl[h]] for h in heads]
        ct = [ct_scr[h] for h in heads]
        nrow = [n_scr[h] for h in heads]
        qk = [_dot_nt(qh[h], kh[h]) for h in heads]
        qc = [_dot(qh[h], ct[h].astype(BF16)) for h in heads]
        dmat = [jnp.where(causal, bcol[h] - brow[h] + irow[h], NEG) for h in heads]
        inter = [bcol[h] + mprev[h] for h in heads]
        mt = [jnp.maximum(jnp.max(dmat[h], axis=-1, keepdims=True), inter[h]) for h in heads]
        a = [jnp.exp(dmat[h] - mt[h]) * qk[h] for h in heads]
        dec = [jnp.exp(inter[h] - mt[h]) for h in heads]
        num = [_dot(a[h].astype(BF16), vh[h]) + dec[h] * qc[h] for h in heads]
        den = [jnp.sum(a[h], axis=-1, keepdims=True)
               + dec[h] * jnp.sum(qh[h].astype(F32) * nrow[h], axis=-1, keepdims=True) for h in heads]
        blast = [bcol[h][lc - 1:lc, :] for h in heads]
        mnew = [jnp.maximum(blast[h] + mprev[h], jnp.max(blast[h] - brow[h] + irow[h], axis=-1, keepdims=True))
                for h in heads]
        wprev = [jnp.exp(blast[h] + mprev[h] - mnew[h]) for h in heads]
        kwt = [kh[h].astype(F32) * jnp.exp(blast[h] - bcol[h] + icol[h] - mnew[h]) for h in heads]
        for h in heads:
            ct_scr[h] = wprev[h] * ct[h] + lax.dot_general(kwt[h].astype(BF16), vh[h], _TN,
                                                           preferred_element_type=F32)
            n_scr[h] = wprev[h] * nrow[h] + jnp.sum(kwt[h], axis=0, keepdims=True)
            m_scr[h] = jnp.broadcast_to(mnew[h], (1, LANES))
        hm = [num[h] / jnp.maximum(jnp.abs(den[h]), jnp.exp(-mt[h])) * _sigmoid(og_ref[lo:hi, hsl[h]].astype(F32))
              for h in heads]
        for h in heads:
            y_ref[lo:hi, hsl[h]] = _rms(hm[h], ghn_ref[:, hsl[h]]).astype(BF16)


def _mlstm(qkb, vb, ob, sm, smt, wconv, bconv, bifc, bifr, ghn, b, s):
    nb = s // ML_BLOCK
    row = lambda w: pl.BlockSpec((ML_BLOCK, w), lambda bi, j: (bi * nb + j, 0))
    return pl.pallas_call(
        _mlstm_kernel,
        out_shape=jax.ShapeDtypeStruct((b * s, ML_WIDTH), BF16),
        grid=(b, nb),
        in_specs=[row(2 * ML_WIDTH), row(ML_WIDTH), row(ML_WIDTH), row(LANES),
                  pl.BlockSpec((LANES, ML_BLOCK), lambda bi, j: (0, bi * nb + j)),
                  _const_spec(wconv.shape), _const_spec(bconv.shape), _const_spec(bifc.shape),
                  _const_spec(bifr.shape), _const_spec(ghn.shape)],
        out_specs=row(ML_WIDTH),
        scratch_shapes=[pltpu.VMEM((SUBLANES, 2 * ML_WIDTH), F32),
                        pltpu.VMEM((ML_HEADS, ML_HD, ML_HD), F32),
                        pltpu.VMEM((ML_HEADS, 1, ML_HD), F32),
                        pltpu.VMEM((ML_HEADS, 1, LANES), F32)],
        compiler_params=pltpu.CompilerParams(dimension_semantics=("arbitrary", "arbitrary"),
                                             vmem_limit_bytes=VMEM_LIMIT),
        name="mlstm",
    )(qkb, vb, ob, sm, smt, wconv, bconv, bifc, bifr, ghn)


RT_BUCKET = N_EXPERTS
RT_RANK = N_EXPERTS + 1
RT_WLO = N_EXPERTS + 2
RT_WHI = N_EXPERTS + 3
RT_LPOS = N_EXPERTS + 4
N_BUCKETS = N_GROUPS * 6
X_ROWS = D_MODEL // LANES
DMA_UNROLL = 8
RUN_CHUNK = SUBLANES
MAX_CHUNKS = TD // RUN_CHUNK + N_BUCKETS
SORT_ROWS = MAX_CHUNKS * RUN_CHUNK
SORTED_ROWS = SORT_ROWS + RUN_CHUNK
TERM_LANES = 32
COMBINE_PARTS = 2


def _merge_kernel(ya_ref, yb_ref, mg_ref, x_ref, wpa_ref, wpb_ref, wout_ref, gffn_ref, wr_ref, br_ref,
                  x1_ref, h2_ref, slab_ref, hist_ref):
    halves = [slice(i * (TD // 2), (i + 1) * (TD // 2)) for i in range(2)]
    pa = [_dot(ya_ref[hs, :], wpa_ref[...]) for hs in halves]
    pb = [_dot(yb_ref[hs, :], wpb_ref[...]) for hs in halves]
    mixed = [(_sigmoid(mg_ref[hs, 0:D_MODEL].astype(F32)) * pa[i]
              + _sigmoid(mg_ref[hs, D_MODEL:2 * D_MODEL].astype(F32)) * pb[i]).astype(BF16)
             for i, hs in enumerate(halves)]
    x1 = [x_ref[hs, :] + _dot(mixed[i], wout_ref[...]) for i, hs in enumerate(halves)]
    h2 = [_rms(x1[i], gffn_ref[...]) for i in range(2)]
    h_hi = [h.astype(BF16) for h in h2]
    for i, hs in enumerate(halves):
        x1_ref[hs, :] = x1[i]
        h2_ref[hs, :] = h_hi[i]

    h_lo = [(h2[i] - h_hi[i].astype(F32)).astype(BF16) for i in range(2)]
    r_hi = [_dot(h, wr_ref[...]) for h in h_hi]
    logit = jnp.concatenate([r_hi[i][:, 0:LANES] + r_hi[i][:, LANES:2 * LANES] + _dot(h_lo[i], wr_ref[:, 0:LANES])
                             for i in range(2)], axis=0) + br_ref[...]
    lane = lax.broadcasted_iota(jnp.int32, logit.shape, 1)
    big = jnp.int32(LANES)
    gmask = (lane >= N_EXPERTS) & (lane < N_EXPERTS + N_GROUPS)
    gl = jnp.where(gmask, logit, NEG)
    gmax = jnp.max(gl, axis=-1, keepdims=True)
    gidx = jnp.min(jnp.where(gmask & (gl == gmax), lane, big), axis=-1, keepdims=True) - N_EXPERTS
    pg_sel = 1.0 / jnp.sum(jnp.where(gmask, jnp.exp(gl - gmax), 0.0), axis=-1, keepdims=True)
    emask = (lane < N_EXPERTS) & (lax.shift_right_logical(lane, 2) == gidx)
    el = jnp.where(emask, logit, NEG)
    e1 = jnp.max(el, axis=-1, keepdims=True)
    i1 = jnp.min(jnp.where(emask & (el == e1), lane, big), axis=-1, keepdims=True)
    emask2 = emask & (lane != i1)
    el2 = jnp.where(emask2, logit, NEG)
    e2 = jnp.max(el2, axis=-1, keepdims=True)
    i2 = jnp.min(jnp.where(emask2 & (el2 == e2), lane, big), axis=-1, keepdims=True)
    x21 = jnp.exp(e2 - e1)
    w1 = pg_sel / (1.0 + x21)
    w2 = pg_sel * x21 / (1.0 + x21)
    first_lo = i1 < i2
    e_lo = jnp.where(first_lo, i1, i2) - EXP_PER_GROUP * gidx
    e_hi = jnp.where(first_lo, i2, i1) - EXP_PER_GROUP * gidx
    pair = lax.shift_right_logical(e_lo * (2 * EXP_PER_GROUP - 1 - e_lo), 1) + (e_hi - e_lo - 1)
    bucket = 6 * gidx + pair
    member = lane == bucket
    onehot = jnp.where(member, 1.0, 0.0)
    rr = lax.broadcasted_iota(jnp.int32, (TD, TD), 0)
    cc = lax.broadcasted_iota(jnp.int32, (TD, TD), 1)
    earlier = _dot((rr > cc).astype(BF16), onehot.astype(BF16))
    rank = jnp.sum(jnp.where(member, earlier, 0.0), axis=-1, keepdims=True)
    slab = jnp.where(lane == i1, w1, 0.0) + jnp.where(lane == i2, w2, 0.0)
    slab = jnp.where(lane == RT_BUCKET, bucket.astype(F32), slab)
    slab = jnp.where(lane == RT_RANK, rank, slab)
    slab = jnp.where(lane == RT_WLO, jnp.where(first_lo, w1, w2), slab)
    slab = jnp.where(lane == RT_WHI, jnp.where(first_lo, w2, w1), slab)
    hist8 = jnp.broadcast_to(jnp.sum(onehot, axis=0, keepdims=True), (SUBLANES, LANES))
    kk = lax.broadcasted_iota(jnp.int32, (LANES, LANES), 0)
    ll = lax.broadcasted_iota(jnp.int32, (LANES, LANES), 1)
    runlen = jnp.ceil(hist8 / RUN_CHUNK) * RUN_CHUNK
    before = sum(_dot(part, (kk < ll).astype(BF16)) for part in _split3(runlen))[0:1, :]
    lpos = jnp.sum(jnp.where(member, before, 0.0), axis=-1, keepdims=True) + rank
    slab = jnp.where(lane == RT_LPOS, lpos, slab)
    slab_ref[...] = slab
    hist_ref[...] = hist8


def _merge(ya, yb, mg, x2, wpa, wpb, wout, gffn, wr, br):
    t = x2.shape[0]
    row = lambda w: pl.BlockSpec((TD, w), lambda i: (i, 0))
    return pl.pallas_call(
        _merge_kernel,
        out_shape=[jax.ShapeDtypeStruct((t, D_MODEL), F32),
                   jax.ShapeDtypeStruct((t, D_MODEL), BF16),
                   jax.ShapeDtypeStruct((t, LANES), F32),
                   jax.ShapeDtypeStruct((t // TD * SUBLANES, LANES), F32)],
        grid=(t // TD,),
        in_specs=[row(NSA_HEADS * NSA_HD), row(ML_WIDTH), row(2 * D_MODEL), row(D_MODEL)]
                 + [_const_spec(a.shape) for a in (wpa, wpb, wout, gffn, wr, br)],
        out_specs=[row(D_MODEL), row(D_MODEL), row(LANES),
                   pl.BlockSpec((SUBLANES, LANES), lambda i: (i, 0))],
        compiler_params=pltpu.CompilerParams(dimension_semantics=("arbitrary",), vmem_limit_bytes=VMEM_LIMIT),
        name="merge",
    )(ya, yb, mg, x2, wpa, wpb, wout, gffn, wr, br)


def _rec_copy(src_ref, src_tok, dst_ref, dst_tok, sem, rows):
    src = src_ref.at[pl.ds(pl.multiple_of(src_tok * rows, rows), rows), :]
    dst = dst_ref.at[pl.ds(pl.multiple_of(dst_tok * rows, rows), rows), :]
    return pltpu.make_async_copy(src, dst, sem)


def _token_copies(n, make):
    def body(g, carry):
        for u in range(DMA_UNROLL):
            make(g * DMA_UNROLL + u).start(priority=u % 2)
        return carry
    lax.fori_loop(0, n // DMA_UNROLL, body, 0)


def _dispatch_kernel(h2_ref, slab_ref, o_ref):
    slab = slab_ref[...]
    perm = (slab[:, RT_LPOS:RT_LPOS + 1]
            == lax.broadcasted_iota(jnp.int32, (1, SORTED_ROWS), 1).astype(F32)).astype(BF16)
    o_ref[:, 0:D_MODEL] = lax.dot_general(perm, h2_ref[...], _TN, preferred_element_type=F32)
    lane = lax.broadcasted_iota(jnp.int32, (1, LANES), 1)
    hi, mid, lo = _split3(jnp.where((lane == RT_WLO) | (lane == RT_WHI), slab, 0.0))
    terms = (hi.astype(F32) + pltpu.roll(mid.astype(F32), TERM_LANES, 1)
             + pltpu.roll(lo.astype(F32), 2 * TERM_LANES, 1)).astype(BF16)
    o_ref[:, D_MODEL:] = lax.dot_general(perm, terms, _TN, preferred_element_type=F32)


def _dispatch(h2, slab):
    t = h2.shape[0]
    return pl.pallas_call(
        _dispatch_kernel,
        out_shape=jax.ShapeDtypeStruct((t // TD * SORTED_ROWS, D_MODEL + LANES), F32),
        grid=(t // TD,),
        in_specs=[pl.BlockSpec((TD, D_MODEL), lambda i: (i, 0)), pl.BlockSpec((TD, LANES), lambda i: (i, 0))],
        out_specs=pl.BlockSpec((SORTED_ROWS, D_MODEL + LANES), lambda i: (i, 0)),
        compiler_params=pltpu.CompilerParams(dimension_semantics=("arbitrary",), vmem_limit_bytes=VMEM_LIMIT),
        name="dispatch",
    )(h2, slab)


MOE_TILES = 2


def _moe_kernel(te_ref, nu_ref, src_ref, hx_ref, w13_ref, w2_ref, y_ref, xbuf, sem):
    step = pl.program_id(0)
    last = pl.num_programs(0) - 1
    n_tiles = pl.num_programs(0) * MOE_TILES
    n_used = nu_ref[0]
    slot = lax.rem(step, 2)
    rows = MOE_TILES * TM
    n_chunks = rows // RUN_CHUNK

    def request(which_step, which):
        for c in range(n_chunks):
            src = pl.ds(pl.multiple_of(src_ref[which_step * n_chunks + c], RUN_CHUNK), RUN_CHUNK)
            pltpu.make_async_copy(hx_ref.at[src, :], xbuf.at[which, c * RUN_CHUNK:(c + 1) * RUN_CHUNK, :],
                                  sem.at[which]).start()

    def drain(which):
        pltpu.make_async_copy(hx_ref.at[0:rows, :], xbuf.at[which], sem.at[which]).wait()

    @pl.when(step == 0)
    def _():
        request(0, 0)

    drain(slot)
    nxt = jnp.minimum(step + 1, last)
    xb = xbuf.at[slot]

    @pl.when(step * MOE_TILES < n_used)
    def _():
        subs = range(MOE_TILES)
        hs = [xb[sub * TM:(sub + 1) * TM, 0:D_MODEL].astype(BF16) for sub in subs]
        slabs = [xb[sub * TM:(sub + 1) * TM, D_MODEL:] for sub in subs]
        ys = [None] * MOE_TILES
        weight = lambda s, lane: sum(s[:, lane + k * TERM_LANES:lane + k * TERM_LANES + 1] for k in range(3))
        for side, lane in ((0, RT_WLO), (1, RT_WHI)):
            es = [te_ref[side * n_tiles + step * MOE_TILES + sub] for sub in subs]
            up = [_dot(hs[sub], w13_ref[es[sub]]) for sub in subs]
            if side == 0:
                request(nxt, 1 - slot)
            act = [(up[sub][:, 0:D_EXPERT] * _sigmoid(up[sub][:, 0:D_EXPERT]) * up[sub][:, D_EXPERT:2 * D_EXPERT]
                    * weight(slabs[sub], lane)).astype(BF16) for sub in subs]
            for sub in subs:
                part = _dot(act[sub], w2_ref[es[sub]])
                ys[sub] = part if ys[sub] is None else ys[sub] + part
        for sub in subs:
            for j in range(X_ROWS):
                y_ref[pl.ds(sub * TM * X_ROWS + j, TM, stride=X_ROWS), :] = ys[sub][:, j * LANES:(j + 1) * LANES]

    @pl.when(step * MOE_TILES >= n_used)
    def _():
        y_ref[...] = jnp.zeros_like(y_ref)
        request(nxt, 1 - slot)

    @pl.when(step == last)
    def _():
        drain(1 - slot)


def _moe(tile_e, n_used, slot_src, hx_sorted, w13, w2):
    n_tiles = tile_e.shape[0] // 2
    rows = MOE_TILES * TM
    return pl.pallas_call(
        _moe_kernel,
        out_shape=jax.ShapeDtypeStruct((n_tiles * TM * X_ROWS, LANES), F32),
        grid_spec=pltpu.PrefetchScalarGridSpec(
            num_scalar_prefetch=3,
            grid=(n_tiles // MOE_TILES,),
            in_specs=[pl.BlockSpec(memory_space=pl.ANY),
                      pl.BlockSpec(w13.shape, lambda k, te, nu, src: (0, 0, 0), pipeline_mode=pl.Buffered(1)),
                      pl.BlockSpec(w2.shape, lambda k, te, nu, src: (0, 0, 0), pipeline_mode=pl.Buffered(1))],
            out_specs=pl.BlockSpec((rows * X_ROWS, LANES), lambda k, te, nu, src: (k, 0)),
            scratch_shapes=[pltpu.VMEM((2, rows, D_MODEL + LANES), F32), pltpu.SemaphoreType.DMA((2,))],
        ),
        compiler_params=pltpu.CompilerParams(dimension_semantics=("arbitrary",), vmem_limit_bytes=VMEM_LIMIT),
        name="moe",
    )(tile_e, n_used, slot_src, hx_sorted, w13, w2)


def _combine_kernel(pos_ref, y_ref, x1_ref, p_ref, gple_ref, wpg_ref, wpp_ref, gfin_ref, o_ref, ybuf, sem):
    i = pl.program_id(0)
    slot = lax.rem(i, 2)

    last = pl.num_programs(0) - 1

    def drain(which):
        pltpu.make_async_copy(y_ref.at[0:TD * X_ROWS, :], ybuf.at[which], sem.at[which]).wait()

    @pl.when(i == 0)
    def _():
        _token_copies(TD, lambda r: _rec_copy(y_ref, pos_ref[r], ybuf.at[0], r, sem.at[0], X_ROWS))

    drain(slot)
    nxt = jnp.minimum(i + 1, last) * TD
    yb = ybuf.at[slot]
    part = TD // COMBINE_PARTS
    for q in range(COMBINE_PARTS):
        rows = slice(q * part, (q + 1) * part)
        y = jnp.concatenate([yb[pl.ds(q * part * X_ROWS + j, part, stride=X_ROWS), :] for j in range(X_ROWS)],
                            axis=1)
        x2 = x1_ref[rows, :] + y
        h3 = _rms(x2, gple_ref[...]).astype(BF16)
        x3 = x2 + _sigmoid(_dot(h3, wpg_ref[...])) * _dot(p_ref[rows, :].astype(BF16), wpp_ref[...])
        o_ref[rows, :] = _rms(x3, gfin_ref[...])
        for r in range(q * part, (q + 1) * part):
            _rec_copy(y_ref, pos_ref[nxt + r], ybuf.at[1 - slot], r, sem.at[1 - slot], X_ROWS).start(
                priority=r % 2)

    @pl.when(i == last)
    def _():
        drain(1 - slot)


def _combine(pos, y_sorted, x1, p2, gple, wpg, wpp, gfin):
    t = x1.shape[0]
    row = lambda w: pl.BlockSpec((TD, w), lambda i, pos_r: (i, 0))
    const = lambda a: pl.BlockSpec(a.shape, lambda i, pos_r: (0,) * a.ndim, pipeline_mode=pl.Buffered(1))
    return pl.pallas_call(
        _combine_kernel,
        out_shape=jax.ShapeDtypeStruct((t, D_MODEL), F32),
        grid_spec=pltpu.PrefetchScalarGridSpec(
            num_scalar_prefetch=1,
            grid=(t // TD,),
            in_specs=[pl.BlockSpec(memory_space=pl.ANY), row(D_MODEL), row(PLE_DIM),
                      const(gple), const(wpg), const(wpp), const(gfin)],
            out_specs=row(D_MODEL),
            scratch_shapes=[pltpu.VMEM((2, TD * X_ROWS, LANES), F32), pltpu.SemaphoreType.DMA((2,))],
        ),
        compiler_params=pltpu.CompilerParams(dimension_semantics=("arbitrary",), vmem_limit_bytes=VMEM_LIMIT),
        name="combine",
    )(pos, y_sorted, x1, p2, gple, wpg, wpp, gfin)


def _routing_tables(slab, hist8):
    t = slab.shape[0]
    nt = t // TD
    n_tiles = t // TM + N_BUCKETS + -(-nt * N_BUCKETS * (RUN_CHUNK - 1) // TM)
    n_tiles += n_tiles % MOE_TILES
    hist = hist8.reshape(nt, SUBLANES, LANES)[:, 0, :]
    runlen = jnp.ceil(hist / RUN_CHUNK) * RUN_CHUNK
    counts = jnp.sum(runlen, axis=0)
    padded = jnp.ceil(counts / TM) * TM
    ends = jnp.cumsum(padded)
    first = (ends - padded)[None, :] + jnp.cumsum(runlen, axis=0) - runlen
    lane = jnp.arange(N_BUCKETS, dtype=F32)[None, :]
    mine = lane == slab[:, RT_BUCKET:RT_BUCKET + 1]
    pos = jnp.sum(jnp.where(mine, jnp.repeat(first[:, :N_BUCKETS], TD, axis=0), 0.0), axis=1) + slab[:, RT_RANK]
    starts = jnp.arange(n_tiles, dtype=F32) * TM
    tile_bucket = jnp.minimum(jnp.sum(ends[None, :N_BUCKETS] <= starts[:, None], axis=1), N_BUCKETS - 1)
    group, pair = tile_bucket // 6, tile_bucket % 6
    e_lo = EXP_PER_GROUP * group + jnp.array([0, 0, 0, 1, 1, 2], jnp.int32)[pair]
    e_hi = EXP_PER_GROUP * group + jnp.array([1, 2, 3, 2, 3, 3], jnp.int32)[pair]
    tile_e = jnp.concatenate([e_lo, e_hi]).astype(jnp.int32)
    n_used = (ends[N_BUCKETS - 1] / TM).astype(jnp.int32).reshape(1)
    nb = N_BUCKETS
    of_b = (tile_bucket[:, None] == jnp.arange(nb)[None, :])[:, None, :]
    pick = lambda a: jnp.sum(jnp.where(of_b, a[None, :, :nb], 0.0), axis=2)
    run_end = jnp.cumsum(runlen, axis=0)
    local = jnp.cumsum(runlen, axis=1) - runlen
    o = (starts - jnp.sum(jnp.where(of_b[:, 0, :], (ends - padded)[None, :nb], 0.0), axis=1))[:, None] \
        + jnp.arange(0, TM, RUN_CHUNK, dtype=F32)[None, :]
    tt = jnp.sum(pick(run_end)[:, None, :] <= o[:, :, None], axis=2)
    row0 = pick(local - (run_end - runlen) + SORTED_ROWS * jnp.arange(nt, dtype=F32)[:, None])
    row = jnp.sum(jnp.where(tt[:, :, None] == jnp.arange(nt)[None, None, :], row0[:, None, :], 0.0), axis=2) + o
    slot_src = jnp.where(tt < nt, row, float(SORT_ROWS)).reshape(-1).astype(jnp.int32)
    return pos.astype(jnp.int32), slot_src, tile_e, n_used


def _pack_inproj_weights(w):
    d = w.shape[0]
    qw = NSA_HEADS * NSA_HD
    kvw = NSA_KV * NSA_HD
    o = 0
    wq = w[:, o:o + qw]; o += qw
    wkc = w[:, o:o + kvw]; o += kvw
    wvc = w[:, o:o + kvw]; o += kvw
    wks = w[:, o:o + kvw]; o += kvw
    wvs = w[:, o:o + kvw]; o += kvw
    wkw = w[:, o:o + kvw]; o += kvw
    wvw = w[:, o:o + kvw]; o += kvw
    wga = w[:, o:o + 3 * NSA_HEADS]; o += 3 * NSA_HEADS
    wqkb = w[:, o:o + 2 * ML_WIDTH]; o += 2 * ML_WIDTH
    wvb = w[:, o:o + ML_WIDTH]; o += ML_WIDTH
    wob = w[:, o:o + ML_WIDTH]; o += ML_WIDTH
    wif = w[:, o:o + 2 * ML_HEADS]; o += 2 * ML_HEADS
    wmg = w[:, o:o + 2 * D_MODEL]
    wsm = jnp.concatenate([wga, wif, jnp.zeros((d, LANES - 3 * NSA_HEADS - 2 * ML_HEADS), w.dtype)], axis=1)
    wcat = jnp.concatenate([wq, wkc, wvc, wks, wkw, wsm, wqkb, wvb, wob, wmg], axis=1).astype(BF16)
    wtr = jnp.concatenate([wvs, wvw, wsm], axis=1).T.astype(BF16)
    return wcat, wtr


def _rope_tables(positions):
    half = ROPE_DIM // 2
    inv = ROPE_THETA ** (-jnp.arange(0, ROPE_DIM, 2, dtype=F32) / ROPE_DIM)
    ang = positions.astype(F32).reshape(-1, 1) * inv[None, :]
    cs = jnp.concatenate([jnp.cos(ang), jnp.sin(ang)], axis=1)
    d = np.arange(LANES) % NSA_HD
    spread = np.zeros((2 * half, 3 * LANES), np.float32)
    lanes = np.arange(LANES)
    rot = d < ROPE_DIM
    spread[(d % half)[rot], lanes[rot]] = 1.0
    hi = (d >= half) & rot
    spread[half + (d % half)[hi], LANES + lanes[hi]] = 1.0
    lo = d < half
    spread[half + (d % half)[lo], 2 * LANES + lanes[lo]] = -1.0
    unrotated = (~rot).astype(np.float32).reshape(1, LANES)
    return cs, jnp.asarray(spread, BF16), jnp.asarray(unrotated)


def _pack_compress_weights(w1, w2, pe):
    half = CMP_LEN // 2
    w1r = w1.reshape(2, half, NSA_HD, CMP_HIDDEN)
    outs = []
    for part in range(2):
        wb = w1r[part].astype(BF16)
        zb = jnp.zeros_like(wb)
        wp = jnp.stack([jnp.stack([wb, zb], axis=2), jnp.stack([zb, wb], axis=2)], axis=1)
        outs.append(wp.reshape(half * NSA_KV * NSA_HD, NSA_KV * CMP_HIDDEN))
    pe8 = jnp.broadcast_to(pe.reshape(1, CMP_LEN * NSA_HD), (SUBLANES, CMP_LEN * NSA_HD)).astype(BF16)
    return outs[0], outs[1], pe8, w1.astype(BF16)


def _stages(x, p, positions, g_mix, w_in, b_if, w_ck1, w_ck2, pe_ck, w_cv1, w_cv2, pe_cv, w_conv, b_conv, g_hn, w_pa, w_pb, w_out, g_ffn, w_rg, b_rg, w_re, b_re, w_e13, w_e2, g_ple, w_pg, w_pp, g_final):
    b, s, d = x.shape
    t = b * s
    cs, spread, unrot = _rope_tables(positions)
    assert w_in.shape[0] == 1, "the final norm is fused into the layer's last kernel: single-layer problem only"
    for i in range(w_in.shape[0]):
        x2 = x.reshape(t, d)
        wcat, wtr = _pack_inproj_weights(w_in[i])
        (qpad, kc_tok, vc_tok, ks, kw, vst, vwt, sm, smt, qkb, vb, ob, mg) = _inproj(
            x2, g_mix[i].reshape(1, d), wcat, wtr, cs, spread, unrot)
        wka, wkb, pek, w1k = _pack_compress_weights(w_ck1[i], w_ck2[i], pe_ck[i])
        wva, wvb, pev, w1v = _pack_compress_weights(w_cv1[i], w_cv2[i], pe_cv[i])
        zpad = jnp.zeros((CMP_HIDDEN, NSA_HD), F32)
        w2k = jnp.stack([jnp.concatenate([w_ck2[i], zpad], axis=1),
                         jnp.concatenate([zpad, w_ck2[i]], axis=1)]).astype(BF16)
        w2vt = w_cv2[i].T.astype(BF16)
        nrow = s // CMP_STRIDE
        rk = kc_tok.reshape(b, nrow, CMP_STRIDE * LANES)
        rv = vc_tok.reshape(b, nrow, CMP_STRIDE * LANES)
        kcb, vct = _compress(rk, rv, wka, wkb, wva, wvb, pek, pev, w1k, w1v, w2k, w2vt)
        ya = _nsa(qpad, kcb, vct, ks, kw, vst, vwt, smt, b, s)
        bif = b_if[i].astype(F32)
        bifc = jnp.zeros((1, LANES), F32).at[0, SM_I:SM_I + 2 * ML_HEADS].set(bif)
        bifr = bifc.reshape(LANES, 1)
        yb = _mlstm(qkb, vb, ob, sm, smt, w_conv[i], b_conv[i].reshape(1, -1), bifc, bifr,
                    g_hn[i].reshape(1, -1), b, s)
        wr = jnp.concatenate([w_re[i], w_rg[i], jnp.zeros((d, LANES - N_EXPERTS - N_GROUPS), F32)], axis=1)
        wr_hi = wr.astype(BF16)
        wr = jnp.concatenate([wr_hi, (wr - wr_hi.astype(F32)).astype(BF16)], axis=1)
        br =jnp.concatenate([b_re[i], b_rg[i], jnp.zeros((LANES - N_EXPERTS - N_GROUPS,), F32)]).reshape(1, LANES)
        x1, h2, slab, hist8 = _merge(ya, yb, mg, x2, w_pa[i].astype(BF16), w_pb[i].astype(BF16),
                                     w_out[i].astype(BF16), g_ffn[i].reshape(1, d), wr, br)
        pos, slot_src, tile_e, n_used = _routing_tables(slab, hist8)
        hx_sorted = _dispatch(h2, slab)
        y_sorted = _moe(tile_e, n_used, slot_src, hx_sorted, w_e13[i].astype(BF16), w_e2[i].astype(BF16))
        out = _combine(pos, y_sorted, x1, p[i].reshape(t, PLE_DIM), g_ple[i].reshape(1, d), w_pg[i].astype(BF16),
                       w_pp[i].astype(BF16), g_final.reshape(1, d))
        x = out.reshape(b, s, d)
    return dict(out=x, qpad=qpad, ks=ks, kcb=kcb, vct=vct, y_a=ya, y_b=yb, x1=x1, pos=pos)


def kernel(x, p, positions, g_mix, w_in, b_if, w_ck1, w_ck2, pe_ck, w_cv1, w_cv2, pe_cv, w_conv, b_conv, g_hn, w_pa, w_pb, w_out, g_ffn, w_rg, b_rg, w_re, b_re, w_e13, w_e2, g_ple, w_pg, w_pp, g_final):
    return _stages(x, p, positions, g_mix, w_in, b_if, w_ck1, w_ck2, pe_ck, w_cv1, w_cv2, pe_cv, w_conv, b_conv, g_hn,
                   w_pa, w_pb, w_out, g_ffn, w_rg, b_rg, w_re, b_re, w_e13, w_e2, g_ple, w_pg, w_pp, g_final)["out"]
```
